```python
import math
import jax, jax.numpy as jnp
from jax import lax
import numpy as np

D_MODEL = 1024
BATCH = 8
SEQ = 8192
DEPTH = 2

HEAD_DIM = 64
N_HEADS = D_MODEL // HEAD_DIM
HEADS_A = N_HEADS // 2
HEADS_B = N_HEADS - HEADS_A
KV_A = 2
KV_B = 2
G_A = HEADS_A // KV_A
G_B = HEADS_B // KV_B
Q_A = HEADS_A * HEAD_DIM
KVD_A = KV_A * HEAD_DIM
Q_B = HEADS_B * HEAD_DIM
KVD_B = KV_B * HEAD_DIM
IN_COLS = Q_A + 2 * KVD_A + Q_B + 2 * KVD_B
IN_SPLITS = (Q_A, Q_A + KVD_A, Q_A + 2 * KVD_A, Q_A + 2 * KVD_A + Q_B, Q_A + 2 * KVD_A + Q_B + KVD_B)
MIX_WIDTH = Q_A + Q_B

GRID_W = 64
ROPE_THETA = 10000.0
Q_BLOCK = 128
WINDOW = 128
N_BUCKETS = 32
MAX_DISTANCE = 128
D_FF = 2816
CONV_W = 3
ALPHA = (2.0 * DEPTH) ** 0.25
BETA = (8.0 * DEPTH) ** -0.25
RMS_EPS = 1e-6
LN_EPS = 1e-5

kernel_name = "hymba_axial_swa_convglu_deepnorm_encoder"


def rms_norm(x, g):
    xf = x.astype(jnp.float32)
    y = xf * lax.rsqrt(jnp.mean(xf * xf, axis=-1, keepdims=True) + RMS_EPS)
    return (y * g.astype(jnp.float32)).astype(x.dtype)


def layer_norm(x, g, b):
    xf = x.astype(jnp.float32)
    mu = jnp.mean(xf, axis=-1, keepdims=True)
    var = jnp.mean(jnp.square(xf - mu), axis=-1, keepdims=True)
    y = (xf - mu) * lax.rsqrt(var + LN_EPS)
    return (y * g.astype(jnp.float32) + b.astype(jnp.float32)).astype(x.dtype)


def axial_rope_tables(seq_len):
    rows_n = seq_len // GRID_W
    row = jnp.repeat(jnp.arange(rows_n, dtype=jnp.float32), GRID_W)
    col = jnp.tile(jnp.arange(GRID_W, dtype=jnp.float32), rows_n)
    half = HEAD_DIM // 2
    inv_freq = ROPE_THETA ** (-jnp.arange(0, half, 2, dtype=jnp.float32) / half)
    ang = jnp.concatenate([row[:, None] * inv_freq, col[:, None] * inv_freq], axis=-1)
    return jnp.cos(ang), jnp.sin(ang)


def apply_rope(x, cos, sin):
    xf = x.astype(jnp.float32).reshape(x.shape[:-1] + (HEAD_DIM // 2, 2))
    x0, x1 = xf[..., 0], xf[..., 1]
    c = cos[None, :, None, :]
    s = sin[None, :, None, :]
    out = jnp.stack([x0 * c - x1 * s, x0 * s + x1 * c], axis=-1).reshape(x.shape)
    return out.astype(x.dtype)


def global_gqa(q, k, v):
    B, S = q.shape[0], q.shape[1]
    nb = S // Q_BLOCK
    qb = q.reshape(B, nb, Q_BLOCK, KV_A, G_A, HEAD_DIM).transpose(1, 0, 2, 3, 4, 5)
    scale = HEAD_DIM ** -0.5

    def block(qi):
        s = jnp.einsum('bqhgd,bkhd->bhgqk', qi, k, preferred_element_type=jnp.float32) * scale
        p = jax.nn.softmax(s, axis=-1)
        return jnp.einsum('bhgqk,bkhd->bqhgd', p.astype(v.dtype), v)

    o = lax.map(block, qb)
    return o.transpose(1, 0, 2, 3, 4, 5).reshape(B, S, Q_A)


def t5_bucket(rel):
    half = N_BUCKETS // 2
    max_exact = half // 2
    bucket = jnp.where(rel > 0, half, 0)
    rp = jnp.abs(rel)
    rpf = jnp.maximum(rp, 1).astype(jnp.float32)
    large = max_exact + (jnp.log(rpf / max_exact) / math.log(MAX_DISTANCE / max_exact)
                         * (half - max_exact)).astype(jnp.int32)
    large = jnp.minimum(large, half - 1)
    return bucket + jnp.where(rp < max_exact, rp, large)


def window_gqa_sink(q, k, v, rel_bias, sink):
    B, S = q.shape[0], q.shape[1]
    nb = S // Q_BLOCK
    scale = HEAD_DIM ** -0.5
    qb = q.reshape(B, nb, Q_BLOCK, KV_B, G_B, HEAD_DIM)
    pad = ((0, 0), (Q_BLOCK, Q_BLOCK), (0, 0), (0, 0))
    kp = jnp.pad(k, pad).reshape(B, nb + 2, Q_BLOCK, KV_B, HEAD_DIM)
    vp = jnp.pad(v, pad).reshape(B, nb + 2, Q_BLOCK, KV_B, HEAD_DIM)
    kb = jnp.concatenate([kp[:, :-2], kp[:, 1:-1], kp[:, 2:]], axis=2)
    vb = jnp.concatenate([vp[:, :-2], vp[:, 1:-1], vp[:, 2:]], axis=2)
    qpos = jnp.arange(Q_BLOCK, dtype=jnp.int32)
    kpos = jnp.arange(3 * Q_BLOCK, dtype=jnp.int32) - Q_BLOCK
    rel = kpos[None, :] - qpos[:, None]
    bias = rel_bias.astype(jnp.float32)[t5_bucket(rel)]
    bias = bias.transpose(2, 0, 1).reshape(KV_B, G_B, Q_BLOCK, 3 * Q_BLOCK)
    kabs = jnp.arange(nb, dtype=jnp.int32)[:, None] * Q_BLOCK + kpos[None, :]
    valid = (jnp.abs(rel) <= WINDOW)[None] & ((kabs >= 0) & (kabs < S))[:, None, :]
    s = jnp.einsum('bnqhgd,bnkhd->bnhgqk', qb, kb, preferred_element_type=jnp.float32) * scale + bias
    s = jnp.where(valid[None, :, None, None], s, -jnp.inf)
    sink_logit = jnp.broadcast_to(sink.astype(jnp.float32).reshape(KV_B, G_B)[None, None, :, :, None, None],
                                  s.shape[:-1] + (1,))
    p = jax.nn.softmax(jnp.concatenate([s, sink_logit], axis=-1), axis=-1)[..., :-1]
    o = jnp.einsum('bnhgqk,bnkhd->bnqhgd', p.astype(v.dtype), vb)
    return o.reshape(B, S, Q_B)


def token_mixer(x, rel_bias, cos, sin, w_in, q_norm, k_norm, sink, out_norm_a, out_norm_b, w_out):
    B, S, _ = x.shape
    h = jnp.einsum('bsd,de->bse', x, w_in)
    qa, ka, va, qb, kb, vb = jnp.split(h, IN_SPLITS, axis=-1)
    qa = apply_rope(rms_norm(qa.reshape(B, S, HEADS_A, HEAD_DIM), q_norm), cos, sin)
    ka = apply_rope(rms_norm(ka.reshape(B, S, KV_A, HEAD_DIM), k_norm), cos, sin)
    va = va.reshape(B, S, KV_A, HEAD_DIM)
    ya = rms_norm(global_gqa(qa, ka, va), out_norm_a)
    yb = window_gqa_sink(qb.reshape(B, S, HEADS_B, HEAD_DIM), kb.reshape(B, S, KV_B, HEAD_DIM),
                         vb.reshape(B, S, KV_B, HEAD_DIM), rel_bias, sink)
    yb = rms_norm(yb, out_norm_b)
    return jnp.einsum('bse,ed->bsd', jnp.concatenate([ya, yb], axis=-1), w_out)


def conv_glu(x, w_gate, w_up, conv_w, conv_b, w_down):
    S = x.shape[1]
    g = jnp.einsum('bsd,df->bsf', x, w_gate)
    u = jnp.einsum('bsd,df->bsf', x, w_up)
    r = CONV_W // 2
    gp = jnp.pad(g, ((0, 0), (r, r), (0, 0)))
    gc = conv_b
    for j in range(CONV_W):
        gc = gc + gp[:, j:j + S] * conv_w[j]
    return jnp.einsum('bsf,fd->bsd', jax.nn.gelu(gc) * u, w_down)


def _fwd_setup_inputs(seed: int = 0) -> dict:
    key = jax.random.key(seed)
    ks = jax.random.split(key, 20)
    f32 = jnp.float32
    L = DEPTH

    def nrm(k, shape, scale):
        return jax.random.normal(k, shape, f32) * scale

    return {
        "x": nrm(ks[0], (BATCH, SEQ, D_MODEL), 1.0),
        "rel_bias": nrm(ks[1], (N_BUCKETS, HEADS_B), 0.1),
        "w_in": nrm(ks[2], (L, D_MODEL, IN_COLS), D_MODEL ** -0.5),
        "q_norm": 1.0 + nrm(ks[3], (L, HEAD_DIM), 0.05),
        "k_norm": 1.0 + nrm(ks[4], (L, HEAD_DIM), 0.05),
        "sink": nrm(ks[5], (L, HEADS_B), 0.5),
        "out_norm_a": 1.0 + nrm(ks[6], (L, Q_A), 0.05),
        "out_norm_b": 1.0 + nrm(ks[7], (L, Q_B), 0.05),
        "w_out": nrm(ks[8], (L, MIX_WIDTH, D_MODEL), BETA * MIX_WIDTH ** -0.5),
        "ln1_g": 1.0 + nrm(ks[9], (L, D_MODEL), 0.05),
        "ln1_b": nrm(ks[10], (L, D_MODEL), 0.01),
        "w_gate": nrm(ks[11], (L, D_MODEL, D_FF), D_MODEL ** -0.5),
        "w_up": nrm(ks[12], (L, D_MODEL, D_FF), D_MODEL ** -0.5),
        "conv_w": nrm(ks[13], (L, CONV_W, D_FF), CONV_W ** -0.5),
        "conv_b": nrm(ks[14], (L, D_FF), 0.01),
        "w_down": nrm(ks[15], (L, D_FF, D_MODEL), BETA * D_FF ** -0.5),
        "ln2_g": 1.0 + nrm(ks[16], (L, D_MODEL), 0.05),
        "ln2_b": nrm(ks[17], (L, D_MODEL), 0.01),
    }


def _fwd_reference(x, rel_bias, w_in, q_norm, k_norm, sink, out_norm_a, out_norm_b, w_out,
              ln1_g, ln1_b, w_gate, w_up, conv_w, conv_b, w_down, ln2_g, ln2_b):
    cos, sin = axial_rope_tables(x.shape[1])
    for l in range(DEPTH):
        mix = token_mixer(x, rel_bias, cos, sin, w_in[l], q_norm[l], k_norm[l], sink[l],
                          out_norm_a[l], out_norm_b[l], w_out[l])
        x = layer_norm(ALPHA * x + mix, ln1_g[l], ln1_b[l])
        ffn = conv_glu(x, w_gate[l], w_up[l], conv_w[l], conv_b[l], w_down[l])
        x = layer_norm(ALPHA * x + ffn, ln2_g[l], ln2_b[l])
    return x


import jax as _jax
import jax.numpy as _jnp

TWIN_FORMAT = 'train_step'
FWD_PARAMS = ['x', 'rel_bias', 'w_in', 'q_norm', 'k_norm', 'sink', 'out_norm_a', 'out_norm_b', 'w_out', 'ln1_g', 'ln1_b', 'w_gate', 'w_up', 'conv_w', 'conv_b', 'w_down', 'ln2_g', 'ln2_b']
TWIN_WEIGHTS = ['rel_bias', 'w_in', 'q_norm', 'k_norm', 'sink', 'out_norm_a', 'out_norm_b', 'w_out', 'ln1_g', 'ln1_b', 'w_gate', 'w_up', 'conv_w', 'conv_b', 'w_down', 'ln2_g', 'ln2_b']
TWIN_DIFF_INPUT = 'x'
TWIN_INPUTS = ['x', 'rel_bias', 'w_in', 'q_norm', 'k_norm', 'sink', 'out_norm_a', 'out_norm_b', 'w_out', 'ln1_g', 'ln1_b', 'w_gate', 'w_up', 'conv_w', 'conv_b', 'w_down', 'ln2_g', 'ln2_b', 'loss_target', 'm_rel_bias', 'm_w_in', 'm_q_norm', 'm_k_norm', 'm_sink', 'm_out_norm_a', 'm_out_norm_b', 'm_w_out', 'm_ln1_g', 'm_ln1_b', 'm_w_gate', 'm_w_up', 'm_conv_w', 'm_conv_b', 'm_w_down', 'm_ln2_g', 'm_ln2_b', 'v_rel_bias', 'v_w_in', 'v_q_norm', 'v_k_norm', 'v_sink', 'v_out_norm_a', 'v_out_norm_b', 'v_w_out', 'v_ln1_g', 'v_ln1_b', 'v_w_gate', 'v_w_up', 'v_conv_w', 'v_conv_b', 'v_w_down', 'v_ln2_g', 'v_ln2_b']
TWIN_OUTPUTS = ['loss', 'grad_x', 'grad_rel_bias', 'grad_w_in', 'grad_q_norm', 'grad_k_norm', 'grad_sink', 'grad_out_norm_a', 'grad_out_norm_b', 'grad_w_out', 'grad_ln1_g', 'grad_ln1_b', 'grad_w_gate', 'grad_w_up', 'grad_conv_w', 'grad_conv_b', 'grad_w_down', 'grad_ln2_g', 'grad_ln2_b', 'delta_rel_bias', 'delta_w_in', 'delta_q_norm', 'delta_k_norm', 'delta_sink', 'delta_out_norm_a', 'delta_out_norm_b', 'delta_w_out', 'delta_ln1_g', 'delta_ln1_b', 'delta_w_gate', 'delta_w_up', 'delta_conv_w', 'delta_conv_b', 'delta_w_down', 'delta_ln2_g', 'delta_ln2_b', 'new_m_rel_bias', 'new_m_w_in', 'new_m_q_norm', 'new_m_k_norm', 'new_m_sink', 'new_m_out_norm_a', 'new_m_out_norm_b', 'new_m_w_out', 'new_m_ln1_g', 'new_m_ln1_b', 'new_m_w_gate', 'new_m_w_up', 'new_m_conv_w', 'new_m_conv_b', 'new_m_w_down', 'new_m_ln2_g', 'new_m_ln2_b', 'new_v_rel_bias', 'new_v_w_in', 'new_v_q_norm', 'new_v_k_norm', 'new_v_sink', 'new_v_out_norm_a', 'new_v_out_norm_b', 'new_v_w_out', 'new_v_ln1_g', 'new_v_ln1_b', 'new_v_w_gate', 'new_v_w_up', 'new_v_conv_w', 'new_v_conv_b', 'new_v_w_down', 'new_v_ln2_g', 'new_v_ln2_b']
TWIN_LEAF_KINDS = {'loss': 'loss', 'grad_x': 'grad_x', 'grad_rel_bias': 'grad_w', 'grad_w_in': 'grad_w', 'grad_q_norm': 'grad_w', 'grad_k_norm': 'grad_w', 'grad_sink': 'grad_w', 'grad_out_norm_a': 'grad_w', 'grad_out_norm_b': 'grad_w', 'grad_w_out': 'grad_w', 'grad_ln1_g': 'grad_w', 'grad_ln1_b': 'grad_w', 'grad_w_gate': 'grad_w', 'grad_w_up': 'grad_w', 'grad_conv_w': 'grad_w', 'grad_conv_b': 'grad_w', 'grad_w_down': 'grad_w', 'grad_ln2_g': 'grad_w', 'grad_ln2_b': 'grad_w', 'delta_rel_bias': 'delta_w', 'delta_w_in': 'delta_w', 'delta_q_norm': 'delta_w', 'delta_k_norm': 'delta_w', 'delta_sink': 'delta_w', 'delta_out_norm_a': 'delta_w', 'delta_out_norm_b': 'delta_w', 'delta_w_out': 'delta_w', 'delta_ln1_g': 'delta_w', 'delta_ln1_b': 'delta_w', 'delta_w_gate': 'delta_w', 'delta_w_up': 'delta_w', 'delta_conv_w': 'delta_w', 'delta_conv_b': 'delta_w', 'delta_w_down': 'delta_w', 'delta_ln2_g': 'delta_w', 'delta_ln2_b': 'delta_w', 'new_m_rel_bias': 'new_m', 'new_m_w_in': 'new_m', 'new_m_q_norm': 'new_m', 'new_m_k_norm': 'new_m', 'new_m_sink': 'new_m', 'new_m_out_norm_a': 'new_m', 'new_m_out_norm_b': 'new_m', 'new_m_w_out': 'new_m', 'new_m_ln1_g': 'new_m', 'new_m_ln1_b': 'new_m', 'new_m_w_gate': 'new_m', 'new_m_w_up': 'new_m', 'new_m_conv_w': 'new_m', 'new_m_conv_b': 'new_m', 'new_m_w_down': 'new_m', 'new_m_ln2_g': 'new_m', 'new_m_ln2_b': 'new_m', 'new_v_rel_bias': 'new_v', 'new_v_w_in': 'new_v', 'new_v_q_norm': 'new_v', 'new_v_k_norm': 'new_v', 'new_v_sink': 'new_v', 'new_v_out_norm_a': 'new_v', 'new_v_out_norm_b': 'new_v', 'new_v_w_out': 'new_v', 'new_v_ln1_g': 'new_v', 'new_v_ln1_b': 'new_v', 'new_v_w_gate': 'new_v', 'new_v_w_up': 'new_v', 'new_v_conv_w': 'new_v', 'new_v_conv_b': 'new_v', 'new_v_w_down': 'new_v', 'new_v_ln2_g': 'new_v', 'new_v_ln2_b': 'new_v'}


def _forward(args):
    return _fwd_reference(*[args[k] for k in FWD_PARAMS])


def _output_shape():
    def fwd():
        inp = _fwd_setup_inputs(0)
        return _fwd_reference(*[inp[k] for k in FWD_PARAMS])
    out = _jax.eval_shape(fwd)
    return out.shape, out.dtype

N_MICROBATCH = 1
ADAM_LR = 0.001
ADAM_B1 = 0.9
ADAM_B2 = 0.999
ADAM_EPS = 1e-08
ADAM_WD = 0.01
ADAM_STEP = 10
PER_EXAMPLE_BATCH_AXIS = {'x': 0, 'loss_target': 0}
SHARED_INPUTS = []
_WEIGHT_DTYPES = {'rel_bias': _jnp.float32, 'w_in': _jnp.float32, 'q_norm': _jnp.float32, 'k_norm': _jnp.float32, 'sink': _jnp.float32, 'out_norm_a': _jnp.float32, 'out_norm_b': _jnp.float32, 'w_out': _jnp.float32, 'ln1_g': _jnp.float32, 'ln1_b': _jnp.float32, 'w_gate': _jnp.float32, 'w_up': _jnp.float32, 'conv_w': _jnp.float32, 'conv_b': _jnp.float32, 'w_down': _jnp.float32, 'ln2_g': _jnp.float32, 'ln2_b': _jnp.float32}
MOMENT_SCALE = {'rel_bias': 1.306190e-01, 'w_in': 1.460906e-01, 'q_norm': 2.609259e-01, 'k_norm': 2.791400e-01, 'sink': 4.472493e-03, 'out_norm_a': 1.780760e-01, 'out_norm_b': 1.240081e-01, 'w_out': 3.111473e-01, 'ln1_g': 5.407390e+00, 'ln1_b': 1.314286e+00, 'w_gate': 3.625531e-02, 'w_up': 3.661654e-02, 'conv_w': 3.632487e-02, 'conv_b': 3.861160e-02, 'w_down': 1.215507e-01, 'ln2_g': 4.591686e+01, 'ln2_b': 3.699905e+00}


def _to_microbatches(a, axis):
    t = _jnp.moveaxis(a, axis, 0)
    t = t.reshape((N_MICROBATCH, t.shape[0] // N_MICROBATCH) + t.shape[1:])
    return _jnp.moveaxis(t, 1, axis + 1)


def setup_inputs(seed: int = 0) -> dict:
    inp = _fwd_setup_inputs(seed)
    key = _jax.random.fold_in(_jax.random.key(seed), 7919)
    shape, _ = _output_shape()
    out = dict(inp)
    out["loss_target"] = _jax.random.normal(_jax.random.fold_in(key, 0), shape, _jnp.float32)
    for i, name in enumerate(TWIN_WEIGHTS):
        w = inp[name].astype(_jnp.float32)
        if MOMENT_SCALE is None:
            s = _jnp.sqrt(_jnp.mean(_jnp.square(w)) + 1e-30)
        else:
            s = MOMENT_SCALE[name]
        km, kv = _jax.random.split(_jax.random.fold_in(key, i + 1))
        out[name] = w
        out["m_" + name] = s * _jax.random.normal(km, w.shape, _jnp.float32)
        out["v_" + name] = (s * s) * _jax.random.uniform(kv, w.shape, _jnp.float32, 0.5, 1.5)
    if N_MICROBATCH > 1:
        for name, axis in PER_EXAMPLE_BATCH_AXIS.items():
            out[name] = _to_microbatches(out[name], axis)
    return {'x': out['x'], 'rel_bias': out['rel_bias'], 'w_in': out['w_in'], 'q_norm': out['q_norm'], 'k_norm': out['k_norm'], 'sink': out['sink'], 'out_norm_a': out['out_norm_a'], 'out_norm_b': out['out_norm_b'], 'w_out': out['w_out'], 'ln1_g': out['ln1_g'], 'ln1_b': out['ln1_b'], 'w_gate': out['w_gate'], 'w_up': out['w_up'], 'conv_w': out['conv_w'], 'conv_b': out['conv_b'], 'w_down': out['w_down'], 'ln2_g': out['ln2_g'], 'ln2_b': out['ln2_b'], 'loss_target': out['loss_target'], 'm_rel_bias': out['m_rel_bias'], 'm_w_in': out['m_w_in'], 'm_q_norm': out['m_q_norm'], 'm_k_norm': out['m_k_norm'], 'm_sink': out['m_sink'], 'm_out_norm_a': out['m_out_norm_a'], 'm_out_norm_b': out['m_out_norm_b'], 'm_w_out': out['m_w_out'], 'm_ln1_g': out['m_ln1_g'], 'm_ln1_b': out['m_ln1_b'], 'm_w_gate': out['m_w_gate'], 'm_w_up': out['m_w_up'], 'm_conv_w': out['m_conv_w'], 'm_conv_b': out['m_conv_b'], 'm_w_down': out['m_w_down'], 'm_ln2_g': out['m_ln2_g'], 'm_ln2_b': out['m_ln2_b'], 'v_rel_bias': out['v_rel_bias'], 'v_w_in': out['v_w_in'], 'v_q_norm': out['v_q_norm'], 'v_k_norm': out['v_k_norm'], 'v_sink': out['v_sink'], 'v_out_norm_a': out['v_out_norm_a'], 'v_out_norm_b': out['v_out_norm_b'], 'v_w_out': out['v_w_out'], 'v_ln1_g': out['v_ln1_g'], 'v_ln1_b': out['v_ln1_b'], 'v_w_gate': out['v_w_gate'], 'v_w_up': out['v_w_up'], 'v_conv_w': out['v_conv_w'], 'v_conv_b': out['v_conv_b'], 'v_w_down': out['v_w_down'], 'v_ln2_g': out['v_ln2_g'], 'v_ln2_b': out['v_ln2_b']}


def _loss(weights, diff, rest, loss_target):
    with _jax.named_scope("forward"):
        args = {**rest, TWIN_DIFF_INPUT: diff, **{k: w.astype(_WEIGHT_DTYPES[k]) for k, w in weights.items()}}
        y = _forward(args)
    with _jax.named_scope("loss_head"):
        err = _jnp.square(y.astype(_jnp.float32) - loss_target)
        return 0.5 * _jnp.sum(_jnp.mean(err, axis=-1)) if err.ndim else 0.5 * err


def _adamw(w, g, m, v):
    m = ADAM_B1 * m + (1.0 - ADAM_B1) * g
    v = ADAM_B2 * v + (1.0 - ADAM_B2) * _jnp.square(g)
    m_hat = m / (1.0 - ADAM_B1 ** ADAM_STEP)
    v_hat = v / (1.0 - ADAM_B2 ** ADAM_STEP)
    delta = -ADAM_LR * (m_hat / (_jnp.sqrt(v_hat) + ADAM_EPS) + ADAM_WD * w)
    return delta, m, v


def reference(x, rel_bias, w_in, q_norm, k_norm, sink, out_norm_a, out_norm_b, w_out, ln1_g, ln1_b, w_gate, w_up, conv_w, conv_b, w_down, ln2_g, ln2_b, loss_target, m_rel_bias, m_w_in, m_q_norm, m_k_norm, m_sink, m_out_norm_a, m_out_norm_b, m_w_out, m_ln1_g, m_ln1_b, m_w_gate, m_w_up, m_conv_w, m_conv_b, m_w_down, m_ln2_g, m_ln2_b, v_rel_bias, v_w_in, v_q_norm, v_k_norm, v_sink, v_out_norm_a, v_out_norm_b, v_w_out, v_ln1_g, v_ln1_b, v_w_gate, v_w_up, v_conv_w, v_conv_b, v_w_down, v_ln2_g, v_ln2_b):
    given = dict(x=x, rel_bias=rel_bias, w_in=w_in, q_norm=q_norm, k_norm=k_norm, sink=sink, out_norm_a=out_norm_a, out_norm_b=out_norm_b, w_out=w_out, ln1_g=ln1_g, ln1_b=ln1_b, w_gate=w_gate, w_up=w_up, conv_w=conv_w, conv_b=conv_b, w_down=w_down, ln2_g=ln2_g, ln2_b=ln2_b, loss_target=loss_target, m_rel_bias=m_rel_bias, m_w_in=m_w_in, m_q_norm=m_q_norm, m_k_norm=m_k_norm, m_sink=m_sink, m_out_norm_a=m_out_norm_a, m_out_norm_b=m_out_norm_b, m_w_out=m_w_out, m_ln1_g=m_ln1_g, m_ln1_b=m_ln1_b, m_w_gate=m_w_gate, m_w_up=m_w_up, m_conv_w=m_conv_w, m_conv_b=m_conv_b, m_w_down=m_w_down, m_ln2_g=m_ln2_g, m_ln2_b=m_ln2_b, v_rel_bias=v_rel_bias, v_w_in=v_w_in, v_q_norm=v_q_norm, v_k_norm=v_k_norm, v_sink=v_sink, v_out_norm_a=v_out_norm_a, v_out_norm_b=v_out_norm_b, v_w_out=v_w_out, v_ln1_g=v_ln1_g, v_ln1_b=v_ln1_b, v_w_gate=v_w_gate, v_w_up=v_w_up, v_conv_w=v_conv_w, v_conv_b=v_conv_b, v_w_down=v_w_down, v_ln2_g=v_ln2_g, v_ln2_b=v_ln2_b)
    weights = {n: given[n] for n in TWIN_WEIGHTS}
    shared = {n: given[n] for n in SHARED_INPUTS}
    per_example = {n: given[n] for n in ['x']}
    grad_fn = _jax.value_and_grad(_loss, argnums=(0, 1))

    def one_microbatch(ex, loss_target):
        ex = dict(ex)
        diff = ex.pop(TWIN_DIFF_INPUT)
        return grad_fn(weights, diff, {**shared, **ex}, loss_target)

    if N_MICROBATCH == 1:
        loss, (grad_w, grad_x) = one_microbatch(per_example, given["loss_target"])
    else:
        def body(carry, xs):
            loss_sum, grad_sum = carry
            l_k, (gw_k, gx_k) = one_microbatch(xs[0], xs[1])
            with _jax.named_scope("update"):
                return (loss_sum + l_k, _jax.tree.map(_jnp.add, grad_sum, gw_k)), gx_k

        init = (_jnp.zeros((), _jnp.float32), _jax.tree.map(_jnp.zeros_like, weights))
        (loss, grad_w), grad_x = _jax.lax.scan(body, init, (per_example, given["loss_target"]))
    with _jax.named_scope("update"):
        delta_w, new_m, new_v = {}, {}, {}
        for n in TWIN_WEIGHTS:
            delta_w[n], new_m[n], new_v[n] = _adamw(weights[n], grad_w[n], given["m_" + n], given["v_" + n])
    return (loss, grad_x, *[grad_w[n] for n in TWIN_WEIGHTS], *[delta_w[n] for n in TWIN_WEIGHTS],
            *[new_m[n] for n in TWIN_WEIGHTS], *[new_v[n] for n in TWIN_WEIGHTS])
```

```python
import functools
import math

import jax
import jax.numpy as jnp
from jax import lax
from jax.experimental import pallas as pl
from jax.experimental.pallas import tpu as pltpu

F32 = jnp.float32
BF16 = jnp.bfloat16
MESH = pl.DeviceIdType.MESH

N_DEV = 8
D_MODEL = 1024
DEPTH = 2
HEAD_DIM = 64
KV = 2
GQ = 4
QW = KV * GQ * HEAD_DIM
KW = KV * HEAD_DIM
ROPE_W = QW + KW
IN_COLS = 2 * (QW + 2 * KW)
D_FF = 2816
GRID_W = 64
ROPE_THETA = 10000.0
WINDOW = 128
N_BUCKETS = 32
MAX_DISTANCE = 128
ALPHA = (2.0 * DEPTH) ** 0.25
RMS_EPS = 1e-6
LN_EPS = 1e-5
SCALE = HEAD_DIM ** -0.5
NEG = -1e30

ADAM_LR = 0.001
ADAM_B1 = 0.9
ADAM_B2 = 0.999
ADAM_EPS = 1e-08
ADAM_WD = 0.01
ADAM_STEP = 10

LANES = 128
VMEM_CAP = 60 * 1024 * 1024
SMALL_NAMES = ("rel_bias", "q_norm", "k_norm", "sink", "out_norm_a", "out_norm_b", "ln1_g", "ln1_b",
               "conv_b", "ln2_g", "ln2_b", "conv_w")


def _params(sem, est_bytes):
    limit = int(min(VMEM_CAP, est_bytes + (8 << 20)))
    return pltpu.CompilerParams(dimension_semantics=sem, vmem_limit_bytes=limit)


def _nbytes(shape, dtype):
    return math.prod(shape) * jnp.dtype(dtype).itemsize


def _exchange(parts, gather, name):
    n = len(parts)
    blk = [p.shape if gather else p.shape[1:] for p in parts]

    def body(*refs):
        ins, outs = refs[:n], refs[n:2 * n]
        send_sems, recv_sems, local_sems = refs[2 * n:]
        me = 4 * lax.axis_index("x") + 2 * lax.axis_index("y") + lax.axis_index("c")

        def src(k, j):
            return ins[k] if gather else ins[k].at[j]

        def remote(k, d):
            peer = lax.rem(me + d, N_DEV)
            return pltpu.make_async_remote_copy(
                src_ref=src(k, peer), dst_ref=outs[k].at[me],
                send_sem=send_sems.at[k * (N_DEV - 1) + d - 1], recv_sem=recv_sems.at[k * (N_DEV - 1) + d - 1],
                device_id=(peer // 4, lax.rem(peer // 2, 2), lax.rem(peer, 2)), device_id_type=MESH)

        def arrival(k, d):
            frm = lax.rem(me + N_DEV - d, N_DEV)
            return pltpu.make_async_remote_copy(
                src_ref=src(k, frm), dst_ref=outs[k].at[frm],
                send_sem=send_sems.at[k * (N_DEV - 1) + d - 1], recv_sem=recv_sems.at[k * (N_DEV - 1) + d - 1],
                device_id=(frm // 4, lax.rem(frm // 2, 2), lax.rem(frm, 2)), device_id_type=MESH)

        local = [pltpu.make_async_copy(src(k, me), outs[k].at[me], local_sems.at[k]) for k in range(n)]
        sends = [remote(k, d) for k in range(n) for d in range(1, N_DEV)]
        for cp in local + sends:
            cp.start()
        for k in range(n):
            for d in range(1, N_DEV):
                arrival(k, d).wait_recv()
        for cp in sends:
            cp.wait_send()
        for cp in local:
            cp.wait()

    hbm = pl.BlockSpec(memory_space=pltpu.HBM)
    return pl.pallas_call(
        body, name=name,
        out_shape=[jax.ShapeDtypeStruct((N_DEV,) + tuple(b), p.dtype) for b, p in zip(blk, parts)],
        in_specs=[hbm] * n, out_specs=[hbm] * n,
        scratch_shapes=[pltpu.SemaphoreType.DMA((n * (N_DEV - 1),)), pltpu.SemaphoreType.DMA((n * (N_DEV - 1),)),
                        pltpu.SemaphoreType.DMA((n,))],
    )(*parts)


def _mm(a_list, b_list, *, name, out_dtype, tm, tn, tk, trans_a=False, add=None, add_scale=1.0):
    na = len(a_list)
    if trans_a:
        K, M = a_list[0].shape
    else:
        M, K = a_list[0].shape
    N = b_list[0].shape[1]
    tm, tn, tk = min(tm, M), min(tn, N), min(tk, K)
    assert M % tm == 0 and N % tn == 0 and K % tk == 0, (name, M, N, K, tm, tn, tk)
    nk = K // tk
    dims = (((0,), (0,)), ((), ())) if trans_a else (((1,), (0,)), ((), ()))

    def body(*refs):
        a_refs, b_refs = refs[:na], refs[na:2 * na]
        add_ref = refs[2 * na] if add is not None else None
        o_ref, acc_ref = refs[-2], refs[-1]
        k = pl.program_id(2)

        @pl.when(k == 0)
        def _():
            acc_ref[...] = jnp.zeros_like(acc_ref)

        part = None
        for a_ref, b_ref in zip(a_refs, b_refs):
            prod = lax.dot_general(a_ref[...].astype(BF16), b_ref[...].astype(BF16), dims,
                                   preferred_element_type=F32)
            part = prod if part is None else part + prod
        acc_ref[...] += part

        @pl.when(k == nk - 1)
        def _():
            res = acc_ref[...]
            if add_ref is not None:
                res = res + add_scale * add_ref[...]
            o_ref[...] = res.astype(o_ref.dtype)

    if trans_a:
        a_spec = pl.BlockSpec((tk, tm), lambda i, j, k: (k, i))
    else:
        a_spec = pl.BlockSpec((tm, tk), lambda i, j, k: (i, k))
    b_spec = pl.BlockSpec((tk, tn), lambda i, j, k: (k, j))
    o_spec = pl.BlockSpec((tm, tn), lambda i, j, k: (i, j))
    in_specs = [a_spec] * na + [b_spec] * na + ([o_spec] if add is not None else [])
    est = (2 * na * (_nbytes((tm, tk), a_list[0].dtype) + _nbytes((tk, tn), b_list[0].dtype))
           + na * (_nbytes((tm, tk), BF16) + _nbytes((tk, tn), BF16))
           + 2 * _nbytes((tm, tn), out_dtype) + 3 * _nbytes((tm, tn), F32)
           + (2 * _nbytes((tm, tn), F32) if add is not None else 0))
    args = list(a_list) + list(b_list) + ([add] if add is not None else [])
    return pl.pallas_call(
        body, name=name, grid=(M // tm, N // tn, nk),
        out_shape=jax.ShapeDtypeStruct((M, N), out_dtype),
        in_specs=in_specs, out_specs=o_spec,
        scratch_shapes=[pltpu.VMEM((tm, tn), F32)],
        compiler_params=_params(("parallel", "parallel", "arbitrary"), est),
    )(*args)


def _mm_res_ln(a, w, res_hat, res_g, res_b, ln_g, ln_b, *, name, tm):
    T, K = a.shape
    D = w.shape[1]
    tm = min(tm, T)

    def body(a_ref, w_ref, rh_ref, rg_ref, rb_ref, g_ref, b_ref, xhat_ref, rstd_ref, xb_ref):
        branch = jnp.dot(a_ref[...].astype(BF16), w_ref[...], preferred_element_type=F32)
        z = ALPHA * (rh_ref[...] * rg_ref[...] + rb_ref[...]) + branch
        mu = jnp.mean(z, axis=1, keepdims=True)
        zc = z - mu
        var = jnp.mean(zc * zc, axis=1, keepdims=True)
        rstd = lax.rsqrt(var + LN_EPS)
        xhat = zc * rstd
        xhat_ref[...] = xhat
        rstd_ref[...] = rstd
        xb_ref[...] = (xhat * g_ref[...] + b_ref[...]).astype(BF16)

    row = pl.BlockSpec((tm, D), lambda i: (i, 0))
    vec = pl.BlockSpec((1, D), lambda i: (0, 0))
    est = (2 * (_nbytes((tm, K), a.dtype) + _nbytes((K, D), BF16)) + 4 * _nbytes((tm, D), F32) * 2
           + 6 * _nbytes((tm, D), F32))
    return pl.pallas_call(
        body, name=name, grid=(T // tm,),
        out_shape=(jax.ShapeDtypeStruct((T, D), F32), jax.ShapeDtypeStruct((T, 1), F32),
                   jax.ShapeDtypeStruct((T, D), BF16)),
        in_specs=[pl.BlockSpec((tm, K), lambda i: (i, 0)), pl.BlockSpec((K, D), lambda i: (0, 0)), row, vec, vec, vec, vec],
        out_specs=(row, pl.BlockSpec((tm, 1), lambda i: (i, 0)), row),
        compiler_params=_params(("parallel",), est),
    )(a, w, res_hat, res_g, res_b, ln_g, ln_b)


def _ln_bwd(xhat, rstd, ln_g, ln_b, *, name, dx=None, target=None, tm=256):
    T, D = xhat.shape
    tm = min(tm, T)
    head = target is not None

    def body(xhat_ref, rstd_ref, g_ref, b_ref, d_ref, dz_ref, dzb_ref, st_ref):
        i = pl.program_id(0)

        @pl.when(i == 0)
        def _():
            st_ref[...] = jnp.zeros_like(st_ref)

        xh = xhat_ref[...]
        g = g_ref[...]
        if head:
            err = (xh * g + b_ref[...]) - d_ref[...]
            dxv = err * (1.0 / D)
            st_ref[2:3, :] += 0.5 * jnp.sum(jnp.sum(err * err, axis=1, keepdims=True) * (1.0 / D), axis=0, keepdims=True)
        else:
            dxv = d_ref[...]
        st_ref[0:1, :] += jnp.sum(dxv * xh, axis=0, keepdims=True)
        st_ref[1:2, :] += jnp.sum(dxv, axis=0, keepdims=True)
        dxh = dxv * g
        m1 = jnp.mean(dxh, axis=1, keepdims=True)
        m2 = jnp.mean(dxh * xh, axis=1, keepdims=True)
        dz = rstd_ref[...] * (dxh - m1 - xh * m2)
        dz_ref[...] = dz
        dzb_ref[...] = dz.astype(BF16)

    row = pl.BlockSpec((tm, D), lambda i: (i, 0))
    vec = pl.BlockSpec((1, D), lambda i: (0, 0))
    est = 2 * 4 * _nbytes((tm, D), F32) + 6 * _nbytes((tm, D), F32)
    return pl.pallas_call(
        body, name=name, grid=(T // tm,),
        out_shape=(jax.ShapeDtypeStruct((T, D), F32), jax.ShapeDtypeStruct((T, D), BF16),
                   jax.ShapeDtypeStruct((8, D), F32)),
        in_specs=[row, pl.BlockSpec((tm, 1), lambda i: (i, 0)), vec, vec, row],
        out_specs=(row, row, pl.BlockSpec((8, D), lambda i: (0, 0))),
        compiler_params=_params(("arbitrary",), est),
    )(xhat, rstd, ln_g, ln_b, target if head else dx)


def _pair_swap(v, even):
    return jnp.where(even, pltpu.roll(v, LANES - 1, 1), pltpu.roll(v, 1, 1))


def _half_sums(v, lo):
    s_lo = jnp.sum(jnp.where(lo, v, 0.0), axis=1, keepdims=True)
    s_hi = jnp.sum(jnp.where(lo, 0.0, v), axis=1, keepdims=True)
    return jnp.where(lo, s_lo, s_hi)


def _qk_rope_fwd(h, gains, cos2, sin2, *, name, tm=256):
    T = h.shape[0]
    tm = min(tm, T)
    nch = ROPE_W // LANES

    def body(h_ref, g_ref, c_ref, s_ref, o_ref):
        lane = lax.broadcasted_iota(jnp.int32, (tm, LANES), 1)
        lo, even = lane < HEAD_DIM, lane % 2 == 0
        c, s = c_ref[...], s_ref[...]
        for j in range(nch):
            x = h_ref[:, j * LANES:(j + 1) * LANES]
            isq = j < QW // LANES
            g = g_ref[0:1, :] if isq else g_ref[1:2, :]
            r = lax.rsqrt(_half_sums(x * x, lo) * (1.0 / HEAD_DIM) + RMS_EPS)
            nrm = x * r * g
            out = nrm * c + _pair_swap(nrm, even) * s
            if isq:
                out = out * SCALE
            o_ref[:, j * LANES:(j + 1) * LANES] = out.astype(BF16)

    est = 2 * (_nbytes((tm, ROPE_W), F32) + _nbytes((tm, ROPE_W), BF16) + 2 * _nbytes((tm, LANES), F32)) + (4 << 20)
    return pl.pallas_call(
        body, name=name, grid=(T // tm,),
        out_shape=jax.ShapeDtypeStruct((T, ROPE_W), BF16),
        in_specs=[pl.BlockSpec((tm, ROPE_W), lambda i: (i, 0)), pl.BlockSpec((8, LANES), lambda i: (0, 0)),
                  pl.BlockSpec((tm, LANES), lambda i: (i, 0)), pl.BlockSpec((tm, LANES), lambda i: (i, 0))],
        out_specs=pl.BlockSpec((tm, ROPE_W), lambda i: (i, 0)),
        compiler_params=_params(("parallel",), est),
    )(h, gains, cos2, sin2)


def _qk_rope_bwd(h, d_out, gains, cos2, sin2, *, name, tm=256):
    T = h.shape[0]
    tm = min(tm, T)
    nch = ROPE_W // LANES

    def body(h_ref, d_ref, g_ref, c_ref, s_ref, dh_ref, dg_ref):
        i = pl.program_id(0)

        @pl.when(i == 0)
        def _():
            dg_ref[...] = jnp.zeros_like(dg_ref)

        lane = lax.broadcasted_iota(jnp.int32, (tm, LANES), 1)
        lo, even = lane < HEAD_DIM, lane % 2 == 0
        c, s = c_ref[...], s_ref[...]
        acc = [None, None]
        for j in range(nch):
            x = h_ref[:, j * LANES:(j + 1) * LANES]
            isq = j < QW // LANES
            g = g_ref[0:1, :] if isq else g_ref[1:2, :]
            d = d_ref[:, j * LANES:(j + 1) * LANES]
            if isq:
                d = d * SCALE
            r = lax.rsqrt(_half_sums(x * x, lo) * (1.0 / HEAD_DIM) + RMS_EPS)
            dn = d * c + _pair_swap(d * s, even)
            xr = x * r
            part = jnp.sum(dn * xr, axis=0, keepdims=True)
            acc[0 if isq else 1] = part if acc[0 if isq else 1] is None else acc[0 if isq else 1] + part
            dng = dn * g
            dx = r * dng - xr * (r * r) * (_half_sums(dng * x, lo) * (1.0 / HEAD_DIM))
            dh_ref[:, j * LANES:(j + 1) * LANES] = dx.astype(BF16)
        for row in range(2):
            folded = acc[row] + pltpu.roll(acc[row], HEAD_DIM, 1)
            dg_ref[row:row + 1, :] += folded

    est = 2 * (2 * _nbytes((tm, ROPE_W), F32) + _nbytes((tm, ROPE_W), BF16) + 2 * _nbytes((tm, LANES), F32)) + (4 << 20)
    return pl.pallas_call(
        body, name=name, grid=(T // tm,),
        out_shape=(jax.ShapeDtypeStruct((T, ROPE_W), BF16), jax.ShapeDtypeStruct((8, LANES), F32)),
        in_specs=[pl.BlockSpec((tm, ROPE_W), lambda i: (i, 0)), pl.BlockSpec((tm, ROPE_W), lambda i: (i, 0)),
                  pl.BlockSpec((8, LANES), lambda i: (0, 0)),
                  pl.BlockSpec((tm, LANES), lambda i: (i, 0)), pl.BlockSpec((tm, LANES), lambda i: (i, 0))],
        out_specs=(pl.BlockSpec((tm, ROPE_W), lambda i: (i, 0)), pl.BlockSpec((8, LANES), lambda i: (0, 0))),
        compiler_params=_params(("arbitrary",), est),
    )(h, d_out, gains, cos2, sin2)


def _attn_a_fwd(q, kT, v, *, tq=512, tk=512):
    G, NQ, HD = q.shape
    T = v.shape[1]
    tq, tk = min(tq, NQ), min(tk, T)
    nk = T // tk

    def body(q_ref, kT_ref, v_ref, o_ref, lse_ref, m_sc, l_sc, acc_sc):
        kv = pl.program_id(2)

        @pl.when(kv == 0)
        def _():
            m_sc[...] = jnp.full_like(m_sc, NEG)
            l_sc[...] = jnp.zeros_like(l_sc)
            acc_sc[...] = jnp.zeros_like(acc_sc)

        s = jnp.dot(q_ref[...], kT_ref[...], preferred_element_type=F32)
        m_prev = m_sc[...]
        m_new = jnp.maximum(m_prev, jnp.max(s, axis=1, keepdims=True))
        a = jnp.exp(m_prev - m_new)
        p = jnp.exp(s - m_new)
        l_sc[...] = a * l_sc[...] + jnp.sum(p, axis=1, keepdims=True)
        acc_sc[...] = a * acc_sc[...] + jnp.dot(p.astype(BF16), v_ref[...], preferred_element_type=F32)
        m_sc[...] = m_new

        @pl.when(kv == nk - 1)
        def _():
            l = l_sc[...]
            o_ref[...] = acc_sc[...] / l
            lse_ref[...] = m_sc[...] + jnp.log(l)

    est = 6 * _nbytes((tq, tk), F32) + (8 << 20)
    return pl.pallas_call(
        body, name="attn_a_fwd", grid=(G, NQ // tq, nk),
        out_shape=(jax.ShapeDtypeStruct((G, NQ, HD), F32), jax.ShapeDtypeStruct((G, NQ, 1), F32)),
        in_specs=[pl.BlockSpec((None, tq, HD), lambda g, i, k: (g, i, 0)),
                  pl.BlockSpec((None, HD, tk), lambda g, i, k: (g, 0, k)),
                  pl.BlockSpec((None, tk, HD), lambda g, i, k: (g, k, 0))],
        out_specs=(pl.BlockSpec((None, tq, HD), lambda g, i, k: (g, i, 0)),
                   pl.BlockSpec((None, tq, 1), lambda g, i, k: (g, i, 0))),
        scratch_shapes=[pltpu.VMEM((tq, 1), F32), pltpu.VMEM((tq, 1), F32), pltpu.VMEM((tq, HD), F32)],
        compiler_params=_params(("parallel", "parallel", "arbitrary"), est),
    )(q, kT, v)


def _attn_a_dq(q, kT, k, vT, do, o, lse, *, tq=512, tk=512):
    G, NQ, HD = q.shape
    T = k.shape[1]
    tq, tk = min(tq, NQ), min(tk, T)
    nk = T // tk

    def body(q_ref, kT_ref, k_ref, vT_ref, do_ref, o_ref, lse_ref, dq_ref, dl_ref, acc_sc, dl_sc):
        kv = pl.program_id(2)

        @pl.when(kv == 0)
        def _():
            acc_sc[...] = jnp.zeros_like(acc_sc)
            dl_sc[...] = jnp.sum(do_ref[...] * o_ref[...], axis=1, keepdims=True)

        s = jnp.dot(q_ref[...], kT_ref[...], preferred_element_type=F32)
        p = jnp.exp(s - lse_ref[...])
        dp = jnp.dot(do_ref[...].astype(BF16), vT_ref[...], preferred_element_type=F32)
        ds = p * (dp - dl_sc[...])
        acc_sc[...] += jnp.dot(ds.astype(BF16), k_ref[...], preferred_element_type=F32)

        @pl.when(kv == nk - 1)
        def _():
            dq_ref[...] = acc_sc[...]
            dl_ref[...] = dl_sc[...]

    qrow = pl.BlockSpec((None, tq, HD), lambda g, i, k: (g, i, 0))
    qcol = pl.BlockSpec((None, tq, 1), lambda g, i, k: (g, i, 0))
    est = 6 * _nbytes((tq, tk), F32) + (8 << 20)
    return pl.pallas_call(
        body, name="attn_a_dq", grid=(G, NQ // tq, nk),
        out_shape=(jax.ShapeDtypeStruct((G, NQ, HD), F32), jax.ShapeDtypeStruct((G, NQ, 1), F32)),
        in_specs=[qrow, pl.BlockSpec((None, HD, tk), lambda g, i, k: (g, 0, k)),
                  pl.BlockSpec((None, tk, HD), lambda g, i, k: (g, k, 0)),
                  pl.BlockSpec((None, HD, tk), lambda g, i, k: (g, 0, k)), qrow, qrow, qcol],
        out_specs=(qrow, qcol),
        scratch_shapes=[pltpu.VMEM((tq, HD), F32), pltpu.VMEM((tq, 1), F32)],
        compiler_params=_params(("parallel", "parallel", "arbitrary"), est),
    )(q, kT, k, vT, do, o, lse)


def _attn_a_dkv(k, v, qT, q, doT, do, lse_row, delta_row, *, tq=512, tk=512):
    G, T, HD = k.shape
    NQ = q.shape[1]
    tq, tk = min(tq, NQ), min(tk, T)
    nq = NQ // tq

    def body(k_ref, v_ref, qT_ref, q_ref, doT_ref, do_ref, lse_ref, dl_ref, dk_ref, dv_ref, dk_sc, dv_sc):
        qi = pl.program_id(2)

        @pl.when(qi == 0)
        def _():
            dk_sc[...] = jnp.zeros_like(dk_sc)
            dv_sc[...] = jnp.zeros_like(dv_sc)

        sT = jnp.dot(k_ref[...], qT_ref[...], preferred_element_type=F32)
        pT = jnp.exp(sT - lse_ref[...])
        dv_sc[...] += jnp.dot(pT.astype(BF16), do_ref[...], preferred_element_type=F32)
        dpT = jnp.dot(v_ref[...], doT_ref[...], preferred_element_type=F32)
        dsT = pT * (dpT - dl_ref[...])
        dk_sc[...] += jnp.dot(dsT.astype(BF16), q_ref[...], preferred_element_type=F32)

        @pl.when(qi == nq - 1)
        def _():
            dk_ref[...] = dk_sc[...]
            dv_ref[...] = dv_sc[...]

    krow = pl.BlockSpec((None, tk, HD), lambda g, j, i: (g, j, 0))
    qrow = pl.BlockSpec((None, tq, HD), lambda g, j, i: (g, i, 0))
    qtr = pl.BlockSpec((None, HD, tq), lambda g, j, i: (g, 0, i))
    qvec = pl.BlockSpec((None, 1, tq), lambda g, j, i: (g, 0, i))
    est = 6 * _nbytes((tq, tk), F32) + (8 << 20)
    return pl.pallas_call(
        body, name="attn_a_dkv", grid=(G, T // tk, nq),
        out_shape=(jax.ShapeDtypeStruct((G, T, HD), F32), jax.ShapeDtypeStruct((G, T, HD), F32)),
        in_specs=[krow, krow, qtr, qrow, qtr, qrow, qvec, qvec],
        out_specs=(krow, krow),
        scratch_shapes=[pltpu.VMEM((tk, HD), F32), pltpu.VMEM((tk, HD), F32)],
        compiler_params=_params(("parallel", "parallel", "arbitrary"), est),
    )(k, v, qT, q, doT, do, lse_row, delta_row)


WB = WINDOW
WK = 3 * WINDOW


def _win_specs(T):
    nb = T // WB
    q4 = pl.BlockSpec((None, GQ, WB, HEAD_DIM), lambda g, n: (g, 0, n, 0))
    col = pl.BlockSpec((None, GQ * WB, 1), lambda g, n: (g, 0, 0))
    bias = pl.BlockSpec((None, GQ * WB, WK), lambda g, n: (g, 0, 0))
    kt = [pl.BlockSpec((None, HEAD_DIM, WB), functools.partial(lambda g, n, o: (g, 0, n + o), o=o)) for o in range(3)]
    kr = [pl.BlockSpec((None, WB, HEAD_DIM), functools.partial(lambda g, n, o: (g, n + o, 0), o=o)) for o in range(3)]
    return nb, q4, col, bias, kt, kr


def _win_mask(n, T):
    qq = lax.rem(lax.broadcasted_iota(jnp.int32, (GQ * WB, WK), 0), WB)
    kk = lax.broadcasted_iota(jnp.int32, (GQ * WB, WK), 1)
    rel = kk - WB - qq
    kabs = n * WB - WB + kk
    return (jnp.abs(rel) <= WINDOW) & (kabs >= 0) & (kabs < T)


def _attn_b_fwd(q, kTp, vp, bias, sink_rows):
    T = q.shape[2]
    nb, q4, col, bias_spec, kt, kr = _win_specs(T)

    def body(q_ref, k0, k1, k2, v0, v1, v2, b_ref, sk_ref, o_ref, lse_ref):
        n = pl.program_id(1)
        qv = q_ref[...].reshape(GQ * WB, HEAD_DIM)
        kT = jnp.concatenate([k0[...], k1[...], k2[...]], axis=1)
        vv = jnp.concatenate([v0[...], v1[...], v2[...]], axis=0)
        s = jnp.dot(qv, kT, preferred_element_type=F32) + b_ref[...]
        s = jnp.where(_win_mask(n, T), s, NEG)
        sk = sk_ref[...]
        m = jnp.maximum(jnp.max(s, axis=1, keepdims=True), sk)
        p = jnp.exp(s - m)
        den = jnp.sum(p, axis=1, keepdims=True) + jnp.exp(sk - m)
        o = jnp.dot(p.astype(BF16), vv, preferred_element_type=F32) / den
        o_ref[...] = o.reshape(GQ, WB, HEAD_DIM)
        lse_ref[...] = (m + jnp.log(den)).reshape(GQ, WB, 1)

    return pl.pallas_call(
        body, name="attn_b_fwd", grid=(KV, nb),
        out_shape=(jax.ShapeDtypeStruct((KV, GQ, T, HEAD_DIM), F32), jax.ShapeDtypeStruct((KV, GQ, T, 1), F32)),
        in_specs=[q4] + kt + kr + [bias_spec, col],
        out_specs=(q4, pl.BlockSpec((None, GQ, WB, 1), lambda g, n: (g, 0, n, 0))),
        compiler_params=_params(("parallel", "parallel"), 24 << 20),
    )(q, kTp, kTp, kTp, vp, vp, vp, bias, sink_rows)


def _attn_b_bwd(q, kTp, kp, vTp, do, o, lse, bias, sink_rows):
    T = q.shape[2]
    nb, q4, col, bias_spec, kt, kr = _win_specs(T)
    Tp = T + 2 * WB

    def body(q_ref, k0, k1, k2, r0, r1, r2, w0, w1, w2, do_ref, o_ref, lse_ref, b_ref, sk_ref,
             dq_ref, dk_ref, dv_ref, db_ref, dsk_ref):
        n = pl.program_id(1)

        @pl.when(n == 0)
        def _():
            dk_ref[...] = jnp.zeros_like(dk_ref)
            dv_ref[...] = jnp.zeros_like(dv_ref)
            db_ref[...] = jnp.zeros_like(db_ref)
            dsk_ref[...] = jnp.zeros_like(dsk_ref)

        qv = q_ref[...].reshape(GQ * WB, HEAD_DIM)
        kT = jnp.concatenate([k0[...], k1[...], k2[...]], axis=1)
        kk = jnp.concatenate([r0[...], r1[...], r2[...]], axis=0)
        vT = jnp.concatenate([w0[...], w1[...], w2[...]], axis=1)
        dov = do_ref[...].reshape(GQ * WB, HEAD_DIM)
        lse = lse_ref[...].reshape(GQ * WB, 1)
        delta = jnp.sum(dov * o_ref[...].reshape(GQ * WB, HEAD_DIM), axis=1, keepdims=True)
        s = jnp.dot(qv, kT, preferred_element_type=F32) + b_ref[...]
        s = jnp.where(_win_mask(n, T), s, NEG)
        p = jnp.exp(s - lse)
        dob = dov.astype(BF16)
        dp = jnp.dot(dob, vT, preferred_element_type=F32)
        ds = p * (dp - delta)
        db_ref[...] += ds
        dsk_ref[...] -= jnp.exp(sk_ref[...] - lse) * delta
        dsb = ds.astype(BF16)
        dq_ref[...] = jnp.dot(dsb, kk, preferred_element_type=F32).reshape(GQ, WB, HEAD_DIM)
        tn = (((0,), (0,)), ((), ()))
        rows = pl.ds(pl.multiple_of(n * WB, WB), WK)
        dk_ref[rows, :] += lax.dot_general(dsb, qv, tn, preferred_element_type=F32)
        dv_ref[rows, :] += lax.dot_general(p.astype(BF16), dob, tn, preferred_element_type=F32)

    kacc = pl.BlockSpec((None, Tp, HEAD_DIM), lambda g, n: (g, 0, 0))
    return pl.pallas_call(
        body, name="attn_b_bwd", grid=(KV, nb),
        out_shape=(jax.ShapeDtypeStruct((KV, GQ, T, HEAD_DIM), F32),
                   jax.ShapeDtypeStruct((KV, Tp, HEAD_DIM), F32), jax.ShapeDtypeStruct((KV, Tp, HEAD_DIM), F32),
                   jax.ShapeDtypeStruct((KV, GQ * WB, WK), F32), jax.ShapeDtypeStruct((KV, GQ * WB, 1), F32)),
        in_specs=[q4] + kt + kr + kt + [q4, q4, pl.BlockSpec((None, GQ, WB, 1), lambda g, n: (g, 0, n, 0)), bias_spec, col],
        out_specs=(q4, kacc, kacc, bias_spec, col),
        compiler_params=_params(("parallel", "arbitrary"), 40 << 20),
    )(q, kTp, kTp, kTp, kp, kp, kp, vTp, vTp, vTp, do, o, lse, bias, sink_rows)


def _bias_table(rel_bias_t, bucket):
    nh, n = rel_bias_t.shape[0], bucket.shape[1]

    def body(rb_ref, bk_ref, o_ref):
        bk = bk_ref[...]
        out = jnp.zeros((nh, n), F32)
        for b in range(N_BUCKETS):
            out = jnp.where(bk == b, rb_ref[:, b:b + 1], out)
        o_ref[...] = out

    return pl.pallas_call(
        body, name="bias_table", out_shape=jax.ShapeDtypeStruct((nh, n), F32),
        compiler_params=pltpu.CompilerParams(vmem_limit_bytes=32 << 20),
    )(rel_bias_t, bucket)


def _bias_sink_grads(db_list, dsk_list, bucket):
    L = len(db_list)

    def body(*refs):
        db_refs, dsk_refs, bk_ref = refs[:L], refs[L:2 * L], refs[2 * L]
        drb_ref, dsink_ref = refs[2 * L + 1], refs[2 * L + 2]
        tot = db_refs[0][...]
        for r in db_refs[1:]:
            tot = tot + r[...]
        bk = bk_ref[...]
        lane = lax.broadcasted_iota(jnp.int32, (2 * GQ, N_BUCKETS), 1)
        out = jnp.zeros((2 * GQ, N_BUCKETS), F32)
        for b in range(N_BUCKETS):
            sb = jnp.sum(jnp.where(bk == b, tot, 0.0), axis=1, keepdims=True)
            out = jnp.where(lane == b, sb, out)
        drb_ref[...] = out
        for l in range(L):
            dsink_ref[l] = jnp.sum(dsk_refs[l][...], axis=1, keepdims=True)

    return pl.pallas_call(
        body, name="bias_sink_grads",
        out_shape=(jax.ShapeDtypeStruct((2 * GQ, N_BUCKETS), F32), jax.ShapeDtypeStruct((L, 2 * GQ, 1), F32)),
        compiler_params=pltpu.CompilerParams(vmem_limit_bytes=32 << 20),
    )(*db_list, *dsk_list, bucket)


def _outnorm_fwd(oa, ob, ga, gb, *, tm=512):
    T = oa.shape[0]
    tm = min(tm, T)

    def body(oa_ref, ob_ref, ga_ref, gb_ref, y_ref):
        for j, (o_ref, g_ref) in enumerate(((oa_ref, ga_ref), (ob_ref, gb_ref))):
            o = o_ref[...]
            r = lax.rsqrt(jnp.mean(o * o, axis=1, keepdims=True) + RMS_EPS)
            y_ref[:, j * QW:(j + 1) * QW] = (o * r * g_ref[...]).astype(BF16)

    half = pl.BlockSpec((tm, QW), lambda i: (i, 0))
    vec = pl.BlockSpec((1, QW), lambda i: (0, 0))
    return pl.pallas_call(
        body, name="outnorm_fwd", grid=(T // tm,),
        out_shape=jax.ShapeDtypeStruct((T, 2 * QW), BF16),
        in_specs=[half, half, vec, vec], out_specs=pl.BlockSpec((tm, 2 * QW), lambda i: (i, 0)),
        compiler_params=_params(("parallel",), 16 << 20),
    )(oa, ob, ga, gb)


def _outnorm_bwd(dy, oa, ob, ga, gb, *, tm=512):
    T = oa.shape[0]
    tm = min(tm, T)

    def body(dy_ref, oa_ref, ob_ref, ga_ref, gb_ref, doa_ref, dob_ref, dg_ref):
        i = pl.program_id(0)

        @pl.when(i == 0)
        def _():
            dg_ref[...] = jnp.zeros_like(dg_ref)

        for j, (o_ref, g_ref, d_ref) in enumerate(((oa_ref, ga_ref, doa_ref), (ob_ref, gb_ref, dob_ref))):
            o = o_ref[...]
            d = dy_ref[:, j * QW:(j + 1) * QW]
            r = lax.rsqrt(jnp.mean(o * o, axis=1, keepdims=True) + RMS_EPS)
            orr = o * r
            dg_ref[j:j + 1, :] += jnp.sum(d * orr, axis=0, keepdims=True)
            dgv = d * g_ref[...]
            d_ref[...] = r * dgv - orr * (r * r) * jnp.mean(dgv * o, axis=1, keepdims=True)

    half = pl.BlockSpec((tm, QW), lambda i: (i, 0))
    vec = pl.BlockSpec((1, QW), lambda i: (0, 0))
    return pl.pallas_call(
        body, name="outnorm_bwd", grid=(T // tm,),
        out_shape=(jax.ShapeDtypeStruct((T, QW), F32), jax.ShapeDtypeStruct((T, QW), F32),
                   jax.ShapeDtypeStruct((8, QW), F32)),
        in_specs=[pl.BlockSpec((tm, 2 * QW), lambda i: (i, 0)), half, half, vec, vec],
        out_specs=(half, half, pl.BlockSpec((8, QW), lambda i: (0, 0))),
        compiler_params=_params(("arbitrary",), 24 << 20),
    )(dy, oa, ob, ga, gb)


GELU_C = math.sqrt(2.0 / math.pi)
GELU_A = 0.044715
HALO = 8


def _gelu_parts(x):
    t = jnp.tanh(GELU_C * (x + GELU_A * (x * x * x)))
    return 0.5 * (1.0 + t), t


def _halo_specs(tm, tn, T):
    nh = tm // HALO
    last = T // HALO - 1
    cur = pl.BlockSpec((tm, tn), lambda j, i: (i, j))
    prev = pl.BlockSpec((HALO, tn), lambda j, i: (jnp.maximum(i * nh - 1, 0), j))
    nxt = pl.BlockSpec((HALO, tn), lambda j, i: (jnp.minimum((i + 1) * nh, last), j))
    return cur, prev, nxt


def _conv_glu_fwd(g, u, conv_w, conv_b, *, tm=256, tn=1408):
    T, F = g.shape
    tm, tn = min(tm, T), min(tn, F)
    cur, prev, nxt = _halo_specs(tm, tn, T)

    def body(g_ref, gp_ref, gn_ref, u_ref, w_ref, b_ref, a_ref):
        i = pl.program_id(1)
        gv = g_ref[...]
        row = lax.broadcasted_iota(jnp.int32, (tm, tn), 0)
        before = jnp.where(i * tm > 0, gp_ref[HALO - 1:HALO, :], 0.0)
        after = jnp.where((i + 1) * tm < T, gn_ref[0:1, :], 0.0)
        gm1 = jnp.where(row == 0, before, pltpu.roll(gv, 1, 0))
        gp1 = jnp.where(row == tm - 1, after, pltpu.roll(gv, tm - 1, 0))
        gc = ((b_ref[...] + gm1 * w_ref[0:1, :]) + gv * w_ref[1:2, :]) + gp1 * w_ref[2:3, :]
        cdf, _ = _gelu_parts(gc)
        a_ref[...] = (gc * cdf * u_ref[...]).astype(BF16)

    wspec = pl.BlockSpec((8, tn), lambda j, i: (0, j))
    est = 2 * (3 * _nbytes((tm, tn), F32)) + 8 * _nbytes((tm, tn), F32)
    return pl.pallas_call(
        body, name="conv_glu_fwd", grid=(F // tn, T // tm),
        out_shape=jax.ShapeDtypeStruct((T, F), BF16),
        in_specs=[cur, prev, nxt, cur, wspec, pl.BlockSpec((1, tn), lambda j, i: (0, j))],
        out_specs=cur,
        compiler_params=_params(("parallel", "parallel"), est),
    )(g, g, g, u, conv_w, conv_b)


def _conv_glu_bwd(dact, g, u, conv_w, conv_b, *, tm=256, tn=1408):
    T, F = g.shape
    tm, tn = min(tm, T), min(tn, F)
    cur, prev, nxt = _halo_specs(tm, tn, T)
    te = tm + 2 * HALO

    def body(d_ref, dp_ref, dn_ref, g_ref, gp_ref, gn_ref, u_ref, up_ref, un_ref, w_ref, b_ref,
             dg_ref, du_ref, dc_ref):
        i = pl.program_id(1)

        @pl.when(i == 0)
        def _():
            dc_ref[...] = jnp.zeros_like(dc_ref)

        grow = i * tm - HALO + lax.broadcasted_iota(jnp.int32, (te, tn), 0)
        valid = (grow >= 0) & (grow < T)
        ge = jnp.where(valid, jnp.concatenate([gp_ref[...], g_ref[...], gn_ref[...]], axis=0), 0.0)
        ue = jnp.concatenate([up_ref[...], u_ref[...], un_ref[...]], axis=0)
        de = jnp.concatenate([dp_ref[...], d_ref[...], dn_ref[...]], axis=0)
        w0, w1, w2 = w_ref[0:1, :], w_ref[1:2, :], w_ref[2:3, :]
        gm1 = pltpu.roll(ge, 1, 0)
        gp1 = pltpu.roll(ge, te - 1, 0)
        gc = ((b_ref[...] + gm1 * w0) + ge * w1) + gp1 * w2
        cdf, t = _gelu_parts(gc)
        dgelu = cdf + 0.5 * gc * (1.0 - t * t) * (GELU_C * (1.0 + 3.0 * GELU_A * (gc * gc)))
        dgc = jnp.where(valid, de * ue * dgelu, 0.0)
        dge = w0 * pltpu.roll(dgc, te - 1, 0) + w1 * dgc + w2 * pltpu.roll(dgc, 1, 0)
        mid = slice(HALO, HALO + tm)
        dg_ref[...] = dge[mid].astype(BF16)
        du_ref[...] = (de[mid] * (gc[mid] * cdf[mid])).astype(BF16)
        dgm = dgc[mid]
        dc_ref[0:1, :] += jnp.sum(dgm * gm1[mid], axis=0, keepdims=True)
        dc_ref[1:2, :] += jnp.sum(dgm * ge[mid], axis=0, keepdims=True)
        dc_ref[2:3, :] += jnp.sum(dgm * gp1[mid], axis=0, keepdims=True)
        dc_ref[3:4, :] += jnp.sum(dgm, axis=0, keepdims=True)

    wspec = pl.BlockSpec((8, tn), lambda j, i: (0, j))
    est = 2 * (3 * _nbytes((tm, tn), F32) + 2 * _nbytes((tm, tn), BF16)) + 16 * _nbytes((te, tn), F32)
    return pl.pallas_call(
        body, name="conv_glu_bwd", grid=(F // tn, T // tm),
        out_shape=(jax.ShapeDtypeStruct((T, F), BF16), jax.ShapeDtypeStruct((T, F), BF16),
                   jax.ShapeDtypeStruct((8, F), F32)),
        in_specs=[cur, prev, nxt, cur, prev, nxt, cur, prev, nxt, wspec, pl.BlockSpec((1, tn), lambda j, i: (0, j))],
        out_specs=(cur, cur, wspec),
        compiler_params=_params(("parallel", "arbitrary"), est),
    )(dact, dact, dact, g, g, g, u, u, u, conv_w, conv_b)


def _adamw_math(w, g, m, v):
    m = ADAM_B1 * m + (1.0 - ADAM_B1) * g
    v = ADAM_B2 * v + (1.0 - ADAM_B2) * (g * g)
    m_hat = m / (1.0 - ADAM_B1 ** ADAM_STEP)
    v_hat = v / (1.0 - ADAM_B2 ** ADAM_STEP)
    delta = -ADAM_LR * (m_hat / (jnp.sqrt(v_hat) + ADAM_EPS) + ADAM_WD * w)
    return delta, m, v


def _adamw(w, m, v, gparts, *, name, tr):
    R, C = w.shape
    tr = min(tr, R)
    assert R % tr == 0

    def body(w_ref, m_ref, v_ref, gp_ref, g_ref, d_ref, nm_ref, nv_ref):
        g = gp_ref[0].astype(F32)
        for j in range(1, N_DEV):
            g = g + gp_ref[j].astype(F32)
        delta, nm, nv = _adamw_math(w_ref[...], g, m_ref[...], v_ref[...])
        g_ref[...] = g
        d_ref[...] = delta
        nm_ref[...] = nm
        nv_ref[...] = nv

    blk = pl.BlockSpec((tr, C), lambda i: (i, 0))
    out = jax.ShapeDtypeStruct((R, C), F32)
    return pl.pallas_call(
        body, name=name, grid=(R // tr,), out_shape=(out, out, out, out),
        in_specs=[blk, blk, blk, pl.BlockSpec((N_DEV, tr, C), lambda i: (0, i, 0))],
        out_specs=(blk, blk, blk, blk),
        compiler_params=_params(("parallel",), 24 << 20),
    )(w, m, v, gparts)


def _rope_tables(T):
    rows_n = T // GRID_W
    row = jnp.repeat(jnp.arange(rows_n, dtype=F32), GRID_W)
    col = jnp.tile(jnp.arange(GRID_W, dtype=F32), rows_n)
    half = HEAD_DIM // 2
    inv_freq = ROPE_THETA ** (-jnp.arange(0, half, 2, dtype=F32) / half)
    ang = jnp.concatenate([row[:, None] * inv_freq, col[:, None] * inv_freq], axis=-1)
    cos, sin = jnp.cos(ang), jnp.sin(ang)
    cos64 = jnp.repeat(cos, 2, axis=-1)
    sin64 = jnp.stack([-sin, sin], axis=-1).reshape(T, HEAD_DIM)
    return jnp.tile(cos64, (1, 2)), jnp.tile(sin64, (1, 2))


def _t5_bucket(rel):
    half = N_BUCKETS // 2
    max_exact = half // 2
    bucket = jnp.where(rel > 0, half, 0)
    rp = jnp.abs(rel)
    rpf = jnp.maximum(rp, 1).astype(F32)
    large = max_exact + (jnp.log(rpf / max_exact) / math.log(MAX_DISTANCE / max_exact)
                         * (half - max_exact)).astype(jnp.int32)
    large = jnp.minimum(large, half - 1)
    return bucket + jnp.where(rp < max_exact, rp, large)


def _window_buckets():
    qpos = jnp.arange(WB, dtype=jnp.int32)
    kpos = jnp.arange(WK, dtype=jnp.int32) - WB
    return _t5_bucket(kpos[None, :] - qpos[:, None])


def _heads_first(a, nh):
    T = a.shape[0]
    return a.reshape(T, nh, HEAD_DIM).transpose(1, 0, 2)


def _heads_last(a):
    nh, T, _ = a.shape
    return a.transpose(1, 0, 2).reshape(T, nh * HEAD_DIM)


def _pad_keys(a):
    return jnp.pad(a, ((0, 0), (WB, WB), (0, 0)))


def _row(v):
    return v.reshape(1, -1)


def _rows8(rows, width):
    a = jnp.stack(list(rows), axis=0)
    return jnp.pad(a, ((0, 8 - a.shape[0]), (0, 0)))


def _layer_fwd(l, xin, W, tabs):
    xhat, xg, xb, x16 = xin
    T = xhat.shape[0]
    cos2, sin2, bias, _ = tabs
    h = _mm([x16], [W["w_in"][l]], name="mm_in", out_dtype=F32, tm=512, tn=IN_COLS, tk=D_MODEL)
    gains = _rows8([jnp.tile(W["q_norm"][l], 2), jnp.tile(W["k_norm"][l], 2)], LANES)
    roped = _qk_rope_fwd(h, gains, cos2, sin2, name="qk_rope_fwd")
    qa = _heads_first(roped[:, :QW], KV * GQ).reshape(KV, GQ * T, HEAD_DIM)
    ka = _heads_first(roped[:, QW:], KV)
    va = _heads_first(h[:, ROPE_W:ROPE_W + KW].astype(BF16), KV)
    kaT, vaT = ka.transpose(0, 2, 1), va.transpose(0, 2, 1)
    oa, lse_a = _attn_a_fwd(qa, kaT, va)
    o0 = ROPE_W + KW
    qb = _heads_first((h[:, o0:o0 + QW] * SCALE).astype(BF16), KV * GQ).reshape(KV, GQ, T, HEAD_DIM)
    kb = _pad_keys(_heads_first(h[:, o0 + QW:o0 + QW + KW].astype(BF16), KV))
    vb = _pad_keys(_heads_first(h[:, o0 + QW + KW:].astype(BF16), KV))
    kbT, vbT = kb.transpose(0, 2, 1), vb.transpose(0, 2, 1)
    sink_rows = jnp.repeat(W["sink"][l], WB).reshape(KV, GQ * WB, 1)
    ob, lse_b = _attn_b_fwd(qb, kbT, vb, bias, sink_rows)
    oa_t = _heads_last(oa.reshape(KV * GQ, T, HEAD_DIM))
    ob_t = _heads_last(ob.reshape(KV * GQ, T, HEAD_DIM))
    ga, gb = _row(W["out_norm_a"][l]), _row(W["out_norm_b"][l])
    ycat = _outnorm_fwd(oa_t, ob_t, ga, gb)
    g1, b1 = _row(W["ln1_g"][l]), _row(W["ln1_b"][l])
    x1hat, rstd1, x1_16 = _mm_res_ln(ycat, W["w_out"][l], xhat, xg, xb, g1, b1, name="mm_out_ln", tm=512)
    gate = _mm([x1_16], [W["w_gate"][l]], name="mm_gate", out_dtype=F32, tm=512, tn=D_FF // 2, tk=D_MODEL)
    up = _mm([x1_16], [W["w_up"][l]], name="mm_up", out_dtype=F32, tm=512, tn=D_FF // 2, tk=D_MODEL)
    cw = jnp.pad(W["conv_w"][l], ((0, 5), (0, 0)))
    cb = _row(W["conv_b"][l])
    act = _conv_glu_fwd(gate, up, cw, cb)
    g2, b2 = _row(W["ln2_g"][l]), _row(W["ln2_b"][l])
    x2hat, rstd2, x2_16 = _mm_res_ln(act, W["w_down"][l], x1hat, g1, b1, g2, b2, name="mm_down_ln", tm=256)
    saved = dict(x16=x16, h=h, gains=gains, qa=qa, ka=ka, kaT=kaT, va=va, vaT=vaT, oa=oa, lse_a=lse_a,
                 qb=qb, kb=kb, kbT=kbT, vbT=vbT, ob=ob, lse_b=lse_b, sink_rows=sink_rows, oa_t=oa_t, ob_t=ob_t,
                 ga=ga, gb=gb, ycat=ycat, x1hat=x1hat, rstd1=rstd1, x1_16=x1_16, g1=g1, b1=b1, gate=gate, up=up,
                 cw=cw, cb=cb, act=act, x2hat=x2hat, rstd2=rstd2, g2=g2, b2=b2)
    return (x2hat, g2, b2, x2_16), saved


def _layer_bwd(l, S, W, WT, tabs, dz2, dz2_16, stats2):
    cos2, sin2, bias, _ = tabs
    T = dz2.shape[0]
    G = {}
    G["ln2_g"], G["ln2_b"] = stats2[0], stats2[1]
    G["w_down"] = _mm([S["act"]], [dz2_16], name="dw_down", out_dtype=BF16, trans_a=True, tm=D_FF // 2, tn=D_MODEL, tk=512)
    dact = _mm([dz2_16], [WT["w_down"][l]], name="mm_dact", out_dtype=F32, tm=512, tn=D_FF // 2, tk=D_MODEL)
    dg, du, dconv = _conv_glu_bwd(dact, S["gate"], S["up"], S["cw"], S["cb"])
    G["conv_w"], G["conv_b"] = dconv[0:3], dconv[3]
    G["w_gate"] = _mm([S["x1_16"]], [dg], name="dw_gate", out_dtype=BF16, trans_a=True, tm=D_MODEL, tn=D_FF // 2, tk=512)
    G["w_up"] = _mm([S["x1_16"]], [du], name="dw_up", out_dtype=BF16, trans_a=True, tm=D_MODEL, tn=D_FF // 2, tk=512)
    dx1 = _mm([dg, du], [WT["w_gate"][l], WT["w_up"][l]], name="mm_dx1", out_dtype=F32, tm=512, tn=D_MODEL,
              tk=D_FF // 2, add=dz2, add_scale=ALPHA)
    dz1, dz1_16, stats1 = _ln_bwd(S["x1hat"], S["rstd1"], S["g1"], S["b1"], name="ln1_bwd", dx=dx1)
    G["ln1_g"], G["ln1_b"] = stats1[0], stats1[1]
    G["w_out"] = _mm([S["ycat"]], [dz1_16], name="dw_out", out_dtype=BF16, trans_a=True, tm=D_MODEL, tn=D_MODEL, tk=512)
    dycat = _mm([dz1_16], [WT["w_out"][l]], name="mm_dycat", out_dtype=F32, tm=512, tn=D_MODEL, tk=D_MODEL)
    doa_t, dob_t, dgn = _outnorm_bwd(dycat, S["oa_t"], S["ob_t"], S["ga"], S["gb"])
    G["out_norm_a"], G["out_norm_b"] = dgn[0], dgn[1]
    doa = _heads_first(doa_t, KV * GQ).reshape(KV, GQ * T, HEAD_DIM)
    dob = _heads_first(dob_t, KV * GQ).reshape(KV, GQ, T, HEAD_DIM)
    dqa, delta = _attn_a_dq(S["qa"], S["kaT"], S["ka"], S["vaT"], doa, S["oa"], S["lse_a"])
    doa16 = doa.astype(BF16)
    dka, dva = _attn_a_dkv(S["ka"], S["va"], S["qa"].transpose(0, 2, 1), S["qa"], doa16.transpose(0, 2, 1), doa16,
                           S["lse_a"].reshape(KV, 1, GQ * T), delta.reshape(KV, 1, GQ * T))
    d_roped = jnp.concatenate([_heads_last(dqa.reshape(KV * GQ, T, HEAD_DIM)), _heads_last(dka)], axis=1)
    dh_rope, dgain = _qk_rope_bwd(S["h"], d_roped, S["gains"], cos2, sin2, name="qk_rope_bwd")
    G["q_norm"], G["k_norm"] = dgain[0, :HEAD_DIM], dgain[1, :HEAD_DIM]
    dqb, dkb, dvb, dbias, dsk = _attn_b_bwd(S["qb"], S["kbT"], S["kb"], S["vbT"], dob, S["ob"], S["lse_b"], bias,
                                            S["sink_rows"])
    dh = jnp.concatenate([
        dh_rope, _heads_last(dva).astype(BF16),
        (_heads_last(dqb.reshape(KV * GQ, T, HEAD_DIM)) * SCALE).astype(BF16),
        _heads_last(dkb[:, WB:WB + T]).astype(BF16), _heads_last(dvb[:, WB:WB + T]).astype(BF16)], axis=1)
    G["w_in"] = _mm([S["x16"]], [dh], name="dw_in", out_dtype=BF16, trans_a=True, tm=D_MODEL, tn=IN_COLS, tk=512)
    dxin = _mm([dh], [WT["w_in"][l]], name="mm_dxin", out_dtype=F32, tm=512, tn=D_MODEL, tk=IN_COLS,
               add=dz1, add_scale=ALPHA)
    return dxin, G, dbias.reshape(KV * GQ, WB * WK), dsk.reshape(KV * GQ, WB)


BIG = ("w_in", "w_out", "w_gate", "w_up", "w_down")
COL_SHARDED = ("w_in", "w_gate", "w_up")
WIRE_COLS = 1024


def _unshard(name, gathered, shard_shape):
    L, r, c = shard_shape
    blocks = gathered.reshape(N_DEV, L, r, c)
    if name in COL_SHARDED:
        return blocks.transpose(1, 2, 0, 3).reshape(L, r, N_DEV * c)
    return blocks.transpose(1, 0, 2, 3).reshape(L, N_DEV * r, c)


def _to_owner_blocks(name, full, shard_shape):
    L, r, c = shard_shape
    if name in COL_SHARDED:
        blocks = full.reshape(L, r, N_DEV, c).transpose(2, 0, 1, 3)
    else:
        blocks = full.reshape(L, N_DEV, r, c).transpose(1, 0, 2, 3)
    return blocks.reshape(N_DEV, -1, WIRE_COLS)


def _pack_small(vals):
    flat = jnp.concatenate([vals[n].reshape(-1).astype(F32) for n in SMALL_NAMES])
    pad = (-flat.shape[0]) % (8 * LANES)
    return jnp.pad(flat, (0, pad)).reshape(-1, LANES)


def _unpack_small(packed, shapes):
    flat = packed.reshape(-1)
    out, off = {}, 0
    for n in SMALL_NAMES:
        size = math.prod(shapes[n])
        out[n] = flat[off:off + size].reshape(shapes[n])
        off += size
    return out


def kernel(x, rel_bias, w_in, q_norm, k_norm, sink, out_norm_a, out_norm_b, w_out, ln1_g, ln1_b, w_gate, w_up, conv_w, conv_b, w_down, ln2_g, ln2_b, loss_target, m_rel_bias, m_w_in, m_q_norm, m_k_norm, m_sink, m_out_norm_a, m_out_norm_b, m_w_out, m_ln1_g, m_ln1_b, m_w_gate, m_w_up, m_conv_w, m_conv_b, m_w_down, m_ln2_g, m_ln2_b, v_rel_bias, v_w_in, v_q_norm, v_k_norm, v_sink, v_out_norm_a, v_out_norm_b, v_w_out, v_ln1_g, v_ln1_b, v_w_gate, v_w_up, v_conv_w, v_conv_b, v_w_down, v_ln2_g, v_ln2_b):
    P = dict(rel_bias=rel_bias, w_in=w_in, q_norm=q_norm, k_norm=k_norm, sink=sink, out_norm_a=out_norm_a,
             out_norm_b=out_norm_b, w_out=w_out, ln1_g=ln1_g, ln1_b=ln1_b, w_gate=w_gate, w_up=w_up, conv_w=conv_w,
             conv_b=conv_b, w_down=w_down, ln2_g=ln2_g, ln2_b=ln2_b)
    M = dict(rel_bias=m_rel_bias, w_in=m_w_in, q_norm=m_q_norm, k_norm=m_k_norm, sink=m_sink, out_norm_a=m_out_norm_a,
             out_norm_b=m_out_norm_b, w_out=m_w_out, ln1_g=m_ln1_g, ln1_b=m_ln1_b, w_gate=m_w_gate, w_up=m_w_up,
             conv_w=m_conv_w, conv_b=m_conv_b, w_down=m_w_down, ln2_g=m_ln2_g, ln2_b=m_ln2_b)
    V = dict(rel_bias=v_rel_bias, w_in=v_w_in, q_norm=v_q_norm, k_norm=v_k_norm, sink=v_sink, out_norm_a=v_out_norm_a,
             out_norm_b=v_out_norm_b, w_out=v_w_out, ln1_g=v_ln1_g, ln1_b=v_ln1_b, w_gate=v_w_gate, w_up=v_w_up,
             conv_w=v_conv_w, conv_b=v_conv_b, w_down=v_w_down, ln2_g=v_ln2_g, ln2_b=v_ln2_b)
    names = list(P)
    T = x.shape[1]
    me = 4 * lax.axis_index("x") + 2 * lax.axis_index("y") + lax.axis_index("c")

    cw_shard = conv_w.reshape(-1)
    cw_pad = (-cw_shard.shape[0]) % LANES
    wire = [P[n].astype(BF16).reshape(-1, WIRE_COLS) for n in BIG] + [jnp.pad(cw_shard, (0, cw_pad)).reshape(-1, LANES)]
    gathered = _exchange(wire, gather=True, name="gather_weights")
    W = {n: _unshard(n, gathered[i], P[n].shape) for i, n in enumerate(BIG)}
    L, taps, fc = conv_w.shape
    cw_all = gathered[-1].reshape(N_DEV, -1)[:, :cw_shard.shape[0]].reshape(N_DEV, L, taps, fc)
    W["conv_w"] = cw_all.transpose(1, 2, 0, 3).reshape(L, taps, N_DEV * fc)
    for n in names:
        if n not in W:
            W[n] = P[n]
    WT = {n: W[n].transpose(0, 2, 1) for n in BIG}

    cos2, sin2 = _rope_tables(T)
    bucket = _window_buckets()
    bias = _bias_table(rel_bias.T, bucket.reshape(1, WB * WK)).reshape(KV, GQ * WB, WK)
    tabs = (cos2, sin2, bias, bucket)

    ones, zeros = jnp.ones((1, D_MODEL), F32), jnp.zeros((1, D_MODEL), F32)
    cur = (x[0], ones, zeros, x[0].astype(BF16))
    saved = []
    for l in range(DEPTH):
        cur, S = _layer_fwd(l, cur, W, tabs)
        saved.append(S)

    grads = [None] * DEPTH
    dbs, dsks = [None] * DEPTH, [None] * DEPTH
    S = saved[-1]
    dz, dz16, stats = _ln_bwd(S["x2hat"], S["rstd2"], S["g2"], S["b2"], name="loss_ln2_bwd", target=loss_target[0])
    loss = lax.psum(stats[2, 0], ("x", "y", "c"))
    for l in reversed(range(DEPTH)):
        S = saved[l]
        dxin, grads[l], dbs[l], dsks[l] = _layer_bwd(l, S, W, WT, tabs, dz, dz16, stats)
        if l > 0:
            Sp = saved[l - 1]
            dz, dz16, stats = _ln_bwd(Sp["x2hat"], Sp["rstd2"], Sp["g2"], Sp["b2"], name="ln2_bwd", dx=dxin)
    grad_x = dxin[None]

    drb, dsink = _bias_sink_grads(dbs, dsks, bucket.reshape(1, WB * WK))
    small_g = {n: jnp.stack([grads[l][n] for l in range(DEPTH)]) for n in SMALL_NAMES if n not in ("rel_bias", "sink")}
    small_g["rel_bias"] = drb.T
    small_g["sink"] = dsink.reshape(DEPTH, KV * GQ)

    big_parts = [_to_owner_blocks(n, jnp.stack([grads[l][n] for l in range(DEPTH)]), P[n].shape) for n in BIG]
    recv = _exchange(big_parts, gather=False, name="scatter_grads")
    small_recv = _exchange([_pack_small(small_g)], gather=True, name="gather_small_grads")[0]

    out_g, out_d, out_m, out_v = {}, {}, {}, {}
    for i, n in enumerate(BIG):
        shp = P[n].shape
        res = _adamw(P[n].reshape(-1, WIRE_COLS), M[n].reshape(-1, WIRE_COLS), V[n].reshape(-1, WIRE_COLS), recv[i],
                     name="adamw_" + n, tr=64)
        out_g[n], out_d[n], out_m[n], out_v[n] = (r.reshape(shp) for r in res)
    full_shapes = {n: W[n].shape for n in SMALL_NAMES}

    def small_state(D):
        vals = {n: D[n] for n in SMALL_NAMES if n != "conv_w"}
        cw = jnp.zeros((L, taps, N_DEV, fc), F32)
        cw = lax.dynamic_update_slice(cw, D["conv_w"].reshape(L, taps, 1, fc), (0, 0, me, 0))
        vals["conv_w"] = cw.reshape(L, taps, N_DEV * fc)
        return _pack_small(vals)

    sw, sm, sv = small_state(P), small_state(M), small_state(V)
    res = _adamw(sw, sm, sv, small_recv, name="adamw_small", tr=sw.shape[0])
    for dst, packed in zip((out_g, out_d, out_m, out_v), res):
        vals = _unpack_small(packed, full_shapes)
        for n in SMALL_NAMES:
            if n == "conv_w":
                sl = lax.dynamic_slice(vals[n].reshape(L, taps, N_DEV, fc), (0, 0, me, 0), (L, taps, 1, fc))
                dst[n] = sl.reshape(L, taps, fc)
            else:
                dst[n] = vals[n]
    return (loss, grad_x, *[out_g[n] for n in names], *[out_d[n] for n in names],
            *[out_m[n] for n in names], *[out_v[n] for n in names])
```

```python
import functools
import math

import jax
import jax.numpy as jnp
from jax import lax
from jax.experimental import pallas as pl
from jax.experimental.pallas import tpu as pltpu

F32 = jnp.float32
BF16 = jnp.bfloat16
MESH = pl.DeviceIdType.MESH

N_DEV = 8
D_MODEL = 1024
DEPTH = 2
HEAD_DIM = 64
KV = 2
GQ = 4
QW = KV * GQ * HEAD_DIM
KW = KV * HEAD_DIM
ROPE_W = QW + KW
IN_COLS = 2 * (QW + 2 * KW)
D_FF = 2816
GRID_W = 64
ROPE_THETA = 10000.0
WINDOW = 128
N_BUCKETS = 32
MAX_DISTANCE = 128
ALPHA = (2.0 * DEPTH) ** 0.25
RMS_EPS = 1e-6
LN_EPS = 1e-5
SCALE = HEAD_DIM ** -0.5
NEG = -1e30

ADAM_LR = 0.001
ADAM_B1 = 0.9
ADAM_B2 = 0.999
ADAM_EPS = 1e-08
ADAM_WD = 0.01
ADAM_STEP = 10

LANES = 128
VMEM_CAP = 60 * 1024 * 1024
SMALL_NAMES = ("rel_bias", "q_norm", "k_norm", "sink", "out_norm_a", "out_norm_b", "ln1_g", "ln1_b",
               "conv_b", "ln2_g", "ln2_b", "conv_w")


def _params(sem, est_bytes):
    limit = int(min(VMEM_CAP, est_bytes + (8 << 20)))
    return pltpu.CompilerParams(dimension_semantics=sem, vmem_limit_bytes=limit)


def _nbytes(shape, dtype):
    return math.prod(shape) * jnp.dtype(dtype).itemsize


def _exchange(parts, gather, name):
    n = len(parts)
    blk = [p.shape if gather else p.shape[1:] for p in parts]

    def body(*refs):
        ins, outs = refs[:n], refs[n:2 * n]
        send_sems, recv_sems, local_sems = refs[2 * n:]
        me = 4 * lax.axis_index("x") + 2 * lax.axis_index("y") + lax.axis_index("c")

        def src(k, j):
            return ins[k] if gather else ins[k].at[j]

        def remote(k, d):
            peer = lax.rem(me + d, N_DEV)
            return pltpu.make_async_remote_copy(
                src_ref=src(k, peer), dst_ref=outs[k].at[me],
                send_sem=send_sems.at[k * (N_DEV - 1) + d - 1], recv_sem=recv_sems.at[k * (N_DEV - 1) + d - 1],
                device_id=(peer // 4, lax.rem(peer // 2, 2), lax.rem(peer, 2)), device_id_type=MESH)

        def arrival(k, d):
            frm = lax.rem(me + N_DEV - d, N_DEV)
            return pltpu.make_async_remote_copy(
                src_ref=src(k, frm), dst_ref=outs[k].at[frm],
                send_sem=send_sems.at[k * (N_DEV - 1) + d - 1], recv_sem=recv_sems.at[k * (N_DEV - 1) + d - 1],
                device_id=(frm // 4, lax.rem(frm // 2, 2), lax.rem(frm, 2)), device_id_type=MESH)

        local = [pltpu.make_async_copy(src(k, me), outs[k].at[me], local_sems.at[k]) for k in range(n)]
        sends = [remote(k, d) for k in range(n) for d in range(1, N_DEV)]
        for cp in local + sends:
            cp.start()
        for k in range(n):
            for d in range(1, N_DEV):
                arrival(k, d).wait_recv()
        for cp in sends:
            cp.wait_send()
        for cp in local:
            cp.wait()

    hbm = pl.BlockSpec(memory_space=pltpu.HBM)
    return pl.pallas_call(
        body, name=name,
        out_shape=[jax.ShapeDtypeStruct((N_DEV,) + tuple(b), p.dtype) for b, p in zip(blk, parts)],
        in_specs=[hbm] * n, out_specs=[hbm] * n,
        scratch_shapes=[pltpu.SemaphoreType.DMA((n * (N_DEV - 1),)), pltpu.SemaphoreType.DMA((n * (N_DEV - 1),)),
                        pltpu.SemaphoreType.DMA((n,))],
    )(*parts)


def _mm(a_list, b_list, *, name, out_dtype, tm, tn, tk, trans_a=False, add=None, add_scale=1.0):
    na = len(a_list)
    if trans_a:
        K, M = a_list[0].shape
    else:
        M, K = a_list[0].shape
    N = b_list[0].shape[1]
    tm, tn, tk = min(tm, M), min(tn, N), min(tk, K)
    assert M % tm == 0 and N % tn == 0 and K % tk == 0, (name, M, N, K, tm, tn, tk)
    nk = K // tk
    dims = (((0,), (0,)), ((), ())) if trans_a else (((1,), (0,)), ((), ()))

    def body(*refs):
        a_refs, b_refs = refs[:na], refs[na:2 * na]
        add_ref = refs[2 * na] if add is not None else None
        o_ref, acc_ref = refs[-2], refs[-1]
        k = pl.program_id(2)

        @pl.when(k == 0)
        def _():
            acc_ref[...] = jnp.zeros_like(acc_ref)

        part = None
        for a_ref, b_ref in zip(a_refs, b_refs):
            prod = lax.dot_general(a_ref[...].astype(BF16), b_ref[...].astype(BF16), dims,
                                   preferred_element_type=F32)
            part = prod if part is None else part + prod
        acc_ref[...] += part

        @pl.when(k == nk - 1)
        def _():
            res = acc_ref[...]
            if add_ref is not None:
                res = res + add_scale * add_ref[...]
            o_ref[...] = res.astype(o_ref.dtype)

    if trans_a:
        a_spec = pl.BlockSpec((tk, tm), lambda i, j, k: (k, i))
    else:
        a_spec = pl.BlockSpec((tm, tk), lambda i, j, k: (i, k))
    b_spec = pl.BlockSpec((tk, tn), lambda i, j, k: (k, j))
    o_spec = pl.BlockSpec((tm, tn), lambda i, j, k: (i, j))
    in_specs = [a_spec] * na + [b_spec] * na + ([o_spec] if add is not None else [])
    est = (2 * na * (_nbytes((tm, tk), a_list[0].dtype) + _nbytes((tk, tn), b_list[0].dtype))
           + na * (_nbytes((tm, tk), BF16) + _nbytes((tk, tn), BF16))
           + 2 * _nbytes((tm, tn), out_dtype) + 3 * _nbytes((tm, tn), F32)
           + (2 * _nbytes((tm, tn), F32) if add is not None else 0))
    args = list(a_list) + list(b_list) + ([add] if add is not None else [])
    return pl.pallas_call(
        body, name=name, grid=(M // tm, N // tn, nk),
        out_shape=jax.ShapeDtypeStruct((M, N), out_dtype),
        in_specs=in_specs, out_specs=o_spec,
        scratch_shapes=[pltpu.VMEM((tm, tn), F32)],
        compiler_params=_params(("parallel", "parallel", "arbitrary"), est),
    )(*args)


def _mm_res_ln(a, w, res_hat, res_g, res_b, ln_g, ln_b, *, name, tm):
    T, K = a.shape
    D = w.shape[1]
    tm = min(tm, T)

    def body(a_ref, w_ref, rh_ref, rg_ref, rb_ref, g_ref, b_ref, xhat_ref, rstd_ref, xb_ref):
        branch = jnp.dot(a_ref[...].astype(BF16), w_ref[...], preferred_element_type=F32)
        z = ALPHA * (rh_ref[...] * rg_ref[...] + rb_ref[...]) + branch
        mu = jnp.mean(z, axis=1, keepdims=True)
        zc = z - mu
        var = jnp.mean(zc * zc, axis=1, keepdims=True)
        rstd = lax.rsqrt(var + LN_EPS)
        xhat = zc * rstd
        xhat_ref[...] = xhat
        rstd_ref[...] = rstd
        xb_ref[...] = (xhat * g_ref[...] + b_ref[...]).astype(BF16)

    row = pl.BlockSpec((tm, D), lambda i: (i, 0))
    vec = pl.BlockSpec((1, D), lambda i: (0, 0))
    est = (2 * (_nbytes((tm, K), a.dtype) + _nbytes((K, D), BF16)) + 4 * _nbytes((tm, D), F32) * 2
           + 6 * _nbytes((tm, D), F32))
    return pl.pallas_call(
        body, name=name, grid=(T // tm,),
        out_shape=(jax.ShapeDtypeStruct((T, D), F32), jax.ShapeDtypeStruct((T, 1), F32),
                   jax.ShapeDtypeStruct((T, D), BF16)),
        in_specs=[pl.BlockSpec((tm, K), lambda i: (i, 0)), pl.BlockSpec((K, D), lambda i: (0, 0)), row, vec, vec, vec, vec],
        out_specs=(row, pl.BlockSpec((tm, 1), lambda i: (i, 0)), row),
        compiler_params=_params(("parallel",), est),
    )(a, w, res_hat, res_g, res_b, ln_g, ln_b)


def _ln_bwd(xhat, rstd, ln_g, ln_b, *, name, dx=None, target=None, tm=256):
    T, D = xhat.shape
    tm = min(tm, T)
    head = target is not None

    def body(xhat_ref, rstd_ref, g_ref, b_ref, d_ref, dz_ref, dzb_ref, st_ref):
        i = pl.program_id(0)

        @pl.when(i == 0)
        def _():
            st_ref[...] = jnp.zeros_like(st_ref)

        xh = xhat_ref[...]
        g = g_ref[...]
        if head:
            err = (xh * g + b_ref[...]) - d_ref[...]
            dxv = err * (1.0 / D)
            st_ref[2:3, :] += 0.5 * jnp.sum(jnp.sum(err * err, axis=1, keepdims=True) * (1.0 / D), axis=0, keepdims=True)
        else:
            dxv = d_ref[...]
        st_ref[0:1, :] += jnp.sum(dxv * xh, axis=0, keepdims=True)
        st_ref[1:2, :] += jnp.sum(dxv, axis=0, keepdims=True)
        dxh = dxv * g
        m1 = jnp.mean(dxh, axis=1, keepdims=True)
        m2 = jnp.mean(dxh * xh, axis=1, keepdims=True)
        dz = rstd_ref[...] * (dxh - m1 - xh * m2)
        dz_ref[...] = dz
        dzb_ref[...] = dz.astype(BF16)

    row = pl.BlockSpec((tm, D), lambda i: (i, 0))
    vec = pl.BlockSpec((1, D), lambda i: (0, 0))
    est = 2 * 4 * _nbytes((tm, D), F32) + 6 * _nbytes((tm, D), F32)
    return pl.pallas_call(
        body, name=name, grid=(T // tm,),
        out_shape=(jax.ShapeDtypeStruct((T, D), F32), jax.ShapeDtypeStruct((T, D), BF16),
                   jax.ShapeDtypeStruct((8, D), F32)),
        in_specs=[row, pl.BlockSpec((tm, 1), lambda i: (i, 0)), vec, vec, row],
        out_specs=(row, row, pl.BlockSpec((8, D), lambda i: (0, 0))),
        compiler_params=_params(("arbitrary",), est),
    )(xhat, rstd, ln_g, ln_b, target if head else dx)


def _pair_swap(v, even):
    return jnp.where(even, pltpu.roll(v, LANES - 1, 1), pltpu.roll(v, 1, 1))


def _half_sums(v, lo):
    s_lo = jnp.sum(jnp.where(lo, v, 0.0), axis=1, keepdims=True)
    s_hi = jnp.sum(jnp.where(lo, 0.0, v), axis=1, keepdims=True)
    return jnp.where(lo, s_lo, s_hi)


def _qk_rope_fwd(h, gains, cos2, sin2, *, name, tm=256):
    T = h.shape[0]
    tm = min(tm, T)
    nch = ROPE_W // LANES

    def body(h_ref, g_ref, c_ref, s_ref, o_ref):
        lane = lax.broadcasted_iota(jnp.int32, (tm, LANES), 1)
        lo, even = lane < HEAD_DIM, lane % 2 == 0
        c, s = c_ref[...], s_ref[...]
        for j in range(nch):
            x = h_ref[:, j * LANES:(j + 1) * LANES]
            isq = j < QW // LANES
            g = g_ref[0:1, :] if isq else g_ref[1:2, :]
            r = lax.rsqrt(_half_sums(x * x, lo) * (1.0 / HEAD_DIM) + RMS_EPS)
            nrm = x * r * g
            out = nrm * c + _pair_swap(nrm, even) * s
            if isq:
                out = out * SCALE
            o_ref[:, j * LANES:(j + 1) * LANES] = out.astype(BF16)

    est = 2 * (_nbytes((tm, ROPE_W), F32) + _nbytes((tm, ROPE_W), BF16) + 2 * _nbytes((tm, LANES), F32)) + (4 << 20)
    return pl.pallas_call(
        body, name=name, grid=(T // tm,),
        out_shape=jax.ShapeDtypeStruct((T, ROPE_W), BF16),
        in_specs=[pl.BlockSpec((tm, ROPE_W), lambda i: (i, 0)), pl.BlockSpec((8, LANES), lambda i: (0, 0)),
                  pl.BlockSpec((tm, LANES), lambda i: (i, 0)), pl.BlockSpec((tm, LANES), lambda i: (i, 0))],
        out_specs=pl.BlockSpec((tm, ROPE_W), lambda i: (i, 0)),
        compiler_params=_params(("parallel",), est),
    )(h, gains, cos2, sin2)


def _qk_rope_bwd(h, d_out, gains, cos2, sin2, *, name, tm=256):
    T = h.shape[0]
    tm = min(tm, T)
    nch = ROPE_W // LANES

    def body(h_ref, d_ref, g_ref, c_ref, s_ref, dh_ref, dg_ref):
        i = pl.program_id(0)

        @pl.when(i == 0)
        def _():
            dg_ref[...] = jnp.zeros_like(dg_ref)

        lane = lax.broadcasted_iota(jnp.int32, (tm, LANES), 1)
        lo, even = lane < HEAD_DIM, lane % 2 == 0
        c, s = c_ref[...], s_ref[...]
        acc = [None, None]
        for j in range(nch):
            x = h_ref[:, j * LANES:(j + 1) * LANES]
            isq = j < QW // LANES
            g = g_ref[0:1, :] if isq else g_ref[1:2, :]
            d = d_ref[:, j * LANES:(j + 1) * LANES]
            if isq:
                d = d * SCALE
            r = lax.rsqrt(_half_sums(x * x, lo) * (1.0 / HEAD_DIM) + RMS_EPS)
            dn = d * c + _pair_swap(d * s, even)
            xr = x * r
            part = jnp.sum(dn * xr, axis=0, keepdims=True)
            acc[0 if isq else 1] = part if acc[0 if isq else 1] is None else acc[0 if isq else 1] + part
            dng = dn * g
            dx = r * dng - xr * (r * r) * (_half_sums(dng * x, lo) * (1.0 / HEAD_DIM))
            dh_ref[:, j * LANES:(j + 1) * LANES] = dx.astype(BF16)
        for row in range(2):
            folded = acc[row] + pltpu.roll(acc[row], HEAD_DIM, 1)
            dg_ref[row:row + 1, :] += folded

    est = 2 * (2 * _nbytes((tm, ROPE_W), F32) + _nbytes((tm, ROPE_W), BF16) + 2 * _nbytes((tm, LANES), F32)) + (4 << 20)
    return pl.pallas_call(
        body, name=name, grid=(T // tm,),
        out_shape=(jax.ShapeDtypeStruct((T, ROPE_W), BF16), jax.ShapeDtypeStruct((8, LANES), F32)),
        in_specs=[pl.BlockSpec((tm, ROPE_W), lambda i: (i, 0)), pl.BlockSpec((tm, ROPE_W), lambda i: (i, 0)),
                  pl.BlockSpec((8, LANES), lambda i: (0, 0)),
                  pl.BlockSpec((tm, LANES), lambda i: (i, 0)), pl.BlockSpec((tm, LANES), lambda i: (i, 0))],
        out_specs=(pl.BlockSpec((tm, ROPE_W), lambda i: (i, 0)), pl.BlockSpec((8, LANES), lambda i: (0, 0))),
        compiler_params=_params(("arbitrary",), est),
    )(h, d_out, gains, cos2, sin2)


def _attn_a_fwd(k, qT, vT, *, tq=2048, tk=512, cq=512):
    G, T, HD = k.shape
    NQ = qT.shape[2]
    tq, tk = min(tq, NQ), min(tk, T)
    cq = min(cq, tq)
    nk = T // tk

    def body(k_ref, qT_ref, vT_ref, oT_ref, lse_ref, m_sc, l_sc, acc_sc):
        kv = pl.program_id(2)

        @pl.when(kv == 0)
        def _():
            m_sc[...] = jnp.full_like(m_sc, NEG)
            l_sc[...] = jnp.zeros_like(l_sc)
            acc_sc[...] = jnp.zeros_like(acc_sc)

        def scores(c):
            return jnp.dot(k_ref[...], qT_ref[:, c * cq:(c + 1) * cq], preferred_element_type=F32)

        nc = tq // cq
        ahead = scores(0)
        for c in range(nc):
            cols = slice(c * cq, (c + 1) * cq)
            sT = ahead
            if c + 1 < nc:
                ahead = scores(c + 1)
            m_prev = m_sc[:, cols]
            m_new = jnp.maximum(m_prev, jnp.max(sT, axis=0, keepdims=True))
            a = jnp.exp(m_prev - m_new)
            pT = jnp.exp(sT - m_new)
            l_sc[:, cols] = a * l_sc[:, cols] + jnp.sum(pT, axis=0, keepdims=True)
            acc_sc[:, cols] = a * acc_sc[:, cols] + jnp.dot(vT_ref[...], pT.astype(BF16), preferred_element_type=F32)
            m_sc[:, cols] = m_new

        @pl.when(kv == nk - 1)
        def _():
            l = l_sc[...]
            oT_ref[...] = acc_sc[...] / l
            lse_ref[...] = m_sc[...] + jnp.log(l)

    qtr = pl.BlockSpec((None, HD, tq), lambda g, i, j: (g, 0, i))
    qvec = pl.BlockSpec((None, 1, tq), lambda g, i, j: (g, 0, i))
    est = 6 * _nbytes((tq, tk), F32) + (8 << 20)
    return pl.pallas_call(
        body, name="attn_a_fwd", grid=(G, NQ // tq, nk),
        out_shape=(jax.ShapeDtypeStruct((G, HD, NQ), F32), jax.ShapeDtypeStruct((G, 1, NQ), F32)),
        in_specs=[pl.BlockSpec((None, tk, HD), lambda g, i, j: (g, j, 0)), qtr,
                  pl.BlockSpec((None, HD, tk), lambda g, i, j: (g, 0, j))],
        out_specs=(qtr, qvec),
        scratch_shapes=[pltpu.VMEM((1, tq), F32), pltpu.VMEM((1, tq), F32), pltpu.VMEM((HD, tq), F32)],
        compiler_params=_params(("parallel", "parallel", "arbitrary"), est),
    )(k, qT, vT)


def _attn_a_delta(oT, doT, *, tq=2048):
    G, HD, NQ = oT.shape
    tq = min(tq, NQ)

    def body(o_ref, d_ref, dl_ref):
        dl_ref[...] = jnp.sum(o_ref[...] * d_ref[...], axis=0, keepdims=True)

    qtr = pl.BlockSpec((None, HD, tq), lambda g, i: (g, 0, i))
    return pl.pallas_call(
        body, name="attn_a_delta", grid=(G, NQ // tq),
        out_shape=jax.ShapeDtypeStruct((G, 1, NQ), F32),
        in_specs=[qtr, qtr], out_specs=pl.BlockSpec((None, 1, tq), lambda g, i: (g, 0, i)),
        compiler_params=_params(("parallel", "parallel"), 8 << 20),
    )(oT, doT)


def _attn_a_bwd(k, v, kT, qT, doT, lse_row, delta_row, *, tq=2048, tk=512, cq=512):
    G, T, HD = k.shape
    NQ = qT.shape[2]
    tq, tk = min(tq, NQ), min(tk, T)
    cq = min(cq, tq)
    nq, nc = NQ // tq, tq // cq
    nt = (((1,), (1,)), ((), ()))

    def body(k_ref, v_ref, kT_ref, qT_ref, doT_ref, lse_ref, dl_ref, dkT_ref, dvT_ref, dqT_ref, dk_sc, dv_sc):
        j, i = pl.program_id(1), pl.program_id(2)

        @pl.when((j == 0) & (i == 0))
        def _():
            dqT_ref[...] = jnp.zeros_like(dqT_ref)

        @pl.when(i == 0)
        def _():
            dk_sc[...] = jnp.zeros_like(dk_sc)
            dv_sc[...] = jnp.zeros_like(dv_sc)

        def scores(c):
            cols = slice(c * cq, (c + 1) * cq)
            return (jnp.dot(k_ref[...], qT_ref[:, cols], preferred_element_type=F32),
                    jnp.dot(v_ref[...], doT_ref[:, cols], preferred_element_type=F32))

        ahead = scores(0)
        dk_part = dv_part = None
        for c in range(nc):
            cols = slice(c * cq, (c + 1) * cq)
            sT, dpT = ahead
            if c + 1 < nc:
                ahead = scores(c + 1)
            pT = jnp.exp(sT - lse_ref[:, cols])
            dsT = (pT * (dpT - dl_ref[:, cols])).astype(BF16)
            dv_c = lax.dot_general(doT_ref[:, cols], pT.astype(BF16), nt, preferred_element_type=F32)
            dk_c = lax.dot_general(qT_ref[:, cols], dsT, nt, preferred_element_type=F32)
            dv_part = dv_c if dv_part is None else dv_part + dv_c
            dk_part = dk_c if dk_part is None else dk_part + dk_c
            out_cols = pl.ds(pl.multiple_of(i * tq + c * cq, cq), cq)
            dqT_ref[:, out_cols] += jnp.dot(kT_ref[...], dsT, preferred_element_type=F32)
        dk_sc[...] += dk_part
        dv_sc[...] += dv_part

        @pl.when(i == nq - 1)
        def _():
            dkT_ref[...] = dk_sc[...]
            dvT_ref[...] = dv_sc[...]

    krow = pl.BlockSpec((None, tk, HD), lambda g, j, i: (g, j, 0))
    ktr = pl.BlockSpec((None, HD, tk), lambda g, j, i: (g, 0, j))
    qtr = pl.BlockSpec((None, HD, tq), lambda g, j, i: (g, 0, i))
    qvec = pl.BlockSpec((None, 1, tq), lambda g, j, i: (g, 0, i))
    whole = pl.BlockSpec((None, HD, NQ), lambda g, j, i: (g, 0, 0))
    est = 8 * _nbytes((cq, tk), F32) + 2 * _nbytes((HD, NQ), F32) + (8 << 20)
    return pl.pallas_call(
        body, name="attn_a_bwd", grid=(G, T // tk, nq),
        out_shape=(jax.ShapeDtypeStruct((G, HD, T), F32), jax.ShapeDtypeStruct((G, HD, T), F32),
                   jax.ShapeDtypeStruct((G, HD, NQ), F32)),
        in_specs=[krow, krow, ktr, qtr, qtr, qvec, qvec],
        out_specs=(ktr, ktr, whole),
        scratch_shapes=[pltpu.VMEM((HD, tk), F32), pltpu.VMEM((HD, tk), F32)],
        compiler_params=_params(("arbitrary", "arbitrary", "arbitrary"), est),
    )(k, v, kT, qT, doT, lse_row, delta_row)


WB = WINDOW
WK = 3 * WINDOW


def _win_specs(T):
    nb = T // WB
    q4 = pl.BlockSpec((None, GQ, WB, HEAD_DIM), lambda g, n: (g, 0, n, 0))
    col = pl.BlockSpec((None, GQ * WB, 1), lambda g, n: (g, 0, 0))
    bias = pl.BlockSpec((None, GQ * WB, WK), lambda g, n: (g, 0, 0))
    kt = [pl.BlockSpec((None, HEAD_DIM, WB), functools.partial(lambda g, n, o: (g, 0, n + o), o=o)) for o in range(3)]
    kr = [pl.BlockSpec((None, WB, HEAD_DIM), functools.partial(lambda g, n, o: (g, n + o, 0), o=o)) for o in range(3)]
    return nb, q4, col, bias, kt, kr


def _win_mask(n, T):
    qq = lax.rem(lax.broadcasted_iota(jnp.int32, (GQ * WB, WK), 0), WB)
    kk = lax.broadcasted_iota(jnp.int32, (GQ * WB, WK), 1)
    rel = kk - WB - qq
    kabs = n * WB - WB + kk
    return (jnp.abs(rel) <= WINDOW) & (kabs >= 0) & (kabs < T)


def _attn_b_fwd(q, kTp, vp, bias, sink_rows):
    T = q.shape[2]
    nb, q4, col, bias_spec, kt, kr = _win_specs(T)

    def body(q_ref, k0, k1, k2, v0, v1, v2, b_ref, sk_ref, o_ref, lse_ref):
        n = pl.program_id(1)
        qv = q_ref[...].reshape(GQ * WB, HEAD_DIM)
        kT = jnp.concatenate([k0[...], k1[...], k2[...]], axis=1)
        vv = jnp.concatenate([v0[...], v1[...], v2[...]], axis=0)
        s = jnp.dot(qv, kT, preferred_element_type=F32) + b_ref[...]
        s = jnp.where(_win_mask(n, T), s, NEG)
        sk = sk_ref[...]
        m = jnp.maximum(jnp.max(s, axis=1, keepdims=True), sk)
        p = jnp.exp(s - m)
        den = jnp.sum(p, axis=1, keepdims=True) + jnp.exp(sk - m)
        o = jnp.dot(p.astype(BF16), vv, preferred_element_type=F32) / den
        o_ref[...] = o.reshape(GQ, WB, HEAD_DIM)
        lse_ref[...] = (m + jnp.log(den)).reshape(GQ, WB, 1)

    return pl.pallas_call(
        body, name="attn_b_fwd", grid=(KV, nb),
        out_shape=(jax.ShapeDtypeStruct((KV, GQ, T, HEAD_DIM), F32), jax.ShapeDtypeStruct((KV, GQ, T, 1), F32)),
        in_specs=[q4] + kt + kr + [bias_spec, col],
        out_specs=(q4, pl.BlockSpec((None, GQ, WB, 1), lambda g, n: (g, 0, n, 0))),
        compiler_params=_params(("parallel", "parallel"), 24 << 20),
    )(q, kTp, kTp, kTp, vp, vp, vp, bias, sink_rows)


def _attn_b_bwd(q, kTp, kp, vTp, do, o, lse, bias, sink_rows):
    T = q.shape[2]
    nb, q4, col, bias_spec, kt, kr = _win_specs(T)
    Tp = T + 2 * WB

    def body(q_ref, k0, k1, k2, r0, r1, r2, w0, w1, w2, do_ref, o_ref, lse_ref, b_ref, sk_ref,
             dq_ref, dk_ref, dv_ref, db_ref, dsk_ref):
        n = pl.program_id(1)

        @pl.when(n == 0)
        def _():
            dk_ref[...] = jnp.zeros_like(dk_ref)
            dv_ref[...] = jnp.zeros_like(dv_ref)
            db_ref[...] = jnp.zeros_like(db_ref)
            dsk_ref[...] = jnp.zeros_like(dsk_ref)

        qv = q_ref[...].reshape(GQ * WB, HEAD_DIM)
        kT = jnp.concatenate([k0[...], k1[...], k2[...]], axis=1)
        kk = jnp.concatenate([r0[...], r1[...], r2[...]], axis=0)
        vT = jnp.concatenate([w0[...], w1[...], w2[...]], axis=1)
        dov = do_ref[...].reshape(GQ * WB, HEAD_DIM)
        lse = lse_ref[...].reshape(GQ * WB, 1)
        delta = jnp.sum(dov * o_ref[...].reshape(GQ * WB, HEAD_DIM), axis=1, keepdims=True)
        s = jnp.dot(qv, kT, preferred_element_type=F32) + b_ref[...]
        s = jnp.where(_win_mask(n, T), s, NEG)
        p = jnp.exp(s - lse)
        dob = dov.astype(BF16)
        dp = jnp.dot(dob, vT, preferred_element_type=F32)
        ds = p * (dp - delta)
        db_ref[...] += ds
        dsk_ref[...] -= jnp.exp(sk_ref[...] - lse) * delta
        dsb = ds.astype(BF16)
        dq_ref[...] = jnp.dot(dsb, kk, preferred_element_type=F32).reshape(GQ, WB, HEAD_DIM)
        tn = (((0,), (0,)), ((), ()))
        rows = pl.ds(pl.multiple_of(n * WB, WB), WK)
        dk_ref[rows, :] += lax.dot_general(dsb, qv, tn, preferred_element_type=F32)
        dv_ref[rows, :] += lax.dot_general(p.astype(BF16), dob, tn, preferred_element_type=F32)

    kacc = pl.BlockSpec((None, Tp, HEAD_DIM), lambda g, n: (g, 0, 0))
    return pl.pallas_call(
        body, name="attn_b_bwd", grid=(KV, nb),
        out_shape=(jax.ShapeDtypeStruct((KV, GQ, T, HEAD_DIM), F32),
                   jax.ShapeDtypeStruct((KV, Tp, HEAD_DIM), F32), jax.ShapeDtypeStruct((KV, Tp, HEAD_DIM), F32),
                   jax.ShapeDtypeStruct((KV, GQ * WB, WK), F32), jax.ShapeDtypeStruct((KV, GQ * WB, 1), F32)),
        in_specs=[q4] + kt + kr + kt + [q4, q4, pl.BlockSpec((None, GQ, WB, 1), lambda g, n: (g, 0, n, 0)), bias_spec, col],
        out_specs=(q4, kacc, kacc, bias_spec, col),
        compiler_params=_params(("parallel", "arbitrary"), 40 << 20),
    )(q, kTp, kTp, kTp, kp, kp, kp, vTp, vTp, vTp, do, o, lse, bias, sink_rows)


def _bias_table(rel_bias_t, bucket):
    nh, n = rel_bias_t.shape[0], bucket.shape[1]

    def body(rb_ref, bk_ref, o_ref):
        bk = bk_ref[...]
        out = jnp.zeros((nh, n), F32)
        for b in range(N_BUCKETS):
            out = jnp.where(bk == b, rb_ref[:, b:b + 1], out)
        o_ref[...] = out

    return pl.pallas_call(
        body, name="bias_table", out_shape=jax.ShapeDtypeStruct((nh, n), F32),
        compiler_params=pltpu.CompilerParams(vmem_limit_bytes=32 << 20),
    )(rel_bias_t, bucket)


def _bias_sink_grads(db_list, dsk_list, bucket):
    L = len(db_list)

    def body(*refs):
        db_refs, dsk_refs, bk_ref = refs[:L], refs[L:2 * L], refs[2 * L]
        drb_ref, dsink_ref = refs[2 * L + 1], refs[2 * L + 2]
        tot = db_refs[0][...]
        for r in db_refs[1:]:
            tot = tot + r[...]
        bk = bk_ref[...]
        lane = lax.broadcasted_iota(jnp.int32, (2 * GQ, N_BUCKETS), 1)
        out = jnp.zeros((2 * GQ, N_BUCKETS), F32)
        for b in range(N_BUCKETS):
            sb = jnp.sum(jnp.where(bk == b, tot, 0.0), axis=1, keepdims=True)
            out = jnp.where(lane == b, sb, out)
        drb_ref[...] = out
        for l in range(L):
            dsink_ref[l] = jnp.sum(dsk_refs[l][...], axis=1, keepdims=True)

    return pl.pallas_call(
        body, name="bias_sink_grads",
        out_shape=(jax.ShapeDtypeStruct((2 * GQ, N_BUCKETS), F32), jax.ShapeDtypeStruct((L, 2 * GQ, 1), F32)),
        compiler_params=pltpu.CompilerParams(vmem_limit_bytes=32 << 20),
    )(*db_list, *dsk_list, bucket)


def _outnorm_fwd(oa, ob, ga, gb, *, tm=512):
    T = oa.shape[0]
    tm = min(tm, T)

    def body(oa_ref, ob_ref, ga_ref, gb_ref, y_ref):
        for j, (o_ref, g_ref) in enumerate(((oa_ref, ga_ref), (ob_ref, gb_ref))):
            o = o_ref[...]
            r = lax.rsqrt(jnp.mean(o * o, axis=1, keepdims=True) + RMS_EPS)
            y_ref[:, j * QW:(j + 1) * QW] = (o * r * g_ref[...]).astype(BF16)

    half = pl.BlockSpec((tm, QW), lambda i: (i, 0))
    vec = pl.BlockSpec((1, QW), lambda i: (0, 0))
    return pl.pallas_call(
        body, name="outnorm_fwd", grid=(T // tm,),
        out_shape=jax.ShapeDtypeStruct((T, 2 * QW), BF16),
        in_specs=[half, half, vec, vec], out_specs=pl.BlockSpec((tm, 2 * QW), lambda i: (i, 0)),
        compiler_params=_params(("parallel",), 16 << 20),
    )(oa, ob, ga, gb)


def _outnorm_bwd(dy, oa, ob, ga, gb, *, tm=512):
    T = oa.shape[0]
    tm = min(tm, T)

    def body(dy_ref, oa_ref, ob_ref, ga_ref, gb_ref, doa_ref, dob_ref, dg_ref):
        i = pl.program_id(0)

        @pl.when(i == 0)
        def _():
            dg_ref[...] = jnp.zeros_like(dg_ref)

        for j, (o_ref, g_ref, d_ref) in enumerate(((oa_ref, ga_ref, doa_ref), (ob_ref, gb_ref, dob_ref))):
            o = o_ref[...]
            d = dy_ref[:, j * QW:(j + 1) * QW]
            r = lax.rsqrt(jnp.mean(o * o, axis=1, keepdims=True) + RMS_EPS)
            orr = o * r
            dg_ref[j:j + 1, :] += jnp.sum(d * orr, axis=0, keepdims=True)
            dgv = d * g_ref[...]
            d_ref[...] = r * dgv - orr * (r * r) * jnp.mean(dgv * o, axis=1, keepdims=True)

    half = pl.BlockSpec((tm, QW), lambda i: (i, 0))
    vec = pl.BlockSpec((1, QW), lambda i: (0, 0))
    return pl.pallas_call(
        body, name="outnorm_bwd", grid=(T // tm,),
        out_shape=(jax.ShapeDtypeStruct((T, QW), F32), jax.ShapeDtypeStruct((T, QW), F32),
                   jax.ShapeDtypeStruct((8, QW), F32)),
        in_specs=[pl.BlockSpec((tm, 2 * QW), lambda i: (i, 0)), half, half, vec, vec],
        out_specs=(half, half, pl.BlockSpec((8, QW), lambda i: (0, 0))),
        compiler_params=_params(("arbitrary",), 24 << 20),
    )(dy, oa, ob, ga, gb)


GELU_C = math.sqrt(2.0 / math.pi)
GELU_A = 0.044715
HALO = 8


def _gelu_parts(x):
    t = jnp.tanh(GELU_C * (x + GELU_A * (x * x * x)))
    return 0.5 * (1.0 + t), t


def _halo_specs(tm, tn, T):
    nh = tm // HALO
    last = T // HALO - 1
    cur = pl.BlockSpec((tm, tn), lambda j, i: (i, j))
    prev = pl.BlockSpec((HALO, tn), lambda j, i: (jnp.maximum(i * nh - 1, 0), j))
    nxt = pl.BlockSpec((HALO, tn), lambda j, i: (jnp.minimum((i + 1) * nh, last), j))
    return cur, prev, nxt


def _conv_glu_fwd(g, u, conv_w, conv_b, *, tm=256, tn=1408):
    T, F = g.shape
    tm, tn = min(tm, T), min(tn, F)
    cur, prev, nxt = _halo_specs(tm, tn, T)

    def body(g_ref, gp_ref, gn_ref, u_ref, w_ref, b_ref, a_ref):
        i = pl.program_id(1)
        gv = g_ref[...]
        row = lax.broadcasted_iota(jnp.int32, (tm, tn), 0)
        before = jnp.where(i * tm > 0, gp_ref[HALO - 1:HALO, :], 0.0)
        after = jnp.where((i + 1) * tm < T, gn_ref[0:1, :], 0.0)
        gm1 = jnp.where(row == 0, before, pltpu.roll(gv, 1, 0))
        gp1 = jnp.where(row == tm - 1, after, pltpu.roll(gv, tm - 1, 0))
        gc = ((b_ref[...] + gm1 * w_ref[0:1, :]) + gv * w_ref[1:2, :]) + gp1 * w_ref[2:3, :]
        cdf, _ = _gelu_parts(gc)
        a_ref[...] = (gc * cdf * u_ref[...]).astype(BF16)

    wspec = pl.BlockSpec((8, tn), lambda j, i: (0, j))
    est = 2 * (3 * _nbytes((tm, tn), F32)) + 8 * _nbytes((tm, tn), F32)
    return pl.pallas_call(
        body, name="conv_glu_fwd", grid=(F // tn, T // tm),
        out_shape=jax.ShapeDtypeStruct((T, F), BF16),
        in_specs=[cur, prev, nxt, cur, wspec, pl.BlockSpec((1, tn), lambda j, i: (0, j))],
        out_specs=cur,
        compiler_params=_params(("parallel", "parallel"), est),
    )(g, g, g, u, conv_w, conv_b)


def _conv_glu_bwd(dact, g, u, conv_w, conv_b, *, tm=256, tn=1408):
    T, F = g.shape
    tm, tn = min(tm, T), min(tn, F)
    cur, prev, nxt = _halo_specs(tm, tn, T)
    te = tm + 2 * HALO

    def body(d_ref, dp_ref, dn_ref, g_ref, gp_ref, gn_ref, u_ref, up_ref, un_ref, w_ref, b_ref,
             dg_ref, du_ref, dc_ref):
        i = pl.program_id(1)

        @pl.when(i == 0)
        def _():
            dc_ref[...] = jnp.zeros_like(dc_ref)

        grow = i * tm - HALO + lax.broadcasted_iota(jnp.int32, (te, tn), 0)
        valid = (grow >= 0) & (grow < T)
        ge = jnp.where(valid, jnp.concatenate([gp_ref[...], g_ref[...], gn_ref[...]], axis=0), 0.0)
        ue = jnp.concatenate([up_ref[...], u_ref[...], un_ref[...]], axis=0)
        de = jnp.concatenate([dp_ref[...], d_ref[...], dn_ref[...]], axis=0)
        w0, w1, w2 = w_ref[0:1, :], w_ref[1:2, :], w_ref[2:3, :]
        gm1 = pltpu.roll(ge, 1, 0)
        gp1 = pltpu.roll(ge, te - 1, 0)
        gc = ((b_ref[...] + gm1 * w0) + ge * w1) + gp1 * w2
        cdf, t = _gelu_parts(gc)
        dgelu = cdf + 0.5 * gc * (1.0 - t * t) * (GELU_C * (1.0 + 3.0 * GELU_A * (gc * gc)))
        dgc = jnp.where(valid, de * ue * dgelu, 0.0)
        dge = w0 * pltpu.roll(dgc, te - 1, 0) + w1 * dgc + w2 * pltpu.roll(dgc, 1, 0)
        mid = slice(HALO, HALO + tm)
        dg_ref[...] = dge[mid].astype(BF16)
        du_ref[...] = (de[mid] * (gc[mid] * cdf[mid])).astype(BF16)
        dgm = dgc[mid]
        dc_ref[0:1, :] += jnp.sum(dgm * gm1[mid], axis=0, keepdims=True)
        dc_ref[1:2, :] += jnp.sum(dgm * ge[mid], axis=0, keepdims=True)
        dc_ref[2:3, :] += jnp.sum(dgm * gp1[mid], axis=0, keepdims=True)
        dc_ref[3:4, :] += jnp.sum(dgm, axis=0, keepdims=True)

    wspec = pl.BlockSpec((8, tn), lambda j, i: (0, j))
    est = 2 * (3 * _nbytes((tm, tn), F32) + 2 * _nbytes((tm, tn), BF16)) + 16 * _nbytes((te, tn), F32)
    return pl.pallas_call(
        body, name="conv_glu_bwd", grid=(F // tn, T // tm),
        out_shape=(jax.ShapeDtypeStruct((T, F), BF16), jax.ShapeDtypeStruct((T, F), BF16),
                   jax.ShapeDtypeStruct((8, F), F32)),
        in_specs=[cur, prev, nxt, cur, prev, nxt, cur, prev, nxt, wspec, pl.BlockSpec((1, tn), lambda j, i: (0, j))],
        out_specs=(cur, cur, wspec),
        compiler_params=_params(("parallel", "arbitrary"), est),
    )(dact, dact, dact, g, g, g, u, u, u, conv_w, conv_b)


def _adamw_math(w, g, m, v):
    m = ADAM_B1 * m + (1.0 - ADAM_B1) * g
    v = ADAM_B2 * v + (1.0 - ADAM_B2) * (g * g)
    m_hat = m / (1.0 - ADAM_B1 ** ADAM_STEP)
    v_hat = v / (1.0 - ADAM_B2 ** ADAM_STEP)
    delta = -ADAM_LR * (m_hat / (jnp.sqrt(v_hat) + ADAM_EPS) + ADAM_WD * w)
    return delta, m, v


def _adamw(w, m, v, gparts, *, name, tr):
    R, C = w.shape
    tr = min(tr, R)
    assert R % tr == 0

    def body(w_ref, m_ref, v_ref, gp_ref, g_ref, d_ref, nm_ref, nv_ref):
        g = gp_ref[0].astype(F32)
        for j in range(1, N_DEV):
            g = g + gp_ref[j].astype(F32)
        delta, nm, nv = _adamw_math(w_ref[...], g, m_ref[...], v_ref[...])
        g_ref[...] = g
        d_ref[...] = delta
        nm_ref[...] = nm
        nv_ref[...] = nv

    blk = pl.BlockSpec((tr, C), lambda i: (i, 0))
    out = jax.ShapeDtypeStruct((R, C), F32)
    return pl.pallas_call(
        body, name=name, grid=(R // tr,), out_shape=(out, out, out, out),
        in_specs=[blk, blk, blk, pl.BlockSpec((N_DEV, tr, C), lambda i: (0, i, 0))],
        out_specs=(blk, blk, blk, blk),
        compiler_params=_params(("parallel",), 24 << 20),
    )(w, m, v, gparts)


def _rope_tables(T):
    rows_n = T // GRID_W
    row = jnp.repeat(jnp.arange(rows_n, dtype=F32), GRID_W)
    col = jnp.tile(jnp.arange(GRID_W, dtype=F32), rows_n)
    half = HEAD_DIM // 2
    inv_freq = ROPE_THETA ** (-jnp.arange(0, half, 2, dtype=F32) / half)
    ang = jnp.concatenate([row[:, None] * inv_freq, col[:, None] * inv_freq], axis=-1)
    cos, sin = jnp.cos(ang), jnp.sin(ang)
    cos64 = jnp.repeat(cos, 2, axis=-1)
    sin64 = jnp.stack([-sin, sin], axis=-1).reshape(T, HEAD_DIM)
    return jnp.tile(cos64, (1, 2)), jnp.tile(sin64, (1, 2))


def _t5_bucket(rel):
    half = N_BUCKETS // 2
    max_exact = half // 2
    bucket = jnp.where(rel > 0, half, 0)
    rp = jnp.abs(rel)
    rpf = jnp.maximum(rp, 1).astype(F32)
    large = max_exact + (jnp.log(rpf / max_exact) / math.log(MAX_DISTANCE / max_exact)
                         * (half - max_exact)).astype(jnp.int32)
    large = jnp.minimum(large, half - 1)
    return bucket + jnp.where(rp < max_exact, rp, large)


def _window_buckets():
    qpos = jnp.arange(WB, dtype=jnp.int32)
    kpos = jnp.arange(WK, dtype=jnp.int32) - WB
    return _t5_bucket(kpos[None, :] - qpos[:, None])


def _heads_first(a, nh):
    T = a.shape[0]
    return a.reshape(T, nh, HEAD_DIM).transpose(1, 0, 2)


def _heads_last(a):
    nh, T, _ = a.shape
    return a.transpose(1, 0, 2).reshape(T, nh * HEAD_DIM)


def _pad_keys(a):
    return jnp.pad(a, ((0, 0), (WB, WB), (0, 0)))


def _row(v):
    return v.reshape(1, -1)


def _rows8(rows, width):
    a = jnp.stack(list(rows), axis=0)
    return jnp.pad(a, ((0, 8 - a.shape[0]), (0, 0)))


def _layer_fwd(l, xin, W, tabs):
    xhat, xg, xb, x16 = xin
    T = xhat.shape[0]
    cos2, sin2, bias, _ = tabs
    h = _mm([x16], [W["w_in"][l]], name="mm_in", out_dtype=F32, tm=512, tn=IN_COLS, tk=D_MODEL)
    gains = _rows8([jnp.tile(W["q_norm"][l], 2), jnp.tile(W["k_norm"][l], 2)], LANES)
    roped = _qk_rope_fwd(h, gains, cos2, sin2, name="qk_rope_fwd")
    qa = _heads_first(roped[:, :QW], KV * GQ).reshape(KV, GQ * T, HEAD_DIM)
    ka = _heads_first(roped[:, QW:], KV)
    va = _heads_first(h[:, ROPE_W:ROPE_W + KW].astype(BF16), KV)
    kaT, vaT = ka.transpose(0, 2, 1), va.transpose(0, 2, 1)
    qaT = qa.transpose(0, 2, 1)
    oaT, lse_a = _attn_a_fwd(ka, qaT, vaT)
    o0 = ROPE_W + KW
    qb = _heads_first((h[:, o0:o0 + QW] * SCALE).astype(BF16), KV * GQ).reshape(KV, GQ, T, HEAD_DIM)
    kb = _pad_keys(_heads_first(h[:, o0 + QW:o0 + QW + KW].astype(BF16), KV))
    vb = _pad_keys(_heads_first(h[:, o0 + QW + KW:].astype(BF16), KV))
    kbT, vbT = kb.transpose(0, 2, 1), vb.transpose(0, 2, 1)
    sink_rows = jnp.repeat(W["sink"][l], WB).reshape(KV, GQ * WB, 1)
    ob, lse_b = _attn_b_fwd(qb, kbT, vb, bias, sink_rows)
    oa_t = oaT.reshape(KV, HEAD_DIM, GQ, T).transpose(3, 0, 2, 1).reshape(T, QW)
    ob_t = _heads_last(ob.reshape(KV * GQ, T, HEAD_DIM))
    ga, gb = _row(W["out_norm_a"][l]), _row(W["out_norm_b"][l])
    ycat = _outnorm_fwd(oa_t, ob_t, ga, gb)
    g1, b1 = _row(W["ln1_g"][l]), _row(W["ln1_b"][l])
    x1hat, rstd1, x1_16 = _mm_res_ln(ycat, W["w_out"][l], xhat, xg, xb, g1, b1, name="mm_out_ln", tm=512)
    gate = _mm([x1_16], [W["w_gate"][l]], name="mm_gate", out_dtype=F32, tm=512, tn=D_FF // 2, tk=D_MODEL)
    up = _mm([x1_16], [W["w_up"][l]], name="mm_up", out_dtype=F32, tm=512, tn=D_FF // 2, tk=D_MODEL)
    cw = jnp.pad(W["conv_w"][l], ((0, 5), (0, 0)))
    cb = _row(W["conv_b"][l])
    act = _conv_glu_fwd(gate, up, cw, cb)
    g2, b2 = _row(W["ln2_g"][l]), _row(W["ln2_b"][l])
    x2hat, rstd2, x2_16 = _mm_res_ln(act, W["w_down"][l], x1hat, g1, b1, g2, b2, name="mm_down_ln", tm=256)
    saved = dict(x16=x16, h=h, gains=gains, qa=qa, qaT=qaT, ka=ka, kaT=kaT, va=va, oaT=oaT, lse_a=lse_a,
                 qb=qb, kb=kb, kbT=kbT, vbT=vbT, ob=ob, lse_b=lse_b, sink_rows=sink_rows, oa_t=oa_t, ob_t=ob_t,
                 ga=ga, gb=gb, ycat=ycat, x1hat=x1hat, rstd1=rstd1, x1_16=x1_16, g1=g1, b1=b1, gate=gate, up=up,
                 cw=cw, cb=cb, act=act, x2hat=x2hat, rstd2=rstd2, g2=g2, b2=b2)
    return (x2hat, g2, b2, x2_16), saved


def _layer_bwd(l, S, W, WT, tabs, dz2, dz2_16, stats2):
    cos2, sin2, bias, _ = tabs
    T = dz2.shape[0]
    G = {}
    G["ln2_g"], G["ln2_b"] = stats2[0], stats2[1]
    G["w_down"] = _mm([S["act"]], [dz2_16], name="dw_down", out_dtype=BF16, trans_a=True, tm=D_FF // 2, tn=D_MODEL, tk=512)
    dact = _mm([dz2_16], [WT["w_down"][l]], name="mm_dact", out_dtype=F32, tm=512, tn=D_FF // 2, tk=D_MODEL)
    dg, du, dconv = _conv_glu_bwd(dact, S["gate"], S["up"], S["cw"], S["cb"])
    G["conv_w"], G["conv_b"] = dconv[0:3], dconv[3]
    G["w_gate"] = _mm([S["x1_16"]], [dg], name="dw_gate", out_dtype=BF16, trans_a=True, tm=D_MODEL, tn=D_FF // 2, tk=512)
    G["w_up"] = _mm([S["x1_16"]], [du], name="dw_up", out_dtype=BF16, trans_a=True, tm=D_MODEL, tn=D_FF // 2, tk=512)
    dx1 = _mm([dg, du], [WT["w_gate"][l], WT["w_up"][l]], name="mm_dx1", out_dtype=F32, tm=512, tn=D_MODEL,
              tk=D_FF // 2, add=dz2, add_scale=ALPHA)
    dz1, dz1_16, stats1 = _ln_bwd(S["x1hat"], S["rstd1"], S["g1"], S["b1"], name="ln1_bwd", dx=dx1)
    G["ln1_g"], G["ln1_b"] = stats1[0], stats1[1]
    G["w_out"] = _mm([S["ycat"]], [dz1_16], name="dw_out", out_dtype=BF16, trans_a=True, tm=D_MODEL, tn=D_MODEL, tk=512)
    dycat = _mm([dz1_16], [WT["w_out"][l]], name="mm_dycat", out_dtype=F32, tm=512, tn=D_MODEL, tk=D_MODEL)
    doa_t, dob_t, dgn = _outnorm_bwd(dycat, S["oa_t"], S["ob_t"], S["ga"], S["gb"])
    G["out_norm_a"], G["out_norm_b"] = dgn[0], dgn[1]
    doa = _heads_first(doa_t, KV * GQ).reshape(KV, GQ * T, HEAD_DIM)
    dob = _heads_first(dob_t, KV * GQ).reshape(KV, GQ, T, HEAD_DIM)
    doaT = doa.transpose(0, 2, 1)
    delta = _attn_a_delta(S["oaT"], doaT)
    dkaT, dvaT, dqaT = _attn_a_bwd(S["ka"], S["va"], S["kaT"], S["qaT"], doaT.astype(BF16), S["lse_a"], delta)
    dqa_t = dqaT.reshape(KV, HEAD_DIM, GQ, T).transpose(3, 0, 2, 1).reshape(T, QW)
    d_roped = jnp.concatenate([dqa_t, dkaT.transpose(2, 0, 1).reshape(T, KW)], axis=1)
    dh_rope, dgain = _qk_rope_bwd(S["h"], d_roped, S["gains"], cos2, sin2, name="qk_rope_bwd")
    G["q_norm"], G["k_norm"] = dgain[0, :HEAD_DIM], dgain[1, :HEAD_DIM]
    dqb, dkb, dvb, dbias, dsk = _attn_b_bwd(S["qb"], S["kbT"], S["kb"], S["vbT"], dob, S["ob"], S["lse_b"], bias,
                                            S["sink_rows"])
    dh = jnp.concatenate([
        dh_rope, dvaT.transpose(2, 0, 1).reshape(T, KW).astype(BF16),
        (_heads_last(dqb.reshape(KV * GQ, T, HEAD_DIM)) * SCALE).astype(BF16),
        _heads_last(dkb[:, WB:WB + T]).astype(BF16), _heads_last(dvb[:, WB:WB + T]).astype(BF16)], axis=1)
    G["w_in"] = _mm([S["x16"]], [dh], name="dw_in", out_dtype=BF16, trans_a=True, tm=D_MODEL, tn=IN_COLS, tk=512)
    dxin = _mm([dh], [WT["w_in"][l]], name="mm_dxin", out_dtype=F32, tm=512, tn=D_MODEL, tk=IN_COLS,
               add=dz1, add_scale=ALPHA)
    return dxin, G, dbias.reshape(KV * GQ, WB * WK), dsk.reshape(KV * GQ, WB)


BIG = ("w_in", "w_out", "w_gate", "w_up", "w_down")
COL_SHARDED = ("w_in", "w_gate", "w_up")
WIRE_COLS = 1024


def _unshard(name, gathered, shard_shape):
    L, r, c = shard_shape
    blocks = gathered.reshape(N_DEV, L, r, c)
    if name in COL_SHARDED:
        return blocks.transpose(1, 2, 0, 3).reshape(L, r, N_DEV * c)
    return blocks.transpose(1, 0, 2, 3).reshape(L, N_DEV * r, c)


def _to_owner_blocks(name, full, shard_shape):
    L, r, c = shard_shape
    if name in COL_SHARDED:
        blocks = full.reshape(L, r, N_DEV, c).transpose(2, 0, 1, 3)
    else:
        blocks = full.reshape(L, N_DEV, r, c).transpose(1, 0, 2, 3)
    return blocks.reshape(N_DEV, -1, WIRE_COLS)


def _pack_small(vals):
    flat = jnp.concatenate([vals[n].reshape(-1).astype(F32) for n in SMALL_NAMES])
    pad = (-flat.shape[0]) % (8 * LANES)
    return jnp.pad(flat, (0, pad)).reshape(-1, LANES)


def _unpack_small(packed, shapes):
    flat = packed.reshape(-1)
    out, off = {}, 0
    for n in SMALL_NAMES:
        size = math.prod(shapes[n])
        out[n] = flat[off:off + size].reshape(shapes[n])
        off += size
    return out


def kernel(x, rel_bias, w_in, q_norm, k_norm, sink, out_norm_a, out_norm_b, w_out, ln1_g, ln1_b, w_gate, w_up, conv_w, conv_b, w_down, ln2_g, ln2_b, loss_target, m_rel_bias, m_w_in, m_q_norm, m_k_norm, m_sink, m_out_norm_a, m_out_norm_b, m_w_out, m_ln1_g, m_ln1_b, m_w_gate, m_w_up, m_conv_w, m_conv_b, m_w_down, m_ln2_g, m_ln2_b, v_rel_bias, v_w_in, v_q_norm, v_k_norm, v_sink, v_out_norm_a, v_out_norm_b, v_w_out, v_ln1_g, v_ln1_b, v_w_gate, v_w_up, v_conv_w, v_conv_b, v_w_down, v_ln2_g, v_ln2_b):
    P = dict(rel_bias=rel_bias, w_in=w_in, q_norm=q_norm, k_norm=k_norm, sink=sink, out_norm_a=out_norm_a,
             out_norm_b=out_norm_b, w_out=w_out, ln1_g=ln1_g, ln1_b=ln1_b, w_gate=w_gate, w_up=w_up, conv_w=conv_w,
             conv_b=conv_b, w_down=w_down, ln2_g=ln2_g, ln2_b=ln2_b)
    M = dict(rel_bias=m_rel_bias, w_in=m_w_in, q_norm=m_q_norm, k_norm=m_k_norm, sink=m_sink, out_norm_a=m_out_norm_a,
             out_norm_b=m_out_norm_b, w_out=m_w_out, ln1_g=m_ln1_g, ln1_b=m_ln1_b, w_gate=m_w_gate, w_up=m_w_up,
             conv_w=m_conv_w, conv_b=m_conv_b, w_down=m_w_down, ln2_g=m_ln2_g, ln2_b=m_ln2_b)
    V = dict(rel_bias=v_rel_bias, w_in=v_w_in, q_norm=v_q_norm, k_norm=v_k_norm, sink=v_sink, out_norm_a=v_out_norm_a,
             out_norm_b=v_out_norm_b, w_out=v_w_out, ln1_g=v_ln1_g, ln1_b=v_ln1_b, w_gate=v_w_gate, w_up=v_w_up,
             conv_w=v_conv_w, conv_b=v_conv_b, w_down=v_w_down, ln2_g=v_ln2_g, ln2_b=v_ln2_b)
    names = list(P)
    T = x.shape[1]
    me = 4 * lax.axis_index("x") + 2 * lax.axis_index("y") + lax.axis_index("c")

    cw_shard = conv_w.reshape(-1)
    cw_pad = (-cw_shard.shape[0]) % LANES
    wire = [P[n].astype(BF16).reshape(-1, WIRE_COLS) for n in BIG] + [jnp.pad(cw_shard, (0, cw_pad)).reshape(-1, LANES)]
    gathered = _exchange(wire, gather=True, name="gather_weights")
    W = {n: _unshard(n, gathered[i], P[n].shape) for i, n in enumerate(BIG)}
    L, taps, fc = conv_w.shape
    cw_all = gathered[-1].reshape(N_DEV, -1)[:, :cw_shard.shape[0]].reshape(N_DEV, L, taps, fc)
    W["conv_w"] = cw_all.transpose(1, 2, 0, 3).reshape(L, taps, N_DEV * fc)
    for n in names:
        if n not in W:
            W[n] = P[n]
    WT = {n: W[n].transpose(0, 2, 1) for n in BIG}

    cos2, sin2 = _rope_tables(T)
    bucket = _window_buckets()
    bias = _bias_table(rel_bias.T, bucket.reshape(1, WB * WK)).reshape(KV, GQ * WB, WK)
    tabs = (cos2, sin2, bias, bucket)

    ones, zeros = jnp.ones((1, D_MODEL), F32), jnp.zeros((1, D_MODEL), F32)
    cur = (x[0], ones, zeros, x[0].astype(BF16))
    saved = []
    for l in range(DEPTH):
        cur, S = _layer_fwd(l, cur, W, tabs)
        saved.append(S)

    grads = [None] * DEPTH
    dbs, dsks = [None] * DEPTH, [None] * DEPTH
    S = saved[-1]
    dz, dz16, stats = _ln_bwd(S["x2hat"], S["rstd2"], S["g2"], S["b2"], name="loss_ln2_bwd", target=loss_target[0])
    loss = lax.psum(stats[2, 0], ("x", "y", "c"))
    for l in reversed(range(DEPTH)):
        S = saved[l]
        dxin, grads[l], dbs[l], dsks[l] = _layer_bwd(l, S, W, WT, tabs, dz, dz16, stats)
        if l > 0:
            Sp = saved[l - 1]
            dz, dz16, stats = _ln_bwd(Sp["x2hat"], Sp["rstd2"], Sp["g2"], Sp["b2"], name="ln2_bwd", dx=dxin)
    grad_x = dxin[None]

    drb, dsink = _bias_sink_grads(dbs, dsks, bucket.reshape(1, WB * WK))
    small_g = {n: jnp.stack([grads[l][n] for l in range(DEPTH)]) for n in SMALL_NAMES if n not in ("rel_bias", "sink")}
    small_g["rel_bias"] = drb.T
    small_g["sink"] = dsink.reshape(DEPTH, KV * GQ)

    big_parts = [_to_owner_blocks(n, jnp.stack([grads[l][n] for l in range(DEPTH)]), P[n].shape) for n in BIG]
    recv = _exchange(big_parts, gather=False, name="scatter_grads")
    small_recv = _exchange([_pack_small(small_g)], gather=True, name="gather_small_grads")[0]

    out_g, out_d, out_m, out_v = {}, {}, {}, {}
    for i, n in enumerate(BIG):
        shp = P[n].shape
        res = _adamw(P[n].reshape(-1, WIRE_COLS), M[n].reshape(-1, WIRE_COLS), V[n].reshape(-1, WIRE_COLS), recv[i],
                     name="adamw_" + n, tr=64)
        out_g[n], out_d[n], out_m[n], out_v[n] = (r.reshape(shp) for r in res)
    full_shapes = {n: W[n].shape for n in SMALL_NAMES}

    def small_state(D):
        vals = {n: D[n] for n in SMALL_NAMES if n != "conv_w"}
        cw = jnp.zeros((L, taps, N_DEV, fc), F32)
        cw = lax.dynamic_update_slice(cw, D["conv_w"].reshape(L, taps, 1, fc), (0, 0, me, 0))
        vals["conv_w"] = cw.reshape(L, taps, N_DEV * fc)
        return _pack_small(vals)

    sw, sm, sv = small_state(P), small_state(M), small_state(V)
    res = _adamw(sw, sm, sv, small_recv, name="adamw_small", tr=sw.shape[0])
    for dst, packed in zip((out_g, out_d, out_m, out_v), res):
        vals = _unpack_small(packed, full_shapes)
        for n in SMALL_NAMES:
            if n == "conv_w":
                sl = lax.dynamic_slice(vals[n].reshape(L, taps, N_DEV, fc), (0, 0, me, 0), (L, taps, 1, fc))
                dst[n] = sl.reshape(L, taps, fc)
            else:
                dst[n] = vals[n]
    return (loss, grad_x, *[out_g[n] for n in names], *[out_d[n] for n in names],
            *[out_m[n] for n in names], *[out_v[n] for n in names])
```

```python
import functools
import math

import jax
import jax.numpy as jnp
from jax import lax
from jax.experimental import pallas as pl
from jax.experimental.pallas import tpu as pltpu

F32 = jnp.float32
BF16 = jnp.bfloat16
MESH = pl.DeviceIdType.MESH

N_DEV = 8
D_MODEL = 1024
DEPTH = 2
HEAD_DIM = 64
KV = 2
GQ = 4
QW = KV * GQ * HEAD_DIM
KW = KV * HEAD_DIM
ROPE_W = QW + KW
IN_COLS = 2 * (QW + 2 * KW)
D_FF = 2816
GRID_W = 64
ROPE_THETA = 10000.0
WINDOW = 128
N_BUCKETS = 32
MAX_DISTANCE = 128
ALPHA = (2.0 * DEPTH) ** 0.25
RMS_EPS = 1e-6
LN_EPS = 1e-5
SCALE = HEAD_DIM ** -0.5
LOG2E = math.log2(math.e)
LN2 = math.log(2.0)
NEG = -1e30
ONES_ROWS = 16

ADAM_LR = 0.001
ADAM_B1 = 0.9
ADAM_B2 = 0.999
ADAM_EPS = 1e-08
ADAM_WD = 0.01
ADAM_STEP = 10

LANES = 128
VMEM_CAP = 60 * 1024 * 1024
SMALL_NAMES = ("rel_bias", "q_norm", "k_norm", "sink", "out_norm_a", "out_norm_b", "ln1_g", "ln1_b",
               "conv_b", "ln2_g", "ln2_b", "conv_w")


def _params(sem, est_bytes):
    limit = int(min(VMEM_CAP, est_bytes + (8 << 20)))
    return pltpu.CompilerParams(dimension_semantics=sem, vmem_limit_bytes=limit)


def _nbytes(shape, dtype):
    return math.prod(shape) * jnp.dtype(dtype).itemsize


def _exchange(parts, gather, name):
    n = len(parts)
    blk = [p.shape if gather else p.shape[1:] for p in parts]

    def body(*refs):
        ins, outs = refs[:n], refs[n:2 * n]
        send_sems, recv_sems, local_sems = refs[2 * n:]
        me = 4 * lax.axis_index("x") + 2 * lax.axis_index("y") + lax.axis_index("c")

        def src(k, j):
            return ins[k] if gather else ins[k].at[j]

        def remote(k, d):
            peer = lax.rem(me + d, N_DEV)
            return pltpu.make_async_remote_copy(
                src_ref=src(k, peer), dst_ref=outs[k].at[me],
                send_sem=send_sems.at[k * (N_DEV - 1) + d - 1], recv_sem=recv_sems.at[k * (N_DEV - 1) + d - 1],
                device_id=(peer // 4, lax.rem(peer // 2, 2), lax.rem(peer, 2)), device_id_type=MESH)

        def arrival(k, d):
            frm = lax.rem(me + N_DEV - d, N_DEV)
            return pltpu.make_async_remote_copy(
                src_ref=src(k, frm), dst_ref=outs[k].at[frm],
                send_sem=send_sems.at[k * (N_DEV - 1) + d - 1], recv_sem=recv_sems.at[k * (N_DEV - 1) + d - 1],
                device_id=(frm // 4, lax.rem(frm // 2, 2), lax.rem(frm, 2)), device_id_type=MESH)

        local = [pltpu.make_async_copy(src(k, me), outs[k].at[me], local_sems.at[k]) for k in range(n)]
        sends = [remote(k, d) for k in range(n) for d in range(1, N_DEV)]
        for cp in local + sends:
            cp.start()
        for k in range(n):
            for d in range(1, N_DEV):
                arrival(k, d).wait_recv()
        for cp in sends:
            cp.wait_send()
        for cp in local:
            cp.wait()

    hbm = pl.BlockSpec(memory_space=pltpu.HBM)
    return pl.pallas_call(
        body, name=name,
        out_shape=[jax.ShapeDtypeStruct((N_DEV,) + tuple(b), p.dtype) for b, p in zip(blk, parts)],
        in_specs=[hbm] * n, out_specs=[hbm] * n,
        scratch_shapes=[pltpu.SemaphoreType.DMA((n * (N_DEV - 1),)), pltpu.SemaphoreType.DMA((n * (N_DEV - 1),)),
                        pltpu.SemaphoreType.DMA((n,))],
    )(*parts)


def _mm(a_list, b_list, *, name, out_dtype, tm, tn, tk, trans_a=False, add=None, add_scale=1.0):
    na = len(a_list)
    if trans_a:
        K, M = a_list[0].shape
    else:
        M, K = a_list[0].shape
    N = b_list[0].shape[1]
    tm, tn, tk = min(tm, M), min(tn, N), min(tk, K)
    assert M % tm == 0 and N % tn == 0 and K % tk == 0, (name, M, N, K, tm, tn, tk)
    nk = K // tk
    dims = (((0,), (0,)), ((), ())) if trans_a else (((1,), (0,)), ((), ()))

    def body(*refs):
        a_refs, b_refs = refs[:na], refs[na:2 * na]
        add_ref = refs[2 * na] if add is not None else None
        o_ref, acc_ref = refs[-2], refs[-1]
        k = pl.program_id(2)

        @pl.when(k == 0)
        def _():
            acc_ref[...] = jnp.zeros_like(acc_ref)

        part = None
        for a_ref, b_ref in zip(a_refs, b_refs):
            prod = lax.dot_general(a_ref[...].astype(BF16), b_ref[...].astype(BF16), dims,
                                   preferred_element_type=F32)
            part = prod if part is None else part + prod
        acc_ref[...] += part

        @pl.when(k == nk - 1)
        def _():
            res = acc_ref[...]
            if add_ref is not None:
                res = res + add_scale * add_ref[...]
            o_ref[...] = res.astype(o_ref.dtype)

    if trans_a:
        a_spec = pl.BlockSpec((tk, tm), lambda i, j, k: (k, i))
    else:
        a_spec = pl.BlockSpec((tm, tk), lambda i, j, k: (i, k))
    b_spec = pl.BlockSpec((tk, tn), lambda i, j, k: (k, j))
    o_spec = pl.BlockSpec((tm, tn), lambda i, j, k: (i, j))
    in_specs = [a_spec] * na + [b_spec] * na + ([o_spec] if add is not None else [])
    est = (2 * na * (_nbytes((tm, tk), a_list[0].dtype) + _nbytes((tk, tn), b_list[0].dtype))
           + na * (_nbytes((tm, tk), BF16) + _nbytes((tk, tn), BF16))
           + 2 * _nbytes((tm, tn), out_dtype) + 3 * _nbytes((tm, tn), F32)
           + (2 * _nbytes((tm, tn), F32) if add is not None else 0))
    args = list(a_list) + list(b_list) + ([add] if add is not None else [])
    return pl.pallas_call(
        body, name=name, grid=(M // tm, N // tn, nk),
        out_shape=jax.ShapeDtypeStruct((M, N), out_dtype),
        in_specs=in_specs, out_specs=o_spec,
        scratch_shapes=[pltpu.VMEM((tm, tn), F32)],
        compiler_params=_params(("parallel", "parallel", "arbitrary"), est),
    )(*args)


def _mm_res_ln(a, w, res_hat, res_g, res_b, ln_g, ln_b, *, name, tm):
    T, K = a.shape
    D = w.shape[1]
    tm = min(tm, T)

    def body(a_ref, w_ref, rh_ref, rg_ref, rb_ref, g_ref, b_ref, xhat_ref, rstd_ref, xb_ref):
        branch = jnp.dot(a_ref[...].astype(BF16), w_ref[...], preferred_element_type=F32)
        z = ALPHA * (rh_ref[...] * rg_ref[...] + rb_ref[...]) + branch
        mu = jnp.mean(z, axis=1, keepdims=True)
        zc = z - mu
        var = jnp.mean(zc * zc, axis=1, keepdims=True)
        rstd = lax.rsqrt(var + LN_EPS)
        xhat = zc * rstd
        xhat_ref[...] = xhat
        rstd_ref[...] = rstd
        xb_ref[...] = (xhat * g_ref[...] + b_ref[...]).astype(BF16)

    row = pl.BlockSpec((tm, D), lambda i: (i, 0))
    vec = pl.BlockSpec((1, D), lambda i: (0, 0))
    est = (2 * (_nbytes((tm, K), a.dtype) + _nbytes((K, D), BF16)) + 4 * _nbytes((tm, D), F32) * 2
           + 6 * _nbytes((tm, D), F32))
    return pl.pallas_call(
        body, name=name, grid=(T // tm,),
        out_shape=(jax.ShapeDtypeStruct((T, D), F32), jax.ShapeDtypeStruct((T, 1), F32),
                   jax.ShapeDtypeStruct((T, D), BF16)),
        in_specs=[pl.BlockSpec((tm, K), lambda i: (i, 0)), pl.BlockSpec((K, D), lambda i: (0, 0)), row, vec, vec, vec, vec],
        out_specs=(row, pl.BlockSpec((tm, 1), lambda i: (i, 0)), row),
        compiler_params=_params(("parallel",), est),
    )(a, w, res_hat, res_g, res_b, ln_g, ln_b)


def _ln_bwd(xhat, rstd, ln_g, ln_b, *, name, dx=None, target=None, tm=256):
    T, D = xhat.shape
    tm = min(tm, T)
    head = target is not None

    def body(xhat_ref, rstd_ref, g_ref, b_ref, d_ref, dz_ref, dzb_ref, st_ref):
        i = pl.program_id(0)

        @pl.when(i == 0)
        def _():
            st_ref[...] = jnp.zeros_like(st_ref)

        xh = xhat_ref[...]
        g = g_ref[...]
        if head:
            err = (xh * g + b_ref[...]) - d_ref[...]
            dxv = err * (1.0 / D)
            st_ref[2:3, :] += 0.5 * jnp.sum(jnp.sum(err * err, axis=1, keepdims=True) * (1.0 / D), axis=0, keepdims=True)
        else:
            dxv = d_ref[...]
        st_ref[0:1, :] += jnp.sum(dxv * xh, axis=0, keepdims=True)
        st_ref[1:2, :] += jnp.sum(dxv, axis=0, keepdims=True)
        dxh = dxv * g
        m1 = jnp.mean(dxh, axis=1, keepdims=True)
        m2 = jnp.mean(dxh * xh, axis=1, keepdims=True)
        dz = rstd_ref[...] * (dxh - m1 - xh * m2)
        dz_ref[...] = dz
        dzb_ref[...] = dz.astype(BF16)

    row = pl.BlockSpec((tm, D), lambda i: (i, 0))
    vec = pl.BlockSpec((1, D), lambda i: (0, 0))
    est = 2 * 4 * _nbytes((tm, D), F32) + 6 * _nbytes((tm, D), F32)
    return pl.pallas_call(
        body, name=name, grid=(T // tm,),
        out_shape=(jax.ShapeDtypeStruct((T, D), F32), jax.ShapeDtypeStruct((T, D), BF16),
                   jax.ShapeDtypeStruct((8, D), F32)),
        in_specs=[row, pl.BlockSpec((tm, 1), lambda i: (i, 0)), vec, vec, row],
        out_specs=(row, row, pl.BlockSpec((8, D), lambda i: (0, 0))),
        compiler_params=_params(("arbitrary",), est),
    )(xhat, rstd, ln_g, ln_b, target if head else dx)


def _pair_swap(v, even):
    return jnp.where(even, pltpu.roll(v, LANES - 1, 1), pltpu.roll(v, 1, 1))


def _half_sums(v, lo):
    s_lo = jnp.sum(jnp.where(lo, v, 0.0), axis=1, keepdims=True)
    s_hi = jnp.sum(jnp.where(lo, 0.0, v), axis=1, keepdims=True)
    return jnp.where(lo, s_lo, s_hi)


def _qk_rope_fwd(h, gains, cos2, sin2, *, name, tm=256):
    T = h.shape[0]
    tm = min(tm, T)
    nch = ROPE_W // LANES

    def body(h_ref, g_ref, c_ref, s_ref, o_ref):
        lane = lax.broadcasted_iota(jnp.int32, (tm, LANES), 1)
        lo, even = lane < HEAD_DIM, lane % 2 == 0
        c, s = c_ref[...], s_ref[...]
        for j in range(nch):
            x = h_ref[:, j * LANES:(j + 1) * LANES]
            isq = j < QW // LANES
            g = g_ref[0:1, :] if isq else g_ref[1:2, :]
            r = lax.rsqrt(_half_sums(x * x, lo) * (1.0 / HEAD_DIM) + RMS_EPS)
            nrm = x * r * g
            out = nrm * c + _pair_swap(nrm, even) * s
            if isq:
                out = out * (SCALE * LOG2E)
            o_ref[:, j * LANES:(j + 1) * LANES] = out.astype(BF16)

    est = 2 * (_nbytes((tm, ROPE_W), F32) + _nbytes((tm, ROPE_W), BF16) + 2 * _nbytes((tm, LANES), F32)) + (4 << 20)
    return pl.pallas_call(
        body, name=name, grid=(T // tm,),
        out_shape=jax.ShapeDtypeStruct((T, ROPE_W), BF16),
        in_specs=[pl.BlockSpec((tm, ROPE_W), lambda i: (i, 0)), pl.BlockSpec((8, LANES), lambda i: (0, 0)),
                  pl.BlockSpec((tm, LANES), lambda i: (i, 0)), pl.BlockSpec((tm, LANES), lambda i: (i, 0))],
        out_specs=pl.BlockSpec((tm, ROPE_W), lambda i: (i, 0)),
        compiler_params=_params(("parallel",), est),
    )(h, gains, cos2, sin2)


def _qk_rope_bwd(h, d_out, gains, cos2, sin2, *, name, tm=256):
    T = h.shape[0]
    tm = min(tm, T)
    nch = ROPE_W // LANES

    def body(h_ref, d_ref, g_ref, c_ref, s_ref, dh_ref, dg_ref):
        i = pl.program_id(0)

        @pl.when(i == 0)
        def _():
            dg_ref[...] = jnp.zeros_like(dg_ref)

        lane = lax.broadcasted_iota(jnp.int32, (tm, LANES), 1)
        lo, even = lane < HEAD_DIM, lane % 2 == 0
        c, s = c_ref[...], s_ref[...]
        acc = [None, None]
        for j in range(nch):
            x = h_ref[:, j * LANES:(j + 1) * LANES]
            isq = j < QW // LANES
            g = g_ref[0:1, :] if isq else g_ref[1:2, :]
            d = d_ref[:, j * LANES:(j + 1) * LANES]
            if isq:
                d = d * SCALE
            r = lax.rsqrt(_half_sums(x * x, lo) * (1.0 / HEAD_DIM) + RMS_EPS)
            dn = d * c + _pair_swap(d * s, even)
            xr = x * r
            part = jnp.sum(dn * xr, axis=0, keepdims=True)
            acc[0 if isq else 1] = part if acc[0 if isq else 1] is None else acc[0 if isq else 1] + part
            dng = dn * g
            dx = r * dng - xr * (r * r) * (_half_sums(dng * x, lo) * (1.0 / HEAD_DIM))
            dh_ref[:, j * LANES:(j + 1) * LANES] = dx.astype(BF16)
        for row in range(2):
            folded = acc[row] + pltpu.roll(acc[row], HEAD_DIM, 1)
            dg_ref[row:row + 1, :] += folded

    est = 2 * (2 * _nbytes((tm, ROPE_W), F32) + _nbytes((tm, ROPE_W), BF16) + 2 * _nbytes((tm, LANES), F32)) + (4 << 20)
    return pl.pallas_call(
        body, name=name, grid=(T // tm,),
        out_shape=(jax.ShapeDtypeStruct((T, ROPE_W), BF16), jax.ShapeDtypeStruct((8, LANES), F32)),
        in_specs=[pl.BlockSpec((tm, ROPE_W), lambda i: (i, 0)), pl.BlockSpec((tm, ROPE_W), lambda i: (i, 0)),
                  pl.BlockSpec((8, LANES), lambda i: (0, 0)),
                  pl.BlockSpec((tm, LANES), lambda i: (i, 0)), pl.BlockSpec((tm, LANES), lambda i: (i, 0))],
        out_specs=(pl.BlockSpec((tm, ROPE_W), lambda i: (i, 0)), pl.BlockSpec((8, LANES), lambda i: (0, 0))),
        compiler_params=_params(("arbitrary",), est),
    )(h, d_out, gains, cos2, sin2)


def _attn_a_fwd(k, qT, v1T, *, tq=4096, tk=512, cq=512):
    G, T, HD = k.shape
    NQ = qT.shape[2]
    HE = v1T.shape[1]
    tq, tk = min(tq, NQ), min(tk, T)
    cq = min(cq, tq)
    nk = T // tk

    def body(k_ref, qT_ref, vT_ref, oT_ref, lse_ref, m_sc, acc_sc):
        kv = pl.program_id(2)

        @pl.when(kv == 0)
        def _():
            m_sc[...] = jnp.full_like(m_sc, NEG)
            acc_sc[...] = jnp.zeros_like(acc_sc)

        def scores(c):
            return jnp.dot(k_ref[...], qT_ref[:, c * cq:(c + 1) * cq], preferred_element_type=F32)

        nc = tq // cq
        ahead = scores(0)
        for c in range(nc):
            cols = slice(c * cq, (c + 1) * cq)
            sT = ahead
            if c + 1 < nc:
                ahead = scores(c + 1)
            m_prev = m_sc[:, cols]
            m_new = jnp.maximum(m_prev, jnp.max(sT, axis=0, keepdims=True))
            pT = jnp.exp2(sT - m_new).astype(BF16)
            acc_sc[:, cols] = (jnp.exp2(m_prev - m_new) * acc_sc[:, cols]
                               + jnp.dot(vT_ref[...], pT, preferred_element_type=F32))
            m_sc[:, cols] = m_new

        @pl.when(kv == nk - 1)
        def _():
            l = acc_sc[HD:HD + 1, :]
            oT_ref[...] = acc_sc[0:HD, :] / l
            lse_ref[...] = m_sc[...] + jnp.log2(l)

    qtr = pl.BlockSpec((None, HD, tq), lambda g, i, j: (g, 0, i))
    qvec = pl.BlockSpec((None, 1, tq), lambda g, i, j: (g, 0, i))
    est = 6 * _nbytes((cq, tk), F32) + (8 << 20)
    return pl.pallas_call(
        body, name="attn_a_fwd", grid=(G, NQ // tq, nk),
        out_shape=(jax.ShapeDtypeStruct((G, HD, NQ), F32), jax.ShapeDtypeStruct((G, 1, NQ), F32)),
        in_specs=[pl.BlockSpec((None, tk, HD), lambda g, i, j: (g, j, 0)), qtr,
                  pl.BlockSpec((None, HE, tk), lambda g, i, j: (g, 0, j))],
        out_specs=(qtr, qvec),
        scratch_shapes=[pltpu.VMEM((1, tq), F32), pltpu.VMEM((HE, tq), F32)],
        compiler_params=_params(("parallel", "parallel", "arbitrary"), est),
    )(k, qT, v1T)


def _attn_a_delta(oT, doT, *, tq=2048):
    G, HD, NQ = oT.shape
    tq = min(tq, NQ)

    def body(o_ref, d_ref, dl_ref):
        dl_ref[...] = jnp.sum(o_ref[...] * d_ref[...], axis=0, keepdims=True)

    qtr = pl.BlockSpec((None, HD, tq), lambda g, i: (g, 0, i))
    return pl.pallas_call(
        body, name="attn_a_delta", grid=(G, NQ // tq),
        out_shape=jax.ShapeDtypeStruct((G, 1, NQ), F32),
        in_specs=[qtr, qtr], out_specs=pl.BlockSpec((None, 1, tq), lambda g, i: (g, 0, i)),
        compiler_params=_params(("parallel", "parallel"), 8 << 20),
    )(oT, doT)


def _attn_a_bwd(k, v, kT, qT, doT, lse_row, delta_row, *, tq=4096, tk=512, cq=512):
    G, T, HD = k.shape
    NQ = qT.shape[2]
    tq, tk = min(tq, NQ), min(tk, T)
    cq = min(cq, tq)
    nq, nc = NQ // tq, tq // cq
    nt = (((1,), (1,)), ((), ()))

    def body(k_ref, v_ref, kT_ref, qT_ref, doT_ref, lse_ref, dl_ref, dkT_ref, dvT_ref, dqT_ref, dk_sc, dv_sc):
        j, i = pl.program_id(1), pl.program_id(2)

        @pl.when((j == 0) & (i == 0))
        def _():
            dqT_ref[...] = jnp.zeros_like(dqT_ref)

        @pl.when(i == 0)
        def _():
            dk_sc[...] = jnp.zeros_like(dk_sc)
            dv_sc[...] = jnp.zeros_like(dv_sc)

        def scores(c):
            cols = slice(c * cq, (c + 1) * cq)
            return (jnp.dot(k_ref[...], qT_ref[:, cols], preferred_element_type=F32),
                    jnp.dot(v_ref[...], doT_ref[:, cols], preferred_element_type=F32))

        ahead = scores(0)
        dk_part = dv_part = None
        for c in range(nc):
            cols = slice(c * cq, (c + 1) * cq)
            sT, dpT = ahead
            if c + 1 < nc:
                ahead = scores(c + 1)
            pT = jnp.exp2(sT - lse_ref[:, cols])
            dsT = (pT * (dpT - dl_ref[:, cols])).astype(BF16)
            dv_c = lax.dot_general(doT_ref[:, cols], pT.astype(BF16), nt, preferred_element_type=F32)
            dk_c = lax.dot_general(qT_ref[:, cols], dsT, nt, preferred_element_type=F32)
            dv_part = dv_c if dv_part is None else dv_part + dv_c
            dk_part = dk_c if dk_part is None else dk_part + dk_c
            out_cols = pl.ds(pl.multiple_of(i * tq + c * cq, cq), cq)
            dqT_ref[:, out_cols] += jnp.dot(kT_ref[...], dsT, preferred_element_type=F32)
        dk_sc[...] += dk_part
        dv_sc[...] += dv_part

        @pl.when(i == nq - 1)
        def _():
            dkT_ref[...] = dk_sc[...] * LN2
            dvT_ref[...] = dv_sc[...]

    krow = pl.BlockSpec((None, tk, HD), lambda g, j, i: (g, j, 0))
    ktr = pl.BlockSpec((None, HD, tk), lambda g, j, i: (g, 0, j))
    qtr = pl.BlockSpec((None, HD, tq), lambda g, j, i: (g, 0, i))
    qvec = pl.BlockSpec((None, 1, tq), lambda g, j, i: (g, 0, i))
    whole = pl.BlockSpec((None, HD, NQ), lambda g, j, i: (g, 0, 0))
    est = 8 * _nbytes((cq, tk), F32) + 2 * _nbytes((HD, NQ), F32) + (8 << 20)
    return pl.pallas_call(
        body, name="attn_a_bwd", grid=(G, T // tk, nq),
        out_shape=(jax.ShapeDtypeStruct((G, HD, T), F32), jax.ShapeDtypeStruct((G, HD, T), F32),
                   jax.ShapeDtypeStruct((G, HD, NQ), F32)),
        in_specs=[krow, krow, ktr, qtr, qtr, qvec, qvec],
        out_specs=(ktr, ktr, whole),
        scratch_shapes=[pltpu.VMEM((HD, tk), F32), pltpu.VMEM((HD, tk), F32)],
        compiler_params=_params(("arbitrary", "arbitrary", "arbitrary"), est),
    )(k, v, kT, qT, doT, lse_row, delta_row)


WB = WINDOW
WK = 3 * WINDOW


def _win_specs(T):
    nb = T // WB
    q4 = pl.BlockSpec((None, GQ, WB, HEAD_DIM), lambda g, n: (g, 0, n, 0))
    col = pl.BlockSpec((None, GQ * WB, 1), lambda g, n: (g, 0, 0))
    bias = pl.BlockSpec((None, GQ * WB, WK), lambda g, n: (g, 0, 0))
    kt = [pl.BlockSpec((None, HEAD_DIM, WB), functools.partial(lambda g, n, o: (g, 0, n + o), o=o)) for o in range(3)]
    kr = [pl.BlockSpec((None, WB, HEAD_DIM), functools.partial(lambda g, n, o: (g, n + o, 0), o=o)) for o in range(3)]
    return nb, q4, col, bias, kt, kr


def _win_mask(n, T):
    kabs = n * WB - WB + lax.broadcasted_iota(jnp.int32, (1, WK), 1)
    return (kabs >= 0) & (kabs < T)


def _attn_b_fwd(q, kTp, vp, bias, sink_rows):
    T = q.shape[2]
    nb, q4, col, bias_spec, kt, kr = _win_specs(T)

    def body(q_ref, k0, k1, k2, v0, v1, v2, b_ref, sk_ref, o_ref, lse_ref):
        n = pl.program_id(1)
        qv = q_ref[...].reshape(GQ * WB, HEAD_DIM)
        kT = jnp.concatenate([k0[...], k1[...], k2[...]], axis=1)
        vv = jnp.concatenate([v0[...], v1[...], v2[...]], axis=0)
        s = jnp.dot(qv, kT, preferred_element_type=F32) + b_ref[...]
        s = jnp.where(_win_mask(n, T), s, NEG)
        sk = sk_ref[...]
        m = jnp.maximum(jnp.max(s, axis=1, keepdims=True), sk)
        p = jnp.exp(s - m)
        den = jnp.sum(p, axis=1, keepdims=True) + jnp.exp(sk - m)
        o = jnp.dot(p.astype(BF16), vv, preferred_element_type=F32) / den
        o_ref[...] = o.reshape(GQ, WB, HEAD_DIM)
        lse_ref[...] = (m + jnp.log(den)).reshape(GQ, WB, 1)

    return pl.pallas_call(
        body, name="attn_b_fwd", grid=(KV, nb),
        out_shape=(jax.ShapeDtypeStruct((KV, GQ, T, HEAD_DIM), F32), jax.ShapeDtypeStruct((KV, GQ, T, 1), F32)),
        in_specs=[q4] + kt + kr + [bias_spec, col],
        out_specs=(q4, pl.BlockSpec((None, GQ, WB, 1), lambda g, n: (g, 0, n, 0))),
        compiler_params=_params(("parallel", "parallel"), 24 << 20),
    )(q, kTp, kTp, kTp, vp, vp, vp, bias, sink_rows)


def _attn_b_bwd(q, kTp, kp, vTp, do, o, lse, bias, sink_rows):
    T = q.shape[2]
    nb, q4, col, bias_spec, kt, kr = _win_specs(T)
    Tp = T + 2 * WB

    def body(q_ref, k0, k1, k2, r0, r1, r2, w0, w1, w2, do_ref, o_ref, lse_ref, b_ref, sk_ref,
             dq_ref, dk_ref, dv_ref, db_ref, dsk_ref):
        n = pl.program_id(1)

        @pl.when(n == 0)
        def _():
            dk_ref[...] = jnp.zeros_like(dk_ref)
            dv_ref[...] = jnp.zeros_like(dv_ref)
            db_ref[...] = jnp.zeros_like(db_ref)
            dsk_ref[...] = jnp.zeros_like(dsk_ref)

        qv = q_ref[...].reshape(GQ * WB, HEAD_DIM)
        kT = jnp.concatenate([k0[...], k1[...], k2[...]], axis=1)
        kk = jnp.concatenate([r0[...], r1[...], r2[...]], axis=0)
        vT = jnp.concatenate([w0[...], w1[...], w2[...]], axis=1)
        dov = do_ref[...].reshape(GQ * WB, HEAD_DIM)
        lse = lse_ref[...].reshape(GQ * WB, 1)
        delta = jnp.sum(dov * o_ref[...].reshape(GQ * WB, HEAD_DIM), axis=1, keepdims=True)
        s = jnp.dot(qv, kT, preferred_element_type=F32) + b_ref[...]
        s = jnp.where(_win_mask(n, T), s, NEG)
        p = jnp.exp(s - lse)
        dob = dov.astype(BF16)
        dp = jnp.dot(dob, vT, preferred_element_type=F32)
        ds = p * (dp - delta)
        db_ref[...] += ds
        dsk_ref[...] -= jnp.exp(sk_ref[...] - lse) * delta
        dsb = ds.astype(BF16)
        dq_ref[...] = jnp.dot(dsb, kk, preferred_element_type=F32).reshape(GQ, WB, HEAD_DIM)
        tn = (((0,), (0,)), ((), ()))
        rows = pl.ds(pl.multiple_of(n * WB, WB), WK)
        dk_ref[rows, :] += lax.dot_general(dsb, qv, tn, preferred_element_type=F32)
        dv_ref[rows, :] += lax.dot_general(p.astype(BF16), dob, tn, preferred_element_type=F32)

    kacc = pl.BlockSpec((None, Tp, HEAD_DIM), lambda g, n: (g, 0, 0))
    return pl.pallas_call(
        body, name="attn_b_bwd", grid=(KV, nb),
        out_shape=(jax.ShapeDtypeStruct((KV, GQ, T, HEAD_DIM), F32),
                   jax.ShapeDtypeStruct((KV, Tp, HEAD_DIM), F32), jax.ShapeDtypeStruct((KV, Tp, HEAD_DIM), F32),
                   jax.ShapeDtypeStruct((KV, GQ * WB, WK), F32), jax.ShapeDtypeStruct((KV, GQ * WB, 1), F32)),
        in_specs=[q4] + kt + kr + kt + [q4, q4, pl.BlockSpec((None, GQ, WB, 1), lambda g, n: (g, 0, n, 0)), bias_spec, col],
        out_specs=(q4, kacc, kacc, bias_spec, col),
        compiler_params=_params(("parallel", "arbitrary"), 40 << 20),
    )(q, kTp, kTp, kTp, kp, kp, kp, vTp, vTp, vTp, do, o, lse, bias, sink_rows)


def _bias_table(rel_bias_t, bucket):
    nh, n = rel_bias_t.shape[0], bucket.shape[1]

    def body(rb_ref, bk_ref, o_ref):
        bk = bk_ref[...]
        out = jnp.full((nh, n), NEG, F32)
        for b in range(N_BUCKETS):
            out = jnp.where(bk == b, rb_ref[:, b:b + 1], out)
        o_ref[...] = out

    return pl.pallas_call(
        body, name="bias_table", out_shape=jax.ShapeDtypeStruct((nh, n), F32),
        compiler_params=pltpu.CompilerParams(vmem_limit_bytes=32 << 20),
    )(rel_bias_t, bucket)


def _bias_sink_grads(db_list, dsk_list, bucket):
    L = len(db_list)

    def body(*refs):
        db_refs, dsk_refs, bk_ref = refs[:L], refs[L:2 * L], refs[2 * L]
        drb_ref, dsink_ref = refs[2 * L + 1], refs[2 * L + 2]
        tot = db_refs[0][...]
        for r in db_refs[1:]:
            tot = tot + r[...]
        bk = bk_ref[...]
        lane = lax.broadcasted_iota(jnp.int32, (2 * GQ, N_BUCKETS), 1)
        out = jnp.zeros((2 * GQ, N_BUCKETS), F32)
        for b in range(N_BUCKETS):
            sb = jnp.sum(jnp.where(bk == b, tot, 0.0), axis=1, keepdims=True)
            out = jnp.where(lane == b, sb, out)
        drb_ref[...] = out
        for l in range(L):
            dsink_ref[l] = jnp.sum(dsk_refs[l][...], axis=1, keepdims=True)

    return pl.pallas_call(
        body, name="bias_sink_grads",
        out_shape=(jax.ShapeDtypeStruct((2 * GQ, N_BUCKETS), F32), jax.ShapeDtypeStruct((L, 2 * GQ, 1), F32)),
        compiler_params=pltpu.CompilerParams(vmem_limit_bytes=32 << 20),
    )(*db_list, *dsk_list, bucket)


def _outnorm_fwd(oa, ob, ga, gb, *, tm=512):
    T = oa.shape[0]
    tm = min(tm, T)

    def body(oa_ref, ob_ref, ga_ref, gb_ref, y_ref):
        for j, (o_ref, g_ref) in enumerate(((oa_ref, ga_ref), (ob_ref, gb_ref))):
            o = o_ref[...]
            r = lax.rsqrt(jnp.mean(o * o, axis=1, keepdims=True) + RMS_EPS)
            y_ref[:, j * QW:(j + 1) * QW] = (o * r * g_ref[...]).astype(BF16)

    half = pl.BlockSpec((tm, QW), lambda i: (i, 0))
    vec = pl.BlockSpec((1, QW), lambda i: (0, 0))
    return pl.pallas_call(
        body, name="outnorm_fwd", grid=(T // tm,),
        out_shape=jax.ShapeDtypeStruct((T, 2 * QW), BF16),
        in_specs=[half, half, vec, vec], out_specs=pl.BlockSpec((tm, 2 * QW), lambda i: (i, 0)),
        compiler_params=_params(("parallel",), 16 << 20),
    )(oa, ob, ga, gb)


def _outnorm_bwd(dy, oa, ob, ga, gb, *, tm=512):
    T = oa.shape[0]
    tm = min(tm, T)

    def body(dy_ref, oa_ref, ob_ref, ga_ref, gb_ref, doa_ref, dob_ref, dg_ref):
        i = pl.program_id(0)

        @pl.when(i == 0)
        def _():
            dg_ref[...] = jnp.zeros_like(dg_ref)

        for j, (o_ref, g_ref, d_ref) in enumerate(((oa_ref, ga_ref, doa_ref), (ob_ref, gb_ref, dob_ref))):
            o = o_ref[...]
            d = dy_ref[:, j * QW:(j + 1) * QW]
            r = lax.rsqrt(jnp.mean(o * o, axis=1, keepdims=True) + RMS_EPS)
            orr = o * r
            dg_ref[j:j + 1, :] += jnp.sum(d * orr, axis=0, keepdims=True)
            dgv = d * g_ref[...]
            d_ref[...] = r * dgv - orr * (r * r) * jnp.mean(dgv * o, axis=1, keepdims=True)

    half = pl.BlockSpec((tm, QW), lambda i: (i, 0))
    vec = pl.BlockSpec((1, QW), lambda i: (0, 0))
    return pl.pallas_call(
        body, name="outnorm_bwd", grid=(T // tm,),
        out_shape=(jax.ShapeDtypeStruct((T, QW), F32), jax.ShapeDtypeStruct((T, QW), F32),
                   jax.ShapeDtypeStruct((8, QW), F32)),
        in_specs=[pl.BlockSpec((tm, 2 * QW), lambda i: (i, 0)), half, half, vec, vec],
        out_specs=(half, half, pl.BlockSpec((8, QW), lambda i: (0, 0))),
        compiler_params=_params(("arbitrary",), 24 << 20),
    )(dy, oa, ob, ga, gb)


GELU_C = math.sqrt(2.0 / math.pi)
GELU_A = 0.044715
HALO = 16


def _gelu_parts(x):
    t = jnp.tanh(GELU_C * (x + GELU_A * (x * x * x)))
    return 0.5 * (1.0 + t), t


def _halo_specs(tm, tn, T):
    nh = tm // HALO
    last = T // HALO - 1
    cur = pl.BlockSpec((tm, tn), lambda j, i: (i, j))
    prev = pl.BlockSpec((HALO, tn), lambda j, i: (jnp.maximum(i * nh - 1, 0), j))
    nxt = pl.BlockSpec((HALO, tn), lambda j, i: (jnp.minimum((i + 1) * nh, last), j))
    return cur, prev, nxt


def _conv_glu_fwd(g, u, conv_w, conv_b, *, tm=256, tn=1408):
    T, F = g.shape
    tm, tn = min(tm, T), min(tn, F)
    cur, prev, nxt = _halo_specs(tm, tn, T)

    def body(g_ref, gp_ref, gn_ref, u_ref, w_ref, b_ref, a_ref):
        i = pl.program_id(1)
        gv = g_ref[...]
        row = lax.broadcasted_iota(jnp.int32, (tm, tn), 0)
        before = jnp.where(i * tm > 0, gp_ref[HALO - 1:HALO, :], 0.0)
        after = jnp.where((i + 1) * tm < T, gn_ref[0:1, :], 0.0)
        gm1 = jnp.where(row == 0, before, pltpu.roll(gv, 1, 0))
        gp1 = jnp.where(row == tm - 1, after, pltpu.roll(gv, tm - 1, 0))
        gc = ((b_ref[...] + gm1 * w_ref[0:1, :]) + gv * w_ref[1:2, :]) + gp1 * w_ref[2:3, :]
        cdf, _ = _gelu_parts(gc)
        a_ref[...] = (gc * cdf * u_ref[...].astype(F32)).astype(BF16)

    wspec = pl.BlockSpec((8, tn), lambda j, i: (0, j))
    est = 2 * (3 * _nbytes((tm, tn), F32)) + 8 * _nbytes((tm, tn), F32)
    return pl.pallas_call(
        body, name="conv_glu_fwd", grid=(F // tn, T // tm),
        out_shape=jax.ShapeDtypeStruct((T, F), BF16),
        in_specs=[cur, prev, nxt, cur, wspec, pl.BlockSpec((1, tn), lambda j, i: (0, j))],
        out_specs=cur,
        compiler_params=_params(("parallel", "parallel"), est),
    )(g, g, g, u, conv_w, conv_b)


def _conv_glu_bwd(dact, g, u, conv_w, conv_b, *, tm=256, tn=1408):
    T, F = g.shape
    tm, tn = min(tm, T), min(tn, F)
    cur, prev, nxt = _halo_specs(tm, tn, T)
    te = tm + 2 * HALO

    def body(d_ref, dp_ref, dn_ref, g_ref, gp_ref, gn_ref, u_ref, up_ref, un_ref, w_ref, b_ref,
             dg_ref, du_ref, dc_ref):
        i = pl.program_id(1)

        @pl.when(i == 0)
        def _():
            dc_ref[...] = jnp.zeros_like(dc_ref)

        grow = i * tm - HALO + lax.broadcasted_iota(jnp.int32, (te, tn), 0)
        valid = (grow >= 0) & (grow < T)
        ge = jnp.where(valid, jnp.concatenate([gp_ref[...], g_ref[...], gn_ref[...]], axis=0), 0.0)
        ue = jnp.concatenate([up_ref[...], u_ref[...], un_ref[...]], axis=0).astype(F32)
        de = jnp.concatenate([dp_ref[...], d_ref[...], dn_ref[...]], axis=0).astype(F32)
        w0, w1, w2 = w_ref[0:1, :], w_ref[1:2, :], w_ref[2:3, :]
        gm1 = pltpu.roll(ge, 1, 0)
        gp1 = pltpu.roll(ge, te - 1, 0)
        gc = ((b_ref[...] + gm1 * w0) + ge * w1) + gp1 * w2
        cdf, t = _gelu_parts(gc)
        dgelu = cdf + 0.5 * gc * (1.0 - t * t) * (GELU_C * (1.0 + 3.0 * GELU_A * (gc * gc)))
        dgc = jnp.where(valid, de * ue * dgelu, 0.0)
        dge = w0 * pltpu.roll(dgc, te - 1, 0) + w1 * dgc + w2 * pltpu.roll(dgc, 1, 0)
        mid = slice(HALO, HALO + tm)
        dg_ref[...] = dge[mid].astype(BF16)
        du_ref[...] = (de[mid] * (gc[mid] * cdf[mid])).astype(BF16)
        dgm = dgc[mid]
        dc_ref[0:1, :] += jnp.sum(dgm * gm1[mid], axis=0, keepdims=True)
        dc_ref[1:2, :] += jnp.sum(dgm * ge[mid], axis=0, keepdims=True)
        dc_ref[2:3, :] += jnp.sum(dgm * gp1[mid], axis=0, keepdims=True)
        dc_ref[3:4, :] += jnp.sum(dgm, axis=0, keepdims=True)

    wspec = pl.BlockSpec((8, tn), lambda j, i: (0, j))
    est = 2 * (3 * _nbytes((tm, tn), F32) + 2 * _nbytes((tm, tn), BF16)) + 16 * _nbytes((te, tn), F32)
    return pl.pallas_call(
        body, name="conv_glu_bwd", grid=(F // tn, T // tm),
        out_shape=(jax.ShapeDtypeStruct((T, F), BF16), jax.ShapeDtypeStruct((T, F), BF16),
                   jax.ShapeDtypeStruct((8, F), F32)),
        in_specs=[cur, prev, nxt, cur, prev, nxt, cur, prev, nxt, wspec, pl.BlockSpec((1, tn), lambda j, i: (0, j))],
        out_specs=(cur, cur, wspec),
        compiler_params=_params(("parallel", "arbitrary"), est),
    )(dact, dact, dact, g, g, g, u, u, u, conv_w, conv_b)


def _adamw_math(w, g, m, v):
    m = ADAM_B1 * m + (1.0 - ADAM_B1) * g
    v = ADAM_B2 * v + (1.0 - ADAM_B2) * (g * g)
    m_hat = m / (1.0 - ADAM_B1 ** ADAM_STEP)
    v_hat = v / (1.0 - ADAM_B2 ** ADAM_STEP)
    delta = -ADAM_LR * (m_hat / (jnp.sqrt(v_hat) + ADAM_EPS) + ADAM_WD * w)
    return delta, m, v


def _adamw(w, m, v, gparts, *, name, tr):
    R, C = w.shape
    tr = min(tr, R)
    assert R % tr == 0

    def body(w_ref, m_ref, v_ref, gp_ref, g_ref, d_ref, nm_ref, nv_ref):
        g = gp_ref[0].astype(F32)
        for j in range(1, N_DEV):
            g = g + gp_ref[j].astype(F32)
        delta, nm, nv = _adamw_math(w_ref[...], g, m_ref[...], v_ref[...])
        g_ref[...] = g
        d_ref[...] = delta
        nm_ref[...] = nm
        nv_ref[...] = nv

    blk = pl.BlockSpec((tr, C), lambda i: (i, 0))
    out = jax.ShapeDtypeStruct((R, C), F32)
    return pl.pallas_call(
        body, name=name, grid=(R // tr,), out_shape=(out, out, out, out),
        in_specs=[blk, blk, blk, pl.BlockSpec((N_DEV, tr, C), lambda i: (0, i, 0))],
        out_specs=(blk, blk, blk, blk),
        compiler_params=_params(("parallel",), 24 << 20),
    )(w, m, v, gparts)


def _rope_tables(T):
    rows_n = T // GRID_W
    row = jnp.repeat(jnp.arange(rows_n, dtype=F32), GRID_W)
    col = jnp.tile(jnp.arange(GRID_W, dtype=F32), rows_n)
    half = HEAD_DIM // 2
    inv_freq = ROPE_THETA ** (-jnp.arange(0, half, 2, dtype=F32) / half)
    ang = jnp.concatenate([row[:, None] * inv_freq, col[:, None] * inv_freq], axis=-1)
    cos, sin = jnp.cos(ang), jnp.sin(ang)
    cos64 = jnp.repeat(cos, 2, axis=-1)
    sin64 = jnp.stack([-sin, sin], axis=-1).reshape(T, HEAD_DIM)
    return jnp.tile(cos64, (1, 2)), jnp.tile(sin64, (1, 2))


def _t5_bucket(rel):
    half = N_BUCKETS // 2
    max_exact = half // 2
    bucket = jnp.where(rel > 0, half, 0)
    rp = jnp.abs(rel)
    rpf = jnp.maximum(rp, 1).astype(F32)
    large = max_exact + (jnp.log(rpf / max_exact) / math.log(MAX_DISTANCE / max_exact)
                         * (half - max_exact)).astype(jnp.int32)
    large = jnp.minimum(large, half - 1)
    return bucket + jnp.where(rp < max_exact, rp, large)


def _window_buckets():
    qpos = jnp.arange(WB, dtype=jnp.int32)
    kpos = jnp.arange(WK, dtype=jnp.int32) - WB
    rel = kpos[None, :] - qpos[:, None]
    return jnp.where(jnp.abs(rel) <= WINDOW, _t5_bucket(rel), -1)


def _heads_first(a, nh):
    T = a.shape[0]
    return a.reshape(T, nh, HEAD_DIM).transpose(1, 0, 2)


def _heads_last(a):
    nh, T, _ = a.shape
    return a.transpose(1, 0, 2).reshape(T, nh * HEAD_DIM)


def _pad_keys(a):
    return jnp.pad(a, ((0, 0), (WB, WB), (0, 0)))


def _row(v):
    return v.reshape(1, -1)


def _rows8(rows, width):
    a = jnp.stack(list(rows), axis=0)
    return jnp.pad(a, ((0, 8 - a.shape[0]), (0, 0)))


def _layer_fwd(l, xin, W, tabs):
    xhat, xg, xb, x16 = xin
    T = xhat.shape[0]
    cos2, sin2, bias, _ = tabs
    h = _mm([x16], [W["w_in"][l]], name="mm_in", out_dtype=F32, tm=512, tn=IN_COLS, tk=D_MODEL)
    gains = _rows8([jnp.tile(W["q_norm"][l], 2), jnp.tile(W["k_norm"][l], 2)], LANES)
    roped = _qk_rope_fwd(h, gains, cos2, sin2, name="qk_rope_fwd")
    qa = _heads_first(roped[:, :QW], KV * GQ).reshape(KV, GQ * T, HEAD_DIM)
    ka = _heads_first(roped[:, QW:], KV)
    va = _heads_first(h[:, ROPE_W:ROPE_W + KW].astype(BF16), KV)
    kaT, vaT = ka.transpose(0, 2, 1), va.transpose(0, 2, 1)
    qaT = qa.transpose(0, 2, 1)
    oaT, lse_a = _attn_a_fwd(ka, qaT, jnp.concatenate([vaT, jnp.ones((KV, ONES_ROWS, T), BF16)], axis=1))
    o0 = ROPE_W + KW
    qb = _heads_first((h[:, o0:o0 + QW] * SCALE).astype(BF16), KV * GQ).reshape(KV, GQ, T, HEAD_DIM)
    kb = _pad_keys(_heads_first(h[:, o0 + QW:o0 + QW + KW].astype(BF16), KV))
    vb = _pad_keys(_heads_first(h[:, o0 + QW + KW:].astype(BF16), KV))
    kbT, vbT = kb.transpose(0, 2, 1), vb.transpose(0, 2, 1)
    sink_rows = jnp.repeat(W["sink"][l], WB).reshape(KV, GQ * WB, 1)
    ob, lse_b = _attn_b_fwd(qb, kbT, vb, bias, sink_rows)
    oa_t = oaT.reshape(KV, HEAD_DIM, GQ, T).transpose(3, 0, 2, 1).reshape(T, QW)
    ob_t = _heads_last(ob.reshape(KV * GQ, T, HEAD_DIM))
    ga, gb = _row(W["out_norm_a"][l]), _row(W["out_norm_b"][l])
    ycat = _outnorm_fwd(oa_t, ob_t, ga, gb)
    g1, b1 = _row(W["ln1_g"][l]), _row(W["ln1_b"][l])
    x1hat, rstd1, x1_16 = _mm_res_ln(ycat, W["w_out"][l], xhat, xg, xb, g1, b1, name="mm_out_ln", tm=512)
    gate = _mm([x1_16], [W["w_gate"][l]], name="mm_gate", out_dtype=F32, tm=512, tn=D_FF // 2, tk=D_MODEL)
    up = _mm([x1_16], [W["w_up"][l]], name="mm_up", out_dtype=BF16, tm=512, tn=D_FF // 2, tk=D_MODEL)
    cw = jnp.pad(W["conv_w"][l], ((0, 5), (0, 0)))
    cb = _row(W["conv_b"][l])
    act = _conv_glu_fwd(gate, up, cw, cb)
    g2, b2 = _row(W["ln2_g"][l]), _row(W["ln2_b"][l])
    x2hat, rstd2, x2_16 = _mm_res_ln(act, W["w_down"][l], x1hat, g1, b1, g2, b2, name="mm_down_ln", tm=256)
    saved = dict(x16=x16, h=h, gains=gains, qa=qa, qaT=qaT, ka=ka, kaT=kaT, va=va, oaT=oaT, lse_a=lse_a,
                 qb=qb, kb=kb, kbT=kbT, vbT=vbT, ob=ob, lse_b=lse_b, sink_rows=sink_rows, oa_t=oa_t, ob_t=ob_t,
                 ga=ga, gb=gb, ycat=ycat, x1hat=x1hat, rstd1=rstd1, x1_16=x1_16, g1=g1, b1=b1, gate=gate, up=up,
                 cw=cw, cb=cb, act=act, x2hat=x2hat, rstd2=rstd2, g2=g2, b2=b2)
    return (x2hat, g2, b2, x2_16), saved


def _layer_bwd(l, S, W, WT, tabs, dz2, dz2_16, stats2):
    cos2, sin2, bias, _ = tabs
    T = dz2.shape[0]
    G = {}
    G["ln2_g"], G["ln2_b"] = stats2[0], stats2[1]
    G["w_down"] = _mm([S["act"]], [dz2_16], name="dw_down", out_dtype=BF16, trans_a=True, tm=D_FF // 2, tn=D_MODEL, tk=512)
    dact = _mm([dz2_16], [WT["w_down"][l]], name="mm_dact", out_dtype=BF16, tm=512, tn=D_FF // 2, tk=D_MODEL)
    dg, du, dconv = _conv_glu_bwd(dact, S["gate"], S["up"], S["cw"], S["cb"])
    G["conv_w"], G["conv_b"] = dconv[0:3], dconv[3]
    G["w_gate"] = _mm([S["x1_16"]], [dg], name="dw_gate", out_dtype=BF16, trans_a=True, tm=D_MODEL, tn=D_FF // 2, tk=512)
    G["w_up"] = _mm([S["x1_16"]], [du], name="dw_up", out_dtype=BF16, trans_a=True, tm=D_MODEL, tn=D_FF // 2, tk=512)
    dx1 = _mm([dg, du], [WT["w_gate"][l], WT["w_up"][l]], name="mm_dx1", out_dtype=F32, tm=512, tn=D_MODEL,
              tk=D_FF // 2, add=dz2, add_scale=ALPHA)
    dz1, dz1_16, stats1 = _ln_bwd(S["x1hat"], S["rstd1"], S["g1"], S["b1"], name="ln1_bwd", dx=dx1)
    G["ln1_g"], G["ln1_b"] = stats1[0], stats1[1]
    G["w_out"] = _mm([S["ycat"]], [dz1_16], name="dw_out", out_dtype=BF16, trans_a=True, tm=D_MODEL, tn=D_MODEL, tk=512)
    dycat = _mm([dz1_16], [WT["w_out"][l]], name="mm_dycat", out_dtype=F32, tm=512, tn=D_MODEL, tk=D_MODEL)
    doa_t, dob_t, dgn = _outnorm_bwd(dycat, S["oa_t"], S["ob_t"], S["ga"], S["gb"])
    G["out_norm_a"], G["out_norm_b"] = dgn[0], dgn[1]
    doa = _heads_first(doa_t, KV * GQ).reshape(KV, GQ * T, HEAD_DIM)
    dob = _heads_first(dob_t, KV * GQ).reshape(KV, GQ, T, HEAD_DIM)
    doaT = doa.transpose(0, 2, 1)
    delta = _attn_a_delta(S["oaT"], doaT)
    dkaT, dvaT, dqaT = _attn_a_bwd(S["ka"], S["va"], S["kaT"], S["qaT"], doaT.astype(BF16), S["lse_a"], delta)
    dqa_t = dqaT.reshape(KV, HEAD_DIM, GQ, T).transpose(3, 0, 2, 1).reshape(T, QW)
    d_roped = jnp.concatenate([dqa_t, dkaT.transpose(2, 0, 1).reshape(T, KW)], axis=1)
    dh_rope, dgain = _qk_rope_bwd(S["h"], d_roped, S["gains"], cos2, sin2, name="qk_rope_bwd")
    G["q_norm"], G["k_norm"] = dgain[0, :HEAD_DIM], dgain[1, :HEAD_DIM]
    dqb, dkb, dvb, dbias, dsk = _attn_b_bwd(S["qb"], S["kbT"], S["kb"], S["vbT"], dob, S["ob"], S["lse_b"], bias,
                                            S["sink_rows"])
    dh = jnp.concatenate([
        dh_rope, dvaT.transpose(2, 0, 1).reshape(T, KW).astype(BF16),
        (_heads_last(dqb.reshape(KV * GQ, T, HEAD_DIM)) * SCALE).astype(BF16),
        _heads_last(dkb[:, WB:WB + T]).astype(BF16), _heads_last(dvb[:, WB:WB + T]).astype(BF16)], axis=1)
    G["w_in"] = _mm([S["x16"]], [dh], name="dw_in", out_dtype=BF16, trans_a=True, tm=D_MODEL, tn=IN_COLS, tk=512)
    dxin = _mm([dh], [WT["w_in"][l]], name="mm_dxin", out_dtype=F32, tm=512, tn=D_MODEL, tk=IN_COLS,
               add=dz1, add_scale=ALPHA)
    return dxin, G, dbias.reshape(KV * GQ, WB * WK), dsk.reshape(KV * GQ, WB)


BIG = ("w_in", "w_out", "w_gate", "w_up", "w_down")
COL_SHARDED = ("w_in", "w_gate", "w_up")
WIRE_COLS = 1024


def _unshard(name, gathered, shard_shape):
    L, r, c = shard_shape
    blocks = gathered.reshape(N_DEV, L, r, c)
    if name in COL_SHARDED:
        return blocks.transpose(1, 2, 0, 3).reshape(L, r, N_DEV * c)
    return blocks.transpose(1, 0, 2, 3).reshape(L, N_DEV * r, c)


def _to_owner_blocks(name, full, shard_shape):
    L, r, c = shard_shape
    if name in COL_SHARDED:
        blocks = full.reshape(L, r, N_DEV, c).transpose(2, 0, 1, 3)
    else:
        blocks = full.reshape(L, N_DEV, r, c).transpose(1, 0, 2, 3)
    return blocks.reshape(N_DEV, -1, WIRE_COLS)


def _pack_small(vals):
    flat = jnp.concatenate([vals[n].reshape(-1).astype(F32) for n in SMALL_NAMES])
    pad = (-flat.shape[0]) % (8 * LANES)
    return jnp.pad(flat, (0, pad)).reshape(-1, LANES)


def _unpack_small(packed, shapes):
    flat = packed.reshape(-1)
    out, off = {}, 0
    for n in SMALL_NAMES:
        size = math.prod(shapes[n])
        out[n] = flat[off:off + size].reshape(shapes[n])
        off += size
    return out


def kernel(x, rel_bias, w_in, q_norm, k_norm, sink, out_norm_a, out_norm_b, w_out, ln1_g, ln1_b, w_gate, w_up, conv_w, conv_b, w_down, ln2_g, ln2_b, loss_target, m_rel_bias, m_w_in, m_q_norm, m_k_norm, m_sink, m_out_norm_a, m_out_norm_b, m_w_out, m_ln1_g, m_ln1_b, m_w_gate, m_w_up, m_conv_w, m_conv_b, m_w_down, m_ln2_g, m_ln2_b, v_rel_bias, v_w_in, v_q_norm, v_k_norm, v_sink, v_out_norm_a, v_out_norm_b, v_w_out, v_ln1_g, v_ln1_b, v_w_gate, v_w_up, v_conv_w, v_conv_b, v_w_down, v_ln2_g, v_ln2_b):
    P = dict(rel_bias=rel_bias, w_in=w_in, q_norm=q_norm, k_norm=k_norm, sink=sink, out_norm_a=out_norm_a,
             out_norm_b=out_norm_b, w_out=w_out, ln1_g=ln1_g, ln1_b=ln1_b, w_gate=w_gate, w_up=w_up, conv_w=conv_w,
             conv_b=conv_b, w_down=w_down, ln2_g=ln2_g, ln2_b=ln2_b)
    M = dict(rel_bias=m_rel_bias, w_in=m_w_in, q_norm=m_q_norm, k_norm=m_k_norm, sink=m_sink, out_norm_a=m_out_norm_a,
             out_norm_b=m_out_norm_b, w_out=m_w_out, ln1_g=m_ln1_g, ln1_b=m_ln1_b, w_gate=m_w_gate, w_up=m_w_up,
             conv_w=m_conv_w, conv_b=m_conv_b, w_down=m_w_down, ln2_g=m_ln2_g, ln2_b=m_ln2_b)
    V = dict(rel_bias=v_rel_bias, w_in=v_w_in, q_norm=v_q_norm, k_norm=v_k_norm, sink=v_sink, out_norm_a=v_out_norm_a,
             out_norm_b=v_out_norm_b, w_out=v_w_out, ln1_g=v_ln1_g, ln1_b=v_ln1_b, w_gate=v_w_gate, w_up=v_w_up,
             conv_w=v_conv_w, conv_b=v_conv_b, w_down=v_w_down, ln2_g=v_ln2_g, ln2_b=v_ln2_b)
    names = list(P)
    T = x.shape[1]
    me = 4 * lax.axis_index("x") + 2 * lax.axis_index("y") + lax.axis_index("c")

    cw_shard = conv_w.reshape(-1)
    cw_pad = (-cw_shard.shape[0]) % LANES
    wire = [P[n].astype(BF16).reshape(-1, WIRE_COLS) for n in BIG] + [jnp.pad(cw_shard, (0, cw_pad)).reshape(-1, LANES)]
    gathered = _exchange(wire, gather=True, name="gather_weights")
    W = {n: _unshard(n, gathered[i], P[n].shape) for i, n in enumerate(BIG)}
    L, taps, fc = conv_w.shape
    cw_all = gathered[-1].reshape(N_DEV, -1)[:, :cw_shard.shape[0]].reshape(N_DEV, L, taps, fc)
    W["conv_w"] = cw_all.transpose(1, 2, 0, 3).reshape(L, taps, N_DEV * fc)
    for n in names:
        if n not in W:
            W[n] = P[n]
    WT = {n: W[n].transpose(0, 2, 1) for n in BIG}

    cos2, sin2 = _rope_tables(T)
    bucket = _window_buckets()
    bias = _bias_table(rel_bias.T, bucket.reshape(1, WB * WK)).reshape(KV, GQ * WB, WK)
    tabs = (cos2, sin2, bias, bucket)

    ones, zeros = jnp.ones((1, D_MODEL), F32), jnp.zeros((1, D_MODEL), F32)
    cur = (x[0], ones, zeros, x[0].astype(BF16))
    saved = []
    for l in range(DEPTH):
        cur, S = _layer_fwd(l, cur, W, tabs)
        saved.append(S)

    grads = [None] * DEPTH
    dbs, dsks = [None] * DEPTH, [None] * DEPTH
    S = saved[-1]
    dz, dz16, stats = _ln_bwd(S["x2hat"], S["rstd2"], S["g2"], S["b2"], name="loss_ln2_bwd", target=loss_target[0])
    loss = lax.psum(stats[2, 0], ("x", "y", "c"))
    for l in reversed(range(DEPTH)):
        S = saved[l]
        dxin, grads[l], dbs[l], dsks[l] = _layer_bwd(l, S, W, WT, tabs, dz, dz16, stats)
        if l > 0:
            Sp = saved[l - 1]
            dz, dz16, stats = _ln_bwd(Sp["x2hat"], Sp["rstd2"], Sp["g2"], Sp["b2"], name="ln2_bwd", dx=dxin)
    grad_x = dxin[None]

    drb, dsink = _bias_sink_grads(dbs, dsks, bucket.reshape(1, WB * WK))
    small_g = {n: jnp.stack([grads[l][n] for l in range(DEPTH)]) for n in SMALL_NAMES if n not in ("rel_bias", "sink")}
    small_g["rel_bias"] = drb.T
    small_g["sink"] = dsink.reshape(DEPTH, KV * GQ)

    big_parts = [_to_owner_blocks(n, jnp.stack([grads[l][n] for l in range(DEPTH)]), P[n].shape) for n in BIG]
    recv = _exchange(big_parts, gather=False, name="scatter_grads")
    small_recv = _exchange([_pack_small(small_g)], gather=True, name="gather_small_grads")[0]

    out_g, out_d, out_m, out_v = {}, {}, {}, {}
    for i, n in enumerate(BIG):
        shp = P[n].shape
        res = _adamw(P[n].reshape(-1, WIRE_COLS), M[n].reshape(-1, WIRE_COLS), V[n].reshape(-1, WIRE_COLS), recv[i],
                     name="adamw_" + n, tr=64)
        out_g[n], out_d[n], out_m[n], out_v[n] = (r.reshape(shp) for r in res)
    full_shapes = {n: W[n].shape for n in SMALL_NAMES}

    def small_state(D):
        vals = {n: D[n] for n in SMALL_NAMES if n != "conv_w"}
        cw = jnp.zeros((L, taps, N_DEV, fc), F32)
        cw = lax.dynamic_update_slice(cw, D["conv_w"].reshape(L, taps, 1, fc), (0, 0, me, 0))
        vals["conv_w"] = cw.reshape(L, taps, N_DEV * fc)
        return _pack_small(vals)

    sw, sm, sv = small_state(P), small_state(M), small_state(V)
    res = _adamw(sw, sm, sv, small_recv, name="adamw_small", tr=sw.shape[0])
    for dst, packed in zip((out_g, out_d, out_m, out_v), res):
        vals = _unpack_small(packed, full_shapes)
        for n in SMALL_NAMES:
            if n == "conv_w":
                sl = lax.dynamic_slice(vals[n].reshape(L, taps, N_DEV, fc), (0, 0, me, 0), (L, taps, 1, fc))
                dst[n] = sl.reshape(L, taps, fc)
            else:
                dst[n] = vals[n]
    return (loss, grad_x, *[out_g[n] for n in names], *[out_d[n] for n in names],
            *[out_m[n] for n in names], *[out_v[n] for n in names])
```

```python
import functools
import math

import jax
import jax.numpy as jnp
from jax import lax
from jax.experimental import pallas as pl
from jax.experimental.pallas import tpu as pltpu

F32 = jnp.float32
BF16 = jnp.bfloat16
MESH = pl.DeviceIdType.MESH

N_DEV = 8
D_MODEL = 1024
DEPTH = 2
HEAD_DIM = 64
KV = 2
GQ = 4
QW = KV * GQ * HEAD_DIM
KW = KV * HEAD_DIM
ROPE_W = QW + KW
IN_COLS = 2 * (QW + 2 * KW)
D_FF = 2816
GRID_W = 64
ROPE_THETA = 10000.0
WINDOW = 128
N_BUCKETS = 32
MAX_DISTANCE = 128
ALPHA = (2.0 * DEPTH) ** 0.25
RMS_EPS = 1e-6
LN_EPS = 1e-5
SCALE = HEAD_DIM ** -0.5
LOG2E = math.log2(math.e)
LN2 = math.log(2.0)
NEG = -1e30
ONES_ROWS = 16

ADAM_LR = 0.001
ADAM_B1 = 0.9
ADAM_B2 = 0.999
ADAM_EPS = 1e-08
ADAM_WD = 0.01
ADAM_STEP = 10

LANES = 128
VMEM_CAP = 60 * 1024 * 1024
SMALL_NAMES = ("rel_bias", "q_norm", "k_norm", "sink", "out_norm_a", "out_norm_b", "ln1_g", "ln1_b",
               "conv_b", "ln2_g", "ln2_b", "conv_w")


def _params(sem, est_bytes):
    limit = int(min(VMEM_CAP, est_bytes + (8 << 20)))
    return pltpu.CompilerParams(dimension_semantics=sem, vmem_limit_bytes=limit)


def _nbytes(shape, dtype):
    return math.prod(shape) * jnp.dtype(dtype).itemsize


class _Comm:
    def __init__(self, parts, gathers):
        self.parts, self.gathers, self.n = list(parts), list(gathers), len(parts)
        hbm = pl.BlockSpec(memory_space=pltpu.HBM)
        self.in_specs = [hbm] * self.n
        self.out_specs = [hbm] * self.n
        self.out_shape = [jax.ShapeDtypeStruct((N_DEV,) + tuple(p.shape if g else p.shape[1:]), p.dtype)
                          for p, g in zip(self.parts, self.gathers)]
        self.scratch = [pltpu.SemaphoreType.DMA((self.n * (N_DEV - 1),)), pltpu.SemaphoreType.DMA((self.n * (N_DEV - 1),)),
                        pltpu.SemaphoreType.DMA((self.n,))]

    def bind(self, ins, outs, sems):
        send_sems, recv_sems, local_sems = sems
        gathers, n = self.gathers, self.n
        me = 4 * lax.axis_index("x") + 2 * lax.axis_index("y") + lax.axis_index("c")

        def src(k, j):
            return ins[k] if gathers[k] else ins[k].at[j]

        def copy(k, d, peer, lands_in):
            return pltpu.make_async_remote_copy(
                src_ref=src(k, peer), dst_ref=outs[k].at[lands_in],
                send_sem=send_sems.at[k * (N_DEV - 1) + d - 1], recv_sem=recv_sems.at[k * (N_DEV - 1) + d - 1],
                device_id=(peer // 4, lax.rem(peer // 2, 2), lax.rem(peer, 2)), device_id_type=MESH)

        def send(k, d):
            return copy(k, d, lax.rem(me + d, N_DEV), me)

        def arrival(k, d):
            frm = lax.rem(me + N_DEV - d, N_DEV)
            return copy(k, d, frm, frm)

        def local(k):
            return pltpu.make_async_copy(src(k, me), outs[k].at[me], local_sems.at[k])

        def start():
            for k in range(n):
                local(k).start()
                for d in range(1, N_DEV):
                    send(k, d).start()

        def finish():
            for k in range(n):
                for d in range(1, N_DEV):
                    arrival(k, d).wait_recv()
            for k in range(n):
                for d in range(1, N_DEV):
                    send(k, d).wait_send()
                local(k).wait()

        return start, finish


def _host_comm(comm, refs, n_in, n_out, n_scratch, grid):
    n = comm.n if comm is not None else 0
    own_in, cin = refs[:n_in], refs[n_in:n_in + n]
    own_out, cout = refs[n_in + n:n_in + n + n_out], refs[n_in + n + n_out:n_in + 2 * n + n_out]
    base = n_in + 2 * n + n_out
    own_scratch, sems = refs[base:base + n_scratch], refs[base + n_scratch:]
    own = tuple(own_in) + tuple(own_out) + tuple(own_scratch)
    if comm is None:
        return own, lambda: None, lambda: None
    start, finish = comm.bind(cin, cout, sems)
    first = last = None
    for ax, size in enumerate(grid):
        pid = pl.program_id(ax)
        first = (pid == 0) if first is None else first & (pid == 0)
        last = (pid == size - 1) if last is None else last & (pid == size - 1)
    return own, lambda: pl.when(first)(start), lambda: pl.when(last)(finish)


def _exchange(parts, gathers, name):
    comm = _Comm(parts, gathers)
    n = comm.n

    def body(*refs):
        start, finish = comm.bind(refs[:n], refs[n:2 * n], refs[2 * n:])
        start()
        finish()

    return pl.pallas_call(body, name=name, out_shape=comm.out_shape, in_specs=comm.in_specs, out_specs=comm.out_specs,
                          scratch_shapes=comm.scratch)(*comm.parts)


def _mm(a_list, b_list, *, name, out_dtype, tm, tn, tk, trans_a=False, add=None, add_scale=1.0):
    na = len(a_list)
    if trans_a:
        K, M = a_list[0].shape
    else:
        M, K = a_list[0].shape
    N = b_list[0].shape[1]
    tm, tn, tk = min(tm, M), min(tn, N), min(tk, K)
    assert M % tm == 0 and N % tn == 0 and K % tk == 0, (name, M, N, K, tm, tn, tk)
    nk = K // tk
    dims = (((0,), (0,)), ((), ())) if trans_a else (((1,), (0,)), ((), ()))

    def body(*refs):
        a_refs, b_refs = refs[:na], refs[na:2 * na]
        add_ref = refs[2 * na] if add is not None else None
        o_ref, acc_ref = refs[-2], refs[-1]
        k = pl.program_id(2)

        @pl.when(k == 0)
        def _():
            acc_ref[...] = jnp.zeros_like(acc_ref)

        part = None
        for a_ref, b_ref in zip(a_refs, b_refs):
            prod = lax.dot_general(a_ref[...].astype(BF16), b_ref[...].astype(BF16), dims,
                                   preferred_element_type=F32)
            part = prod if part is None else part + prod
        acc_ref[...] += part

        @pl.when(k == nk - 1)
        def _():
            res = acc_ref[...]
            if add_ref is not None:
                res = res + add_scale * add_ref[...]
            o_ref[...] = res.astype(o_ref.dtype)

    if trans_a:
        a_spec = pl.BlockSpec((tk, tm), lambda i, j, k: (k, i))
    else:
        a_spec = pl.BlockSpec((tm, tk), lambda i, j, k: (i, k))
    b_spec = pl.BlockSpec((tk, tn), lambda i, j, k: (k, j))
    o_spec = pl.BlockSpec((tm, tn), lambda i, j, k: (i, j))
    in_specs = [a_spec] * na + [b_spec] * na + ([o_spec] if add is not None else [])
    est = (2 * na * (_nbytes((tm, tk), a_list[0].dtype) + _nbytes((tk, tn), b_list[0].dtype))
           + na * (_nbytes((tm, tk), BF16) + _nbytes((tk, tn), BF16))
           + 2 * _nbytes((tm, tn), out_dtype) + 3 * _nbytes((tm, tn), F32)
           + (2 * _nbytes((tm, tn), F32) if add is not None else 0))
    args = list(a_list) + list(b_list) + ([add] if add is not None else [])
    return pl.pallas_call(
        body, name=name, grid=(M // tm, N // tn, nk),
        out_shape=jax.ShapeDtypeStruct((M, N), out_dtype),
        in_specs=in_specs, out_specs=o_spec,
        scratch_shapes=[pltpu.VMEM((tm, tn), F32)],
        compiler_params=_params(("parallel", "parallel", "arbitrary"), est),
    )(*args)


def _mm_res_ln(a, w, res_hat, res_g, res_b, ln_g, ln_b, *, name, tm):
    T, K = a.shape
    D = w.shape[1]
    tm = min(tm, T)

    def body(a_ref, w_ref, rh_ref, rg_ref, rb_ref, g_ref, b_ref, xhat_ref, rstd_ref, xb_ref):
        branch = jnp.dot(a_ref[...].astype(BF16), w_ref[...], preferred_element_type=F32)
        z = ALPHA * (rh_ref[...] * rg_ref[...] + rb_ref[...]) + branch
        mu = jnp.mean(z, axis=1, keepdims=True)
        zc = z - mu
        var = jnp.mean(zc * zc, axis=1, keepdims=True)
        rstd = lax.rsqrt(var + LN_EPS)
        xhat = zc * rstd
        xhat_ref[...] = xhat
        rstd_ref[...] = rstd
        xb_ref[...] = (xhat * g_ref[...] + b_ref[...]).astype(BF16)

    row = pl.BlockSpec((tm, D), lambda i: (i, 0))
    vec = pl.BlockSpec((1, D), lambda i: (0, 0))
    est = (2 * (_nbytes((tm, K), a.dtype) + _nbytes((K, D), BF16)) + 4 * _nbytes((tm, D), F32) * 2
           + 6 * _nbytes((tm, D), F32))
    return pl.pallas_call(
        body, name=name, grid=(T // tm,),
        out_shape=(jax.ShapeDtypeStruct((T, D), F32), jax.ShapeDtypeStruct((T, 1), F32),
                   jax.ShapeDtypeStruct((T, D), BF16)),
        in_specs=[pl.BlockSpec((tm, K), lambda i: (i, 0)), pl.BlockSpec((K, D), lambda i: (0, 0)), row, vec, vec, vec, vec],
        out_specs=(row, pl.BlockSpec((tm, 1), lambda i: (i, 0)), row),
        compiler_params=_params(("parallel",), est),
    )(a, w, res_hat, res_g, res_b, ln_g, ln_b)


def _ln_bwd(xhat, rstd, ln_g, ln_b, *, name, dx=None, target=None, tm=256):
    T, D = xhat.shape
    tm = min(tm, T)
    head = target is not None

    def body(xhat_ref, rstd_ref, g_ref, b_ref, d_ref, dz_ref, dzb_ref, st_ref):
        i = pl.program_id(0)

        @pl.when(i == 0)
        def _():
            st_ref[...] = jnp.zeros_like(st_ref)

        xh = xhat_ref[...]
        g = g_ref[...]
        if head:
            err = (xh * g + b_ref[...]) - d_ref[...]
            dxv = err * (1.0 / D)
            st_ref[2:3, :] += 0.5 * jnp.sum(jnp.sum(err * err, axis=1, keepdims=True) * (1.0 / D), axis=0, keepdims=True)
        else:
            dxv = d_ref[...]
        st_ref[0:1, :] += jnp.sum(dxv * xh, axis=0, keepdims=True)
        st_ref[1:2, :] += jnp.sum(dxv, axis=0, keepdims=True)
        dxh = dxv * g
        m1 = jnp.mean(dxh, axis=1, keepdims=True)
        m2 = jnp.mean(dxh * xh, axis=1, keepdims=True)
        dz = rstd_ref[...] * (dxh - m1 - xh * m2)
        dz_ref[...] = dz
        dzb_ref[...] = dz.astype(BF16)

    row = pl.BlockSpec((tm, D), lambda i: (i, 0))
    vec = pl.BlockSpec((1, D), lambda i: (0, 0))
    est = 2 * 4 * _nbytes((tm, D), F32) + 6 * _nbytes((tm, D), F32)
    return pl.pallas_call(
        body, name=name, grid=(T // tm,),
        out_shape=(jax.ShapeDtypeStruct((T, D), F32), jax.ShapeDtypeStruct((T, D), BF16),
                   jax.ShapeDtypeStruct((8, D), F32)),
        in_specs=[row, pl.BlockSpec((tm, 1), lambda i: (i, 0)), vec, vec, row],
        out_specs=(row, row, pl.BlockSpec((8, D), lambda i: (0, 0))),
        compiler_params=_params(("arbitrary",), est),
    )(xhat, rstd, ln_g, ln_b, target if head else dx)


def _pair_swap(v, even):
    return jnp.where(even, pltpu.roll(v, LANES - 1, 1), pltpu.roll(v, 1, 1))


def _half_sums(v, lo):
    s_lo = jnp.sum(jnp.where(lo, v, 0.0), axis=1, keepdims=True)
    s_hi = jnp.sum(jnp.where(lo, 0.0, v), axis=1, keepdims=True)
    return jnp.where(lo, s_lo, s_hi)


def _qk_rope_fwd(h, gains, cos2, sin2, *, name, tm=256):
    T = h.shape[0]
    tm = min(tm, T)
    nch = ROPE_W // LANES

    def body(h_ref, g_ref, c_ref, s_ref, o_ref):
        lane = lax.broadcasted_iota(jnp.int32, (tm, LANES), 1)
        lo, even = lane < HEAD_DIM, lane % 2 == 0
        c, s = c_ref[...], s_ref[...]
        for j in range(nch):
            x = h_ref[:, j * LANES:(j + 1) * LANES]
            isq = j < QW // LANES
            g = g_ref[0:1, :] if isq else g_ref[1:2, :]
            r = lax.rsqrt(_half_sums(x * x, lo) * (1.0 / HEAD_DIM) + RMS_EPS)
            nrm = x * r * g
            out = nrm * c + _pair_swap(nrm, even) * s
            if isq:
                out = out * (SCALE * LOG2E)
            o_ref[:, j * LANES:(j + 1) * LANES] = out.astype(BF16)

    est = 2 * (_nbytes((tm, ROPE_W), F32) + _nbytes((tm, ROPE_W), BF16) + 2 * _nbytes((tm, LANES), F32)) + (4 << 20)
    return pl.pallas_call(
        body, name=name, grid=(T // tm,),
        out_shape=jax.ShapeDtypeStruct((T, ROPE_W), BF16),
        in_specs=[pl.BlockSpec((tm, ROPE_W), lambda i: (i, 0)), pl.BlockSpec((8, LANES), lambda i: (0, 0)),
                  pl.BlockSpec((tm, LANES), lambda i: (i, 0)), pl.BlockSpec((tm, LANES), lambda i: (i, 0))],
        out_specs=pl.BlockSpec((tm, ROPE_W), lambda i: (i, 0)),
        compiler_params=_params(("parallel",), est),
    )(h, gains, cos2, sin2)


def _qk_rope_bwd(h, d_out, gains, cos2, sin2, *, name, tm=256):
    T = h.shape[0]
    tm = min(tm, T)
    nch = ROPE_W // LANES

    def body(h_ref, d_ref, g_ref, c_ref, s_ref, dh_ref, dg_ref):
        i = pl.program_id(0)

        @pl.when(i == 0)
        def _():
            dg_ref[...] = jnp.zeros_like(dg_ref)

        lane = lax.broadcasted_iota(jnp.int32, (tm, LANES), 1)
        lo, even = lane < HEAD_DIM, lane % 2 == 0
        c, s = c_ref[...], s_ref[...]
        acc = [None, None]
        for j in range(nch):
            x = h_ref[:, j * LANES:(j + 1) * LANES]
            isq = j < QW // LANES
            g = g_ref[0:1, :] if isq else g_ref[1:2, :]
            d = d_ref[:, j * LANES:(j + 1) * LANES]
            if isq:
                d = d * SCALE
            r = lax.rsqrt(_half_sums(x * x, lo) * (1.0 / HEAD_DIM) + RMS_EPS)
            dn = d * c + _pair_swap(d * s, even)
            xr = x * r
            part = jnp.sum(dn * xr, axis=0, keepdims=True)
            acc[0 if isq else 1] = part if acc[0 if isq else 1] is None else acc[0 if isq else 1] + part
            dng = dn * g
            dx = r * dng - xr * (r * r) * (_half_sums(dng * x, lo) * (1.0 / HEAD_DIM))
            dh_ref[:, j * LANES:(j + 1) * LANES] = dx.astype(BF16)
        for row in range(2):
            folded = acc[row] + pltpu.roll(acc[row], HEAD_DIM, 1)
            dg_ref[row:row + 1, :] += folded

    est = 2 * (2 * _nbytes((tm, ROPE_W), F32) + _nbytes((tm, ROPE_W), BF16) + 2 * _nbytes((tm, LANES), F32)) + (4 << 20)
    return pl.pallas_call(
        body, name=name, grid=(T // tm,),
        out_shape=(jax.ShapeDtypeStruct((T, ROPE_W), BF16), jax.ShapeDtypeStruct((8, LANES), F32)),
        in_specs=[pl.BlockSpec((tm, ROPE_W), lambda i: (i, 0)), pl.BlockSpec((tm, ROPE_W), lambda i: (i, 0)),
                  pl.BlockSpec((8, LANES), lambda i: (0, 0)),
                  pl.BlockSpec((tm, LANES), lambda i: (i, 0)), pl.BlockSpec((tm, LANES), lambda i: (i, 0))],
        out_specs=(pl.BlockSpec((tm, ROPE_W), lambda i: (i, 0)), pl.BlockSpec((8, LANES), lambda i: (0, 0))),
        compiler_params=_params(("arbitrary",), est),
    )(h, d_out, gains, cos2, sin2)


def _attn_a_fwd(k, qT, v1T, *, comm=None, tq=4096, tk=512, cq=512):
    G, T, HD = k.shape
    NQ = qT.shape[2]
    HE = v1T.shape[1]
    tq, tk = min(tq, NQ), min(tk, T)
    cq = min(cq, tq)
    nk = T // tk
    grid = (G, NQ // tq, nk)

    def body(*refs):
        (k_ref, qT_ref, vT_ref, oT_ref, lse_ref, m_sc, acc_sc), comm_start, comm_finish = _host_comm(
            comm, refs, 3, 2, 2, grid)
        kv = pl.program_id(2)
        comm_start()

        @pl.when(kv == 0)
        def _():
            m_sc[...] = jnp.full_like(m_sc, NEG)
            acc_sc[...] = jnp.zeros_like(acc_sc)

        def scores(c):
            return jnp.dot(k_ref[...], qT_ref[:, c * cq:(c + 1) * cq], preferred_element_type=F32)

        nc = tq // cq
        ahead = scores(0)
        for c in range(nc):
            cols = slice(c * cq, (c + 1) * cq)
            sT = ahead
            if c + 1 < nc:
                ahead = scores(c + 1)
            m_prev = m_sc[:, cols]
            m_new = jnp.maximum(m_prev, jnp.max(sT, axis=0, keepdims=True))
            pT = jnp.exp2(sT - m_new).astype(BF16)
            acc_sc[:, cols] = (jnp.exp2(m_prev - m_new) * acc_sc[:, cols]
                               + jnp.dot(vT_ref[...], pT, preferred_element_type=F32))
            m_sc[:, cols] = m_new

        @pl.when(kv == nk - 1)
        def _():
            l = acc_sc[HD:HD + 1, :]
            oT_ref[...] = acc_sc[0:HD, :] / l
            lse_ref[...] = m_sc[...] + jnp.log2(l)

        comm_finish()

    qtr = pl.BlockSpec((None, HD, tq), lambda g, i, j: (g, 0, i))
    qvec = pl.BlockSpec((None, 1, tq), lambda g, i, j: (g, 0, i))
    est = 6 * _nbytes((cq, tk), F32) + (8 << 20)
    hosted = comm is not None
    return pl.pallas_call(
        body, name="attn_a_fwd_comm" if hosted else "attn_a_fwd", grid=grid,
        out_shape=[jax.ShapeDtypeStruct((G, HD, NQ), F32), jax.ShapeDtypeStruct((G, 1, NQ), F32)]
        + (comm.out_shape if hosted else []),
        in_specs=[pl.BlockSpec((None, tk, HD), lambda g, i, j: (g, j, 0)), qtr,
                  pl.BlockSpec((None, HE, tk), lambda g, i, j: (g, 0, j))] + (comm.in_specs if hosted else []),
        out_specs=[qtr, qvec] + (comm.out_specs if hosted else []),
        scratch_shapes=[pltpu.VMEM((1, tq), F32), pltpu.VMEM((HE, tq), F32)] + (comm.scratch if hosted else []),
        compiler_params=_params(("arbitrary",) * 3 if hosted else ("parallel", "parallel", "arbitrary"), est),
    )(k, qT, v1T, *(comm.parts if hosted else []))


def _attn_a_delta(oT, doT, *, tq=2048):
    G, HD, NQ = oT.shape
    tq = min(tq, NQ)

    def body(o_ref, d_ref, dl_ref):
        dl_ref[...] = jnp.sum(o_ref[...] * d_ref[...], axis=0, keepdims=True)

    qtr = pl.BlockSpec((None, HD, tq), lambda g, i: (g, 0, i))
    return pl.pallas_call(
        body, name="attn_a_delta", grid=(G, NQ // tq),
        out_shape=jax.ShapeDtypeStruct((G, 1, NQ), F32),
        in_specs=[qtr, qtr], out_specs=pl.BlockSpec((None, 1, tq), lambda g, i: (g, 0, i)),
        compiler_params=_params(("parallel", "parallel"), 8 << 20),
    )(oT, doT)


def _attn_a_bwd(k, v, kT, qT, doT, lse_row, delta_row, *, comm=None, tq=4096, tk=512, cq=512):
    G, T, HD = k.shape
    NQ = qT.shape[2]
    tq, tk = min(tq, NQ), min(tk, T)
    cq = min(cq, tq)
    nq, nc = NQ // tq, tq // cq
    nt = (((1,), (1,)), ((), ()))

    grid = (G, T // tk, nq)

    def body(*refs):
        (k_ref, v_ref, kT_ref, qT_ref, doT_ref, lse_ref, dl_ref, dkT_ref, dvT_ref, dqT_ref, dk_sc, dv_sc), \
            comm_start, comm_finish = _host_comm(comm, refs, 7, 3, 2, grid)
        j, i = pl.program_id(1), pl.program_id(2)
        comm_start()

        @pl.when((j == 0) & (i == 0))
        def _():
            dqT_ref[...] = jnp.zeros_like(dqT_ref)

        @pl.when(i == 0)
        def _():
            dk_sc[...] = jnp.zeros_like(dk_sc)
            dv_sc[...] = jnp.zeros_like(dv_sc)

        def scores(c):
            cols = slice(c * cq, (c + 1) * cq)
            return (jnp.dot(k_ref[...], qT_ref[:, cols], preferred_element_type=F32),
                    jnp.dot(v_ref[...], doT_ref[:, cols], preferred_element_type=F32))

        ahead = scores(0)
        dk_part = dv_part = None
        for c in range(nc):
            cols = slice(c * cq, (c + 1) * cq)
            sT, dpT = ahead
            if c + 1 < nc:
                ahead = scores(c + 1)
            pT = jnp.exp2(sT - lse_ref[:, cols])
            dsT = (pT * (dpT - dl_ref[:, cols])).astype(BF16)
            dv_c = lax.dot_general(doT_ref[:, cols], pT.astype(BF16), nt, preferred_element_type=F32)
            dk_c = lax.dot_general(qT_ref[:, cols], dsT, nt, preferred_element_type=F32)
            dv_part = dv_c if dv_part is None else dv_part + dv_c
            dk_part = dk_c if dk_part is None else dk_part + dk_c
            out_cols = pl.ds(pl.multiple_of(i * tq + c * cq, cq), cq)
            dqT_ref[:, out_cols] += jnp.dot(kT_ref[...], dsT, preferred_element_type=F32)
        dk_sc[...] += dk_part
        dv_sc[...] += dv_part

        @pl.when(i == nq - 1)
        def _():
            dkT_ref[...] = dk_sc[...] * LN2
            dvT_ref[...] = dv_sc[...]

        comm_finish()

    krow = pl.BlockSpec((None, tk, HD), lambda g, j, i: (g, j, 0))
    ktr = pl.BlockSpec((None, HD, tk), lambda g, j, i: (g, 0, j))
    qtr = pl.BlockSpec((None, HD, tq), lambda g, j, i: (g, 0, i))
    qvec = pl.BlockSpec((None, 1, tq), lambda g, j, i: (g, 0, i))
    whole = pl.BlockSpec((None, HD, NQ), lambda g, j, i: (g, 0, 0))
    est = 8 * _nbytes((cq, tk), F32) + 2 * _nbytes((HD, NQ), F32) + (8 << 20)
    hosted = comm is not None
    return pl.pallas_call(
        body, name="attn_a_bwd_comm" if hosted else "attn_a_bwd", grid=grid,
        out_shape=[jax.ShapeDtypeStruct((G, HD, T), F32), jax.ShapeDtypeStruct((G, HD, T), F32),
                   jax.ShapeDtypeStruct((G, HD, NQ), F32)] + (comm.out_shape if hosted else []),
        in_specs=[krow, krow, ktr, qtr, qtr, qvec, qvec] + (comm.in_specs if hosted else []),
        out_specs=[ktr, ktr, whole] + (comm.out_specs if hosted else []),
        scratch_shapes=[pltpu.VMEM((HD, tk), F32), pltpu.VMEM((HD, tk), F32)] + (comm.scratch if hosted else []),
        compiler_params=_params(("arbitrary", "arbitrary", "arbitrary"), est),
    )(k, v, kT, qT, doT, lse_row, delta_row, *(comm.parts if hosted else []))


WB = WINDOW
WK = 3 * WINDOW


def _win_specs(T):
    nb = T // WB
    q4 = pl.BlockSpec((None, GQ, WB, HEAD_DIM), lambda g, n: (g, 0, n, 0))
    col = pl.BlockSpec((None, GQ * WB, 1), lambda g, n: (g, 0, 0))
    bias = pl.BlockSpec((None, GQ * WB, WK), lambda g, n: (g, 0, 0))
    kt = [pl.BlockSpec((None, HEAD_DIM, WB), functools.partial(lambda g, n, o: (g, 0, n + o), o=o)) for o in range(3)]
    kr = [pl.BlockSpec((None, WB, HEAD_DIM), functools.partial(lambda g, n, o: (g, n + o, 0), o=o)) for o in range(3)]
    return nb, q4, col, bias, kt, kr


def _win_mask(n, T):
    kabs = n * WB - WB + lax.broadcasted_iota(jnp.int32, (1, WK), 1)
    return (kabs >= 0) & (kabs < T)


def _attn_b_fwd(q, kTp, vp, bias, sink_rows):
    T = q.shape[2]
    nb, q4, col, bias_spec, kt, kr = _win_specs(T)

    def body(q_ref, k0, k1, k2, v0, v1, v2, b_ref, sk_ref, o_ref, lse_ref):
        n = pl.program_id(1)
        qv = q_ref[...].reshape(GQ * WB, HEAD_DIM)
        kT = jnp.concatenate([k0[...], k1[...], k2[...]], axis=1)
        vv = jnp.concatenate([v0[...], v1[...], v2[...]], axis=0)
        s = jnp.dot(qv, kT, preferred_element_type=F32) + b_ref[...]
        s = jnp.where(_win_mask(n, T), s, NEG)
        sk = sk_ref[...]
        m = jnp.maximum(jnp.max(s, axis=1, keepdims=True), sk)
        p = jnp.exp(s - m)
        den = jnp.sum(p, axis=1, keepdims=True) + jnp.exp(sk - m)
        o = jnp.dot(p.astype(BF16), vv, preferred_element_type=F32) / den
        o_ref[...] = o.reshape(GQ, WB, HEAD_DIM)
        lse_ref[...] = (m + jnp.log(den)).reshape(GQ, WB, 1)

    return pl.pallas_call(
        body, name="attn_b_fwd", grid=(KV, nb),
        out_shape=(jax.ShapeDtypeStruct((KV, GQ, T, HEAD_DIM), F32), jax.ShapeDtypeStruct((KV, GQ, T, 1), F32)),
        in_specs=[q4] + kt + kr + [bias_spec, col],
        out_specs=(q4, pl.BlockSpec((None, GQ, WB, 1), lambda g, n: (g, 0, n, 0))),
        compiler_params=_params(("parallel", "parallel"), 24 << 20),
    )(q, kTp, kTp, kTp, vp, vp, vp, bias, sink_rows)


def _attn_b_bwd(q, kTp, kp, vTp, do, o, lse, bias, sink_rows):
    T = q.shape[2]
    nb, q4, col, bias_spec, kt, kr = _win_specs(T)
    Tp = T + 2 * WB

    def body(q_ref, k0, k1, k2, r0, r1, r2, w0, w1, w2, do_ref, o_ref, lse_ref, b_ref, sk_ref,
             dq_ref, dk_ref, dv_ref, db_ref, dsk_ref):
        n = pl.program_id(1)

        @pl.when(n == 0)
        def _():
            dk_ref[...] = jnp.zeros_like(dk_ref)
            dv_ref[...] = jnp.zeros_like(dv_ref)
            db_ref[...] = jnp.zeros_like(db_ref)
            dsk_ref[...] = jnp.zeros_like(dsk_ref)

        qv = q_ref[...].reshape(GQ * WB, HEAD_DIM)
        kT = jnp.concatenate([k0[...], k1[...], k2[...]], axis=1)
        kk = jnp.concatenate([r0[...], r1[...], r2[...]], axis=0)
        vT = jnp.concatenate([w0[...], w1[...], w2[...]], axis=1)
        dov = do_ref[...].reshape(GQ * WB, HEAD_DIM)
        lse = lse_ref[...].reshape(GQ * WB, 1)
        delta = jnp.sum(dov * o_ref[...].reshape(GQ * WB, HEAD_DIM), axis=1, keepdims=True)
        s = jnp.dot(qv, kT, preferred_element_type=F32) + b_ref[...]
        s = jnp.where(_win_mask(n, T), s, NEG)
        p = jnp.exp(s - lse)
        dob = dov.astype(BF16)
        dp = jnp.dot(dob, vT, preferred_element_type=F32)
        ds = p * (dp - delta)
        db_ref[...] += ds
        dsk_ref[...] -= jnp.exp(sk_ref[...] - lse) * delta
        dsb = ds.astype(BF16)
        dq_ref[...] = jnp.dot(dsb, kk, preferred_element_type=F32).reshape(GQ, WB, HEAD_DIM)
        tn = (((0,), (0,)), ((), ()))
        rows = pl.ds(pl.multiple_of(n * WB, WB), WK)
        dk_ref[rows, :] += lax.dot_general(dsb, qv, tn, preferred_element_type=F32)
        dv_ref[rows, :] += lax.dot_general(p.astype(BF16), dob, tn, preferred_element_type=F32)

    kacc = pl.BlockSpec((None, Tp, HEAD_DIM), lambda g, n: (g, 0, 0))
    return pl.pallas_call(
        body, name="attn_b_bwd", grid=(KV, nb),
        out_shape=(jax.ShapeDtypeStruct((KV, GQ, T, HEAD_DIM), F32),
                   jax.ShapeDtypeStruct((KV, Tp, HEAD_DIM), F32), jax.ShapeDtypeStruct((KV, Tp, HEAD_DIM), F32),
                   jax.ShapeDtypeStruct((KV, GQ * WB, WK), F32), jax.ShapeDtypeStruct((KV, GQ * WB, 1), F32)),
        in_specs=[q4] + kt + kr + kt + [q4, q4, pl.BlockSpec((None, GQ, WB, 1), lambda g, n: (g, 0, n, 0)), bias_spec, col],
        out_specs=(q4, kacc, kacc, bias_spec, col),
        compiler_params=_params(("parallel", "arbitrary"), 40 << 20),
    )(q, kTp, kTp, kTp, kp, kp, kp, vTp, vTp, vTp, do, o, lse, bias, sink_rows)


def _bias_table(rel_bias_t, bucket):
    nh, n = rel_bias_t.shape[0], bucket.shape[1]

    def body(rb_ref, bk_ref, o_ref):
        bk = bk_ref[...]
        out = jnp.full((nh, n), NEG, F32)
        for b in range(N_BUCKETS):
            out = jnp.where(bk == b, rb_ref[:, b:b + 1], out)
        o_ref[...] = out

    return pl.pallas_call(
        body, name="bias_table", out_shape=jax.ShapeDtypeStruct((nh, n), F32),
        compiler_params=pltpu.CompilerParams(vmem_limit_bytes=32 << 20),
    )(rel_bias_t, bucket)


def _bias_sink_grads(db_list, dsk_list, bucket):
    L = len(db_list)

    def body(*refs):
        db_refs, dsk_refs, bk_ref = refs[:L], refs[L:2 * L], refs[2 * L]
        drb_ref, dsink_ref = refs[2 * L + 1], refs[2 * L + 2]
        tot = db_refs[0][...]
        for r in db_refs[1:]:
            tot = tot + r[...]
        bk = bk_ref[...]
        lane = lax.broadcasted_iota(jnp.int32, (2 * GQ, N_BUCKETS), 1)
        out = jnp.zeros((2 * GQ, N_BUCKETS), F32)
        for b in range(N_BUCKETS):
            sb = jnp.sum(jnp.where(bk == b, tot, 0.0), axis=1, keepdims=True)
            out = jnp.where(lane == b, sb, out)
        drb_ref[...] = out
        for l in range(L):
            dsink_ref[l] = jnp.sum(dsk_refs[l][...], axis=1, keepdims=True)

    return pl.pallas_call(
        body, name="bias_sink_grads",
        out_shape=(jax.ShapeDtypeStruct((2 * GQ, N_BUCKETS), F32), jax.ShapeDtypeStruct((L, 2 * GQ, 1), F32)),
        compiler_params=pltpu.CompilerParams(vmem_limit_bytes=32 << 20),
    )(*db_list, *dsk_list, bucket)


def _outnorm_fwd(oa, ob, ga, gb, *, tm=512):
    T = oa.shape[0]
    tm = min(tm, T)

    def body(oa_ref, ob_ref, ga_ref, gb_ref, y_ref):
        for j, (o_ref, g_ref) in enumerate(((oa_ref, ga_ref), (ob_ref, gb_ref))):
            o = o_ref[...]
            r = lax.rsqrt(jnp.mean(o * o, axis=1, keepdims=True) + RMS_EPS)
            y_ref[:, j * QW:(j + 1) * QW] = (o * r * g_ref[...]).astype(BF16)

    half = pl.BlockSpec((tm, QW), lambda i: (i, 0))
    vec = pl.BlockSpec((1, QW), lambda i: (0, 0))
    return pl.pallas_call(
        body, name="outnorm_fwd", grid=(T // tm,),
        out_shape=jax.ShapeDtypeStruct((T, 2 * QW), BF16),
        in_specs=[half, half, vec, vec], out_specs=pl.BlockSpec((tm, 2 * QW), lambda i: (i, 0)),
        compiler_params=_params(("parallel",), 16 << 20),
    )(oa, ob, ga, gb)


def _outnorm_bwd(dy, oa, ob, ga, gb, *, tm=512):
    T = oa.shape[0]
    tm = min(tm, T)

    def body(dy_ref, oa_ref, ob_ref, ga_ref, gb_ref, doa_ref, dob_ref, dg_ref):
        i = pl.program_id(0)

        @pl.when(i == 0)
        def _():
            dg_ref[...] = jnp.zeros_like(dg_ref)

        for j, (o_ref, g_ref, d_ref) in enumerate(((oa_ref, ga_ref, doa_ref), (ob_ref, gb_ref, dob_ref))):
            o = o_ref[...]
            d = dy_ref[:, j * QW:(j + 1) * QW]
            r = lax.rsqrt(jnp.mean(o * o, axis=1, keepdims=True) + RMS_EPS)
            orr = o * r
            dg_ref[j:j + 1, :] += jnp.sum(d * orr, axis=0, keepdims=True)
            dgv = d * g_ref[...]
            d_ref[...] = r * dgv - orr * (r * r) * jnp.mean(dgv * o, axis=1, keepdims=True)

    half = pl.BlockSpec((tm, QW), lambda i: (i, 0))
    vec = pl.BlockSpec((1, QW), lambda i: (0, 0))
    return pl.pallas_call(
        body, name="outnorm_bwd", grid=(T // tm,),
        out_shape=(jax.ShapeDtypeStruct((T, QW), F32), jax.ShapeDtypeStruct((T, QW), F32),
                   jax.ShapeDtypeStruct((8, QW), F32)),
        in_specs=[pl.BlockSpec((tm, 2 * QW), lambda i: (i, 0)), half, half, vec, vec],
        out_specs=(half, half, pl.BlockSpec((8, QW), lambda i: (0, 0))),
        compiler_params=_params(("arbitrary",), 24 << 20),
    )(dy, oa, ob, ga, gb)


GELU_C = math.sqrt(2.0 / math.pi)
GELU_A = 0.044715
HALO = 16


def _gelu_parts(x):
    t = jnp.tanh(GELU_C * (x + GELU_A * (x * x * x)))
    return 0.5 * (1.0 + t), t


def _halo_specs(tm, tn, T):
    nh = tm // HALO
    last = T // HALO - 1
    cur = pl.BlockSpec((tm, tn), lambda j, i: (i, j))
    prev = pl.BlockSpec((HALO, tn), lambda j, i: (jnp.maximum(i * nh - 1, 0), j))
    nxt = pl.BlockSpec((HALO, tn), lambda j, i: (jnp.minimum((i + 1) * nh, last), j))
    return cur, prev, nxt


def _conv_glu_fwd(g, u, conv_w, conv_b, *, tm=256, tn=1408):
    T, F = g.shape
    tm, tn = min(tm, T), min(tn, F)
    cur, prev, nxt = _halo_specs(tm, tn, T)

    def body(g_ref, gp_ref, gn_ref, u_ref, w_ref, b_ref, a_ref):
        i = pl.program_id(1)
        gv = g_ref[...]
        row = lax.broadcasted_iota(jnp.int32, (tm, tn), 0)
        before = jnp.where(i * tm > 0, gp_ref[HALO - 1:HALO, :], 0.0)
        after = jnp.where((i + 1) * tm < T, gn_ref[0:1, :], 0.0)
        gm1 = jnp.where(row == 0, before, pltpu.roll(gv, 1, 0))
        gp1 = jnp.where(row == tm - 1, after, pltpu.roll(gv, tm - 1, 0))
        gc = ((b_ref[...] + gm1 * w_ref[0:1, :]) + gv * w_ref[1:2, :]) + gp1 * w_ref[2:3, :]
        cdf, _ = _gelu_parts(gc)
        a_ref[...] = (gc * cdf * u_ref[...].astype(F32)).astype(BF16)

    wspec = pl.BlockSpec((8, tn), lambda j, i: (0, j))
    est = 2 * (3 * _nbytes((tm, tn), F32)) + 8 * _nbytes((tm, tn), F32)
    return pl.pallas_call(
        body, name="conv_glu_fwd", grid=(F // tn, T // tm),
        out_shape=jax.ShapeDtypeStruct((T, F), BF16),
        in_specs=[cur, prev, nxt, cur, wspec, pl.BlockSpec((1, tn), lambda j, i: (0, j))],
        out_specs=cur,
        compiler_params=_params(("parallel", "parallel"), est),
    )(g, g, g, u, conv_w, conv_b)


def _conv_glu_bwd(dact, g, u, conv_w, conv_b, *, tm=256, tn=1408):
    T, F = g.shape
    tm, tn = min(tm, T), min(tn, F)
    cur, prev, nxt = _halo_specs(tm, tn, T)
    te = tm + 2 * HALO

    def body(d_ref, dp_ref, dn_ref, g_ref, gp_ref, gn_ref, u_ref, up_ref, un_ref, w_ref, b_ref,
             dg_ref, du_ref, dc_ref):
        i = pl.program_id(1)

        @pl.when(i == 0)
        def _():
            dc_ref[...] = jnp.zeros_like(dc_ref)

        grow = i * tm - HALO + lax.broadcasted_iota(jnp.int32, (te, tn), 0)
        valid = (grow >= 0) & (grow < T)
        ge = jnp.where(valid, jnp.concatenate([gp_ref[...], g_ref[...], gn_ref[...]], axis=0), 0.0)
        ue = jnp.concatenate([up_ref[...], u_ref[...], un_ref[...]], axis=0).astype(F32)
        de = jnp.concatenate([dp_ref[...], d_ref[...], dn_ref[...]], axis=0).astype(F32)
        w0, w1, w2 = w_ref[0:1, :], w_ref[1:2, :], w_ref[2:3, :]
        gm1 = pltpu.roll(ge, 1, 0)
        gp1 = pltpu.roll(ge, te - 1, 0)
        gc = ((b_ref[...] + gm1 * w0) + ge * w1) + gp1 * w2
        cdf, t = _gelu_parts(gc)
        dgelu = cdf + 0.5 * gc * (1.0 - t * t) * (GELU_C * (1.0 + 3.0 * GELU_A * (gc * gc)))
        dgc = jnp.where(valid, de * ue * dgelu, 0.0)
        dge = w0 * pltpu.roll(dgc, te - 1, 0) + w1 * dgc + w2 * pltpu.roll(dgc, 1, 0)
        mid = slice(HALO, HALO + tm)
        dg_ref[...] = dge[mid].astype(BF16)
        du_ref[...] = (de[mid] * (gc[mid] * cdf[mid])).astype(BF16)
        dgm = dgc[mid]
        dc_ref[0:1, :] += jnp.sum(dgm * gm1[mid], axis=0, keepdims=True)
        dc_ref[1:2, :] += jnp.sum(dgm * ge[mid], axis=0, keepdims=True)
        dc_ref[2:3, :] += jnp.sum(dgm * gp1[mid], axis=0, keepdims=True)
        dc_ref[3:4, :] += jnp.sum(dgm, axis=0, keepdims=True)

    wspec = pl.BlockSpec((8, tn), lambda j, i: (0, j))
    est = 2 * (3 * _nbytes((tm, tn), F32) + 2 * _nbytes((tm, tn), BF16)) + 16 * _nbytes((te, tn), F32)
    return pl.pallas_call(
        body, name="conv_glu_bwd", grid=(F // tn, T // tm),
        out_shape=(jax.ShapeDtypeStruct((T, F), BF16), jax.ShapeDtypeStruct((T, F), BF16),
                   jax.ShapeDtypeStruct((8, F), F32)),
        in_specs=[cur, prev, nxt, cur, prev, nxt, cur, prev, nxt, wspec, pl.BlockSpec((1, tn), lambda j, i: (0, j))],
        out_specs=(cur, cur, wspec),
        compiler_params=_params(("parallel", "arbitrary"), est),
    )(dact, dact, dact, g, g, g, u, u, u, conv_w, conv_b)


def _adamw_math(w, g, m, v):
    m = ADAM_B1 * m + (1.0 - ADAM_B1) * g
    v = ADAM_B2 * v + (1.0 - ADAM_B2) * (g * g)
    m_hat = m / (1.0 - ADAM_B1 ** ADAM_STEP)
    v_hat = v / (1.0 - ADAM_B2 ** ADAM_STEP)
    delta = -ADAM_LR * (m_hat / (jnp.sqrt(v_hat) + ADAM_EPS) + ADAM_WD * w)
    return delta, m, v


def _adamw(w, m, v, gparts, *, name, tr):
    R, C = w.shape
    tr = min(tr, R)
    assert R % tr == 0

    def body(w_ref, m_ref, v_ref, gp_ref, g_ref, d_ref, nm_ref, nv_ref):
        g = gp_ref[0].astype(F32)
        for j in range(1, N_DEV):
            g = g + gp_ref[j].astype(F32)
        delta, nm, nv = _adamw_math(w_ref[...], g, m_ref[...], v_ref[...])
        g_ref[...] = g
        d_ref[...] = delta
        nm_ref[...] = nm
        nv_ref[...] = nv

    blk = pl.BlockSpec((tr, C), lambda i: (i, 0))
    out = jax.ShapeDtypeStruct((R, C), F32)
    return pl.pallas_call(
        body, name=name, grid=(R // tr,), out_shape=(out, out, out, out),
        in_specs=[blk, blk, blk, pl.BlockSpec((N_DEV, tr, C), lambda i: (0, i, 0))],
        out_specs=(blk, blk, blk, blk),
        compiler_params=_params(("parallel",), 24 << 20),
    )(w, m, v, gparts)


def _rope_tables(T):
    rows_n = T // GRID_W
    row = jnp.repeat(jnp.arange(rows_n, dtype=F32), GRID_W)
    col = jnp.tile(jnp.arange(GRID_W, dtype=F32), rows_n)
    half = HEAD_DIM // 2
    inv_freq = ROPE_THETA ** (-jnp.arange(0, half, 2, dtype=F32) / half)
    ang = jnp.concatenate([row[:, None] * inv_freq, col[:, None] * inv_freq], axis=-1)
    cos, sin = jnp.cos(ang), jnp.sin(ang)
    cos64 = jnp.repeat(cos, 2, axis=-1)
    sin64 = jnp.stack([-sin, sin], axis=-1).reshape(T, HEAD_DIM)
    return jnp.tile(cos64, (1, 2)), jnp.tile(sin64, (1, 2))


def _t5_bucket(rel):
    half = N_BUCKETS // 2
    max_exact = half // 2
    bucket = jnp.where(rel > 0, half, 0)
    rp = jnp.abs(rel)
    rpf = jnp.maximum(rp, 1).astype(F32)
    large = max_exact + (jnp.log(rpf / max_exact) / math.log(MAX_DISTANCE / max_exact)
                         * (half - max_exact)).astype(jnp.int32)
    large = jnp.minimum(large, half - 1)
    return bucket + jnp.where(rp < max_exact, rp, large)


def _window_buckets():
    qpos = jnp.arange(WB, dtype=jnp.int32)
    kpos = jnp.arange(WK, dtype=jnp.int32) - WB
    rel = kpos[None, :] - qpos[:, None]
    return jnp.where(jnp.abs(rel) <= WINDOW, _t5_bucket(rel), -1)


def _heads_first(a, nh):
    T = a.shape[0]
    return a.reshape(T, nh, HEAD_DIM).transpose(1, 0, 2)


def _heads_last(a):
    nh, T, _ = a.shape
    return a.transpose(1, 0, 2).reshape(T, nh * HEAD_DIM)


def _pad_keys(a):
    return jnp.pad(a, ((0, 0), (WB, WB), (0, 0)))


def _row(v):
    return v.reshape(1, -1)


def _rows8(rows, width):
    a = jnp.stack(list(rows), axis=0)
    return jnp.pad(a, ((0, 8 - a.shape[0]), (0, 0)))


def _layer_fwd(l, xin, W, tabs, comm=None, on_comm=None):
    xhat, xg, xb, x16 = xin
    T = xhat.shape[0]
    cos2, sin2, bias, _ = tabs
    h = _mm([x16], [W["w_in"][l]], name="mm_in", out_dtype=F32, tm=512, tn=IN_COLS, tk=D_MODEL)
    gains = _rows8([jnp.tile(W["q_norm"][l], 2), jnp.tile(W["k_norm"][l], 2)], LANES)
    roped = _qk_rope_fwd(h, gains, cos2, sin2, name="qk_rope_fwd")
    qa = _heads_first(roped[:, :QW], KV * GQ).reshape(KV, GQ * T, HEAD_DIM)
    ka = _heads_first(roped[:, QW:], KV)
    va = _heads_first(h[:, ROPE_W:ROPE_W + KW].astype(BF16), KV)
    kaT, vaT = ka.transpose(0, 2, 1), va.transpose(0, 2, 1)
    qaT = qa.transpose(0, 2, 1)
    res = _attn_a_fwd(ka, qaT, jnp.concatenate([vaT, jnp.ones((KV, ONES_ROWS, T), BF16)], axis=1), comm=comm)
    oaT, lse_a = res[0], res[1]
    if comm is not None:
        on_comm(res[2:])
    o0 = ROPE_W + KW
    qb = _heads_first((h[:, o0:o0 + QW] * SCALE).astype(BF16), KV * GQ).reshape(KV, GQ, T, HEAD_DIM)
    kb = _pad_keys(_heads_first(h[:, o0 + QW:o0 + QW + KW].astype(BF16), KV))
    vb = _pad_keys(_heads_first(h[:, o0 + QW + KW:].astype(BF16), KV))
    kbT, vbT = kb.transpose(0, 2, 1), vb.transpose(0, 2, 1)
    sink_rows = jnp.repeat(W["sink"][l], WB).reshape(KV, GQ * WB, 1)
    ob, lse_b = _attn_b_fwd(qb, kbT, vb, bias, sink_rows)
    oa_t = oaT.reshape(KV, HEAD_DIM, GQ, T).transpose(3, 0, 2, 1).reshape(T, QW)
    ob_t = _heads_last(ob.reshape(KV * GQ, T, HEAD_DIM))
    ga, gb = _row(W["out_norm_a"][l]), _row(W["out_norm_b"][l])
    ycat = _outnorm_fwd(oa_t, ob_t, ga, gb)
    g1, b1 = _row(W["ln1_g"][l]), _row(W["ln1_b"][l])
    x1hat, rstd1, x1_16 = _mm_res_ln(ycat, W["w_out"][l], xhat, xg, xb, g1, b1, name="mm_out_ln", tm=512)
    gate = _mm([x1_16], [W["w_gate"][l]], name="mm_gate", out_dtype=F32, tm=512, tn=D_FF // 2, tk=D_MODEL)
    up = _mm([x1_16], [W["w_up"][l]], name="mm_up", out_dtype=BF16, tm=512, tn=D_FF // 2, tk=D_MODEL)
    cw = jnp.pad(W["conv_w"][l], ((0, 5), (0, 0)))
    cb = _row(W["conv_b"][l])
    act = _conv_glu_fwd(gate, up, cw, cb)
    g2, b2 = _row(W["ln2_g"][l]), _row(W["ln2_b"][l])
    x2hat, rstd2, x2_16 = _mm_res_ln(act, W["w_down"][l], x1hat, g1, b1, g2, b2, name="mm_down_ln", tm=256)
    saved = dict(x16=x16, h=h, gains=gains, qa=qa, qaT=qaT, ka=ka, kaT=kaT, va=va, oaT=oaT, lse_a=lse_a,
                 qb=qb, kb=kb, kbT=kbT, vbT=vbT, ob=ob, lse_b=lse_b, sink_rows=sink_rows, oa_t=oa_t, ob_t=ob_t,
                 ga=ga, gb=gb, ycat=ycat, x1hat=x1hat, rstd1=rstd1, x1_16=x1_16, g1=g1, b1=b1, gate=gate, up=up,
                 cw=cw, cb=cb, act=act, x2hat=x2hat, rstd2=rstd2, g2=g2, b2=b2)
    return (x2hat, g2, b2, x2_16), saved


def _layer_bwd(l, S, W, WT, tabs, dz2, dz2_16, stats2, scatter=None):
    cos2, sin2, bias, _ = tabs
    T = dz2.shape[0]
    G = {}
    G["ln2_g"], G["ln2_b"] = stats2[0], stats2[1]
    G["w_down"] = _mm([S["act"]], [dz2_16], name="dw_down", out_dtype=BF16, trans_a=True, tm=D_FF // 2, tn=D_MODEL, tk=512)
    dact = _mm([dz2_16], [WT["w_down"][l]], name="mm_dact", out_dtype=BF16, tm=512, tn=D_FF // 2, tk=D_MODEL)
    dg, du, dconv = _conv_glu_bwd(dact, S["gate"], S["up"], S["cw"], S["cb"])
    G["conv_w"], G["conv_b"] = dconv[0:3], dconv[3]
    G["w_gate"] = _mm([S["x1_16"]], [dg], name="dw_gate", out_dtype=BF16, trans_a=True, tm=D_MODEL, tn=D_FF // 2, tk=512)
    G["w_up"] = _mm([S["x1_16"]], [du], name="dw_up", out_dtype=BF16, trans_a=True, tm=D_MODEL, tn=D_FF // 2, tk=512)
    dx1 = _mm([dg, du], [WT["w_gate"][l], WT["w_up"][l]], name="mm_dx1", out_dtype=F32, tm=512, tn=D_MODEL,
              tk=D_FF // 2, add=dz2, add_scale=ALPHA)
    dz1, dz1_16, stats1 = _ln_bwd(S["x1hat"], S["rstd1"], S["g1"], S["b1"], name="ln1_bwd", dx=dx1)
    G["ln1_g"], G["ln1_b"] = stats1[0], stats1[1]
    G["w_out"] = _mm([S["ycat"]], [dz1_16], name="dw_out", out_dtype=BF16, trans_a=True, tm=D_MODEL, tn=D_MODEL, tk=512)
    dycat = _mm([dz1_16], [WT["w_out"][l]], name="mm_dycat", out_dtype=F32, tm=512, tn=D_MODEL, tk=D_MODEL)
    doa_t, dob_t, dgn = _outnorm_bwd(dycat, S["oa_t"], S["ob_t"], S["ga"], S["gb"])
    G["out_norm_a"], G["out_norm_b"] = dgn[0], dgn[1]
    doa = _heads_first(doa_t, KV * GQ).reshape(KV, GQ * T, HEAD_DIM)
    dob = _heads_first(dob_t, KV * GQ).reshape(KV, GQ, T, HEAD_DIM)
    doaT = doa.transpose(0, 2, 1)
    delta = _attn_a_delta(S["oaT"], doaT)
    res = _attn_a_bwd(S["ka"], S["va"], S["kaT"], S["qaT"], doaT.astype(BF16), S["lse_a"], delta,
                      comm=scatter(G) if scatter is not None else None)
    dkaT, dvaT, dqaT = res[:3]
    dqa_t = dqaT.reshape(KV, HEAD_DIM, GQ, T).transpose(3, 0, 2, 1).reshape(T, QW)
    d_roped = jnp.concatenate([dqa_t, dkaT.transpose(2, 0, 1).reshape(T, KW)], axis=1)
    dh_rope, dgain = _qk_rope_bwd(S["h"], d_roped, S["gains"], cos2, sin2, name="qk_rope_bwd")
    G["q_norm"], G["k_norm"] = dgain[0, :HEAD_DIM], dgain[1, :HEAD_DIM]
    dqb, dkb, dvb, dbias, dsk = _attn_b_bwd(S["qb"], S["kbT"], S["kb"], S["vbT"], dob, S["ob"], S["lse_b"], bias,
                                            S["sink_rows"])
    dh = jnp.concatenate([
        dh_rope, dvaT.transpose(2, 0, 1).reshape(T, KW).astype(BF16),
        (_heads_last(dqb.reshape(KV * GQ, T, HEAD_DIM)) * SCALE).astype(BF16),
        _heads_last(dkb[:, WB:WB + T]).astype(BF16), _heads_last(dvb[:, WB:WB + T]).astype(BF16)], axis=1)
    G["w_in"] = _mm([S["x16"]], [dh], name="dw_in", out_dtype=BF16, trans_a=True, tm=D_MODEL, tn=IN_COLS, tk=512)
    dxin = _mm([dh], [WT["w_in"][l]], name="mm_dxin", out_dtype=F32, tm=512, tn=D_MODEL, tk=IN_COLS,
               add=dz1, add_scale=ALPHA)
    return dxin, G, dbias.reshape(KV * GQ, WB * WK), dsk.reshape(KV * GQ, WB), res[3:]


BIG = ("w_in", "w_out", "w_gate", "w_up", "w_down")
COL_SHARDED = ("w_in", "w_gate", "w_up")
WIRE_COLS = 1024


def _unshard(name, gathered, shard_shape):
    _, r, c = shard_shape
    blocks = gathered.reshape(N_DEV, r, c)
    if name in COL_SHARDED:
        return blocks.transpose(1, 0, 2).reshape(r, N_DEV * c)
    return blocks.reshape(N_DEV * r, c)


def _to_owner_blocks(name, full, shard_shape):
    _, r, c = shard_shape
    if name in COL_SHARDED:
        blocks = full.reshape(r, N_DEV, c).transpose(1, 0, 2)
    else:
        blocks = full.reshape(N_DEV, r, c)
    return blocks.reshape(N_DEV, -1, WIRE_COLS)


def _pack_small(vals):
    flat = jnp.concatenate([vals[n].reshape(-1).astype(F32) for n in SMALL_NAMES])
    pad = (-flat.shape[0]) % (8 * LANES)
    return jnp.pad(flat, (0, pad)).reshape(-1, LANES)


def _unpack_small(packed, shapes):
    flat = packed.reshape(-1)
    out, off = {}, 0
    for n in SMALL_NAMES:
        size = math.prod(shapes[n])
        out[n] = flat[off:off + size].reshape(shapes[n])
        off += size
    return out


def kernel(x, rel_bias, w_in, q_norm, k_norm, sink, out_norm_a, out_norm_b, w_out, ln1_g, ln1_b, w_gate, w_up, conv_w, conv_b, w_down, ln2_g, ln2_b, loss_target, m_rel_bias, m_w_in, m_q_norm, m_k_norm, m_sink, m_out_norm_a, m_out_norm_b, m_w_out, m_ln1_g, m_ln1_b, m_w_gate, m_w_up, m_conv_w, m_conv_b, m_w_down, m_ln2_g, m_ln2_b, v_rel_bias, v_w_in, v_q_norm, v_k_norm, v_sink, v_out_norm_a, v_out_norm_b, v_w_out, v_ln1_g, v_ln1_b, v_w_gate, v_w_up, v_conv_w, v_conv_b, v_w_down, v_ln2_g, v_ln2_b):
    P = dict(rel_bias=rel_bias, w_in=w_in, q_norm=q_norm, k_norm=k_norm, sink=sink, out_norm_a=out_norm_a,
             out_norm_b=out_norm_b, w_out=w_out, ln1_g=ln1_g, ln1_b=ln1_b, w_gate=w_gate, w_up=w_up, conv_w=conv_w,
             conv_b=conv_b, w_down=w_down, ln2_g=ln2_g, ln2_b=ln2_b)
    M = dict(rel_bias=m_rel_bias, w_in=m_w_in, q_norm=m_q_norm, k_norm=m_k_norm, sink=m_sink, out_norm_a=m_out_norm_a,
             out_norm_b=m_out_norm_b, w_out=m_w_out, ln1_g=m_ln1_g, ln1_b=m_ln1_b, w_gate=m_w_gate, w_up=m_w_up,
             conv_w=m_conv_w, conv_b=m_conv_b, w_down=m_w_down, ln2_g=m_ln2_g, ln2_b=m_ln2_b)
    V = dict(rel_bias=v_rel_bias, w_in=v_w_in, q_norm=v_q_norm, k_norm=v_k_norm, sink=v_sink, out_norm_a=v_out_norm_a,
             out_norm_b=v_out_norm_b, w_out=v_w_out, ln1_g=v_ln1_g, ln1_b=v_ln1_b, w_gate=v_w_gate, w_up=v_w_up,
             conv_w=v_conv_w, conv_b=v_conv_b, w_down=v_w_down, ln2_g=v_ln2_g, ln2_b=v_ln2_b)
    names = list(P)
    T = x.shape[1]
    me = 4 * lax.axis_index("x") + 2 * lax.axis_index("y") + lax.axis_index("c")

    L, taps, fc = conv_w.shape
    W = {n: ([None] * DEPTH if n in BIG else P[n]) for n in names}
    WT = {n: [None] * DEPTH for n in BIG}

    def wire(n, l):
        return P[n][l].astype(BF16).reshape(-1, WIRE_COLS)

    def take(n, l, gathered):
        W[n][l] = _unshard(n, gathered, P[n].shape)
        WT[n][l] = W[n][l].T

    take("w_in", 0, _exchange([wire("w_in", 0)], [True], name="gather_w_in0")[0])
    later = [(n, l) for l in range(DEPTH) for n in BIG if (n, l) != ("w_in", 0)]
    cw_shard = conv_w.reshape(-1)
    cw_wire = jnp.pad(cw_shard, (0, (-cw_shard.shape[0]) % LANES)).reshape(-1, LANES)
    gather_rest = _Comm([wire(n, l) for n, l in later] + [cw_wire], [True] * (len(later) + 1))

    def on_gathered(outs):
        for (n, l), g in zip(later, outs):
            take(n, l, g)
        cw_all = outs[-1].reshape(N_DEV, -1)[:, :cw_shard.shape[0]].reshape(N_DEV, L, taps, fc)
        W["conv_w"] = cw_all.transpose(1, 2, 0, 3).reshape(L, taps, N_DEV * fc)

    cos2, sin2 = _rope_tables(T)
    bucket = _window_buckets()
    bias = _bias_table(rel_bias.T, bucket.reshape(1, WB * WK)).reshape(KV, GQ * WB, WK)
    tabs = (cos2, sin2, bias, bucket)

    ones, zeros = jnp.ones((1, D_MODEL), F32), jnp.zeros((1, D_MODEL), F32)
    cur = (x[0], ones, zeros, x[0].astype(BF16))
    saved = []
    for l in range(DEPTH):
        cur, S = _layer_fwd(l, cur, W, tabs, comm=gather_rest if l == 0 else None, on_comm=on_gathered)
        saved.append(S)

    def owner_blocks(n, l):
        return _to_owner_blocks(n, grads[l][n], P[n].shape)

    early = ([(n, l) for l in range(1, DEPTH) for n in BIG] + [(n, 0) for n in BIG if n != "w_in"])

    def scatter_early(g0):
        grads[0] = g0
        return _Comm([owner_blocks(n, l) for n, l in early], [False] * len(early))

    grads = [None] * DEPTH
    dbs, dsks = [None] * DEPTH, [None] * DEPTH
    S = saved[-1]
    dz, dz16, stats = _ln_bwd(S["x2hat"], S["rstd2"], S["g2"], S["b2"], name="loss_ln2_bwd", target=loss_target[0])
    loss = lax.psum(stats[2, 0], ("x", "y", "c"))
    recv = {}
    for l in reversed(range(DEPTH)):
        S = saved[l]
        dxin, grads[l], dbs[l], dsks[l], got = _layer_bwd(l, S, W, WT, tabs, dz, dz16, stats,
                                                         scatter=scatter_early if l == 0 else None)
        if l == 0:
            recv.update(zip(early, got))
        if l > 0:
            Sp = saved[l - 1]
            dz, dz16, stats = _ln_bwd(Sp["x2hat"], Sp["rstd2"], Sp["g2"], Sp["b2"], name="ln2_bwd", dx=dxin)
    grad_x = dxin[None]

    drb, dsink = _bias_sink_grads(dbs, dsks, bucket.reshape(1, WB * WK))
    small_g = {n: jnp.stack([grads[l][n] for l in range(DEPTH)]) for n in SMALL_NAMES if n not in ("rel_bias", "sink")}
    small_g["rel_bias"] = drb.T
    small_g["sink"] = dsink.reshape(DEPTH, KV * GQ)
    recv[("w_in", 0)], small_recv = _exchange([owner_blocks("w_in", 0), _pack_small(small_g)], [False, True],
                                              name="scatter_w_in0_gather_small")

    out_g, out_d, out_m, out_v = {}, {}, {}, {}
    for n in BIG:
        shp = P[n].shape
        gparts = jnp.concatenate([recv[(n, l)] for l in range(DEPTH)], axis=1)
        res = _adamw(P[n].reshape(-1, WIRE_COLS), M[n].reshape(-1, WIRE_COLS), V[n].reshape(-1, WIRE_COLS), gparts,
                     name="adamw_" + n, tr=64)
        out_g[n], out_d[n], out_m[n], out_v[n] = (r.reshape(shp) for r in res)
    full_shapes = {n: W[n].shape for n in SMALL_NAMES}

    def small_state(D):
        vals = {n: D[n] for n in SMALL_NAMES if n != "conv_w"}
        cw = jnp.zeros((L, taps, N_DEV, fc), F32)
        cw = lax.dynamic_update_slice(cw, D["conv_w"].reshape(L, taps, 1, fc), (0, 0, me, 0))
        vals["conv_w"] = cw.reshape(L, taps, N_DEV * fc)
        return _pack_small(vals)

    sw, sm, sv = small_state(P), small_state(M), small_state(V)
    res = _adamw(sw, sm, sv, small_recv, name="adamw_small", tr=sw.shape[0])
    for dst, packed in zip((out_g, out_d, out_m, out_v), res):
        vals = _unpack_small(packed, full_shapes)
        for n in SMALL_NAMES:
            if n == "conv_w":
                sl = lax.dynamic_slice(vals[n].reshape(L, taps, N_DEV, fc), (0, 0, me, 0), (L, taps, 1, fc))
                dst[n] = sl.reshape(L, taps, fc)
            else:
                dst[n] = vals[n]
    return (loss, grad_x, *[out_g[n] for n in names], *[out_d[n] for n in names],
            *[out_m[n] for n in names], *[out_v[n] for n in names])
```

```python
import functools
import math

import jax
import jax.numpy as jnp
from jax import lax
from jax.experimental import pallas as pl
from jax.experimental.pallas import tpu as pltpu

F32 = jnp.float32
BF16 = jnp.bfloat16
MESH = pl.DeviceIdType.MESH

N_DEV = 8
D_MODEL = 1024
DEPTH = 2
HEAD_DIM = 64
KV = 2
GQ = 4
QW = KV * GQ * HEAD_DIM
KW = KV * HEAD_DIM
ROPE_W = QW + KW
IN_COLS = 2 * (QW + 2 * KW)
D_FF = 2816
GRID_W = 64
ROPE_THETA = 10000.0
WINDOW = 128
N_BUCKETS = 32
MAX_DISTANCE = 128
ALPHA = (2.0 * DEPTH) ** 0.25
RMS_EPS = 1e-6
LN_EPS = 1e-5
SCALE = HEAD_DIM ** -0.5
LOG2E = math.log2(math.e)
LN2 = math.log(2.0)
NEG = -1e30
ONES_ROWS = 16

ADAM_LR = 0.001
ADAM_B1 = 0.9
ADAM_B2 = 0.999
ADAM_EPS = 1e-08
ADAM_WD = 0.01
ADAM_STEP = 10

LANES = 128
VMEM_CAP = 60 * 1024 * 1024
SMALL_NAMES = ("rel_bias", "q_norm", "k_norm", "sink", "out_norm_a", "out_norm_b", "ln1_g", "ln1_b",
               "conv_b", "ln2_g", "ln2_b", "conv_w")


def _params(sem, est_bytes):
    limit = int(min(VMEM_CAP, est_bytes + (8 << 20)))
    return pltpu.CompilerParams(dimension_semantics=sem, vmem_limit_bytes=limit)


def _nbytes(shape, dtype):
    return math.prod(shape) * jnp.dtype(dtype).itemsize


class _Comm:
    def __init__(self, parts, gathers):
        self.parts, self.gathers, self.n = list(parts), list(gathers), len(parts)
        hbm = pl.BlockSpec(memory_space=pltpu.HBM)
        self.in_specs = [hbm] * self.n
        self.out_specs = [hbm] * self.n
        self.out_shape = [jax.ShapeDtypeStruct((N_DEV,) + tuple(p.shape if g else p.shape[1:]), p.dtype)
                          for p, g in zip(self.parts, self.gathers)]
        self.scratch = [pltpu.SemaphoreType.DMA((self.n * (N_DEV - 1),)), pltpu.SemaphoreType.DMA((self.n * (N_DEV - 1),)),
                        pltpu.SemaphoreType.DMA((self.n,))]

    def bind(self, ins, outs, sems):
        send_sems, recv_sems, local_sems = sems
        gathers, n = self.gathers, self.n
        me = 4 * lax.axis_index("x") + 2 * lax.axis_index("y") + lax.axis_index("c")

        def src(k, j):
            return ins[k] if gathers[k] else ins[k].at[j]

        def copy(k, d, peer, lands_in):
            return pltpu.make_async_remote_copy(
                src_ref=src(k, peer), dst_ref=outs[k].at[lands_in],
                send_sem=send_sems.at[k * (N_DEV - 1) + d - 1], recv_sem=recv_sems.at[k * (N_DEV - 1) + d - 1],
                device_id=(peer // 4, lax.rem(peer // 2, 2), lax.rem(peer, 2)), device_id_type=MESH)

        def send(k, d):
            return copy(k, d, lax.rem(me + d, N_DEV), me)

        def arrival(k, d):
            frm = lax.rem(me + N_DEV - d, N_DEV)
            return copy(k, d, frm, frm)

        def local(k):
            return pltpu.make_async_copy(src(k, me), outs[k].at[me], local_sems.at[k])

        def start():
            for k in range(n):
                local(k).start()
                for d in range(1, N_DEV):
                    send(k, d).start()

        def finish():
            for k in range(n):
                for d in range(1, N_DEV):
                    arrival(k, d).wait_recv()
            for k in range(n):
                for d in range(1, N_DEV):
                    send(k, d).wait_send()
                local(k).wait()

        return start, finish


def _host_comm(comm, refs, n_in, n_out, n_scratch, grid):
    n = comm.n if comm is not None else 0
    own_in, cin = refs[:n_in], refs[n_in:n_in + n]
    own_out, cout = refs[n_in + n:n_in + n + n_out], refs[n_in + n + n_out:n_in + 2 * n + n_out]
    base = n_in + 2 * n + n_out
    own_scratch, sems = refs[base:base + n_scratch], refs[base + n_scratch:]
    own = tuple(own_in) + tuple(own_out) + tuple(own_scratch)
    if comm is None:
        return own, lambda: None, lambda: None
    start, finish = comm.bind(cin, cout, sems)
    first = last = None
    for ax, size in enumerate(grid):
        pid = pl.program_id(ax)
        first = (pid == 0) if first is None else first & (pid == 0)
        last = (pid == size - 1) if last is None else last & (pid == size - 1)
    return own, lambda: pl.when(first)(start), lambda: pl.when(last)(finish)


def _exchange(parts, gathers, name):
    comm = _Comm(parts, gathers)
    n = comm.n

    def body(*refs):
        start, finish = comm.bind(refs[:n], refs[n:2 * n], refs[2 * n:])
        start()
        finish()

    return pl.pallas_call(body, name=name, out_shape=comm.out_shape, in_specs=comm.in_specs, out_specs=comm.out_specs,
                          scratch_shapes=comm.scratch)(*comm.parts)


def _mm(a_list, b_list, *, name, out_dtype, tm, tn, tk, trans_a=False, add=None, add_scale=1.0):
    na = len(a_list)
    if trans_a:
        K, M = a_list[0].shape
    else:
        M, K = a_list[0].shape
    N = b_list[0].shape[1]
    tm, tn, tk = min(tm, M), min(tn, N), min(tk, K)
    assert M % tm == 0 and N % tn == 0 and K % tk == 0, (name, M, N, K, tm, tn, tk)
    nk = K // tk
    dims = (((0,), (0,)), ((), ())) if trans_a else (((1,), (0,)), ((), ()))

    def body(*refs):
        a_refs, b_refs = refs[:na], refs[na:2 * na]
        add_ref = refs[2 * na] if add is not None else None
        o_ref, acc_ref = refs[-2], refs[-1]
        k = pl.program_id(2)

        @pl.when(k == 0)
        def _():
            acc_ref[...] = jnp.zeros_like(acc_ref)

        part = None
        for a_ref, b_ref in zip(a_refs, b_refs):
            prod = lax.dot_general(a_ref[...].astype(BF16), b_ref[...].astype(BF16), dims,
                                   preferred_element_type=F32)
            part = prod if part is None else part + prod
        acc_ref[...] += part

        @pl.when(k == nk - 1)
        def _():
            res = acc_ref[...]
            if add_ref is not None:
                res = res + add_scale * add_ref[...]
            o_ref[...] = res.astype(o_ref.dtype)

    if trans_a:
        a_spec = pl.BlockSpec((tk, tm), lambda i, j, k: (k, i))
    else:
        a_spec = pl.BlockSpec((tm, tk), lambda i, j, k: (i, k))
    b_spec = pl.BlockSpec((tk, tn), lambda i, j, k: (k, j))
    o_spec = pl.BlockSpec((tm, tn), lambda i, j, k: (i, j))
    in_specs = [a_spec] * na + [b_spec] * na + ([o_spec] if add is not None else [])
    est = (2 * na * (_nbytes((tm, tk), a_list[0].dtype) + _nbytes((tk, tn), b_list[0].dtype))
           + na * (_nbytes((tm, tk), BF16) + _nbytes((tk, tn), BF16))
           + 2 * _nbytes((tm, tn), out_dtype) + 3 * _nbytes((tm, tn), F32)
           + (2 * _nbytes((tm, tn), F32) if add is not None else 0))
    args = list(a_list) + list(b_list) + ([add] if add is not None else [])
    return pl.pallas_call(
        body, name=name, grid=(M // tm, N // tn, nk),
        out_shape=jax.ShapeDtypeStruct((M, N), out_dtype),
        in_specs=in_specs, out_specs=o_spec,
        scratch_shapes=[pltpu.VMEM((tm, tn), F32)],
        compiler_params=_params(("parallel", "parallel", "arbitrary"), est),
    )(*args)


def _mm_res_ln(a, w, res_hat, res_g, res_b, ln_g, ln_b, *, name, tm):
    T, K = a.shape
    D = w.shape[1]
    tm = min(tm, T)

    def body(a_ref, w_ref, rh_ref, rg_ref, rb_ref, g_ref, b_ref, xhat_ref, rstd_ref, xb_ref):
        branch = jnp.dot(a_ref[...].astype(BF16), w_ref[...], preferred_element_type=F32)
        z = ALPHA * (rh_ref[...] * rg_ref[...] + rb_ref[...]) + branch
        mu = jnp.mean(z, axis=1, keepdims=True)
        zc = z - mu
        var = jnp.mean(zc * zc, axis=1, keepdims=True)
        rstd = lax.rsqrt(var + LN_EPS)
        xhat = zc * rstd
        xhat_ref[...] = xhat
        rstd_ref[...] = rstd
        xb_ref[...] = (xhat * g_ref[...] + b_ref[...]).astype(BF16)

    row = pl.BlockSpec((tm, D), lambda i: (i, 0))
    vec = pl.BlockSpec((1, D), lambda i: (0, 0))
    est = (2 * (_nbytes((tm, K), a.dtype) + _nbytes((K, D), BF16)) + 4 * _nbytes((tm, D), F32) * 2
           + 6 * _nbytes((tm, D), F32))
    return pl.pallas_call(
        body, name=name, grid=(T // tm,),
        out_shape=(jax.ShapeDtypeStruct((T, D), F32), jax.ShapeDtypeStruct((T, 1), F32),
                   jax.ShapeDtypeStruct((T, D), BF16)),
        in_specs=[pl.BlockSpec((tm, K), lambda i: (i, 0)), pl.BlockSpec((K, D), lambda i: (0, 0)), row, vec, vec, vec, vec],
        out_specs=(row, pl.BlockSpec((tm, 1), lambda i: (i, 0)), row),
        compiler_params=_params(("parallel",), est),
    )(a, w, res_hat, res_g, res_b, ln_g, ln_b)


def _ln_bwd(xhat, rstd, ln_g, ln_b, *, name, dx=None, target=None, tm=256):
    T, D = xhat.shape
    tm = min(tm, T)
    head = target is not None

    def body(xhat_ref, rstd_ref, g_ref, b_ref, d_ref, dz_ref, dzb_ref, st_ref):
        i = pl.program_id(0)

        @pl.when(i == 0)
        def _():
            st_ref[...] = jnp.zeros_like(st_ref)

        xh = xhat_ref[...]
        g = g_ref[...]
        if head:
            err = (xh * g + b_ref[...]) - d_ref[...]
            dxv = err * (1.0 / D)
            st_ref[2:3, :] += 0.5 * jnp.sum(jnp.sum(err * err, axis=1, keepdims=True) * (1.0 / D), axis=0, keepdims=True)
        else:
            dxv = d_ref[...]
        st_ref[0:1, :] += jnp.sum(dxv * xh, axis=0, keepdims=True)
        st_ref[1:2, :] += jnp.sum(dxv, axis=0, keepdims=True)
        dxh = dxv * g
        m1 = jnp.mean(dxh, axis=1, keepdims=True)
        m2 = jnp.mean(dxh * xh, axis=1, keepdims=True)
        dz = rstd_ref[...] * (dxh - m1 - xh * m2)
        dz_ref[...] = dz
        dzb_ref[...] = dz.astype(BF16)

    row = pl.BlockSpec((tm, D), lambda i: (i, 0))
    vec = pl.BlockSpec((1, D), lambda i: (0, 0))
    est = 2 * 4 * _nbytes((tm, D), F32) + 6 * _nbytes((tm, D), F32)
    return pl.pallas_call(
        body, name=name, grid=(T // tm,),
        out_shape=(jax.ShapeDtypeStruct((T, D), F32), jax.ShapeDtypeStruct((T, D), BF16),
                   jax.ShapeDtypeStruct((8, D), F32)),
        in_specs=[row, pl.BlockSpec((tm, 1), lambda i: (i, 0)), vec, vec, row],
        out_specs=(row, row, pl.BlockSpec((8, D), lambda i: (0, 0))),
        compiler_params=_params(("arbitrary",), est),
    )(xhat, rstd, ln_g, ln_b, target if head else dx)


def _pair_swap(v, even):
    return jnp.where(even, pltpu.roll(v, LANES - 1, 1), pltpu.roll(v, 1, 1))


def _half_sums(v, lo):
    s_lo = jnp.sum(jnp.where(lo, v, 0.0), axis=1, keepdims=True)
    s_hi = jnp.sum(jnp.where(lo, 0.0, v), axis=1, keepdims=True)
    return jnp.where(lo, s_lo, s_hi)


def _qk_rope_fwd(h, gains, cos2, sin2, *, name, tm=256):
    T = h.shape[0]
    tm = min(tm, T)
    nch = ROPE_W // LANES

    def body(h_ref, g_ref, c_ref, s_ref, o_ref):
        lane = lax.broadcasted_iota(jnp.int32, (tm, LANES), 1)
        lo, even = lane < HEAD_DIM, lane % 2 == 0
        c, s = c_ref[...], s_ref[...]
        for j in range(nch):
            x = h_ref[:, j * LANES:(j + 1) * LANES]
            isq = j < QW // LANES
            g = g_ref[0:1, :] if isq else g_ref[1:2, :]
            r = lax.rsqrt(_half_sums(x * x, lo) * (1.0 / HEAD_DIM) + RMS_EPS)
            nrm = x * r * g
            out = nrm * c + _pair_swap(nrm, even) * s
            if isq:
                out = out * (SCALE * LOG2E)
            o_ref[:, j * LANES:(j + 1) * LANES] = out.astype(BF16)

    est = 2 * (_nbytes((tm, ROPE_W), F32) + _nbytes((tm, ROPE_W), BF16) + 2 * _nbytes((tm, LANES), F32)) + (4 << 20)
    return pl.pallas_call(
        body, name=name, grid=(T // tm,),
        out_shape=jax.ShapeDtypeStruct((T, ROPE_W), BF16),
        in_specs=[pl.BlockSpec((tm, ROPE_W), lambda i: (i, 0)), pl.BlockSpec((8, LANES), lambda i: (0, 0)),
                  pl.BlockSpec((tm, LANES), lambda i: (i, 0)), pl.BlockSpec((tm, LANES), lambda i: (i, 0))],
        out_specs=pl.BlockSpec((tm, ROPE_W), lambda i: (i, 0)),
        compiler_params=_params(("parallel",), est),
    )(h, gains, cos2, sin2)


def _qk_rope_bwd(h, d_out, gains, cos2, sin2, *, name, tm=256):
    T = h.shape[0]
    tm = min(tm, T)
    nch = ROPE_W // LANES

    def body(h_ref, d_ref, g_ref, c_ref, s_ref, dh_ref, dg_ref):
        i = pl.program_id(0)

        @pl.when(i == 0)
        def _():
            dg_ref[...] = jnp.zeros_like(dg_ref)

        lane = lax.broadcasted_iota(jnp.int32, (tm, LANES), 1)
        lo, even = lane < HEAD_DIM, lane % 2 == 0
        c, s = c_ref[...], s_ref[...]
        acc = [None, None]
        for j in range(nch):
            x = h_ref[:, j * LANES:(j + 1) * LANES]
            isq = j < QW // LANES
            g = g_ref[0:1, :] if isq else g_ref[1:2, :]
            d = d_ref[:, j * LANES:(j + 1) * LANES]
            if isq:
                d = d * SCALE
            r = lax.rsqrt(_half_sums(x * x, lo) * (1.0 / HEAD_DIM) + RMS_EPS)
            dn = d * c + _pair_swap(d * s, even)
            xr = x * r
            part = jnp.sum(dn * xr, axis=0, keepdims=True)
            acc[0 if isq else 1] = part if acc[0 if isq else 1] is None else acc[0 if isq else 1] + part
            dng = dn * g
            dx = r * dng - xr * (r * r) * (_half_sums(dng * x, lo) * (1.0 / HEAD_DIM))
            dh_ref[:, j * LANES:(j + 1) * LANES] = dx.astype(BF16)
        for row in range(2):
            folded = acc[row] + pltpu.roll(acc[row], HEAD_DIM, 1)
            dg_ref[row:row + 1, :] += folded

    est = 2 * (2 * _nbytes((tm, ROPE_W), F32) + _nbytes((tm, ROPE_W), BF16) + 2 * _nbytes((tm, LANES), F32)) + (4 << 20)
    return pl.pallas_call(
        body, name=name, grid=(T // tm,),
        out_shape=(jax.ShapeDtypeStruct((T, ROPE_W), BF16), jax.ShapeDtypeStruct((8, LANES), F32)),
        in_specs=[pl.BlockSpec((tm, ROPE_W), lambda i: (i, 0)), pl.BlockSpec((tm, ROPE_W), lambda i: (i, 0)),
                  pl.BlockSpec((8, LANES), lambda i: (0, 0)),
                  pl.BlockSpec((tm, LANES), lambda i: (i, 0)), pl.BlockSpec((tm, LANES), lambda i: (i, 0))],
        out_specs=(pl.BlockSpec((tm, ROPE_W), lambda i: (i, 0)), pl.BlockSpec((8, LANES), lambda i: (0, 0))),
        compiler_params=_params(("arbitrary",), est),
    )(h, d_out, gains, cos2, sin2)


def _attn_a_fwd(k, qT, v1T, *, comm=None, tq=4096, tk=1024, cq=512):
    G, T, HD = k.shape
    NQ = qT.shape[2]
    HE = v1T.shape[1]
    tq, tk = min(tq, NQ), min(tk, T)
    cq = min(cq, tq)
    nk = T // tk
    grid = (G, NQ // tq, nk)

    def body(*refs):
        (k_ref, qT_ref, vT_ref, oT_ref, lse_ref, m_sc, acc_sc), comm_start, comm_finish = _host_comm(
            comm, refs, 3, 2, 2, grid)
        kv = pl.program_id(2)
        comm_start()

        @pl.when(kv == 0)
        def _():
            m_sc[...] = jnp.full_like(m_sc, NEG)
            acc_sc[...] = jnp.zeros_like(acc_sc)

        def scores(c):
            return jnp.dot(k_ref[...], qT_ref[:, c * cq:(c + 1) * cq], preferred_element_type=F32)

        nc = tq // cq
        ahead = scores(0)
        for c in range(nc):
            cols = slice(c * cq, (c + 1) * cq)
            sT = ahead
            if c + 1 < nc:
                ahead = scores(c + 1)
            m_prev = m_sc[:, cols]
            m_new = jnp.maximum(m_prev, jnp.max(sT, axis=0, keepdims=True))
            pT = jnp.exp2(sT - m_new).astype(BF16)
            acc_sc[:, cols] = (jnp.exp2(m_prev - m_new) * acc_sc[:, cols]
                               + jnp.dot(vT_ref[...], pT, preferred_element_type=F32))
            m_sc[:, cols] = m_new

        @pl.when(kv == nk - 1)
        def _():
            l = acc_sc[HD:HD + 1, :]
            oT_ref[...] = acc_sc[0:HD, :] / l
            lse_ref[...] = m_sc[...] + jnp.log2(l)

        comm_finish()

    qtr = pl.BlockSpec((None, HD, tq), lambda g, i, j: (g, 0, i))
    qvec = pl.BlockSpec((None, 1, tq), lambda g, i, j: (g, 0, i))
    est = 6 * _nbytes((cq, tk), F32) + (8 << 20)
    hosted = comm is not None
    return pl.pallas_call(
        body, name="attn_a_fwd_comm" if hosted else "attn_a_fwd", grid=grid,
        out_shape=[jax.ShapeDtypeStruct((G, HD, NQ), F32), jax.ShapeDtypeStruct((G, 1, NQ), F32)]
        + (comm.out_shape if hosted else []),
        in_specs=[pl.BlockSpec((None, tk, HD), lambda g, i, j: (g, j, 0)), qtr,
                  pl.BlockSpec((None, HE, tk), lambda g, i, j: (g, 0, j))] + (comm.in_specs if hosted else []),
        out_specs=[qtr, qvec] + (comm.out_specs if hosted else []),
        scratch_shapes=[pltpu.VMEM((1, tq), F32), pltpu.VMEM((HE, tq), F32)] + (comm.scratch if hosted else []),
        compiler_params=_params(("arbitrary",) * 3 if hosted else ("parallel", "parallel", "arbitrary"), est),
    )(k, qT, v1T, *(comm.parts if hosted else []))


def _attn_a_delta(oT, doT, *, tq=2048):
    G, HD, NQ = oT.shape
    tq = min(tq, NQ)

    def body(o_ref, d_ref, dl_ref):
        dl_ref[...] = jnp.sum(o_ref[...] * d_ref[...], axis=0, keepdims=True)

    qtr = pl.BlockSpec((None, HD, tq), lambda g, i: (g, 0, i))
    return pl.pallas_call(
        body, name="attn_a_delta", grid=(G, NQ // tq),
        out_shape=jax.ShapeDtypeStruct((G, 1, NQ), F32),
        in_specs=[qtr, qtr], out_specs=pl.BlockSpec((None, 1, tq), lambda g, i: (g, 0, i)),
        compiler_params=_params(("parallel", "parallel"), 8 << 20),
    )(oT, doT)


def _attn_a_bwd(k, v, kT, qT, doT, lse_row, delta_row, *, comm=None, tq=4096, tk=1024, cq=512):
    G, T, HD = k.shape
    NQ = qT.shape[2]
    tq, tk = min(tq, NQ), min(tk, T)
    cq = min(cq, tq)
    nq, nc = NQ // tq, tq // cq
    nt = (((1,), (1,)), ((), ()))

    grid = (G, T // tk, nq)

    def body(*refs):
        (k_ref, v_ref, kT_ref, qT_ref, doT_ref, lse_ref, dl_ref, dkT_ref, dvT_ref, dqT_ref, dk_sc, dv_sc), \
            comm_start, comm_finish = _host_comm(comm, refs, 7, 3, 2, grid)
        j, i = pl.program_id(1), pl.program_id(2)
        comm_start()

        @pl.when((j == 0) & (i == 0))
        def _():
            dqT_ref[...] = jnp.zeros_like(dqT_ref)

        @pl.when(i == 0)
        def _():
            dk_sc[...] = jnp.zeros_like(dk_sc)
            dv_sc[...] = jnp.zeros_like(dv_sc)

        def scores(c):
            cols = slice(c * cq, (c + 1) * cq)
            return (jnp.dot(k_ref[...], qT_ref[:, cols], preferred_element_type=F32),
                    jnp.dot(v_ref[...], doT_ref[:, cols], preferred_element_type=F32))

        ahead = scores(0)
        dk_part = dv_part = None
        for c in range(nc):
            cols = slice(c * cq, (c + 1) * cq)
            sT, dpT = ahead
            if c + 1 < nc:
                ahead = scores(c + 1)
            pT = jnp.exp2(sT - lse_ref[:, cols])
            dsT = (pT * (dpT - dl_ref[:, cols])).astype(BF16)
            dv_c = lax.dot_general(doT_ref[:, cols], pT.astype(BF16), nt, preferred_element_type=F32)
            dk_c = lax.dot_general(qT_ref[:, cols], dsT, nt, preferred_element_type=F32)
            dv_part = dv_c if dv_part is None else dv_part + dv_c
            dk_part = dk_c if dk_part is None else dk_part + dk_c
            out_cols = pl.ds(pl.multiple_of(i * tq + c * cq, cq), cq)
            dqT_ref[:, out_cols] += jnp.dot(kT_ref[...], dsT, preferred_element_type=F32)
        dk_sc[...] += dk_part
        dv_sc[...] += dv_part

        @pl.when(i == nq - 1)
        def _():
            dkT_ref[...] = dk_sc[...] * LN2
            dvT_ref[...] = dv_sc[...]

        comm_finish()

    krow = pl.BlockSpec((None, tk, HD), lambda g, j, i: (g, j, 0))
    ktr = pl.BlockSpec((None, HD, tk), lambda g, j, i: (g, 0, j))
    qtr = pl.BlockSpec((None, HD, tq), lambda g, j, i: (g, 0, i))
    qvec = pl.BlockSpec((None, 1, tq), lambda g, j, i: (g, 0, i))
    whole = pl.BlockSpec((None, HD, NQ), lambda g, j, i: (g, 0, 0))
    est = 8 * _nbytes((cq, tk), F32) + 2 * _nbytes((HD, NQ), F32) + (8 << 20)
    hosted = comm is not None
    return pl.pallas_call(
        body, name="attn_a_bwd_comm" if hosted else "attn_a_bwd", grid=grid,
        out_shape=[jax.ShapeDtypeStruct((G, HD, T), F32), jax.ShapeDtypeStruct((G, HD, T), F32),
                   jax.ShapeDtypeStruct((G, HD, NQ), F32)] + (comm.out_shape if hosted else []),
        in_specs=[krow, krow, ktr, qtr, qtr, qvec, qvec] + (comm.in_specs if hosted else []),
        out_specs=[ktr, ktr, whole] + (comm.out_specs if hosted else []),
        scratch_shapes=[pltpu.VMEM((HD, tk), F32), pltpu.VMEM((HD, tk), F32)] + (comm.scratch if hosted else []),
        compiler_params=_params(("arbitrary", "arbitrary", "arbitrary"), est),
    )(k, v, kT, qT, doT, lse_row, delta_row, *(comm.parts if hosted else []))


WB = WINDOW
WK = 3 * WINDOW


QB_COL0 = (ROPE_W + KW) // (GQ * HEAD_DIM)
KB_COL = (ROPE_W + KW + QW) // KW
GW = GQ * HEAD_DIM


def _win_in_specs(T):
    nb = T // WB
    q = [pl.BlockSpec((WB, GW), functools.partial(lambda n, g: (n, QB_COL0 + g), g=g)) for g in range(KV)]
    kv = [pl.BlockSpec((WB, KW), functools.partial(lambda n, o, c: (jnp.clip(n + o, 0, nb - 1), c), o=o, c=c))
          for c in (KB_COL, KB_COL + 1) for o in (-1, 0, 1)]
    return nb, q + kv


def _win_valid(n, T):
    kabs = n * WB - WB + lax.broadcasted_iota(jnp.int32, (WK, GQ * WB), 0)
    return (kabs >= 0) & (kabs < T)


def _heads_to_lanes(t):
    return jnp.concatenate([t[i * HEAD_DIM:(i + 1) * HEAD_DIM] for i in range(GQ)], axis=1)


def _lanes_to_heads(t):
    return jnp.concatenate([t[:, i * WB:(i + 1) * WB] for i in range(GQ)], axis=0)


def _attn_b_fwd(h, biasT, sink_rows):
    T = h.shape[0]
    nb, in_specs = _win_in_specs(T)

    def body(q0, q1, k0, k1, k2, v0, v1, v2, b_ref, sk_ref, o_ref, lse_ref):
        n = pl.program_id(0)
        kwin = jnp.concatenate([k0[...], k1[...], k2[...]], axis=0)
        vT = jnp.concatenate([v0[...], v1[...], v2[...]], axis=0).T
        valid = _win_valid(n, T)
        qT = [_heads_to_lanes((q[...] * SCALE).T).astype(BF16) for q in (q0, q1)]
        sT = [jnp.dot(kwin[:, g * HEAD_DIM:(g + 1) * HEAD_DIM].astype(BF16), qT[g], preferred_element_type=F32)
              for g in range(KV)]
        oT = []
        for g in range(KV):
            s = jnp.where(valid, sT[g] + b_ref[g], NEG)
            sk = sk_ref[g]
            m = jnp.maximum(jnp.max(s, axis=0, keepdims=True), sk)
            p = jnp.exp(s - m)
            den = jnp.sum(p, axis=0, keepdims=True) + jnp.exp(sk - m)
            o = jnp.dot(vT[g * HEAD_DIM:(g + 1) * HEAD_DIM].astype(BF16), p.astype(BF16),
                        preferred_element_type=F32) / den
            lse_ref[g] = m + jnp.log(den)
            oT.append(_lanes_to_heads(o))
        o_ref[...] = jnp.concatenate(oT, axis=0).T

    whole = lambda *shape: pl.BlockSpec(shape, lambda n: (0,) * len(shape))
    return pl.pallas_call(
        body, name="attn_b_fwd", grid=(nb,),
        out_shape=(jax.ShapeDtypeStruct((T, QW), F32), jax.ShapeDtypeStruct((nb, KV, 1, GQ * WB), F32)),
        in_specs=in_specs + [whole(KV, WK, GQ * WB), whole(KV, 1, GQ * WB)],
        out_specs=(pl.BlockSpec((WB, QW), lambda n: (n, 0)),
                   pl.BlockSpec((None, KV, 1, GQ * WB), lambda n: (n, 0, 0, 0))),
        compiler_params=_params(("parallel",), 24 << 20),
    )(*([h] * 8), biasT, sink_rows)


def _attn_b_bwd(h, do, o, lse, biasT, sink_rows):
    T = h.shape[0]
    nb, in_specs = _win_in_specs(T)
    Tp = T + 2 * WB
    nt = (((1,), (1,)), ((), ()))

    def body(q0, q1, k0, k1, k2, v0, v1, v2, do_ref, o_ref, lse_ref, b_ref, sk_ref,
             dq_ref, dk_ref, dv_ref, db_ref, dsk_ref):
        n = pl.program_id(0)

        @pl.when(n == 0)
        def _():
            dk_ref[...] = jnp.zeros_like(dk_ref)
            dv_ref[...] = jnp.zeros_like(dv_ref)
            db_ref[...] = jnp.zeros_like(db_ref)
            dsk_ref[...] = jnp.zeros_like(dsk_ref)

        kwin = jnp.concatenate([k0[...], k1[...], k2[...]], axis=0)
        vwin = jnp.concatenate([v0[...], v1[...], v2[...]], axis=0)
        kT = kwin.T
        valid = _win_valid(n, T)
        doT_all, oT_all = do_ref[...].T, o_ref[...].T
        qT, doT, delta, sT, dpT = [], [], [], [], []
        for g, q in enumerate((q0, q1)):
            hd = slice(g * HEAD_DIM, (g + 1) * HEAD_DIM)
            qT.append(_heads_to_lanes((q[...] * SCALE).T).astype(BF16))
            d = _heads_to_lanes(doT_all[g * GW:(g + 1) * GW])
            delta.append(jnp.sum(d * _heads_to_lanes(oT_all[g * GW:(g + 1) * GW]), axis=0, keepdims=True))
            doT.append(d.astype(BF16))
            sT.append(jnp.dot(kwin[:, hd].astype(BF16), qT[g], preferred_element_type=F32))
            dpT.append(jnp.dot(vwin[:, hd].astype(BF16), doT[g], preferred_element_type=F32))
        dq, dk, dv = [], [], []
        for g in range(KV):
            lse_g = lse_ref[g]
            p = jnp.exp(jnp.where(valid, sT[g] + b_ref[g], NEG) - lse_g)
            ds = p * (dpT[g] - delta[g])
            db_ref[g] += ds
            dsk_ref[g] -= jnp.exp(sk_ref[g] - lse_g) * delta[g]
            dsb = ds.astype(BF16)
            dqT = jnp.dot(kT[g * HEAD_DIM:(g + 1) * HEAD_DIM].astype(BF16), dsb, preferred_element_type=F32)
            dq.append(_lanes_to_heads(dqT))
            dk.append(lax.dot_general(dsb, qT[g], nt, preferred_element_type=F32))
            dv.append(lax.dot_general(p.astype(BF16), doT[g], nt, preferred_element_type=F32))
        dq_ref[...] = (jnp.concatenate(dq, axis=0).T * SCALE).astype(BF16)
        win = pl.ds(pl.multiple_of(n * WB, WB), WK)
        dk_ref[win, :] += jnp.concatenate(dk, axis=1)
        dv_ref[win, :] += jnp.concatenate(dv, axis=1)

    whole = lambda *shape: pl.BlockSpec(shape, lambda n: (0,) * len(shape))
    tok = pl.BlockSpec((WB, QW), lambda n: (n, 0))
    return pl.pallas_call(
        body, name="attn_b_bwd", grid=(nb,),
        out_shape=(jax.ShapeDtypeStruct((T, QW), BF16),
                   jax.ShapeDtypeStruct((Tp, KW), F32), jax.ShapeDtypeStruct((Tp, KW), F32),
                   jax.ShapeDtypeStruct((KV, WK, GQ * WB), F32), jax.ShapeDtypeStruct((KV, 1, GQ * WB), F32)),
        in_specs=in_specs + [tok, tok, pl.BlockSpec((None, KV, 1, GQ * WB), lambda n: (n, 0, 0, 0)),
                             whole(KV, WK, GQ * WB), whole(KV, 1, GQ * WB)],
        out_specs=(tok, whole(Tp, KW), whole(Tp, KW), whole(KV, WK, GQ * WB), whole(KV, 1, GQ * WB)),
        compiler_params=_params(("arbitrary",), 40 << 20),
    )(*([h] * 8), do, o, lse, biasT, sink_rows)


def _bias_table(rel_bias_t, bucket):
    nh, n = rel_bias_t.shape[0], bucket.shape[1]

    def body(rb_ref, bk_ref, o_ref):
        bk = bk_ref[...]
        out = jnp.full((nh, n), NEG, F32)
        for b in range(N_BUCKETS):
            out = jnp.where(bk == b, rb_ref[:, b:b + 1], out)
        o_ref[...] = out

    return pl.pallas_call(
        body, name="bias_table", out_shape=jax.ShapeDtypeStruct((nh, n), F32),
        compiler_params=pltpu.CompilerParams(vmem_limit_bytes=32 << 20),
    )(rel_bias_t, bucket)


def _bias_sink_grads(db_list, dsk_list, bucket):
    L = len(db_list)

    def body(*refs):
        db_refs, dsk_refs, bk_ref = refs[:L], refs[L:2 * L], refs[2 * L]
        drb_ref, dsink_ref = refs[2 * L + 1], refs[2 * L + 2]
        tot = db_refs[0][...]
        for r in db_refs[1:]:
            tot = tot + r[...]
        bk = bk_ref[...]
        lane = lax.broadcasted_iota(jnp.int32, (2 * GQ, N_BUCKETS), 1)
        out = jnp.zeros((2 * GQ, N_BUCKETS), F32)
        for b in range(N_BUCKETS):
            sb = jnp.sum(jnp.where(bk == b, tot, 0.0), axis=1, keepdims=True)
            out = jnp.where(lane == b, sb, out)
        drb_ref[...] = out
        for l in range(L):
            dsink_ref[l] = jnp.sum(dsk_refs[l][...], axis=1, keepdims=True)

    return pl.pallas_call(
        body, name="bias_sink_grads",
        out_shape=(jax.ShapeDtypeStruct((2 * GQ, N_BUCKETS), F32), jax.ShapeDtypeStruct((L, 2 * GQ, 1), F32)),
        compiler_params=pltpu.CompilerParams(vmem_limit_bytes=32 << 20),
    )(*db_list, *dsk_list, bucket)


def _outnorm_fwd(oa, ob, ga, gb, *, tm=512):
    T = oa.shape[0]
    tm = min(tm, T)

    def body(oa_ref, ob_ref, ga_ref, gb_ref, y_ref):
        for j, (o_ref, g_ref) in enumerate(((oa_ref, ga_ref), (ob_ref, gb_ref))):
            o = o_ref[...]
            r = lax.rsqrt(jnp.mean(o * o, axis=1, keepdims=True) + RMS_EPS)
            y_ref[:, j * QW:(j + 1) * QW] = (o * r * g_ref[...]).astype(BF16)

    half = pl.BlockSpec((tm, QW), lambda i: (i, 0))
    vec = pl.BlockSpec((1, QW), lambda i: (0, 0))
    return pl.pallas_call(
        body, name="outnorm_fwd", grid=(T // tm,),
        out_shape=jax.ShapeDtypeStruct((T, 2 * QW), BF16),
        in_specs=[half, half, vec, vec], out_specs=pl.BlockSpec((tm, 2 * QW), lambda i: (i, 0)),
        compiler_params=_params(("parallel",), 16 << 20),
    )(oa, ob, ga, gb)


def _outnorm_bwd(dy, oa, ob, ga, gb, *, tm=512):
    T = oa.shape[0]
    tm = min(tm, T)

    def body(dy_ref, oa_ref, ob_ref, ga_ref, gb_ref, doa_ref, dob_ref, dg_ref):
        i = pl.program_id(0)

        @pl.when(i == 0)
        def _():
            dg_ref[...] = jnp.zeros_like(dg_ref)

        for j, (o_ref, g_ref, d_ref) in enumerate(((oa_ref, ga_ref, doa_ref), (ob_ref, gb_ref, dob_ref))):
            o = o_ref[...]
            d = dy_ref[:, j * QW:(j + 1) * QW]
            r = lax.rsqrt(jnp.mean(o * o, axis=1, keepdims=True) + RMS_EPS)
            orr = o * r
            dg_ref[j:j + 1, :] += jnp.sum(d * orr, axis=0, keepdims=True)
            dgv = d * g_ref[...]
            d_ref[...] = r * dgv - orr * (r * r) * jnp.mean(dgv * o, axis=1, keepdims=True)

    half = pl.BlockSpec((tm, QW), lambda i: (i, 0))
    vec = pl.BlockSpec((1, QW), lambda i: (0, 0))
    return pl.pallas_call(
        body, name="outnorm_bwd", grid=(T // tm,),
        out_shape=(jax.ShapeDtypeStruct((T, QW), F32), jax.ShapeDtypeStruct((T, QW), F32),
                   jax.ShapeDtypeStruct((8, QW), F32)),
        in_specs=[pl.BlockSpec((tm, 2 * QW), lambda i: (i, 0)), half, half, vec, vec],
        out_specs=(half, half, pl.BlockSpec((8, QW), lambda i: (0, 0))),
        compiler_params=_params(("arbitrary",), 24 << 20),
    )(dy, oa, ob, ga, gb)


GELU_C = math.sqrt(2.0 / math.pi)
GELU_A = 0.044715
HALO = 16


def _gelu_parts(x):
    t = jnp.tanh(GELU_C * (x + GELU_A * (x * x * x)))
    return 0.5 * (1.0 + t), t


def _halo_specs(tm, tn, T):
    nh = tm // HALO
    last = T // HALO - 1
    cur = pl.BlockSpec((tm, tn), lambda j, i: (i, j))
    prev = pl.BlockSpec((HALO, tn), lambda j, i: (jnp.maximum(i * nh - 1, 0), j))
    nxt = pl.BlockSpec((HALO, tn), lambda j, i: (jnp.minimum((i + 1) * nh, last), j))
    return cur, prev, nxt


def _conv_glu_fwd(g, u, conv_w, conv_b, *, tm=256, tn=1408):
    T, F = g.shape
    tm, tn = min(tm, T), min(tn, F)
    cur, prev, nxt = _halo_specs(tm, tn, T)

    def body(g_ref, gp_ref, gn_ref, u_ref, w_ref, b_ref, a_ref):
        i = pl.program_id(1)
        gv = g_ref[...]
        row = lax.broadcasted_iota(jnp.int32, (tm, tn), 0)
        before = jnp.where(i * tm > 0, gp_ref[HALO - 1:HALO, :], 0.0)
        after = jnp.where((i + 1) * tm < T, gn_ref[0:1, :], 0.0)
        gm1 = jnp.where(row == 0, before, pltpu.roll(gv, 1, 0))
        gp1 = jnp.where(row == tm - 1, after, pltpu.roll(gv, tm - 1, 0))
        gc = ((b_ref[...] + gm1 * w_ref[0:1, :]) + gv * w_ref[1:2, :]) + gp1 * w_ref[2:3, :]
        cdf, _ = _gelu_parts(gc)
        a_ref[...] = (gc * cdf * u_ref[...].astype(F32)).astype(BF16)

    wspec = pl.BlockSpec((8, tn), lambda j, i: (0, j))
    est = 2 * (3 * _nbytes((tm, tn), F32)) + 8 * _nbytes((tm, tn), F32)
    return pl.pallas_call(
        body, name="conv_glu_fwd", grid=(F // tn, T // tm),
        out_shape=jax.ShapeDtypeStruct((T, F), BF16),
        in_specs=[cur, prev, nxt, cur, wspec, pl.BlockSpec((1, tn), lambda j, i: (0, j))],
        out_specs=cur,
        compiler_params=_params(("parallel", "parallel"), est),
    )(g, g, g, u, conv_w, conv_b)


def _conv_glu_bwd(dact, g, u, conv_w, conv_b, *, tm=256, tn=1408):
    T, F = g.shape
    tm, tn = min(tm, T), min(tn, F)
    cur, prev, nxt = _halo_specs(tm, tn, T)
    te = tm + 2 * HALO

    def body(d_ref, dp_ref, dn_ref, g_ref, gp_ref, gn_ref, u_ref, up_ref, un_ref, w_ref, b_ref,
             dg_ref, du_ref, dc_ref):
        i = pl.program_id(1)

        @pl.when(i == 0)
        def _():
            dc_ref[...] = jnp.zeros_like(dc_ref)

        grow = i * tm - HALO + lax.broadcasted_iota(jnp.int32, (te, tn), 0)
        valid = (grow >= 0) & (grow < T)
        ge = jnp.where(valid, jnp.concatenate([gp_ref[...], g_ref[...], gn_ref[...]], axis=0), 0.0)
        ue = jnp.concatenate([up_ref[...], u_ref[...], un_ref[...]], axis=0).astype(F32)
        de = jnp.concatenate([dp_ref[...], d_ref[...], dn_ref[...]], axis=0).astype(F32)
        w0, w1, w2 = w_ref[0:1, :], w_ref[1:2, :], w_ref[2:3, :]
        gm1 = pltpu.roll(ge, 1, 0)
        gp1 = pltpu.roll(ge, te - 1, 0)
        gc = ((b_ref[...] + gm1 * w0) + ge * w1) + gp1 * w2
        cdf, t = _gelu_parts(gc)
        dgelu = cdf + 0.5 * gc * (1.0 - t * t) * (GELU_C * (1.0 + 3.0 * GELU_A * (gc * gc)))
        dgc = jnp.where(valid, de * ue * dgelu, 0.0)
        dge = w0 * pltpu.roll(dgc, te - 1, 0) + w1 * dgc + w2 * pltpu.roll(dgc, 1, 0)
        mid = slice(HALO, HALO + tm)
        dg_ref[...] = dge[mid].astype(BF16)
        du_ref[...] = (de[mid] * (gc[mid] * cdf[mid])).astype(BF16)
        dgm = dgc[mid]
        dc_ref[0:1, :] += jnp.sum(dgm * gm1[mid], axis=0, keepdims=True)
        dc_ref[1:2, :] += jnp.sum(dgm * ge[mid], axis=0, keepdims=True)
        dc_ref[2:3, :] += jnp.sum(dgm * gp1[mid], axis=0, keepdims=True)
        dc_ref[3:4, :] += jnp.sum(dgm, axis=0, keepdims=True)

    wspec = pl.BlockSpec((8, tn), lambda j, i: (0, j))
    est = 2 * (3 * _nbytes((tm, tn), F32) + 2 * _nbytes((tm, tn), BF16)) + 16 * _nbytes((te, tn), F32)
    return pl.pallas_call(
        body, name="conv_glu_bwd", grid=(F // tn, T // tm),
        out_shape=(jax.ShapeDtypeStruct((T, F), BF16), jax.ShapeDtypeStruct((T, F), BF16),
                   jax.ShapeDtypeStruct((8, F), F32)),
        in_specs=[cur, prev, nxt, cur, prev, nxt, cur, prev, nxt, wspec, pl.BlockSpec((1, tn), lambda j, i: (0, j))],
        out_specs=(cur, cur, wspec),
        compiler_params=_params(("parallel", "arbitrary"), est),
    )(dact, dact, dact, g, g, g, u, u, u, conv_w, conv_b)


def _adamw_math(w, g, m, v):
    m = ADAM_B1 * m + (1.0 - ADAM_B1) * g
    v = ADAM_B2 * v + (1.0 - ADAM_B2) * (g * g)
    m_hat = m / (1.0 - ADAM_B1 ** ADAM_STEP)
    v_hat = v / (1.0 - ADAM_B2 ** ADAM_STEP)
    delta = -ADAM_LR * (m_hat / (jnp.sqrt(v_hat) + ADAM_EPS) + ADAM_WD * w)
    return delta, m, v


def _adamw(w, m, v, gparts, *, name, tr):
    R, C = w.shape
    tr = min(tr, R)
    assert R % tr == 0

    def body(w_ref, m_ref, v_ref, gp_ref, g_ref, d_ref, nm_ref, nv_ref):
        g = gp_ref[0].astype(F32)
        for j in range(1, N_DEV):
            g = g + gp_ref[j].astype(F32)
        delta, nm, nv = _adamw_math(w_ref[...], g, m_ref[...], v_ref[...])
        g_ref[...] = g
        d_ref[...] = delta
        nm_ref[...] = nm
        nv_ref[...] = nv

    blk = pl.BlockSpec((tr, C), lambda i: (i, 0))
    out = jax.ShapeDtypeStruct((R, C), F32)
    return pl.pallas_call(
        body, name=name, grid=(R // tr,), out_shape=(out, out, out, out),
        in_specs=[blk, blk, blk, pl.BlockSpec((N_DEV, tr, C), lambda i: (0, i, 0))],
        out_specs=(blk, blk, blk, blk),
        compiler_params=_params(("parallel",), 24 << 20),
    )(w, m, v, gparts)


def _rope_tables(T):
    rows_n = T // GRID_W
    row = jnp.repeat(jnp.arange(rows_n, dtype=F32), GRID_W)
    col = jnp.tile(jnp.arange(GRID_W, dtype=F32), rows_n)
    half = HEAD_DIM // 2
    inv_freq = ROPE_THETA ** (-jnp.arange(0, half, 2, dtype=F32) / half)
    ang = jnp.concatenate([row[:, None] * inv_freq, col[:, None] * inv_freq], axis=-1)
    cos, sin = jnp.cos(ang), jnp.sin(ang)
    cos64 = jnp.repeat(cos, 2, axis=-1)
    sin64 = jnp.stack([-sin, sin], axis=-1).reshape(T, HEAD_DIM)
    return jnp.tile(cos64, (1, 2)), jnp.tile(sin64, (1, 2))


def _t5_bucket(rel):
    half = N_BUCKETS // 2
    max_exact = half // 2
    bucket = jnp.where(rel > 0, half, 0)
    rp = jnp.abs(rel)
    rpf = jnp.maximum(rp, 1).astype(F32)
    large = max_exact + (jnp.log(rpf / max_exact) / math.log(MAX_DISTANCE / max_exact)
                         * (half - max_exact)).astype(jnp.int32)
    large = jnp.minimum(large, half - 1)
    return bucket + jnp.where(rp < max_exact, rp, large)


def _window_buckets():
    qpos = jnp.arange(WB, dtype=jnp.int32)
    kpos = jnp.arange(WK, dtype=jnp.int32) - WB
    rel = kpos[None, :] - qpos[:, None]
    return jnp.where(jnp.abs(rel) <= WINDOW, _t5_bucket(rel), -1)


def _heads_first(a, nh):
    T = a.shape[0]
    return a.reshape(T, nh, HEAD_DIM).transpose(1, 0, 2)


def _row(v):
    return v.reshape(1, -1)


def _rows8(rows, width):
    a = jnp.stack(list(rows), axis=0)
    return jnp.pad(a, ((0, 8 - a.shape[0]), (0, 0)))


def _layer_fwd(l, xin, W, tabs, comm=None, on_comm=None):
    xhat, xg, xb, x16 = xin
    T = xhat.shape[0]
    cos2, sin2, biasT = tabs
    h = _mm([x16], [W["w_in"][l]], name="mm_in", out_dtype=F32, tm=512, tn=IN_COLS, tk=D_MODEL)
    gains = _rows8([jnp.tile(W["q_norm"][l], 2), jnp.tile(W["k_norm"][l], 2)], LANES)
    roped = _qk_rope_fwd(h, gains, cos2, sin2, name="qk_rope_fwd")
    qa = _heads_first(roped[:, :QW], KV * GQ).reshape(KV, GQ * T, HEAD_DIM)
    ka = _heads_first(roped[:, QW:], KV)
    va = _heads_first(h[:, ROPE_W:ROPE_W + KW].astype(BF16), KV)
    kaT, vaT = ka.transpose(0, 2, 1), va.transpose(0, 2, 1)
    qaT = qa.transpose(0, 2, 1)
    res = _attn_a_fwd(ka, qaT, jnp.concatenate([vaT, jnp.ones((KV, ONES_ROWS, T), BF16)], axis=1), comm=comm)
    oaT, lse_a = res[0], res[1]
    if comm is not None:
        on_comm(res[2:])
    sink_rows = jnp.repeat(W["sink"][l], WB).reshape(KV, 1, GQ * WB)
    ob_t, lse_b = _attn_b_fwd(h, biasT, sink_rows)
    oa_t = oaT.reshape(KV, HEAD_DIM, GQ, T).transpose(3, 0, 2, 1).reshape(T, QW)
    ga, gb = _row(W["out_norm_a"][l]), _row(W["out_norm_b"][l])
    ycat = _outnorm_fwd(oa_t, ob_t, ga, gb)
    g1, b1 = _row(W["ln1_g"][l]), _row(W["ln1_b"][l])
    x1hat, rstd1, x1_16 = _mm_res_ln(ycat, W["w_out"][l], xhat, xg, xb, g1, b1, name="mm_out_ln", tm=512)
    gate = _mm([x1_16], [W["w_gate"][l]], name="mm_gate", out_dtype=F32, tm=512, tn=D_FF // 2, tk=D_MODEL)
    up = _mm([x1_16], [W["w_up"][l]], name="mm_up", out_dtype=BF16, tm=512, tn=D_FF // 2, tk=D_MODEL)
    cw = jnp.pad(W["conv_w"][l], ((0, 5), (0, 0)))
    cb = _row(W["conv_b"][l])
    act = _conv_glu_fwd(gate, up, cw, cb)
    g2, b2 = _row(W["ln2_g"][l]), _row(W["ln2_b"][l])
    x2hat, rstd2, x2_16 = _mm_res_ln(act, W["w_down"][l], x1hat, g1, b1, g2, b2, name="mm_down_ln", tm=256)
    saved = dict(x16=x16, h=h, gains=gains, qa=qa, qaT=qaT, ka=ka, kaT=kaT, va=va, oaT=oaT, lse_a=lse_a,
                 lse_b=lse_b, sink_rows=sink_rows, oa_t=oa_t, ob_t=ob_t,
                 ga=ga, gb=gb, ycat=ycat, x1hat=x1hat, rstd1=rstd1, x1_16=x1_16, g1=g1, b1=b1, gate=gate, up=up,
                 cw=cw, cb=cb, act=act, x2hat=x2hat, rstd2=rstd2, g2=g2, b2=b2)
    return (x2hat, g2, b2, x2_16), saved


def _layer_bwd(l, S, W, WT, tabs, dz2, dz2_16, stats2, scatter=None):
    cos2, sin2, biasT = tabs
    T = dz2.shape[0]
    G = {}
    G["ln2_g"], G["ln2_b"] = stats2[0], stats2[1]
    G["w_down"] = _mm([S["act"]], [dz2_16], name="dw_down", out_dtype=BF16, trans_a=True, tm=D_FF // 2, tn=D_MODEL, tk=512)
    dact = _mm([dz2_16], [WT["w_down"][l]], name="mm_dact", out_dtype=BF16, tm=512, tn=D_FF // 2, tk=D_MODEL)
    dg, du, dconv = _conv_glu_bwd(dact, S["gate"], S["up"], S["cw"], S["cb"])
    G["conv_w"], G["conv_b"] = dconv[0:3], dconv[3]
    G["w_gate"] = _mm([S["x1_16"]], [dg], name="dw_gate", out_dtype=BF16, trans_a=True, tm=D_MODEL, tn=D_FF // 2, tk=512)
    G["w_up"] = _mm([S["x1_16"]], [du], name="dw_up", out_dtype=BF16, trans_a=True, tm=D_MODEL, tn=D_FF // 2, tk=512)
    dx1 = _mm([dg, du], [WT["w_gate"][l], WT["w_up"][l]], name="mm_dx1", out_dtype=F32, tm=512, tn=D_MODEL,
              tk=D_FF // 2, add=dz2, add_scale=ALPHA)
    dz1, dz1_16, stats1 = _ln_bwd(S["x1hat"], S["rstd1"], S["g1"], S["b1"], name="ln1_bwd", dx=dx1)
    G["ln1_g"], G["ln1_b"] = stats1[0], stats1[1]
    G["w_out"] = _mm([S["ycat"]], [dz1_16], name="dw_out", out_dtype=BF16, trans_a=True, tm=D_MODEL, tn=D_MODEL, tk=512)
    dycat = _mm([dz1_16], [WT["w_out"][l]], name="mm_dycat", out_dtype=F32, tm=512, tn=D_MODEL, tk=D_MODEL)
    doa_t, dob_t, dgn = _outnorm_bwd(dycat, S["oa_t"], S["ob_t"], S["ga"], S["gb"])
    G["out_norm_a"], G["out_norm_b"] = dgn[0], dgn[1]
    doa = _heads_first(doa_t, KV * GQ).reshape(KV, GQ * T, HEAD_DIM)
    doaT = doa.transpose(0, 2, 1)
    delta = _attn_a_delta(S["oaT"], doaT)
    res = _attn_a_bwd(S["ka"], S["va"], S["kaT"], S["qaT"], doaT.astype(BF16), S["lse_a"], delta,
                      comm=scatter(G) if scatter is not None else None)
    dkaT, dvaT, dqaT = res[:3]
    dqa_t = dqaT.reshape(KV, HEAD_DIM, GQ, T).transpose(3, 0, 2, 1).reshape(T, QW)
    d_roped = jnp.concatenate([dqa_t, dkaT.transpose(2, 0, 1).reshape(T, KW)], axis=1)
    dh_rope, dgain = _qk_rope_bwd(S["h"], d_roped, S["gains"], cos2, sin2, name="qk_rope_bwd")
    G["q_norm"], G["k_norm"] = dgain[0, :HEAD_DIM], dgain[1, :HEAD_DIM]
    dqb_t, dkb, dvb, dbiasT, dsk = _attn_b_bwd(S["h"], dob_t, S["ob_t"], S["lse_b"], biasT, S["sink_rows"])
    dh = jnp.concatenate([
        dh_rope, dvaT.transpose(2, 0, 1).reshape(T, KW).astype(BF16), dqb_t,
        dkb[WB:WB + T].astype(BF16), dvb[WB:WB + T].astype(BF16)], axis=1)
    dbias = dbiasT.reshape(KV, WK, GQ, WB).transpose(0, 2, 1, 3)
    G["w_in"] = _mm([S["x16"]], [dh], name="dw_in", out_dtype=BF16, trans_a=True, tm=D_MODEL, tn=IN_COLS, tk=512)
    dxin = _mm([dh], [WT["w_in"][l]], name="mm_dxin", out_dtype=F32, tm=512, tn=D_MODEL, tk=IN_COLS,
               add=dz1, add_scale=ALPHA)
    return dxin, G, dbias.reshape(KV * GQ, WK * WB), dsk.reshape(KV * GQ, WB), res[3:]


BIG = ("w_in", "w_out", "w_gate", "w_up", "w_down")
COL_SHARDED = ("w_in", "w_gate", "w_up")
WIRE_COLS = 1024


def _unshard(name, gathered, shard_shape):
    _, r, c = shard_shape
    blocks = gathered.reshape(N_DEV, r, c)
    if name in COL_SHARDED:
        return blocks.transpose(1, 0, 2).reshape(r, N_DEV * c)
    return blocks.reshape(N_DEV * r, c)


def _to_owner_blocks(name, full, shard_shape):
    _, r, c = shard_shape
    if name in COL_SHARDED:
        blocks = full.reshape(r, N_DEV, c).transpose(1, 0, 2)
    else:
        blocks = full.reshape(N_DEV, r, c)
    return blocks.reshape(N_DEV, -1, WIRE_COLS)


def _pack_small(vals):
    flat = jnp.concatenate([vals[n].reshape(-1).astype(F32) for n in SMALL_NAMES])
    pad = (-flat.shape[0]) % (8 * LANES)
    return jnp.pad(flat, (0, pad)).reshape(-1, LANES)


def _unpack_small(packed, shapes):
    flat = packed.reshape(-1)
    out, off = {}, 0
    for n in SMALL_NAMES:
        size = math.prod(shapes[n])
        out[n] = flat[off:off + size].reshape(shapes[n])
        off += size
    return out


def kernel(x, rel_bias, w_in, q_norm, k_norm, sink, out_norm_a, out_norm_b, w_out, ln1_g, ln1_b, w_gate, w_up, conv_w, conv_b, w_down, ln2_g, ln2_b, loss_target, m_rel_bias, m_w_in, m_q_norm, m_k_norm, m_sink, m_out_norm_a, m_out_norm_b, m_w_out, m_ln1_g, m_ln1_b, m_w_gate, m_w_up, m_conv_w, m_conv_b, m_w_down, m_ln2_g, m_ln2_b, v_rel_bias, v_w_in, v_q_norm, v_k_norm, v_sink, v_out_norm_a, v_out_norm_b, v_w_out, v_ln1_g, v_ln1_b, v_w_gate, v_w_up, v_conv_w, v_conv_b, v_w_down, v_ln2_g, v_ln2_b):
    P = dict(rel_bias=rel_bias, w_in=w_in, q_norm=q_norm, k_norm=k_norm, sink=sink, out_norm_a=out_norm_a,
             out_norm_b=out_norm_b, w_out=w_out, ln1_g=ln1_g, ln1_b=ln1_b, w_gate=w_gate, w_up=w_up, conv_w=conv_w,
             conv_b=conv_b, w_down=w_down, ln2_g=ln2_g, ln2_b=ln2_b)
    M = dict(rel_bias=m_rel_bias, w_in=m_w_in, q_norm=m_q_norm, k_norm=m_k_norm, sink=m_sink, out_norm_a=m_out_norm_a,
             out_norm_b=m_out_norm_b, w_out=m_w_out, ln1_g=m_ln1_g, ln1_b=m_ln1_b, w_gate=m_w_gate, w_up=m_w_up,
             conv_w=m_conv_w, conv_b=m_conv_b, w_down=m_w_down, ln2_g=m_ln2_g, ln2_b=m_ln2_b)
    V = dict(rel_bias=v_rel_bias, w_in=v_w_in, q_norm=v_q_norm, k_norm=v_k_norm, sink=v_sink, out_norm_a=v_out_norm_a,
             out_norm_b=v_out_norm_b, w_out=v_w_out, ln1_g=v_ln1_g, ln1_b=v_ln1_b, w_gate=v_w_gate, w_up=v_w_up,
             conv_w=v_conv_w, conv_b=v_conv_b, w_down=v_w_down, ln2_g=v_ln2_g, ln2_b=v_ln2_b)
    names = list(P)
    T = x.shape[1]
    me = 4 * lax.axis_index("x") + 2 * lax.axis_index("y") + lax.axis_index("c")

    L, taps, fc = conv_w.shape
    W = {n: ([None] * DEPTH if n in BIG else P[n]) for n in names}
    WT = {n: [None] * DEPTH for n in BIG}

    def wire(n, l):
        return P[n][l].astype(BF16).reshape(-1, WIRE_COLS)

    def take(n, l, gathered):
        W[n][l] = _unshard(n, gathered, P[n].shape)
        WT[n][l] = W[n][l].T

    take("w_in", 0, _exchange([wire("w_in", 0)], [True], name="gather_w_in0")[0])
    later = [(n, l) for l in range(DEPTH) for n in BIG if (n, l) != ("w_in", 0)]
    cw_shard = conv_w.reshape(-1)
    cw_wire = jnp.pad(cw_shard, (0, (-cw_shard.shape[0]) % LANES)).reshape(-1, LANES)
    gather_rest = _Comm([wire(n, l) for n, l in later] + [cw_wire], [True] * (len(later) + 1))

    def on_gathered(outs):
        for (n, l), g in zip(later, outs):
            take(n, l, g)
        cw_all = outs[-1].reshape(N_DEV, -1)[:, :cw_shard.shape[0]].reshape(N_DEV, L, taps, fc)
        W["conv_w"] = cw_all.transpose(1, 2, 0, 3).reshape(L, taps, N_DEV * fc)

    cos2, sin2 = _rope_tables(T)
    bucket = _window_buckets()
    bias = _bias_table(rel_bias.T, bucket.reshape(1, WB * WK))
    biasT = bias.reshape(KV, GQ, WB, WK).transpose(0, 3, 1, 2).reshape(KV, WK, GQ * WB)
    tabs = (cos2, sin2, biasT)

    ones, zeros = jnp.ones((1, D_MODEL), F32), jnp.zeros((1, D_MODEL), F32)
    cur = (x[0], ones, zeros, x[0].astype(BF16))
    saved = []
    for l in range(DEPTH):
        cur, S = _layer_fwd(l, cur, W, tabs, comm=gather_rest if l == 0 else None, on_comm=on_gathered)
        saved.append(S)

    def owner_blocks(n, l):
        return _to_owner_blocks(n, grads[l][n], P[n].shape)

    early = ([(n, l) for l in range(1, DEPTH) for n in BIG] + [(n, 0) for n in BIG if n != "w_in"])

    def scatter_early(g0):
        grads[0] = g0
        return _Comm([owner_blocks(n, l) for n, l in early], [False] * len(early))

    grads = [None] * DEPTH
    dbs, dsks = [None] * DEPTH, [None] * DEPTH
    S = saved[-1]
    dz, dz16, stats = _ln_bwd(S["x2hat"], S["rstd2"], S["g2"], S["b2"], name="loss_ln2_bwd", target=loss_target[0])
    loss = lax.psum(stats[2, 0], ("x", "y", "c"))
    recv = {}
    for l in reversed(range(DEPTH)):
        S = saved[l]
        dxin, grads[l], dbs[l], dsks[l], got = _layer_bwd(l, S, W, WT, tabs, dz, dz16, stats,
                                                         scatter=scatter_early if l == 0 else None)
        if l == 0:
            recv.update(zip(early, got))
        if l > 0:
            Sp = saved[l - 1]
            dz, dz16, stats = _ln_bwd(Sp["x2hat"], Sp["rstd2"], Sp["g2"], Sp["b2"], name="ln2_bwd", dx=dxin)
    grad_x = dxin[None]

    drb, dsink = _bias_sink_grads(dbs, dsks, bucket.T.reshape(1, WK * WB))
    small_g = {n: jnp.stack([grads[l][n] for l in range(DEPTH)]) for n in SMALL_NAMES if n not in ("rel_bias", "sink")}
    small_g["rel_bias"] = drb.T
    small_g["sink"] = dsink.reshape(DEPTH, KV * GQ)
    recv[("w_in", 0)], small_recv = _exchange([owner_blocks("w_in", 0), _pack_small(small_g)], [False, True],
                                              name="scatter_w_in0_gather_small")

    out_g, out_d, out_m, out_v = {}, {}, {}, {}
    for n in BIG:
        shp = P[n].shape
        gparts = jnp.concatenate([recv[(n, l)] for l in range(DEPTH)], axis=1)
        res = _adamw(P[n].reshape(-1, WIRE_COLS), M[n].reshape(-1, WIRE_COLS), V[n].reshape(-1, WIRE_COLS), gparts,
                     name="adamw_" + n, tr=64)
        out_g[n], out_d[n], out_m[n], out_v[n] = (r.reshape(shp) for r in res)
    full_shapes = {n: W[n].shape for n in SMALL_NAMES}

    def small_state(D):
        vals = {n: D[n] for n in SMALL_NAMES if n != "conv_w"}
        cw = jnp.zeros((L, taps, N_DEV, fc), F32)
        cw = lax.dynamic_update_slice(cw, D["conv_w"].reshape(L, taps, 1, fc), (0, 0, me, 0))
        vals["conv_w"] = cw.reshape(L, taps, N_DEV * fc)
        return _pack_small(vals)

    sw, sm, sv = small_state(P), small_state(M), small_state(V)
    res = _adamw(sw, sm, sv, small_recv, name="adamw_small", tr=sw.shape[0])
    for dst, packed in zip((out_g, out_d, out_m, out_v), res):
        vals = _unpack_small(packed, full_shapes)
        for n in SMALL_NAMES:
            if n == "conv_w":
                sl = lax.dynamic_slice(vals[n].reshape(L, taps, N_DEV, fc), (0, 0, me, 0), (L, taps, 1, fc))
                dst[n] = sl.reshape(L, taps, fc)
            else:
                dst[n] = vals[n]
    return (loss, grad_x, *[out_g[n] for n in names], *[out_d[n] for n in names],
            *[out_m[n] for n in names], *[out_v[n] for n in names])
```

```python
import functools
import math

import jax
import jax.numpy as jnp
from jax import lax
from jax.experimental import pallas as pl
from jax.experimental.pallas import tpu as pltpu

F32 = jnp.float32
BF16 = jnp.bfloat16
MESH = pl.DeviceIdType.MESH

N_DEV = 8
D_MODEL = 1024
DEPTH = 2
HEAD_DIM = 64
KV = 2
GQ = 4
QW = KV * GQ * HEAD_DIM
KW = KV * HEAD_DIM
ROPE_W = QW + KW
IN_COLS = 2 * (QW + 2 * KW)
D_FF = 2816
GRID_W = 64
ROPE_THETA = 10000.0
WINDOW = 128
N_BUCKETS = 32
MAX_DISTANCE = 128
ALPHA = (2.0 * DEPTH) ** 0.25
RMS_EPS = 1e-6
LN_EPS = 1e-5
SCALE = HEAD_DIM ** -0.5
LOG2E = math.log2(math.e)
LN2 = math.log(2.0)
NEG = -1e30
ONES_ROWS = 16

ADAM_LR = 0.001
ADAM_B1 = 0.9
ADAM_B2 = 0.999
ADAM_EPS = 1e-08
ADAM_WD = 0.01
ADAM_STEP = 10

LANES = 128
VMEM_CAP = 60 * 1024 * 1024
SMALL_NAMES = ("rel_bias", "q_norm", "k_norm", "sink", "out_norm_a", "out_norm_b", "ln1_g", "ln1_b",
               "conv_b", "ln2_g", "ln2_b", "conv_w")


def _params(sem, est_bytes):
    limit = int(min(VMEM_CAP, est_bytes + (8 << 20)))
    return pltpu.CompilerParams(dimension_semantics=sem, vmem_limit_bytes=limit)


def _nbytes(shape, dtype):
    return math.prod(shape) * jnp.dtype(dtype).itemsize


class _Comm:
    def __init__(self, parts, gathers):
        self.parts, self.gathers, self.n = list(parts), list(gathers), len(parts)
        hbm = pl.BlockSpec(memory_space=pltpu.HBM)
        self.in_specs = [hbm] * self.n
        self.out_specs = [hbm] * self.n
        self.out_shape = [jax.ShapeDtypeStruct((N_DEV,) + tuple(p.shape if g else p.shape[1:]), p.dtype)
                          for p, g in zip(self.parts, self.gathers)]
        self.scratch = [pltpu.SemaphoreType.DMA((self.n * (N_DEV - 1),)), pltpu.SemaphoreType.DMA((self.n * (N_DEV - 1),)),
                        pltpu.SemaphoreType.DMA((self.n,))]

    def bind(self, ins, outs, sems):
        send_sems, recv_sems, local_sems = sems
        gathers, n = self.gathers, self.n
        me = 4 * lax.axis_index("x") + 2 * lax.axis_index("y") + lax.axis_index("c")

        def src(k, j):
            return ins[k] if gathers[k] else ins[k].at[j]

        def copy(k, d, peer, lands_in):
            return pltpu.make_async_remote_copy(
                src_ref=src(k, peer), dst_ref=outs[k].at[lands_in],
                send_sem=send_sems.at[k * (N_DEV - 1) + d - 1], recv_sem=recv_sems.at[k * (N_DEV - 1) + d - 1],
                device_id=(peer // 4, lax.rem(peer // 2, 2), lax.rem(peer, 2)), device_id_type=MESH)

        def send(k, d):
            return copy(k, d, lax.rem(me + d, N_DEV), me)

        def arrival(k, d):
            frm = lax.rem(me + N_DEV - d, N_DEV)
            return copy(k, d, frm, frm)

        def local(k):
            return pltpu.make_async_copy(src(k, me), outs[k].at[me], local_sems.at[k])

        def start():
            for k in range(n):
                local(k).start()
                for d in range(1, N_DEV):
                    send(k, d).start()

        def finish():
            for k in range(n):
                for d in range(1, N_DEV):
                    arrival(k, d).wait_recv()
            for k in range(n):
                for d in range(1, N_DEV):
                    send(k, d).wait_send()
                local(k).wait()

        return start, finish


def _host_comm(comm, refs, n_in, n_out, n_scratch, grid):
    n = comm.n if comm is not None else 0
    own_in, cin = refs[:n_in], refs[n_in:n_in + n]
    own_out, cout = refs[n_in + n:n_in + n + n_out], refs[n_in + n + n_out:n_in + 2 * n + n_out]
    base = n_in + 2 * n + n_out
    own_scratch, sems = refs[base:base + n_scratch], refs[base + n_scratch:]
    own = tuple(own_in) + tuple(own_out) + tuple(own_scratch)
    if comm is None:
        return own, lambda: None, lambda: None
    start, finish = comm.bind(cin, cout, sems)
    first = last = None
    for ax, size in enumerate(grid):
        pid = pl.program_id(ax)
        first = (pid == 0) if first is None else first & (pid == 0)
        last = (pid == size - 1) if last is None else last & (pid == size - 1)
    return own, lambda: pl.when(first)(start), lambda: pl.when(last)(finish)


def _exchange(parts, gathers, name):
    comm = _Comm(parts, gathers)
    n = comm.n

    def body(*refs):
        start, finish = comm.bind(refs[:n], refs[n:2 * n], refs[2 * n:])
        start()
        finish()

    return pl.pallas_call(body, name=name, out_shape=comm.out_shape, in_specs=comm.in_specs, out_specs=comm.out_specs,
                          scratch_shapes=comm.scratch)(*comm.parts)


def _mm(a_list, b_list, *, name, out_dtype, tm, tn, tk, trans_a=False, add=None, add_scale=1.0):
    na = len(a_list)
    if trans_a:
        K, M = a_list[0].shape
    else:
        M, K = a_list[0].shape
    N = b_list[0].shape[1]
    tm, tn, tk = min(tm, M), min(tn, N), min(tk, K)
    assert M % tm == 0 and N % tn == 0 and K % tk == 0, (name, M, N, K, tm, tn, tk)
    nk = K // tk
    dims = (((0,), (0,)), ((), ())) if trans_a else (((1,), (0,)), ((), ()))

    def body(*refs):
        a_refs, b_refs = refs[:na], refs[na:2 * na]
        add_ref = refs[2 * na] if add is not None else None
        o_ref, acc_ref = refs[-2], refs[-1]
        k = pl.program_id(2)

        @pl.when(k == 0)
        def _():
            acc_ref[...] = jnp.zeros_like(acc_ref)

        part = None
        for a_ref, b_ref in zip(a_refs, b_refs):
            prod = lax.dot_general(a_ref[...].astype(BF16), b_ref[...].astype(BF16), dims,
                                   preferred_element_type=F32)
            part = prod if part is None else part + prod
        acc_ref[...] += part

        @pl.when(k == nk - 1)
        def _():
            res = acc_ref[...]
            if add_ref is not None:
                res = res + add_scale * add_ref[...]
            o_ref[...] = res.astype(o_ref.dtype)

    if trans_a:
        a_spec = pl.BlockSpec((tk, tm), lambda i, j, k: (k, i))
    else:
        a_spec = pl.BlockSpec((tm, tk), lambda i, j, k: (i, k))
    b_spec = pl.BlockSpec((tk, tn), lambda i, j, k: (k, j))
    o_spec = pl.BlockSpec((tm, tn), lambda i, j, k: (i, j))
    in_specs = [a_spec] * na + [b_spec] * na + ([o_spec] if add is not None else [])
    est = (2 * na * (_nbytes((tm, tk), a_list[0].dtype) + _nbytes((tk, tn), b_list[0].dtype))
           + na * (_nbytes((tm, tk), BF16) + _nbytes((tk, tn), BF16))
           + 2 * _nbytes((tm, tn), out_dtype) + 3 * _nbytes((tm, tn), F32)
           + (2 * _nbytes((tm, tn), F32) if add is not None else 0))
    args = list(a_list) + list(b_list) + ([add] if add is not None else [])
    return pl.pallas_call(
        body, name=name, grid=(M // tm, N // tn, nk),
        out_shape=jax.ShapeDtypeStruct((M, N), out_dtype),
        in_specs=in_specs, out_specs=o_spec,
        scratch_shapes=[pltpu.VMEM((tm, tn), F32)],
        compiler_params=_params(("parallel", "parallel", "arbitrary"), est),
    )(*args)


def _mm_res_ln(a, w, res_hat, res_g, res_b, ln_g, ln_b, *, name, tm):
    T, K = a.shape
    D = w.shape[1]
    tm = min(tm, T)

    def body(a_ref, w_ref, rh_ref, rg_ref, rb_ref, g_ref, b_ref, xhat_ref, rstd_ref, xb_ref):
        branch = jnp.dot(a_ref[...].astype(BF16), w_ref[...], preferred_element_type=F32)
        z = ALPHA * (rh_ref[...] * rg_ref[...] + rb_ref[...]) + branch
        mu = jnp.mean(z, axis=1, keepdims=True)
        zc = z - mu
        var = jnp.mean(zc * zc, axis=1, keepdims=True)
        rstd = lax.rsqrt(var + LN_EPS)
        xhat = zc * rstd
        xhat_ref[...] = xhat
        rstd_ref[...] = rstd
        xb_ref[...] = (xhat * g_ref[...] + b_ref[...]).astype(BF16)

    row = pl.BlockSpec((tm, D), lambda i: (i, 0))
    vec = pl.BlockSpec((1, D), lambda i: (0, 0))
    est = (2 * (_nbytes((tm, K), a.dtype) + _nbytes((K, D), BF16)) + 4 * _nbytes((tm, D), F32) * 2
           + 6 * _nbytes((tm, D), F32))
    return pl.pallas_call(
        body, name=name, grid=(T // tm,),
        out_shape=(jax.ShapeDtypeStruct((T, D), F32), jax.ShapeDtypeStruct((T, 1), F32),
                   jax.ShapeDtypeStruct((T, D), BF16)),
        in_specs=[pl.BlockSpec((tm, K), lambda i: (i, 0)), pl.BlockSpec((K, D), lambda i: (0, 0)), row, vec, vec, vec, vec],
        out_specs=(row, pl.BlockSpec((tm, 1), lambda i: (i, 0)), row),
        compiler_params=_params(("parallel",), est),
    )(a, w, res_hat, res_g, res_b, ln_g, ln_b)


def _ln_bwd(xhat, rstd, ln_g, ln_b, *, name, dx=None, target=None, tm=256):
    T, D = xhat.shape
    tm = min(tm, T)
    head = target is not None

    def body(xhat_ref, rstd_ref, g_ref, b_ref, d_ref, dz_ref, dzb_ref, st_ref):
        i = pl.program_id(0)

        @pl.when(i == 0)
        def _():
            st_ref[...] = jnp.zeros_like(st_ref)

        xh = xhat_ref[...]
        g = g_ref[...]
        if head:
            err = (xh * g + b_ref[...]) - d_ref[...]
            dxv = err * (1.0 / D)
            st_ref[2:3, :] += 0.5 * jnp.sum(jnp.sum(err * err, axis=1, keepdims=True) * (1.0 / D), axis=0, keepdims=True)
        else:
            dxv = d_ref[...]
        st_ref[0:1, :] += jnp.sum(dxv * xh, axis=0, keepdims=True)
        st_ref[1:2, :] += jnp.sum(dxv, axis=0, keepdims=True)
        dxh = dxv * g
        m1 = jnp.mean(dxh, axis=1, keepdims=True)
        m2 = jnp.mean(dxh * xh, axis=1, keepdims=True)
        dz = rstd_ref[...] * (dxh - m1 - xh * m2)
        dz_ref[...] = dz
        dzb_ref[...] = dz.astype(BF16)

    row = pl.BlockSpec((tm, D), lambda i: (i, 0))
    vec = pl.BlockSpec((1, D), lambda i: (0, 0))
    est = 2 * 4 * _nbytes((tm, D), F32) + 6 * _nbytes((tm, D), F32)
    return pl.pallas_call(
        body, name=name, grid=(T // tm,),
        out_shape=(jax.ShapeDtypeStruct((T, D), F32), jax.ShapeDtypeStruct((T, D), BF16),
                   jax.ShapeDtypeStruct((8, D), F32)),
        in_specs=[row, pl.BlockSpec((tm, 1), lambda i: (i, 0)), vec, vec, row],
        out_specs=(row, row, pl.BlockSpec((8, D), lambda i: (0, 0))),
        compiler_params=_params(("arbitrary",), est),
    )(xhat, rstd, ln_g, ln_b, target if head else dx)


def _pair_swap(v, even):
    return jnp.where(even, pltpu.roll(v, LANES - 1, 1), pltpu.roll(v, 1, 1))


def _half_sums(v, lo):
    s_lo = jnp.sum(jnp.where(lo, v, 0.0), axis=1, keepdims=True)
    s_hi = jnp.sum(jnp.where(lo, 0.0, v), axis=1, keepdims=True)
    return jnp.where(lo, s_lo, s_hi)


A_COLS = ROPE_W + KW
A_HEADS = A_COLS // HEAD_DIM
A_K0, A_V0 = KV * GQ, KV * GQ + KV


def _qk_rope_fwd(h, gains, cos2, sin2, *, name, tm=256):
    T = h.shape[0]
    tm = min(tm, T)
    nch = A_COLS // LANES

    def body(h_ref, g_ref, c_ref, s_ref, oT_ref, kv_ref):
        lane = lax.broadcasted_iota(jnp.int32, (tm, LANES), 1)
        lo, even = lane < HEAD_DIM, lane % 2 == 0
        c, s = c_ref[...], s_ref[...]
        for j in range(nch):
            x = h_ref[:, j * LANES:(j + 1) * LANES]
            isq, isv = j < QW // LANES, j == nch - 1
            if isv:
                out = x
            else:
                g = g_ref[0:1, :] if isq else g_ref[1:2, :]
                r = lax.rsqrt(_half_sums(x * x, lo) * (1.0 / HEAD_DIM) + RMS_EPS)
                nrm = x * r * g
                out = nrm * c + _pair_swap(nrm, even) * s
            if isq:
                out = out * (SCALE * LOG2E)
            else:
                kv_ref[:, (j - QW // LANES) * LANES:(j - QW // LANES + 1) * LANES] = out.astype(BF16)
            oT_ref[j * LANES:(j + 1) * LANES, :] = out.T.astype(BF16)

    est = 2 * (_nbytes((tm, A_COLS), F32) + 2 * _nbytes((tm, A_COLS), BF16) + 2 * _nbytes((tm, LANES), F32)) + (4 << 20)
    return pl.pallas_call(
        body, name=name, grid=(T // tm,),
        out_shape=(jax.ShapeDtypeStruct((A_COLS, T), BF16), jax.ShapeDtypeStruct((T, 2 * KW), BF16)),
        in_specs=[pl.BlockSpec((tm, A_COLS), lambda i: (i, 0)), pl.BlockSpec((8, LANES), lambda i: (0, 0)),
                  pl.BlockSpec((tm, LANES), lambda i: (i, 0)), pl.BlockSpec((tm, LANES), lambda i: (i, 0))],
        out_specs=(pl.BlockSpec((A_COLS, tm), lambda i: (0, i)), pl.BlockSpec((tm, 2 * KW), lambda i: (i, 0))),
        compiler_params=_params(("parallel",), est),
    )(h, gains, cos2, sin2)


def _qk_rope_bwd(h, dqT, dkT, gains, cos2, sin2, *, name, tm=256):
    T = h.shape[0]
    tm = min(tm, T)
    nch = ROPE_W // LANES

    def body(h_ref, dq_ref, dk_ref, g_ref, c_ref, s_ref, dh_ref, dg_ref):
        i = pl.program_id(0)

        @pl.when(i == 0)
        def _():
            dg_ref[...] = jnp.zeros_like(dg_ref)

        lane = lax.broadcasted_iota(jnp.int32, (tm, LANES), 1)
        lo, even = lane < HEAD_DIM, lane % 2 == 0
        c, s = c_ref[...], s_ref[...]
        acc = [None, None]
        for j in range(nch):
            x = h_ref[:, j * LANES:(j + 1) * LANES]
            isq = j < QW // LANES
            g = g_ref[0:1, :] if isq else g_ref[1:2, :]
            d = dq_ref[j * LANES:(j + 1) * LANES, :].T * SCALE if isq else dk_ref[...].T
            r = lax.rsqrt(_half_sums(x * x, lo) * (1.0 / HEAD_DIM) + RMS_EPS)
            dn = d * c + _pair_swap(d * s, even)
            xr = x * r
            part = jnp.sum(dn * xr, axis=0, keepdims=True)
            acc[0 if isq else 1] = part if acc[0 if isq else 1] is None else acc[0 if isq else 1] + part
            dng = dn * g
            dx = r * dng - xr * (r * r) * (_half_sums(dng * x, lo) * (1.0 / HEAD_DIM))
            dh_ref[:, j * LANES:(j + 1) * LANES] = dx.astype(BF16)
        for row in range(2):
            folded = acc[row] + pltpu.roll(acc[row], HEAD_DIM, 1)
            dg_ref[row:row + 1, :] += folded

    est = 2 * (2 * _nbytes((tm, ROPE_W), F32) + _nbytes((tm, ROPE_W), BF16) + 2 * _nbytes((tm, LANES), F32)) + (4 << 20)
    return pl.pallas_call(
        body, name=name, grid=(T // tm,),
        out_shape=(jax.ShapeDtypeStruct((T, ROPE_W), BF16), jax.ShapeDtypeStruct((8, LANES), F32)),
        in_specs=[pl.BlockSpec((tm, ROPE_W), lambda i: (i, 0)), pl.BlockSpec((QW, tm), lambda i: (0, i)),
                  pl.BlockSpec((KW, tm), lambda i: (0, i)), pl.BlockSpec((8, LANES), lambda i: (0, 0)),
                  pl.BlockSpec((tm, LANES), lambda i: (i, 0)), pl.BlockSpec((tm, LANES), lambda i: (i, 0))],
        out_specs=(pl.BlockSpec((tm, ROPE_W), lambda i: (i, 0)), pl.BlockSpec((8, LANES), lambda i: (0, 0))),
        compiler_params=_params(("arbitrary",), est),
    )(h, dqT, dkT, gains, cos2, sin2)


def _attn_a_fwd(k, hT, *, comm=None, tq=4096, tk=2048, cq=512):
    G, T, HD = k.shape
    HE = HD + ONES_ROWS
    tq, tk = min(tq, T), min(tk, T)
    cq = min(cq, tq)
    nk, nt = T // tk, T // tq
    grid = (G, GQ * nt, nk)

    def body(*refs):
        (k_ref, qT_ref, v_ref, oT_ref, lse_ref, m_sc, acc_sc), comm_start, comm_finish = _host_comm(
            comm, refs, 3, 2, 2, grid)
        kv = pl.program_id(2)
        comm_start()
        v1T = jnp.concatenate([v_ref[...], jnp.ones((ONES_ROWS, tk), BF16)], axis=0)

        @pl.when(kv == 0)
        def _():
            m_sc[...] = jnp.full_like(m_sc, NEG)
            acc_sc[...] = jnp.zeros_like(acc_sc)

        def scores(c):
            return jnp.dot(k_ref[...], qT_ref[:, c * cq:(c + 1) * cq], preferred_element_type=F32)

        nc = tq // cq
        ahead = scores(0)
        for c in range(nc):
            cols = slice(c * cq, (c + 1) * cq)
            sT = ahead
            if c + 1 < nc:
                ahead = scores(c + 1)
            m_prev = m_sc[:, cols]
            m_new = jnp.maximum(m_prev, jnp.max(sT, axis=0, keepdims=True))
            pT = jnp.exp2(sT - m_new).astype(BF16)
            acc_sc[:, cols] = (jnp.exp2(m_prev - m_new) * acc_sc[:, cols]
                               + jnp.dot(v1T, pT, preferred_element_type=F32))
            m_sc[:, cols] = m_new

        @pl.when(kv == nk - 1)
        def _():
            l = acc_sc[HD:HD + 1, :]
            oT_ref[...] = acc_sc[0:HD, :] / l
            lse_ref[...] = m_sc[...] + jnp.log2(l)

        comm_finish()

    qtr = pl.BlockSpec((None, HD, tq), lambda g, i, j: (g * GQ + i // nt, 0, i % nt))
    qvec = pl.BlockSpec((None, 1, tq), lambda g, i, j: (g * GQ + i // nt, 0, i % nt))
    est = 6 * _nbytes((cq, tk), F32) + (8 << 20)
    hosted = comm is not None
    return pl.pallas_call(
        body, name="attn_a_fwd_comm" if hosted else "attn_a_fwd", grid=grid,
        out_shape=[jax.ShapeDtypeStruct((G * GQ, HD, T), F32), jax.ShapeDtypeStruct((G * GQ, 1, T), F32)]
        + (comm.out_shape if hosted else []),
        in_specs=[pl.BlockSpec((None, tk, HD), lambda g, i, j: (g, j, 0)), qtr,
                  pl.BlockSpec((None, HD, tk), lambda g, i, j: (A_V0 + g, 0, j))] + (comm.in_specs if hosted else []),
        out_specs=[qtr, qvec] + (comm.out_specs if hosted else []),
        scratch_shapes=[pltpu.VMEM((1, tq), F32), pltpu.VMEM((HE, tq), F32)] + (comm.scratch if hosted else []),
        compiler_params=_params(("arbitrary",) * 3 if hosted else ("parallel", "parallel", "arbitrary"), est),
    )(k, hT, hT, *(comm.parts if hosted else []))


def _attn_a_bwd(k, v, hT, doT, lse_row, delta_row, *, comm=None, tq=4096, tk=1024, cq=256):
    G, T, HD = k.shape
    tq, tk = min(tq, T), min(tk, T)
    cq = min(cq, tq)
    nqt = T // tq
    nq, nc = GQ * nqt, tq // cq
    nt = (((1,), (1,)), ((), ()))

    grid = (G, T // tk, nq)

    def body(*refs):
        (k_ref, v_ref, kT_ref, qT_ref, doT_ref, lse_ref, dl_ref, dkT_ref, dvT_ref, dqT_ref, dk_sc, dv_sc), \
            comm_start, comm_finish = _host_comm(comm, refs, 7, 3, 2, grid)
        j, i = pl.program_id(1), pl.program_id(2)
        comm_start()

        @pl.when((j == 0) & (i == 0))
        def _():
            dqT_ref[...] = jnp.zeros_like(dqT_ref)

        @pl.when(i == 0)
        def _():
            dk_sc[...] = jnp.zeros_like(dk_sc)
            dv_sc[...] = jnp.zeros_like(dv_sc)

        def scores(c):
            cols = slice(c * cq, (c + 1) * cq)
            return (jnp.dot(k_ref[...], qT_ref[:, cols], preferred_element_type=F32),
                    jnp.dot(v_ref[...], doT_ref[:, cols], preferred_element_type=F32))

        ahead = scores(0)
        dk_part = dv_part = None
        for c in range(nc):
            cols = slice(c * cq, (c + 1) * cq)
            sT, dpT = ahead
            if c + 1 < nc:
                ahead = scores(c + 1)
            pT = jnp.exp2(sT - lse_ref[:, cols])
            dsT = (pT * (dpT - dl_ref[:, cols])).astype(BF16)
            dv_c = lax.dot_general(doT_ref[:, cols], pT.astype(BF16), nt, preferred_element_type=F32)
            dk_c = lax.dot_general(qT_ref[:, cols], dsT, nt, preferred_element_type=F32)
            dv_part = dv_c if dv_part is None else dv_part + dv_c
            dk_part = dk_c if dk_part is None else dk_part + dk_c
            out_cols = pl.ds(pl.multiple_of((i % nqt) * tq + c * cq, cq), cq)
            dqT_ref[i // nqt, :, out_cols] += jnp.dot(kT_ref[...], dsT, preferred_element_type=F32)
        dk_sc[...] += dk_part
        dv_sc[...] += dv_part

        @pl.when(i == nq - 1)
        def _():
            dkT_ref[...] = dk_sc[...] * LN2
            dvT_ref[...] = dv_sc[...]

        comm_finish()

    krow = pl.BlockSpec((None, tk, HD), lambda g, j, i: (g, j, 0))
    ktr = pl.BlockSpec((None, HD, tk), lambda g, j, i: (g, 0, j))
    ktr_h = pl.BlockSpec((None, HD, tk), lambda g, j, i: (A_K0 + g, 0, j))
    qtr = pl.BlockSpec((None, HD, tq), lambda g, j, i: (g * GQ + i // nqt, 0, i % nqt))
    qvec = pl.BlockSpec((None, 1, tq), lambda g, j, i: (g * GQ + i // nqt, 0, i % nqt))
    whole = pl.BlockSpec((GQ, HD, T), lambda g, j, i: (g, 0, 0))
    est = 8 * _nbytes((cq, tk), F32) + 2 * _nbytes((GQ, HD, T), F32) + (8 << 20)
    hosted = comm is not None
    return pl.pallas_call(
        body, name="attn_a_bwd_comm" if hosted else "attn_a_bwd", grid=grid,
        out_shape=[jax.ShapeDtypeStruct((G, HD, T), F32), jax.ShapeDtypeStruct((G, HD, T), F32),
                   jax.ShapeDtypeStruct((G * GQ, HD, T), F32)] + (comm.out_shape if hosted else []),
        in_specs=[krow, krow, ktr_h, qtr, qtr, qvec, qvec] + (comm.in_specs if hosted else []),
        out_specs=[ktr, ktr, whole] + (comm.out_specs if hosted else []),
        scratch_shapes=[pltpu.VMEM((HD, tk), F32), pltpu.VMEM((HD, tk), F32)] + (comm.scratch if hosted else []),
        compiler_params=_params(("arbitrary", "arbitrary", "arbitrary"), est),
    )(k, v, hT, hT, doT, lse_row, delta_row, *(comm.parts if hosted else []))


WB = WINDOW
WK = 3 * WINDOW


QB_COL0 = (ROPE_W + KW) // (GQ * HEAD_DIM)
KB_COL = (ROPE_W + KW + QW) // KW
GW = GQ * HEAD_DIM


def _win_in_specs(T):
    nb = T // WB
    q = [pl.BlockSpec((WB, GW), functools.partial(lambda n, g: (n, QB_COL0 + g), g=g)) for g in range(KV)]
    kv = [pl.BlockSpec((WB, KW), functools.partial(lambda n, o, c: (jnp.clip(n + o, 0, nb - 1), c), o=o, c=c))
          for c in (KB_COL, KB_COL + 1) for o in (-1, 0, 1)]
    return nb, q + kv


def _win_valid(n, T):
    kabs = n * WB - WB + lax.broadcasted_iota(jnp.int32, (WK, GQ * WB), 0)
    return (kabs >= 0) & (kabs < T)


def _heads_to_lanes(t):
    return jnp.concatenate([t[i * HEAD_DIM:(i + 1) * HEAD_DIM] for i in range(GQ)], axis=1)


def _lanes_to_heads(t):
    return jnp.concatenate([t[:, i * WB:(i + 1) * WB] for i in range(GQ)], axis=0)


def _attn_b_fwd(h, biasT, sink_rows):
    T = h.shape[0]
    nb, in_specs = _win_in_specs(T)

    def body(q0, q1, k0, k1, k2, v0, v1, v2, b_ref, sk_ref, o_ref, lse_ref):
        n = pl.program_id(0)
        kwin = jnp.concatenate([k0[...], k1[...], k2[...]], axis=0)
        vT = jnp.concatenate([v0[...], v1[...], v2[...]], axis=0).T
        valid = _win_valid(n, T)
        qT = [_heads_to_lanes((q[...] * SCALE).T).astype(BF16) for q in (q0, q1)]
        sT = [jnp.dot(kwin[:, g * HEAD_DIM:(g + 1) * HEAD_DIM].astype(BF16), qT[g], preferred_element_type=F32)
              for g in range(KV)]
        oT = []
        for g in range(KV):
            s = jnp.where(valid, sT[g] + b_ref[g], NEG)
            sk = sk_ref[g]
            m = jnp.maximum(jnp.max(s, axis=0, keepdims=True), sk)
            p = jnp.exp(s - m)
            den = jnp.sum(p, axis=0, keepdims=True) + jnp.exp(sk - m)
            o = jnp.dot(vT[g * HEAD_DIM:(g + 1) * HEAD_DIM].astype(BF16), p.astype(BF16),
                        preferred_element_type=F32) / den
            lse_ref[g] = m + jnp.log(den)
            oT.append(_lanes_to_heads(o))
        o_ref[...] = jnp.concatenate(oT, axis=0).T

    whole = lambda *shape: pl.BlockSpec(shape, lambda n: (0,) * len(shape))
    return pl.pallas_call(
        body, name="attn_b_fwd", grid=(nb,),
        out_shape=(jax.ShapeDtypeStruct((T, QW), F32), jax.ShapeDtypeStruct((nb, KV, 1, GQ * WB), F32)),
        in_specs=in_specs + [whole(KV, WK, GQ * WB), whole(KV, 1, GQ * WB)],
        out_specs=(pl.BlockSpec((WB, QW), lambda n: (n, 0)),
                   pl.BlockSpec((None, KV, 1, GQ * WB), lambda n: (n, 0, 0, 0))),
        compiler_params=_params(("parallel",), 24 << 20),
    )(*([h] * 8), biasT, sink_rows)


def _attn_b_bwd(h, do, o, lse, biasT, sink_rows):
    T = h.shape[0]
    nb, in_specs = _win_in_specs(T)
    Tp = T + 2 * WB
    nt = (((1,), (1,)), ((), ()))

    def body(q0, q1, k0, k1, k2, v0, v1, v2, do_ref, o_ref, lse_ref, b_ref, sk_ref,
             dq_ref, dk_ref, dv_ref, db_ref, dsk_ref):
        n = pl.program_id(0)

        @pl.when(n == 0)
        def _():
            dk_ref[...] = jnp.zeros_like(dk_ref)
            dv_ref[...] = jnp.zeros_like(dv_ref)
            db_ref[...] = jnp.zeros_like(db_ref)
            dsk_ref[...] = jnp.zeros_like(dsk_ref)

        kwin = jnp.concatenate([k0[...], k1[...], k2[...]], axis=0)
        vwin = jnp.concatenate([v0[...], v1[...], v2[...]], axis=0)
        kT = kwin.T
        valid = _win_valid(n, T)
        doT_all, oT_all = do_ref[...].T, o_ref[...].T
        qT, doT, delta, sT, dpT = [], [], [], [], []
        for g, q in enumerate((q0, q1)):
            hd = slice(g * HEAD_DIM, (g + 1) * HEAD_DIM)
            qT.append(_heads_to_lanes((q[...] * SCALE).T).astype(BF16))
            d = _heads_to_lanes(doT_all[g * GW:(g + 1) * GW])
            delta.append(jnp.sum(d * _heads_to_lanes(oT_all[g * GW:(g + 1) * GW]), axis=0, keepdims=True))
            doT.append(d.astype(BF16))
            sT.append(jnp.dot(kwin[:, hd].astype(BF16), qT[g], preferred_element_type=F32))
            dpT.append(jnp.dot(vwin[:, hd].astype(BF16), doT[g], preferred_element_type=F32))
        dq, dk, dv = [], [], []
        for g in range(KV):
            lse_g = lse_ref[g]
            p = jnp.exp(jnp.where(valid, sT[g] + b_ref[g], NEG) - lse_g)
            ds = p * (dpT[g] - delta[g])
            db_ref[g] += ds
            dsk_ref[g] -= jnp.exp(sk_ref[g] - lse_g) * delta[g]
            dsb = ds.astype(BF16)
            dqT = jnp.dot(kT[g * HEAD_DIM:(g + 1) * HEAD_DIM].astype(BF16), dsb, preferred_element_type=F32)
            dq.append(_lanes_to_heads(dqT))
            dk.append(lax.dot_general(dsb, qT[g], nt, preferred_element_type=F32))
            dv.append(lax.dot_general(p.astype(BF16), doT[g], nt, preferred_element_type=F32))
        dq_ref[...] = (jnp.concatenate(dq, axis=0).T * SCALE).astype(BF16)
        win = pl.ds(pl.multiple_of(n * WB, WB), WK)
        dk_ref[win, :] += jnp.concatenate(dk, axis=1)
        dv_ref[win, :] += jnp.concatenate(dv, axis=1)

    whole = lambda *shape: pl.BlockSpec(shape, lambda n: (0,) * len(shape))
    tok = pl.BlockSpec((WB, QW), lambda n: (n, 0))
    return pl.pallas_call(
        body, name="attn_b_bwd", grid=(nb,),
        out_shape=(jax.ShapeDtypeStruct((T, QW), BF16),
                   jax.ShapeDtypeStruct((Tp, KW), F32), jax.ShapeDtypeStruct((Tp, KW), F32),
                   jax.ShapeDtypeStruct((KV, WK, GQ * WB), F32), jax.ShapeDtypeStruct((KV, 1, GQ * WB), F32)),
        in_specs=in_specs + [tok, tok, pl.BlockSpec((None, KV, 1, GQ * WB), lambda n: (n, 0, 0, 0)),
                             whole(KV, WK, GQ * WB), whole(KV, 1, GQ * WB)],
        out_specs=(tok, whole(Tp, KW), whole(Tp, KW), whole(KV, WK, GQ * WB), whole(KV, 1, GQ * WB)),
        compiler_params=_params(("arbitrary",), 40 << 20),
    )(*([h] * 8), do, o, lse, biasT, sink_rows)


def _bias_table(rel_bias_t, bucket):
    nh, n = rel_bias_t.shape[0], bucket.shape[1]

    def body(rb_ref, bk_ref, o_ref):
        bk = bk_ref[...]
        out = jnp.full((nh, n), NEG, F32)
        for b in range(N_BUCKETS):
            out = jnp.where(bk == b, rb_ref[:, b:b + 1], out)
        o_ref[...] = out

    return pl.pallas_call(
        body, name="bias_table", out_shape=jax.ShapeDtypeStruct((nh, n), F32),
        compiler_params=pltpu.CompilerParams(vmem_limit_bytes=32 << 20),
    )(rel_bias_t, bucket)


def _bias_sink_grads(db_list, dsk_list, bucket):
    L = len(db_list)

    def body(*refs):
        db_refs, dsk_refs, bk_ref = refs[:L], refs[L:2 * L], refs[2 * L]
        drb_ref, dsink_ref = refs[2 * L + 1], refs[2 * L + 2]
        tot = db_refs[0][...]
        for r in db_refs[1:]:
            tot = tot + r[...]
        bk = bk_ref[...]
        lane = lax.broadcasted_iota(jnp.int32, (2 * GQ, N_BUCKETS), 1)
        out = jnp.zeros((2 * GQ, N_BUCKETS), F32)
        for b in range(N_BUCKETS):
            sb = jnp.sum(jnp.where(bk == b, tot, 0.0), axis=1, keepdims=True)
            out = jnp.where(lane == b, sb, out)
        drb_ref[...] = out
        for l in range(L):
            dsink_ref[l] = jnp.sum(dsk_refs[l][...], axis=1, keepdims=True)

    return pl.pallas_call(
        body, name="bias_sink_grads",
        out_shape=(jax.ShapeDtypeStruct((2 * GQ, N_BUCKETS), F32), jax.ShapeDtypeStruct((L, 2 * GQ, 1), F32)),
        compiler_params=pltpu.CompilerParams(vmem_limit_bytes=32 << 20),
    )(*db_list, *dsk_list, bucket)


def _outnorm_fwd(oaT, ob, ga, gb, *, tm=512):
    T = ob.shape[0]
    tm = min(tm, T)

    def body(oaT_ref, ob_ref, ga_ref, gb_ref, y_ref):
        for j, (o, g_ref) in enumerate(((oaT_ref[...].T, ga_ref), (ob_ref[...], gb_ref))):
            r = lax.rsqrt(jnp.mean(o * o, axis=1, keepdims=True) + RMS_EPS)
            y_ref[:, j * QW:(j + 1) * QW] = (o * r * g_ref[...]).astype(BF16)

    half = pl.BlockSpec((tm, QW), lambda i: (i, 0))
    halfT = pl.BlockSpec((QW, tm), lambda i: (0, i))
    vec = pl.BlockSpec((1, QW), lambda i: (0, 0))
    return pl.pallas_call(
        body, name="outnorm_fwd", grid=(T // tm,),
        out_shape=jax.ShapeDtypeStruct((T, 2 * QW), BF16),
        in_specs=[halfT, half, vec, vec], out_specs=pl.BlockSpec((tm, 2 * QW), lambda i: (i, 0)),
        compiler_params=_params(("parallel",), 16 << 20),
    )(oaT, ob, ga, gb)


def _outnorm_bwd(dy, oaT, ob, ga, gb, *, tm=512):
    T = ob.shape[0]
    tm = min(tm, T)
    nh = QW // HEAD_DIM

    def body(dy_ref, oaT_ref, ob_ref, ga_ref, gb_ref, doaT_ref, dl_ref, dob_ref, dg_ref):
        i = pl.program_id(0)

        @pl.when(i == 0)
        def _():
            dg_ref[...] = jnp.zeros_like(dg_ref)

        oaT = oaT_ref[...]
        for j, (o, g_ref) in enumerate(((oaT.T, ga_ref), (ob_ref[...], gb_ref))):
            d = dy_ref[:, j * QW:(j + 1) * QW]
            r = lax.rsqrt(jnp.mean(o * o, axis=1, keepdims=True) + RMS_EPS)
            orr = o * r
            dg_ref[j:j + 1, :] += jnp.sum(d * orr, axis=0, keepdims=True)
            dgv = d * g_ref[...]
            do = r * dgv - orr * (r * r) * jnp.mean(dgv * o, axis=1, keepdims=True)
            if j == 0:
                doT = do.T
                doaT_ref[...] = doT.astype(BF16)
                prod = doT * oaT
                dl_ref[...] = jnp.concatenate(
                    [jnp.sum(prod[a * HEAD_DIM:(a + 1) * HEAD_DIM], axis=0, keepdims=True) for a in range(nh)], axis=0)
            else:
                dob_ref[...] = do

    half = pl.BlockSpec((tm, QW), lambda i: (i, 0))
    halfT = pl.BlockSpec((QW, tm), lambda i: (0, i))
    vec = pl.BlockSpec((1, QW), lambda i: (0, 0))
    return pl.pallas_call(
        body, name="outnorm_bwd", grid=(T // tm,),
        out_shape=(jax.ShapeDtypeStruct((QW, T), BF16), jax.ShapeDtypeStruct((nh, T), F32),
                   jax.ShapeDtypeStruct((T, QW), F32), jax.ShapeDtypeStruct((8, QW), F32)),
        in_specs=[pl.BlockSpec((tm, 2 * QW), lambda i: (i, 0)), halfT, half, vec, vec],
        out_specs=(halfT, pl.BlockSpec((nh, tm), lambda i: (0, i)), half, pl.BlockSpec((8, QW), lambda i: (0, 0))),
        compiler_params=_params(("arbitrary",), 32 << 20),
    )(dy, oaT, ob, ga, gb)


GELU_C = math.sqrt(2.0 / math.pi)
GELU_A = 0.044715
HALO = 16


def _gelu_parts(x):
    t = jnp.tanh(GELU_C * (x + GELU_A * (x * x * x)))
    return 0.5 * (1.0 + t), t


def _halo_specs(tm, tn, T):
    nh = tm // HALO
    last = T // HALO - 1
    cur = pl.BlockSpec((tm, tn), lambda j, i: (i, j))
    prev = pl.BlockSpec((HALO, tn), lambda j, i: (jnp.maximum(i * nh - 1, 0), j))
    nxt = pl.BlockSpec((HALO, tn), lambda j, i: (jnp.minimum((i + 1) * nh, last), j))
    return cur, prev, nxt


def _conv_glu_fwd(g, u, conv_w, conv_b, *, tm=256, tn=1408):
    T, F = g.shape
    tm, tn = min(tm, T), min(tn, F)
    cur, prev, nxt = _halo_specs(tm, tn, T)

    def body(g_ref, gp_ref, gn_ref, u_ref, w_ref, b_ref, a_ref):
        i = pl.program_id(1)
        gv = g_ref[...]
        row = lax.broadcasted_iota(jnp.int32, (tm, tn), 0)
        before = jnp.where(i * tm > 0, gp_ref[HALO - 1:HALO, :], 0.0)
        after = jnp.where((i + 1) * tm < T, gn_ref[0:1, :], 0.0)
        gm1 = jnp.where(row == 0, before, pltpu.roll(gv, 1, 0))
        gp1 = jnp.where(row == tm - 1, after, pltpu.roll(gv, tm - 1, 0))
        gc = ((b_ref[...] + gm1 * w_ref[0:1, :]) + gv * w_ref[1:2, :]) + gp1 * w_ref[2:3, :]
        cdf, _ = _gelu_parts(gc)
        a_ref[...] = (gc * cdf * u_ref[...].astype(F32)).astype(BF16)

    wspec = pl.BlockSpec((8, tn), lambda j, i: (0, j))
    est = 2 * (3 * _nbytes((tm, tn), F32)) + 8 * _nbytes((tm, tn), F32)
    return pl.pallas_call(
        body, name="conv_glu_fwd", grid=(F // tn, T // tm),
        out_shape=jax.ShapeDtypeStruct((T, F), BF16),
        in_specs=[cur, prev, nxt, cur, wspec, pl.BlockSpec((1, tn), lambda j, i: (0, j))],
        out_specs=cur,
        compiler_params=_params(("parallel", "parallel"), est),
    )(g, g, g, u, conv_w, conv_b)


def _conv_glu_bwd(dact, g, u, conv_w, conv_b, *, tm=256, tn=1408):
    T, F = g.shape
    tm, tn = min(tm, T), min(tn, F)
    cur, prev, nxt = _halo_specs(tm, tn, T)
    te = tm + 2 * HALO

    def body(d_ref, dp_ref, dn_ref, g_ref, gp_ref, gn_ref, u_ref, up_ref, un_ref, w_ref, b_ref,
             dg_ref, du_ref, dc_ref):
        i = pl.program_id(1)

        @pl.when(i == 0)
        def _():
            dc_ref[...] = jnp.zeros_like(dc_ref)

        grow = i * tm - HALO + lax.broadcasted_iota(jnp.int32, (te, tn), 0)
        valid = (grow >= 0) & (grow < T)
        ge = jnp.where(valid, jnp.concatenate([gp_ref[...], g_ref[...], gn_ref[...]], axis=0), 0.0)
        ue = jnp.concatenate([up_ref[...], u_ref[...], un_ref[...]], axis=0).astype(F32)
        de = jnp.concatenate([dp_ref[...], d_ref[...], dn_ref[...]], axis=0).astype(F32)
        w0, w1, w2 = w_ref[0:1, :], w_ref[1:2, :], w_ref[2:3, :]
        gm1 = pltpu.roll(ge, 1, 0)
        gp1 = pltpu.roll(ge, te - 1, 0)
        gc = ((b_ref[...] + gm1 * w0) + ge * w1) + gp1 * w2
        cdf, t = _gelu_parts(gc)
        dgelu = cdf + 0.5 * gc * (1.0 - t * t) * (GELU_C * (1.0 + 3.0 * GELU_A * (gc * gc)))
        dgc = jnp.where(valid, de * ue * dgelu, 0.0)
        dge = w0 * pltpu.roll(dgc, te - 1, 0) + w1 * dgc + w2 * pltpu.roll(dgc, 1, 0)
        mid = slice(HALO, HALO + tm)
        dg_ref[...] = dge[mid].astype(BF16)
        du_ref[...] = (de[mid] * (gc[mid] * cdf[mid])).astype(BF16)
        dgm = dgc[mid]
        dc_ref[0:1, :] += jnp.sum(dgm * gm1[mid], axis=0, keepdims=True)
        dc_ref[1:2, :] += jnp.sum(dgm * ge[mid], axis=0, keepdims=True)
        dc_ref[2:3, :] += jnp.sum(dgm * gp1[mid], axis=0, keepdims=True)
        dc_ref[3:4, :] += jnp.sum(dgm, axis=0, keepdims=True)

    wspec = pl.BlockSpec((8, tn), lambda j, i: (0, j))
    est = 2 * (3 * _nbytes((tm, tn), F32) + 2 * _nbytes((tm, tn), BF16)) + 16 * _nbytes((te, tn), F32)
    return pl.pallas_call(
        body, name="conv_glu_bwd", grid=(F // tn, T // tm),
        out_shape=(jax.ShapeDtypeStruct((T, F), BF16), jax.ShapeDtypeStruct((T, F), BF16),
                   jax.ShapeDtypeStruct((8, F), F32)),
        in_specs=[cur, prev, nxt, cur, prev, nxt, cur, prev, nxt, wspec, pl.BlockSpec((1, tn), lambda j, i: (0, j))],
        out_specs=(cur, cur, wspec),
        compiler_params=_params(("parallel", "arbitrary"), est),
    )(dact, dact, dact, g, g, g, u, u, u, conv_w, conv_b)


def _adamw_math(w, g, m, v):
    m = ADAM_B1 * m + (1.0 - ADAM_B1) * g
    v = ADAM_B2 * v + (1.0 - ADAM_B2) * (g * g)
    m_hat = m / (1.0 - ADAM_B1 ** ADAM_STEP)
    v_hat = v / (1.0 - ADAM_B2 ** ADAM_STEP)
    delta = -ADAM_LR * (m_hat / (jnp.sqrt(v_hat) + ADAM_EPS) + ADAM_WD * w)
    return delta, m, v


def _adamw(w, m, v, gparts, *, name, tr):
    R, C = w.shape
    tr = min(tr, R)
    assert R % tr == 0

    def body(w_ref, m_ref, v_ref, gp_ref, g_ref, d_ref, nm_ref, nv_ref):
        g = gp_ref[0].astype(F32)
        for j in range(1, N_DEV):
            g = g + gp_ref[j].astype(F32)
        delta, nm, nv = _adamw_math(w_ref[...], g, m_ref[...], v_ref[...])
        g_ref[...] = g
        d_ref[...] = delta
        nm_ref[...] = nm
        nv_ref[...] = nv

    blk = pl.BlockSpec((tr, C), lambda i: (i, 0))
    out = jax.ShapeDtypeStruct((R, C), F32)
    return pl.pallas_call(
        body, name=name, grid=(R // tr,), out_shape=(out, out, out, out),
        in_specs=[blk, blk, blk, pl.BlockSpec((N_DEV, tr, C), lambda i: (0, i, 0))],
        out_specs=(blk, blk, blk, blk),
        compiler_params=_params(("parallel",), 24 << 20),
    )(w, m, v, gparts)


def _rope_tables(T):
    rows_n = T // GRID_W
    row = jnp.repeat(jnp.arange(rows_n, dtype=F32), GRID_W)
    col = jnp.tile(jnp.arange(GRID_W, dtype=F32), rows_n)
    half = HEAD_DIM // 2
    inv_freq = ROPE_THETA ** (-jnp.arange(0, half, 2, dtype=F32) / half)
    ang = jnp.concatenate([row[:, None] * inv_freq, col[:, None] * inv_freq], axis=-1)
    cos, sin = jnp.cos(ang), jnp.sin(ang)
    cos64 = jnp.repeat(cos, 2, axis=-1)
    sin64 = jnp.stack([-sin, sin], axis=-1).reshape(T, HEAD_DIM)
    return jnp.tile(cos64, (1, 2)), jnp.tile(sin64, (1, 2))


def _t5_bucket(rel):
    half = N_BUCKETS // 2
    max_exact = half // 2
    bucket = jnp.where(rel > 0, half, 0)
    rp = jnp.abs(rel)
    rpf = jnp.maximum(rp, 1).astype(F32)
    large = max_exact + (jnp.log(rpf / max_exact) / math.log(MAX_DISTANCE / max_exact)
                         * (half - max_exact)).astype(jnp.int32)
    large = jnp.minimum(large, half - 1)
    return bucket + jnp.where(rp < max_exact, rp, large)


def _window_buckets():
    qpos = jnp.arange(WB, dtype=jnp.int32)
    kpos = jnp.arange(WK, dtype=jnp.int32) - WB
    rel = kpos[None, :] - qpos[:, None]
    return jnp.where(jnp.abs(rel) <= WINDOW, _t5_bucket(rel), -1)


def _heads_first(a, nh):
    T = a.shape[0]
    return a.reshape(T, nh, HEAD_DIM).transpose(1, 0, 2)


def _row(v):
    return v.reshape(1, -1)


def _rows8(rows, width):
    a = jnp.stack(list(rows), axis=0)
    return jnp.pad(a, ((0, 8 - a.shape[0]), (0, 0)))


def _layer_fwd(l, xin, W, tabs, comm=None, on_comm=None):
    xhat, xg, xb, x16 = xin
    T = xhat.shape[0]
    cos2, sin2, biasT = tabs
    h = _mm([x16], [W["w_in"][l]], name="mm_in", out_dtype=F32, tm=512, tn=IN_COLS, tk=D_MODEL)
    gains = _rows8([jnp.tile(W["q_norm"][l], 2), jnp.tile(W["k_norm"][l], 2)], LANES)
    hT, kv_nat = _qk_rope_fwd(h, gains, cos2, sin2, name="qk_rope_fwd")
    hT = hT.reshape(A_HEADS, HEAD_DIM, T)
    ka, va = _heads_first(kv_nat[:, :KW], KV), _heads_first(kv_nat[:, KW:], KV)
    res = _attn_a_fwd(ka, hT, comm=comm)
    oaT, lse_a = res[0].reshape(QW, T), res[1]
    if comm is not None:
        on_comm(res[2:])
    sink_rows = jnp.repeat(W["sink"][l], WB).reshape(KV, 1, GQ * WB)
    ob_t, lse_b = _attn_b_fwd(h, biasT, sink_rows)
    ga, gb = _row(W["out_norm_a"][l]), _row(W["out_norm_b"][l])
    ycat = _outnorm_fwd(oaT, ob_t, ga, gb)
    g1, b1 = _row(W["ln1_g"][l]), _row(W["ln1_b"][l])
    x1hat, rstd1, x1_16 = _mm_res_ln(ycat, W["w_out"][l], xhat, xg, xb, g1, b1, name="mm_out_ln", tm=512)
    gate = _mm([x1_16], [W["w_gate"][l]], name="mm_gate", out_dtype=F32, tm=512, tn=D_FF // 2, tk=D_MODEL)
    up = _mm([x1_16], [W["w_up"][l]], name="mm_up", out_dtype=BF16, tm=512, tn=D_FF // 2, tk=D_MODEL)
    cw = jnp.pad(W["conv_w"][l], ((0, 5), (0, 0)))
    cb = _row(W["conv_b"][l])
    act = _conv_glu_fwd(gate, up, cw, cb)
    g2, b2 = _row(W["ln2_g"][l]), _row(W["ln2_b"][l])
    x2hat, rstd2, x2_16 = _mm_res_ln(act, W["w_down"][l], x1hat, g1, b1, g2, b2, name="mm_down_ln", tm=256)
    saved = dict(x16=x16, h=h, gains=gains, hT=hT, ka=ka, va=va, oaT=oaT, lse_a=lse_a,
                 lse_b=lse_b, sink_rows=sink_rows, ob_t=ob_t,
                 ga=ga, gb=gb, ycat=ycat, x1hat=x1hat, rstd1=rstd1, x1_16=x1_16, g1=g1, b1=b1, gate=gate, up=up,
                 cw=cw, cb=cb, act=act, x2hat=x2hat, rstd2=rstd2, g2=g2, b2=b2)
    return (x2hat, g2, b2, x2_16), saved


def _layer_bwd(l, S, W, WT, tabs, dz2, dz2_16, stats2, scatter=None):
    cos2, sin2, biasT = tabs
    T = dz2.shape[0]
    G = {}
    G["ln2_g"], G["ln2_b"] = stats2[0], stats2[1]
    G["w_down"] = _mm([S["act"]], [dz2_16], name="dw_down", out_dtype=BF16, trans_a=True, tm=D_FF // 2, tn=D_MODEL, tk=512)
    dact = _mm([dz2_16], [WT["w_down"][l]], name="mm_dact", out_dtype=BF16, tm=512, tn=D_FF // 2, tk=D_MODEL)
    dg, du, dconv = _conv_glu_bwd(dact, S["gate"], S["up"], S["cw"], S["cb"])
    G["conv_w"], G["conv_b"] = dconv[0:3], dconv[3]
    G["w_gate"] = _mm([S["x1_16"]], [dg], name="dw_gate", out_dtype=BF16, trans_a=True, tm=D_MODEL, tn=D_FF // 2, tk=512)
    G["w_up"] = _mm([S["x1_16"]], [du], name="dw_up", out_dtype=BF16, trans_a=True, tm=D_MODEL, tn=D_FF // 2, tk=512)
    dx1 = _mm([dg, du], [WT["w_gate"][l], WT["w_up"][l]], name="mm_dx1", out_dtype=F32, tm=512, tn=D_MODEL,
              tk=D_FF // 2, add=dz2, add_scale=ALPHA)
    dz1, dz1_16, stats1 = _ln_bwd(S["x1hat"], S["rstd1"], S["g1"], S["b1"], name="ln1_bwd", dx=dx1)
    G["ln1_g"], G["ln1_b"] = stats1[0], stats1[1]
    G["w_out"] = _mm([S["ycat"]], [dz1_16], name="dw_out", out_dtype=BF16, trans_a=True, tm=D_MODEL, tn=D_MODEL, tk=512)
    dycat = _mm([dz1_16], [WT["w_out"][l]], name="mm_dycat", out_dtype=F32, tm=512, tn=D_MODEL, tk=D_MODEL)
    doaT, delta, dob_t, dgn = _outnorm_bwd(dycat, S["oaT"], S["ob_t"], S["ga"], S["gb"])
    G["out_norm_a"], G["out_norm_b"] = dgn[0], dgn[1]
    res = _attn_a_bwd(S["ka"], S["va"], S["hT"], doaT.reshape(KV * GQ, HEAD_DIM, T), S["lse_a"],
                      delta.reshape(KV * GQ, 1, T), comm=scatter(G) if scatter is not None else None)
    dkaT, dvaT, dqaT = res[:3]
    dh_rope, dgain = _qk_rope_bwd(S["h"], dqaT.reshape(QW, T), dkaT.reshape(KW, T), S["gains"], cos2, sin2,
                                  name="qk_rope_bwd")
    G["q_norm"], G["k_norm"] = dgain[0, :HEAD_DIM], dgain[1, :HEAD_DIM]
    dqb_t, dkb, dvb, dbiasT, dsk = _attn_b_bwd(S["h"], dob_t, S["ob_t"], S["lse_b"], biasT, S["sink_rows"])
    dh = jnp.concatenate([
        dh_rope, dvaT.transpose(2, 0, 1).reshape(T, KW).astype(BF16), dqb_t,
        dkb[WB:WB + T].astype(BF16), dvb[WB:WB + T].astype(BF16)], axis=1)
    dbias = dbiasT.reshape(KV, WK, GQ, WB).transpose(0, 2, 1, 3)
    G["w_in"] = _mm([S["x16"]], [dh], name="dw_in", out_dtype=BF16, trans_a=True, tm=D_MODEL, tn=IN_COLS, tk=512)
    dxin = _mm([dh], [WT["w_in"][l]], name="mm_dxin", out_dtype=F32, tm=512, tn=D_MODEL, tk=IN_COLS,
               add=dz1, add_scale=ALPHA)
    return dxin, G, dbias.reshape(KV * GQ, WK * WB), dsk.reshape(KV * GQ, WB), res[3:]


BIG = ("w_in", "w_out", "w_gate", "w_up", "w_down")
COL_SHARDED = ("w_in", "w_gate", "w_up")
WIRE_COLS = 1024


def _unshard(name, gathered, shard_shape):
    _, r, c = shard_shape
    blocks = gathered.reshape(N_DEV, r, c)
    if name in COL_SHARDED:
        return blocks.transpose(1, 0, 2).reshape(r, N_DEV * c)
    return blocks.reshape(N_DEV * r, c)


def _to_owner_blocks(name, full, shard_shape):
    _, r, c = shard_shape
    if name in COL_SHARDED:
        blocks = full.reshape(r, N_DEV, c).transpose(1, 0, 2)
    else:
        blocks = full.reshape(N_DEV, r, c)
    return blocks.reshape(N_DEV, -1, WIRE_COLS)


def _pack_small(vals, tail):
    flat = jnp.concatenate([vals[n].reshape(-1).astype(F32) for n in SMALL_NAMES] + [tail])
    pad = (-flat.shape[0]) % (8 * LANES)
    return jnp.pad(flat, (0, pad)).reshape(-1, LANES)


def _unpack_small(packed, shapes):
    flat = packed.reshape(-1)
    out, off = {}, 0
    for n in SMALL_NAMES:
        size = math.prod(shapes[n])
        out[n] = flat[off:off + size].reshape(shapes[n])
        off += size
    return out, flat[off]


def kernel(x, rel_bias, w_in, q_norm, k_norm, sink, out_norm_a, out_norm_b, w_out, ln1_g, ln1_b, w_gate, w_up, conv_w, conv_b, w_down, ln2_g, ln2_b, loss_target, m_rel_bias, m_w_in, m_q_norm, m_k_norm, m_sink, m_out_norm_a, m_out_norm_b, m_w_out, m_ln1_g, m_ln1_b, m_w_gate, m_w_up, m_conv_w, m_conv_b, m_w_down, m_ln2_g, m_ln2_b, v_rel_bias, v_w_in, v_q_norm, v_k_norm, v_sink, v_out_norm_a, v_out_norm_b, v_w_out, v_ln1_g, v_ln1_b, v_w_gate, v_w_up, v_conv_w, v_conv_b, v_w_down, v_ln2_g, v_ln2_b):
    P = dict(rel_bias=rel_bias, w_in=w_in, q_norm=q_norm, k_norm=k_norm, sink=sink, out_norm_a=out_norm_a,
             out_norm_b=out_norm_b, w_out=w_out, ln1_g=ln1_g, ln1_b=ln1_b, w_gate=w_gate, w_up=w_up, conv_w=conv_w,
             conv_b=conv_b, w_down=w_down, ln2_g=ln2_g, ln2_b=ln2_b)
    M = dict(rel_bias=m_rel_bias, w_in=m_w_in, q_norm=m_q_norm, k_norm=m_k_norm, sink=m_sink, out_norm_a=m_out_norm_a,
             out_norm_b=m_out_norm_b, w_out=m_w_out, ln1_g=m_ln1_g, ln1_b=m_ln1_b, w_gate=m_w_gate, w_up=m_w_up,
             conv_w=m_conv_w, conv_b=m_conv_b, w_down=m_w_down, ln2_g=m_ln2_g, ln2_b=m_ln2_b)
    V = dict(rel_bias=v_rel_bias, w_in=v_w_in, q_norm=v_q_norm, k_norm=v_k_norm, sink=v_sink, out_norm_a=v_out_norm_a,
             out_norm_b=v_out_norm_b, w_out=v_w_out, ln1_g=v_ln1_g, ln1_b=v_ln1_b, w_gate=v_w_gate, w_up=v_w_up,
             conv_w=v_conv_w, conv_b=v_conv_b, w_down=v_w_down, ln2_g=v_ln2_g, ln2_b=v_ln2_b)
    names = list(P)
    T = x.shape[1]
    me = 4 * lax.axis_index("x") + 2 * lax.axis_index("y") + lax.axis_index("c")

    L, taps, fc = conv_w.shape
    W = {n: ([None] * DEPTH if n in BIG else P[n]) for n in names}
    WT = {n: [None] * DEPTH for n in BIG}

    def wire(n, l):
        return P[n][l].astype(BF16).reshape(-1, WIRE_COLS)

    def take(n, l, gathered):
        W[n][l] = _unshard(n, gathered, P[n].shape)
        WT[n][l] = W[n][l].T

    take("w_in", 0, _exchange([wire("w_in", 0)], [True], name="gather_w_in0")[0])
    later = [(n, l) for l in range(DEPTH) for n in BIG if (n, l) != ("w_in", 0)]
    cw_shard = conv_w.reshape(-1)
    cw_wire = jnp.pad(cw_shard, (0, (-cw_shard.shape[0]) % LANES)).reshape(-1, LANES)
    gather_rest = _Comm([wire(n, l) for n, l in later] + [cw_wire], [True] * (len(later) + 1))

    def on_gathered(outs):
        for (n, l), g in zip(later, outs):
            take(n, l, g)
        cw_all = outs[-1].reshape(N_DEV, -1)[:, :cw_shard.shape[0]].reshape(N_DEV, L, taps, fc)
        W["conv_w"] = cw_all.transpose(1, 2, 0, 3).reshape(L, taps, N_DEV * fc)

    cos2, sin2 = _rope_tables(T)
    bucket = _window_buckets()
    bias = _bias_table(rel_bias.T, bucket.reshape(1, WB * WK))
    biasT = bias.reshape(KV, GQ, WB, WK).transpose(0, 3, 1, 2).reshape(KV, WK, GQ * WB)
    tabs = (cos2, sin2, biasT)

    ones, zeros = jnp.ones((1, D_MODEL), F32), jnp.zeros((1, D_MODEL), F32)
    cur = (x[0], ones, zeros, x[0].astype(BF16))
    saved = []
    for l in range(DEPTH):
        cur, S = _layer_fwd(l, cur, W, tabs, comm=gather_rest if l == 0 else None, on_comm=on_gathered)
        saved.append(S)

    def owner_blocks(n, l):
        return _to_owner_blocks(n, grads[l][n], P[n].shape)

    early = ([(n, l) for l in range(1, DEPTH) for n in BIG] + [(n, 0) for n in BIG if n != "w_in"])

    def scatter_early(g0):
        grads[0] = g0
        return _Comm([owner_blocks(n, l) for n, l in early], [False] * len(early))

    grads = [None] * DEPTH
    dbs, dsks = [None] * DEPTH, [None] * DEPTH
    S = saved[-1]
    dz, dz16, stats = _ln_bwd(S["x2hat"], S["rstd2"], S["g2"], S["b2"], name="loss_ln2_bwd", target=loss_target[0])
    loss_part = stats[2, 0:1]
    recv = {}
    for l in reversed(range(DEPTH)):
        S = saved[l]
        dxin, grads[l], dbs[l], dsks[l], got = _layer_bwd(l, S, W, WT, tabs, dz, dz16, stats,
                                                         scatter=scatter_early if l == 0 else None)
        if l == 0:
            recv.update(zip(early, got))
        if l > 0:
            Sp = saved[l - 1]
            dz, dz16, stats = _ln_bwd(Sp["x2hat"], Sp["rstd2"], Sp["g2"], Sp["b2"], name="ln2_bwd", dx=dxin)
    grad_x = dxin[None]

    drb, dsink = _bias_sink_grads(dbs, dsks, bucket.T.reshape(1, WK * WB))
    small_g = {n: jnp.stack([grads[l][n] for l in range(DEPTH)]) for n in SMALL_NAMES if n not in ("rel_bias", "sink")}
    small_g["rel_bias"] = drb.T
    small_g["sink"] = dsink.reshape(DEPTH, KV * GQ)
    recv[("w_in", 0)], small_recv = _exchange([owner_blocks("w_in", 0), _pack_small(small_g, loss_part)], [False, True],
                                              name="scatter_w_in0_gather_small")

    out_g, out_d, out_m, out_v = {}, {}, {}, {}
    for n in BIG:
        shp = P[n].shape
        gparts = jnp.concatenate([recv[(n, l)] for l in range(DEPTH)], axis=1)
        res = _adamw(P[n].reshape(-1, WIRE_COLS), M[n].reshape(-1, WIRE_COLS), V[n].reshape(-1, WIRE_COLS), gparts,
                     name="adamw_" + n, tr=64)
        out_g[n], out_d[n], out_m[n], out_v[n] = (r.reshape(shp) for r in res)
    full_shapes = {n: W[n].shape for n in SMALL_NAMES}

    def small_state(D):
        vals = {n: D[n] for n in SMALL_NAMES if n != "conv_w"}
        cw = jnp.zeros((L, taps, N_DEV, fc), F32)
        cw = lax.dynamic_update_slice(cw, D["conv_w"].reshape(L, taps, 1, fc), (0, 0, me, 0))
        vals["conv_w"] = cw.reshape(L, taps, N_DEV * fc)
        return _pack_small(vals, jnp.zeros((1,), F32))

    sw, sm, sv = small_state(P), small_state(M), small_state(V)
    res = _adamw(sw, sm, sv, small_recv, name="adamw_small", tr=sw.shape[0])
    loss = _unpack_small(res[0], full_shapes)[1]
    for dst, packed in zip((out_g, out_d, out_m, out_v), res):
        vals, _ = _unpack_small(packed, full_shapes)
        for n in SMALL_NAMES:
            if n == "conv_w":
                sl = lax.dynamic_slice(vals[n].reshape(L, taps, N_DEV, fc), (0, 0, me, 0), (L, taps, 1, fc))
                dst[n] = sl.reshape(L, taps, fc)
            else:
                dst[n] = vals[n]
    return (loss, grad_x, *[out_g[n] for n in names], *[out_d[n] for n in names],
            *[out_m[n] for n in names], *[out_v[n] for n in names])
```

```python
import functools
import math

import jax
import jax.numpy as jnp
from jax import lax
from jax.experimental import pallas as pl
from jax.experimental.pallas import tpu as pltpu

F32 = jnp.float32
BF16 = jnp.bfloat16
MESH = pl.DeviceIdType.MESH

N_DEV = 8
D_MODEL = 1024
DEPTH = 2
HEAD_DIM = 64
KV = 2
GQ = 4
QW = KV * GQ * HEAD_DIM
KW = KV * HEAD_DIM
ROPE_W = QW + KW
IN_COLS = 2 * (QW + 2 * KW)
D_FF = 2816
GRID_W = 64
ROPE_THETA = 10000.0
WINDOW = 128
N_BUCKETS = 32
MAX_DISTANCE = 128
ALPHA = (2.0 * DEPTH) ** 0.25
RMS_EPS = 1e-6
LN_EPS = 1e-5
SCALE = HEAD_DIM ** -0.5
LOG2E = math.log2(math.e)
LN2 = math.log(2.0)
NEG = -1e30
ONES_ROWS = 16

ADAM_LR = 0.001
ADAM_B1 = 0.9
ADAM_B2 = 0.999
ADAM_EPS = 1e-08
ADAM_WD = 0.01
ADAM_STEP = 10

LANES = 128
MM_ROWS = 1024
MM_ROWS_ACC = 512
DW_TOKENS = 1024
VMEM_CAP = 60 * 1024 * 1024
SMALL_NAMES = ("rel_bias", "q_norm", "k_norm", "sink", "out_norm_a", "out_norm_b", "ln1_g", "ln1_b",
               "conv_b", "ln2_g", "ln2_b", "conv_w")


def _params(sem, est_bytes):
    limit = int(min(VMEM_CAP, est_bytes + (8 << 20)))
    return pltpu.CompilerParams(dimension_semantics=sem, vmem_limit_bytes=limit)


def _nbytes(shape, dtype):
    return math.prod(shape) * jnp.dtype(dtype).itemsize


class _Comm:
    def __init__(self, parts, gathers):
        self.parts, self.gathers, self.n = list(parts), list(gathers), len(parts)
        hbm = pl.BlockSpec(memory_space=pltpu.HBM)
        self.in_specs = [hbm] * self.n
        self.out_specs = [hbm] * self.n
        self.out_shape = [jax.ShapeDtypeStruct((N_DEV,) + tuple(p.shape if g else p.shape[1:]), p.dtype)
                          for p, g in zip(self.parts, self.gathers)]
        self.scratch = [pltpu.SemaphoreType.DMA((self.n * (N_DEV - 1),)), pltpu.SemaphoreType.DMA((self.n * (N_DEV - 1),)),
                        pltpu.SemaphoreType.DMA((self.n,))]

    def bind(self, ins, outs, sems):
        send_sems, recv_sems, local_sems = sems
        gathers, n = self.gathers, self.n
        me = 4 * lax.axis_index("x") + 2 * lax.axis_index("y") + lax.axis_index("c")

        def src(k, j):
            return ins[k] if gathers[k] else ins[k].at[j]

        def copy(k, d, peer, lands_in):
            return pltpu.make_async_remote_copy(
                src_ref=src(k, peer), dst_ref=outs[k].at[lands_in],
                send_sem=send_sems.at[k * (N_DEV - 1) + d - 1], recv_sem=recv_sems.at[k * (N_DEV - 1) + d - 1],
                device_id=(peer // 4, lax.rem(peer // 2, 2), lax.rem(peer, 2)), device_id_type=MESH)

        def send(k, d):
            return copy(k, d, lax.rem(me + d, N_DEV), me)

        def arrival(k, d):
            frm = lax.rem(me + N_DEV - d, N_DEV)
            return copy(k, d, frm, frm)

        def local(k):
            return pltpu.make_async_copy(src(k, me), outs[k].at[me], local_sems.at[k])

        def start():
            for k in range(n):
                local(k).start()
                for d in range(1, N_DEV):
                    send(k, d).start()

        def finish():
            for k in range(n):
                for d in range(1, N_DEV):
                    arrival(k, d).wait_recv()
            for k in range(n):
                for d in range(1, N_DEV):
                    send(k, d).wait_send()
                local(k).wait()

        return start, finish


def _host_comm(comm, refs, n_in, n_out, n_scratch, grid):
    n = comm.n if comm is not None else 0
    own_in, cin = refs[:n_in], refs[n_in:n_in + n]
    own_out, cout = refs[n_in + n:n_in + n + n_out], refs[n_in + n + n_out:n_in + 2 * n + n_out]
    base = n_in + 2 * n + n_out
    own_scratch, sems = refs[base:base + n_scratch], refs[base + n_scratch:]
    own = tuple(own_in) + tuple(own_out) + tuple(own_scratch)
    if comm is None:
        return own, lambda: None, lambda: None
    start, finish = comm.bind(cin, cout, sems)
    first = last = None
    for ax, size in enumerate(grid):
        pid = pl.program_id(ax)
        first = (pid == 0) if first is None else first & (pid == 0)
        last = (pid == size - 1) if last is None else last & (pid == size - 1)
    return own, lambda: pl.when(first)(start), lambda: pl.when(last)(finish)


def _exchange(parts, gathers, name):
    comm = _Comm(parts, gathers)
    n = comm.n

    def body(*refs):
        start, finish = comm.bind(refs[:n], refs[n:2 * n], refs[2 * n:])
        start()
        finish()

    return pl.pallas_call(body, name=name, out_shape=comm.out_shape, in_specs=comm.in_specs, out_specs=comm.out_specs,
                          scratch_shapes=comm.scratch)(*comm.parts)


def _mm(a_list, b_list, *, name, out_dtype, tm, tn, tk, trans_a=False, trans_b=False, add=None, add_scale=1.0):
    assert not (trans_a and trans_b)
    na = len(a_list)
    if trans_a:
        K, M = a_list[0].shape
    else:
        M, K = a_list[0].shape
    N = b_list[0].shape[0 if trans_b else 1]
    tm, tn, tk = min(tm, M), min(tn, N), min(tk, K)
    assert M % tm == 0 and N % tn == 0 and K % tk == 0, (name, M, N, K, tm, tn, tk)
    nk = K // tk
    dims = (((0,), (0,)), ((), ())) if trans_a else (((1,), (1 if trans_b else 0,)), ((), ()))

    def body(*refs):
        a_refs, b_refs = refs[:na], refs[na:2 * na]
        add_ref = refs[2 * na] if add is not None else None
        o_ref = refs[2 * na + (add is not None)]
        k = pl.program_id(2)

        part = None
        for a_ref, b_ref in zip(a_refs, b_refs):
            prod = lax.dot_general(a_ref[...].astype(BF16), b_ref[...].astype(BF16), dims,
                                   preferred_element_type=F32)
            part = prod if part is None else part + prod

        def finish(res):
            if add_ref is not None:
                res = res + add_scale * add_ref[...]
            o_ref[...] = res.astype(o_ref.dtype)

        if nk == 1:
            finish(part)
        else:
            acc_ref = refs[-1]

            @pl.when(k == 0)
            def _():
                acc_ref[...] = part

            @pl.when(k > 0)
            def _():
                acc_ref[...] += part

            @pl.when(k == nk - 1)
            def _():
                finish(acc_ref[...])

    if trans_a:
        a_spec = pl.BlockSpec((tk, tm), lambda i, j, k: (k, i))
    else:
        a_spec = pl.BlockSpec((tm, tk), lambda i, j, k: (i, k))
    if trans_b:
        b_spec = pl.BlockSpec((tn, tk), lambda i, j, k: (j, k))
    else:
        b_spec = pl.BlockSpec((tk, tn), lambda i, j, k: (k, j))
    o_spec = pl.BlockSpec((tm, tn), lambda i, j, k: (i, j))
    in_specs = [a_spec] * na + [b_spec] * na + ([o_spec] if add is not None else [])
    est = (2 * na * (_nbytes((tm, tk), a_list[0].dtype) + _nbytes((tk, tn), b_list[0].dtype))
           + na * (_nbytes((tm, tk), BF16) + _nbytes((tk, tn), BF16))
           + 2 * _nbytes((tm, tn), out_dtype) + 3 * _nbytes((tm, tn), F32)
           + (2 * _nbytes((tm, tn), F32) if add is not None else 0))
    args = list(a_list) + list(b_list) + ([add] if add is not None else [])
    return pl.pallas_call(
        body, name=name, grid=(M // tm, N // tn, nk),
        out_shape=jax.ShapeDtypeStruct((M, N), out_dtype),
        in_specs=in_specs, out_specs=o_spec,
        scratch_shapes=[pltpu.VMEM((tm, tn), F32)] if nk > 1 else [],
        compiler_params=_params(("parallel", "parallel", "arbitrary"), est),
    )(*args)


def _mm_res_ln(a, w, res_hat, res_g, res_b, ln_g, ln_b, *, name, tm):
    T, K = a.shape
    D = w.shape[1]
    tm = min(tm, T)

    def body(a_ref, w_ref, rh_ref, rg_ref, rb_ref, g_ref, b_ref, xhat_ref, rstd_ref, xb_ref):
        branch = jnp.dot(a_ref[...].astype(BF16), w_ref[...], preferred_element_type=F32)
        z = ALPHA * (rh_ref[...] * rg_ref[...] + rb_ref[...]) + branch
        mu = jnp.mean(z, axis=1, keepdims=True)
        zc = z - mu
        var = jnp.mean(zc * zc, axis=1, keepdims=True)
        rstd = lax.rsqrt(var + LN_EPS)
        xhat = zc * rstd
        xhat_ref[...] = xhat
        rstd_ref[...] = rstd
        xb_ref[...] = (xhat * g_ref[...] + b_ref[...]).astype(BF16)

    row = pl.BlockSpec((tm, D), lambda i: (i, 0))
    vec = pl.BlockSpec((1, D), lambda i: (0, 0))
    est = (2 * (_nbytes((tm, K), a.dtype) + _nbytes((K, D), BF16)) + 4 * _nbytes((tm, D), F32) * 2
           + 6 * _nbytes((tm, D), F32))
    return pl.pallas_call(
        body, name=name, grid=(T // tm,),
        out_shape=(jax.ShapeDtypeStruct((T, D), F32), jax.ShapeDtypeStruct((T, 1), F32),
                   jax.ShapeDtypeStruct((T, D), BF16)),
        in_specs=[pl.BlockSpec((tm, K), lambda i: (i, 0)), pl.BlockSpec((K, D), lambda i: (0, 0)), row, vec, vec, vec, vec],
        out_specs=(row, pl.BlockSpec((tm, 1), lambda i: (i, 0)), row),
        compiler_params=_params(("parallel",), est),
    )(a, w, res_hat, res_g, res_b, ln_g, ln_b)


def _ln_bwd(xhat, rstd, ln_g, ln_b, *, name, dx=None, target=None, tm=256):
    T, D = xhat.shape
    tm = min(tm, T)
    head = target is not None

    def body(xhat_ref, rstd_ref, g_ref, b_ref, d_ref, dz_ref, dzb_ref, st_ref):
        i = pl.program_id(0)

        @pl.when(i == 0)
        def _():
            st_ref[...] = jnp.zeros_like(st_ref)

        xh = xhat_ref[...]
        g = g_ref[...]
        if head:
            err = (xh * g + b_ref[...]) - d_ref[...]
            dxv = err * (1.0 / D)
            st_ref[2:3, :] += 0.5 * jnp.sum(jnp.sum(err * err, axis=1, keepdims=True) * (1.0 / D), axis=0, keepdims=True)
        else:
            dxv = d_ref[...]
        st_ref[0:1, :] += jnp.sum(dxv * xh, axis=0, keepdims=True)
        st_ref[1:2, :] += jnp.sum(dxv, axis=0, keepdims=True)
        dxh = dxv * g
        m1 = jnp.mean(dxh, axis=1, keepdims=True)
        m2 = jnp.mean(dxh * xh, axis=1, keepdims=True)
        dz = rstd_ref[...] * (dxh - m1 - xh * m2)
        dz_ref[...] = dz
        dzb_ref[...] = dz.astype(BF16)

    row = pl.BlockSpec((tm, D), lambda i: (i, 0))
    vec = pl.BlockSpec((1, D), lambda i: (0, 0))
    est = 2 * 4 * _nbytes((tm, D), F32) + 6 * _nbytes((tm, D), F32)
    return pl.pallas_call(
        body, name=name, grid=(T // tm,),
        out_shape=(jax.ShapeDtypeStruct((T, D), F32), jax.ShapeDtypeStruct((T, D), BF16),
                   jax.ShapeDtypeStruct((8, D), F32)),
        in_specs=[row, pl.BlockSpec((tm, 1), lambda i: (i, 0)), vec, vec, row],
        out_specs=(row, row, pl.BlockSpec((8, D), lambda i: (0, 0))),
        compiler_params=_params(("arbitrary",), est),
    )(xhat, rstd, ln_g, ln_b, target if head else dx)


def _pair_swap(v, even):
    return jnp.where(even, pltpu.roll(v, LANES - 1, 1), pltpu.roll(v, 1, 1))


def _half_sums(v, lo):
    s_lo = jnp.sum(jnp.where(lo, v, 0.0), axis=1, keepdims=True)
    s_hi = jnp.sum(jnp.where(lo, 0.0, v), axis=1, keepdims=True)
    return jnp.where(lo, s_lo, s_hi)


A_COLS = ROPE_W + KW
A_HEADS = A_COLS // HEAD_DIM
A_K0, A_V0 = KV * GQ, KV * GQ + KV


def _qk_rope_fwd(h, gains, cos2, sin2, *, name, tm=256):
    T = h.shape[0]
    tm = min(tm, T)
    nch = A_COLS // LANES

    def body(h_ref, g_ref, c_ref, s_ref, oT_ref, kv_ref):
        lane = lax.broadcasted_iota(jnp.int32, (tm, LANES), 1)
        lo, even = lane < HEAD_DIM, lane % 2 == 0
        c, s = c_ref[...], s_ref[...]
        for j in range(nch):
            x = h_ref[:, j * LANES:(j + 1) * LANES]
            isq, isv = j < QW // LANES, j == nch - 1
            if isv:
                out = x
            else:
                g = g_ref[0:1, :] if isq else g_ref[1:2, :]
                r = lax.rsqrt(_half_sums(x * x, lo) * (1.0 / HEAD_DIM) + RMS_EPS)
                nrm = x * r * g
                out = nrm * c + _pair_swap(nrm, even) * s
            if isq:
                out = out * (SCALE * LOG2E)
            else:
                kv_ref[:, (j - QW // LANES) * LANES:(j - QW // LANES + 1) * LANES] = out.astype(BF16)
            oT_ref[j * LANES:(j + 1) * LANES, :] = out.T.astype(BF16)

    est = 2 * (_nbytes((tm, A_COLS), F32) + 2 * _nbytes((tm, A_COLS), BF16) + 2 * _nbytes((tm, LANES), F32)) + (4 << 20)
    return pl.pallas_call(
        body, name=name, grid=(T // tm,),
        out_shape=(jax.ShapeDtypeStruct((A_COLS, T), BF16), jax.ShapeDtypeStruct((T, 2 * KW), BF16)),
        in_specs=[pl.BlockSpec((tm, A_COLS), lambda i: (i, 0)), pl.BlockSpec((8, LANES), lambda i: (0, 0)),
                  pl.BlockSpec((tm, LANES), lambda i: (i, 0)), pl.BlockSpec((tm, LANES), lambda i: (i, 0))],
        out_specs=(pl.BlockSpec((A_COLS, tm), lambda i: (0, i)), pl.BlockSpec((tm, 2 * KW), lambda i: (i, 0))),
        compiler_params=_params(("parallel",), est),
    )(h, gains, cos2, sin2)


def _qk_rope_bwd(h, dqT, dkT, gains, cos2, sin2, *, name, tm=256):
    T = h.shape[0]
    tm = min(tm, T)
    nch = ROPE_W // LANES

    def body(h_ref, dq_ref, dk_ref, g_ref, c_ref, s_ref, dh_ref, dg_ref):
        i = pl.program_id(0)

        @pl.when(i == 0)
        def _():
            dg_ref[...] = jnp.zeros_like(dg_ref)

        lane = lax.broadcasted_iota(jnp.int32, (tm, LANES), 1)
        lo, even = lane < HEAD_DIM, lane % 2 == 0
        c, s = c_ref[...], s_ref[...]
        acc = [None, None]
        for j in range(nch):
            x = h_ref[:, j * LANES:(j + 1) * LANES]
            isq = j < QW // LANES
            g = g_ref[0:1, :] if isq else g_ref[1:2, :]
            d = dq_ref[j * LANES:(j + 1) * LANES, :].T * SCALE if isq else dk_ref[...].T
            r = lax.rsqrt(_half_sums(x * x, lo) * (1.0 / HEAD_DIM) + RMS_EPS)
            dn = d * c + _pair_swap(d * s, even)
            xr = x * r
            part = jnp.sum(dn * xr, axis=0, keepdims=True)
            acc[0 if isq else 1] = part if acc[0 if isq else 1] is None else acc[0 if isq else 1] + part
            dng = dn * g
            dx = r * dng - xr * (r * r) * (_half_sums(dng * x, lo) * (1.0 / HEAD_DIM))
            dh_ref[:, j * LANES:(j + 1) * LANES] = dx.astype(BF16)
        for row in range(2):
            folded = acc[row] + pltpu.roll(acc[row], HEAD_DIM, 1)
            dg_ref[row:row + 1, :] += folded

    est = 2 * (2 * _nbytes((tm, ROPE_W), F32) + _nbytes((tm, ROPE_W), BF16) + 2 * _nbytes((tm, LANES), F32)) + (4 << 20)
    return pl.pallas_call(
        body, name=name, grid=(T // tm,),
        out_shape=(jax.ShapeDtypeStruct((T, ROPE_W), BF16), jax.ShapeDtypeStruct((8, LANES), F32)),
        in_specs=[pl.BlockSpec((tm, ROPE_W), lambda i: (i, 0)), pl.BlockSpec((QW, tm), lambda i: (0, i)),
                  pl.BlockSpec((KW, tm), lambda i: (0, i)), pl.BlockSpec((8, LANES), lambda i: (0, 0)),
                  pl.BlockSpec((tm, LANES), lambda i: (i, 0)), pl.BlockSpec((tm, LANES), lambda i: (i, 0))],
        out_specs=(pl.BlockSpec((tm, ROPE_W), lambda i: (i, 0)), pl.BlockSpec((8, LANES), lambda i: (0, 0))),
        compiler_params=_params(("arbitrary",), est),
    )(h, dqT, dkT, gains, cos2, sin2)


def _attn_a_fwd(k, hT, *, comm=None, tq=4096, tk=2048, cq=512):
    G, T, HD = k.shape
    HE = HD + ONES_ROWS
    tq, tk = min(tq, T), min(tk, T)
    cq = min(cq, tq)
    nk, nt = T // tk, T // tq
    grid = (G, GQ * nt, nk)

    def body(*refs):
        (k_ref, qT_ref, v_ref, oT_ref, lse_ref, m_sc, acc_sc), comm_start, comm_finish = _host_comm(
            comm, refs, 3, 2, 2, grid)
        kv = pl.program_id(2)
        comm_start()
        v1T = jnp.concatenate([v_ref[...], jnp.ones((ONES_ROWS, tk), BF16)], axis=0)

        @pl.when(kv == 0)
        def _():
            m_sc[...] = jnp.full_like(m_sc, NEG)
            acc_sc[...] = jnp.zeros_like(acc_sc)

        def scores(c):
            return jnp.dot(k_ref[...], qT_ref[:, c * cq:(c + 1) * cq], preferred_element_type=F32)

        nc = tq // cq
        ahead = scores(0)
        for c in range(nc):
            cols = slice(c * cq, (c + 1) * cq)
            sT = ahead
            if c + 1 < nc:
                ahead = scores(c + 1)
            m_prev = m_sc[:, cols]
            m_new = jnp.maximum(m_prev, jnp.max(sT, axis=0, keepdims=True))
            pT = jnp.exp2(sT - m_new).astype(BF16)
            acc_sc[:, cols] = (jnp.exp2(m_prev - m_new) * acc_sc[:, cols]
                               + jnp.dot(v1T, pT, preferred_element_type=F32))
            m_sc[:, cols] = m_new

        @pl.when(kv == nk - 1)
        def _():
            l = acc_sc[HD:HD + 1, :]
            oT_ref[...] = acc_sc[0:HD, :] / l
            lse_ref[...] = m_sc[...] + jnp.log2(l)

        comm_finish()

    qtr = pl.BlockSpec((None, HD, tq), lambda g, i, j: (g * GQ + i // nt, 0, i % nt))
    qvec = pl.BlockSpec((None, 1, tq), lambda g, i, j: (g * GQ + i // nt, 0, i % nt))
    est = 6 * _nbytes((cq, tk), F32) + (8 << 20)
    hosted = comm is not None
    return pl.pallas_call(
        body, name="attn_a_fwd_comm" if hosted else "attn_a_fwd", grid=grid,
        out_shape=[jax.ShapeDtypeStruct((G * GQ, HD, T), F32), jax.ShapeDtypeStruct((G * GQ, 1, T), F32)]
        + (comm.out_shape if hosted else []),
        in_specs=[pl.BlockSpec((None, tk, HD), lambda g, i, j: (g, j, 0)), qtr,
                  pl.BlockSpec((None, HD, tk), lambda g, i, j: (A_V0 + g, 0, j))] + (comm.in_specs if hosted else []),
        out_specs=[qtr, qvec] + (comm.out_specs if hosted else []),
        scratch_shapes=[pltpu.VMEM((1, tq), F32), pltpu.VMEM((HE, tq), F32)] + (comm.scratch if hosted else []),
        compiler_params=_params(("arbitrary",) * 3 if hosted else ("parallel", "parallel", "arbitrary"), est),
    )(k, hT, hT, *(comm.parts if hosted else []))


def _attn_a_bwd(k, v, hT, doT, lse_row, delta_row, *, comm=None, tq=4096, tk=1024, cq=256):
    G, T, HD = k.shape
    tq, tk = min(tq, T), min(tk, T)
    cq = min(cq, tq)
    nqt = T // tq
    nq, nc = GQ * nqt, tq // cq
    nt = (((1,), (1,)), ((), ()))

    grid = (G, T // tk, nq)

    def body(*refs):
        (k_ref, v_ref, kT_ref, qT_ref, doT_ref, lse_ref, dl_ref, dkT_ref, dvT_ref, dqT_ref, dk_sc, dv_sc), \
            comm_start, comm_finish = _host_comm(comm, refs, 7, 3, 2, grid)
        j, i = pl.program_id(1), pl.program_id(2)
        comm_start()

        @pl.when((j == 0) & (i == 0))
        def _():
            dqT_ref[...] = jnp.zeros_like(dqT_ref)

        @pl.when(i == 0)
        def _():
            dk_sc[...] = jnp.zeros_like(dk_sc)
            dv_sc[...] = jnp.zeros_like(dv_sc)

        def scores(c):
            cols = slice(c * cq, (c + 1) * cq)
            return (jnp.dot(k_ref[...], qT_ref[:, cols], preferred_element_type=F32),
                    jnp.dot(v_ref[...], doT_ref[:, cols], preferred_element_type=F32))

        ahead = scores(0)
        dk_part = dv_part = None
        for c in range(nc):
            cols = slice(c * cq, (c + 1) * cq)
            sT, dpT = ahead
            if c + 1 < nc:
                ahead = scores(c + 1)
            pT = jnp.exp2(sT - lse_ref[:, cols])
            dsT = (pT * (dpT - dl_ref[:, cols])).astype(BF16)
            dv_c = lax.dot_general(doT_ref[:, cols], pT.astype(BF16), nt, preferred_element_type=F32)
            dk_c = lax.dot_general(qT_ref[:, cols], dsT, nt, preferred_element_type=F32)
            dv_part = dv_c if dv_part is None else dv_part + dv_c
            dk_part = dk_c if dk_part is None else dk_part + dk_c
            out_cols = pl.ds(pl.multiple_of((i % nqt) * tq + c * cq, cq), cq)
            dqT_ref[i // nqt, :, out_cols] += jnp.dot(kT_ref[...], dsT, preferred_element_type=F32)
        dk_sc[...] += dk_part
        dv_sc[...] += dv_part

        @pl.when(i == nq - 1)
        def _():
            dkT_ref[...] = dk_sc[...] * LN2
            dvT_ref[...] = dv_sc[...]

        comm_finish()

    krow = pl.BlockSpec((None, tk, HD), lambda g, j, i: (g, j, 0))
    ktr = pl.BlockSpec((None, HD, tk), lambda g, j, i: (g, 0, j))
    ktr_h = pl.BlockSpec((None, HD, tk), lambda g, j, i: (A_K0 + g, 0, j))
    qtr = pl.BlockSpec((None, HD, tq), lambda g, j, i: (g * GQ + i // nqt, 0, i % nqt))
    qvec = pl.BlockSpec((None, 1, tq), lambda g, j, i: (g * GQ + i // nqt, 0, i % nqt))
    whole = pl.BlockSpec((GQ, HD, T), lambda g, j, i: (g, 0, 0))
    est = 8 * _nbytes((cq, tk), F32) + 2 * _nbytes((GQ, HD, T), F32) + (8 << 20)
    hosted = comm is not None
    return pl.pallas_call(
        body, name="attn_a_bwd_comm" if hosted else "attn_a_bwd", grid=grid,
        out_shape=[jax.ShapeDtypeStruct((G, HD, T), F32), jax.ShapeDtypeStruct((G, HD, T), F32),
                   jax.ShapeDtypeStruct((G * GQ, HD, T), F32)] + (comm.out_shape if hosted else []),
        in_specs=[krow, krow, ktr_h, qtr, qtr, qvec, qvec] + (comm.in_specs if hosted else []),
        out_specs=[ktr, ktr, whole] + (comm.out_specs if hosted else []),
        scratch_shapes=[pltpu.VMEM((HD, tk), F32), pltpu.VMEM((HD, tk), F32)] + (comm.scratch if hosted else []),
        compiler_params=_params(("arbitrary", "arbitrary", "arbitrary"), est),
    )(k, v, hT, hT, doT, lse_row, delta_row, *(comm.parts if hosted else []))


WB = WINDOW
WK = 3 * WINDOW


QB_COL0 = (ROPE_W + KW) // (GQ * HEAD_DIM)
KB_COL = (ROPE_W + KW + QW) // KW
GW = GQ * HEAD_DIM


def _win_in_specs(T):
    nb = T // WB
    q = [pl.BlockSpec((WB, GW), functools.partial(lambda n, g: (n, QB_COL0 + g), g=g)) for g in range(KV)]
    kv = [pl.BlockSpec((WB, KW), functools.partial(lambda n, o, c: (jnp.clip(n + o, 0, nb - 1), c), o=o, c=c))
          for c in (KB_COL, KB_COL + 1) for o in (-1, 0, 1)]
    return nb, q + kv


def _win_valid(n, T):
    kabs = n * WB - WB + lax.broadcasted_iota(jnp.int32, (WK, GQ * WB), 0)
    return (kabs >= 0) & (kabs < T)


def _heads_to_lanes(t):
    return jnp.concatenate([t[i * HEAD_DIM:(i + 1) * HEAD_DIM] for i in range(GQ)], axis=1)


def _lanes_to_heads(t):
    return jnp.concatenate([t[:, i * WB:(i + 1) * WB] for i in range(GQ)], axis=0)


def _attn_b_fwd(h, biasT, sink_rows):
    T = h.shape[0]
    nb, in_specs = _win_in_specs(T)

    def body(q0, q1, k0, k1, k2, v0, v1, v2, b_ref, sk_ref, o_ref, lse_ref):
        n = pl.program_id(0)
        kwin = jnp.concatenate([k0[...], k1[...], k2[...]], axis=0)
        vT = jnp.concatenate([v0[...], v1[...], v2[...]], axis=0).T
        valid = _win_valid(n, T)
        qT = [_heads_to_lanes((q[...] * SCALE).T).astype(BF16) for q in (q0, q1)]
        sT = [jnp.dot(kwin[:, g * HEAD_DIM:(g + 1) * HEAD_DIM].astype(BF16), qT[g], preferred_element_type=F32)
              for g in range(KV)]
        oT = []
        for g in range(KV):
            s = jnp.where(valid, sT[g] + b_ref[g], NEG)
            sk = sk_ref[g]
            m = jnp.maximum(jnp.max(s, axis=0, keepdims=True), sk)
            p = jnp.exp(s - m)
            den = jnp.sum(p, axis=0, keepdims=True) + jnp.exp(sk - m)
            o = jnp.dot(vT[g * HEAD_DIM:(g + 1) * HEAD_DIM].astype(BF16), p.astype(BF16),
                        preferred_element_type=F32) / den
            lse_ref[g] = m + jnp.log(den)
            oT.append(_lanes_to_heads(o))
        o_ref[...] = jnp.concatenate(oT, axis=0).T

    whole = lambda *shape: pl.BlockSpec(shape, lambda n: (0,) * len(shape))
    return pl.pallas_call(
        body, name="attn_b_fwd", grid=(nb,),
        out_shape=(jax.ShapeDtypeStruct((T, QW), F32), jax.ShapeDtypeStruct((nb, KV, 1, GQ * WB), F32)),
        in_specs=in_specs + [whole(KV, WK, GQ * WB), whole(KV, 1, GQ * WB)],
        out_specs=(pl.BlockSpec((WB, QW), lambda n: (n, 0)),
                   pl.BlockSpec((None, KV, 1, GQ * WB), lambda n: (n, 0, 0, 0))),
        compiler_params=_params(("parallel",), 24 << 20),
    )(*([h] * 8), biasT, sink_rows)


def _attn_b_bwd(h, do, o, lse, biasT, sink_rows):
    T = h.shape[0]
    nb, in_specs = _win_in_specs(T)
    Tp = T + 2 * WB
    nt = (((1,), (1,)), ((), ()))

    def body(q0, q1, k0, k1, k2, v0, v1, v2, do_ref, o_ref, lse_ref, b_ref, sk_ref,
             dq_ref, dk_ref, dv_ref, db_ref, dsk_ref):
        n = pl.program_id(0)

        @pl.when(n == 0)
        def _():
            dk_ref[...] = jnp.zeros_like(dk_ref)
            dv_ref[...] = jnp.zeros_like(dv_ref)
            db_ref[...] = jnp.zeros_like(db_ref)
            dsk_ref[...] = jnp.zeros_like(dsk_ref)

        kwin = jnp.concatenate([k0[...], k1[...], k2[...]], axis=0)
        vwin = jnp.concatenate([v0[...], v1[...], v2[...]], axis=0)
        kT = kwin.T
        valid = _win_valid(n, T)
        doT_all, oT_all = do_ref[...].T, o_ref[...].T
        qT, doT, delta, sT, dpT = [], [], [], [], []
        for g, q in enumerate((q0, q1)):
            hd = slice(g * HEAD_DIM, (g + 1) * HEAD_DIM)
            qT.append(_heads_to_lanes((q[...] * SCALE).T).astype(BF16))
            d = _heads_to_lanes(doT_all[g * GW:(g + 1) * GW])
            delta.append(jnp.sum(d * _heads_to_lanes(oT_all[g * GW:(g + 1) * GW]), axis=0, keepdims=True))
            doT.append(d.astype(BF16))
            sT.append(jnp.dot(kwin[:, hd].astype(BF16), qT[g], preferred_element_type=F32))
            dpT.append(jnp.dot(vwin[:, hd].astype(BF16), doT[g], preferred_element_type=F32))
        dq, dk, dv = [], [], []
        for g in range(KV):
            lse_g = lse_ref[g]
            p = jnp.exp(jnp.where(valid, sT[g] + b_ref[g], NEG) - lse_g)
            ds = p * (dpT[g] - delta[g])
            db_ref[g] += ds
            dsk_ref[g] -= jnp.exp(sk_ref[g] - lse_g) * delta[g]
            dsb = ds.astype(BF16)
            dqT = jnp.dot(kT[g * HEAD_DIM:(g + 1) * HEAD_DIM].astype(BF16), dsb, preferred_element_type=F32)
            dq.append(_lanes_to_heads(dqT))
            dk.append(lax.dot_general(dsb, qT[g], nt, preferred_element_type=F32))
            dv.append(lax.dot_general(p.astype(BF16), doT[g], nt, preferred_element_type=F32))
        dq_ref[...] = (jnp.concatenate(dq, axis=0).T * SCALE).astype(BF16)
        win = pl.ds(pl.multiple_of(n * WB, WB), WK)
        dk_ref[win, :] += jnp.concatenate(dk, axis=1)
        dv_ref[win, :] += jnp.concatenate(dv, axis=1)

    whole = lambda *shape: pl.BlockSpec(shape, lambda n: (0,) * len(shape))
    tok = pl.BlockSpec((WB, QW), lambda n: (n, 0))
    return pl.pallas_call(
        body, name="attn_b_bwd", grid=(nb,),
        out_shape=(jax.ShapeDtypeStruct((T, QW), BF16),
                   jax.ShapeDtypeStruct((Tp, KW), F32), jax.ShapeDtypeStruct((Tp, KW), F32),
                   jax.ShapeDtypeStruct((KV, WK, GQ * WB), F32), jax.ShapeDtypeStruct((KV, 1, GQ * WB), F32)),
        in_specs=in_specs + [tok, tok, pl.BlockSpec((None, KV, 1, GQ * WB), lambda n: (n, 0, 0, 0)),
                             whole(KV, WK, GQ * WB), whole(KV, 1, GQ * WB)],
        out_specs=(tok, whole(Tp, KW), whole(Tp, KW), whole(KV, WK, GQ * WB), whole(KV, 1, GQ * WB)),
        compiler_params=_params(("arbitrary",), 40 << 20),
    )(*([h] * 8), do, o, lse, biasT, sink_rows)


def _bias_table(rel_bias_t, bucket):
    nh, n = rel_bias_t.shape[0], bucket.shape[1]

    def body(rb_ref, bk_ref, o_ref):
        bk = bk_ref[...]
        out = jnp.full((nh, n), NEG, F32)
        for b in range(N_BUCKETS):
            out = jnp.where(bk == b, rb_ref[:, b:b + 1], out)
        o_ref[...] = out

    return pl.pallas_call(
        body, name="bias_table", out_shape=jax.ShapeDtypeStruct((nh, n), F32),
        compiler_params=pltpu.CompilerParams(vmem_limit_bytes=32 << 20),
    )(rel_bias_t, bucket)


def _bias_sink_grads(db_list, dsk_list, bucket):
    L = len(db_list)

    def body(*refs):
        db_refs, dsk_refs, bk_ref = refs[:L], refs[L:2 * L], refs[2 * L]
        drb_ref, dsink_ref = refs[2 * L + 1], refs[2 * L + 2]
        tot = db_refs[0][...]
        for r in db_refs[1:]:
            tot = tot + r[...]
        bk = bk_ref[...]
        lane = lax.broadcasted_iota(jnp.int32, (2 * GQ, N_BUCKETS), 1)
        out = jnp.zeros((2 * GQ, N_BUCKETS), F32)
        for b in range(N_BUCKETS):
            sb = jnp.sum(jnp.where(bk == b, tot, 0.0), axis=1, keepdims=True)
            out = jnp.where(lane == b, sb, out)
        drb_ref[...] = out
        for l in range(L):
            dsink_ref[l] = jnp.sum(dsk_refs[l][...], axis=1, keepdims=True)

    return pl.pallas_call(
        body, name="bias_sink_grads",
        out_shape=(jax.ShapeDtypeStruct((2 * GQ, N_BUCKETS), F32), jax.ShapeDtypeStruct((L, 2 * GQ, 1), F32)),
        compiler_params=pltpu.CompilerParams(vmem_limit_bytes=32 << 20),
    )(*db_list, *dsk_list, bucket)


def _outnorm_fwd(oaT, ob, ga, gb, *, tm=512):
    T = ob.shape[0]
    tm = min(tm, T)

    def body(oaT_ref, ob_ref, ga_ref, gb_ref, y_ref):
        for j, (o, g_ref) in enumerate(((oaT_ref[...].T, ga_ref), (ob_ref[...], gb_ref))):
            r = lax.rsqrt(jnp.mean(o * o, axis=1, keepdims=True) + RMS_EPS)
            y_ref[:, j * QW:(j + 1) * QW] = (o * r * g_ref[...]).astype(BF16)

    half = pl.BlockSpec((tm, QW), lambda i: (i, 0))
    halfT = pl.BlockSpec((QW, tm), lambda i: (0, i))
    vec = pl.BlockSpec((1, QW), lambda i: (0, 0))
    return pl.pallas_call(
        body, name="outnorm_fwd", grid=(T // tm,),
        out_shape=jax.ShapeDtypeStruct((T, 2 * QW), BF16),
        in_specs=[halfT, half, vec, vec], out_specs=pl.BlockSpec((tm, 2 * QW), lambda i: (i, 0)),
        compiler_params=_params(("parallel",), 16 << 20),
    )(oaT, ob, ga, gb)


def _outnorm_bwd(dy, oaT, ob, ga, gb, *, tm=512):
    T = ob.shape[0]
    tm = min(tm, T)
    nh = QW // HEAD_DIM

    def body(dy_ref, oaT_ref, ob_ref, ga_ref, gb_ref, doaT_ref, dl_ref, dob_ref, dg_ref):
        i = pl.program_id(0)

        @pl.when(i == 0)
        def _():
            dg_ref[...] = jnp.zeros_like(dg_ref)

        oaT = oaT_ref[...]
        for j, (o, g_ref) in enumerate(((oaT.T, ga_ref), (ob_ref[...], gb_ref))):
            d = dy_ref[:, j * QW:(j + 1) * QW]
            r = lax.rsqrt(jnp.mean(o * o, axis=1, keepdims=True) + RMS_EPS)
            orr = o * r
            dg_ref[j:j + 1, :] += jnp.sum(d * orr, axis=0, keepdims=True)
            dgv = d * g_ref[...]
            do = r * dgv - orr * (r * r) * jnp.mean(dgv * o, axis=1, keepdims=True)
            if j == 0:
                doT = do.T
                doaT_ref[...] = doT.astype(BF16)
                prod = doT * oaT
                dl_ref[...] = jnp.concatenate(
                    [jnp.sum(prod[a * HEAD_DIM:(a + 1) * HEAD_DIM], axis=0, keepdims=True) for a in range(nh)], axis=0)
            else:
                dob_ref[...] = do

    half = pl.BlockSpec((tm, QW), lambda i: (i, 0))
    halfT = pl.BlockSpec((QW, tm), lambda i: (0, i))
    vec = pl.BlockSpec((1, QW), lambda i: (0, 0))
    return pl.pallas_call(
        body, name="outnorm_bwd", grid=(T // tm,),
        out_shape=(jax.ShapeDtypeStruct((QW, T), BF16), jax.ShapeDtypeStruct((nh, T), F32),
                   jax.ShapeDtypeStruct((T, QW), F32), jax.ShapeDtypeStruct((8, QW), F32)),
        in_specs=[pl.BlockSpec((tm, 2 * QW), lambda i: (i, 0)), halfT, half, vec, vec],
        out_specs=(halfT, pl.BlockSpec((nh, tm), lambda i: (0, i)), half, pl.BlockSpec((8, QW), lambda i: (0, 0))),
        compiler_params=_params(("arbitrary",), 32 << 20),
    )(dy, oaT, ob, ga, gb)


GELU_C = math.sqrt(2.0 / math.pi)
GELU_A = 0.044715
HALO = 16


def _gelu_parts(x):
    t = jnp.tanh(GELU_C * (x + GELU_A * (x * x * x)))
    return 0.5 * (1.0 + t), t


def _halo_specs(tm, tn, T):
    nh = tm // HALO
    last = T // HALO - 1
    cur = pl.BlockSpec((tm, tn), lambda j, i: (i, j))
    prev = pl.BlockSpec((HALO, tn), lambda j, i: (jnp.maximum(i * nh - 1, 0), j))
    nxt = pl.BlockSpec((HALO, tn), lambda j, i: (jnp.minimum((i + 1) * nh, last), j))
    return cur, prev, nxt


def _conv_glu_fwd(g, u, conv_w, conv_b, *, tm=256, tn=1408):
    T, F = g.shape
    tm, tn = min(tm, T), min(tn, F)
    cur, prev, nxt = _halo_specs(tm, tn, T)

    def body(g_ref, gp_ref, gn_ref, u_ref, w_ref, b_ref, a_ref):
        i = pl.program_id(1)
        gv = g_ref[...]
        row = lax.broadcasted_iota(jnp.int32, (tm, tn), 0)
        before = jnp.where(i * tm > 0, gp_ref[HALO - 1:HALO, :], 0.0)
        after = jnp.where((i + 1) * tm < T, gn_ref[0:1, :], 0.0)
        gm1 = jnp.where(row == 0, before, pltpu.roll(gv, 1, 0))
        gp1 = jnp.where(row == tm - 1, after, pltpu.roll(gv, tm - 1, 0))
        gc = ((b_ref[...] + gm1 * w_ref[0:1, :]) + gv * w_ref[1:2, :]) + gp1 * w_ref[2:3, :]
        cdf, _ = _gelu_parts(gc)
        a_ref[...] = (gc * cdf * u_ref[...].astype(F32)).astype(BF16)

    wspec = pl.BlockSpec((8, tn), lambda j, i: (0, j))
    est = 2 * (3 * _nbytes((tm, tn), F32)) + 8 * _nbytes((tm, tn), F32)
    return pl.pallas_call(
        body, name="conv_glu_fwd", grid=(F // tn, T // tm),
        out_shape=jax.ShapeDtypeStruct((T, F), BF16),
        in_specs=[cur, prev, nxt, cur, wspec, pl.BlockSpec((1, tn), lambda j, i: (0, j))],
        out_specs=cur,
        compiler_params=_params(("parallel", "parallel"), est),
    )(g, g, g, u, conv_w, conv_b)


def _conv_glu_bwd(dact, g, u, conv_w, conv_b, *, tm=256, tn=1408):
    T, F = g.shape
    tm, tn = min(tm, T), min(tn, F)
    cur, prev, nxt = _halo_specs(tm, tn, T)
    te = tm + 2 * HALO

    def body(d_ref, dp_ref, dn_ref, g_ref, gp_ref, gn_ref, u_ref, up_ref, un_ref, w_ref, b_ref,
             dg_ref, du_ref, dc_ref):
        i = pl.program_id(1)

        @pl.when(i == 0)
        def _():
            dc_ref[...] = jnp.zeros_like(dc_ref)

        grow = i * tm - HALO + lax.broadcasted_iota(jnp.int32, (te, tn), 0)
        valid = (grow >= 0) & (grow < T)
        ge = jnp.where(valid, jnp.concatenate([gp_ref[...], g_ref[...], gn_ref[...]], axis=0), 0.0)
        ue = jnp.concatenate([up_ref[...], u_ref[...], un_ref[...]], axis=0).astype(F32)
        de = jnp.concatenate([dp_ref[...], d_ref[...], dn_ref[...]], axis=0).astype(F32)
        w0, w1, w2 = w_ref[0:1, :], w_ref[1:2, :], w_ref[2:3, :]
        gm1 = pltpu.roll(ge, 1, 0)
        gp1 = pltpu.roll(ge, te - 1, 0)
        gc = ((b_ref[...] + gm1 * w0) + ge * w1) + gp1 * w2
        cdf, t = _gelu_parts(gc)
        dgelu = cdf + 0.5 * gc * (1.0 - t * t) * (GELU_C * (1.0 + 3.0 * GELU_A * (gc * gc)))
        dgc = jnp.where(valid, de * ue * dgelu, 0.0)
        dge = w0 * pltpu.roll(dgc, te - 1, 0) + w1 * dgc + w2 * pltpu.roll(dgc, 1, 0)
        mid = slice(HALO, HALO + tm)
        dg_ref[...] = dge[mid].astype(BF16)
        du_ref[...] = (de[mid] * (gc[mid] * cdf[mid])).astype(BF16)
        dgm = dgc[mid]
        dc_ref[0:1, :] += jnp.sum(dgm * gm1[mid], axis=0, keepdims=True)
        dc_ref[1:2, :] += jnp.sum(dgm * ge[mid], axis=0, keepdims=True)
        dc_ref[2:3, :] += jnp.sum(dgm * gp1[mid], axis=0, keepdims=True)
        dc_ref[3:4, :] += jnp.sum(dgm, axis=0, keepdims=True)

    wspec = pl.BlockSpec((8, tn), lambda j, i: (0, j))
    est = 2 * (3 * _nbytes((tm, tn), F32) + 2 * _nbytes((tm, tn), BF16)) + 16 * _nbytes((te, tn), F32)
    return pl.pallas_call(
        body, name="conv_glu_bwd", grid=(F // tn, T // tm),
        out_shape=(jax.ShapeDtypeStruct((T, F), BF16), jax.ShapeDtypeStruct((T, F), BF16),
                   jax.ShapeDtypeStruct((8, F), F32)),
        in_specs=[cur, prev, nxt, cur, prev, nxt, cur, prev, nxt, wspec, pl.BlockSpec((1, tn), lambda j, i: (0, j))],
        out_specs=(cur, cur, wspec),
        compiler_params=_params(("parallel", "arbitrary"), est),
    )(dact, dact, dact, g, g, g, u, u, u, conv_w, conv_b)


def _adamw_math(w, g, m, v):
    m = ADAM_B1 * m + (1.0 - ADAM_B1) * g
    v = ADAM_B2 * v + (1.0 - ADAM_B2) * (g * g)
    m_hat = m / (1.0 - ADAM_B1 ** ADAM_STEP)
    v_hat = v / (1.0 - ADAM_B2 ** ADAM_STEP)
    delta = -ADAM_LR * (m_hat / (jnp.sqrt(v_hat) + ADAM_EPS) + ADAM_WD * w)
    return delta, m, v


def _adamw(w, m, v, gparts, *, name, tr):
    R, C = w.shape
    tr = min(tr, R)
    assert R % tr == 0

    def body(w_ref, m_ref, v_ref, gp_ref, g_ref, d_ref, nm_ref, nv_ref):
        g = gp_ref[0].astype(F32)
        for j in range(1, N_DEV):
            g = g + gp_ref[j].astype(F32)
        delta, nm, nv = _adamw_math(w_ref[...], g, m_ref[...], v_ref[...])
        g_ref[...] = g
        d_ref[...] = delta
        nm_ref[...] = nm
        nv_ref[...] = nv

    blk = pl.BlockSpec((tr, C), lambda i: (i, 0))
    out = jax.ShapeDtypeStruct((R, C), F32)
    return pl.pallas_call(
        body, name=name, grid=(R // tr,), out_shape=(out, out, out, out),
        in_specs=[blk, blk, blk, pl.BlockSpec((N_DEV, tr, C), lambda i: (0, i, 0))],
        out_specs=(blk, blk, blk, blk),
        compiler_params=_params(("parallel",), 24 << 20),
    )(w, m, v, gparts)


def _rope_tables(T):
    rows_n = T // GRID_W
    row = jnp.repeat(jnp.arange(rows_n, dtype=F32), GRID_W)
    col = jnp.tile(jnp.arange(GRID_W, dtype=F32), rows_n)
    half = HEAD_DIM // 2
    inv_freq = ROPE_THETA ** (-jnp.arange(0, half, 2, dtype=F32) / half)
    ang = jnp.concatenate([row[:, None] * inv_freq, col[:, None] * inv_freq], axis=-1)
    cos, sin = jnp.cos(ang), jnp.sin(ang)
    cos64 = jnp.repeat(cos, 2, axis=-1)
    sin64 = jnp.stack([-sin, sin], axis=-1).reshape(T, HEAD_DIM)
    return jnp.tile(cos64, (1, 2)), jnp.tile(sin64, (1, 2))


def _t5_bucket(rel):
    half = N_BUCKETS // 2
    max_exact = half // 2
    bucket = jnp.where(rel > 0, half, 0)
    rp = jnp.abs(rel)
    rpf = jnp.maximum(rp, 1).astype(F32)
    large = max_exact + (jnp.log(rpf / max_exact) / math.log(MAX_DISTANCE / max_exact)
                         * (half - max_exact)).astype(jnp.int32)
    large = jnp.minimum(large, half - 1)
    return bucket + jnp.where(rp < max_exact, rp, large)


def _window_buckets():
    qpos = jnp.arange(WB, dtype=jnp.int32)
    kpos = jnp.arange(WK, dtype=jnp.int32) - WB
    rel = kpos[None, :] - qpos[:, None]
    return jnp.where(jnp.abs(rel) <= WINDOW, _t5_bucket(rel), -1)


def _heads_first(a, nh):
    T = a.shape[0]
    return a.reshape(T, nh, HEAD_DIM).transpose(1, 0, 2)


def _row(v):
    return v.reshape(1, -1)


def _rows8(rows, width):
    a = jnp.stack(list(rows), axis=0)
    return jnp.pad(a, ((0, 8 - a.shape[0]), (0, 0)))


def _layer_fwd(l, xin, W, tabs, comm=None, on_comm=None):
    xhat, xg, xb, x16 = xin
    T = xhat.shape[0]
    cos2, sin2, biasT = tabs
    h = _mm([x16], [W["w_in"][l]], name="mm_in", out_dtype=F32, tm=MM_ROWS, tn=IN_COLS, tk=D_MODEL)
    gains = _rows8([jnp.tile(W["q_norm"][l], 2), jnp.tile(W["k_norm"][l], 2)], LANES)
    hT, kv_nat = _qk_rope_fwd(h, gains, cos2, sin2, name="qk_rope_fwd")
    hT = hT.reshape(A_HEADS, HEAD_DIM, T)
    ka, va = _heads_first(kv_nat[:, :KW], KV), _heads_first(kv_nat[:, KW:], KV)
    res = _attn_a_fwd(ka, hT, comm=comm)
    oaT, lse_a = res[0].reshape(QW, T), res[1]
    if comm is not None:
        on_comm(res[2:])
    sink_rows = jnp.repeat(W["sink"][l], WB).reshape(KV, 1, GQ * WB)
    ob_t, lse_b = _attn_b_fwd(h, biasT, sink_rows)
    ga, gb = _row(W["out_norm_a"][l]), _row(W["out_norm_b"][l])
    ycat = _outnorm_fwd(oaT, ob_t, ga, gb)
    g1, b1 = _row(W["ln1_g"][l]), _row(W["ln1_b"][l])
    x1hat, rstd1, x1_16 = _mm_res_ln(ycat, W["w_out"][l], xhat, xg, xb, g1, b1, name="mm_out_ln", tm=512)
    gate = _mm([x1_16], [W["w_gate"][l]], name="mm_gate", out_dtype=F32, tm=MM_ROWS, tn=D_FF // 2, tk=D_MODEL)
    up = _mm([x1_16], [W["w_up"][l]], name="mm_up", out_dtype=BF16, tm=MM_ROWS, tn=D_FF // 2, tk=D_MODEL)
    cw = jnp.pad(W["conv_w"][l], ((0, 5), (0, 0)))
    cb = _row(W["conv_b"][l])
    act = _conv_glu_fwd(gate, up, cw, cb)
    g2, b2 = _row(W["ln2_g"][l]), _row(W["ln2_b"][l])
    x2hat, rstd2, x2_16 = _mm_res_ln(act, W["w_down"][l], x1hat, g1, b1, g2, b2, name="mm_down_ln", tm=256)
    saved = dict(x16=x16, h=h, gains=gains, hT=hT, ka=ka, va=va, oaT=oaT, lse_a=lse_a,
                 lse_b=lse_b, sink_rows=sink_rows, ob_t=ob_t,
                 ga=ga, gb=gb, ycat=ycat, x1hat=x1hat, rstd1=rstd1, x1_16=x1_16, g1=g1, b1=b1, gate=gate, up=up,
                 cw=cw, cb=cb, act=act, x2hat=x2hat, rstd2=rstd2, g2=g2, b2=b2)
    return (x2hat, g2, b2, x2_16), saved


def _layer_bwd(l, S, W, tabs, dz2, dz2_16, stats2, scatter=None):
    cos2, sin2, biasT = tabs
    T = dz2.shape[0]
    G = {}
    G["ln2_g"], G["ln2_b"] = stats2[0], stats2[1]
    G["w_down"] = _mm([S["act"]], [dz2_16], name="dw_down", out_dtype=BF16, trans_a=True, tm=D_FF // 2, tn=D_MODEL, tk=DW_TOKENS)
    dact = _mm([dz2_16], [W["w_down"][l]], name="mm_dact", out_dtype=BF16, trans_b=True, tm=MM_ROWS, tn=D_FF // 2,
               tk=D_MODEL)
    dg, du, dconv = _conv_glu_bwd(dact, S["gate"], S["up"], S["cw"], S["cb"])
    G["conv_w"], G["conv_b"] = dconv[0:3], dconv[3]
    G["w_gate"] = _mm([S["x1_16"]], [dg], name="dw_gate", out_dtype=BF16, trans_a=True, tm=D_MODEL, tn=D_FF // 2, tk=DW_TOKENS)
    G["w_up"] = _mm([S["x1_16"]], [du], name="dw_up", out_dtype=BF16, trans_a=True, tm=D_MODEL, tn=D_FF // 2, tk=DW_TOKENS)
    dx1 = _mm([dg, du], [W["w_gate"][l], W["w_up"][l]], name="mm_dx1", out_dtype=F32, trans_b=True, tm=MM_ROWS_ACC,
              tn=D_MODEL, tk=D_FF // 2, add=dz2, add_scale=ALPHA)
    dz1, dz1_16, stats1 = _ln_bwd(S["x1hat"], S["rstd1"], S["g1"], S["b1"], name="ln1_bwd", dx=dx1)
    G["ln1_g"], G["ln1_b"] = stats1[0], stats1[1]
    G["w_out"] = _mm([S["ycat"]], [dz1_16], name="dw_out", out_dtype=BF16, trans_a=True, tm=D_MODEL, tn=D_MODEL, tk=DW_TOKENS)
    dycat = _mm([dz1_16], [W["w_out"][l]], name="mm_dycat", out_dtype=F32, trans_b=True, tm=MM_ROWS, tn=D_MODEL,
                tk=D_MODEL)
    doaT, delta, dob_t, dgn = _outnorm_bwd(dycat, S["oaT"], S["ob_t"], S["ga"], S["gb"])
    G["out_norm_a"], G["out_norm_b"] = dgn[0], dgn[1]
    res = _attn_a_bwd(S["ka"], S["va"], S["hT"], doaT.reshape(KV * GQ, HEAD_DIM, T), S["lse_a"],
                      delta.reshape(KV * GQ, 1, T), comm=scatter(G) if scatter is not None else None)
    dkaT, dvaT, dqaT = res[:3]
    dh_rope, dgain = _qk_rope_bwd(S["h"], dqaT.reshape(QW, T), dkaT.reshape(KW, T), S["gains"], cos2, sin2,
                                  name="qk_rope_bwd")
    G["q_norm"], G["k_norm"] = dgain[0, :HEAD_DIM], dgain[1, :HEAD_DIM]
    dqb_t, dkb, dvb, dbiasT, dsk = _attn_b_bwd(S["h"], dob_t, S["ob_t"], S["lse_b"], biasT, S["sink_rows"])
    dh = jnp.concatenate([
        dh_rope, dvaT.transpose(2, 0, 1).reshape(T, KW).astype(BF16), dqb_t,
        dkb[WB:WB + T].astype(BF16), dvb[WB:WB + T].astype(BF16)], axis=1)
    dbias = dbiasT.reshape(KV, WK, GQ, WB).transpose(0, 2, 1, 3)
    G["w_in"] = _mm([S["x16"]], [dh], name="dw_in", out_dtype=BF16, trans_a=True, tm=D_MODEL, tn=IN_COLS, tk=DW_TOKENS)
    dxin = _mm([dh], [W["w_in"][l]], name="mm_dxin", out_dtype=F32, trans_b=True, tm=MM_ROWS, tn=D_MODEL, tk=IN_COLS,
               add=dz1, add_scale=ALPHA)
    return dxin, G, dbias.reshape(KV * GQ, WK * WB), dsk.reshape(KV * GQ, WB), res[3:]


BIG = ("w_in", "w_out", "w_gate", "w_up", "w_down")
COL_SHARDED = ("w_in", "w_gate", "w_up")


def _unshard(name, blocks):
    _, r, c = blocks.shape
    if name in COL_SHARDED:
        return blocks.transpose(1, 0, 2).reshape(r, N_DEV * c)
    return blocks.reshape(N_DEV * r, c)


def _to_owner_blocks(name, full, shard_shape):
    _, r, c = shard_shape
    if name in COL_SHARDED:
        return full.reshape(r, N_DEV, c).transpose(1, 0, 2)
    return full.reshape(N_DEV, r, c)


def _pack_small(vals, tail):
    flat = jnp.concatenate([vals[n].reshape(-1).astype(F32) for n in SMALL_NAMES] + [tail])
    pad = (-flat.shape[0]) % (8 * LANES)
    return jnp.pad(flat, (0, pad)).reshape(-1, LANES)


def _unpack_small(packed, shapes):
    flat = packed.reshape(-1)
    out, off = {}, 0
    for n in SMALL_NAMES:
        size = math.prod(shapes[n])
        out[n] = flat[off:off + size].reshape(shapes[n])
        off += size
    return out, flat[off]


def kernel(x, rel_bias, w_in, q_norm, k_norm, sink, out_norm_a, out_norm_b, w_out, ln1_g, ln1_b, w_gate, w_up, conv_w, conv_b, w_down, ln2_g, ln2_b, loss_target, m_rel_bias, m_w_in, m_q_norm, m_k_norm, m_sink, m_out_norm_a, m_out_norm_b, m_w_out, m_ln1_g, m_ln1_b, m_w_gate, m_w_up, m_conv_w, m_conv_b, m_w_down, m_ln2_g, m_ln2_b, v_rel_bias, v_w_in, v_q_norm, v_k_norm, v_sink, v_out_norm_a, v_out_norm_b, v_w_out, v_ln1_g, v_ln1_b, v_w_gate, v_w_up, v_conv_w, v_conv_b, v_w_down, v_ln2_g, v_ln2_b):
    P = dict(rel_bias=rel_bias, w_in=w_in, q_norm=q_norm, k_norm=k_norm, sink=sink, out_norm_a=out_norm_a,
             out_norm_b=out_norm_b, w_out=w_out, ln1_g=ln1_g, ln1_b=ln1_b, w_gate=w_gate, w_up=w_up, conv_w=conv_w,
             conv_b=conv_b, w_down=w_down, ln2_g=ln2_g, ln2_b=ln2_b)
    M = dict(rel_bias=m_rel_bias, w_in=m_w_in, q_norm=m_q_norm, k_norm=m_k_norm, sink=m_sink, out_norm_a=m_out_norm_a,
             out_norm_b=m_out_norm_b, w_out=m_w_out, ln1_g=m_ln1_g, ln1_b=m_ln1_b, w_gate=m_w_gate, w_up=m_w_up,
             conv_w=m_conv_w, conv_b=m_conv_b, w_down=m_w_down, ln2_g=m_ln2_g, ln2_b=m_ln2_b)
    V = dict(rel_bias=v_rel_bias, w_in=v_w_in, q_norm=v_q_norm, k_norm=v_k_norm, sink=v_sink, out_norm_a=v_out_norm_a,
             out_norm_b=v_out_norm_b, w_out=v_w_out, ln1_g=v_ln1_g, ln1_b=v_ln1_b, w_gate=v_w_gate, w_up=v_w_up,
             conv_w=v_conv_w, conv_b=v_conv_b, w_down=v_w_down, ln2_g=v_ln2_g, ln2_b=v_ln2_b)
    names = list(P)
    T = x.shape[1]
    me = 4 * lax.axis_index("x") + 2 * lax.axis_index("y") + lax.axis_index("c")

    L, taps, fc = conv_w.shape
    W = {n: ([None] * DEPTH if n in BIG else P[n]) for n in names}

    def wire(n, l):
        return P[n][l].astype(BF16)

    def take(n, l, gathered):
        W[n][l] = _unshard(n, gathered)

    take("w_in", 0, _exchange([wire("w_in", 0)], [True], name="gather_w_in0")[0])
    later = [(n, l) for l in range(DEPTH) for n in BIG if (n, l) != ("w_in", 0)]
    cw_shard = conv_w.reshape(-1)
    cw_wire = jnp.pad(cw_shard, (0, (-cw_shard.shape[0]) % LANES)).reshape(-1, LANES)
    gather_rest = _Comm([wire(n, l) for n, l in later] + [cw_wire], [True] * (len(later) + 1))

    def on_gathered(outs):
        for (n, l), g in zip(later, outs):
            take(n, l, g)
        cw_all = outs[-1].reshape(N_DEV, -1)[:, :cw_shard.shape[0]].reshape(N_DEV, L, taps, fc)
        W["conv_w"] = cw_all.transpose(1, 2, 0, 3).reshape(L, taps, N_DEV * fc)

    cos2, sin2 = _rope_tables(T)
    bucket = _window_buckets()
    bias = _bias_table(rel_bias.T, bucket.reshape(1, WB * WK))
    biasT = bias.reshape(KV, GQ, WB, WK).transpose(0, 3, 1, 2).reshape(KV, WK, GQ * WB)
    tabs = (cos2, sin2, biasT)

    ones, zeros = jnp.ones((1, D_MODEL), F32), jnp.zeros((1, D_MODEL), F32)
    cur = (x[0], ones, zeros, x[0].astype(BF16))
    saved = []
    for l in range(DEPTH):
        cur, S = _layer_fwd(l, cur, W, tabs, comm=gather_rest if l == 0 else None, on_comm=on_gathered)
        saved.append(S)

    def owner_blocks(n, l):
        return _to_owner_blocks(n, grads[l][n], P[n].shape)

    early = ([(n, l) for l in range(1, DEPTH) for n in BIG] + [(n, 0) for n in BIG if n != "w_in"])

    def scatter_early(g0):
        grads[0] = g0
        return _Comm([owner_blocks(n, l) for n, l in early], [False] * len(early))

    grads = [None] * DEPTH
    dbs, dsks = [None] * DEPTH, [None] * DEPTH
    S = saved[-1]
    dz, dz16, stats = _ln_bwd(S["x2hat"], S["rstd2"], S["g2"], S["b2"], name="loss_ln2_bwd", target=loss_target[0])
    loss_part = stats[2, 0:1]
    recv = {}
    for l in reversed(range(DEPTH)):
        S = saved[l]
        dxin, grads[l], dbs[l], dsks[l], got = _layer_bwd(l, S, W, tabs, dz, dz16, stats,
                                                         scatter=scatter_early if l == 0 else None)
        if l == 0:
            recv.update(zip(early, got))
        if l > 0:
            Sp = saved[l - 1]
            dz, dz16, stats = _ln_bwd(Sp["x2hat"], Sp["rstd2"], Sp["g2"], Sp["b2"], name="ln2_bwd", dx=dxin)
    grad_x = dxin[None]

    drb, dsink = _bias_sink_grads(dbs, dsks, bucket.T.reshape(1, WK * WB))
    small_g = {n: jnp.stack([grads[l][n] for l in range(DEPTH)]) for n in SMALL_NAMES if n not in ("rel_bias", "sink")}
    small_g["rel_bias"] = drb.T
    small_g["sink"] = dsink.reshape(DEPTH, KV * GQ)
    recv[("w_in", 0)], small_recv = _exchange([owner_blocks("w_in", 0), _pack_small(small_g, loss_part)], [False, True],
                                              name="scatter_w_in0_gather_small")

    out_g, out_d, out_m, out_v = {}, {}, {}, {}
    for n in BIG:
        shp = P[n].shape
        gparts = jnp.concatenate([recv[(n, l)] for l in range(DEPTH)], axis=1)
        rows, cols = shp[0] * shp[1], shp[2]
        res = _adamw(P[n].reshape(rows, cols), M[n].reshape(rows, cols), V[n].reshape(rows, cols), gparts,
                     name="adamw_" + n, tr=math.gcd(rows, 256))
        out_g[n], out_d[n], out_m[n], out_v[n] = (r.reshape(shp) for r in res)
    full_shapes = {n: W[n].shape for n in SMALL_NAMES}

    def small_state(D):
        vals = {n: D[n] for n in SMALL_NAMES if n != "conv_w"}
        cw = jnp.zeros((L, taps, N_DEV, fc), F32)
        cw = lax.dynamic_update_slice(cw, D["conv_w"].reshape(L, taps, 1, fc), (0, 0, me, 0))
        vals["conv_w"] = cw.reshape(L, taps, N_DEV * fc)
        return _pack_small(vals, jnp.zeros((1,), F32))

    sw, sm, sv = small_state(P), small_state(M), small_state(V)
    res = _adamw(sw, sm, sv, small_recv, name="adamw_small", tr=sw.shape[0])
    loss = _unpack_small(res[0], full_shapes)[1]
    for dst, packed in zip((out_g, out_d, out_m, out_v), res):
        vals, _ = _unpack_small(packed, full_shapes)
        for n in SMALL_NAMES:
            if n == "conv_w":
                sl = lax.dynamic_slice(vals[n].reshape(L, taps, N_DEV, fc), (0, 0, me, 0), (L, taps, 1, fc))
                dst[n] = sl.reshape(L, taps, fc)
            else:
                dst[n] = vals[n]
    return (loss, grad_x, *[out_g[n] for n in names], *[out_d[n] for n in names],
            *[out_m[n] for n in names], *[out_v[n] for n in names])
```

```python
import functools
import math

import jax
import jax.numpy as jnp
from jax import lax
from jax.experimental import pallas as pl
from jax.experimental.pallas import tpu as pltpu

F32 = jnp.float32
BF16 = jnp.bfloat16
MESH = pl.DeviceIdType.MESH

N_DEV = 8
D_MODEL = 1024
DEPTH = 2
HEAD_DIM = 64
KV = 2
GQ = 4
QW = KV * GQ * HEAD_DIM
KW = KV * HEAD_DIM
ROPE_W = QW + KW
IN_COLS = 2 * (QW + 2 * KW)
D_FF = 2816
GRID_W = 64
ROPE_THETA = 10000.0
WINDOW = 128
N_BUCKETS = 32
MAX_DISTANCE = 128
ALPHA = (2.0 * DEPTH) ** 0.25
RMS_EPS = 1e-6
LN_EPS = 1e-5
SCALE = HEAD_DIM ** -0.5
LOG2E = math.log2(math.e)
LN2 = math.log(2.0)
NEG = -1e30
ONES_ROWS = 16

ADAM_LR = 0.001
ADAM_B1 = 0.9
ADAM_B2 = 0.999
ADAM_EPS = 1e-08
ADAM_WD = 0.01
ADAM_STEP = 10

LANES = 128
MM_ROWS = 1024
MM_ROWS_WIDE = 512
DW_TOKENS = 1024
VMEM_CAP = 60 * 1024 * 1024
SMALL_NAMES = ("rel_bias", "q_norm", "k_norm", "sink", "out_norm_a", "out_norm_b", "ln1_g", "ln1_b",
               "conv_b", "ln2_g", "ln2_b", "conv_w")


def _params(sem, est_bytes):
    limit = int(min(VMEM_CAP, est_bytes + (8 << 20)))
    return pltpu.CompilerParams(dimension_semantics=sem, vmem_limit_bytes=limit)


def _nbytes(shape, dtype):
    return math.prod(shape) * jnp.dtype(dtype).itemsize


class _Comm:
    def __init__(self, parts, gathers):
        self.parts, self.gathers, self.n = list(parts), list(gathers), len(parts)
        hbm = pl.BlockSpec(memory_space=pltpu.HBM)
        self.in_specs = [hbm] * self.n
        self.out_specs = [hbm] * self.n
        self.out_shape = [jax.ShapeDtypeStruct((N_DEV,) + tuple(p.shape if g else p.shape[1:]), p.dtype)
                          for p, g in zip(self.parts, self.gathers)]
        self.scratch = [pltpu.SemaphoreType.DMA((self.n * (N_DEV - 1),)), pltpu.SemaphoreType.DMA((self.n * (N_DEV - 1),)),
                        pltpu.SemaphoreType.DMA((self.n,))]

    def bind(self, ins, outs, sems):
        send_sems, recv_sems, local_sems = sems
        gathers, n = self.gathers, self.n
        me = 4 * lax.axis_index("x") + 2 * lax.axis_index("y") + lax.axis_index("c")

        def src(k, j):
            return ins[k] if gathers[k] else ins[k].at[j]

        def copy(k, d, peer, lands_in):
            return pltpu.make_async_remote_copy(
                src_ref=src(k, peer), dst_ref=outs[k].at[lands_in],
                send_sem=send_sems.at[k * (N_DEV - 1) + d - 1], recv_sem=recv_sems.at[k * (N_DEV - 1) + d - 1],
                device_id=(peer // 4, lax.rem(peer // 2, 2), lax.rem(peer, 2)), device_id_type=MESH)

        def send(k, d):
            return copy(k, d, lax.rem(me + d, N_DEV), me)

        def arrival(k, d):
            frm = lax.rem(me + N_DEV - d, N_DEV)
            return copy(k, d, frm, frm)

        def local(k):
            return pltpu.make_async_copy(src(k, me), outs[k].at[me], local_sems.at[k])

        def start():
            for k in range(n):
                local(k).start()
                for d in range(1, N_DEV):
                    send(k, d).start()

        def finish():
            for k in range(n):
                for d in range(1, N_DEV):
                    arrival(k, d).wait_recv()
            for k in range(n):
                for d in range(1, N_DEV):
                    send(k, d).wait_send()
                local(k).wait()

        return start, finish


def _host_comm(comm, refs, n_in, n_out, n_scratch, grid):
    n = comm.n if comm is not None else 0
    own_in, cin = refs[:n_in], refs[n_in:n_in + n]
    own_out, cout = refs[n_in + n:n_in + n + n_out], refs[n_in + n + n_out:n_in + 2 * n + n_out]
    base = n_in + 2 * n + n_out
    own_scratch, sems = refs[base:base + n_scratch], refs[base + n_scratch:]
    own = tuple(own_in) + tuple(own_out) + tuple(own_scratch)
    if comm is None:
        return own, lambda: None, lambda: None
    start, finish = comm.bind(cin, cout, sems)
    first = last = None
    for ax, size in enumerate(grid):
        pid = pl.program_id(ax)
        first = (pid == 0) if first is None else first & (pid == 0)
        last = (pid == size - 1) if last is None else last & (pid == size - 1)
    return own, lambda: pl.when(first)(start), lambda: pl.when(last)(finish)


def _exchange(parts, gathers, name):
    comm = _Comm(parts, gathers)
    n = comm.n

    def body(*refs):
        start, finish = comm.bind(refs[:n], refs[n:2 * n], refs[2 * n:])
        start()
        finish()

    return pl.pallas_call(body, name=name, out_shape=comm.out_shape, in_specs=comm.in_specs, out_specs=comm.out_specs,
                          scratch_shapes=comm.scratch)(*comm.parts)


def _mm(a_list, b_list, *, name, out_dtype, tm, tn, tk, trans_a=False, trans_b=False, add=None, add_scale=1.0):
    assert not (trans_a and trans_b)
    na = len(a_list)
    if trans_a:
        K, M = a_list[0].shape
    else:
        M, K = a_list[0].shape
    N = b_list[0].shape[0 if trans_b else 1]
    tm, tn, tk = min(tm, M), min(tn, N), min(tk, K)
    assert M % tm == 0 and N % tn == 0 and K % tk == 0, (name, M, N, K, tm, tn, tk)
    nk = K // tk
    dims = (((0,), (0,)), ((), ())) if trans_a else (((1,), (1 if trans_b else 0,)), ((), ()))

    def body(*refs):
        a_refs, b_refs = refs[:na], refs[na:2 * na]
        add_ref = refs[2 * na] if add is not None else None
        o_ref = refs[2 * na + (add is not None)]
        k = pl.program_id(2)

        part = None
        for a_ref, b_ref in zip(a_refs, b_refs):
            prod = lax.dot_general(a_ref[...].astype(BF16), b_ref[...].astype(BF16), dims,
                                   preferred_element_type=F32)
            part = prod if part is None else part + prod

        def finish(res):
            if add_ref is not None:
                res = res + add_scale * add_ref[...]
            o_ref[...] = res.astype(o_ref.dtype)

        if nk == 1:
            finish(part)
        else:
            acc_ref = refs[-1]

            @pl.when(k == 0)
            def _():
                acc_ref[...] = part

            @pl.when(k > 0)
            def _():
                acc_ref[...] += part

            @pl.when(k == nk - 1)
            def _():
                finish(acc_ref[...])

    if trans_a:
        a_spec = pl.BlockSpec((tk, tm), lambda i, j, k: (k, i))
    else:
        a_spec = pl.BlockSpec((tm, tk), lambda i, j, k: (i, k))
    if trans_b:
        b_spec = pl.BlockSpec((tn, tk), lambda i, j, k: (j, k))
    else:
        b_spec = pl.BlockSpec((tk, tn), lambda i, j, k: (k, j))
    o_spec = pl.BlockSpec((tm, tn), lambda i, j, k: (i, j))
    in_specs = [a_spec] * na + [b_spec] * na + ([o_spec] if add is not None else [])
    est = (2 * na * (_nbytes((tm, tk), a_list[0].dtype) + _nbytes((tk, tn), b_list[0].dtype))
           + na * (_nbytes((tm, tk), BF16) + _nbytes((tk, tn), BF16))
           + 2 * _nbytes((tm, tn), out_dtype) + 3 * _nbytes((tm, tn), F32)
           + (2 * _nbytes((tm, tn), F32) if add is not None else 0))
    args = list(a_list) + list(b_list) + ([add] if add is not None else [])
    return pl.pallas_call(
        body, name=name, grid=(M // tm, N // tn, nk),
        out_shape=jax.ShapeDtypeStruct((M, N), out_dtype),
        in_specs=in_specs, out_specs=o_spec,
        scratch_shapes=[pltpu.VMEM((tm, tn), F32)] if nk > 1 else [],
        compiler_params=_params(("parallel", "parallel", "arbitrary"), est),
    )(*args)


def _mm_res_ln(a, w, res_hat, res_g, res_b, ln_g, ln_b, *, name, tm):
    T, K = a.shape
    D = w.shape[1]
    tm = min(tm, T)

    def body(a_ref, w_ref, rh_ref, rg_ref, rb_ref, g_ref, b_ref, xhat_ref, rstd_ref, xb_ref):
        branch = jnp.dot(a_ref[...].astype(BF16), w_ref[...], preferred_element_type=F32)
        z = ALPHA * (rh_ref[...] * rg_ref[...] + rb_ref[...]) + branch
        mu = jnp.mean(z, axis=1, keepdims=True)
        zc = z - mu
        var = jnp.mean(zc * zc, axis=1, keepdims=True)
        rstd = lax.rsqrt(var + LN_EPS)
        xhat = zc * rstd
        xhat_ref[...] = xhat
        rstd_ref[...] = rstd
        xb_ref[...] = (xhat * g_ref[...] + b_ref[...]).astype(BF16)

    row = pl.BlockSpec((tm, D), lambda i: (i, 0))
    vec = pl.BlockSpec((1, D), lambda i: (0, 0))
    est = (2 * (_nbytes((tm, K), a.dtype) + _nbytes((K, D), BF16)) + 4 * _nbytes((tm, D), F32) * 2
           + 6 * _nbytes((tm, D), F32))
    return pl.pallas_call(
        body, name=name, grid=(T // tm,),
        out_shape=(jax.ShapeDtypeStruct((T, D), F32), jax.ShapeDtypeStruct((T, 1), F32),
                   jax.ShapeDtypeStruct((T, D), BF16)),
        in_specs=[pl.BlockSpec((tm, K), lambda i: (i, 0)), pl.BlockSpec((K, D), lambda i: (0, 0)), row, vec, vec, vec, vec],
        out_specs=(row, pl.BlockSpec((tm, 1), lambda i: (i, 0)), row),
        compiler_params=_params(("parallel",), est),
    )(a, w, res_hat, res_g, res_b, ln_g, ln_b)


def _ln_bwd(xhat, rstd, ln_g, ln_b, *, name, dx=None, target=None, tm=256):
    T, D = xhat.shape
    tm = min(tm, T)
    head = target is not None

    def body(xhat_ref, rstd_ref, g_ref, b_ref, d_ref, dz_ref, dzb_ref, st_ref):
        i = pl.program_id(0)

        @pl.when(i == 0)
        def _():
            st_ref[...] = jnp.zeros_like(st_ref)

        xh = xhat_ref[...]
        g = g_ref[...]
        if head:
            err = (xh * g + b_ref[...]) - d_ref[...]
            dxv = err * (1.0 / D)
            st_ref[2:3, :] += 0.5 * jnp.sum(jnp.sum(err * err, axis=1, keepdims=True) * (1.0 / D), axis=0, keepdims=True)
        else:
            dxv = d_ref[...]
        st_ref[0:1, :] += jnp.sum(dxv * xh, axis=0, keepdims=True)
        st_ref[1:2, :] += jnp.sum(dxv, axis=0, keepdims=True)
        dxh = dxv * g
        m1 = jnp.mean(dxh, axis=1, keepdims=True)
        m2 = jnp.mean(dxh * xh, axis=1, keepdims=True)
        dz = rstd_ref[...] * (dxh - m1 - xh * m2)
        dz_ref[...] = dz
        dzb_ref[...] = dz.astype(BF16)

    row = pl.BlockSpec((tm, D), lambda i: (i, 0))
    vec = pl.BlockSpec((1, D), lambda i: (0, 0))
    est = 2 * 4 * _nbytes((tm, D), F32) + 6 * _nbytes((tm, D), F32)
    return pl.pallas_call(
        body, name=name, grid=(T // tm,),
        out_shape=(jax.ShapeDtypeStruct((T, D), F32), jax.ShapeDtypeStruct((T, D), BF16),
                   jax.ShapeDtypeStruct((8, D), F32)),
        in_specs=[row, pl.BlockSpec((tm, 1), lambda i: (i, 0)), vec, vec, row],
        out_specs=(row, row, pl.BlockSpec((8, D), lambda i: (0, 0))),
        compiler_params=_params(("arbitrary",), est),
    )(xhat, rstd, ln_g, ln_b, target if head else dx)


def _pair_swap(v, even):
    return jnp.where(even, pltpu.roll(v, LANES - 1, 1), pltpu.roll(v, 1, 1))


def _half_sums(v, lo):
    s_lo = jnp.sum(jnp.where(lo, v, 0.0), axis=1, keepdims=True)
    s_hi = jnp.sum(jnp.where(lo, 0.0, v), axis=1, keepdims=True)
    return jnp.where(lo, s_lo, s_hi)


A_COLS = ROPE_W + KW
A_HEADS = A_COLS // HEAD_DIM
A_K0, A_V0 = KV * GQ, KV * GQ + KV


def _qk_rope_fwd(h, gains, cos2, sin2, *, name, tm=256):
    T = h.shape[0]
    tm = min(tm, T)
    nch = A_COLS // LANES

    def body(h_ref, g_ref, c_ref, s_ref, oT_ref, kv_ref):
        lane = lax.broadcasted_iota(jnp.int32, (tm, LANES), 1)
        lo, even = lane < HEAD_DIM, lane % 2 == 0
        c, s = c_ref[...], s_ref[...]
        for j in range(nch):
            x = h_ref[:, j * LANES:(j + 1) * LANES]
            isq, isv = j < QW // LANES, j == nch - 1
            if isv:
                out = x
            else:
                g = g_ref[0:1, :] if isq else g_ref[1:2, :]
                r = lax.rsqrt(_half_sums(x * x, lo) * (1.0 / HEAD_DIM) + RMS_EPS)
                nrm = x * r * g
                out = nrm * c + _pair_swap(nrm, even) * s
            if isq:
                out = out * (SCALE * LOG2E)
            else:
                kv_ref[:, (j - QW // LANES) * LANES:(j - QW // LANES + 1) * LANES] = out.astype(BF16)
            oT_ref[j * LANES:(j + 1) * LANES, :] = out.T.astype(BF16)

    est = 2 * (_nbytes((tm, A_COLS), F32) + 2 * _nbytes((tm, A_COLS), BF16) + 2 * _nbytes((tm, LANES), F32)) + (4 << 20)
    return pl.pallas_call(
        body, name=name, grid=(T // tm,),
        out_shape=(jax.ShapeDtypeStruct((A_COLS, T), BF16), jax.ShapeDtypeStruct((T, 2 * KW), BF16)),
        in_specs=[pl.BlockSpec((tm, A_COLS), lambda i: (i, 0)), pl.BlockSpec((8, LANES), lambda i: (0, 0)),
                  pl.BlockSpec((tm, LANES), lambda i: (i, 0)), pl.BlockSpec((tm, LANES), lambda i: (i, 0))],
        out_specs=(pl.BlockSpec((A_COLS, tm), lambda i: (0, i)), pl.BlockSpec((tm, 2 * KW), lambda i: (i, 0))),
        compiler_params=_params(("parallel",), est),
    )(h, gains, cos2, sin2)


def _qk_rope_bwd(h, dqT, dkT, gains, cos2, sin2, *, name, tm=256):
    T = h.shape[0]
    tm = min(tm, T)
    nch = ROPE_W // LANES

    def body(h_ref, dq_ref, dk_ref, g_ref, c_ref, s_ref, dh_ref, dg_ref):
        i = pl.program_id(0)

        @pl.when(i == 0)
        def _():
            dg_ref[...] = jnp.zeros_like(dg_ref)

        lane = lax.broadcasted_iota(jnp.int32, (tm, LANES), 1)
        lo, even = lane < HEAD_DIM, lane % 2 == 0
        c, s = c_ref[...], s_ref[...]
        acc = [None, None]
        for j in range(nch):
            x = h_ref[:, j * LANES:(j + 1) * LANES]
            isq = j < QW // LANES
            g = g_ref[0:1, :] if isq else g_ref[1:2, :]
            d = dq_ref[j * LANES:(j + 1) * LANES, :].T * SCALE if isq else dk_ref[...].T
            r = lax.rsqrt(_half_sums(x * x, lo) * (1.0 / HEAD_DIM) + RMS_EPS)
            dn = d * c + _pair_swap(d * s, even)
            xr = x * r
            part = jnp.sum(dn * xr, axis=0, keepdims=True)
            acc[0 if isq else 1] = part if acc[0 if isq else 1] is None else acc[0 if isq else 1] + part
            dng = dn * g
            dx = r * dng - xr * (r * r) * (_half_sums(dng * x, lo) * (1.0 / HEAD_DIM))
            dh_ref[:, j * LANES:(j + 1) * LANES] = dx.astype(BF16)
        for row in range(2):
            folded = acc[row] + pltpu.roll(acc[row], HEAD_DIM, 1)
            dg_ref[row:row + 1, :] += folded

    est = 2 * (2 * _nbytes((tm, ROPE_W), F32) + _nbytes((tm, ROPE_W), BF16) + 2 * _nbytes((tm, LANES), F32)) + (4 << 20)
    return pl.pallas_call(
        body, name=name, grid=(T // tm,),
        out_shape=(jax.ShapeDtypeStruct((T, ROPE_W), BF16), jax.ShapeDtypeStruct((8, LANES), F32)),
        in_specs=[pl.BlockSpec((tm, ROPE_W), lambda i: (i, 0)), pl.BlockSpec((QW, tm), lambda i: (0, i)),
                  pl.BlockSpec((KW, tm), lambda i: (0, i)), pl.BlockSpec((8, LANES), lambda i: (0, 0)),
                  pl.BlockSpec((tm, LANES), lambda i: (i, 0)), pl.BlockSpec((tm, LANES), lambda i: (i, 0))],
        out_specs=(pl.BlockSpec((tm, ROPE_W), lambda i: (i, 0)), pl.BlockSpec((8, LANES), lambda i: (0, 0))),
        compiler_params=_params(("arbitrary",), est),
    )(h, dqT, dkT, gains, cos2, sin2)


def _attn_a_fwd(k, hT, *, comm=None, tq=4096, tk=2048, cq=512):
    G, T, HD = k.shape
    HE = HD + ONES_ROWS
    tq, tk = min(tq, T), min(tk, T)
    cq = min(cq, tq)
    nk, nt = T // tk, T // tq
    grid = (G, GQ * nt, nk)

    def body(*refs):
        (k_ref, qT_ref, v_ref, oT_ref, lse_ref, m_sc, acc_sc), comm_start, comm_finish = _host_comm(
            comm, refs, 3, 2, 2, grid)
        kv = pl.program_id(2)
        comm_start()
        v1T = jnp.concatenate([v_ref[...], jnp.ones((ONES_ROWS, tk), BF16)], axis=0)

        @pl.when(kv == 0)
        def _():
            m_sc[...] = jnp.full_like(m_sc, NEG)
            acc_sc[...] = jnp.zeros_like(acc_sc)

        def scores(c):
            return jnp.dot(k_ref[...], qT_ref[:, c * cq:(c + 1) * cq], preferred_element_type=F32)

        nc = tq // cq
        ahead = scores(0)
        for c in range(nc):
            cols = slice(c * cq, (c + 1) * cq)
            sT = ahead
            if c + 1 < nc:
                ahead = scores(c + 1)
            m_prev = m_sc[:, cols]
            m_new = jnp.maximum(m_prev, jnp.max(sT, axis=0, keepdims=True))
            pT = jnp.exp2(sT - m_new).astype(BF16)
            acc_sc[:, cols] = (jnp.exp2(m_prev - m_new) * acc_sc[:, cols]
                               + jnp.dot(v1T, pT, preferred_element_type=F32))
            m_sc[:, cols] = m_new

        @pl.when(kv == nk - 1)
        def _():
            l = acc_sc[HD:HD + 1, :]
            oT_ref[...] = acc_sc[0:HD, :] / l
            lse_ref[...] = m_sc[...] + jnp.log2(l)

        comm_finish()

    qtr = pl.BlockSpec((None, HD, tq), lambda g, i, j: (g * GQ + i // nt, 0, i % nt))
    qvec = pl.BlockSpec((None, 1, tq), lambda g, i, j: (g * GQ + i // nt, 0, i % nt))
    est = 6 * _nbytes((cq, tk), F32) + (8 << 20)
    hosted = comm is not None
    return pl.pallas_call(
        body, name="attn_a_fwd_comm" if hosted else "attn_a_fwd", grid=grid,
        out_shape=[jax.ShapeDtypeStruct((G * GQ, HD, T), F32), jax.ShapeDtypeStruct((G * GQ, 1, T), F32)]
        + (comm.out_shape if hosted else []),
        in_specs=[pl.BlockSpec((None, tk, HD), lambda g, i, j: (g, j, 0)), qtr,
                  pl.BlockSpec((None, HD, tk), lambda g, i, j: (A_V0 + g, 0, j))] + (comm.in_specs if hosted else []),
        out_specs=[qtr, qvec] + (comm.out_specs if hosted else []),
        scratch_shapes=[pltpu.VMEM((1, tq), F32), pltpu.VMEM((HE, tq), F32)] + (comm.scratch if hosted else []),
        compiler_params=_params(("arbitrary",) * 3 if hosted else ("parallel", "parallel", "arbitrary"), est),
    )(k, hT, hT, *(comm.parts if hosted else []))


def _attn_a_bwd(k, v, hT, doT, lse_row, delta_row, *, comm=None, tq=4096, tk=1024, cq=256):
    G, T, HD = k.shape
    tq, tk = min(tq, T), min(tk, T)
    cq = min(cq, tq)
    nqt = T // tq
    nq, nc = GQ * nqt, tq // cq
    nt = (((1,), (1,)), ((), ()))

    grid = (G, T // tk, nq)

    def body(*refs):
        (k_ref, v_ref, kT_ref, qT_ref, doT_ref, lse_ref, dl_ref, dkT_ref, dvT_ref, dqT_ref, dk_sc, dv_sc), \
            comm_start, comm_finish = _host_comm(comm, refs, 7, 3, 2, grid)
        j, i = pl.program_id(1), pl.program_id(2)
        comm_start()

        @pl.when((j == 0) & (i == 0))
        def _():
            dqT_ref[...] = jnp.zeros_like(dqT_ref)

        @pl.when(i == 0)
        def _():
            dk_sc[...] = jnp.zeros_like(dk_sc)
            dv_sc[...] = jnp.zeros_like(dv_sc)

        def scores(c):
            cols = slice(c * cq, (c + 1) * cq)
            return (jnp.dot(k_ref[...], qT_ref[:, cols], preferred_element_type=F32),
                    jnp.dot(v_ref[...], doT_ref[:, cols], preferred_element_type=F32))

        ahead = scores(0)
        dk_part = dv_part = None
        for c in range(nc):
            cols = slice(c * cq, (c + 1) * cq)
            sT, dpT = ahead
            if c + 1 < nc:
                ahead = scores(c + 1)
            pT = jnp.exp2(sT - lse_ref[:, cols])
            dsT = (pT * (dpT - dl_ref[:, cols])).astype(BF16)
            dv_c = lax.dot_general(doT_ref[:, cols], pT.astype(BF16), nt, preferred_element_type=F32)
            dk_c = lax.dot_general(qT_ref[:, cols], dsT, nt, preferred_element_type=F32)
            dv_part = dv_c if dv_part is None else dv_part + dv_c
            dk_part = dk_c if dk_part is None else dk_part + dk_c
            out_cols = pl.ds(pl.multiple_of((i % nqt) * tq + c * cq, cq), cq)
            dqT_ref[i // nqt, :, out_cols] += jnp.dot(kT_ref[...], dsT, preferred_element_type=F32)
        dk_sc[...] += dk_part
        dv_sc[...] += dv_part

        @pl.when(i == nq - 1)
        def _():
            dkT_ref[...] = dk_sc[...] * LN2
            dvT_ref[...] = dv_sc[...]

        comm_finish()

    krow = pl.BlockSpec((None, tk, HD), lambda g, j, i: (g, j, 0))
    ktr = pl.BlockSpec((None, HD, tk), lambda g, j, i: (g, 0, j))
    ktr_h = pl.BlockSpec((None, HD, tk), lambda g, j, i: (A_K0 + g, 0, j))
    qtr = pl.BlockSpec((None, HD, tq), lambda g, j, i: (g * GQ + i // nqt, 0, i % nqt))
    qvec = pl.BlockSpec((None, 1, tq), lambda g, j, i: (g * GQ + i // nqt, 0, i % nqt))
    whole = pl.BlockSpec((GQ, HD, T), lambda g, j, i: (g, 0, 0))
    est = 8 * _nbytes((cq, tk), F32) + 2 * _nbytes((GQ, HD, T), F32) + (8 << 20)
    hosted = comm is not None
    return pl.pallas_call(
        body, name="attn_a_bwd_comm" if hosted else "attn_a_bwd", grid=grid,
        out_shape=[jax.ShapeDtypeStruct((G, HD, T), F32), jax.ShapeDtypeStruct((G, HD, T), F32),
                   jax.ShapeDtypeStruct((G * GQ, HD, T), F32)] + (comm.out_shape if hosted else []),
        in_specs=[krow, krow, ktr_h, qtr, qtr, qvec, qvec] + (comm.in_specs if hosted else []),
        out_specs=[ktr, ktr, whole] + (comm.out_specs if hosted else []),
        scratch_shapes=[pltpu.VMEM((HD, tk), F32), pltpu.VMEM((HD, tk), F32)] + (comm.scratch if hosted else []),
        compiler_params=_params(("arbitrary", "arbitrary", "arbitrary"), est),
    )(k, v, hT, hT, doT, lse_row, delta_row, *(comm.parts if hosted else []))


WB = WINDOW
WK = 3 * WINDOW


QB_COL0 = (ROPE_W + KW) // (GQ * HEAD_DIM)
KB_COL = (ROPE_W + KW + QW) // KW
GW = GQ * HEAD_DIM


WSTEP = 2


def _win_in_specs(T):
    nb = T // WB
    assert nb % WSTEP == 0
    ns = nb // WSTEP
    q = [pl.BlockSpec((WSTEP * WB, GW), functools.partial(lambda n, g: (n, QB_COL0 + g), g=g)) for g in range(KV)]
    kv = [pl.BlockSpec((WB, KW), functools.partial(lambda n, o, c: (jnp.clip(WSTEP * n + o, 0, nb - 1), c), o=o, c=c))
          for c in (KB_COL, KB_COL + 1) for o in range(-1, WSTEP + 1)]
    bias = [pl.BlockSpec((None, KV, WK, GQ * WB), lambda n: (jnp.where(n == 0, 0, 1), 0, 0, 0)),
            pl.BlockSpec((None, KV, WK, GQ * WB), lambda n: (jnp.where(n == ns - 1, 2, 1), 0, 0, 0))]
    return ns, q + kv, bias


def _end_tables(biasT):
    key = lax.broadcasted_iota(jnp.int32, biasT.shape, 1)
    return jnp.stack([jnp.where(key < WB, NEG, biasT), biasT, jnp.where(key >= 2 * WB, NEG, biasT)])


def _heads_to_lanes(t):
    return jnp.concatenate([t[i * HEAD_DIM:(i + 1) * HEAD_DIM] for i in range(GQ)], axis=1)


def _lanes_to_heads(t):
    return jnp.concatenate([t[:, i * WB:(i + 1) * WB] for i in range(GQ)], axis=0)


def _attn_b_fwd(h, biasT, sink_rows):
    T = h.shape[0]
    ns, in_specs, bias_specs = _win_in_specs(T)
    nkv = WSTEP + 2

    def body(*refs):
        q_refs, k_refs, v_refs = refs[:KV], refs[KV:KV + nkv], refs[KV + nkv:KV + 2 * nkv]
        b_refs, sk_ref, o_ref, lse_ref = refs[KV + 2 * nkv:KV + 2 * nkv + WSTEP], *refs[KV + 2 * nkv + WSTEP:]
        ks, vs = [r[...] for r in k_refs], [r[...] for r in v_refs]
        outs = []
        for b, b_ref in enumerate(b_refs):
            rows = slice(b * WB, (b + 1) * WB)
            kwin = jnp.concatenate(ks[b:b + 3], axis=0)
            vT = jnp.concatenate(vs[b:b + 3], axis=0).T
            qT = [_heads_to_lanes((q[rows, :] * SCALE).T).astype(BF16) for q in q_refs]
            sT = [jnp.dot(kwin[:, g * HEAD_DIM:(g + 1) * HEAD_DIM].astype(BF16), qT[g], preferred_element_type=F32)
                  for g in range(KV)]
            oT = []
            for g in range(KV):
                s = sT[g] + b_ref[g]
                sk = sk_ref[g]
                m = jnp.maximum(jnp.max(s, axis=0, keepdims=True), sk)
                p = jnp.exp(s - m)
                den = jnp.sum(p, axis=0, keepdims=True) + jnp.exp(sk - m)
                o = jnp.dot(vT[g * HEAD_DIM:(g + 1) * HEAD_DIM].astype(BF16), p.astype(BF16),
                            preferred_element_type=F32) / den
                lse_ref[b, g] = m + jnp.log(den)
                oT.append(_lanes_to_heads(o))
            outs.append(jnp.concatenate(oT, axis=0).T)
        o_ref[...] = jnp.concatenate(outs, axis=0)

    whole = lambda *shape: pl.BlockSpec(shape, lambda n: (0,) * len(shape))
    return pl.pallas_call(
        body, name="attn_b_fwd", grid=(ns,),
        out_shape=(jax.ShapeDtypeStruct((T, QW), F32), jax.ShapeDtypeStruct((ns * WSTEP, KV, 1, GQ * WB), F32)),
        in_specs=in_specs + bias_specs + [whole(KV, 1, GQ * WB)],
        out_specs=(pl.BlockSpec((WSTEP * WB, QW), lambda n: (n, 0)),
                   pl.BlockSpec((WSTEP, KV, 1, GQ * WB), lambda n: (n, 0, 0, 0))),
        compiler_params=_params(("parallel",), 32 << 20),
    )(*([h] * (KV + 2 * nkv)), *([biasT] * WSTEP), sink_rows)


def _attn_b_bwd(h, do, o, lse, biasT, sink_rows):
    T = h.shape[0]
    ns, in_specs, bias_specs = _win_in_specs(T)
    nkv = WSTEP + 2
    Tp = T + 2 * WB
    nt = (((1,), (1,)), ((), ()))

    def body(*refs):
        q_refs, k_refs, v_refs = refs[:KV], refs[KV:KV + nkv], refs[KV + nkv:KV + 2 * nkv]
        at = KV + 2 * nkv
        do_ref, o_ref, lse_ref = refs[at:at + 3]
        b_refs, sk_ref = refs[at + 3:at + 3 + WSTEP], refs[at + 3 + WSTEP]
        dq_ref, dk_ref, dv_ref, db_ref, dsk_ref = refs[at + 4 + WSTEP:]
        n = pl.program_id(0)

        @pl.when(n == 0)
        def _():
            dk_ref[...] = jnp.zeros_like(dk_ref)
            dv_ref[...] = jnp.zeros_like(dv_ref)
            db_ref[...] = jnp.zeros_like(db_ref)
            dsk_ref[...] = jnp.zeros_like(dsk_ref)

        ks, vs = [r[...] for r in k_refs], [r[...] for r in v_refs]
        for b, b_ref in enumerate(b_refs):
            rows = slice(b * WB, (b + 1) * WB)
            kwin = jnp.concatenate(ks[b:b + 3], axis=0)
            vwin = jnp.concatenate(vs[b:b + 3], axis=0)
            kT = kwin.T
            doT_all, oT_all = do_ref[rows, :].T, o_ref[rows, :].T
            qT, doT, delta, sT, dpT = [], [], [], [], []
            for g, q in enumerate(q_refs):
                hd = slice(g * HEAD_DIM, (g + 1) * HEAD_DIM)
                qT.append(_heads_to_lanes((q[rows, :] * SCALE).T).astype(BF16))
                d = _heads_to_lanes(doT_all[g * GW:(g + 1) * GW])
                delta.append(jnp.sum(d * _heads_to_lanes(oT_all[g * GW:(g + 1) * GW]), axis=0, keepdims=True))
                doT.append(d.astype(BF16))
                sT.append(jnp.dot(kwin[:, hd].astype(BF16), qT[g], preferred_element_type=F32))
                dpT.append(jnp.dot(vwin[:, hd].astype(BF16), doT[g], preferred_element_type=F32))
            dq, dk, dv = [], [], []
            for g in range(KV):
                lse_g = lse_ref[b, g]
                p = jnp.exp(sT[g] + b_ref[g] - lse_g)
                ds = p * (dpT[g] - delta[g])
                db_ref[g] += ds
                dsk_ref[g] -= jnp.exp(sk_ref[g] - lse_g) * delta[g]
                dsb = ds.astype(BF16)
                dqT = jnp.dot(kT[g * HEAD_DIM:(g + 1) * HEAD_DIM].astype(BF16), dsb, preferred_element_type=F32)
                dq.append(_lanes_to_heads(dqT))
                dk.append(lax.dot_general(dsb, qT[g], nt, preferred_element_type=F32))
                dv.append(lax.dot_general(p.astype(BF16), doT[g], nt, preferred_element_type=F32))
            dq_ref[rows, :] = (jnp.concatenate(dq, axis=0).T * SCALE).astype(BF16)
            win = pl.ds(pl.multiple_of((WSTEP * n + b) * WB, WB), WK)
            dk_ref[win, :] += jnp.concatenate(dk, axis=1)
            dv_ref[win, :] += jnp.concatenate(dv, axis=1)

    whole = lambda *shape: pl.BlockSpec(shape, lambda n: (0,) * len(shape))
    tok = pl.BlockSpec((WSTEP * WB, QW), lambda n: (n, 0))
    return pl.pallas_call(
        body, name="attn_b_bwd", grid=(ns,),
        out_shape=(jax.ShapeDtypeStruct((T, QW), BF16),
                   jax.ShapeDtypeStruct((Tp, KW), F32), jax.ShapeDtypeStruct((Tp, KW), F32),
                   jax.ShapeDtypeStruct((KV, WK, GQ * WB), F32), jax.ShapeDtypeStruct((KV, 1, GQ * WB), F32)),
        in_specs=in_specs + [tok, tok, pl.BlockSpec((WSTEP, KV, 1, GQ * WB), lambda n: (n, 0, 0, 0))]
        + bias_specs + [whole(KV, 1, GQ * WB)],
        out_specs=(tok, whole(Tp, KW), whole(Tp, KW), whole(KV, WK, GQ * WB), whole(KV, 1, GQ * WB)),
        compiler_params=_params(("arbitrary",), 48 << 20),
    )(*([h] * (KV + 2 * nkv)), do, o, lse, *([biasT] * WSTEP), sink_rows)


def _bias_table(rel_bias_t, bucket):
    nh, n = rel_bias_t.shape[0], bucket.shape[1]

    def body(rb_ref, bk_ref, o_ref):
        bk = bk_ref[...]
        out = jnp.full((nh, n), NEG, F32)
        for b in range(N_BUCKETS):
            out = jnp.where(bk == b, rb_ref[:, b:b + 1], out)
        o_ref[...] = out

    return pl.pallas_call(
        body, name="bias_table", out_shape=jax.ShapeDtypeStruct((nh, n), F32),
        compiler_params=pltpu.CompilerParams(vmem_limit_bytes=32 << 20),
    )(rel_bias_t, bucket)


def _bias_sink_grads(db_list, dsk_list, bucket):
    L = len(db_list)

    def body(*refs):
        db_refs, dsk_refs, bk_ref = refs[:L], refs[L:2 * L], refs[2 * L]
        drb_ref, dsink_ref = refs[2 * L + 1], refs[2 * L + 2]
        tot = db_refs[0][...]
        for r in db_refs[1:]:
            tot = tot + r[...]
        bk = bk_ref[...]
        lane = lax.broadcasted_iota(jnp.int32, (2 * GQ, N_BUCKETS), 1)
        out = jnp.zeros((2 * GQ, N_BUCKETS), F32)
        for b in range(N_BUCKETS):
            sb = jnp.sum(jnp.where(bk == b, tot, 0.0), axis=1, keepdims=True)
            out = jnp.where(lane == b, sb, out)
        drb_ref[...] = out
        for l in range(L):
            dsink_ref[l] = jnp.sum(dsk_refs[l][...], axis=1, keepdims=True)

    return pl.pallas_call(
        body, name="bias_sink_grads",
        out_shape=(jax.ShapeDtypeStruct((2 * GQ, N_BUCKETS), F32), jax.ShapeDtypeStruct((L, 2 * GQ, 1), F32)),
        compiler_params=pltpu.CompilerParams(vmem_limit_bytes=32 << 20),
    )(*db_list, *dsk_list, bucket)


def _outnorm_fwd(oaT, ob, ga, gb, *, tm=512):
    T = ob.shape[0]
    tm = min(tm, T)

    def body(oaT_ref, ob_ref, ga_ref, gb_ref, y_ref):
        for j, (o, g_ref) in enumerate(((oaT_ref[...].T, ga_ref), (ob_ref[...], gb_ref))):
            r = lax.rsqrt(jnp.mean(o * o, axis=1, keepdims=True) + RMS_EPS)
            y_ref[:, j * QW:(j + 1) * QW] = (o * r * g_ref[...]).astype(BF16)

    half = pl.BlockSpec((tm, QW), lambda i: (i, 0))
    halfT = pl.BlockSpec((QW, tm), lambda i: (0, i))
    vec = pl.BlockSpec((1, QW), lambda i: (0, 0))
    return pl.pallas_call(
        body, name="outnorm_fwd", grid=(T // tm,),
        out_shape=jax.ShapeDtypeStruct((T, 2 * QW), BF16),
        in_specs=[halfT, half, vec, vec], out_specs=pl.BlockSpec((tm, 2 * QW), lambda i: (i, 0)),
        compiler_params=_params(("parallel",), 16 << 20),
    )(oaT, ob, ga, gb)


def _outnorm_bwd(dy, oaT, ob, ga, gb, *, tm=512):
    T = ob.shape[0]
    tm = min(tm, T)
    nh = QW // HEAD_DIM

    def body(dy_ref, oaT_ref, ob_ref, ga_ref, gb_ref, doaT_ref, dl_ref, dob_ref, dg_ref):
        i = pl.program_id(0)

        @pl.when(i == 0)
        def _():
            dg_ref[...] = jnp.zeros_like(dg_ref)

        oaT = oaT_ref[...]
        for j, (o, g_ref) in enumerate(((oaT.T, ga_ref), (ob_ref[...], gb_ref))):
            d = dy_ref[:, j * QW:(j + 1) * QW]
            r = lax.rsqrt(jnp.mean(o * o, axis=1, keepdims=True) + RMS_EPS)
            orr = o * r
            dg_ref[j:j + 1, :] += jnp.sum(d * orr, axis=0, keepdims=True)
            dgv = d * g_ref[...]
            do = r * dgv - orr * (r * r) * jnp.mean(dgv * o, axis=1, keepdims=True)
            if j == 0:
                doT = do.T
                doaT_ref[...] = doT.astype(BF16)
                prod = doT * oaT
                dl_ref[...] = jnp.concatenate(
                    [jnp.sum(prod[a * HEAD_DIM:(a + 1) * HEAD_DIM], axis=0, keepdims=True) for a in range(nh)], axis=0)
            else:
                dob_ref[...] = do

    half = pl.BlockSpec((tm, QW), lambda i: (i, 0))
    halfT = pl.BlockSpec((QW, tm), lambda i: (0, i))
    vec = pl.BlockSpec((1, QW), lambda i: (0, 0))
    return pl.pallas_call(
        body, name="outnorm_bwd", grid=(T // tm,),
        out_shape=(jax.ShapeDtypeStruct((QW, T), BF16), jax.ShapeDtypeStruct((nh, T), F32),
                   jax.ShapeDtypeStruct((T, QW), F32), jax.ShapeDtypeStruct((8, QW), F32)),
        in_specs=[pl.BlockSpec((tm, 2 * QW), lambda i: (i, 0)), halfT, half, vec, vec],
        out_specs=(halfT, pl.BlockSpec((nh, tm), lambda i: (0, i)), half, pl.BlockSpec((8, QW), lambda i: (0, 0))),
        compiler_params=_params(("arbitrary",), 32 << 20),
    )(dy, oaT, ob, ga, gb)


GELU_C = math.sqrt(2.0 / math.pi)
GELU_A = 0.044715
HALO = 16


def _gelu_parts(x):
    t = jnp.tanh(GELU_C * (x + GELU_A * (x * x * x)))
    return 0.5 * (1.0 + t), t


def _halo_specs(tm, tn, T):
    nh = tm // HALO
    last = T // HALO - 1
    cur = pl.BlockSpec((tm, tn), lambda j, i: (i, j))
    prev = pl.BlockSpec((HALO, tn), lambda j, i: (jnp.maximum(i * nh - 1, 0), j))
    nxt = pl.BlockSpec((HALO, tn), lambda j, i: (jnp.minimum((i + 1) * nh, last), j))
    return cur, prev, nxt


def _conv_glu_fwd(g, u, conv_w, conv_b, *, tm=256, tn=1408):
    T, F = g.shape
    tm, tn = min(tm, T), min(tn, F)
    cur, prev, nxt = _halo_specs(tm, tn, T)

    def body(g_ref, gp_ref, gn_ref, u_ref, w_ref, b_ref, a_ref):
        i = pl.program_id(1)
        gv = g_ref[...]
        row = lax.broadcasted_iota(jnp.int32, (tm, tn), 0)
        before = jnp.where(i * tm > 0, gp_ref[HALO - 1:HALO, :], 0.0)
        after = jnp.where((i + 1) * tm < T, gn_ref[0:1, :], 0.0)
        gm1 = jnp.where(row == 0, before, pltpu.roll(gv, 1, 0))
        gp1 = jnp.where(row == tm - 1, after, pltpu.roll(gv, tm - 1, 0))
        gc = ((b_ref[...] + gm1 * w_ref[0:1, :]) + gv * w_ref[1:2, :]) + gp1 * w_ref[2:3, :]
        cdf, _ = _gelu_parts(gc)
        a_ref[...] = (gc * cdf * u_ref[...].astype(F32)).astype(BF16)

    wspec = pl.BlockSpec((8, tn), lambda j, i: (0, j))
    est = 2 * (3 * _nbytes((tm, tn), F32)) + 8 * _nbytes((tm, tn), F32)
    return pl.pallas_call(
        body, name="conv_glu_fwd", grid=(F // tn, T // tm),
        out_shape=jax.ShapeDtypeStruct((T, F), BF16),
        in_specs=[cur, prev, nxt, cur, wspec, pl.BlockSpec((1, tn), lambda j, i: (0, j))],
        out_specs=cur,
        compiler_params=_params(("parallel", "parallel"), est),
    )(g, g, g, u, conv_w, conv_b)


def _conv_glu_bwd(dact, g, u, conv_w, conv_b, *, tm=256, tn=1408):
    T, F = g.shape
    tm, tn = min(tm, T), min(tn, F)
    cur, prev, nxt = _halo_specs(tm, tn, T)
    te = tm + 2 * HALO

    def body(d_ref, dp_ref, dn_ref, g_ref, gp_ref, gn_ref, u_ref, up_ref, un_ref, w_ref, b_ref,
             dg_ref, du_ref, dc_ref):
        i = pl.program_id(1)

        @pl.when(i == 0)
        def _():
            dc_ref[...] = jnp.zeros_like(dc_ref)

        grow = i * tm - HALO + lax.broadcasted_iota(jnp.int32, (te, tn), 0)
        valid = (grow >= 0) & (grow < T)
        ge = jnp.where(valid, jnp.concatenate([gp_ref[...], g_ref[...], gn_ref[...]], axis=0), 0.0)
        ue = jnp.concatenate([up_ref[...], u_ref[...], un_ref[...]], axis=0).astype(F32)
        de = jnp.concatenate([dp_ref[...], d_ref[...], dn_ref[...]], axis=0).astype(F32)
        w0, w1, w2 = w_ref[0:1, :], w_ref[1:2, :], w_ref[2:3, :]
        gm1 = pltpu.roll(ge, 1, 0)
        gp1 = pltpu.roll(ge, te - 1, 0)
        gc = ((b_ref[...] + gm1 * w0) + ge * w1) + gp1 * w2
        cdf, t = _gelu_parts(gc)
        dgelu = cdf + 0.5 * gc * (1.0 - t * t) * (GELU_C * (1.0 + 3.0 * GELU_A * (gc * gc)))
        dgc = jnp.where(valid, de * ue * dgelu, 0.0)
        dge = w0 * pltpu.roll(dgc, te - 1, 0) + w1 * dgc + w2 * pltpu.roll(dgc, 1, 0)
        mid = slice(HALO, HALO + tm)
        dg_ref[...] = dge[mid].astype(BF16)
        du_ref[...] = (de[mid] * (gc[mid] * cdf[mid])).astype(BF16)
        dgm = dgc[mid]
        dc_ref[0:1, :] += jnp.sum(dgm * gm1[mid], axis=0, keepdims=True)
        dc_ref[1:2, :] += jnp.sum(dgm * ge[mid], axis=0, keepdims=True)
        dc_ref[2:3, :] += jnp.sum(dgm * gp1[mid], axis=0, keepdims=True)
        dc_ref[3:4, :] += jnp.sum(dgm, axis=0, keepdims=True)

    wspec = pl.BlockSpec((8, tn), lambda j, i: (0, j))
    est = 2 * (3 * _nbytes((tm, tn), F32) + 2 * _nbytes((tm, tn), BF16)) + 16 * _nbytes((te, tn), F32)
    return pl.pallas_call(
        body, name="conv_glu_bwd", grid=(F // tn, T // tm),
        out_shape=(jax.ShapeDtypeStruct((T, F), BF16), jax.ShapeDtypeStruct((T, F), BF16),
                   jax.ShapeDtypeStruct((8, F), F32)),
        in_specs=[cur, prev, nxt, cur, prev, nxt, cur, prev, nxt, wspec, pl.BlockSpec((1, tn), lambda j, i: (0, j))],
        out_specs=(cur, cur, wspec),
        compiler_params=_params(("parallel", "arbitrary"), est),
    )(dact, dact, dact, g, g, g, u, u, u, conv_w, conv_b)


def _adamw_math(w, g, m, v):
    m = ADAM_B1 * m + (1.0 - ADAM_B1) * g
    v = ADAM_B2 * v + (1.0 - ADAM_B2) * (g * g)
    m_hat = m / (1.0 - ADAM_B1 ** ADAM_STEP)
    v_hat = v / (1.0 - ADAM_B2 ** ADAM_STEP)
    delta = -ADAM_LR * (m_hat / (jnp.sqrt(v_hat) + ADAM_EPS) + ADAM_WD * w)
    return delta, m, v


def _adamw(w, m, v, gparts, *, name, tr):
    R, C = w.shape
    tr = min(tr, R)
    assert R % tr == 0

    def body(w_ref, m_ref, v_ref, gp_ref, g_ref, d_ref, nm_ref, nv_ref):
        g = gp_ref[0].astype(F32)
        for j in range(1, N_DEV):
            g = g + gp_ref[j].astype(F32)
        delta, nm, nv = _adamw_math(w_ref[...], g, m_ref[...], v_ref[...])
        g_ref[...] = g
        d_ref[...] = delta
        nm_ref[...] = nm
        nv_ref[...] = nv

    blk = pl.BlockSpec((tr, C), lambda i: (i, 0))
    out = jax.ShapeDtypeStruct((R, C), F32)
    return pl.pallas_call(
        body, name=name, grid=(R // tr,), out_shape=(out, out, out, out),
        in_specs=[blk, blk, blk, pl.BlockSpec((N_DEV, tr, C), lambda i: (0, i, 0))],
        out_specs=(blk, blk, blk, blk),
        compiler_params=_params(("parallel",), 24 << 20),
    )(w, m, v, gparts)


def _rope_tables(T):
    rows_n = T // GRID_W
    row = jnp.repeat(jnp.arange(rows_n, dtype=F32), GRID_W)
    col = jnp.tile(jnp.arange(GRID_W, dtype=F32), rows_n)
    half = HEAD_DIM // 2
    inv_freq = ROPE_THETA ** (-jnp.arange(0, half, 2, dtype=F32) / half)
    ang = jnp.concatenate([row[:, None] * inv_freq, col[:, None] * inv_freq], axis=-1)
    cos, sin = jnp.cos(ang), jnp.sin(ang)
    cos64 = jnp.repeat(cos, 2, axis=-1)
    sin64 = jnp.stack([-sin, sin], axis=-1).reshape(T, HEAD_DIM)
    return jnp.tile(cos64, (1, 2)), jnp.tile(sin64, (1, 2))


def _t5_bucket(rel):
    half = N_BUCKETS // 2
    max_exact = half // 2
    bucket = jnp.where(rel > 0, half, 0)
    rp = jnp.abs(rel)
    rpf = jnp.maximum(rp, 1).astype(F32)
    large = max_exact + (jnp.log(rpf / max_exact) / math.log(MAX_DISTANCE / max_exact)
                         * (half - max_exact)).astype(jnp.int32)
    large = jnp.minimum(large, half - 1)
    return bucket + jnp.where(rp < max_exact, rp, large)


def _window_buckets():
    qpos = jnp.arange(WB, dtype=jnp.int32)
    kpos = jnp.arange(WK, dtype=jnp.int32) - WB
    rel = kpos[None, :] - qpos[:, None]
    return jnp.where(jnp.abs(rel) <= WINDOW, _t5_bucket(rel), -1)


def _heads_first(a, nh):
    T = a.shape[0]
    return a.reshape(T, nh, HEAD_DIM).transpose(1, 0, 2)


def _row(v):
    return v.reshape(1, -1)


def _rows8(rows, width):
    a = jnp.stack(list(rows), axis=0)
    return jnp.pad(a, ((0, 8 - a.shape[0]), (0, 0)))


def _layer_fwd(l, xin, W, tabs, comm=None, on_comm=None):
    xhat, xg, xb, x16 = xin
    T = xhat.shape[0]
    cos2, sin2, biasT = tabs
    h = _mm([x16], [W["w_in"][l]], name="mm_in", out_dtype=F32, tm=MM_ROWS, tn=IN_COLS, tk=D_MODEL)
    gains = _rows8([jnp.tile(W["q_norm"][l], 2), jnp.tile(W["k_norm"][l], 2)], LANES)
    hT, kv_nat = _qk_rope_fwd(h, gains, cos2, sin2, name="qk_rope_fwd")
    hT = hT.reshape(A_HEADS, HEAD_DIM, T)
    ka, va = _heads_first(kv_nat[:, :KW], KV), _heads_first(kv_nat[:, KW:], KV)
    res = _attn_a_fwd(ka, hT, comm=comm)
    oaT, lse_a = res[0].reshape(QW, T), res[1]
    if comm is not None:
        on_comm(res[2:])
    sink_rows = jnp.repeat(W["sink"][l], WB).reshape(KV, 1, GQ * WB)
    ob_t, lse_b = _attn_b_fwd(h, biasT, sink_rows)
    ga, gb = _row(W["out_norm_a"][l]), _row(W["out_norm_b"][l])
    ycat = _outnorm_fwd(oaT, ob_t, ga, gb)
    g1, b1 = _row(W["ln1_g"][l]), _row(W["ln1_b"][l])
    x1hat, rstd1, x1_16 = _mm_res_ln(ycat, W["w_out"][l], xhat, xg, xb, g1, b1, name="mm_out_ln", tm=512)
    gate = _mm([x1_16], [W["w_gate"][l]], name="mm_gate", out_dtype=F32, tm=MM_ROWS, tn=D_FF // 2, tk=D_MODEL)
    up = _mm([x1_16], [W["w_up"][l]], name="mm_up", out_dtype=BF16, tm=MM_ROWS, tn=D_FF // 2, tk=D_MODEL)
    cw = jnp.pad(W["conv_w"][l], ((0, 5), (0, 0)))
    cb = _row(W["conv_b"][l])
    act = _conv_glu_fwd(gate, up, cw, cb)
    g2, b2 = _row(W["ln2_g"][l]), _row(W["ln2_b"][l])
    x2hat, rstd2, x2_16 = _mm_res_ln(act, W["w_down"][l], x1hat, g1, b1, g2, b2, name="mm_down_ln", tm=256)
    saved = dict(x16=x16, h=h, gains=gains, hT=hT, ka=ka, va=va, oaT=oaT, lse_a=lse_a,
                 lse_b=lse_b, sink_rows=sink_rows, ob_t=ob_t,
                 ga=ga, gb=gb, ycat=ycat, x1hat=x1hat, rstd1=rstd1, x1_16=x1_16, g1=g1, b1=b1, gate=gate, up=up,
                 cw=cw, cb=cb, act=act, x2hat=x2hat, rstd2=rstd2, g2=g2, b2=b2)
    return (x2hat, g2, b2, x2_16), saved


def _layer_bwd(l, S, W, tabs, dz2, dz2_16, stats2, scatter=None):
    cos2, sin2, biasT = tabs
    T = dz2.shape[0]
    G = {}
    G["ln2_g"], G["ln2_b"] = stats2[0], stats2[1]
    G["w_down"] = _mm([S["act"]], [dz2_16], name="dw_down", out_dtype=BF16, trans_a=True, tm=D_FF // 2, tn=D_MODEL, tk=DW_TOKENS)
    dact = _mm([dz2_16], [W["w_down"][l]], name="mm_dact", out_dtype=BF16, trans_b=True, tm=MM_ROWS, tn=D_FF // 2,
               tk=D_MODEL)
    dg, du, dconv = _conv_glu_bwd(dact, S["gate"], S["up"], S["cw"], S["cb"])
    G["conv_w"], G["conv_b"] = dconv[0:3], dconv[3]
    G["w_gate"] = _mm([S["x1_16"]], [dg], name="dw_gate", out_dtype=BF16, trans_a=True, tm=D_MODEL, tn=D_FF // 2, tk=DW_TOKENS)
    G["w_up"] = _mm([S["x1_16"]], [du], name="dw_up", out_dtype=BF16, trans_a=True, tm=D_MODEL, tn=D_FF // 2, tk=DW_TOKENS)
    dx1 = _mm([dg, du], [W["w_gate"][l], W["w_up"][l]], name="mm_dx1", out_dtype=F32, trans_b=True, tm=MM_ROWS_WIDE,
              tn=D_MODEL, tk=D_FF, add=dz2, add_scale=ALPHA)
    dz1, dz1_16, stats1 = _ln_bwd(S["x1hat"], S["rstd1"], S["g1"], S["b1"], name="ln1_bwd", dx=dx1)
    G["ln1_g"], G["ln1_b"] = stats1[0], stats1[1]
    G["w_out"] = _mm([S["ycat"]], [dz1_16], name="dw_out", out_dtype=BF16, trans_a=True, tm=D_MODEL, tn=D_MODEL, tk=DW_TOKENS)
    dycat = _mm([dz1_16], [W["w_out"][l]], name="mm_dycat", out_dtype=F32, trans_b=True, tm=MM_ROWS, tn=D_MODEL,
                tk=D_MODEL)
    doaT, delta, dob_t, dgn = _outnorm_bwd(dycat, S["oaT"], S["ob_t"], S["ga"], S["gb"])
    G["out_norm_a"], G["out_norm_b"] = dgn[0], dgn[1]
    res = _attn_a_bwd(S["ka"], S["va"], S["hT"], doaT.reshape(KV * GQ, HEAD_DIM, T), S["lse_a"],
                      delta.reshape(KV * GQ, 1, T), comm=scatter(G) if scatter is not None else None)
    dkaT, dvaT, dqaT = res[:3]
    dh_rope, dgain = _qk_rope_bwd(S["h"], dqaT.reshape(QW, T), dkaT.reshape(KW, T), S["gains"], cos2, sin2,
                                  name="qk_rope_bwd")
    G["q_norm"], G["k_norm"] = dgain[0, :HEAD_DIM], dgain[1, :HEAD_DIM]
    dqb_t, dkb, dvb, dbiasT, dsk = _attn_b_bwd(S["h"], dob_t, S["ob_t"], S["lse_b"], biasT, S["sink_rows"])
    dh = jnp.concatenate([
        dh_rope, dvaT.transpose(2, 0, 1).reshape(T, KW).astype(BF16), dqb_t,
        dkb[WB:WB + T].astype(BF16), dvb[WB:WB + T].astype(BF16)], axis=1)
    dbias = dbiasT.reshape(KV, WK, GQ, WB).transpose(0, 2, 1, 3)
    G["w_in"] = _mm([S["x16"]], [dh], name="dw_in", out_dtype=BF16, trans_a=True, tm=D_MODEL, tn=IN_COLS, tk=DW_TOKENS)
    dxin = _mm([dh], [W["w_in"][l]], name="mm_dxin", out_dtype=F32, trans_b=True, tm=MM_ROWS, tn=D_MODEL, tk=IN_COLS,
               add=dz1, add_scale=ALPHA)
    return dxin, G, dbias.reshape(KV * GQ, WK * WB), dsk.reshape(KV * GQ, WB), res[3:]


BIG = ("w_in", "w_out", "w_gate", "w_up", "w_down")
COL_SHARDED = ("w_in", "w_gate", "w_up")


def _unshard(name, blocks):
    _, r, c = blocks.shape
    if name in COL_SHARDED:
        return blocks.transpose(1, 0, 2).reshape(r, N_DEV * c)
    return blocks.reshape(N_DEV * r, c)


def _to_owner_blocks(name, full, shard_shape):
    _, r, c = shard_shape
    if name in COL_SHARDED:
        return full.reshape(r, N_DEV, c).transpose(1, 0, 2)
    return full.reshape(N_DEV, r, c)


def _pack_small(vals, tail):
    flat = jnp.concatenate([vals[n].reshape(-1).astype(F32) for n in SMALL_NAMES] + [tail])
    pad = (-flat.shape[0]) % (8 * LANES)
    return jnp.pad(flat, (0, pad)).reshape(-1, LANES)


def _unpack_small(packed, shapes):
    flat = packed.reshape(-1)
    out, off = {}, 0
    for n in SMALL_NAMES:
        size = math.prod(shapes[n])
        out[n] = flat[off:off + size].reshape(shapes[n])
        off += size
    return out, flat[off]


def kernel(x, rel_bias, w_in, q_norm, k_norm, sink, out_norm_a, out_norm_b, w_out, ln1_g, ln1_b, w_gate, w_up, conv_w, conv_b, w_down, ln2_g, ln2_b, loss_target, m_rel_bias, m_w_in, m_q_norm, m_k_norm, m_sink, m_out_norm_a, m_out_norm_b, m_w_out, m_ln1_g, m_ln1_b, m_w_gate, m_w_up, m_conv_w, m_conv_b, m_w_down, m_ln2_g, m_ln2_b, v_rel_bias, v_w_in, v_q_norm, v_k_norm, v_sink, v_out_norm_a, v_out_norm_b, v_w_out, v_ln1_g, v_ln1_b, v_w_gate, v_w_up, v_conv_w, v_conv_b, v_w_down, v_ln2_g, v_ln2_b):
    P = dict(rel_bias=rel_bias, w_in=w_in, q_norm=q_norm, k_norm=k_norm, sink=sink, out_norm_a=out_norm_a,
             out_norm_b=out_norm_b, w_out=w_out, ln1_g=ln1_g, ln1_b=ln1_b, w_gate=w_gate, w_up=w_up, conv_w=conv_w,
             conv_b=conv_b, w_down=w_down, ln2_g=ln2_g, ln2_b=ln2_b)
    M = dict(rel_bias=m_rel_bias, w_in=m_w_in, q_norm=m_q_norm, k_norm=m_k_norm, sink=m_sink, out_norm_a=m_out_norm_a,
             out_norm_b=m_out_norm_b, w_out=m_w_out, ln1_g=m_ln1_g, ln1_b=m_ln1_b, w_gate=m_w_gate, w_up=m_w_up,
             conv_w=m_conv_w, conv_b=m_conv_b, w_down=m_w_down, ln2_g=m_ln2_g, ln2_b=m_ln2_b)
    V = dict(rel_bias=v_rel_bias, w_in=v_w_in, q_norm=v_q_norm, k_norm=v_k_norm, sink=v_sink, out_norm_a=v_out_norm_a,
             out_norm_b=v_out_norm_b, w_out=v_w_out, ln1_g=v_ln1_g, ln1_b=v_ln1_b, w_gate=v_w_gate, w_up=v_w_up,
             conv_w=v_conv_w, conv_b=v_conv_b, w_down=v_w_down, ln2_g=v_ln2_g, ln2_b=v_ln2_b)
    names = list(P)
    T = x.shape[1]
    me = 4 * lax.axis_index("x") + 2 * lax.axis_index("y") + lax.axis_index("c")

    L, taps, fc = conv_w.shape
    W = {n: ([None] * DEPTH if n in BIG else P[n]) for n in names}

    def wire(n, l):
        return P[n][l].astype(BF16)

    def take(n, l, gathered):
        W[n][l] = _unshard(n, gathered)

    take("w_in", 0, _exchange([wire("w_in", 0)], [True], name="gather_w_in0")[0])
    later = [(n, l) for l in range(DEPTH) for n in BIG if (n, l) != ("w_in", 0)]
    cw_shard = conv_w.reshape(-1)
    cw_wire = jnp.pad(cw_shard, (0, (-cw_shard.shape[0]) % LANES)).reshape(-1, LANES)
    gather_rest = _Comm([wire(n, l) for n, l in later] + [cw_wire], [True] * (len(later) + 1))

    def on_gathered(outs):
        for (n, l), g in zip(later, outs):
            take(n, l, g)
        cw_all = outs[-1].reshape(N_DEV, -1)[:, :cw_shard.shape[0]].reshape(N_DEV, L, taps, fc)
        W["conv_w"] = cw_all.transpose(1, 2, 0, 3).reshape(L, taps, N_DEV * fc)

    cos2, sin2 = _rope_tables(T)
    bucket = _window_buckets()
    bias = _bias_table(rel_bias.T, bucket.reshape(1, WB * WK))
    biasT = bias.reshape(KV, GQ, WB, WK).transpose(0, 3, 1, 2).reshape(KV, WK, GQ * WB)
    biasT = _end_tables(biasT)
    tabs = (cos2, sin2, biasT)

    ones, zeros = jnp.ones((1, D_MODEL), F32), jnp.zeros((1, D_MODEL), F32)
    cur = (x[0], ones, zeros, x[0].astype(BF16))
    saved = []
    for l in range(DEPTH):
        cur, S = _layer_fwd(l, cur, W, tabs, comm=gather_rest if l == 0 else None, on_comm=on_gathered)
        saved.append(S)

    def owner_blocks(n, l):
        return _to_owner_blocks(n, grads[l][n], P[n].shape)

    early = ([(n, l) for l in range(1, DEPTH) for n in BIG] + [(n, 0) for n in BIG if n != "w_in"])

    def scatter_early(g0):
        grads[0] = g0
        return _Comm([owner_blocks(n, l) for n, l in early], [False] * len(early))

    grads = [None] * DEPTH
    dbs, dsks = [None] * DEPTH, [None] * DEPTH
    S = saved[-1]
    dz, dz16, stats = _ln_bwd(S["x2hat"], S["rstd2"], S["g2"], S["b2"], name="loss_ln2_bwd", target=loss_target[0])
    loss_part = stats[2, 0:1]
    recv = {}
    for l in reversed(range(DEPTH)):
        S = saved[l]
        dxin, grads[l], dbs[l], dsks[l], got = _layer_bwd(l, S, W, tabs, dz, dz16, stats,
                                                         scatter=scatter_early if l == 0 else None)
        if l == 0:
            recv.update(zip(early, got))
        if l > 0:
            Sp = saved[l - 1]
            dz, dz16, stats = _ln_bwd(Sp["x2hat"], Sp["rstd2"], Sp["g2"], Sp["b2"], name="ln2_bwd", dx=dxin)
    grad_x = dxin[None]

    drb, dsink = _bias_sink_grads(dbs, dsks, bucket.T.reshape(1, WK * WB))
    small_g = {n: jnp.stack([grads[l][n] for l in range(DEPTH)]) for n in SMALL_NAMES if n not in ("rel_bias", "sink")}
    small_g["rel_bias"] = drb.T
    small_g["sink"] = dsink.reshape(DEPTH, KV * GQ)
    recv[("w_in", 0)], small_recv = _exchange([owner_blocks("w_in", 0), _pack_small(small_g, loss_part)], [False, True],
                                              name="scatter_w_in0_gather_small")

    out_g, out_d, out_m, out_v = {}, {}, {}, {}
    for n in BIG:
        shp = P[n].shape
        gparts = jnp.concatenate([recv[(n, l)] for l in range(DEPTH)], axis=1)
        rows, cols = shp[0] * shp[1], shp[2]
        res = _adamw(P[n].reshape(rows, cols), M[n].reshape(rows, cols), V[n].reshape(rows, cols), gparts,
                     name="adamw_" + n, tr=math.gcd(rows, 256))
        out_g[n], out_d[n], out_m[n], out_v[n] = (r.reshape(shp) for r in res)
    full_shapes = {n: W[n].shape for n in SMALL_NAMES}

    def small_state(D):
        vals = {n: D[n] for n in SMALL_NAMES if n != "conv_w"}
        cw = jnp.zeros((L, taps, N_DEV, fc), F32)
        cw = lax.dynamic_update_slice(cw, D["conv_w"].reshape(L, taps, 1, fc), (0, 0, me, 0))
        vals["conv_w"] = cw.reshape(L, taps, N_DEV * fc)
        return _pack_small(vals, jnp.zeros((1,), F32))

    sw, sm, sv = small_state(P), small_state(M), small_state(V)
    res = _adamw(sw, sm, sv, small_recv, name="adamw_small", tr=sw.shape[0])
    loss = _unpack_small(res[0], full_shapes)[1]
    for dst, packed in zip((out_g, out_d, out_m, out_v), res):
        vals, _ = _unpack_small(packed, full_shapes)
        for n in SMALL_NAMES:
            if n == "conv_w":
                sl = lax.dynamic_slice(vals[n].reshape(L, taps, N_DEV, fc), (0, 0, me, 0), (L, taps, 1, fc))
                dst[n] = sl.reshape(L, taps, fc)
            else:
                dst[n] = vals[n]
    return (loss, grad_x, *[out_g[n] for n in names], *[out_d[n] for n in names],
            *[out_m[n] for n in names], *[out_v[n] for n in names])
```

```python
import functools
import math

import jax
import jax.numpy as jnp
from jax import lax
from jax.experimental import pallas as pl
from jax.experimental.pallas import tpu as pltpu

F32 = jnp.float32
BF16 = jnp.bfloat16
MESH = pl.DeviceIdType.MESH

N_DEV = 8
D_MODEL = 1024
DEPTH = 2
HEAD_DIM = 64
KV = 2
GQ = 4
QW = KV * GQ * HEAD_DIM
KW = KV * HEAD_DIM
ROPE_W = QW + KW
IN_COLS = 2 * (QW + 2 * KW)
D_FF = 2816
GRID_W = 64
ROPE_THETA = 10000.0
WINDOW = 128
N_BUCKETS = 32
MAX_DISTANCE = 128
ALPHA = (2.0 * DEPTH) ** 0.25
RMS_EPS = 1e-6
LN_EPS = 1e-5
SCALE = HEAD_DIM ** -0.5
LOG2E = math.log2(math.e)
LN2 = math.log(2.0)
NEG = -1e30
ONES_ROWS = 16

ADAM_LR = 0.001
ADAM_B1 = 0.9
ADAM_B2 = 0.999
ADAM_EPS = 1e-08
ADAM_WD = 0.01
ADAM_STEP = 10

LANES = 128
MM_ROWS = 1024
MM_ROWS_WIDE = 512
DW_TOKENS = 1024
DW_TOKENS_WIDE = 512
VMEM_CAP = 60 * 1024 * 1024
SMALL_NAMES = ("rel_bias", "q_norm", "k_norm", "sink", "out_norm_a", "out_norm_b", "ln1_g", "ln1_b",
               "conv_b", "ln2_g", "ln2_b", "conv_w")


def _params(sem, est_bytes):
    limit = int(min(VMEM_CAP, est_bytes + (8 << 20)))
    return pltpu.CompilerParams(dimension_semantics=sem, vmem_limit_bytes=limit)


def _nbytes(shape, dtype):
    return math.prod(shape) * jnp.dtype(dtype).itemsize


class _Comm:
    def __init__(self, parts, gathers):
        self.parts, self.gathers, self.n = list(parts), list(gathers), len(parts)
        hbm = pl.BlockSpec(memory_space=pltpu.HBM)
        self.in_specs = [hbm] * self.n
        self.out_specs = [hbm] * self.n
        self.out_shape = [jax.ShapeDtypeStruct((N_DEV,) + tuple(p.shape if g else p.shape[1:]), p.dtype)
                          for p, g in zip(self.parts, self.gathers)]
        self.scratch = [pltpu.SemaphoreType.DMA((self.n * (N_DEV - 1),)), pltpu.SemaphoreType.DMA((self.n * (N_DEV - 1),)),
                        pltpu.SemaphoreType.DMA((self.n,))]

    def bind(self, ins, outs, sems):
        send_sems, recv_sems, local_sems = sems
        gathers, n = self.gathers, self.n
        me = 4 * lax.axis_index("x") + 2 * lax.axis_index("y") + lax.axis_index("c")

        def src(k, j):
            return ins[k] if gathers[k] else ins[k].at[j]

        def copy(k, d, peer, lands_in):
            return pltpu.make_async_remote_copy(
                src_ref=src(k, peer), dst_ref=outs[k].at[lands_in],
                send_sem=send_sems.at[k * (N_DEV - 1) + d - 1], recv_sem=recv_sems.at[k * (N_DEV - 1) + d - 1],
                device_id=(peer // 4, lax.rem(peer // 2, 2), lax.rem(peer, 2)), device_id_type=MESH)

        def send(k, d):
            return copy(k, d, lax.rem(me + d, N_DEV), me)

        def arrival(k, d):
            frm = lax.rem(me + N_DEV - d, N_DEV)
            return copy(k, d, frm, frm)

        def local(k):
            return pltpu.make_async_copy(src(k, me), outs[k].at[me], local_sems.at[k])

        def start():
            for k in range(n):
                local(k).start()
                for d in range(1, N_DEV):
                    send(k, d).start()

        def finish():
            for k in range(n):
                for d in range(1, N_DEV):
                    arrival(k, d).wait_recv()
            for k in range(n):
                for d in range(1, N_DEV):
                    send(k, d).wait_send()
                local(k).wait()

        return start, finish


def _host_comm(comm, refs, n_in, n_out, n_scratch, grid):
    n = comm.n if comm is not None else 0
    own_in, cin = refs[:n_in], refs[n_in:n_in + n]
    own_out, cout = refs[n_in + n:n_in + n + n_out], refs[n_in + n + n_out:n_in + 2 * n + n_out]
    base = n_in + 2 * n + n_out
    own_scratch, sems = refs[base:base + n_scratch], refs[base + n_scratch:]
    own = tuple(own_in) + tuple(own_out) + tuple(own_scratch)
    if comm is None:
        return own, lambda: None, lambda: None
    start, finish = comm.bind(cin, cout, sems)
    first = last = None
    for ax, size in enumerate(grid):
        pid = pl.program_id(ax)
        first = (pid == 0) if first is None else first & (pid == 0)
        last = (pid == size - 1) if last is None else last & (pid == size - 1)
    return own, lambda: pl.when(first)(start), lambda: pl.when(last)(finish)


def _exchange(parts, gathers, name):
    comm = _Comm(parts, gathers)
    n = comm.n

    def body(*refs):
        start, finish = comm.bind(refs[:n], refs[n:2 * n], refs[2 * n:])
        start()
        finish()

    return pl.pallas_call(body, name=name, out_shape=comm.out_shape, in_specs=comm.in_specs, out_specs=comm.out_specs,
                          scratch_shapes=comm.scratch)(*comm.parts)


def _mm(a_list, b_list, *, name, out_dtype, tm, tn, tk, trans_a=False, trans_b=False, add=None, add_scale=1.0):
    assert not (trans_a and trans_b)
    na = len(a_list)
    if trans_a:
        K, M = a_list[0].shape
    else:
        M, K = a_list[0].shape
    N = b_list[0].shape[0 if trans_b else 1]
    tm, tn, tk = min(tm, M), min(tn, N), min(tk, K)
    assert M % tm == 0 and N % tn == 0 and K % tk == 0, (name, M, N, K, tm, tn, tk)
    nk = K // tk
    dims = (((0,), (0,)), ((), ())) if trans_a else (((1,), (1 if trans_b else 0,)), ((), ()))

    def body(*refs):
        a_refs, b_refs = refs[:na], refs[na:2 * na]
        add_ref = refs[2 * na] if add is not None else None
        o_ref = refs[2 * na + (add is not None)]
        k = pl.program_id(2)

        part = None
        for a_ref, b_ref in zip(a_refs, b_refs):
            prod = lax.dot_general(a_ref[...].astype(BF16), b_ref[...].astype(BF16), dims,
                                   preferred_element_type=F32)
            part = prod if part is None else part + prod

        def finish(res):
            if add_ref is not None:
                res = res + add_scale * add_ref[...]
            o_ref[...] = res.astype(o_ref.dtype)

        if nk == 1:
            finish(part)
        else:
            acc_ref = refs[-1]

            @pl.when(k == 0)
            def _():
                acc_ref[...] = part

            @pl.when(k > 0)
            def _():
                acc_ref[...] += part

            @pl.when(k == nk - 1)
            def _():
                finish(acc_ref[...])

    if trans_a:
        a_spec = pl.BlockSpec((tk, tm), lambda i, j, k: (k, i))
    else:
        a_spec = pl.BlockSpec((tm, tk), lambda i, j, k: (i, k))
    if trans_b:
        b_spec = pl.BlockSpec((tn, tk), lambda i, j, k: (j, k))
    else:
        b_spec = pl.BlockSpec((tk, tn), lambda i, j, k: (k, j))
    o_spec = pl.BlockSpec((tm, tn), lambda i, j, k: (i, j))
    in_specs = [a_spec] * na + [b_spec] * na + ([o_spec] if add is not None else [])
    est = (2 * na * (_nbytes((tm, tk), a_list[0].dtype) + _nbytes((tk, tn), b_list[0].dtype))
           + na * (_nbytes((tm, tk), BF16) + _nbytes((tk, tn), BF16))
           + 2 * _nbytes((tm, tn), out_dtype) + 3 * _nbytes((tm, tn), F32)
           + (2 * _nbytes((tm, tn), F32) if add is not None else 0))
    args = list(a_list) + list(b_list) + ([add] if add is not None else [])
    return pl.pallas_call(
        body, name=name, grid=(M // tm, N // tn, nk),
        out_shape=jax.ShapeDtypeStruct((M, N), out_dtype),
        in_specs=in_specs, out_specs=o_spec,
        scratch_shapes=[pltpu.VMEM((tm, tn), F32)] if nk > 1 else [],
        compiler_params=_params(("parallel", "parallel", "arbitrary"), est),
    )(*args)


def _mm_res_ln(a, w, res_hat, res_g, res_b, ln_g, ln_b, *, name, tm):
    T, K = a.shape
    D = w.shape[1]
    tm = min(tm, T)

    def body(a_ref, w_ref, rh_ref, rg_ref, rb_ref, g_ref, b_ref, xhat_ref, rstd_ref, xb_ref):
        branch = jnp.dot(a_ref[...].astype(BF16), w_ref[...], preferred_element_type=F32)
        z = ALPHA * (rh_ref[...] * rg_ref[...] + rb_ref[...]) + branch
        mu = jnp.mean(z, axis=1, keepdims=True)
        zc = z - mu
        var = jnp.mean(zc * zc, axis=1, keepdims=True)
        rstd = lax.rsqrt(var + LN_EPS)
        xhat = zc * rstd
        xhat_ref[...] = xhat
        rstd_ref[...] = rstd
        xb_ref[...] = (xhat * g_ref[...] + b_ref[...]).astype(BF16)

    row = pl.BlockSpec((tm, D), lambda i: (i, 0))
    vec = pl.BlockSpec((1, D), lambda i: (0, 0))
    est = (2 * (_nbytes((tm, K), a.dtype) + _nbytes((K, D), BF16)) + 4 * _nbytes((tm, D), F32) * 2
           + 6 * _nbytes((tm, D), F32))
    return pl.pallas_call(
        body, name=name, grid=(T // tm,),
        out_shape=(jax.ShapeDtypeStruct((T, D), F32), jax.ShapeDtypeStruct((T, 1), F32),
                   jax.ShapeDtypeStruct((T, D), BF16)),
        in_specs=[pl.BlockSpec((tm, K), lambda i: (i, 0)), pl.BlockSpec((K, D), lambda i: (0, 0)), row, vec, vec, vec, vec],
        out_specs=(row, pl.BlockSpec((tm, 1), lambda i: (i, 0)), row),
        compiler_params=_params(("parallel",), est),
    )(a, w, res_hat, res_g, res_b, ln_g, ln_b)


def _ln_bwd(xhat, rstd, ln_g, ln_b, *, name, dx=None, target=None, tm=256):
    T, D = xhat.shape
    tm = min(tm, T)
    head = target is not None

    def body(xhat_ref, rstd_ref, g_ref, b_ref, d_ref, dz_ref, dzb_ref, st_ref):
        i = pl.program_id(0)

        @pl.when(i == 0)
        def _():
            st_ref[...] = jnp.zeros_like(st_ref)

        xh = xhat_ref[...]
        g = g_ref[...]
        if head:
            err = (xh * g + b_ref[...]) - d_ref[...]
            dxv = err * (1.0 / D)
            st_ref[2:3, :] += 0.5 * jnp.sum(jnp.sum(err * err, axis=1, keepdims=True) * (1.0 / D), axis=0, keepdims=True)
        else:
            dxv = d_ref[...]
        st_ref[0:1, :] += jnp.sum(dxv * xh, axis=0, keepdims=True)
        st_ref[1:2, :] += jnp.sum(dxv, axis=0, keepdims=True)
        dxh = dxv * g
        m1 = jnp.mean(dxh, axis=1, keepdims=True)
        m2 = jnp.mean(dxh * xh, axis=1, keepdims=True)
        dz = rstd_ref[...] * (dxh - m1 - xh * m2)
        dz_ref[...] = dz
        dzb_ref[...] = dz.astype(BF16)

    row = pl.BlockSpec((tm, D), lambda i: (i, 0))
    vec = pl.BlockSpec((1, D), lambda i: (0, 0))
    est = 2 * 4 * _nbytes((tm, D), F32) + 6 * _nbytes((tm, D), F32)
    return pl.pallas_call(
        body, name=name, grid=(T // tm,),
        out_shape=(jax.ShapeDtypeStruct((T, D), F32), jax.ShapeDtypeStruct((T, D), BF16),
                   jax.ShapeDtypeStruct((8, D), F32)),
        in_specs=[row, pl.BlockSpec((tm, 1), lambda i: (i, 0)), vec, vec, row],
        out_specs=(row, row, pl.BlockSpec((8, D), lambda i: (0, 0))),
        compiler_params=_params(("arbitrary",), est),
    )(xhat, rstd, ln_g, ln_b, target if head else dx)


def _pair_swap(v, even):
    return jnp.where(even, pltpu.roll(v, LANES - 1, 1), pltpu.roll(v, 1, 1))


def _half_sums(v, lo):
    s_lo = jnp.sum(jnp.where(lo, v, 0.0), axis=1, keepdims=True)
    s_hi = jnp.sum(jnp.where(lo, 0.0, v), axis=1, keepdims=True)
    return jnp.where(lo, s_lo, s_hi)


A_COLS = ROPE_W + KW
A_HEADS = A_COLS // HEAD_DIM
A_K0, A_V0 = KV * GQ, KV * GQ + KV


def _qk_rope_fwd(h, gains, cos2, sin2, *, name, tm=256):
    T = h.shape[0]
    tm = min(tm, T)
    nch = A_COLS // LANES

    def body(h_ref, g_ref, c_ref, s_ref, oT_ref, kv_ref):
        lane = lax.broadcasted_iota(jnp.int32, (tm, LANES), 1)
        lo, even = lane < HEAD_DIM, lane % 2 == 0
        c, s = c_ref[...], s_ref[...]
        for j in range(nch):
            x = h_ref[:, j * LANES:(j + 1) * LANES]
            isq, isv = j < QW // LANES, j == nch - 1
            if isv:
                out = x
            else:
                g = g_ref[0:1, :] if isq else g_ref[1:2, :]
                r = lax.rsqrt(_half_sums(x * x, lo) * (1.0 / HEAD_DIM) + RMS_EPS)
                nrm = x * r * g
                out = nrm * c + _pair_swap(nrm, even) * s
            if isq:
                out = out * (SCALE * LOG2E)
            else:
                kv_ref[:, (j - QW // LANES) * LANES:(j - QW // LANES + 1) * LANES] = out.astype(BF16)
            oT_ref[j * LANES:(j + 1) * LANES, :] = out.T.astype(BF16)

    est = 2 * (_nbytes((tm, A_COLS), F32) + 2 * _nbytes((tm, A_COLS), BF16) + 2 * _nbytes((tm, LANES), F32)) + (4 << 20)
    return pl.pallas_call(
        body, name=name, grid=(T // tm,),
        out_shape=(jax.ShapeDtypeStruct((A_COLS, T), BF16), jax.ShapeDtypeStruct((T, 2 * KW), BF16)),
        in_specs=[pl.BlockSpec((tm, A_COLS), lambda i: (i, 0)), pl.BlockSpec((8, LANES), lambda i: (0, 0)),
                  pl.BlockSpec((tm, LANES), lambda i: (i, 0)), pl.BlockSpec((tm, LANES), lambda i: (i, 0))],
        out_specs=(pl.BlockSpec((A_COLS, tm), lambda i: (0, i)), pl.BlockSpec((tm, 2 * KW), lambda i: (i, 0))),
        compiler_params=_params(("parallel",), est),
    )(h, gains, cos2, sin2)


def _qk_rope_bwd(h, dqT, dkT, gains, cos2, sin2, *, name, tm=256):
    T = h.shape[0]
    tm = min(tm, T)
    nch = ROPE_W // LANES

    def body(h_ref, dq_ref, dk_ref, g_ref, c_ref, s_ref, dh_ref, dg_ref):
        i = pl.program_id(0)

        @pl.when(i == 0)
        def _():
            dg_ref[...] = jnp.zeros_like(dg_ref)

        lane = lax.broadcasted_iota(jnp.int32, (tm, LANES), 1)
        lo, even = lane < HEAD_DIM, lane % 2 == 0
        c, s = c_ref[...], s_ref[...]
        acc = [None, None]
        for j in range(nch):
            x = h_ref[:, j * LANES:(j + 1) * LANES]
            isq = j < QW // LANES
            g = g_ref[0:1, :] if isq else g_ref[1:2, :]
            d = dq_ref[j * LANES:(j + 1) * LANES, :].T * SCALE if isq else dk_ref[...].T
            r = lax.rsqrt(_half_sums(x * x, lo) * (1.0 / HEAD_DIM) + RMS_EPS)
            dn = d * c + _pair_swap(d * s, even)
            xr = x * r
            part = jnp.sum(dn * xr, axis=0, keepdims=True)
            acc[0 if isq else 1] = part if acc[0 if isq else 1] is None else acc[0 if isq else 1] + part
            dng = dn * g
            dx = r * dng - xr * (r * r) * (_half_sums(dng * x, lo) * (1.0 / HEAD_DIM))
            dh_ref[:, j * LANES:(j + 1) * LANES] = dx.astype(BF16)
        for row in range(2):
            folded = acc[row] + pltpu.roll(acc[row], HEAD_DIM, 1)
            dg_ref[row:row + 1, :] += folded

    est = 2 * (2 * _nbytes((tm, ROPE_W), F32) + _nbytes((tm, ROPE_W), BF16) + 2 * _nbytes((tm, LANES), F32)) + (4 << 20)
    return pl.pallas_call(
        body, name=name, grid=(T // tm,),
        out_shape=(jax.ShapeDtypeStruct((T, ROPE_W), BF16), jax.ShapeDtypeStruct((8, LANES), F32)),
        in_specs=[pl.BlockSpec((tm, ROPE_W), lambda i: (i, 0)), pl.BlockSpec((QW, tm), lambda i: (0, i)),
                  pl.BlockSpec((KW, tm), lambda i: (0, i)), pl.BlockSpec((8, LANES), lambda i: (0, 0)),
                  pl.BlockSpec((tm, LANES), lambda i: (i, 0)), pl.BlockSpec((tm, LANES), lambda i: (i, 0))],
        out_specs=(pl.BlockSpec((tm, ROPE_W), lambda i: (i, 0)), pl.BlockSpec((8, LANES), lambda i: (0, 0))),
        compiler_params=_params(("arbitrary",), est),
    )(h, dqT, dkT, gains, cos2, sin2)


def _attn_a_fwd(k, hT, *, comm=None, tq=4096, tk=2048, cq=512):
    G, T, HD = k.shape
    HE = HD + ONES_ROWS
    tq, tk = min(tq, T), min(tk, T)
    cq = min(cq, tq)
    nk, nt = T // tk, T // tq
    grid = (G, GQ * nt, nk)

    def body(*refs):
        (k_ref, qT_ref, v_ref, oT_ref, lse_ref, m_sc, acc_sc), comm_start, comm_finish = _host_comm(
            comm, refs, 3, 2, 2, grid)
        kv = pl.program_id(2)
        comm_start()
        v1T = jnp.concatenate([v_ref[...], jnp.ones((ONES_ROWS, tk), BF16)], axis=0)

        @pl.when(kv == 0)
        def _():
            m_sc[...] = jnp.full_like(m_sc, NEG)
            acc_sc[...] = jnp.zeros_like(acc_sc)

        def scores(c):
            return jnp.dot(k_ref[...], qT_ref[:, c * cq:(c + 1) * cq], preferred_element_type=F32)

        nc = tq // cq
        ahead = scores(0)
        for c in range(nc):
            cols = slice(c * cq, (c + 1) * cq)
            sT = ahead
            if c + 1 < nc:
                ahead = scores(c + 1)
            m_prev = m_sc[:, cols]
            m_new = jnp.maximum(m_prev, jnp.max(sT, axis=0, keepdims=True))
            pT = jnp.exp2(sT - m_new).astype(BF16)
            acc_sc[:, cols] = (jnp.exp2(m_prev - m_new) * acc_sc[:, cols]
                               + jnp.dot(v1T, pT, preferred_element_type=F32))
            m_sc[:, cols] = m_new

        @pl.when(kv == nk - 1)
        def _():
            l = acc_sc[HD:HD + 1, :]
            oT_ref[...] = acc_sc[0:HD, :] / l
            lse_ref[...] = m_sc[...] + jnp.log2(l)

        comm_finish()

    qtr = pl.BlockSpec((None, HD, tq), lambda g, i, j: (g * GQ + i // nt, 0, i % nt))
    qvec = pl.BlockSpec((None, 1, tq), lambda g, i, j: (g * GQ + i // nt, 0, i % nt))
    est = 6 * _nbytes((cq, tk), F32) + (8 << 20)
    hosted = comm is not None
    return pl.pallas_call(
        body, name="attn_a_fwd_comm" if hosted else "attn_a_fwd", grid=grid,
        out_shape=[jax.ShapeDtypeStruct((G * GQ, HD, T), F32), jax.ShapeDtypeStruct((G * GQ, 1, T), F32)]
        + (comm.out_shape if hosted else []),
        in_specs=[pl.BlockSpec((None, tk, HD), lambda g, i, j: (g, j, 0)), qtr,
                  pl.BlockSpec((None, HD, tk), lambda g, i, j: (A_V0 + g, 0, j))] + (comm.in_specs if hosted else []),
        out_specs=[qtr, qvec] + (comm.out_specs if hosted else []),
        scratch_shapes=[pltpu.VMEM((1, tq), F32), pltpu.VMEM((HE, tq), F32)] + (comm.scratch if hosted else []),
        compiler_params=_params(("arbitrary",) * 3 if hosted else ("parallel", "parallel", "arbitrary"), est),
    )(k, hT, hT, *(comm.parts if hosted else []))


def _attn_a_bwd(k, v, hT, doT, lse_row, delta_row, *, comm=None, tq=4096, tk=1024, cq=256):
    G, T, HD = k.shape
    tq, tk = min(tq, T), min(tk, T)
    cq = min(cq, tq)
    nqt = T // tq
    nq, nc = GQ * nqt, tq // cq
    nt = (((1,), (1,)), ((), ()))

    grid = (G, T // tk, nq)

    def body(*refs):
        (k_ref, v_ref, kT_ref, qT_ref, doT_ref, lse_ref, dl_ref, dkT_ref, dvT_ref, dqT_ref, dk_sc, dv_sc), \
            comm_start, comm_finish = _host_comm(comm, refs, 7, 3, 2, grid)
        j, i = pl.program_id(1), pl.program_id(2)
        comm_start()

        @pl.when((j == 0) & (i == 0))
        def _():
            dqT_ref[...] = jnp.zeros_like(dqT_ref)

        @pl.when(i == 0)
        def _():
            dk_sc[...] = jnp.zeros_like(dk_sc)
            dv_sc[...] = jnp.zeros_like(dv_sc)

        def scores(c):
            cols = slice(c * cq, (c + 1) * cq)
            return (jnp.dot(k_ref[...], qT_ref[:, cols], preferred_element_type=F32),
                    jnp.dot(v_ref[...], doT_ref[:, cols], preferred_element_type=F32))

        ahead = scores(0)
        dk_part = dv_part = None
        for c in range(nc):
            cols = slice(c * cq, (c + 1) * cq)
            sT, dpT = ahead
            if c + 1 < nc:
                ahead = scores(c + 1)
            pT = jnp.exp2(sT - lse_ref[:, cols])
            dsT = (pT * (dpT - dl_ref[:, cols])).astype(BF16)
            dv_c = lax.dot_general(doT_ref[:, cols], pT.astype(BF16), nt, preferred_element_type=F32)
            dk_c = lax.dot_general(qT_ref[:, cols], dsT, nt, preferred_element_type=F32)
            dv_part = dv_c if dv_part is None else dv_part + dv_c
            dk_part = dk_c if dk_part is None else dk_part + dk_c
            out_cols = pl.ds(pl.multiple_of((i % nqt) * tq + c * cq, cq), cq)
            dqT_ref[i // nqt, :, out_cols] += jnp.dot(kT_ref[...], dsT, preferred_element_type=F32)
        dk_sc[...] += dk_part
        dv_sc[...] += dv_part

        @pl.when(i == nq - 1)
        def _():
            dkT_ref[...] = dk_sc[...] * LN2
            dvT_ref[...] = dv_sc[...]

        comm_finish()

    krow = pl.BlockSpec((None, tk, HD), lambda g, j, i: (g, j, 0))
    ktr = pl.BlockSpec((None, HD, tk), lambda g, j, i: (g, 0, j))
    ktr_h = pl.BlockSpec((None, HD, tk), lambda g, j, i: (A_K0 + g, 0, j))
    qtr = pl.BlockSpec((None, HD, tq), lambda g, j, i: (g * GQ + i // nqt, 0, i % nqt))
    qvec = pl.BlockSpec((None, 1, tq), lambda g, j, i: (g * GQ + i // nqt, 0, i % nqt))
    whole = pl.BlockSpec((GQ, HD, T), lambda g, j, i: (g, 0, 0))
    est = 8 * _nbytes((cq, tk), F32) + 2 * _nbytes((GQ, HD, T), F32) + (8 << 20)
    hosted = comm is not None
    return pl.pallas_call(
        body, name="attn_a_bwd_comm" if hosted else "attn_a_bwd", grid=grid,
        out_shape=[jax.ShapeDtypeStruct((G, HD, T), F32), jax.ShapeDtypeStruct((G, HD, T), F32),
                   jax.ShapeDtypeStruct((G * GQ, HD, T), F32)] + (comm.out_shape if hosted else []),
        in_specs=[krow, krow, ktr_h, qtr, qtr, qvec, qvec] + (comm.in_specs if hosted else []),
        out_specs=[ktr, ktr, whole] + (comm.out_specs if hosted else []),
        scratch_shapes=[pltpu.VMEM((HD, tk), F32), pltpu.VMEM((HD, tk), F32)] + (comm.scratch if hosted else []),
        compiler_params=_params(("arbitrary", "arbitrary", "arbitrary"), est),
    )(k, v, hT, hT, doT, lse_row, delta_row, *(comm.parts if hosted else []))


WB = WINDOW
WK = 3 * WINDOW


QB_COL0 = (ROPE_W + KW) // (GQ * HEAD_DIM)
KB_COL = (ROPE_W + KW + QW) // KW
GW = GQ * HEAD_DIM


WSTEP = 2


def _win_in_specs(T):
    nb = T // WB
    assert nb % WSTEP == 0
    ns = nb // WSTEP
    q = [pl.BlockSpec((WSTEP * WB, GW), functools.partial(lambda n, g: (n, QB_COL0 + g), g=g)) for g in range(KV)]
    kv = [pl.BlockSpec((WB, KW), functools.partial(lambda n, o, c: (jnp.clip(WSTEP * n + o, 0, nb - 1), c), o=o, c=c))
          for c in (KB_COL, KB_COL + 1) for o in range(-1, WSTEP + 1)]
    bias = [pl.BlockSpec((None, KV, WK, GQ * WB), lambda n: (jnp.where(n == 0, 0, 1), 0, 0, 0)),
            pl.BlockSpec((None, KV, WK, GQ * WB), lambda n: (jnp.where(n == ns - 1, 2, 1), 0, 0, 0))]
    return ns, q + kv, bias


def _end_tables(biasT):
    key = lax.broadcasted_iota(jnp.int32, biasT.shape, 1)
    return jnp.stack([jnp.where(key < WB, NEG, biasT), biasT, jnp.where(key >= 2 * WB, NEG, biasT)])


def _heads_to_lanes(t):
    return jnp.concatenate([t[i * HEAD_DIM:(i + 1) * HEAD_DIM] for i in range(GQ)], axis=1)


def _lanes_to_heads(t):
    return jnp.concatenate([t[:, i * WB:(i + 1) * WB] for i in range(GQ)], axis=0)


def _attn_b_fwd(h, biasT, sink_rows):
    T = h.shape[0]
    ns, in_specs, bias_specs = _win_in_specs(T)
    nkv = WSTEP + 2

    def body(*refs):
        q_refs, k_refs, v_refs = refs[:KV], refs[KV:KV + nkv], refs[KV + nkv:KV + 2 * nkv]
        b_refs, sk_ref, o_ref, lse_ref = refs[KV + 2 * nkv:KV + 2 * nkv + WSTEP], *refs[KV + 2 * nkv + WSTEP:]
        ks, vs = [r[...] for r in k_refs], [r[...] for r in v_refs]
        outs = []
        for b, b_ref in enumerate(b_refs):
            rows = slice(b * WB, (b + 1) * WB)
            kwin = jnp.concatenate(ks[b:b + 3], axis=0)
            vT = jnp.concatenate(vs[b:b + 3], axis=0).T
            qT = [_heads_to_lanes((q[rows, :] * SCALE).T).astype(BF16) for q in q_refs]
            sT = [jnp.dot(kwin[:, g * HEAD_DIM:(g + 1) * HEAD_DIM].astype(BF16), qT[g], preferred_element_type=F32)
                  for g in range(KV)]
            oT = []
            for g in range(KV):
                s = sT[g] + b_ref[g]
                sk = sk_ref[g]
                m = jnp.maximum(jnp.max(s, axis=0, keepdims=True), sk)
                p = jnp.exp(s - m)
                den = jnp.sum(p, axis=0, keepdims=True) + jnp.exp(sk - m)
                o = jnp.dot(vT[g * HEAD_DIM:(g + 1) * HEAD_DIM].astype(BF16), p.astype(BF16),
                            preferred_element_type=F32) / den
                lse_ref[b, g] = m + jnp.log(den)
                oT.append(_lanes_to_heads(o))
            outs.append(jnp.concatenate(oT, axis=0).T)
        o_ref[...] = jnp.concatenate(outs, axis=0)

    whole = lambda *shape: pl.BlockSpec(shape, lambda n: (0,) * len(shape))
    return pl.pallas_call(
        body, name="attn_b_fwd", grid=(ns,),
        out_shape=(jax.ShapeDtypeStruct((T, QW), F32), jax.ShapeDtypeStruct((ns * WSTEP, KV, 1, GQ * WB), F32)),
        in_specs=in_specs + bias_specs + [whole(KV, 1, GQ * WB)],
        out_specs=(pl.BlockSpec((WSTEP * WB, QW), lambda n: (n, 0)),
                   pl.BlockSpec((WSTEP, KV, 1, GQ * WB), lambda n: (n, 0, 0, 0))),
        compiler_params=_params(("parallel",), 32 << 20),
    )(*([h] * (KV + 2 * nkv)), *([biasT] * WSTEP), sink_rows)


def _attn_b_bwd(h, do, o, lse, biasT, sink_rows):
    T = h.shape[0]
    ns, in_specs, bias_specs = _win_in_specs(T)
    nkv = WSTEP + 2
    Tp = T + 2 * WB
    nt = (((1,), (1,)), ((), ()))

    def body(*refs):
        q_refs, k_refs, v_refs = refs[:KV], refs[KV:KV + nkv], refs[KV + nkv:KV + 2 * nkv]
        at = KV + 2 * nkv
        do_ref, o_ref, lse_ref = refs[at:at + 3]
        b_refs, sk_ref = refs[at + 3:at + 3 + WSTEP], refs[at + 3 + WSTEP]
        dq_ref, dk_ref, dv_ref, db_ref, dsk_ref = refs[at + 4 + WSTEP:]
        n = pl.program_id(0)

        @pl.when(n == 0)
        def _():
            dk_ref[...] = jnp.zeros_like(dk_ref)
            dv_ref[...] = jnp.zeros_like(dv_ref)
            db_ref[...] = jnp.zeros_like(db_ref)
            dsk_ref[...] = jnp.zeros_like(dsk_ref)

        ks, vs = [r[...] for r in k_refs], [r[...] for r in v_refs]
        for b, b_ref in enumerate(b_refs):
            rows = slice(b * WB, (b + 1) * WB)
            kwin = jnp.concatenate(ks[b:b + 3], axis=0)
            vwin = jnp.concatenate(vs[b:b + 3], axis=0)
            kT = kwin.T
            doT_all, oT_all = do_ref[rows, :].T, o_ref[rows, :].T
            qT, doT, delta, sT, dpT = [], [], [], [], []
            for g, q in enumerate(q_refs):
                hd = slice(g * HEAD_DIM, (g + 1) * HEAD_DIM)
                qT.append(_heads_to_lanes((q[rows, :] * SCALE).T).astype(BF16))
                d = _heads_to_lanes(doT_all[g * GW:(g + 1) * GW])
                delta.append(jnp.sum(d * _heads_to_lanes(oT_all[g * GW:(g + 1) * GW]), axis=0, keepdims=True))
                doT.append(d.astype(BF16))
                sT.append(jnp.dot(kwin[:, hd].astype(BF16), qT[g], preferred_element_type=F32))
                dpT.append(jnp.dot(vwin[:, hd].astype(BF16), doT[g], preferred_element_type=F32))
            dq, dk, dv = [], [], []
            for g in range(KV):
                lse_g = lse_ref[b, g]
                p = jnp.exp(sT[g] + b_ref[g] - lse_g)
                ds = p * (dpT[g] - delta[g])
                db_ref[g] += ds
                dsk_ref[g] -= jnp.exp(sk_ref[g] - lse_g) * delta[g]
                dsb = ds.astype(BF16)
                dqT = jnp.dot(kT[g * HEAD_DIM:(g + 1) * HEAD_DIM].astype(BF16), dsb, preferred_element_type=F32)
                dq.append(_lanes_to_heads(dqT))
                dk.append(lax.dot_general(dsb, qT[g], nt, preferred_element_type=F32))
                dv.append(lax.dot_general(p.astype(BF16), doT[g], nt, preferred_element_type=F32))
            dq_ref[rows, :] = (jnp.concatenate(dq, axis=0).T * SCALE).astype(BF16)
            win = pl.ds(pl.multiple_of((WSTEP * n + b) * WB, WB), WK)
            dk_ref[win, :] += jnp.concatenate(dk, axis=1)
            dv_ref[win, :] += jnp.concatenate(dv, axis=1)

    whole = lambda *shape: pl.BlockSpec(shape, lambda n: (0,) * len(shape))
    tok = pl.BlockSpec((WSTEP * WB, QW), lambda n: (n, 0))
    return pl.pallas_call(
        body, name="attn_b_bwd", grid=(ns,),
        out_shape=(jax.ShapeDtypeStruct((T, QW), BF16),
                   jax.ShapeDtypeStruct((Tp, KW), F32), jax.ShapeDtypeStruct((Tp, KW), F32),
                   jax.ShapeDtypeStruct((KV, WK, GQ * WB), F32), jax.ShapeDtypeStruct((KV, 1, GQ * WB), F32)),
        in_specs=in_specs + [tok, tok, pl.BlockSpec((WSTEP, KV, 1, GQ * WB), lambda n: (n, 0, 0, 0))]
        + bias_specs + [whole(KV, 1, GQ * WB)],
        out_specs=(tok, whole(Tp, KW), whole(Tp, KW), whole(KV, WK, GQ * WB), whole(KV, 1, GQ * WB)),
        compiler_params=_params(("arbitrary",), 48 << 20),
    )(*([h] * (KV + 2 * nkv)), do, o, lse, *([biasT] * WSTEP), sink_rows)


def _bias_table(rel_bias_t, bucket):
    nh, n = rel_bias_t.shape[0], bucket.shape[1]

    def body(rb_ref, bk_ref, o_ref):
        bk = bk_ref[...]
        out = jnp.full((nh, n), NEG, F32)
        for b in range(N_BUCKETS):
            out = jnp.where(bk == b, rb_ref[:, b:b + 1], out)
        o_ref[...] = out

    return pl.pallas_call(
        body, name="bias_table", out_shape=jax.ShapeDtypeStruct((nh, n), F32),
        compiler_params=pltpu.CompilerParams(vmem_limit_bytes=32 << 20),
    )(rel_bias_t, bucket)


def _bias_sink_grads(db_list, dsk_list, bucket):
    L = len(db_list)

    def body(*refs):
        db_refs, dsk_refs, bk_ref = refs[:L], refs[L:2 * L], refs[2 * L]
        drb_ref, dsink_ref = refs[2 * L + 1], refs[2 * L + 2]
        tot = db_refs[0][...]
        for r in db_refs[1:]:
            tot = tot + r[...]
        bk = bk_ref[...]
        lane = lax.broadcasted_iota(jnp.int32, (2 * GQ, N_BUCKETS), 1)
        out = jnp.zeros((2 * GQ, N_BUCKETS), F32)
        for b in range(N_BUCKETS):
            sb = jnp.sum(jnp.where(bk == b, tot, 0.0), axis=1, keepdims=True)
            out = jnp.where(lane == b, sb, out)
        drb_ref[...] = out
        for l in range(L):
            dsink_ref[l] = jnp.sum(dsk_refs[l][...], axis=1, keepdims=True)

    return pl.pallas_call(
        body, name="bias_sink_grads",
        out_shape=(jax.ShapeDtypeStruct((2 * GQ, N_BUCKETS), F32), jax.ShapeDtypeStruct((L, 2 * GQ, 1), F32)),
        compiler_params=pltpu.CompilerParams(vmem_limit_bytes=32 << 20),
    )(*db_list, *dsk_list, bucket)


def _outnorm_fwd(oaT, ob, ga, gb, *, tm=512):
    T = ob.shape[0]
    tm = min(tm, T)

    def body(oaT_ref, ob_ref, ga_ref, gb_ref, y_ref):
        for j, (o, g_ref) in enumerate(((oaT_ref[...].T, ga_ref), (ob_ref[...], gb_ref))):
            r = lax.rsqrt(jnp.mean(o * o, axis=1, keepdims=True) + RMS_EPS)
            y_ref[:, j * QW:(j + 1) * QW] = (o * r * g_ref[...]).astype(BF16)

    half = pl.BlockSpec((tm, QW), lambda i: (i, 0))
    halfT = pl.BlockSpec((QW, tm), lambda i: (0, i))
    vec = pl.BlockSpec((1, QW), lambda i: (0, 0))
    return pl.pallas_call(
        body, name="outnorm_fwd", grid=(T // tm,),
        out_shape=jax.ShapeDtypeStruct((T, 2 * QW), BF16),
        in_specs=[halfT, half, vec, vec], out_specs=pl.BlockSpec((tm, 2 * QW), lambda i: (i, 0)),
        compiler_params=_params(("parallel",), 16 << 20),
    )(oaT, ob, ga, gb)


def _outnorm_bwd(dy, oaT, ob, ga, gb, *, tm=512):
    T = ob.shape[0]
    tm = min(tm, T)
    nh = QW // HEAD_DIM

    def body(dy_ref, oaT_ref, ob_ref, ga_ref, gb_ref, doaT_ref, dl_ref, dob_ref, dg_ref):
        i = pl.program_id(0)

        @pl.when(i == 0)
        def _():
            dg_ref[...] = jnp.zeros_like(dg_ref)

        oaT = oaT_ref[...]
        for j, (o, g_ref) in enumerate(((oaT.T, ga_ref), (ob_ref[...], gb_ref))):
            d = dy_ref[:, j * QW:(j + 1) * QW]
            r = lax.rsqrt(jnp.mean(o * o, axis=1, keepdims=True) + RMS_EPS)
            orr = o * r
            dg_ref[j:j + 1, :] += jnp.sum(d * orr, axis=0, keepdims=True)
            dgv = d * g_ref[...]
            do = r * dgv - orr * (r * r) * jnp.mean(dgv * o, axis=1, keepdims=True)
            if j == 0:
                doT = do.T
                doaT_ref[...] = doT.astype(BF16)
                prod = doT * oaT
                dl_ref[...] = jnp.concatenate(
                    [jnp.sum(prod[a * HEAD_DIM:(a + 1) * HEAD_DIM], axis=0, keepdims=True) for a in range(nh)], axis=0)
            else:
                dob_ref[...] = do

    half = pl.BlockSpec((tm, QW), lambda i: (i, 0))
    halfT = pl.BlockSpec((QW, tm), lambda i: (0, i))
    vec = pl.BlockSpec((1, QW), lambda i: (0, 0))
    return pl.pallas_call(
        body, name="outnorm_bwd", grid=(T // tm,),
        out_shape=(jax.ShapeDtypeStruct((QW, T), BF16), jax.ShapeDtypeStruct((nh, T), F32),
                   jax.ShapeDtypeStruct((T, QW), F32), jax.ShapeDtypeStruct((8, QW), F32)),
        in_specs=[pl.BlockSpec((tm, 2 * QW), lambda i: (i, 0)), halfT, half, vec, vec],
        out_specs=(halfT, pl.BlockSpec((nh, tm), lambda i: (0, i)), half, pl.BlockSpec((8, QW), lambda i: (0, 0))),
        compiler_params=_params(("arbitrary",), 32 << 20),
    )(dy, oaT, ob, ga, gb)


GELU_C = math.sqrt(2.0 / math.pi)
GELU_A = 0.044715
HALO = 16
SUB = 8


def _gelu_parts(x):
    x2 = x * x
    t = jnp.tanh(x * (GELU_C + (GELU_C * GELU_A) * x2))
    return 0.5 * (1.0 + t), t, x2


def _halo_specs(tm, tn, T):
    nh = tm // HALO
    last = T // HALO - 1
    cur = pl.BlockSpec((tm, tn), lambda j, i: (i, j))
    prev = pl.BlockSpec((HALO, tn), lambda j, i: (jnp.maximum(i * nh - 1, 0), j))
    nxt = pl.BlockSpec((HALO, tn), lambda j, i: (jnp.minimum((i + 1) * nh, last), j))
    return cur, prev, nxt


def _conv_glu_fwd(g, u, conv_w, conv_b, *, tm=256, tn=1408):
    T, F = g.shape
    tm, tn = min(tm, T), min(tn, F)
    cur, prev, nxt = _halo_specs(tm, tn, T)

    def body(g_ref, gp_ref, gn_ref, u_ref, w_ref, b_ref, a_ref):
        i = pl.program_id(1)
        gv = g_ref[...]
        before = jnp.where(i * tm > 0, gp_ref[HALO - SUB:, :], 0.0)
        after = jnp.where((i + 1) * tm < T, gn_ref[0:SUB, :], 0.0)
        gm1 = pltpu.roll(jnp.concatenate([before, gv], axis=0), 1, 0)[SUB:]
        gp1 = pltpu.roll(jnp.concatenate([gv, after], axis=0), tm + SUB - 1, 0)[:tm]
        gc = ((b_ref[...] + gm1 * w_ref[0:1, :]) + gv * w_ref[1:2, :]) + gp1 * w_ref[2:3, :]
        cdf, _, _ = _gelu_parts(gc)
        a_ref[...] = (gc * cdf * u_ref[...].astype(F32)).astype(BF16)

    wspec = pl.BlockSpec((8, tn), lambda j, i: (0, j))
    est = 2 * (3 * _nbytes((tm, tn), F32)) + 8 * _nbytes((tm, tn), F32)
    return pl.pallas_call(
        body, name="conv_glu_fwd", grid=(F // tn, T // tm),
        out_shape=jax.ShapeDtypeStruct((T, F), BF16),
        in_specs=[cur, prev, nxt, cur, wspec, pl.BlockSpec((1, tn), lambda j, i: (0, j))],
        out_specs=cur,
        compiler_params=_params(("parallel", "parallel"), est),
    )(g, g, g, u, conv_w, conv_b)


def _conv_glu_bwd(dact, g, u, conv_w, conv_b, *, tm=256, tn=1408):
    T, F = g.shape
    tm, tn = min(tm, T), min(tn, F)
    cur, prev, nxt = _halo_specs(tm, tn, T)
    te = tm + 2 * HALO

    def body(d_ref, dp_ref, dn_ref, g_ref, gp_ref, gn_ref, u_ref, up_ref, un_ref, w_ref, b_ref,
             dg_ref, du_ref, dc_ref):
        i = pl.program_id(1)

        @pl.when(i == 0)
        def _():
            dc_ref[...] = jnp.zeros_like(dc_ref)

        has_prev, has_next = i * tm > 0, (i + 1) * tm < T
        ge = jnp.concatenate([jnp.where(has_prev, gp_ref[...], 0.0), g_ref[...],
                              jnp.where(has_next, gn_ref[...], 0.0)], axis=0)
        ue = jnp.concatenate([up_ref[...], u_ref[...], un_ref[...]], axis=0).astype(F32)
        de = jnp.concatenate([jnp.where(has_prev, dp_ref[...].astype(F32), 0.0), d_ref[...].astype(F32),
                              jnp.where(has_next, dn_ref[...].astype(F32), 0.0)], axis=0)
        w0, w1, w2 = w_ref[0:1, :], w_ref[1:2, :], w_ref[2:3, :]
        gm1 = pltpu.roll(ge, 1, 0)
        gp1 = pltpu.roll(ge, te - 1, 0)
        gc = ((b_ref[...] + gm1 * w0) + ge * w1) + gp1 * w2
        cdf, t, gc2 = _gelu_parts(gc)
        dgelu = cdf + (0.5 * gc) * (1.0 - t * t) * (GELU_C + (3.0 * GELU_C * GELU_A) * gc2)
        dgc = de * ue * dgelu
        dge = w0 * pltpu.roll(dgc, te - 1, 0) + w1 * dgc + w2 * pltpu.roll(dgc, 1, 0)
        mid = slice(HALO, HALO + tm)
        dg_ref[...] = dge[mid].astype(BF16)
        du_ref[...] = (de[mid] * (gc[mid] * cdf[mid])).astype(BF16)
        dgm = dgc[mid]
        dc_ref[0:1, :] += jnp.sum(dgm * gm1[mid], axis=0, keepdims=True)
        dc_ref[1:2, :] += jnp.sum(dgm * ge[mid], axis=0, keepdims=True)
        dc_ref[2:3, :] += jnp.sum(dgm * gp1[mid], axis=0, keepdims=True)
        dc_ref[3:4, :] += jnp.sum(dgm, axis=0, keepdims=True)

    wspec = pl.BlockSpec((8, tn), lambda j, i: (0, j))
    est = 2 * (3 * _nbytes((tm, tn), F32) + 2 * _nbytes((tm, tn), BF16)) + 16 * _nbytes((te, tn), F32)
    return pl.pallas_call(
        body, name="conv_glu_bwd", grid=(F // tn, T // tm),
        out_shape=(jax.ShapeDtypeStruct((T, F), BF16), jax.ShapeDtypeStruct((T, F), BF16),
                   jax.ShapeDtypeStruct((8, F), F32)),
        in_specs=[cur, prev, nxt, cur, prev, nxt, cur, prev, nxt, wspec, pl.BlockSpec((1, tn), lambda j, i: (0, j))],
        out_specs=(cur, cur, wspec),
        compiler_params=_params(("parallel", "arbitrary"), est),
    )(dact, dact, dact, g, g, g, u, u, u, conv_w, conv_b)


def _adamw_math(w, g, m, v):
    m = ADAM_B1 * m + (1.0 - ADAM_B1) * g
    v = ADAM_B2 * v + (1.0 - ADAM_B2) * (g * g)
    m_hat = m / (1.0 - ADAM_B1 ** ADAM_STEP)
    v_hat = v / (1.0 - ADAM_B2 ** ADAM_STEP)
    delta = -ADAM_LR * (m_hat / (jnp.sqrt(v_hat) + ADAM_EPS) + ADAM_WD * w)
    return delta, m, v


def _adamw(w, m, v, gparts, *, name, tr):
    R, C = w.shape
    tr = min(tr, R)
    assert R % tr == 0

    def body(w_ref, m_ref, v_ref, gp_ref, g_ref, d_ref, nm_ref, nv_ref):
        g = gp_ref[0].astype(F32)
        for j in range(1, N_DEV):
            g = g + gp_ref[j].astype(F32)
        delta, nm, nv = _adamw_math(w_ref[...], g, m_ref[...], v_ref[...])
        g_ref[...] = g
        d_ref[...] = delta
        nm_ref[...] = nm
        nv_ref[...] = nv

    blk = pl.BlockSpec((tr, C), lambda i: (i, 0))
    out = jax.ShapeDtypeStruct((R, C), F32)
    return pl.pallas_call(
        body, name=name, grid=(R // tr,), out_shape=(out, out, out, out),
        in_specs=[blk, blk, blk, pl.BlockSpec((N_DEV, tr, C), lambda i: (0, i, 0))],
        out_specs=(blk, blk, blk, blk),
        compiler_params=_params(("parallel",), 24 << 20),
    )(w, m, v, gparts)


def _rope_tables(T):
    rows_n = T // GRID_W
    row = jnp.repeat(jnp.arange(rows_n, dtype=F32), GRID_W)
    col = jnp.tile(jnp.arange(GRID_W, dtype=F32), rows_n)
    half = HEAD_DIM // 2
    inv_freq = ROPE_THETA ** (-jnp.arange(0, half, 2, dtype=F32) / half)
    ang = jnp.concatenate([row[:, None] * inv_freq, col[:, None] * inv_freq], axis=-1)
    cos, sin = jnp.cos(ang), jnp.sin(ang)
    cos64 = jnp.repeat(cos, 2, axis=-1)
    sin64 = jnp.stack([-sin, sin], axis=-1).reshape(T, HEAD_DIM)
    return jnp.tile(cos64, (1, 2)), jnp.tile(sin64, (1, 2))


def _t5_bucket(rel):
    half = N_BUCKETS // 2
    max_exact = half // 2
    bucket = jnp.where(rel > 0, half, 0)
    rp = jnp.abs(rel)
    rpf = jnp.maximum(rp, 1).astype(F32)
    large = max_exact + (jnp.log(rpf / max_exact) / math.log(MAX_DISTANCE / max_exact)
                         * (half - max_exact)).astype(jnp.int32)
    large = jnp.minimum(large, half - 1)
    return bucket + jnp.where(rp < max_exact, rp, large)


def _window_buckets():
    qpos = jnp.arange(WB, dtype=jnp.int32)
    kpos = jnp.arange(WK, dtype=jnp.int32) - WB
    rel = kpos[None, :] - qpos[:, None]
    return jnp.where(jnp.abs(rel) <= WINDOW, _t5_bucket(rel), -1)


def _heads_first(a, nh):
    T = a.shape[0]
    return a.reshape(T, nh, HEAD_DIM).transpose(1, 0, 2)


def _row(v):
    return v.reshape(1, -1)


def _rows8(rows, width):
    a = jnp.stack(list(rows), axis=0)
    return jnp.pad(a, ((0, 8 - a.shape[0]), (0, 0)))


def _layer_fwd(l, xin, W, tabs, comm=None, on_comm=None):
    xhat, xg, xb, x16 = xin
    T = xhat.shape[0]
    cos2, sin2, biasT = tabs
    h = _mm([x16], [W["w_in"][l]], name="mm_in", out_dtype=F32, tm=MM_ROWS, tn=IN_COLS, tk=D_MODEL)
    gains = _rows8([jnp.tile(W["q_norm"][l], 2), jnp.tile(W["k_norm"][l], 2)], LANES)
    hT, kv_nat = _qk_rope_fwd(h, gains, cos2, sin2, name="qk_rope_fwd")
    hT = hT.reshape(A_HEADS, HEAD_DIM, T)
    ka, va = _heads_first(kv_nat[:, :KW], KV), _heads_first(kv_nat[:, KW:], KV)
    res = _attn_a_fwd(ka, hT, comm=comm)
    oaT, lse_a = res[0].reshape(QW, T), res[1]
    if comm is not None:
        on_comm(res[2:])
    sink_rows = jnp.repeat(W["sink"][l], WB).reshape(KV, 1, GQ * WB)
    ob_t, lse_b = _attn_b_fwd(h, biasT, sink_rows)
    ga, gb = _row(W["out_norm_a"][l]), _row(W["out_norm_b"][l])
    ycat = _outnorm_fwd(oaT, ob_t, ga, gb)
    g1, b1 = _row(W["ln1_g"][l]), _row(W["ln1_b"][l])
    x1hat, rstd1, x1_16 = _mm_res_ln(ycat, W["w_out"][l], xhat, xg, xb, g1, b1, name="mm_out_ln", tm=512)
    gate = _mm([x1_16], [W["w_gate"][l]], name="mm_gate", out_dtype=F32, tm=MM_ROWS_WIDE, tn=D_FF, tk=D_MODEL)
    up = _mm([x1_16], [W["w_up"][l]], name="mm_up", out_dtype=BF16, tm=MM_ROWS_WIDE, tn=D_FF, tk=D_MODEL)
    cw = jnp.pad(W["conv_w"][l], ((0, 5), (0, 0)))
    cb = _row(W["conv_b"][l])
    act = _conv_glu_fwd(gate, up, cw, cb)
    g2, b2 = _row(W["ln2_g"][l]), _row(W["ln2_b"][l])
    x2hat, rstd2, x2_16 = _mm_res_ln(act, W["w_down"][l], x1hat, g1, b1, g2, b2, name="mm_down_ln", tm=256)
    saved = dict(x16=x16, h=h, gains=gains, hT=hT, ka=ka, va=va, oaT=oaT, lse_a=lse_a,
                 lse_b=lse_b, sink_rows=sink_rows, ob_t=ob_t,
                 ga=ga, gb=gb, ycat=ycat, x1hat=x1hat, rstd1=rstd1, x1_16=x1_16, g1=g1, b1=b1, gate=gate, up=up,
                 cw=cw, cb=cb, act=act, x2hat=x2hat, rstd2=rstd2, g2=g2, b2=b2)
    return (x2hat, g2, b2, x2_16), saved


def _layer_bwd(l, S, W, tabs, dz2, dz2_16, stats2, scatter=None):
    cos2, sin2, biasT = tabs
    T = dz2.shape[0]
    G = {}
    G["ln2_g"], G["ln2_b"] = stats2[0], stats2[1]
    G["w_down"] = _mm([S["act"]], [dz2_16], name="dw_down", out_dtype=BF16, trans_a=True, tm=D_FF // 2, tn=D_MODEL, tk=DW_TOKENS)
    dact = _mm([dz2_16], [W["w_down"][l]], name="mm_dact", out_dtype=BF16, trans_b=True, tm=MM_ROWS_WIDE, tn=D_FF,
               tk=D_MODEL)
    dg, du, dconv = _conv_glu_bwd(dact, S["gate"], S["up"], S["cw"], S["cb"])
    G["conv_w"], G["conv_b"] = dconv[0:3], dconv[3]
    G["w_gate"] = _mm([S["x1_16"]], [dg], name="dw_gate", out_dtype=BF16, trans_a=True, tm=D_MODEL, tn=D_FF, tk=DW_TOKENS_WIDE)
    G["w_up"] = _mm([S["x1_16"]], [du], name="dw_up", out_dtype=BF16, trans_a=True, tm=D_MODEL, tn=D_FF, tk=DW_TOKENS_WIDE)
    dx1 = _mm([dg, du], [W["w_gate"][l], W["w_up"][l]], name="mm_dx1", out_dtype=F32, trans_b=True, tm=MM_ROWS_WIDE,
              tn=D_MODEL, tk=D_FF, add=dz2, add_scale=ALPHA)
    dz1, dz1_16, stats1 = _ln_bwd(S["x1hat"], S["rstd1"], S["g1"], S["b1"], name="ln1_bwd", dx=dx1)
    G["ln1_g"], G["ln1_b"] = stats1[0], stats1[1]
    G["w_out"] = _mm([S["ycat"]], [dz1_16], name="dw_out", out_dtype=BF16, trans_a=True, tm=D_MODEL, tn=D_MODEL, tk=DW_TOKENS)
    dycat = _mm([dz1_16], [W["w_out"][l]], name="mm_dycat", out_dtype=F32, trans_b=True, tm=MM_ROWS, tn=D_MODEL,
                tk=D_MODEL)
    doaT, delta, dob_t, dgn = _outnorm_bwd(dycat, S["oaT"], S["ob_t"], S["ga"], S["gb"])
    G["out_norm_a"], G["out_norm_b"] = dgn[0], dgn[1]
    res = _attn_a_bwd(S["ka"], S["va"], S["hT"], doaT.reshape(KV * GQ, HEAD_DIM, T), S["lse_a"],
                      delta.reshape(KV * GQ, 1, T), comm=scatter(G) if scatter is not None else None)
    dkaT, dvaT, dqaT = res[:3]
    dh_rope, dgain = _qk_rope_bwd(S["h"], dqaT.reshape(QW, T), dkaT.reshape(KW, T), S["gains"], cos2, sin2,
                                  name="qk_rope_bwd")
    G["q_norm"], G["k_norm"] = dgain[0, :HEAD_DIM], dgain[1, :HEAD_DIM]
    dqb_t, dkb, dvb, dbiasT, dsk = _attn_b_bwd(S["h"], dob_t, S["ob_t"], S["lse_b"], biasT, S["sink_rows"])
    dh = jnp.concatenate([
        dh_rope, dvaT.transpose(2, 0, 1).reshape(T, KW).astype(BF16), dqb_t,
        dkb[WB:WB + T].astype(BF16), dvb[WB:WB + T].astype(BF16)], axis=1)
    dbias = dbiasT.reshape(KV, WK, GQ, WB).transpose(0, 2, 1, 3)
    G["w_in"] = _mm([S["x16"]], [dh], name="dw_in", out_dtype=BF16, trans_a=True, tm=D_MODEL, tn=IN_COLS, tk=DW_TOKENS)
    dxin = _mm([dh], [W["w_in"][l]], name="mm_dxin", out_dtype=F32, trans_b=True, tm=MM_ROWS, tn=D_MODEL, tk=IN_COLS,
               add=dz1, add_scale=ALPHA)
    return dxin, G, dbias.reshape(KV * GQ, WK * WB), dsk.reshape(KV * GQ, WB), res[3:]


BIG = ("w_in", "w_out", "w_gate", "w_up", "w_down")
COL_SHARDED = ("w_in", "w_gate", "w_up")


def _unshard(name, blocks):
    _, r, c = blocks.shape
    if name in COL_SHARDED:
        return blocks.transpose(1, 0, 2).reshape(r, N_DEV * c)
    return blocks.reshape(N_DEV * r, c)


def _to_owner_blocks(name, full, shard_shape):
    _, r, c = shard_shape
    if name in COL_SHARDED:
        return full.reshape(r, N_DEV, c).transpose(1, 0, 2)
    return full.reshape(N_DEV, r, c)


def _pack_small(vals, tail):
    flat = jnp.concatenate([vals[n].reshape(-1).astype(F32) for n in SMALL_NAMES] + [tail])
    pad = (-flat.shape[0]) % (8 * LANES)
    return jnp.pad(flat, (0, pad)).reshape(-1, LANES)


def _unpack_small(packed, shapes):
    flat = packed.reshape(-1)
    out, off = {}, 0
    for n in SMALL_NAMES:
        size = math.prod(shapes[n])
        out[n] = flat[off:off + size].reshape(shapes[n])
        off += size
    return out, flat[off]


def kernel(x, rel_bias, w_in, q_norm, k_norm, sink, out_norm_a, out_norm_b, w_out, ln1_g, ln1_b, w_gate, w_up, conv_w, conv_b, w_down, ln2_g, ln2_b, loss_target, m_rel_bias, m_w_in, m_q_norm, m_k_norm, m_sink, m_out_norm_a, m_out_norm_b, m_w_out, m_ln1_g, m_ln1_b, m_w_gate, m_w_up, m_conv_w, m_conv_b, m_w_down, m_ln2_g, m_ln2_b, v_rel_bias, v_w_in, v_q_norm, v_k_norm, v_sink, v_out_norm_a, v_out_norm_b, v_w_out, v_ln1_g, v_ln1_b, v_w_gate, v_w_up, v_conv_w, v_conv_b, v_w_down, v_ln2_g, v_ln2_b):
    P = dict(rel_bias=rel_bias, w_in=w_in, q_norm=q_norm, k_norm=k_norm, sink=sink, out_norm_a=out_norm_a,
             out_norm_b=out_norm_b, w_out=w_out, ln1_g=ln1_g, ln1_b=ln1_b, w_gate=w_gate, w_up=w_up, conv_w=conv_w,
             conv_b=conv_b, w_down=w_down, ln2_g=ln2_g, ln2_b=ln2_b)
    M = dict(rel_bias=m_rel_bias, w_in=m_w_in, q_norm=m_q_norm, k_norm=m_k_norm, sink=m_sink, out_norm_a=m_out_norm_a,
             out_norm_b=m_out_norm_b, w_out=m_w_out, ln1_g=m_ln1_g, ln1_b=m_ln1_b, w_gate=m_w_gate, w_up=m_w_up,
             conv_w=m_conv_w, conv_b=m_conv_b, w_down=m_w_down, ln2_g=m_ln2_g, ln2_b=m_ln2_b)
    V = dict(rel_bias=v_rel_bias, w_in=v_w_in, q_norm=v_q_norm, k_norm=v_k_norm, sink=v_sink, out_norm_a=v_out_norm_a,
             out_norm_b=v_out_norm_b, w_out=v_w_out, ln1_g=v_ln1_g, ln1_b=v_ln1_b, w_gate=v_w_gate, w_up=v_w_up,
             conv_w=v_conv_w, conv_b=v_conv_b, w_down=v_w_down, ln2_g=v_ln2_g, ln2_b=v_ln2_b)
    names = list(P)
    T = x.shape[1]
    me = 4 * lax.axis_index("x") + 2 * lax.axis_index("y") + lax.axis_index("c")

    L, taps, fc = conv_w.shape
    W = {n: ([None] * DEPTH if n in BIG else P[n]) for n in names}

    def wire(n, l):
        return P[n][l].astype(BF16)

    def take(n, l, gathered):
        W[n][l] = _unshard(n, gathered)

    take("w_in", 0, _exchange([wire("w_in", 0)], [True], name="gather_w_in0")[0])
    later = [(n, l) for l in range(DEPTH) for n in BIG if (n, l) != ("w_in", 0)]
    cw_shard = conv_w.reshape(-1)
    cw_wire = jnp.pad(cw_shard, (0, (-cw_shard.shape[0]) % LANES)).reshape(-1, LANES)
    gather_rest = _Comm([wire(n, l) for n, l in later] + [cw_wire], [True] * (len(later) + 1))

    def on_gathered(outs):
        for (n, l), g in zip(later, outs):
            take(n, l, g)
        cw_all = outs[-1].reshape(N_DEV, -1)[:, :cw_shard.shape[0]].reshape(N_DEV, L, taps, fc)
        W["conv_w"] = cw_all.transpose(1, 2, 0, 3).reshape(L, taps, N_DEV * fc)

    cos2, sin2 = _rope_tables(T)
    bucket = _window_buckets()
    bias = _bias_table(rel_bias.T, bucket.reshape(1, WB * WK))
    biasT = bias.reshape(KV, GQ, WB, WK).transpose(0, 3, 1, 2).reshape(KV, WK, GQ * WB)
    biasT = _end_tables(biasT)
    tabs = (cos2, sin2, biasT)

    ones, zeros = jnp.ones((1, D_MODEL), F32), jnp.zeros((1, D_MODEL), F32)
    cur = (x[0], ones, zeros, x[0].astype(BF16))
    saved = []
    for l in range(DEPTH):
        cur, S = _layer_fwd(l, cur, W, tabs, comm=gather_rest if l == 0 else None, on_comm=on_gathered)
        saved.append(S)

    def owner_blocks(n, l):
        return _to_owner_blocks(n, grads[l][n], P[n].shape)

    early = ([(n, l) for l in range(1, DEPTH) for n in BIG] + [(n, 0) for n in BIG if n != "w_in"])

    def scatter_early(g0):
        grads[0] = g0
        return _Comm([owner_blocks(n, l) for n, l in early], [False] * len(early))

    grads = [None] * DEPTH
    dbs, dsks = [None] * DEPTH, [None] * DEPTH
    S = saved[-1]
    dz, dz16, stats = _ln_bwd(S["x2hat"], S["rstd2"], S["g2"], S["b2"], name="loss_ln2_bwd", target=loss_target[0])
    loss_part = stats[2, 0:1]
    recv = {}
    for l in reversed(range(DEPTH)):
        S = saved[l]
        dxin, grads[l], dbs[l], dsks[l], got = _layer_bwd(l, S, W, tabs, dz, dz16, stats,
                                                         scatter=scatter_early if l == 0 else None)
        if l == 0:
            recv.update(zip(early, got))
        if l > 0:
            Sp = saved[l - 1]
            dz, dz16, stats = _ln_bwd(Sp["x2hat"], Sp["rstd2"], Sp["g2"], Sp["b2"], name="ln2_bwd", dx=dxin)
    grad_x = dxin[None]

    drb, dsink = _bias_sink_grads(dbs, dsks, bucket.T.reshape(1, WK * WB))
    small_g = {n: jnp.stack([grads[l][n] for l in range(DEPTH)]) for n in SMALL_NAMES if n not in ("rel_bias", "sink")}
    small_g["rel_bias"] = drb.T
    small_g["sink"] = dsink.reshape(DEPTH, KV * GQ)
    recv[("w_in", 0)], small_recv = _exchange([owner_blocks("w_in", 0), _pack_small(small_g, loss_part)], [False, True],
                                              name="scatter_w_in0_gather_small")

    out_g, out_d, out_m, out_v = {}, {}, {}, {}
    for n in BIG:
        shp = P[n].shape
        gparts = jnp.concatenate([recv[(n, l)] for l in range(DEPTH)], axis=1)
        rows, cols = shp[0] * shp[1], shp[2]
        res = _adamw(P[n].reshape(rows, cols), M[n].reshape(rows, cols), V[n].reshape(rows, cols), gparts,
                     name="adamw_" + n, tr=math.gcd(rows, 256))
        out_g[n], out_d[n], out_m[n], out_v[n] = (r.reshape(shp) for r in res)
    full_shapes = {n: W[n].shape for n in SMALL_NAMES}

    def small_state(D):
        vals = {n: D[n] for n in SMALL_NAMES if n != "conv_w"}
        cw = jnp.zeros((L, taps, N_DEV, fc), F32)
        cw = lax.dynamic_update_slice(cw, D["conv_w"].reshape(L, taps, 1, fc), (0, 0, me, 0))
        vals["conv_w"] = cw.reshape(L, taps, N_DEV * fc)
        return _pack_small(vals, jnp.zeros((1,), F32))

    sw, sm, sv = small_state(P), small_state(M), small_state(V)
    res = _adamw(sw, sm, sv, small_recv, name="adamw_small", tr=sw.shape[0])
    loss = _unpack_small(res[0], full_shapes)[1]
    for dst, packed in zip((out_g, out_d, out_m, out_v), res):
        vals, _ = _unpack_small(packed, full_shapes)
        for n in SMALL_NAMES:
            if n == "conv_w":
                sl = lax.dynamic_slice(vals[n].reshape(L, taps, N_DEV, fc), (0, 0, me, 0), (L, taps, 1, fc))
                dst[n] = sl.reshape(L, taps, fc)
            else:
                dst[n] = vals[n]
    return (loss, grad_x, *[out_g[n] for n in names], *[out_d[n] for n in names],
            *[out_m[n] for n in names], *[out_v[n] for n in names])
```

```python
import functools
import math

import jax
import jax.numpy as jnp
from jax import lax
from jax.experimental import pallas as pl
from jax.experimental.pallas import tpu as pltpu

F32 = jnp.float32
BF16 = jnp.bfloat16
MESH = pl.DeviceIdType.MESH

N_DEV = 8
D_MODEL = 1024
DEPTH = 2
HEAD_DIM = 64
KV = 2
GQ = 4
QW = KV * GQ * HEAD_DIM
KW = KV * HEAD_DIM
ROPE_W = QW + KW
IN_COLS = 2 * (QW + 2 * KW)
D_FF = 2816
GRID_W = 64
ROPE_THETA = 10000.0
WINDOW = 128
N_BUCKETS = 32
MAX_DISTANCE = 128
ALPHA = (2.0 * DEPTH) ** 0.25
RMS_EPS = 1e-6
LN_EPS = 1e-5
SCALE = HEAD_DIM ** -0.5
LOG2E = math.log2(math.e)
LN2 = math.log(2.0)
NEG = -1e30
ONES_ROWS = 16

ADAM_LR = 0.001
ADAM_B1 = 0.9
ADAM_B2 = 0.999
ADAM_EPS = 1e-08
ADAM_WD = 0.01
ADAM_STEP = 10

LANES = 128
MM_ROWS = 1024
MM_ROWS_WIDE = 512
DW_TOKENS = 1024
VMEM_CAP = 60 * 1024 * 1024
SMALL_NAMES = ("rel_bias", "q_norm", "k_norm", "sink", "out_norm_a", "out_norm_b", "ln1_g", "ln1_b",
               "conv_b", "ln2_g", "ln2_b", "conv_w")


def _params(sem, est_bytes):
    limit = int(min(VMEM_CAP, est_bytes + (8 << 20)))
    return pltpu.CompilerParams(dimension_semantics=sem, vmem_limit_bytes=limit)


def _nbytes(shape, dtype):
    return math.prod(shape) * jnp.dtype(dtype).itemsize


class _Comm:
    def __init__(self, parts, gathers):
        self.parts, self.gathers, self.n = list(parts), list(gathers), len(parts)
        hbm = pl.BlockSpec(memory_space=pltpu.HBM)
        self.in_specs = [hbm] * self.n
        self.out_specs = [hbm] * self.n
        self.out_shape = [jax.ShapeDtypeStruct((N_DEV,) + tuple(p.shape if g else p.shape[1:]), p.dtype)
                          for p, g in zip(self.parts, self.gathers)]
        self.scratch = [pltpu.SemaphoreType.DMA((self.n * (N_DEV - 1),)), pltpu.SemaphoreType.DMA((self.n * (N_DEV - 1),)),
                        pltpu.SemaphoreType.DMA((self.n,))]

    def bind(self, ins, outs, sems):
        send_sems, recv_sems, local_sems = sems
        gathers, n = self.gathers, self.n
        me = 4 * lax.axis_index("x") + 2 * lax.axis_index("y") + lax.axis_index("c")

        def src(k, j):
            return ins[k] if gathers[k] else ins[k].at[j]

        def copy(k, d, peer, lands_in):
            return pltpu.make_async_remote_copy(
                src_ref=src(k, peer), dst_ref=outs[k].at[lands_in],
                send_sem=send_sems.at[k * (N_DEV - 1) + d - 1], recv_sem=recv_sems.at[k * (N_DEV - 1) + d - 1],
                device_id=(peer // 4, lax.rem(peer // 2, 2), lax.rem(peer, 2)), device_id_type=MESH)

        def send(k, d):
            return copy(k, d, lax.rem(me + d, N_DEV), me)

        def arrival(k, d):
            frm = lax.rem(me + N_DEV - d, N_DEV)
            return copy(k, d, frm, frm)

        def local(k):
            return pltpu.make_async_copy(src(k, me), outs[k].at[me], local_sems.at[k])

        def start():
            for k in range(n):
                local(k).start()
                for d in range(1, N_DEV):
                    send(k, d).start()

        def finish():
            for k in range(n):
                for d in range(1, N_DEV):
                    arrival(k, d).wait_recv()
            for k in range(n):
                for d in range(1, N_DEV):
                    send(k, d).wait_send()
                local(k).wait()

        return start, finish


def _host_comm(comm, refs, n_in, n_out, n_scratch, grid):
    n = comm.n if comm is not None else 0
    own_in, cin = refs[:n_in], refs[n_in:n_in + n]
    own_out, cout = refs[n_in + n:n_in + n + n_out], refs[n_in + n + n_out:n_in + 2 * n + n_out]
    base = n_in + 2 * n + n_out
    own_scratch, sems = refs[base:base + n_scratch], refs[base + n_scratch:]
    own = tuple(own_in) + tuple(own_out) + tuple(own_scratch)
    if comm is None:
        return own, lambda: None, lambda: None
    start, finish = comm.bind(cin, cout, sems)
    first = last = None
    for ax, size in enumerate(grid):
        pid = pl.program_id(ax)
        first = (pid == 0) if first is None else first & (pid == 0)
        last = (pid == size - 1) if last is None else last & (pid == size - 1)
    return own, lambda: pl.when(first)(start), lambda: pl.when(last)(finish)


def _exchange(parts, gathers, name):
    comm = _Comm(parts, gathers)
    n = comm.n

    def body(*refs):
        start, finish = comm.bind(refs[:n], refs[n:2 * n], refs[2 * n:])
        start()
        finish()

    return pl.pallas_call(body, name=name, out_shape=comm.out_shape, in_specs=comm.in_specs, out_specs=comm.out_specs,
                          scratch_shapes=comm.scratch)(*comm.parts)


def _mm(a_list, b_list, *, name, out_dtype, tm, tn, tk, trans_a=False, trans_b=False, add=None, add_scale=1.0,
        comm=None):
    assert not (trans_a and trans_b)
    na = len(a_list)
    if trans_a:
        K, M = a_list[0].shape
    else:
        M, K = a_list[0].shape
    N = b_list[0].shape[0 if trans_b else 1]
    tm, tn, tk = min(tm, M), min(tn, N), min(tk, K)
    assert M % tm == 0 and N % tn == 0 and K % tk == 0, (name, M, N, K, tm, tn, tk)
    nk = K // tk
    dims = (((0,), (0,)), ((), ())) if trans_a else (((1,), (1 if trans_b else 0,)), ((), ()))

    n_in = 2 * na + (add is not None)
    grid = (M // tm, N // tn, nk)

    def body(*refs):
        refs, comm_start, comm_finish = _host_comm(comm, refs, n_in, 1, int(nk > 1), grid)
        a_refs, b_refs = refs[:na], refs[na:2 * na]
        add_ref = refs[2 * na] if add is not None else None
        o_ref = refs[n_in]
        k = pl.program_id(2)
        comm_start()

        part = None
        for a_ref, b_ref in zip(a_refs, b_refs):
            prod = lax.dot_general(a_ref[...].astype(BF16), b_ref[...].astype(BF16), dims,
                                   preferred_element_type=F32)
            part = prod if part is None else part + prod

        def finish(res):
            if add_ref is not None:
                res = res + add_scale * add_ref[...]
            o_ref[...] = res.astype(o_ref.dtype)

        if nk == 1:
            finish(part)
        else:
            acc_ref = refs[-1]

            @pl.when(k == 0)
            def _():
                acc_ref[...] = part

            @pl.when(k > 0)
            def _():
                acc_ref[...] += part

            @pl.when(k == nk - 1)
            def _():
                finish(acc_ref[...])

        comm_finish()

    if trans_a:
        a_spec = pl.BlockSpec((tk, tm), lambda i, j, k: (k, i))
    else:
        a_spec = pl.BlockSpec((tm, tk), lambda i, j, k: (i, k))
    if trans_b:
        b_spec = pl.BlockSpec((tn, tk), lambda i, j, k: (j, k))
    else:
        b_spec = pl.BlockSpec((tk, tn), lambda i, j, k: (k, j))
    o_spec = pl.BlockSpec((tm, tn), lambda i, j, k: (i, j))
    in_specs = [a_spec] * na + [b_spec] * na + ([o_spec] if add is not None else [])
    est = (2 * na * (_nbytes((tm, tk), a_list[0].dtype) + _nbytes((tk, tn), b_list[0].dtype))
           + na * (_nbytes((tm, tk), BF16) + _nbytes((tk, tn), BF16))
           + 2 * _nbytes((tm, tn), out_dtype) + 3 * _nbytes((tm, tn), F32)
           + (2 * _nbytes((tm, tn), F32) if add is not None else 0))
    args = list(a_list) + list(b_list) + ([add] if add is not None else [])
    scratch = [pltpu.VMEM((tm, tn), F32)] if nk > 1 else []
    out = jax.ShapeDtypeStruct((M, N), out_dtype)
    if comm is None:
        return pl.pallas_call(
            body, name=name, grid=grid, out_shape=out, in_specs=in_specs, out_specs=o_spec, scratch_shapes=scratch,
            compiler_params=_params(("parallel", "parallel", "arbitrary"), est),
        )(*args)
    return pl.pallas_call(
        body, name=name, grid=grid, out_shape=[out] + comm.out_shape,
        in_specs=in_specs + comm.in_specs, out_specs=[o_spec] + comm.out_specs, scratch_shapes=scratch + comm.scratch,
        compiler_params=_params(("arbitrary",) * 3, est),
    )(*args, *comm.parts)


def _mm_res_ln(a, w, res_hat, res_g, res_b, ln_g, ln_b, *, name, tm):
    T, K = a.shape
    D = w.shape[1]
    tm = min(tm, T)

    def body(a_ref, w_ref, rh_ref, rg_ref, rb_ref, g_ref, b_ref, xhat_ref, rstd_ref, xb_ref):
        branch = jnp.dot(a_ref[...].astype(BF16), w_ref[...], preferred_element_type=F32)
        z = ALPHA * (rh_ref[...] * rg_ref[...] + rb_ref[...]) + branch
        mu = jnp.mean(z, axis=1, keepdims=True)
        zc = z - mu
        var = jnp.mean(zc * zc, axis=1, keepdims=True)
        rstd = lax.rsqrt(var + LN_EPS)
        xhat = zc * rstd
        xhat_ref[...] = xhat
        rstd_ref[...] = rstd
        xb_ref[...] = (xhat * g_ref[...] + b_ref[...]).astype(BF16)

    row = pl.BlockSpec((tm, D), lambda i: (i, 0))
    vec = pl.BlockSpec((1, D), lambda i: (0, 0))
    est = (2 * (_nbytes((tm, K), a.dtype) + _nbytes((K, D), BF16)) + 4 * _nbytes((tm, D), F32) * 2
           + 6 * _nbytes((tm, D), F32))
    return pl.pallas_call(
        body, name=name, grid=(T // tm,),
        out_shape=(jax.ShapeDtypeStruct((T, D), F32), jax.ShapeDtypeStruct((T, 1), F32),
                   jax.ShapeDtypeStruct((T, D), BF16)),
        in_specs=[pl.BlockSpec((tm, K), lambda i: (i, 0)), pl.BlockSpec((K, D), lambda i: (0, 0)), row, vec, vec, vec, vec],
        out_specs=(row, pl.BlockSpec((tm, 1), lambda i: (i, 0)), row),
        compiler_params=_params(("parallel",), est),
    )(a, w, res_hat, res_g, res_b, ln_g, ln_b)


def _ln_bwd(xhat, rstd, ln_g, ln_b, *, name, dx=None, target=None, tm=256):
    T, D = xhat.shape
    tm = min(tm, T)
    head = target is not None

    def body(xhat_ref, rstd_ref, g_ref, b_ref, d_ref, dz_ref, dzb_ref, st_ref):
        i = pl.program_id(0)

        @pl.when(i == 0)
        def _():
            st_ref[...] = jnp.zeros_like(st_ref)

        xh = xhat_ref[...]
        g = g_ref[...]
        if head:
            err = (xh * g + b_ref[...]) - d_ref[...]
            dxv = err * (1.0 / D)
            st_ref[2:3, :] += 0.5 * jnp.sum(jnp.sum(err * err, axis=1, keepdims=True) * (1.0 / D), axis=0, keepdims=True)
        else:
            dxv = d_ref[...]
        st_ref[0:1, :] += jnp.sum(dxv * xh, axis=0, keepdims=True)
        st_ref[1:2, :] += jnp.sum(dxv, axis=0, keepdims=True)
        dxh = dxv * g
        m1 = jnp.mean(dxh, axis=1, keepdims=True)
        m2 = jnp.mean(dxh * xh, axis=1, keepdims=True)
        dz = rstd_ref[...] * (dxh - m1 - xh * m2)
        dz_ref[...] = dz
        dzb_ref[...] = dz.astype(BF16)

    row = pl.BlockSpec((tm, D), lambda i: (i, 0))
    vec = pl.BlockSpec((1, D), lambda i: (0, 0))
    est = 2 * 4 * _nbytes((tm, D), F32) + 6 * _nbytes((tm, D), F32)
    return pl.pallas_call(
        body, name=name, grid=(T // tm,),
        out_shape=(jax.ShapeDtypeStruct((T, D), F32), jax.ShapeDtypeStruct((T, D), BF16),
                   jax.ShapeDtypeStruct((8, D), F32)),
        in_specs=[row, pl.BlockSpec((tm, 1), lambda i: (i, 0)), vec, vec, row],
        out_specs=(row, row, pl.BlockSpec((8, D), lambda i: (0, 0))),
        compiler_params=_params(("arbitrary",), est),
    )(xhat, rstd, ln_g, ln_b, target if head else dx)


def _pair_swap(v, even):
    return jnp.where(even, pltpu.roll(v, LANES - 1, 1), pltpu.roll(v, 1, 1))


def _half_sums(v, lo):
    s_lo = jnp.sum(jnp.where(lo, v, 0.0), axis=1, keepdims=True)
    s_hi = jnp.sum(jnp.where(lo, 0.0, v), axis=1, keepdims=True)
    return jnp.where(lo, s_lo, s_hi)


A_COLS = ROPE_W + KW
A_HEADS = A_COLS // HEAD_DIM
A_K0, A_V0 = KV * GQ, KV * GQ + KV


def _qk_rope_fwd(h, gains, cos2, sin2, *, name, tm=256):
    T = h.shape[0]
    tm = min(tm, T)
    nch = A_COLS // LANES

    def body(h_ref, g_ref, c_ref, s_ref, oT_ref, kv_ref):
        lane = lax.broadcasted_iota(jnp.int32, (tm, LANES), 1)
        lo, even = lane < HEAD_DIM, lane % 2 == 0
        c, s = c_ref[...], s_ref[...]
        for j in range(nch):
            x = h_ref[:, j * LANES:(j + 1) * LANES]
            isq, isv = j < QW // LANES, j == nch - 1
            if isv:
                out = x
            else:
                g = g_ref[0:1, :] if isq else g_ref[1:2, :]
                r = lax.rsqrt(_half_sums(x * x, lo) * (1.0 / HEAD_DIM) + RMS_EPS)
                nrm = x * r * g
                out = nrm * c + _pair_swap(nrm, even) * s
            if isq:
                out = out * (SCALE * LOG2E)
            else:
                kv_ref[:, (j - QW // LANES) * LANES:(j - QW // LANES + 1) * LANES] = out.astype(BF16)
            oT_ref[j * LANES:(j + 1) * LANES, :] = out.T.astype(BF16)

    est = 2 * (_nbytes((tm, A_COLS), F32) + 2 * _nbytes((tm, A_COLS), BF16) + 2 * _nbytes((tm, LANES), F32)) + (4 << 20)
    return pl.pallas_call(
        body, name=name, grid=(T // tm,),
        out_shape=(jax.ShapeDtypeStruct((A_COLS, T), BF16), jax.ShapeDtypeStruct((T, 2 * KW), BF16)),
        in_specs=[pl.BlockSpec((tm, A_COLS), lambda i: (i, 0)), pl.BlockSpec((8, LANES), lambda i: (0, 0)),
                  pl.BlockSpec((tm, LANES), lambda i: (i, 0)), pl.BlockSpec((tm, LANES), lambda i: (i, 0))],
        out_specs=(pl.BlockSpec((A_COLS, tm), lambda i: (0, i)), pl.BlockSpec((tm, 2 * KW), lambda i: (i, 0))),
        compiler_params=_params(("parallel",), est),
    )(h, gains, cos2, sin2)


def _qk_rope_bwd(h, dqT, dkT, gains, cos2, sin2, *, name, tm=256):
    T = h.shape[0]
    tm = min(tm, T)
    nch = ROPE_W // LANES

    def body(h_ref, dq_ref, dk_ref, g_ref, c_ref, s_ref, dh_ref, dg_ref):
        i = pl.program_id(0)

        @pl.when(i == 0)
        def _():
            dg_ref[...] = jnp.zeros_like(dg_ref)

        lane = lax.broadcasted_iota(jnp.int32, (tm, LANES), 1)
        lo, even = lane < HEAD_DIM, lane % 2 == 0
        c, s = c_ref[...], s_ref[...]
        acc = [None, None]
        for j in range(nch):
            x = h_ref[:, j * LANES:(j + 1) * LANES]
            isq = j < QW // LANES
            g = g_ref[0:1, :] if isq else g_ref[1:2, :]
            d = dq_ref[j * LANES:(j + 1) * LANES, :].T * SCALE if isq else dk_ref[...].T
            r = lax.rsqrt(_half_sums(x * x, lo) * (1.0 / HEAD_DIM) + RMS_EPS)
            dn = d * c + _pair_swap(d * s, even)
            xr = x * r
            part = jnp.sum(dn * xr, axis=0, keepdims=True)
            acc[0 if isq else 1] = part if acc[0 if isq else 1] is None else acc[0 if isq else 1] + part
            dng = dn * g
            dx = r * dng - xr * (r * r) * (_half_sums(dng * x, lo) * (1.0 / HEAD_DIM))
            dh_ref[:, j * LANES:(j + 1) * LANES] = dx.astype(BF16)
        for row in range(2):
            folded = acc[row] + pltpu.roll(acc[row], HEAD_DIM, 1)
            dg_ref[row:row + 1, :] += folded

    est = 2 * (2 * _nbytes((tm, ROPE_W), F32) + _nbytes((tm, ROPE_W), BF16) + 2 * _nbytes((tm, LANES), F32)) + (4 << 20)
    return pl.pallas_call(
        body, name=name, grid=(T // tm,),
        out_shape=(jax.ShapeDtypeStruct((T, ROPE_W), BF16), jax.ShapeDtypeStruct((8, LANES), F32)),
        in_specs=[pl.BlockSpec((tm, ROPE_W), lambda i: (i, 0)), pl.BlockSpec((QW, tm), lambda i: (0, i)),
                  pl.BlockSpec((KW, tm), lambda i: (0, i)), pl.BlockSpec((8, LANES), lambda i: (0, 0)),
                  pl.BlockSpec((tm, LANES), lambda i: (i, 0)), pl.BlockSpec((tm, LANES), lambda i: (i, 0))],
        out_specs=(pl.BlockSpec((tm, ROPE_W), lambda i: (i, 0)), pl.BlockSpec((8, LANES), lambda i: (0, 0))),
        compiler_params=_params(("arbitrary",), est),
    )(h, dqT, dkT, gains, cos2, sin2)


def _attn_a_fwd(k, hT, *, comm=None, tq=4096, tk=2048, cq=512):
    G, T, HD = k.shape
    HE = HD + ONES_ROWS
    tq, tk = min(tq, T), min(tk, T)
    cq = min(cq, tq)
    nk, nt = T // tk, T // tq
    grid = (G, GQ * nt, nk)

    def body(*refs):
        (k_ref, qT_ref, v_ref, oT_ref, lse_ref, m_sc, acc_sc), comm_start, comm_finish = _host_comm(
            comm, refs, 3, 2, 2, grid)
        kv = pl.program_id(2)
        comm_start()
        v1T = jnp.concatenate([v_ref[...], jnp.ones((ONES_ROWS, tk), BF16)], axis=0)

        @pl.when(kv == 0)
        def _():
            m_sc[...] = jnp.full_like(m_sc, NEG)
            acc_sc[...] = jnp.zeros_like(acc_sc)

        def scores(c):
            return jnp.dot(k_ref[...], qT_ref[:, c * cq:(c + 1) * cq], preferred_element_type=F32)

        nc = tq // cq
        ahead = scores(0)
        for c in range(nc):
            cols = slice(c * cq, (c + 1) * cq)
            sT = ahead
            if c + 1 < nc:
                ahead = scores(c + 1)
            m_prev = m_sc[:, cols]
            m_new = jnp.maximum(m_prev, jnp.max(sT, axis=0, keepdims=True))
            pT = jnp.exp2(sT - m_new).astype(BF16)
            acc_sc[:, cols] = (jnp.exp2(m_prev - m_new) * acc_sc[:, cols]
                               + jnp.dot(v1T, pT, preferred_element_type=F32))
            m_sc[:, cols] = m_new

        @pl.when(kv == nk - 1)
        def _():
            l = acc_sc[HD:HD + 1, :]
            oT_ref[...] = acc_sc[0:HD, :] / l
            lse_ref[...] = m_sc[...] + jnp.log2(l)

        comm_finish()

    qtr = pl.BlockSpec((None, HD, tq), lambda g, i, j: (g * GQ + i // nt, 0, i % nt))
    qvec = pl.BlockSpec((None, 1, tq), lambda g, i, j: (g * GQ + i // nt, 0, i % nt))
    est = 6 * _nbytes((cq, tk), F32) + (8 << 20)
    hosted = comm is not None
    return pl.pallas_call(
        body, name="attn_a_fwd_comm" if hosted else "attn_a_fwd", grid=grid,
        out_shape=[jax.ShapeDtypeStruct((G * GQ, HD, T), F32), jax.ShapeDtypeStruct((G * GQ, 1, T), F32)]
        + (comm.out_shape if hosted else []),
        in_specs=[pl.BlockSpec((None, tk, HD), lambda g, i, j: (g, j, 0)), qtr,
                  pl.BlockSpec((None, HD, tk), lambda g, i, j: (A_V0 + g, 0, j))] + (comm.in_specs if hosted else []),
        out_specs=[qtr, qvec] + (comm.out_specs if hosted else []),
        scratch_shapes=[pltpu.VMEM((1, tq), F32), pltpu.VMEM((HE, tq), F32)] + (comm.scratch if hosted else []),
        compiler_params=_params(("arbitrary",) * 3 if hosted else ("parallel", "parallel", "arbitrary"), est),
    )(k, hT, hT, *(comm.parts if hosted else []))


def _attn_a_bwd(k, v, hT, doT, lse_row, delta_row, *, comm=None, tq=4096, tk=1024, cq=256):
    G, T, HD = k.shape
    tq, tk = min(tq, T), min(tk, T)
    cq = min(cq, tq)
    nqt = T // tq
    nq, nc = GQ * nqt, tq // cq
    nt = (((1,), (1,)), ((), ()))

    grid = (G, T // tk, nq)

    def body(*refs):
        (k_ref, v_ref, kT_ref, qT_ref, doT_ref, lse_ref, dl_ref, dkT_ref, dvT_ref, dqT_ref, dk_sc, dv_sc), \
            comm_start, comm_finish = _host_comm(comm, refs, 7, 3, 2, grid)
        j, i = pl.program_id(1), pl.program_id(2)
        comm_start()

        @pl.when((j == 0) & (i == 0))
        def _():
            dqT_ref[...] = jnp.zeros_like(dqT_ref)

        @pl.when(i == 0)
        def _():
            dk_sc[...] = jnp.zeros_like(dk_sc)
            dv_sc[...] = jnp.zeros_like(dv_sc)

        def scores(c):
            cols = slice(c * cq, (c + 1) * cq)
            return (jnp.dot(k_ref[...], qT_ref[:, cols], preferred_element_type=F32),
                    jnp.dot(v_ref[...], doT_ref[:, cols], preferred_element_type=F32))

        ahead = scores(0)
        dk_part = dv_part = None
        for c in range(nc):
            cols = slice(c * cq, (c + 1) * cq)
            sT, dpT = ahead
            if c + 1 < nc:
                ahead = scores(c + 1)
            pT = jnp.exp2(sT - lse_ref[:, cols])
            dsT = (pT * (dpT - dl_ref[:, cols])).astype(BF16)
            dv_c = lax.dot_general(doT_ref[:, cols], pT.astype(BF16), nt, preferred_element_type=F32)
            dk_c = lax.dot_general(qT_ref[:, cols], dsT, nt, preferred_element_type=F32)
            dv_part = dv_c if dv_part is None else dv_part + dv_c
            dk_part = dk_c if dk_part is None else dk_part + dk_c
            out_cols = pl.ds(pl.multiple_of((i % nqt) * tq + c * cq, cq), cq)
            dqT_ref[i // nqt, :, out_cols] += jnp.dot(kT_ref[...], dsT, preferred_element_type=F32)
        dk_sc[...] += dk_part
        dv_sc[...] += dv_part

        @pl.when(i == nq - 1)
        def _():
            dkT_ref[...] = dk_sc[...] * LN2
            dvT_ref[...] = dv_sc[...]

        comm_finish()

    krow = pl.BlockSpec((None, tk, HD), lambda g, j, i: (g, j, 0))
    ktr = pl.BlockSpec((None, HD, tk), lambda g, j, i: (g, 0, j))
    ktr_h = pl.BlockSpec((None, HD, tk), lambda g, j, i: (A_K0 + g, 0, j))
    qtr = pl.BlockSpec((None, HD, tq), lambda g, j, i: (g * GQ + i // nqt, 0, i % nqt))
    qvec = pl.BlockSpec((None, 1, tq), lambda g, j, i: (g * GQ + i // nqt, 0, i % nqt))
    whole = pl.BlockSpec((GQ, HD, T), lambda g, j, i: (g, 0, 0))
    est = 8 * _nbytes((cq, tk), F32) + 2 * _nbytes((GQ, HD, T), F32) + (8 << 20)
    hosted = comm is not None
    return pl.pallas_call(
        body, name="attn_a_bwd_comm" if hosted else "attn_a_bwd", grid=grid,
        out_shape=[jax.ShapeDtypeStruct((G, HD, T), F32), jax.ShapeDtypeStruct((G, HD, T), F32),
                   jax.ShapeDtypeStruct((G * GQ, HD, T), F32)] + (comm.out_shape if hosted else []),
        in_specs=[krow, krow, ktr_h, qtr, qtr, qvec, qvec] + (comm.in_specs if hosted else []),
        out_specs=[ktr, ktr, whole] + (comm.out_specs if hosted else []),
        scratch_shapes=[pltpu.VMEM((HD, tk), F32), pltpu.VMEM((HD, tk), F32)] + (comm.scratch if hosted else []),
        compiler_params=_params(("arbitrary", "arbitrary", "arbitrary"), est),
    )(k, v, hT, hT, doT, lse_row, delta_row, *(comm.parts if hosted else []))


WB = WINDOW
WK = 3 * WINDOW


QB_COL0 = (ROPE_W + KW) // (GQ * HEAD_DIM)
KB_COL = (ROPE_W + KW + QW) // KW
GW = GQ * HEAD_DIM


WSTEP = 2


def _win_in_specs(T):
    nb = T // WB
    assert nb % WSTEP == 0
    ns = nb // WSTEP
    q = [pl.BlockSpec((WSTEP * WB, GW), functools.partial(lambda n, g: (n, QB_COL0 + g), g=g)) for g in range(KV)]
    kv = [pl.BlockSpec((WB, KW), functools.partial(lambda n, o, c: (jnp.clip(WSTEP * n + o, 0, nb - 1), c), o=o, c=c))
          for c in (KB_COL, KB_COL + 1) for o in range(-1, WSTEP + 1)]
    bias = [pl.BlockSpec((None, KV, WK, GQ * WB), lambda n: (jnp.where(n == 0, 0, 1), 0, 0, 0)),
            pl.BlockSpec((None, KV, WK, GQ * WB), lambda n: (jnp.where(n == ns - 1, 2, 1), 0, 0, 0))]
    return ns, q + kv, bias


def _end_tables(biasT):
    key = lax.broadcasted_iota(jnp.int32, biasT.shape, 1)
    return jnp.stack([jnp.where(key < WB, NEG, biasT), biasT, jnp.where(key >= 2 * WB, NEG, biasT)])


def _heads_to_lanes(t):
    return jnp.concatenate([t[i * HEAD_DIM:(i + 1) * HEAD_DIM] for i in range(GQ)], axis=1)


def _lanes_to_heads(t):
    return jnp.concatenate([t[:, i * WB:(i + 1) * WB] for i in range(GQ)], axis=0)


def _attn_b_fwd(h, biasT, sink_rows):
    T = h.shape[0]
    ns, in_specs, bias_specs = _win_in_specs(T)
    nkv = WSTEP + 2

    def body(*refs):
        q_refs, k_refs, v_refs = refs[:KV], refs[KV:KV + nkv], refs[KV + nkv:KV + 2 * nkv]
        b_refs, sk_ref, o_ref, lse_ref = refs[KV + 2 * nkv:KV + 2 * nkv + WSTEP], *refs[KV + 2 * nkv + WSTEP:]
        ks, vs = [r[...] for r in k_refs], [r[...] for r in v_refs]
        outs = []
        for b, b_ref in enumerate(b_refs):
            rows = slice(b * WB, (b + 1) * WB)
            kwin = jnp.concatenate(ks[b:b + 3], axis=0)
            vT = jnp.concatenate(vs[b:b + 3], axis=0).T
            qT = [_heads_to_lanes((q[rows, :] * SCALE).T).astype(BF16) for q in q_refs]
            sT = [jnp.dot(kwin[:, g * HEAD_DIM:(g + 1) * HEAD_DIM].astype(BF16), qT[g], preferred_element_type=F32)
                  for g in range(KV)]
            oT = []
            for g in range(KV):
                s = sT[g] + b_ref[g]
                sk = sk_ref[g]
                m = jnp.maximum(jnp.max(s, axis=0, keepdims=True), sk)
                p = jnp.exp(s - m)
                den = jnp.sum(p, axis=0, keepdims=True) + jnp.exp(sk - m)
                o = jnp.dot(vT[g * HEAD_DIM:(g + 1) * HEAD_DIM].astype(BF16), p.astype(BF16),
                            preferred_element_type=F32) / den
                lse_ref[b, g] = m + jnp.log(den)
                oT.append(_lanes_to_heads(o))
            outs.append(jnp.concatenate(oT, axis=0).T)
        o_ref[...] = jnp.concatenate(outs, axis=0)

    whole = lambda *shape: pl.BlockSpec(shape, lambda n: (0,) * len(shape))
    return pl.pallas_call(
        body, name="attn_b_fwd", grid=(ns,),
        out_shape=(jax.ShapeDtypeStruct((T, QW), F32), jax.ShapeDtypeStruct((ns * WSTEP, KV, 1, GQ * WB), F32)),
        in_specs=in_specs + bias_specs + [whole(KV, 1, GQ * WB)],
        out_specs=(pl.BlockSpec((WSTEP * WB, QW), lambda n: (n, 0)),
                   pl.BlockSpec((WSTEP, KV, 1, GQ * WB), lambda n: (n, 0, 0, 0))),
        compiler_params=_params(("parallel",), 32 << 20),
    )(*([h] * (KV + 2 * nkv)), *([biasT] * WSTEP), sink_rows)


def _attn_b_bwd(h, do, o, lse, biasT, sink_rows):
    T = h.shape[0]
    ns, in_specs, bias_specs = _win_in_specs(T)
    nkv = WSTEP + 2
    Tp = T + 2 * WB
    nt = (((1,), (1,)), ((), ()))

    def body(*refs):
        q_refs, k_refs, v_refs = refs[:KV], refs[KV:KV + nkv], refs[KV + nkv:KV + 2 * nkv]
        at = KV + 2 * nkv
        do_ref, o_ref, lse_ref = refs[at:at + 3]
        b_refs, sk_ref = refs[at + 3:at + 3 + WSTEP], refs[at + 3 + WSTEP]
        dq_ref, dk_ref, dv_ref, db_ref, dsk_ref = refs[at + 4 + WSTEP:]
        n = pl.program_id(0)

        @pl.when(n == 0)
        def _():
            dk_ref[...] = jnp.zeros_like(dk_ref)
            dv_ref[...] = jnp.zeros_like(dv_ref)
            db_ref[...] = jnp.zeros_like(db_ref)
            dsk_ref[...] = jnp.zeros_like(dsk_ref)

        ks, vs = [r[...] for r in k_refs], [r[...] for r in v_refs]
        for b, b_ref in enumerate(b_refs):
            rows = slice(b * WB, (b + 1) * WB)
            kwin = jnp.concatenate(ks[b:b + 3], axis=0)
            vwin = jnp.concatenate(vs[b:b + 3], axis=0)
            kT = kwin.T
            doT_all, oT_all = do_ref[rows, :].T, o_ref[rows, :].T
            qT, doT, delta, sT, dpT = [], [], [], [], []
            for g, q in enumerate(q_refs):
                hd = slice(g * HEAD_DIM, (g + 1) * HEAD_DIM)
                qT.append(_heads_to_lanes((q[rows, :] * SCALE).T).astype(BF16))
                d = _heads_to_lanes(doT_all[g * GW:(g + 1) * GW])
                delta.append(jnp.sum(d * _heads_to_lanes(oT_all[g * GW:(g + 1) * GW]), axis=0, keepdims=True))
                doT.append(d.astype(BF16))
                sT.append(jnp.dot(kwin[:, hd].astype(BF16), qT[g], preferred_element_type=F32))
                dpT.append(jnp.dot(vwin[:, hd].astype(BF16), doT[g], preferred_element_type=F32))
            dq, dk, dv = [], [], []
            for g in range(KV):
                lse_g = lse_ref[b, g]
                p = jnp.exp(sT[g] + b_ref[g] - lse_g)
                ds = p * (dpT[g] - delta[g])
                db_ref[g] += ds
                dsk_ref[g] -= jnp.exp(sk_ref[g] - lse_g) * delta[g]
                dsb = ds.astype(BF16)
                dqT = jnp.dot(kT[g * HEAD_DIM:(g + 1) * HEAD_DIM].astype(BF16), dsb, preferred_element_type=F32)
                dq.append(_lanes_to_heads(dqT))
                dk.append(lax.dot_general(dsb, qT[g], nt, preferred_element_type=F32))
                dv.append(lax.dot_general(p.astype(BF16), doT[g], nt, preferred_element_type=F32))
            dq_ref[rows, :] = (jnp.concatenate(dq, axis=0).T * SCALE).astype(BF16)
            win = pl.ds(pl.multiple_of((WSTEP * n + b) * WB, WB), WK)
            dk_ref[win, :] += jnp.concatenate(dk, axis=1)
            dv_ref[win, :] += jnp.concatenate(dv, axis=1)

    whole = lambda *shape: pl.BlockSpec(shape, lambda n: (0,) * len(shape))
    tok = pl.BlockSpec((WSTEP * WB, QW), lambda n: (n, 0))
    return pl.pallas_call(
        body, name="attn_b_bwd", grid=(ns,),
        out_shape=(jax.ShapeDtypeStruct((T, QW), BF16),
                   jax.ShapeDtypeStruct((Tp, KW), F32), jax.ShapeDtypeStruct((Tp, KW), F32),
                   jax.ShapeDtypeStruct((KV, WK, GQ * WB), F32), jax.ShapeDtypeStruct((KV, 1, GQ * WB), F32)),
        in_specs=in_specs + [tok, tok, pl.BlockSpec((WSTEP, KV, 1, GQ * WB), lambda n: (n, 0, 0, 0))]
        + bias_specs + [whole(KV, 1, GQ * WB)],
        out_specs=(tok, whole(Tp, KW), whole(Tp, KW), whole(KV, WK, GQ * WB), whole(KV, 1, GQ * WB)),
        compiler_params=_params(("arbitrary",), 48 << 20),
    )(*([h] * (KV + 2 * nkv)), do, o, lse, *([biasT] * WSTEP), sink_rows)


def _bias_table(rel_bias_t, bucket):
    nh, n = rel_bias_t.shape[0], bucket.shape[1]

    def body(rb_ref, bk_ref, o_ref):
        bk = bk_ref[...]
        out = jnp.full((nh, n), NEG, F32)
        for b in range(N_BUCKETS):
            out = jnp.where(bk == b, rb_ref[:, b:b + 1], out)
        o_ref[...] = out

    return pl.pallas_call(
        body, name="bias_table", out_shape=jax.ShapeDtypeStruct((nh, n), F32),
        compiler_params=pltpu.CompilerParams(vmem_limit_bytes=32 << 20),
    )(rel_bias_t, bucket)


def _bias_sink_grads(db_list, dsk_list, bucket):
    L = len(db_list)

    def body(*refs):
        db_refs, dsk_refs, bk_ref = refs[:L], refs[L:2 * L], refs[2 * L]
        drb_ref, dsink_ref = refs[2 * L + 1], refs[2 * L + 2]
        tot = db_refs[0][...]
        for r in db_refs[1:]:
            tot = tot + r[...]
        bk = bk_ref[...]
        lane = lax.broadcasted_iota(jnp.int32, (2 * GQ, N_BUCKETS), 1)
        out = jnp.zeros((2 * GQ, N_BUCKETS), F32)
        for b in range(N_BUCKETS):
            sb = jnp.sum(jnp.where(bk == b, tot, 0.0), axis=1, keepdims=True)
            out = jnp.where(lane == b, sb, out)
        drb_ref[...] = out
        for l in range(L):
            dsink_ref[l] = jnp.sum(dsk_refs[l][...], axis=1, keepdims=True)

    return pl.pallas_call(
        body, name="bias_sink_grads",
        out_shape=(jax.ShapeDtypeStruct((2 * GQ, N_BUCKETS), F32), jax.ShapeDtypeStruct((L, 2 * GQ, 1), F32)),
        compiler_params=pltpu.CompilerParams(vmem_limit_bytes=32 << 20),
    )(*db_list, *dsk_list, bucket)


def _outnorm_fwd(oaT, ob, ga, gb, *, tm=512):
    T = ob.shape[0]
    tm = min(tm, T)

    def body(oaT_ref, ob_ref, ga_ref, gb_ref, y_ref):
        for j, (o, g_ref) in enumerate(((oaT_ref[...].T, ga_ref), (ob_ref[...], gb_ref))):
            r = lax.rsqrt(jnp.mean(o * o, axis=1, keepdims=True) + RMS_EPS)
            y_ref[:, j * QW:(j + 1) * QW] = (o * r * g_ref[...]).astype(BF16)

    half = pl.BlockSpec((tm, QW), lambda i: (i, 0))
    halfT = pl.BlockSpec((QW, tm), lambda i: (0, i))
    vec = pl.BlockSpec((1, QW), lambda i: (0, 0))
    return pl.pallas_call(
        body, name="outnorm_fwd", grid=(T // tm,),
        out_shape=jax.ShapeDtypeStruct((T, 2 * QW), BF16),
        in_specs=[halfT, half, vec, vec], out_specs=pl.BlockSpec((tm, 2 * QW), lambda i: (i, 0)),
        compiler_params=_params(("parallel",), 16 << 20),
    )(oaT, ob, ga, gb)


def _outnorm_bwd(dy, oaT, ob, ga, gb, *, tm=512):
    T = ob.shape[0]
    tm = min(tm, T)
    nh = QW // HEAD_DIM

    def body(dy_ref, oaT_ref, ob_ref, ga_ref, gb_ref, doaT_ref, dl_ref, dob_ref, dg_ref):
        i = pl.program_id(0)

        @pl.when(i == 0)
        def _():
            dg_ref[...] = jnp.zeros_like(dg_ref)

        oaT = oaT_ref[...]
        for j, (o, g_ref) in enumerate(((oaT.T, ga_ref), (ob_ref[...], gb_ref))):
            d = dy_ref[:, j * QW:(j + 1) * QW]
            r = lax.rsqrt(jnp.mean(o * o, axis=1, keepdims=True) + RMS_EPS)
            orr = o * r
            dg_ref[j:j + 1, :] += jnp.sum(d * orr, axis=0, keepdims=True)
            dgv = d * g_ref[...]
            do = r * dgv - orr * (r * r) * jnp.mean(dgv * o, axis=1, keepdims=True)
            if j == 0:
                doT = do.T
                doaT_ref[...] = doT.astype(BF16)
                prod = doT * oaT
                dl_ref[...] = jnp.concatenate(
                    [jnp.sum(prod[a * HEAD_DIM:(a + 1) * HEAD_DIM], axis=0, keepdims=True) for a in range(nh)], axis=0)
            else:
                dob_ref[...] = do

    half = pl.BlockSpec((tm, QW), lambda i: (i, 0))
    halfT = pl.BlockSpec((QW, tm), lambda i: (0, i))
    vec = pl.BlockSpec((1, QW), lambda i: (0, 0))
    return pl.pallas_call(
        body, name="outnorm_bwd", grid=(T // tm,),
        out_shape=(jax.ShapeDtypeStruct((QW, T), BF16), jax.ShapeDtypeStruct((nh, T), F32),
                   jax.ShapeDtypeStruct((T, QW), F32), jax.ShapeDtypeStruct((8, QW), F32)),
        in_specs=[pl.BlockSpec((tm, 2 * QW), lambda i: (i, 0)), halfT, half, vec, vec],
        out_specs=(halfT, pl.BlockSpec((nh, tm), lambda i: (0, i)), half, pl.BlockSpec((8, QW), lambda i: (0, 0))),
        compiler_params=_params(("arbitrary",), 32 << 20),
    )(dy, oaT, ob, ga, gb)


GELU_C = math.sqrt(2.0 / math.pi)
GELU_A = 0.044715
HALO = 16
SUB = 8


def _gelu_parts(x):
    x2 = x * x
    t = jnp.tanh(x * (GELU_C + (GELU_C * GELU_A) * x2))
    return 0.5 * (1.0 + t), t, x2


def _halo_specs(tm, tn, T):
    nh = tm // HALO
    last = T // HALO - 1
    cur = pl.BlockSpec((tm, tn), lambda j, i: (i, j))
    prev = pl.BlockSpec((HALO, tn), lambda j, i: (jnp.maximum(i * nh - 1, 0), j))
    nxt = pl.BlockSpec((HALO, tn), lambda j, i: (jnp.minimum((i + 1) * nh, last), j))
    return cur, prev, nxt


def _conv_glu_fwd(g, u, conv_w, conv_b, *, tm=256, tn=1408):
    T, F = g.shape
    tm, tn = min(tm, T), min(tn, F)
    cur, prev, nxt = _halo_specs(tm, tn, T)

    def body(g_ref, gp_ref, gn_ref, u_ref, w_ref, b_ref, a_ref):
        i = pl.program_id(1)
        gv = g_ref[...]
        before = jnp.where(i * tm > 0, gp_ref[HALO - SUB:, :], 0.0)
        after = jnp.where((i + 1) * tm < T, gn_ref[0:SUB, :], 0.0)
        gm1 = pltpu.roll(jnp.concatenate([before, gv], axis=0), 1, 0)[SUB:]
        gp1 = pltpu.roll(jnp.concatenate([gv, after], axis=0), tm + SUB - 1, 0)[:tm]
        gc = ((b_ref[...] + gm1 * w_ref[0:1, :]) + gv * w_ref[1:2, :]) + gp1 * w_ref[2:3, :]
        cdf, _, _ = _gelu_parts(gc)
        a_ref[...] = (gc * cdf * u_ref[...].astype(F32)).astype(BF16)

    wspec = pl.BlockSpec((8, tn), lambda j, i: (0, j))
    est = 2 * (3 * _nbytes((tm, tn), F32)) + 8 * _nbytes((tm, tn), F32)
    return pl.pallas_call(
        body, name="conv_glu_fwd", grid=(F // tn, T // tm),
        out_shape=jax.ShapeDtypeStruct((T, F), BF16),
        in_specs=[cur, prev, nxt, cur, wspec, pl.BlockSpec((1, tn), lambda j, i: (0, j))],
        out_specs=cur,
        compiler_params=_params(("parallel", "parallel"), est),
    )(g, g, g, u, conv_w, conv_b)


def _conv_glu_bwd(dact, g, u, conv_w, conv_b, *, tm=256, tn=1408):
    T, F = g.shape
    tm, tn = min(tm, T), min(tn, F)
    cur, prev, nxt = _halo_specs(tm, tn, T)
    te = tm + 2 * HALO

    def body(d_ref, dp_ref, dn_ref, g_ref, gp_ref, gn_ref, u_ref, up_ref, un_ref, w_ref, b_ref,
             dg_ref, du_ref, dc_ref):
        i = pl.program_id(1)

        @pl.when(i == 0)
        def _():
            dc_ref[...] = jnp.zeros_like(dc_ref)

        has_prev, has_next = i * tm > 0, (i + 1) * tm < T
        ge = jnp.concatenate([jnp.where(has_prev, gp_ref[...], 0.0), g_ref[...],
                              jnp.where(has_next, gn_ref[...], 0.0)], axis=0)
        ue = jnp.concatenate([up_ref[...], u_ref[...], un_ref[...]], axis=0).astype(F32)
        de = jnp.concatenate([jnp.where(has_prev, dp_ref[...].astype(F32), 0.0), d_ref[...].astype(F32),
                              jnp.where(has_next, dn_ref[...].astype(F32), 0.0)], axis=0)
        w0, w1, w2 = w_ref[0:1, :], w_ref[1:2, :], w_ref[2:3, :]
        gm1 = pltpu.roll(ge, 1, 0)
        gp1 = pltpu.roll(ge, te - 1, 0)
        gc = ((b_ref[...] + gm1 * w0) + ge * w1) + gp1 * w2
        cdf, t, gc2 = _gelu_parts(gc)
        dgelu = cdf + (0.5 * gc) * (1.0 - t * t) * (GELU_C + (3.0 * GELU_C * GELU_A) * gc2)
        dgc = de * ue * dgelu
        dge = w0 * pltpu.roll(dgc, te - 1, 0) + w1 * dgc + w2 * pltpu.roll(dgc, 1, 0)
        mid = slice(HALO, HALO + tm)
        dg_ref[...] = dge[mid].astype(BF16)
        du_ref[...] = (de[mid] * (gc[mid] * cdf[mid])).astype(BF16)
        dgm = dgc[mid]
        dc_ref[0:1, :] += jnp.sum(dgm * gm1[mid], axis=0, keepdims=True)
        dc_ref[1:2, :] += jnp.sum(dgm * ge[mid], axis=0, keepdims=True)
        dc_ref[2:3, :] += jnp.sum(dgm * gp1[mid], axis=0, keepdims=True)
        dc_ref[3:4, :] += jnp.sum(dgm, axis=0, keepdims=True)

    wspec = pl.BlockSpec((8, tn), lambda j, i: (0, j))
    est = 2 * (3 * _nbytes((tm, tn), F32) + 2 * _nbytes((tm, tn), BF16)) + 16 * _nbytes((te, tn), F32)
    return pl.pallas_call(
        body, name="conv_glu_bwd", grid=(F // tn, T // tm),
        out_shape=(jax.ShapeDtypeStruct((T, F), BF16), jax.ShapeDtypeStruct((T, F), BF16),
                   jax.ShapeDtypeStruct((8, F), F32)),
        in_specs=[cur, prev, nxt, cur, prev, nxt, cur, prev, nxt, wspec, pl.BlockSpec((1, tn), lambda j, i: (0, j))],
        out_specs=(cur, cur, wspec),
        compiler_params=_params(("parallel", "arbitrary"), est),
    )(dact, dact, dact, g, g, g, u, u, u, conv_w, conv_b)


def _adamw_math(w, g, m, v):
    m = ADAM_B1 * m + (1.0 - ADAM_B1) * g
    v = ADAM_B2 * v + (1.0 - ADAM_B2) * (g * g)
    m_hat = m / (1.0 - ADAM_B1 ** ADAM_STEP)
    v_hat = v / (1.0 - ADAM_B2 ** ADAM_STEP)
    delta = -ADAM_LR * (m_hat / (jnp.sqrt(v_hat) + ADAM_EPS) + ADAM_WD * w)
    return delta, m, v


def _adamw(w, m, v, gparts, *, name, tr):
    R, C = w.shape
    tr = min(tr, R)
    assert R % tr == 0

    def body(w_ref, m_ref, v_ref, gp_ref, g_ref, d_ref, nm_ref, nv_ref):
        g = gp_ref[0].astype(F32)
        for j in range(1, N_DEV):
            g = g + gp_ref[j].astype(F32)
        delta, nm, nv = _adamw_math(w_ref[...], g, m_ref[...], v_ref[...])
        g_ref[...] = g
        d_ref[...] = delta
        nm_ref[...] = nm
        nv_ref[...] = nv

    blk = pl.BlockSpec((tr, C), lambda i: (i, 0))
    out = jax.ShapeDtypeStruct((R, C), F32)
    return pl.pallas_call(
        body, name=name, grid=(R // tr,), out_shape=(out, out, out, out),
        in_specs=[blk, blk, blk, pl.BlockSpec((N_DEV, tr, C), lambda i: (0, i, 0))],
        out_specs=(blk, blk, blk, blk),
        compiler_params=_params(("parallel",), 24 << 20),
    )(w, m, v, gparts)


def _rope_tables(T):
    rows_n = T // GRID_W
    row = jnp.repeat(jnp.arange(rows_n, dtype=F32), GRID_W)
    col = jnp.tile(jnp.arange(GRID_W, dtype=F32), rows_n)
    half = HEAD_DIM // 2
    inv_freq = ROPE_THETA ** (-jnp.arange(0, half, 2, dtype=F32) / half)
    ang = jnp.concatenate([row[:, None] * inv_freq, col[:, None] * inv_freq], axis=-1)
    cos, sin = jnp.cos(ang), jnp.sin(ang)
    cos64 = jnp.repeat(cos, 2, axis=-1)
    sin64 = jnp.stack([-sin, sin], axis=-1).reshape(T, HEAD_DIM)
    return jnp.tile(cos64, (1, 2)), jnp.tile(sin64, (1, 2))


def _t5_bucket(rel):
    half = N_BUCKETS // 2
    max_exact = half // 2
    bucket = jnp.where(rel > 0, half, 0)
    rp = jnp.abs(rel)
    rpf = jnp.maximum(rp, 1).astype(F32)
    large = max_exact + (jnp.log(rpf / max_exact) / math.log(MAX_DISTANCE / max_exact)
                         * (half - max_exact)).astype(jnp.int32)
    large = jnp.minimum(large, half - 1)
    return bucket + jnp.where(rp < max_exact, rp, large)


def _window_buckets():
    qpos = jnp.arange(WB, dtype=jnp.int32)
    kpos = jnp.arange(WK, dtype=jnp.int32) - WB
    rel = kpos[None, :] - qpos[:, None]
    return jnp.where(jnp.abs(rel) <= WINDOW, _t5_bucket(rel), -1)


def _heads_first(a, nh):
    T = a.shape[0]
    return a.reshape(T, nh, HEAD_DIM).transpose(1, 0, 2)


def _row(v):
    return v.reshape(1, -1)


def _rows8(rows, width):
    a = jnp.stack(list(rows), axis=0)
    return jnp.pad(a, ((0, 8 - a.shape[0]), (0, 0)))


def _layer_fwd(l, xin, W, tabs, comm=None, on_comm=None):
    xhat, xg, xb, x16 = xin
    T = xhat.shape[0]
    cos2, sin2, biasT = tabs
    h = _mm([x16], [W["w_in"][l]], name="mm_in", out_dtype=F32, tm=MM_ROWS, tn=IN_COLS, tk=D_MODEL)
    gains = _rows8([jnp.tile(W["q_norm"][l], 2), jnp.tile(W["k_norm"][l], 2)], LANES)
    hT, kv_nat = _qk_rope_fwd(h, gains, cos2, sin2, name="qk_rope_fwd")
    hT = hT.reshape(A_HEADS, HEAD_DIM, T)
    ka, va = _heads_first(kv_nat[:, :KW], KV), _heads_first(kv_nat[:, KW:], KV)
    res = _attn_a_fwd(ka, hT, comm=comm)
    oaT, lse_a = res[0].reshape(QW, T), res[1]
    if comm is not None:
        on_comm(res[2:])
    sink_rows = jnp.repeat(W["sink"][l], WB).reshape(KV, 1, GQ * WB)
    ob_t, lse_b = _attn_b_fwd(h, biasT, sink_rows)
    ga, gb = _row(W["out_norm_a"][l]), _row(W["out_norm_b"][l])
    ycat = _outnorm_fwd(oaT, ob_t, ga, gb)
    g1, b1 = _row(W["ln1_g"][l]), _row(W["ln1_b"][l])
    x1hat, rstd1, x1_16 = _mm_res_ln(ycat, W["w_out"][l], xhat, xg, xb, g1, b1, name="mm_out_ln", tm=512)
    gate = _mm([x1_16], [W["w_gate"][l]], name="mm_gate", out_dtype=F32, tm=MM_ROWS_WIDE, tn=D_FF, tk=D_MODEL)
    up = _mm([x1_16], [W["w_up"][l]], name="mm_up", out_dtype=BF16, tm=MM_ROWS_WIDE, tn=D_FF, tk=D_MODEL)
    cw = jnp.pad(W["conv_w"][l], ((0, 5), (0, 0)))
    cb = _row(W["conv_b"][l])
    act = _conv_glu_fwd(gate, up, cw, cb)
    g2, b2 = _row(W["ln2_g"][l]), _row(W["ln2_b"][l])
    x2hat, rstd2, x2_16 = _mm_res_ln(act, W["w_down"][l], x1hat, g1, b1, g2, b2, name="mm_down_ln", tm=256)
    saved = dict(x16=x16, h=h, gains=gains, hT=hT, ka=ka, va=va, oaT=oaT, lse_a=lse_a,
                 lse_b=lse_b, sink_rows=sink_rows, ob_t=ob_t,
                 ga=ga, gb=gb, ycat=ycat, x1hat=x1hat, rstd1=rstd1, x1_16=x1_16, g1=g1, b1=b1, gate=gate, up=up,
                 cw=cw, cb=cb, act=act, x2hat=x2hat, rstd2=rstd2, g2=g2, b2=b2)
    return (x2hat, g2, b2, x2_16), saved


def _layer_bwd(l, S, W, tabs, dz2, dz2_16, stats2, scatter=None, tail=None):
    cos2, sin2, biasT = tabs
    T = dz2.shape[0]
    G = {}
    G["ln2_g"], G["ln2_b"] = stats2[0], stats2[1]
    G["w_down"] = _mm([S["act"]], [dz2_16], name="dw_down", out_dtype=BF16, trans_a=True, tm=D_FF // 2, tn=D_MODEL, tk=DW_TOKENS)
    dact = _mm([dz2_16], [W["w_down"][l]], name="mm_dact", out_dtype=BF16, trans_b=True, tm=MM_ROWS_WIDE, tn=D_FF,
               tk=D_MODEL)
    dg, du, dconv = _conv_glu_bwd(dact, S["gate"], S["up"], S["cw"], S["cb"])
    G["conv_w"], G["conv_b"] = dconv[0:3], dconv[3]
    G["w_gate"] = _mm([S["x1_16"]], [dg], name="dw_gate", out_dtype=BF16, trans_a=True, tm=D_MODEL, tn=D_FF // 2, tk=DW_TOKENS)
    G["w_up"] = _mm([S["x1_16"]], [du], name="dw_up", out_dtype=BF16, trans_a=True, tm=D_MODEL, tn=D_FF // 2, tk=DW_TOKENS)
    dx1 = _mm([dg, du], [W["w_gate"][l], W["w_up"][l]], name="mm_dx1", out_dtype=F32, trans_b=True, tm=MM_ROWS_WIDE,
              tn=D_MODEL, tk=D_FF, add=dz2, add_scale=ALPHA)
    dz1, dz1_16, stats1 = _ln_bwd(S["x1hat"], S["rstd1"], S["g1"], S["b1"], name="ln1_bwd", dx=dx1)
    G["ln1_g"], G["ln1_b"] = stats1[0], stats1[1]
    G["w_out"] = _mm([S["ycat"]], [dz1_16], name="dw_out", out_dtype=BF16, trans_a=True, tm=D_MODEL, tn=D_MODEL, tk=DW_TOKENS)
    dycat = _mm([dz1_16], [W["w_out"][l]], name="mm_dycat", out_dtype=F32, trans_b=True, tm=MM_ROWS, tn=D_MODEL,
                tk=D_MODEL)
    doaT, delta, dob_t, dgn = _outnorm_bwd(dycat, S["oaT"], S["ob_t"], S["ga"], S["gb"])
    G["out_norm_a"], G["out_norm_b"] = dgn[0], dgn[1]
    res = _attn_a_bwd(S["ka"], S["va"], S["hT"], doaT.reshape(KV * GQ, HEAD_DIM, T), S["lse_a"],
                      delta.reshape(KV * GQ, 1, T), comm=scatter(G) if scatter is not None else None)
    dkaT, dvaT, dqaT = res[:3]
    dh_rope, dgain = _qk_rope_bwd(S["h"], dqaT.reshape(QW, T), dkaT.reshape(KW, T), S["gains"], cos2, sin2,
                                  name="qk_rope_bwd")
    G["q_norm"], G["k_norm"] = dgain[0, :HEAD_DIM], dgain[1, :HEAD_DIM]
    dqb_t, dkb, dvb, dbiasT, dsk = _attn_b_bwd(S["h"], dob_t, S["ob_t"], S["lse_b"], biasT, S["sink_rows"])
    dh = jnp.concatenate([
        dh_rope, dvaT.transpose(2, 0, 1).reshape(T, KW).astype(BF16), dqb_t,
        dkb[WB:WB + T].astype(BF16), dvb[WB:WB + T].astype(BF16)], axis=1)
    dbias = dbiasT.reshape(KV, WK, GQ, WB).transpose(0, 2, 1, 3)
    G["w_in"] = _mm([S["x16"]], [dh], name="dw_in", out_dtype=BF16, trans_a=True, tm=D_MODEL, tn=IN_COLS, tk=DW_TOKENS)
    dbias, dsk = dbias.reshape(KV * GQ, WK * WB), dsk.reshape(KV * GQ, WB)
    comm = tail(G, dbias, dsk) if tail is not None else None
    out = _mm([dh], [W["w_in"][l]], name="mm_dxin" if comm is None else "mm_dxin_comm", out_dtype=F32, trans_b=True,
              tm=MM_ROWS, tn=D_MODEL, tk=IN_COLS, add=dz1, add_scale=ALPHA, comm=comm)
    dxin, sent = (out, ()) if comm is None else (out[0], out[1:])
    return dxin, G, dbias, dsk, res[3:], sent


BIG = ("w_in", "w_out", "w_gate", "w_up", "w_down")
COL_SHARDED = ("w_in", "w_gate", "w_up")


def _unshard(name, blocks):
    _, r, c = blocks.shape
    if name in COL_SHARDED:
        return blocks.transpose(1, 0, 2).reshape(r, N_DEV * c)
    return blocks.reshape(N_DEV * r, c)


def _to_owner_blocks(name, full, shard_shape):
    _, r, c = shard_shape
    if name in COL_SHARDED:
        return full.reshape(r, N_DEV, c).transpose(1, 0, 2)
    return full.reshape(N_DEV, r, c)


def _pack_small(vals, tail):
    flat = jnp.concatenate([vals[n].reshape(-1).astype(F32) for n in SMALL_NAMES] + [tail])
    pad = (-flat.shape[0]) % (8 * LANES)
    return jnp.pad(flat, (0, pad)).reshape(-1, LANES)


def _unpack_small(packed, shapes):
    flat = packed.reshape(-1)
    out, off = {}, 0
    for n in SMALL_NAMES:
        size = math.prod(shapes[n])
        out[n] = flat[off:off + size].reshape(shapes[n])
        off += size
    return out, flat[off]


def kernel(x, rel_bias, w_in, q_norm, k_norm, sink, out_norm_a, out_norm_b, w_out, ln1_g, ln1_b, w_gate, w_up, conv_w, conv_b, w_down, ln2_g, ln2_b, loss_target, m_rel_bias, m_w_in, m_q_norm, m_k_norm, m_sink, m_out_norm_a, m_out_norm_b, m_w_out, m_ln1_g, m_ln1_b, m_w_gate, m_w_up, m_conv_w, m_conv_b, m_w_down, m_ln2_g, m_ln2_b, v_rel_bias, v_w_in, v_q_norm, v_k_norm, v_sink, v_out_norm_a, v_out_norm_b, v_w_out, v_ln1_g, v_ln1_b, v_w_gate, v_w_up, v_conv_w, v_conv_b, v_w_down, v_ln2_g, v_ln2_b):
    P = dict(rel_bias=rel_bias, w_in=w_in, q_norm=q_norm, k_norm=k_norm, sink=sink, out_norm_a=out_norm_a,
             out_norm_b=out_norm_b, w_out=w_out, ln1_g=ln1_g, ln1_b=ln1_b, w_gate=w_gate, w_up=w_up, conv_w=conv_w,
             conv_b=conv_b, w_down=w_down, ln2_g=ln2_g, ln2_b=ln2_b)
    M = dict(rel_bias=m_rel_bias, w_in=m_w_in, q_norm=m_q_norm, k_norm=m_k_norm, sink=m_sink, out_norm_a=m_out_norm_a,
             out_norm_b=m_out_norm_b, w_out=m_w_out, ln1_g=m_ln1_g, ln1_b=m_ln1_b, w_gate=m_w_gate, w_up=m_w_up,
             conv_w=m_conv_w, conv_b=m_conv_b, w_down=m_w_down, ln2_g=m_ln2_g, ln2_b=m_ln2_b)
    V = dict(rel_bias=v_rel_bias, w_in=v_w_in, q_norm=v_q_norm, k_norm=v_k_norm, sink=v_sink, out_norm_a=v_out_norm_a,
             out_norm_b=v_out_norm_b, w_out=v_w_out, ln1_g=v_ln1_g, ln1_b=v_ln1_b, w_gate=v_w_gate, w_up=v_w_up,
             conv_w=v_conv_w, conv_b=v_conv_b, w_down=v_w_down, ln2_g=v_ln2_g, ln2_b=v_ln2_b)
    names = list(P)
    T = x.shape[1]
    me = 4 * lax.axis_index("x") + 2 * lax.axis_index("y") + lax.axis_index("c")

    L, taps, fc = conv_w.shape
    W = {n: ([None] * DEPTH if n in BIG else P[n]) for n in names}

    def wire(n, l):
        return P[n][l].astype(BF16)

    def take(n, l, gathered):
        W[n][l] = _unshard(n, gathered)

    take("w_in", 0, _exchange([wire("w_in", 0)], [True], name="gather_w_in0")[0])
    later = [(n, l) for l in range(DEPTH) for n in BIG if (n, l) != ("w_in", 0)]
    cw_shard = conv_w.reshape(-1)
    cw_wire = jnp.pad(cw_shard, (0, (-cw_shard.shape[0]) % LANES)).reshape(-1, LANES)
    gather_rest = _Comm([wire(n, l) for n, l in later] + [cw_wire], [True] * (len(later) + 1))

    def on_gathered(outs):
        for (n, l), g in zip(later, outs):
            take(n, l, g)
        cw_all = outs[-1].reshape(N_DEV, -1)[:, :cw_shard.shape[0]].reshape(N_DEV, L, taps, fc)
        W["conv_w"] = cw_all.transpose(1, 2, 0, 3).reshape(L, taps, N_DEV * fc)

    cos2, sin2 = _rope_tables(T)
    bucket = _window_buckets()
    bias = _bias_table(rel_bias.T, bucket.reshape(1, WB * WK))
    biasT = bias.reshape(KV, GQ, WB, WK).transpose(0, 3, 1, 2).reshape(KV, WK, GQ * WB)
    biasT = _end_tables(biasT)
    tabs = (cos2, sin2, biasT)

    ones, zeros = jnp.ones((1, D_MODEL), F32), jnp.zeros((1, D_MODEL), F32)
    cur = (x[0], ones, zeros, x[0].astype(BF16))
    saved = []
    for l in range(DEPTH):
        cur, S = _layer_fwd(l, cur, W, tabs, comm=gather_rest if l == 0 else None, on_comm=on_gathered)
        saved.append(S)

    def owner_blocks(n, l):
        return _to_owner_blocks(n, grads[l][n], P[n].shape)

    early = ([(n, l) for l in range(1, DEPTH) for n in BIG] + [(n, 0) for n in BIG if n != "w_in"])

    def scatter_early(g0):
        grads[0] = g0
        return _Comm([owner_blocks(n, l) for n, l in early], [False] * len(early))

    grads = [None] * DEPTH
    dbs, dsks = [None] * DEPTH, [None] * DEPTH
    S = saved[-1]
    dz, dz16, stats = _ln_bwd(S["x2hat"], S["rstd2"], S["g2"], S["b2"], name="loss_ln2_bwd", target=loss_target[0])
    loss_part = stats[2, 0:1]
    def scatter_last(g0, db0, dsk0):
        grads[0], dbs[0], dsks[0] = g0, db0, dsk0
        drb, dsink = _bias_sink_grads(dbs, dsks, bucket.T.reshape(1, WK * WB))
        small_g = {n: jnp.stack([grads[l][n] for l in range(DEPTH)])
                   for n in SMALL_NAMES if n not in ("rel_bias", "sink")}
        small_g["rel_bias"] = drb.T
        small_g["sink"] = dsink.reshape(DEPTH, KV * GQ)
        return _Comm([owner_blocks("w_in", 0), _pack_small(small_g, loss_part)], [False, True])

    recv = {}
    for l in reversed(range(DEPTH)):
        S = saved[l]
        dxin, grads[l], dbs[l], dsks[l], got, sent = _layer_bwd(
            l, S, W, tabs, dz, dz16, stats, scatter=scatter_early if l == 0 else None,
            tail=scatter_last if l == 0 else None)
        if l == 0:
            recv.update(zip(early, got))
            recv[("w_in", 0)], small_recv = sent
        if l > 0:
            Sp = saved[l - 1]
            dz, dz16, stats = _ln_bwd(Sp["x2hat"], Sp["rstd2"], Sp["g2"], Sp["b2"], name="ln2_bwd", dx=dxin)
    grad_x = dxin[None]

    out_g, out_d, out_m, out_v = {}, {}, {}, {}
    for n in BIG:
        shp = P[n].shape
        gparts = jnp.concatenate([recv[(n, l)] for l in range(DEPTH)], axis=1)
        rows, cols = shp[0] * shp[1], shp[2]
        res = _adamw(P[n].reshape(rows, cols), M[n].reshape(rows, cols), V[n].reshape(rows, cols), gparts,
                     name="adamw_" + n, tr=math.gcd(rows, 256))
        out_g[n], out_d[n], out_m[n], out_v[n] = (r.reshape(shp) for r in res)
    full_shapes = {n: W[n].shape for n in SMALL_NAMES}

    def small_state(D):
        vals = {n: D[n] for n in SMALL_NAMES if n != "conv_w"}
        cw = jnp.zeros((L, taps, N_DEV, fc), F32)
        cw = lax.dynamic_update_slice(cw, D["conv_w"].reshape(L, taps, 1, fc), (0, 0, me, 0))
        vals["conv_w"] = cw.reshape(L, taps, N_DEV * fc)
        return _pack_small(vals, jnp.zeros((1,), F32))

    sw, sm, sv = small_state(P), small_state(M), small_state(V)
    res = _adamw(sw, sm, sv, small_recv, name="adamw_small", tr=sw.shape[0])
    loss = _unpack_small(res[0], full_shapes)[1]
    for dst, packed in zip((out_g, out_d, out_m, out_v), res):
        vals, _ = _unpack_small(packed, full_shapes)
        for n in SMALL_NAMES:
            if n == "conv_w":
                sl = lax.dynamic_slice(vals[n].reshape(L, taps, N_DEV, fc), (0, 0, me, 0), (L, taps, 1, fc))
                dst[n] = sl.reshape(L, taps, fc)
            else:
                dst[n] = vals[n]
    return (loss, grad_x, *[out_g[n] for n in names], *[out_d[n] for n in names],
            *[out_m[n] for n in names], *[out_v[n] for n in names])
```

```python
import functools
import math

import jax
import jax.numpy as jnp
from jax import lax
from jax.experimental import pallas as pl
from jax.experimental.pallas import tpu as pltpu

F32 = jnp.float32
BF16 = jnp.bfloat16
MESH = pl.DeviceIdType.MESH

N_DEV = 8
D_MODEL = 1024
DEPTH = 2
HEAD_DIM = 64
KV = 2
GQ = 4
QW = KV * GQ * HEAD_DIM
KW = KV * HEAD_DIM
ROPE_W = QW + KW
IN_COLS = 2 * (QW + 2 * KW)
D_FF = 2816
GRID_W = 64
ROPE_THETA = 10000.0
WINDOW = 128
N_BUCKETS = 32
MAX_DISTANCE = 128
ALPHA = (2.0 * DEPTH) ** 0.25
RMS_EPS = 1e-6
LN_EPS = 1e-5
SCALE = HEAD_DIM ** -0.5
LOG2E = math.log2(math.e)
LN2 = math.log(2.0)
NEG = -1e30
ONES_ROWS = 16

ADAM_LR = 0.001
ADAM_B1 = 0.9
ADAM_B2 = 0.999
ADAM_EPS = 1e-08
ADAM_WD = 0.01
ADAM_STEP = 10

LANES = 128
MM_ROWS = 1024
MM_ROWS_WIDE = 512
DW_TOKENS = 1024
VMEM_CAP = 60 * 1024 * 1024
SMALL_NAMES = ("rel_bias", "q_norm", "k_norm", "sink", "out_norm_a", "out_norm_b", "ln1_g", "ln1_b",
               "conv_b", "ln2_g", "ln2_b", "conv_w")


def _params(sem, est_bytes):
    limit = int(min(VMEM_CAP, est_bytes + (8 << 20)))
    return pltpu.CompilerParams(dimension_semantics=sem, vmem_limit_bytes=limit)


def _nbytes(shape, dtype):
    return math.prod(shape) * jnp.dtype(dtype).itemsize


class _Comm:
    def __init__(self, parts, gathers):
        self.parts, self.gathers, self.n = list(parts), list(gathers), len(parts)
        hbm = pl.BlockSpec(memory_space=pltpu.HBM)
        self.in_specs = [hbm] * self.n
        self.out_specs = [hbm] * self.n
        self.out_shape = [jax.ShapeDtypeStruct((N_DEV,) + tuple(p.shape if g else p.shape[1:]), p.dtype)
                          for p, g in zip(self.parts, self.gathers)]
        self.scratch = [pltpu.SemaphoreType.DMA((self.n * (N_DEV - 1),)), pltpu.SemaphoreType.DMA((self.n * (N_DEV - 1),)),
                        pltpu.SemaphoreType.DMA((self.n,))]

    def bind(self, ins, outs, sems):
        send_sems, recv_sems, local_sems = sems
        gathers, n = self.gathers, self.n
        me = 4 * lax.axis_index("x") + 2 * lax.axis_index("y") + lax.axis_index("c")

        def src(k, j):
            return ins[k] if gathers[k] else ins[k].at[j]

        def copy(k, d, peer, lands_in):
            return pltpu.make_async_remote_copy(
                src_ref=src(k, peer), dst_ref=outs[k].at[lands_in],
                send_sem=send_sems.at[k * (N_DEV - 1) + d - 1], recv_sem=recv_sems.at[k * (N_DEV - 1) + d - 1],
                device_id=(peer // 4, lax.rem(peer // 2, 2), lax.rem(peer, 2)), device_id_type=MESH)

        def send(k, d):
            return copy(k, d, lax.rem(me + d, N_DEV), me)

        def arrival(k, d):
            frm = lax.rem(me + N_DEV - d, N_DEV)
            return copy(k, d, frm, frm)

        def local(k):
            return pltpu.make_async_copy(src(k, me), outs[k].at[me], local_sems.at[k])

        def start():
            for k in range(n):
                local(k).start()
                for d in range(1, N_DEV):
                    send(k, d).start()

        def finish():
            for k in range(n):
                for d in range(1, N_DEV):
                    arrival(k, d).wait_recv()
            for k in range(n):
                for d in range(1, N_DEV):
                    send(k, d).wait_send()
                local(k).wait()

        return start, finish


def _host_comm(comm, refs, n_in, n_out, n_scratch, grid):
    n = comm.n if comm is not None else 0
    own_in, cin = refs[:n_in], refs[n_in:n_in + n]
    own_out, cout = refs[n_in + n:n_in + n + n_out], refs[n_in + n + n_out:n_in + 2 * n + n_out]
    base = n_in + 2 * n + n_out
    own_scratch, sems = refs[base:base + n_scratch], refs[base + n_scratch:]
    own = tuple(own_in) + tuple(own_out) + tuple(own_scratch)
    if comm is None:
        return own, lambda: None, lambda: None
    start, finish = comm.bind(cin, cout, sems)
    first = last = None
    for ax, size in enumerate(grid):
        pid = pl.program_id(ax)
        first = (pid == 0) if first is None else first & (pid == 0)
        last = (pid == size - 1) if last is None else last & (pid == size - 1)
    return own, lambda: pl.when(first)(start), lambda: pl.when(last)(finish)


def _exchange(parts, gathers, name):
    comm = _Comm(parts, gathers)
    n = comm.n

    def body(*refs):
        start, finish = comm.bind(refs[:n], refs[n:2 * n], refs[2 * n:])
        start()
        finish()

    return pl.pallas_call(body, name=name, out_shape=comm.out_shape, in_specs=comm.in_specs, out_specs=comm.out_specs,
                          scratch_shapes=comm.scratch)(*comm.parts)


def _mm(a_list, b_list, *, name, out_dtype, tm, tn, tk, trans_a=False, trans_b=False, add=None, add_scale=1.0):
    assert not (trans_a and trans_b)
    na = len(a_list)
    if trans_a:
        K, M = a_list[0].shape
    else:
        M, K = a_list[0].shape
    N = b_list[0].shape[0 if trans_b else 1]
    tm, tn, tk = min(tm, M), min(tn, N), min(tk, K)
    assert M % tm == 0 and N % tn == 0 and K % tk == 0, (name, M, N, K, tm, tn, tk)
    nk = K // tk
    dims = (((0,), (0,)), ((), ())) if trans_a else (((1,), (1 if trans_b else 0,)), ((), ()))

    def body(*refs):
        a_refs, b_refs = refs[:na], refs[na:2 * na]
        add_ref = refs[2 * na] if add is not None else None
        o_ref = refs[2 * na + (add is not None)]
        k = pl.program_id(2)

        part = None
        for a_ref, b_ref in zip(a_refs, b_refs):
            prod = lax.dot_general(a_ref[...].astype(BF16), b_ref[...].astype(BF16), dims,
                                   preferred_element_type=F32)
            part = prod if part is None else part + prod

        def finish(res):
            if add_ref is not None:
                res = res + add_scale * add_ref[...]
            o_ref[...] = res.astype(o_ref.dtype)

        if nk == 1:
            finish(part)
        else:
            acc_ref = refs[-1]

            @pl.when(k == 0)
            def _():
                acc_ref[...] = part

            @pl.when(k > 0)
            def _():
                acc_ref[...] += part

            @pl.when(k == nk - 1)
            def _():
                finish(acc_ref[...])

    if trans_a:
        a_spec = pl.BlockSpec((tk, tm), lambda i, j, k: (k, i))
    else:
        a_spec = pl.BlockSpec((tm, tk), lambda i, j, k: (i, k))
    if trans_b:
        b_spec = pl.BlockSpec((tn, tk), lambda i, j, k: (j, k))
    else:
        b_spec = pl.BlockSpec((tk, tn), lambda i, j, k: (k, j))
    o_spec = pl.BlockSpec((tm, tn), lambda i, j, k: (i, j))
    in_specs = [a_spec] * na + [b_spec] * na + ([o_spec] if add is not None else [])
    est = (2 * na * (_nbytes((tm, tk), a_list[0].dtype) + _nbytes((tk, tn), b_list[0].dtype))
           + na * (_nbytes((tm, tk), BF16) + _nbytes((tk, tn), BF16))
           + 2 * _nbytes((tm, tn), out_dtype) + 3 * _nbytes((tm, tn), F32)
           + (2 * _nbytes((tm, tn), F32) if add is not None else 0))
    args = list(a_list) + list(b_list) + ([add] if add is not None else [])
    return pl.pallas_call(
        body, name=name, grid=(M // tm, N // tn, nk),
        out_shape=jax.ShapeDtypeStruct((M, N), out_dtype),
        in_specs=in_specs, out_specs=o_spec,
        scratch_shapes=[pltpu.VMEM((tm, tn), F32)] if nk > 1 else [],
        compiler_params=_params(("parallel", "parallel", "arbitrary"), est),
    )(*args)


def _mm_res_ln(a, w, res_hat, res_g, res_b, ln_g, ln_b, *, name, tm):
    T, K = a.shape
    D = w.shape[1]
    tm = min(tm, T)

    def body(a_ref, w_ref, rh_ref, rg_ref, rb_ref, g_ref, b_ref, xhat_ref, rstd_ref, xb_ref):
        branch = jnp.dot(a_ref[...].astype(BF16), w_ref[...], preferred_element_type=F32)
        z = ALPHA * (rh_ref[...] * rg_ref[...] + rb_ref[...]) + branch
        mu = jnp.mean(z, axis=1, keepdims=True)
        zc = z - mu
        var = jnp.mean(zc * zc, axis=1, keepdims=True)
        rstd = lax.rsqrt(var + LN_EPS)
        xhat = zc * rstd
        xhat_ref[...] = xhat
        rstd_ref[...] = rstd
        xb_ref[...] = (xhat * g_ref[...] + b_ref[...]).astype(BF16)

    row = pl.BlockSpec((tm, D), lambda i: (i, 0))
    vec = pl.BlockSpec((1, D), lambda i: (0, 0))
    est = (2 * (_nbytes((tm, K), a.dtype) + _nbytes((K, D), BF16)) + 4 * _nbytes((tm, D), F32) * 2
           + 6 * _nbytes((tm, D), F32))
    return pl.pallas_call(
        body, name=name, grid=(T // tm,),
        out_shape=(jax.ShapeDtypeStruct((T, D), F32), jax.ShapeDtypeStruct((T, 1), F32),
                   jax.ShapeDtypeStruct((T, D), BF16)),
        in_specs=[pl.BlockSpec((tm, K), lambda i: (i, 0)), pl.BlockSpec((K, D), lambda i: (0, 0)), row, vec, vec, vec, vec],
        out_specs=(row, pl.BlockSpec((tm, 1), lambda i: (i, 0)), row),
        compiler_params=_params(("parallel",), est),
    )(a, w, res_hat, res_g, res_b, ln_g, ln_b)


def _ln_bwd(xhat, rstd, ln_g, ln_b, *, name, dx=None, target=None, tm=256):
    T, D = xhat.shape
    tm = min(tm, T)
    head = target is not None

    def body(xhat_ref, rstd_ref, g_ref, b_ref, d_ref, dz_ref, dzb_ref, st_ref):
        i = pl.program_id(0)

        @pl.when(i == 0)
        def _():
            st_ref[...] = jnp.zeros_like(st_ref)

        xh = xhat_ref[...]
        g = g_ref[...]
        if head:
            err = (xh * g + b_ref[...]) - d_ref[...]
            dxv = err * (1.0 / D)
            st_ref[2:3, :] += 0.5 * jnp.sum(jnp.sum(err * err, axis=1, keepdims=True) * (1.0 / D), axis=0, keepdims=True)
        else:
            dxv = d_ref[...]
        st_ref[0:1, :] += jnp.sum(dxv * xh, axis=0, keepdims=True)
        st_ref[1:2, :] += jnp.sum(dxv, axis=0, keepdims=True)
        dxh = dxv * g
        m1 = jnp.mean(dxh, axis=1, keepdims=True)
        m2 = jnp.mean(dxh * xh, axis=1, keepdims=True)
        dz = rstd_ref[...] * (dxh - m1 - xh * m2)
        dz_ref[...] = dz
        dzb_ref[...] = dz.astype(BF16)

    row = pl.BlockSpec((tm, D), lambda i: (i, 0))
    vec = pl.BlockSpec((1, D), lambda i: (0, 0))
    est = 2 * 4 * _nbytes((tm, D), F32) + 6 * _nbytes((tm, D), F32)
    return pl.pallas_call(
        body, name=name, grid=(T // tm,),
        out_shape=(jax.ShapeDtypeStruct((T, D), F32), jax.ShapeDtypeStruct((T, D), BF16),
                   jax.ShapeDtypeStruct((8, D), F32)),
        in_specs=[row, pl.BlockSpec((tm, 1), lambda i: (i, 0)), vec, vec, row],
        out_specs=(row, row, pl.BlockSpec((8, D), lambda i: (0, 0))),
        compiler_params=_params(("arbitrary",), est),
    )(xhat, rstd, ln_g, ln_b, target if head else dx)


def _pair_swap(v, even):
    return jnp.where(even, pltpu.roll(v, LANES - 1, 1), pltpu.roll(v, 1, 1))


def _half_sums(v, lo):
    s_lo = jnp.sum(jnp.where(lo, v, 0.0), axis=1, keepdims=True)
    s_hi = jnp.sum(jnp.where(lo, 0.0, v), axis=1, keepdims=True)
    return jnp.where(lo, s_lo, s_hi)


A_COLS = ROPE_W + KW
A_HEADS = A_COLS // HEAD_DIM
A_K0, A_V0 = KV * GQ, KV * GQ + KV


def _qk_rope_fwd(h, gains, cos2, sin2, *, name, tm=256):
    T = h.shape[0]
    tm = min(tm, T)
    nch = A_COLS // LANES

    def body(h_ref, g_ref, c_ref, s_ref, oT_ref, kv_ref):
        lane = lax.broadcasted_iota(jnp.int32, (tm, LANES), 1)
        lo, even = lane < HEAD_DIM, lane % 2 == 0
        c, s = c_ref[...], s_ref[...]
        for j in range(nch):
            x = h_ref[:, j * LANES:(j + 1) * LANES]
            isq, isv = j < QW // LANES, j == nch - 1
            if isv:
                out = x
            else:
                g = g_ref[0:1, :] if isq else g_ref[1:2, :]
                r = lax.rsqrt(_half_sums(x * x, lo) * (1.0 / HEAD_DIM) + RMS_EPS)
                nrm = x * r * g
                out = nrm * c + _pair_swap(nrm, even) * s
            if isq:
                out = out * (SCALE * LOG2E)
            else:
                kv_ref[:, (j - QW // LANES) * LANES:(j - QW // LANES + 1) * LANES] = out.astype(BF16)
            oT_ref[j * LANES:(j + 1) * LANES, :] = out.T.astype(BF16)

    est = 2 * (_nbytes((tm, A_COLS), F32) + 2 * _nbytes((tm, A_COLS), BF16) + 2 * _nbytes((tm, LANES), F32)) + (4 << 20)
    return pl.pallas_call(
        body, name=name, grid=(T // tm,),
        out_shape=(jax.ShapeDtypeStruct((A_COLS, T), BF16), jax.ShapeDtypeStruct((T, 2 * KW), BF16)),
        in_specs=[pl.BlockSpec((tm, A_COLS), lambda i: (i, 0)), pl.BlockSpec((8, LANES), lambda i: (0, 0)),
                  pl.BlockSpec((tm, LANES), lambda i: (i, 0)), pl.BlockSpec((tm, LANES), lambda i: (i, 0))],
        out_specs=(pl.BlockSpec((A_COLS, tm), lambda i: (0, i)), pl.BlockSpec((tm, 2 * KW), lambda i: (i, 0))),
        compiler_params=_params(("parallel",), est),
    )(h, gains, cos2, sin2)


def _qk_rope_bwd(h, dqT, dkT, gains, cos2, sin2, *, name, tm=256):
    T = h.shape[0]
    tm = min(tm, T)
    nch = ROPE_W // LANES

    def body(h_ref, dq_ref, dk_ref, g_ref, c_ref, s_ref, dh_ref, dg_ref):
        i = pl.program_id(0)

        @pl.when(i == 0)
        def _():
            dg_ref[...] = jnp.zeros_like(dg_ref)

        lane = lax.broadcasted_iota(jnp.int32, (tm, LANES), 1)
        lo, even = lane < HEAD_DIM, lane % 2 == 0
        c, s = c_ref[...], s_ref[...]
        acc = [None, None]
        for j in range(nch):
            x = h_ref[:, j * LANES:(j + 1) * LANES]
            isq = j < QW // LANES
            g = g_ref[0:1, :] if isq else g_ref[1:2, :]
            d = dq_ref[j * LANES:(j + 1) * LANES, :].T * SCALE if isq else dk_ref[...].T
            r = lax.rsqrt(_half_sums(x * x, lo) * (1.0 / HEAD_DIM) + RMS_EPS)
            dn = d * c + _pair_swap(d * s, even)
            xr = x * r
            part = jnp.sum(dn * xr, axis=0, keepdims=True)
            acc[0 if isq else 1] = part if acc[0 if isq else 1] is None else acc[0 if isq else 1] + part
            dng = dn * g
            dx = r * dng - xr * (r * r) * (_half_sums(dng * x, lo) * (1.0 / HEAD_DIM))
            dh_ref[:, j * LANES:(j + 1) * LANES] = dx.astype(BF16)
        for row in range(2):
            folded = acc[row] + pltpu.roll(acc[row], HEAD_DIM, 1)
            dg_ref[row:row + 1, :] += folded

    est = 2 * (2 * _nbytes((tm, ROPE_W), F32) + _nbytes((tm, ROPE_W), BF16) + 2 * _nbytes((tm, LANES), F32)) + (4 << 20)
    return pl.pallas_call(
        body, name=name, grid=(T // tm,),
        out_shape=(jax.ShapeDtypeStruct((T, ROPE_W), BF16), jax.ShapeDtypeStruct((8, LANES), F32)),
        in_specs=[pl.BlockSpec((tm, ROPE_W), lambda i: (i, 0)), pl.BlockSpec((QW, tm), lambda i: (0, i)),
                  pl.BlockSpec((KW, tm), lambda i: (0, i)), pl.BlockSpec((8, LANES), lambda i: (0, 0)),
                  pl.BlockSpec((tm, LANES), lambda i: (i, 0)), pl.BlockSpec((tm, LANES), lambda i: (i, 0))],
        out_specs=(pl.BlockSpec((tm, ROPE_W), lambda i: (i, 0)), pl.BlockSpec((8, LANES), lambda i: (0, 0))),
        compiler_params=_params(("arbitrary",), est),
    )(h, dqT, dkT, gains, cos2, sin2)


def _attn_a_fwd(k, hT, *, comm=None, tq=4096, tk=2048, cq=512):
    G, T, HD = k.shape
    HE = HD + ONES_ROWS
    tq, tk = min(tq, T), min(tk, T)
    cq = min(cq, tq)
    nk, nt = T // tk, T // tq
    grid = (G, GQ * nt, nk)

    def body(*refs):
        (k_ref, qT_ref, v_ref, oT_ref, lse_ref, m_sc, acc_sc), comm_start, comm_finish = _host_comm(
            comm, refs, 3, 2, 2, grid)
        kv = pl.program_id(2)
        comm_start()
        v1T = jnp.concatenate([v_ref[...], jnp.ones((ONES_ROWS, tk), BF16)], axis=0)

        @pl.when(kv == 0)
        def _():
            m_sc[...] = jnp.full_like(m_sc, NEG)
            acc_sc[...] = jnp.zeros_like(acc_sc)

        def scores(c):
            return jnp.dot(k_ref[...], qT_ref[:, c * cq:(c + 1) * cq], preferred_element_type=F32)

        nc = tq // cq
        ahead = scores(0)
        for c in range(nc):
            cols = slice(c * cq, (c + 1) * cq)
            sT = ahead
            if c + 1 < nc:
                ahead = scores(c + 1)
            m_prev = m_sc[:, cols]
            m_new = jnp.maximum(m_prev, jnp.max(sT, axis=0, keepdims=True))
            pT = jnp.exp2(sT - m_new).astype(BF16)
            acc_sc[:, cols] = (jnp.exp2(m_prev - m_new) * acc_sc[:, cols]
                               + jnp.dot(v1T, pT, preferred_element_type=F32))
            m_sc[:, cols] = m_new

        @pl.when(kv == nk - 1)
        def _():
            l = acc_sc[HD:HD + 1, :]
            oT_ref[...] = acc_sc[0:HD, :] / l
            lse_ref[...] = m_sc[...] + jnp.log2(l)

        comm_finish()

    qtr = pl.BlockSpec((None, HD, tq), lambda g, i, j: (g * GQ + i // nt, 0, i % nt))
    qvec = pl.BlockSpec((None, 1, tq), lambda g, i, j: (g * GQ + i // nt, 0, i % nt))
    est = 6 * _nbytes((cq, tk), F32) + (8 << 20)
    hosted = comm is not None
    return pl.pallas_call(
        body, name="attn_a_fwd_comm" if hosted else "attn_a_fwd", grid=grid,
        out_shape=[jax.ShapeDtypeStruct((G * GQ, HD, T), F32), jax.ShapeDtypeStruct((G * GQ, 1, T), F32)]
        + (comm.out_shape if hosted else []),
        in_specs=[pl.BlockSpec((None, tk, HD), lambda g, i, j: (g, j, 0)), qtr,
                  pl.BlockSpec((None, HD, tk), lambda g, i, j: (A_V0 + g, 0, j))] + (comm.in_specs if hosted else []),
        out_specs=[qtr, qvec] + (comm.out_specs if hosted else []),
        scratch_shapes=[pltpu.VMEM((1, tq), F32), pltpu.VMEM((HE, tq), F32)] + (comm.scratch if hosted else []),
        compiler_params=_params(("arbitrary",) * 3 if hosted else ("parallel", "parallel", "arbitrary"), est),
    )(k, hT, hT, *(comm.parts if hosted else []))


def _attn_a_bwd(k, v, hT, doT, lse_row, delta_row, *, comm=None, tq=4096, tk=1024, cq=256):
    G, T, HD = k.shape
    tq, tk = min(tq, T), min(tk, T)
    cq = min(cq, tq)
    nqt = T // tq
    nq, nc = GQ * nqt, tq // cq
    nt = (((1,), (1,)), ((), ()))

    grid = (G, T // tk, nq)

    def body(*refs):
        (k_ref, v_ref, kT_ref, qT_ref, doT_ref, lse_ref, dl_ref, dkT_ref, dvT_ref, dqT_ref, dk_sc, dv_sc), \
            comm_start, comm_finish = _host_comm(comm, refs, 7, 3, 2, grid)
        j, i = pl.program_id(1), pl.program_id(2)
        comm_start()

        @pl.when((j == 0) & (i == 0))
        def _():
            dqT_ref[...] = jnp.zeros_like(dqT_ref)

        @pl.when(i == 0)
        def _():
            dk_sc[...] = jnp.zeros_like(dk_sc)
            dv_sc[...] = jnp.zeros_like(dv_sc)

        def scores(c):
            cols = slice(c * cq, (c + 1) * cq)
            return (jnp.dot(k_ref[...], qT_ref[:, cols], preferred_element_type=F32),
                    jnp.dot(v_ref[...], doT_ref[:, cols], preferred_element_type=F32))

        ahead = scores(0)
        dk_part = dv_part = None
        for c in range(nc):
            cols = slice(c * cq, (c + 1) * cq)
            sT, dpT = ahead
            if c + 1 < nc:
                ahead = scores(c + 1)
            pT = jnp.exp2(sT - lse_ref[:, cols])
            dsT = (pT * (dpT - dl_ref[:, cols])).astype(BF16)
            dv_c = lax.dot_general(doT_ref[:, cols], pT.astype(BF16), nt, preferred_element_type=F32)
            dk_c = lax.dot_general(qT_ref[:, cols], dsT, nt, preferred_element_type=F32)
            dv_part = dv_c if dv_part is None else dv_part + dv_c
            dk_part = dk_c if dk_part is None else dk_part + dk_c
            out_cols = pl.ds(pl.multiple_of((i % nqt) * tq + c * cq, cq), cq)
            dqT_ref[i // nqt, :, out_cols] += jnp.dot(kT_ref[...], dsT, preferred_element_type=F32)
        dk_sc[...] += dk_part
        dv_sc[...] += dv_part

        @pl.when(i == nq - 1)
        def _():
            dkT_ref[...] = dk_sc[...] * LN2
            dvT_ref[...] = dv_sc[...]

        comm_finish()

    krow = pl.BlockSpec((None, tk, HD), lambda g, j, i: (g, j, 0))
    ktr = pl.BlockSpec((None, HD, tk), lambda g, j, i: (g, 0, j))
    ktr_h = pl.BlockSpec((None, HD, tk), lambda g, j, i: (A_K0 + g, 0, j))
    qtr = pl.BlockSpec((None, HD, tq), lambda g, j, i: (g * GQ + i // nqt, 0, i % nqt))
    qvec = pl.BlockSpec((None, 1, tq), lambda g, j, i: (g * GQ + i // nqt, 0, i % nqt))
    whole = pl.BlockSpec((GQ, HD, T), lambda g, j, i: (g, 0, 0))
    est = 8 * _nbytes((cq, tk), F32) + 2 * _nbytes((GQ, HD, T), F32) + (8 << 20)
    hosted = comm is not None
    return pl.pallas_call(
        body, name="attn_a_bwd_comm" if hosted else "attn_a_bwd", grid=grid,
        out_shape=[jax.ShapeDtypeStruct((G, HD, T), F32), jax.ShapeDtypeStruct((G, HD, T), F32),
                   jax.ShapeDtypeStruct((G * GQ, HD, T), F32)] + (comm.out_shape if hosted else []),
        in_specs=[krow, krow, ktr_h, qtr, qtr, qvec, qvec] + (comm.in_specs if hosted else []),
        out_specs=[ktr, ktr, whole] + (comm.out_specs if hosted else []),
        scratch_shapes=[pltpu.VMEM((HD, tk), F32), pltpu.VMEM((HD, tk), F32)] + (comm.scratch if hosted else []),
        compiler_params=_params(("arbitrary", "arbitrary", "arbitrary"), est),
    )(k, v, hT, hT, doT, lse_row, delta_row, *(comm.parts if hosted else []))


WB = WINDOW
WK = 3 * WINDOW


QB_COL0 = (ROPE_W + KW) // (GQ * HEAD_DIM)
KB_COL = (ROPE_W + KW + QW) // KW
GW = GQ * HEAD_DIM


WSTEP = 2


def _win_in_specs(T):
    nb = T // WB
    assert nb % WSTEP == 0
    ns = nb // WSTEP
    q = [pl.BlockSpec((WSTEP * WB, GW), functools.partial(lambda n, g: (n, QB_COL0 + g), g=g)) for g in range(KV)]
    kv = [pl.BlockSpec((WB, KW), functools.partial(lambda n, o, c: (jnp.clip(WSTEP * n + o, 0, nb - 1), c), o=o, c=c))
          for c in (KB_COL, KB_COL + 1) for o in range(-1, WSTEP + 1)]
    bias = [pl.BlockSpec((None, KV, WK, GQ * WB), lambda n: (jnp.where(n == 0, 0, 1), 0, 0, 0)),
            pl.BlockSpec((None, KV, WK, GQ * WB), lambda n: (jnp.where(n == ns - 1, 2, 1), 0, 0, 0))]
    return ns, q + kv, bias


def _end_tables(biasT):
    key = lax.broadcasted_iota(jnp.int32, biasT.shape, 1)
    return jnp.stack([jnp.where(key < WB, NEG, biasT), biasT, jnp.where(key >= 2 * WB, NEG, biasT)])


def _heads_to_lanes(t):
    return jnp.concatenate([t[i * HEAD_DIM:(i + 1) * HEAD_DIM] for i in range(GQ)], axis=1)


def _lanes_to_heads(t):
    return jnp.concatenate([t[:, i * WB:(i + 1) * WB] for i in range(GQ)], axis=0)


def _attn_b_fwd(h, biasT, sink_rows):
    T = h.shape[0]
    ns, in_specs, bias_specs = _win_in_specs(T)
    nkv = WSTEP + 2

    def body(*refs):
        q_refs, k_refs, v_refs = refs[:KV], refs[KV:KV + nkv], refs[KV + nkv:KV + 2 * nkv]
        b_refs, sk_ref, o_ref, lse_ref = refs[KV + 2 * nkv:KV + 2 * nkv + WSTEP], *refs[KV + 2 * nkv + WSTEP:]
        ks, vs = [r[...] for r in k_refs], [r[...] for r in v_refs]
        outs = []
        for b, b_ref in enumerate(b_refs):
            rows = slice(b * WB, (b + 1) * WB)
            kwin = jnp.concatenate(ks[b:b + 3], axis=0)
            vT = jnp.concatenate(vs[b:b + 3], axis=0).T
            qT = [_heads_to_lanes((q[rows, :] * SCALE).T).astype(BF16) for q in q_refs]
            sT = [jnp.dot(kwin[:, g * HEAD_DIM:(g + 1) * HEAD_DIM].astype(BF16), qT[g], preferred_element_type=F32)
                  for g in range(KV)]
            oT = []
            for g in range(KV):
                s = sT[g] + b_ref[g]
                sk = sk_ref[g]
                m = jnp.maximum(jnp.max(s, axis=0, keepdims=True), sk)
                p = jnp.exp(s - m)
                den = jnp.sum(p, axis=0, keepdims=True) + jnp.exp(sk - m)
                o = jnp.dot(vT[g * HEAD_DIM:(g + 1) * HEAD_DIM].astype(BF16), p.astype(BF16),
                            preferred_element_type=F32) / den
                lse_ref[b, g] = m + jnp.log(den)
                oT.append(_lanes_to_heads(o))
            outs.append(jnp.concatenate(oT, axis=0).T)
        o_ref[...] = jnp.concatenate(outs, axis=0)

    whole = lambda *shape: pl.BlockSpec(shape, lambda n: (0,) * len(shape))
    return pl.pallas_call(
        body, name="attn_b_fwd", grid=(ns,),
        out_shape=(jax.ShapeDtypeStruct((T, QW), F32), jax.ShapeDtypeStruct((ns * WSTEP, KV, 1, GQ * WB), F32)),
        in_specs=in_specs + bias_specs + [whole(KV, 1, GQ * WB)],
        out_specs=(pl.BlockSpec((WSTEP * WB, QW), lambda n: (n, 0)),
                   pl.BlockSpec((WSTEP, KV, 1, GQ * WB), lambda n: (n, 0, 0, 0))),
        compiler_params=_params(("parallel",), 32 << 20),
    )(*([h] * (KV + 2 * nkv)), *([biasT] * WSTEP), sink_rows)


def _attn_b_bwd(h, do, o, lse, biasT, sink_rows):
    T = h.shape[0]
    ns, in_specs, bias_specs = _win_in_specs(T)
    nkv = WSTEP + 2
    Tp = T + 2 * WB
    nt = (((1,), (1,)), ((), ()))

    def body(*refs):
        q_refs, k_refs, v_refs = refs[:KV], refs[KV:KV + nkv], refs[KV + nkv:KV + 2 * nkv]
        at = KV + 2 * nkv
        do_ref, o_ref, lse_ref = refs[at:at + 3]
        b_refs, sk_ref = refs[at + 3:at + 3 + WSTEP], refs[at + 3 + WSTEP]
        dq_ref, dk_ref, dv_ref, db_ref, dsk_ref = refs[at + 4 + WSTEP:]
        n = pl.program_id(0)

        @pl.when(n == 0)
        def _():
            dk_ref[...] = jnp.zeros_like(dk_ref)
            dv_ref[...] = jnp.zeros_like(dv_ref)
            db_ref[...] = jnp.zeros_like(db_ref)
            dsk_ref[...] = jnp.zeros_like(dsk_ref)

        ks, vs = [r[...] for r in k_refs], [r[...] for r in v_refs]
        for b, b_ref in enumerate(b_refs):
            rows = slice(b * WB, (b + 1) * WB)
            kwin = jnp.concatenate(ks[b:b + 3], axis=0)
            vwin = jnp.concatenate(vs[b:b + 3], axis=0)
            kT = kwin.T
            doT_all, oT_all = do_ref[rows, :].T, o_ref[rows, :].T
            qT, doT, delta, sT, dpT = [], [], [], [], []
            for g, q in enumerate(q_refs):
                hd = slice(g * HEAD_DIM, (g + 1) * HEAD_DIM)
                qT.append(_heads_to_lanes((q[rows, :] * SCALE).T).astype(BF16))
                d = _heads_to_lanes(doT_all[g * GW:(g + 1) * GW])
                delta.append(jnp.sum(d * _heads_to_lanes(oT_all[g * GW:(g + 1) * GW]), axis=0, keepdims=True))
                doT.append(d.astype(BF16))
                sT.append(jnp.dot(kwin[:, hd].astype(BF16), qT[g], preferred_element_type=F32))
                dpT.append(jnp.dot(vwin[:, hd].astype(BF16), doT[g], preferred_element_type=F32))
            dq, dk, dv = [], [], []
            for g in range(KV):
                lse_g = lse_ref[b, g]
                p = jnp.exp(sT[g] + b_ref[g] - lse_g)
                ds = p * (dpT[g] - delta[g])
                db_ref[g] += ds
                dsk_ref[g] -= jnp.exp(sk_ref[g] - lse_g) * delta[g]
                dsb = ds.astype(BF16)
                dqT = jnp.dot(kT[g * HEAD_DIM:(g + 1) * HEAD_DIM].astype(BF16), dsb, preferred_element_type=F32)
                dq.append(_lanes_to_heads(dqT))
                dk.append(lax.dot_general(dsb, qT[g], nt, preferred_element_type=F32))
                dv.append(lax.dot_general(p.astype(BF16), doT[g], nt, preferred_element_type=F32))
            dq_ref[rows, :] = (jnp.concatenate(dq, axis=0).T * SCALE).astype(BF16)
            win = pl.ds(pl.multiple_of((WSTEP * n + b) * WB, WB), WK)
            dk_ref[win, :] += jnp.concatenate(dk, axis=1)
            dv_ref[win, :] += jnp.concatenate(dv, axis=1)

    whole = lambda *shape: pl.BlockSpec(shape, lambda n: (0,) * len(shape))
    tok = pl.BlockSpec((WSTEP * WB, QW), lambda n: (n, 0))
    return pl.pallas_call(
        body, name="attn_b_bwd", grid=(ns,),
        out_shape=(jax.ShapeDtypeStruct((T, QW), BF16),
                   jax.ShapeDtypeStruct((Tp, KW), F32), jax.ShapeDtypeStruct((Tp, KW), F32),
                   jax.ShapeDtypeStruct((KV, WK, GQ * WB), F32), jax.ShapeDtypeStruct((KV, 1, GQ * WB), F32)),
        in_specs=in_specs + [tok, tok, pl.BlockSpec((WSTEP, KV, 1, GQ * WB), lambda n: (n, 0, 0, 0))]
        + bias_specs + [whole(KV, 1, GQ * WB)],
        out_specs=(tok, whole(Tp, KW), whole(Tp, KW), whole(KV, WK, GQ * WB), whole(KV, 1, GQ * WB)),
        compiler_params=_params(("arbitrary",), 48 << 20),
    )(*([h] * (KV + 2 * nkv)), do, o, lse, *([biasT] * WSTEP), sink_rows)


def _bias_table(rel_bias_t, bucket):
    nh, n = rel_bias_t.shape[0], bucket.shape[1]

    def body(rb_ref, bk_ref, o_ref):
        bk = bk_ref[...]
        out = jnp.full((nh, n), NEG, F32)
        for b in range(N_BUCKETS):
            out = jnp.where(bk == b, rb_ref[:, b:b + 1], out)
        o_ref[...] = out

    return pl.pallas_call(
        body, name="bias_table", out_shape=jax.ShapeDtypeStruct((nh, n), F32),
        compiler_params=pltpu.CompilerParams(vmem_limit_bytes=32 << 20),
    )(rel_bias_t, bucket)


def _bias_sink_grads(db_list, dsk_list, bucket):
    L = len(db_list)

    def body(*refs):
        db_refs, dsk_refs, bk_ref = refs[:L], refs[L:2 * L], refs[2 * L]
        drb_ref, dsink_ref = refs[2 * L + 1], refs[2 * L + 2]
        tot = db_refs[0][...]
        for r in db_refs[1:]:
            tot = tot + r[...]
        bk = bk_ref[...]
        lane = lax.broadcasted_iota(jnp.int32, (2 * GQ, N_BUCKETS), 1)
        out = jnp.zeros((2 * GQ, N_BUCKETS), F32)
        for b in range(N_BUCKETS):
            sb = jnp.sum(jnp.where(bk == b, tot, 0.0), axis=1, keepdims=True)
            out = jnp.where(lane == b, sb, out)
        drb_ref[...] = out
        for l in range(L):
            dsink_ref[l] = jnp.sum(dsk_refs[l][...], axis=1, keepdims=True)

    return pl.pallas_call(
        body, name="bias_sink_grads",
        out_shape=(jax.ShapeDtypeStruct((2 * GQ, N_BUCKETS), F32), jax.ShapeDtypeStruct((L, 2 * GQ, 1), F32)),
        compiler_params=pltpu.CompilerParams(vmem_limit_bytes=32 << 20),
    )(*db_list, *dsk_list, bucket)


def _outnorm_fwd(oaT, ob, ga, gb, *, tm=512):
    T = ob.shape[0]
    tm = min(tm, T)

    def body(oaT_ref, ob_ref, ga_ref, gb_ref, y_ref):
        for j, (o, g_ref) in enumerate(((oaT_ref[...].T, ga_ref), (ob_ref[...], gb_ref))):
            r = lax.rsqrt(jnp.mean(o * o, axis=1, keepdims=True) + RMS_EPS)
            y_ref[:, j * QW:(j + 1) * QW] = (o * r * g_ref[...]).astype(BF16)

    half = pl.BlockSpec((tm, QW), lambda i: (i, 0))
    halfT = pl.BlockSpec((QW, tm), lambda i: (0, i))
    vec = pl.BlockSpec((1, QW), lambda i: (0, 0))
    return pl.pallas_call(
        body, name="outnorm_fwd", grid=(T // tm,),
        out_shape=jax.ShapeDtypeStruct((T, 2 * QW), BF16),
        in_specs=[halfT, half, vec, vec], out_specs=pl.BlockSpec((tm, 2 * QW), lambda i: (i, 0)),
        compiler_params=_params(("parallel",), 16 << 20),
    )(oaT, ob, ga, gb)


def _outnorm_bwd(dy, oaT, ob, ga, gb, *, tm=512):
    T = ob.shape[0]
    tm = min(tm, T)
    nh = QW // HEAD_DIM

    def body(dy_ref, oaT_ref, ob_ref, ga_ref, gb_ref, doaT_ref, dl_ref, dob_ref, dg_ref):
        i = pl.program_id(0)

        @pl.when(i == 0)
        def _():
            dg_ref[...] = jnp.zeros_like(dg_ref)

        oaT = oaT_ref[...]
        for j, (o, g_ref) in enumerate(((oaT.T, ga_ref), (ob_ref[...], gb_ref))):
            d = dy_ref[:, j * QW:(j + 1) * QW]
            r = lax.rsqrt(jnp.mean(o * o, axis=1, keepdims=True) + RMS_EPS)
            orr = o * r
            dg_ref[j:j + 1, :] += jnp.sum(d * orr, axis=0, keepdims=True)
            dgv = d * g_ref[...]
            do = r * dgv - orr * (r * r) * jnp.mean(dgv * o, axis=1, keepdims=True)
            if j == 0:
                doT = do.T
                doaT_ref[...] = doT.astype(BF16)
                prod = doT * oaT
                dl_ref[...] = jnp.concatenate(
                    [jnp.sum(prod[a * HEAD_DIM:(a + 1) * HEAD_DIM], axis=0, keepdims=True) for a in range(nh)], axis=0)
            else:
                dob_ref[...] = do

    half = pl.BlockSpec((tm, QW), lambda i: (i, 0))
    halfT = pl.BlockSpec((QW, tm), lambda i: (0, i))
    vec = pl.BlockSpec((1, QW), lambda i: (0, 0))
    return pl.pallas_call(
        body, name="outnorm_bwd", grid=(T // tm,),
        out_shape=(jax.ShapeDtypeStruct((QW, T), BF16), jax.ShapeDtypeStruct((nh, T), F32),
                   jax.ShapeDtypeStruct((T, QW), F32), jax.ShapeDtypeStruct((8, QW), F32)),
        in_specs=[pl.BlockSpec((tm, 2 * QW), lambda i: (i, 0)), halfT, half, vec, vec],
        out_specs=(halfT, pl.BlockSpec((nh, tm), lambda i: (0, i)), half, pl.BlockSpec((8, QW), lambda i: (0, 0))),
        compiler_params=_params(("arbitrary",), 32 << 20),
    )(dy, oaT, ob, ga, gb)


GELU_C = math.sqrt(2.0 / math.pi)
GELU_A = 0.044715
HALO = 16
SUB = 8


def _gelu_parts(x):
    x2 = x * x
    t = jnp.tanh(x * (GELU_C + (GELU_C * GELU_A) * x2))
    return 0.5 * (1.0 + t), t, x2


def _halo_specs(tm, tn, T):
    nh = tm // HALO
    last = T // HALO - 1
    cur = pl.BlockSpec((tm, tn), lambda j, i: (i, j))
    prev = pl.BlockSpec((HALO, tn), lambda j, i: (jnp.maximum(i * nh - 1, 0), j))
    nxt = pl.BlockSpec((HALO, tn), lambda j, i: (jnp.minimum((i + 1) * nh, last), j))
    return cur, prev, nxt


def _conv_glu_fwd(g, u, conv_w, conv_b, *, tm=256, tn=1408):
    T, F = g.shape
    tm, tn = min(tm, T), min(tn, F)
    cur, prev, nxt = _halo_specs(tm, tn, T)

    def body(g_ref, gp_ref, gn_ref, u_ref, w_ref, b_ref, a_ref):
        i = pl.program_id(1)
        gv = g_ref[...]
        before = jnp.where(i * tm > 0, gp_ref[HALO - SUB:, :], 0.0)
        after = jnp.where((i + 1) * tm < T, gn_ref[0:SUB, :], 0.0)
        gm1 = pltpu.roll(jnp.concatenate([before, gv], axis=0), 1, 0)[SUB:]
        gp1 = pltpu.roll(jnp.concatenate([gv, after], axis=0), tm + SUB - 1, 0)[:tm]
        gc = ((b_ref[...] + gm1 * w_ref[0:1, :]) + gv * w_ref[1:2, :]) + gp1 * w_ref[2:3, :]
        cdf, _, _ = _gelu_parts(gc)
        a_ref[...] = (gc * cdf * u_ref[...].astype(F32)).astype(BF16)

    wspec = pl.BlockSpec((8, tn), lambda j, i: (0, j))
    est = 2 * (3 * _nbytes((tm, tn), F32)) + 8 * _nbytes((tm, tn), F32)
    return pl.pallas_call(
        body, name="conv_glu_fwd", grid=(F // tn, T // tm),
        out_shape=jax.ShapeDtypeStruct((T, F), BF16),
        in_specs=[cur, prev, nxt, cur, wspec, pl.BlockSpec((1, tn), lambda j, i: (0, j))],
        out_specs=cur,
        compiler_params=_params(("parallel", "parallel"), est),
    )(g, g, g, u, conv_w, conv_b)


def _conv_glu_bwd(dact, g, u, conv_w, conv_b, *, tm=256, tn=1408):
    T, F = g.shape
    tm, tn = min(tm, T), min(tn, F)
    cur, prev, nxt = _halo_specs(tm, tn, T)
    te = tm + 2 * HALO

    def body(d_ref, dp_ref, dn_ref, g_ref, gp_ref, gn_ref, u_ref, up_ref, un_ref, w_ref, b_ref,
             dg_ref, du_ref, dc_ref):
        i = pl.program_id(1)

        @pl.when(i == 0)
        def _():
            dc_ref[...] = jnp.zeros_like(dc_ref)

        has_prev, has_next = i * tm > 0, (i + 1) * tm < T
        ge = jnp.concatenate([jnp.where(has_prev, gp_ref[...], 0.0), g_ref[...],
                              jnp.where(has_next, gn_ref[...], 0.0)], axis=0)
        ue = jnp.concatenate([up_ref[...], u_ref[...], un_ref[...]], axis=0).astype(F32)
        de = jnp.concatenate([jnp.where(has_prev, dp_ref[...].astype(F32), 0.0), d_ref[...].astype(F32),
                              jnp.where(has_next, dn_ref[...].astype(F32), 0.0)], axis=0)
        w0, w1, w2 = w_ref[0:1, :], w_ref[1:2, :], w_ref[2:3, :]
        gm1 = pltpu.roll(ge, 1, 0)
        gp1 = pltpu.roll(ge, te - 1, 0)
        gc = ((b_ref[...] + gm1 * w0) + ge * w1) + gp1 * w2
        cdf, t, gc2 = _gelu_parts(gc)
        dgelu = cdf + (0.5 * gc) * (1.0 - t * t) * (GELU_C + (3.0 * GELU_C * GELU_A) * gc2)
        dgc = de * ue * dgelu
        dge = w0 * pltpu.roll(dgc, te - 1, 0) + w1 * dgc + w2 * pltpu.roll(dgc, 1, 0)
        mid = slice(HALO, HALO + tm)
        dg_ref[...] = dge[mid].astype(BF16)
        du_ref[...] = (de[mid] * (gc[mid] * cdf[mid])).astype(BF16)
        dgm = dgc[mid]
        dc_ref[0:1, :] += jnp.sum(dgm * gm1[mid], axis=0, keepdims=True)
        dc_ref[1:2, :] += jnp.sum(dgm * ge[mid], axis=0, keepdims=True)
        dc_ref[2:3, :] += jnp.sum(dgm * gp1[mid], axis=0, keepdims=True)
        dc_ref[3:4, :] += jnp.sum(dgm, axis=0, keepdims=True)

    wspec = pl.BlockSpec((8, tn), lambda j, i: (0, j))
    est = 2 * (3 * _nbytes((tm, tn), F32) + 2 * _nbytes((tm, tn), BF16)) + 16 * _nbytes((te, tn), F32)
    return pl.pallas_call(
        body, name="conv_glu_bwd", grid=(F // tn, T // tm),
        out_shape=(jax.ShapeDtypeStruct((T, F), BF16), jax.ShapeDtypeStruct((T, F), BF16),
                   jax.ShapeDtypeStruct((8, F), F32)),
        in_specs=[cur, prev, nxt, cur, prev, nxt, cur, prev, nxt, wspec, pl.BlockSpec((1, tn), lambda j, i: (0, j))],
        out_specs=(cur, cur, wspec),
        compiler_params=_params(("parallel", "arbitrary"), est),
    )(dact, dact, dact, g, g, g, u, u, u, conv_w, conv_b)


def _adamw_math(w, g, m, v):
    m = ADAM_B1 * m + (1.0 - ADAM_B1) * g
    v = ADAM_B2 * v + (1.0 - ADAM_B2) * (g * g)
    m_hat = m / (1.0 - ADAM_B1 ** ADAM_STEP)
    v_hat = v / (1.0 - ADAM_B2 ** ADAM_STEP)
    delta = -ADAM_LR * (m_hat / (jnp.sqrt(v_hat) + ADAM_EPS) + ADAM_WD * w)
    return delta, m, v


def _adamw(w, m, v, gparts, *, name, tr):
    R, C = w.shape
    tr = min(tr, R)
    assert R % tr == 0

    def body(w_ref, m_ref, v_ref, gp_ref, g_ref, d_ref, nm_ref, nv_ref):
        g = gp_ref[0].astype(F32)
        for j in range(1, N_DEV):
            g = g + gp_ref[j].astype(F32)
        delta, nm, nv = _adamw_math(w_ref[...], g, m_ref[...], v_ref[...])
        g_ref[...] = g
        d_ref[...] = delta
        nm_ref[...] = nm
        nv_ref[...] = nv

    blk = pl.BlockSpec((tr, C), lambda i: (i, 0))
    out = jax.ShapeDtypeStruct((R, C), F32)
    return pl.pallas_call(
        body, name=name, grid=(R // tr,), out_shape=(out, out, out, out),
        in_specs=[blk, blk, blk, pl.BlockSpec((N_DEV, tr, C), lambda i: (0, i, 0))],
        out_specs=(blk, blk, blk, blk),
        compiler_params=_params(("parallel",), 24 << 20),
    )(w, m, v, gparts)


def _rope_tables(T):
    rows_n = T // GRID_W
    row = jnp.repeat(jnp.arange(rows_n, dtype=F32), GRID_W)
    col = jnp.tile(jnp.arange(GRID_W, dtype=F32), rows_n)
    half = HEAD_DIM // 2
    inv_freq = ROPE_THETA ** (-jnp.arange(0, half, 2, dtype=F32) / half)
    ang = jnp.concatenate([row[:, None] * inv_freq, col[:, None] * inv_freq], axis=-1)
    cos, sin = jnp.cos(ang), jnp.sin(ang)
    cos64 = jnp.repeat(cos, 2, axis=-1)
    sin64 = jnp.stack([-sin, sin], axis=-1).reshape(T, HEAD_DIM)
    return jnp.tile(cos64, (1, 2)), jnp.tile(sin64, (1, 2))


def _t5_bucket(rel):
    half = N_BUCKETS // 2
    max_exact = half // 2
    bucket = jnp.where(rel > 0, half, 0)
    rp = jnp.abs(rel)
    rpf = jnp.maximum(rp, 1).astype(F32)
    large = max_exact + (jnp.log(rpf / max_exact) / math.log(MAX_DISTANCE / max_exact)
                         * (half - max_exact)).astype(jnp.int32)
    large = jnp.minimum(large, half - 1)
    return bucket + jnp.where(rp < max_exact, rp, large)


def _window_buckets():
    qpos = jnp.arange(WB, dtype=jnp.int32)
    kpos = jnp.arange(WK, dtype=jnp.int32) - WB
    rel = kpos[None, :] - qpos[:, None]
    return jnp.where(jnp.abs(rel) <= WINDOW, _t5_bucket(rel), -1)


def _heads_first(a, nh):
    T = a.shape[0]
    return a.reshape(T, nh, HEAD_DIM).transpose(1, 0, 2)


def _row(v):
    return v.reshape(1, -1)


def _rows8(rows, width):
    a = jnp.stack(list(rows), axis=0)
    return jnp.pad(a, ((0, 8 - a.shape[0]), (0, 0)))


def _layer_fwd(l, xin, W, tabs, comm=None, on_comm=None):
    xhat, xg, xb, x16 = xin
    T = xhat.shape[0]
    cos2, sin2, biasT = tabs
    h = _mm([x16], [W["w_in"][l]], name="mm_in", out_dtype=F32, tm=MM_ROWS, tn=IN_COLS, tk=D_MODEL)
    gains = _rows8([jnp.tile(W["q_norm"][l], 2), jnp.tile(W["k_norm"][l], 2)], LANES)
    hT, kv_nat = _qk_rope_fwd(h, gains, cos2, sin2, name="qk_rope_fwd")
    hT = hT.reshape(A_HEADS, HEAD_DIM, T)
    ka, va = _heads_first(kv_nat[:, :KW], KV), _heads_first(kv_nat[:, KW:], KV)
    res = _attn_a_fwd(ka, hT, comm=comm)
    oaT, lse_a = res[0].reshape(QW, T), res[1]
    if comm is not None:
        on_comm(res[2:])
    sink_rows = jnp.repeat(W["sink"][l], WB).reshape(KV, 1, GQ * WB)
    ob_t, lse_b = _attn_b_fwd(h, biasT, sink_rows)
    ga, gb = _row(W["out_norm_a"][l]), _row(W["out_norm_b"][l])
    ycat = _outnorm_fwd(oaT, ob_t, ga, gb)
    g1, b1 = _row(W["ln1_g"][l]), _row(W["ln1_b"][l])
    x1hat, rstd1, x1_16 = _mm_res_ln(ycat, W["w_out"][l], xhat, xg, xb, g1, b1, name="mm_out_ln", tm=512)
    gate = _mm([x1_16], [W["w_gate"][l]], name="mm_gate", out_dtype=F32, tm=MM_ROWS_WIDE, tn=D_FF, tk=D_MODEL)
    up = _mm([x1_16], [W["w_up"][l]], name="mm_up", out_dtype=BF16, tm=MM_ROWS_WIDE, tn=D_FF, tk=D_MODEL)
    cw = jnp.pad(W["conv_w"][l], ((0, 5), (0, 0)))
    cb = _row(W["conv_b"][l])
    act = _conv_glu_fwd(gate, up, cw, cb)
    g2, b2 = _row(W["ln2_g"][l]), _row(W["ln2_b"][l])
    x2hat, rstd2, x2_16 = _mm_res_ln(act, W["w_down"][l], x1hat, g1, b1, g2, b2, name="mm_down_ln", tm=256)
    saved = dict(x16=x16, h=h, gains=gains, hT=hT, ka=ka, va=va, oaT=oaT, lse_a=lse_a,
                 lse_b=lse_b, sink_rows=sink_rows, ob_t=ob_t,
                 ga=ga, gb=gb, ycat=ycat, x1hat=x1hat, rstd1=rstd1, x1_16=x1_16, g1=g1, b1=b1, gate=gate, up=up,
                 cw=cw, cb=cb, act=act, x2hat=x2hat, rstd2=rstd2, g2=g2, b2=b2)
    return (x2hat, g2, b2, x2_16), saved


def _layer_bwd(l, S, W, tabs, dz2, dz2_16, stats2, scatter=None):
    cos2, sin2, biasT = tabs
    T = dz2.shape[0]
    G = {}
    G["ln2_g"], G["ln2_b"] = stats2[0], stats2[1]
    G["w_down"] = _mm([S["act"]], [dz2_16], name="dw_down", out_dtype=BF16, trans_a=True, tm=D_FF // 2, tn=D_MODEL, tk=DW_TOKENS)
    dact = _mm([dz2_16], [W["w_down"][l]], name="mm_dact", out_dtype=BF16, trans_b=True, tm=MM_ROWS_WIDE, tn=D_FF,
               tk=D_MODEL)
    dg, du, dconv = _conv_glu_bwd(dact, S["gate"], S["up"], S["cw"], S["cb"])
    G["conv_w"], G["conv_b"] = dconv[0:3], dconv[3]
    G["w_gate"] = _mm([S["x1_16"]], [dg], name="dw_gate", out_dtype=BF16, trans_a=True, tm=D_MODEL, tn=D_FF // 2, tk=DW_TOKENS)
    G["w_up"] = _mm([S["x1_16"]], [du], name="dw_up", out_dtype=BF16, trans_a=True, tm=D_MODEL, tn=D_FF // 2, tk=DW_TOKENS)
    dx1 = _mm([dg, du], [W["w_gate"][l], W["w_up"][l]], name="mm_dx1", out_dtype=F32, trans_b=True, tm=MM_ROWS_WIDE,
              tn=D_MODEL, tk=D_FF, add=dz2, add_scale=ALPHA)
    dz1, dz1_16, stats1 = _ln_bwd(S["x1hat"], S["rstd1"], S["g1"], S["b1"], name="ln1_bwd", dx=dx1)
    G["ln1_g"], G["ln1_b"] = stats1[0], stats1[1]
    G["w_out"] = _mm([S["ycat"]], [dz1_16], name="dw_out", out_dtype=BF16, trans_a=True, tm=D_MODEL, tn=D_MODEL, tk=DW_TOKENS)
    dycat = _mm([dz1_16], [W["w_out"][l]], name="mm_dycat", out_dtype=F32, trans_b=True, tm=MM_ROWS, tn=D_MODEL,
                tk=D_MODEL)
    doaT, delta, dob_t, dgn = _outnorm_bwd(dycat, S["oaT"], S["ob_t"], S["ga"], S["gb"])
    G["out_norm_a"], G["out_norm_b"] = dgn[0], dgn[1]
    res = _attn_a_bwd(S["ka"], S["va"], S["hT"], doaT.reshape(KV * GQ, HEAD_DIM, T), S["lse_a"],
                      delta.reshape(KV * GQ, 1, T), comm=scatter(G) if scatter is not None else None)
    dkaT, dvaT, dqaT = res[:3]
    dh_rope, dgain = _qk_rope_bwd(S["h"], dqaT.reshape(QW, T), dkaT.reshape(KW, T), S["gains"], cos2, sin2,
                                  name="qk_rope_bwd")
    G["q_norm"], G["k_norm"] = dgain[0, :HEAD_DIM], dgain[1, :HEAD_DIM]
    dqb_t, dkb, dvb, dbiasT, dsk = _attn_b_bwd(S["h"], dob_t, S["ob_t"], S["lse_b"], biasT, S["sink_rows"])
    dh = jnp.concatenate([
        dh_rope, dvaT.transpose(2, 0, 1).reshape(T, KW).astype(BF16), dqb_t,
        dkb[WB:WB + T].astype(BF16), dvb[WB:WB + T].astype(BF16)], axis=1)
    dbias = dbiasT.reshape(KV, WK, GQ, WB).transpose(0, 2, 1, 3)
    G["w_in"] = _mm([S["x16"]], [dh], name="dw_in", out_dtype=BF16, trans_a=True, tm=D_MODEL, tn=IN_COLS, tk=DW_TOKENS)
    dxin = _mm([dh], [W["w_in"][l]], name="mm_dxin", out_dtype=F32, trans_b=True, tm=MM_ROWS, tn=D_MODEL, tk=IN_COLS,
               add=dz1, add_scale=ALPHA)
    return dxin, G, dbias.reshape(KV * GQ, WK * WB), dsk.reshape(KV * GQ, WB), res[3:]


BIG = ("w_in", "w_out", "w_gate", "w_up", "w_down")
COL_SHARDED = ("w_in", "w_gate", "w_up")


def _unshard(name, blocks):
    _, r, c = blocks.shape
    if name in COL_SHARDED:
        return blocks.transpose(1, 0, 2).reshape(r, N_DEV * c)
    return blocks.reshape(N_DEV * r, c)


def _to_owner_blocks(name, full, shard_shape):
    _, r, c = shard_shape
    if name in COL_SHARDED:
        return full.reshape(r, N_DEV, c).transpose(1, 0, 2)
    return full.reshape(N_DEV, r, c)


def _pack_small(vals, tail):
    flat = jnp.concatenate([vals[n].reshape(-1).astype(F32) for n in SMALL_NAMES] + [tail])
    pad = (-flat.shape[0]) % (8 * LANES)
    return jnp.pad(flat, (0, pad)).reshape(-1, LANES)


def _unpack_small(packed, shapes):
    flat = packed.reshape(-1)
    out, off = {}, 0
    for n in SMALL_NAMES:
        size = math.prod(shapes[n])
        out[n] = flat[off:off + size].reshape(shapes[n])
        off += size
    return out, flat[off]


def kernel(x, rel_bias, w_in, q_norm, k_norm, sink, out_norm_a, out_norm_b, w_out, ln1_g, ln1_b, w_gate, w_up, conv_w, conv_b, w_down, ln2_g, ln2_b, loss_target, m_rel_bias, m_w_in, m_q_norm, m_k_norm, m_sink, m_out_norm_a, m_out_norm_b, m_w_out, m_ln1_g, m_ln1_b, m_w_gate, m_w_up, m_conv_w, m_conv_b, m_w_down, m_ln2_g, m_ln2_b, v_rel_bias, v_w_in, v_q_norm, v_k_norm, v_sink, v_out_norm_a, v_out_norm_b, v_w_out, v_ln1_g, v_ln1_b, v_w_gate, v_w_up, v_conv_w, v_conv_b, v_w_down, v_ln2_g, v_ln2_b):
    P = dict(rel_bias=rel_bias, w_in=w_in, q_norm=q_norm, k_norm=k_norm, sink=sink, out_norm_a=out_norm_a,
             out_norm_b=out_norm_b, w_out=w_out, ln1_g=ln1_g, ln1_b=ln1_b, w_gate=w_gate, w_up=w_up, conv_w=conv_w,
             conv_b=conv_b, w_down=w_down, ln2_g=ln2_g, ln2_b=ln2_b)
    M = dict(rel_bias=m_rel_bias, w_in=m_w_in, q_norm=m_q_norm, k_norm=m_k_norm, sink=m_sink, out_norm_a=m_out_norm_a,
             out_norm_b=m_out_norm_b, w_out=m_w_out, ln1_g=m_ln1_g, ln1_b=m_ln1_b, w_gate=m_w_gate, w_up=m_w_up,
             conv_w=m_conv_w, conv_b=m_conv_b, w_down=m_w_down, ln2_g=m_ln2_g, ln2_b=m_ln2_b)
    V = dict(rel_bias=v_rel_bias, w_in=v_w_in, q_norm=v_q_norm, k_norm=v_k_norm, sink=v_sink, out_norm_a=v_out_norm_a,
             out_norm_b=v_out_norm_b, w_out=v_w_out, ln1_g=v_ln1_g, ln1_b=v_ln1_b, w_gate=v_w_gate, w_up=v_w_up,
             conv_w=v_conv_w, conv_b=v_conv_b, w_down=v_w_down, ln2_g=v_ln2_g, ln2_b=v_ln2_b)
    names = list(P)
    T = x.shape[1]
    me = 4 * lax.axis_index("x") + 2 * lax.axis_index("y") + lax.axis_index("c")

    L, taps, fc = conv_w.shape
    W = {n: ([None] * DEPTH if n in BIG else P[n]) for n in names}

    def wire(n, l):
        return P[n][l].astype(BF16)

    def take(n, l, gathered):
        W[n][l] = _unshard(n, gathered)

    take("w_in", 0, _exchange([wire("w_in", 0)], [True], name="gather_w_in0")[0])
    later = [(n, l) for l in range(DEPTH) for n in BIG if (n, l) != ("w_in", 0)]
    cw_shard = conv_w.reshape(-1)
    cw_wire = jnp.pad(cw_shard, (0, (-cw_shard.shape[0]) % LANES)).reshape(-1, LANES)
    gather_rest = _Comm([wire(n, l) for n, l in later] + [cw_wire], [True] * (len(later) + 1))

    def on_gathered(outs):
        for (n, l), g in zip(later, outs):
            take(n, l, g)
        cw_all = outs[-1].reshape(N_DEV, -1)[:, :cw_shard.shape[0]].reshape(N_DEV, L, taps, fc)
        W["conv_w"] = cw_all.transpose(1, 2, 0, 3).reshape(L, taps, N_DEV * fc)

    cos2, sin2 = _rope_tables(T)
    bucket = _window_buckets()
    bias = _bias_table(rel_bias.T, bucket.reshape(1, WB * WK))
    biasT = bias.reshape(KV, GQ, WB, WK).transpose(0, 3, 1, 2).reshape(KV, WK, GQ * WB)
    biasT = _end_tables(biasT)
    tabs = (cos2, sin2, biasT)

    ones, zeros = jnp.ones((1, D_MODEL), F32), jnp.zeros((1, D_MODEL), F32)
    cur = (x[0], ones, zeros, x[0].astype(BF16))
    saved = []
    for l in range(DEPTH):
        cur, S = _layer_fwd(l, cur, W, tabs, comm=gather_rest if l == 0 else None, on_comm=on_gathered)
        saved.append(S)

    def owner_blocks(n, l):
        return _to_owner_blocks(n, grads[l][n], P[n].shape)

    early = ([(n, l) for l in range(1, DEPTH) for n in BIG] + [(n, 0) for n in BIG if n != "w_in"])

    def scatter_early(g0):
        grads[0] = g0
        return _Comm([owner_blocks(n, l) for n, l in early], [False] * len(early))

    grads = [None] * DEPTH
    dbs, dsks = [None] * DEPTH, [None] * DEPTH
    S = saved[-1]
    dz, dz16, stats = _ln_bwd(S["x2hat"], S["rstd2"], S["g2"], S["b2"], name="loss_ln2_bwd", target=loss_target[0])
    loss_part = stats[2, 0:1]
    recv = {}
    for l in reversed(range(DEPTH)):
        S = saved[l]
        dxin, grads[l], dbs[l], dsks[l], got = _layer_bwd(l, S, W, tabs, dz, dz16, stats,
                                                         scatter=scatter_early if l == 0 else None)
        if l == 0:
            recv.update(zip(early, got))
        if l > 0:
            Sp = saved[l - 1]
            dz, dz16, stats = _ln_bwd(Sp["x2hat"], Sp["rstd2"], Sp["g2"], Sp["b2"], name="ln2_bwd", dx=dxin)
    grad_x = dxin[None]

    drb, dsink = _bias_sink_grads(dbs, dsks, bucket.T.reshape(1, WK * WB))
    small_g = {n: jnp.stack([grads[l][n] for l in range(DEPTH)]) for n in SMALL_NAMES if n not in ("rel_bias", "sink")}
    small_g["rel_bias"] = drb.T
    small_g["sink"] = dsink.reshape(DEPTH, KV * GQ)
    recv[("w_in", 0)], small_recv = _exchange([owner_blocks("w_in", 0), _pack_small(small_g, loss_part)], [False, True],
                                              name="scatter_w_in0_gather_small")

    out_g, out_d, out_m, out_v = {}, {}, {}, {}
    for n in BIG:
        shp = P[n].shape
        gparts = jnp.concatenate([recv[(n, l)] for l in range(DEPTH)], axis=1)
        rows, cols = shp[0] * shp[1], shp[2]
        res = _adamw(P[n].reshape(rows, cols), M[n].reshape(rows, cols), V[n].reshape(rows, cols), gparts,
                     name="adamw_" + n, tr=math.gcd(rows, 256))
        out_g[n], out_d[n], out_m[n], out_v[n] = (r.reshape(shp) for r in res)
    full_shapes = {n: W[n].shape for n in SMALL_NAMES}

    def small_state(D):
        vals = {n: D[n] for n in SMALL_NAMES if n != "conv_w"}
        cw = jnp.zeros((L, taps, N_DEV, fc), F32)
        cw = lax.dynamic_update_slice(cw, D["conv_w"].reshape(L, taps, 1, fc), (0, 0, me, 0))
        vals["conv_w"] = cw.reshape(L, taps, N_DEV * fc)
        return _pack_small(vals, jnp.zeros((1,), F32))

    sw, sm, sv = small_state(P), small_state(M), small_state(V)
    res = _adamw(sw, sm, sv, small_recv, name="adamw_small", tr=sw.shape[0])
    loss = _unpack_small(res[0], full_shapes)[1]
    for dst, packed in zip((out_g, out_d, out_m, out_v), res):
        vals, _ = _unpack_small(packed, full_shapes)
        for n in SMALL_NAMES:
            if n == "conv_w":
                sl = lax.dynamic_slice(vals[n].reshape(L, taps, N_DEV, fc), (0, 0, me, 0), (L, taps, 1, fc))
                dst[n] = sl.reshape(L, taps, fc)
            else:
                dst[n] = vals[n]
    return (loss, grad_x, *[out_g[n] for n in names], *[out_d[n] for n in names],
            *[out_m[n] for n in names], *[out_v[n] for n in names])
```

```python
import functools
import math

import jax
import jax.numpy as jnp
from jax import lax
from jax.experimental import pallas as pl
from jax.experimental.pallas import tpu as pltpu

F32 = jnp.float32
BF16 = jnp.bfloat16
MESH = pl.DeviceIdType.MESH

N_DEV = 8
D_MODEL = 1024
DEPTH = 2
HEAD_DIM = 64
KV = 2
GQ = 4
QW = KV * GQ * HEAD_DIM
KW = KV * HEAD_DIM
ROPE_W = QW + KW
IN_COLS = 2 * (QW + 2 * KW)
D_FF = 2816
GRID_W = 64
ROPE_THETA = 10000.0
WINDOW = 128
N_BUCKETS = 32
MAX_DISTANCE = 128
ALPHA = (2.0 * DEPTH) ** 0.25
RMS_EPS = 1e-6
LN_EPS = 1e-5
SCALE = HEAD_DIM ** -0.5
LOG2E = math.log2(math.e)
LN2 = math.log(2.0)
NEG = -1e30
ONES_ROWS = 16

ADAM_LR = 0.001
ADAM_B1 = 0.9
ADAM_B2 = 0.999
ADAM_EPS = 1e-08
ADAM_WD = 0.01
ADAM_STEP = 10

LANES = 128
MM_ROWS = 1024
MM_ROWS_WIDE = 512
DW_TOKENS = 1024
VMEM_CAP = 60 * 1024 * 1024
SMALL_NAMES = ("rel_bias", "q_norm", "k_norm", "sink", "out_norm_a", "out_norm_b", "ln1_g", "ln1_b",
               "conv_b", "ln2_g", "ln2_b", "conv_w")


def _params(sem, est_bytes):
    limit = int(min(VMEM_CAP, est_bytes + (8 << 20)))
    return pltpu.CompilerParams(dimension_semantics=sem, vmem_limit_bytes=limit)


def _nbytes(shape, dtype):
    return math.prod(shape) * jnp.dtype(dtype).itemsize


class _Comm:
    def __init__(self, parts, gathers):
        self.parts, self.gathers, self.n = list(parts), list(gathers), len(parts)
        hbm = pl.BlockSpec(memory_space=pltpu.HBM)
        self.in_specs = [hbm] * self.n
        self.out_specs = [hbm] * self.n
        self.out_shape = [jax.ShapeDtypeStruct((N_DEV,) + tuple(p.shape if g else p.shape[1:]), p.dtype)
                          for p, g in zip(self.parts, self.gathers)]
        self.scratch = [pltpu.SemaphoreType.DMA((self.n * (N_DEV - 1),)), pltpu.SemaphoreType.DMA((self.n * (N_DEV - 1),)),
                        pltpu.SemaphoreType.DMA((self.n,))]

    def bind(self, ins, outs, sems):
        send_sems, recv_sems, local_sems = sems
        gathers, n = self.gathers, self.n
        me = 4 * lax.axis_index("x") + 2 * lax.axis_index("y") + lax.axis_index("c")

        def src(k, j):
            return ins[k] if gathers[k] else ins[k].at[j]

        def copy(k, d, peer, lands_in):
            return pltpu.make_async_remote_copy(
                src_ref=src(k, peer), dst_ref=outs[k].at[lands_in],
                send_sem=send_sems.at[k * (N_DEV - 1) + d - 1], recv_sem=recv_sems.at[k * (N_DEV - 1) + d - 1],
                device_id=(peer // 4, lax.rem(peer // 2, 2), lax.rem(peer, 2)), device_id_type=MESH)

        def send(k, d):
            return copy(k, d, lax.rem(me + d, N_DEV), me)

        def arrival(k, d):
            frm = lax.rem(me + N_DEV - d, N_DEV)
            return copy(k, d, frm, frm)

        def local(k):
            return pltpu.make_async_copy(src(k, me), outs[k].at[me], local_sems.at[k])

        def start():
            for k in range(n):
                local(k).start()
                for d in range(1, N_DEV):
                    send(k, d).start()

        def finish():
            for k in range(n):
                for d in range(1, N_DEV):
                    arrival(k, d).wait_recv()
            for k in range(n):
                for d in range(1, N_DEV):
                    send(k, d).wait_send()
                local(k).wait()

        return start, finish


def _host_comm(comm, refs, n_in, n_out, n_scratch, grid):
    n = comm.n if comm is not None else 0
    own_in, cin = refs[:n_in], refs[n_in:n_in + n]
    own_out, cout = refs[n_in + n:n_in + n + n_out], refs[n_in + n + n_out:n_in + 2 * n + n_out]
    base = n_in + 2 * n + n_out
    own_scratch, sems = refs[base:base + n_scratch], refs[base + n_scratch:]
    own = tuple(own_in) + tuple(own_out) + tuple(own_scratch)
    if comm is None:
        return own, lambda: None, lambda: None
    start, finish = comm.bind(cin, cout, sems)
    first = last = None
    for ax, size in enumerate(grid):
        pid = pl.program_id(ax)
        first = (pid == 0) if first is None else first & (pid == 0)
        last = (pid == size - 1) if last is None else last & (pid == size - 1)
    return own, lambda: pl.when(first)(start), lambda: pl.when(last)(finish)


def _exchange(parts, gathers, name):
    comm = _Comm(parts, gathers)
    n = comm.n

    def body(*refs):
        start, finish = comm.bind(refs[:n], refs[n:2 * n], refs[2 * n:])
        start()
        finish()

    return pl.pallas_call(body, name=name, out_shape=comm.out_shape, in_specs=comm.in_specs, out_specs=comm.out_specs,
                          scratch_shapes=comm.scratch)(*comm.parts)


def _mm(a_list, b_list, *, name, out_dtype, tm, tn, tk, trans_a=False, trans_b=False, add=None, add_scale=1.0):
    assert not (trans_a and trans_b)
    na = len(a_list)
    if trans_a:
        K, M = a_list[0].shape
    else:
        M, K = a_list[0].shape
    N = b_list[0].shape[0 if trans_b else 1]
    tm, tn, tk = min(tm, M), min(tn, N), min(tk, K)
    assert M % tm == 0 and N % tn == 0 and K % tk == 0, (name, M, N, K, tm, tn, tk)
    nk = K // tk
    dims = (((0,), (0,)), ((), ())) if trans_a else (((1,), (1 if trans_b else 0,)), ((), ()))

    def body(*refs):
        a_refs, b_refs = refs[:na], refs[na:2 * na]
        add_ref = refs[2 * na] if add is not None else None
        o_ref = refs[2 * na + (add is not None)]
        k = pl.program_id(2)

        part = None
        for a_ref, b_ref in zip(a_refs, b_refs):
            prod = lax.dot_general(a_ref[...].astype(BF16), b_ref[...].astype(BF16), dims,
                                   preferred_element_type=F32)
            part = prod if part is None else part + prod

        def finish(res):
            if add_ref is not None:
                res = res + add_scale * add_ref[...]
            o_ref[...] = res.astype(o_ref.dtype)

        if nk == 1:
            finish(part)
        else:
            acc_ref = refs[-1]

            @pl.when(k == 0)
            def _():
                acc_ref[...] = part

            @pl.when(k > 0)
            def _():
                acc_ref[...] += part

            @pl.when(k == nk - 1)
            def _():
                finish(acc_ref[...])

    if trans_a:
        a_spec = pl.BlockSpec((tk, tm), lambda i, j, k: (k, i))
    else:
        a_spec = pl.BlockSpec((tm, tk), lambda i, j, k: (i, k))
    if trans_b:
        b_spec = pl.BlockSpec((tn, tk), lambda i, j, k: (j, k))
    else:
        b_spec = pl.BlockSpec((tk, tn), lambda i, j, k: (k, j))
    o_spec = pl.BlockSpec((tm, tn), lambda i, j, k: (i, j))
    in_specs = [a_spec] * na + [b_spec] * na + ([o_spec] if add is not None else [])
    est = (2 * na * (_nbytes((tm, tk), a_list[0].dtype) + _nbytes((tk, tn), b_list[0].dtype))
           + na * (_nbytes((tm, tk), BF16) + _nbytes((tk, tn), BF16))
           + 2 * _nbytes((tm, tn), out_dtype) + 3 * _nbytes((tm, tn), F32)
           + (2 * _nbytes((tm, tn), F32) if add is not None else 0))
    args = list(a_list) + list(b_list) + ([add] if add is not None else [])
    return pl.pallas_call(
        body, name=name, grid=(M // tm, N // tn, nk),
        out_shape=jax.ShapeDtypeStruct((M, N), out_dtype),
        in_specs=in_specs, out_specs=o_spec,
        scratch_shapes=[pltpu.VMEM((tm, tn), F32)] if nk > 1 else [],
        compiler_params=_params(("parallel", "parallel", "arbitrary"), est),
    )(*args)


def _mm_res_ln(a, w, res_hat, res_g, res_b, ln_g, ln_b, *, name, tm, target=None):
    T, K = a.shape
    D = w.shape[1]
    tm = min(tm, T)
    head = target is not None

    def body(a_ref, w_ref, rh_ref, rg_ref, rb_ref, g_ref, b_ref, *rest):
        branch = jnp.dot(a_ref[...].astype(BF16), w_ref[...], preferred_element_type=F32)
        z = ALPHA * (rh_ref[...] * rg_ref[...] + rb_ref[...]) + branch
        mu = jnp.mean(z, axis=1, keepdims=True)
        zc = z - mu
        var = jnp.mean(zc * zc, axis=1, keepdims=True)
        rstd = lax.rsqrt(var + LN_EPS)
        xhat = zc * rstd
        if not head:
            xhat_ref, rstd_ref, xb_ref = rest
            xhat_ref[...] = xhat
            rstd_ref[...] = rstd
            xb_ref[...] = (xhat * g_ref[...] + b_ref[...]).astype(BF16)
            return
        t_ref, dz_ref, dzb_ref, st_ref = rest

        @pl.when(pl.program_id(0) == 0)
        def _():
            st_ref[...] = jnp.zeros_like(st_ref)

        g = g_ref[...]
        err = (xhat * g + b_ref[...]) - t_ref[...]
        dx = err * (1.0 / D)
        st_ref[2:3, :] += 0.5 * jnp.sum(jnp.sum(err * err, axis=1, keepdims=True) * (1.0 / D), axis=0, keepdims=True)
        st_ref[0:1, :] += jnp.sum(dx * xhat, axis=0, keepdims=True)
        st_ref[1:2, :] += jnp.sum(dx, axis=0, keepdims=True)
        dxh = dx * g
        dz = rstd * (dxh - jnp.mean(dxh, axis=1, keepdims=True) - xhat * jnp.mean(dxh * xhat, axis=1, keepdims=True))
        dz_ref[...] = dz
        dzb_ref[...] = dz.astype(BF16)

    row = pl.BlockSpec((tm, D), lambda i: (i, 0))
    vec = pl.BlockSpec((1, D), lambda i: (0, 0))
    est = (2 * (_nbytes((tm, K), a.dtype) + _nbytes((K, D), BF16)) + 4 * _nbytes((tm, D), F32) * 2
           + 8 * _nbytes((tm, D), F32))
    in_specs = [pl.BlockSpec((tm, K), lambda i: (i, 0)), pl.BlockSpec((K, D), lambda i: (0, 0)), row, vec, vec, vec, vec]
    if head:
        return pl.pallas_call(
            body, name=name, grid=(T // tm,),
            out_shape=(jax.ShapeDtypeStruct((T, D), F32), jax.ShapeDtypeStruct((T, D), BF16),
                       jax.ShapeDtypeStruct((8, D), F32)),
            in_specs=in_specs + [row], out_specs=(row, row, pl.BlockSpec((8, D), lambda i: (0, 0))),
            compiler_params=_params(("arbitrary",), est),
        )(a, w, res_hat, res_g, res_b, ln_g, ln_b, target)
    return pl.pallas_call(
        body, name=name, grid=(T // tm,),
        out_shape=(jax.ShapeDtypeStruct((T, D), F32), jax.ShapeDtypeStruct((T, 1), F32),
                   jax.ShapeDtypeStruct((T, D), BF16)),
        in_specs=in_specs, out_specs=(row, pl.BlockSpec((tm, 1), lambda i: (i, 0)), row),
        compiler_params=_params(("parallel",), est),
    )(a, w, res_hat, res_g, res_b, ln_g, ln_b)


def _ln_bwd(xhat, rstd, ln_g, dx, *, name, tm=256):
    T, D = xhat.shape
    tm = min(tm, T)

    def body(xhat_ref, rstd_ref, g_ref, d_ref, dz_ref, dzb_ref, st_ref):
        i = pl.program_id(0)

        @pl.when(i == 0)
        def _():
            st_ref[...] = jnp.zeros_like(st_ref)

        xh = xhat_ref[...]
        dxv = d_ref[...]
        st_ref[0:1, :] += jnp.sum(dxv * xh, axis=0, keepdims=True)
        st_ref[1:2, :] += jnp.sum(dxv, axis=0, keepdims=True)
        dxh = dxv * g_ref[...]
        m1 = jnp.mean(dxh, axis=1, keepdims=True)
        m2 = jnp.mean(dxh * xh, axis=1, keepdims=True)
        dz = rstd_ref[...] * (dxh - m1 - xh * m2)
        dz_ref[...] = dz
        dzb_ref[...] = dz.astype(BF16)

    row = pl.BlockSpec((tm, D), lambda i: (i, 0))
    vec = pl.BlockSpec((1, D), lambda i: (0, 0))
    est = 2 * 4 * _nbytes((tm, D), F32) + 6 * _nbytes((tm, D), F32)
    return pl.pallas_call(
        body, name=name, grid=(T // tm,),
        out_shape=(jax.ShapeDtypeStruct((T, D), F32), jax.ShapeDtypeStruct((T, D), BF16),
                   jax.ShapeDtypeStruct((8, D), F32)),
        in_specs=[row, pl.BlockSpec((tm, 1), lambda i: (i, 0)), vec, row],
        out_specs=(row, row, pl.BlockSpec((8, D), lambda i: (0, 0))),
        compiler_params=_params(("arbitrary",), est),
    )(xhat, rstd, ln_g, dx)


def _pair_swap(v, even):
    return jnp.where(even, pltpu.roll(v, LANES - 1, 1), pltpu.roll(v, 1, 1))


def _half_sums(v, lo):
    s_lo = jnp.sum(jnp.where(lo, v, 0.0), axis=1, keepdims=True)
    s_hi = jnp.sum(jnp.where(lo, 0.0, v), axis=1, keepdims=True)
    return jnp.where(lo, s_lo, s_hi)


A_COLS = ROPE_W + KW
A_HEADS = A_COLS // HEAD_DIM
A_K0, A_V0 = KV * GQ, KV * GQ + KV


def _qk_rope_fwd(h, gains, cos2, sin2, *, name, tm=256):
    T = h.shape[0]
    tm = min(tm, T)
    nch = A_COLS // LANES

    def body(h_ref, g_ref, c_ref, s_ref, oT_ref, kv_ref):
        lane = lax.broadcasted_iota(jnp.int32, (tm, LANES), 1)
        lo, even = lane < HEAD_DIM, lane % 2 == 0
        c, s = c_ref[...], s_ref[...]
        for j in range(nch):
            x = h_ref[:, j * LANES:(j + 1) * LANES]
            isq, isv = j < QW // LANES, j == nch - 1
            if isv:
                out = x
            else:
                g = g_ref[0:1, :] if isq else g_ref[1:2, :]
                r = lax.rsqrt(_half_sums(x * x, lo) * (1.0 / HEAD_DIM) + RMS_EPS)
                nrm = x * r * g
                out = nrm * c + _pair_swap(nrm, even) * s
            if isq:
                out = out * (SCALE * LOG2E)
            else:
                kv_ref[:, (j - QW // LANES) * LANES:(j - QW // LANES + 1) * LANES] = out.astype(BF16)
            oT_ref[j * LANES:(j + 1) * LANES, :] = out.T.astype(BF16)

    est = 2 * (_nbytes((tm, A_COLS), F32) + 2 * _nbytes((tm, A_COLS), BF16) + 2 * _nbytes((tm, LANES), F32)) + (4 << 20)
    return pl.pallas_call(
        body, name=name, grid=(T // tm,),
        out_shape=(jax.ShapeDtypeStruct((A_COLS, T), BF16), jax.ShapeDtypeStruct((T, 2 * KW), BF16)),
        in_specs=[pl.BlockSpec((tm, A_COLS), lambda i: (i, 0)), pl.BlockSpec((8, LANES), lambda i: (0, 0)),
                  pl.BlockSpec((tm, LANES), lambda i: (i, 0)), pl.BlockSpec((tm, LANES), lambda i: (i, 0))],
        out_specs=(pl.BlockSpec((A_COLS, tm), lambda i: (0, i)), pl.BlockSpec((tm, 2 * KW), lambda i: (i, 0))),
        compiler_params=_params(("parallel",), est),
    )(h, gains, cos2, sin2)


def _qk_rope_bwd(h, dqT, dkT, gains, cos2, sin2, *, name, tm=256):
    T = h.shape[0]
    tm = min(tm, T)
    nch = ROPE_W // LANES

    def body(h_ref, dq_ref, dk_ref, g_ref, c_ref, s_ref, dh_ref, dg_ref):
        i = pl.program_id(0)

        @pl.when(i == 0)
        def _():
            dg_ref[...] = jnp.zeros_like(dg_ref)

        lane = lax.broadcasted_iota(jnp.int32, (tm, LANES), 1)
        lo, even = lane < HEAD_DIM, lane % 2 == 0
        c, s = c_ref[...], s_ref[...]
        acc = [None, None]
        for j in range(nch):
            x = h_ref[:, j * LANES:(j + 1) * LANES]
            isq = j < QW // LANES
            g = g_ref[0:1, :] if isq else g_ref[1:2, :]
            d = dq_ref[j * LANES:(j + 1) * LANES, :].T * SCALE if isq else dk_ref[...].T
            r = lax.rsqrt(_half_sums(x * x, lo) * (1.0 / HEAD_DIM) + RMS_EPS)
            dn = d * c + _pair_swap(d * s, even)
            xr = x * r
            part = jnp.sum(dn * xr, axis=0, keepdims=True)
            acc[0 if isq else 1] = part if acc[0 if isq else 1] is None else acc[0 if isq else 1] + part
            dng = dn * g
            dx = r * dng - xr * (r * r) * (_half_sums(dng * x, lo) * (1.0 / HEAD_DIM))
            dh_ref[:, j * LANES:(j + 1) * LANES] = dx.astype(BF16)
        for row in range(2):
            folded = acc[row] + pltpu.roll(acc[row], HEAD_DIM, 1)
            dg_ref[row:row + 1, :] += folded

    est = 2 * (2 * _nbytes((tm, ROPE_W), F32) + _nbytes((tm, ROPE_W), BF16) + 2 * _nbytes((tm, LANES), F32)) + (4 << 20)
    return pl.pallas_call(
        body, name=name, grid=(T // tm,),
        out_shape=(jax.ShapeDtypeStruct((T, ROPE_W), BF16), jax.ShapeDtypeStruct((8, LANES), F32)),
        in_specs=[pl.BlockSpec((tm, ROPE_W), lambda i: (i, 0)), pl.BlockSpec((QW, tm), lambda i: (0, i)),
                  pl.BlockSpec((KW, tm), lambda i: (0, i)), pl.BlockSpec((8, LANES), lambda i: (0, 0)),
                  pl.BlockSpec((tm, LANES), lambda i: (i, 0)), pl.BlockSpec((tm, LANES), lambda i: (i, 0))],
        out_specs=(pl.BlockSpec((tm, ROPE_W), lambda i: (i, 0)), pl.BlockSpec((8, LANES), lambda i: (0, 0))),
        compiler_params=_params(("arbitrary",), est),
    )(h, dqT, dkT, gains, cos2, sin2)


def _attn_a_fwd(k, hT, *, comm=None, tq=4096, tk=2048, cq=512):
    G, T, HD = k.shape
    HE = HD + ONES_ROWS
    tq, tk = min(tq, T), min(tk, T)
    cq = min(cq, tq)
    nk, nt = T // tk, T // tq
    grid = (G, GQ * nt, nk)

    def body(*refs):
        (k_ref, qT_ref, v_ref, oT_ref, lse_ref, m_sc, acc_sc), comm_start, comm_finish = _host_comm(
            comm, refs, 3, 2, 2, grid)
        kv = pl.program_id(2)
        comm_start()
        v1T = jnp.concatenate([v_ref[...], jnp.ones((ONES_ROWS, tk), BF16)], axis=0)

        @pl.when(kv == 0)
        def _():
            m_sc[...] = jnp.full_like(m_sc, NEG)
            acc_sc[...] = jnp.zeros_like(acc_sc)

        def scores(c):
            return jnp.dot(k_ref[...], qT_ref[:, c * cq:(c + 1) * cq], preferred_element_type=F32)

        nc = tq // cq
        ahead = scores(0)
        for c in range(nc):
            cols = slice(c * cq, (c + 1) * cq)
            sT = ahead
            if c + 1 < nc:
                ahead = scores(c + 1)
            m_prev = m_sc[:, cols]
            m_new = jnp.maximum(m_prev, jnp.max(sT, axis=0, keepdims=True))
            pT = jnp.exp2(sT - m_new).astype(BF16)
            acc_sc[:, cols] = (jnp.exp2(m_prev - m_new) * acc_sc[:, cols]
                               + jnp.dot(v1T, pT, preferred_element_type=F32))
            m_sc[:, cols] = m_new

        @pl.when(kv == nk - 1)
        def _():
            l = acc_sc[HD:HD + 1, :]
            oT_ref[...] = acc_sc[0:HD, :] / l
            lse_ref[...] = m_sc[...] + jnp.log2(l)

        comm_finish()

    qtr = pl.BlockSpec((None, HD, tq), lambda g, i, j: (g * GQ + i // nt, 0, i % nt))
    qvec = pl.BlockSpec((None, 1, tq), lambda g, i, j: (g * GQ + i // nt, 0, i % nt))
    est = 6 * _nbytes((cq, tk), F32) + (8 << 20)
    hosted = comm is not None
    return pl.pallas_call(
        body, name="attn_a_fwd_comm" if hosted else "attn_a_fwd", grid=grid,
        out_shape=[jax.ShapeDtypeStruct((G * GQ, HD, T), F32), jax.ShapeDtypeStruct((G * GQ, 1, T), F32)]
        + (comm.out_shape if hosted else []),
        in_specs=[pl.BlockSpec((None, tk, HD), lambda g, i, j: (g, j, 0)), qtr,
                  pl.BlockSpec((None, HD, tk), lambda g, i, j: (A_V0 + g, 0, j))] + (comm.in_specs if hosted else []),
        out_specs=[qtr, qvec] + (comm.out_specs if hosted else []),
        scratch_shapes=[pltpu.VMEM((1, tq), F32), pltpu.VMEM((HE, tq), F32)] + (comm.scratch if hosted else []),
        compiler_params=_params(("arbitrary",) * 3 if hosted else ("parallel", "parallel", "arbitrary"), est),
    )(k, hT, hT, *(comm.parts if hosted else []))


def _attn_a_bwd(k, v, hT, doT, lse_row, delta_row, *, comm=None, tq=4096, tk=1024, cq=256):
    G, T, HD = k.shape
    tq, tk = min(tq, T), min(tk, T)
    cq = min(cq, tq)
    nqt = T // tq
    nq, nc = GQ * nqt, tq // cq
    nt = (((1,), (1,)), ((), ()))

    grid = (G, T // tk, nq)

    def body(*refs):
        (k_ref, v_ref, kT_ref, qT_ref, doT_ref, lse_ref, dl_ref, dkT_ref, dvT_ref, dqT_ref, dk_sc, dv_sc), \
            comm_start, comm_finish = _host_comm(comm, refs, 7, 3, 2, grid)
        j, i = pl.program_id(1), pl.program_id(2)
        comm_start()

        @pl.when((j == 0) & (i == 0))
        def _():
            dqT_ref[...] = jnp.zeros_like(dqT_ref)

        @pl.when(i == 0)
        def _():
            dk_sc[...] = jnp.zeros_like(dk_sc)
            dv_sc[...] = jnp.zeros_like(dv_sc)

        def scores(c):
            cols = slice(c * cq, (c + 1) * cq)
            return (jnp.dot(k_ref[...], qT_ref[:, cols], preferred_element_type=F32),
                    jnp.dot(v_ref[...], doT_ref[:, cols], preferred_element_type=F32))

        ahead = scores(0)
        dk_part = dv_part = None
        for c in range(nc):
            cols = slice(c * cq, (c + 1) * cq)
            sT, dpT = ahead
            if c + 1 < nc:
                ahead = scores(c + 1)
            pT = jnp.exp2(sT - lse_ref[:, cols])
            dsT = (pT * (dpT - dl_ref[:, cols])).astype(BF16)
            dv_c = lax.dot_general(doT_ref[:, cols], pT.astype(BF16), nt, preferred_element_type=F32)
            dk_c = lax.dot_general(qT_ref[:, cols], dsT, nt, preferred_element_type=F32)
            dv_part = dv_c if dv_part is None else dv_part + dv_c
            dk_part = dk_c if dk_part is None else dk_part + dk_c
            out_cols = pl.ds(pl.multiple_of((i % nqt) * tq + c * cq, cq), cq)
            dqT_ref[i // nqt, :, out_cols] += jnp.dot(kT_ref[...], dsT, preferred_element_type=F32)
        dk_sc[...] += dk_part
        dv_sc[...] += dv_part

        @pl.when(i == nq - 1)
        def _():
            dkT_ref[...] = dk_sc[...] * LN2
            dvT_ref[...] = dv_sc[...]

        comm_finish()

    krow = pl.BlockSpec((None, tk, HD), lambda g, j, i: (g, j, 0))
    ktr = pl.BlockSpec((None, HD, tk), lambda g, j, i: (g, 0, j))
    ktr_h = pl.BlockSpec((None, HD, tk), lambda g, j, i: (A_K0 + g, 0, j))
    qtr = pl.BlockSpec((None, HD, tq), lambda g, j, i: (g * GQ + i // nqt, 0, i % nqt))
    qvec = pl.BlockSpec((None, 1, tq), lambda g, j, i: (g * GQ + i // nqt, 0, i % nqt))
    whole = pl.BlockSpec((GQ, HD, T), lambda g, j, i: (g, 0, 0))
    est = 8 * _nbytes((cq, tk), F32) + 2 * _nbytes((GQ, HD, T), F32) + (8 << 20)
    hosted = comm is not None
    return pl.pallas_call(
        body, name="attn_a_bwd_comm" if hosted else "attn_a_bwd", grid=grid,
        out_shape=[jax.ShapeDtypeStruct((G, HD, T), F32), jax.ShapeDtypeStruct((G, HD, T), F32),
                   jax.ShapeDtypeStruct((G * GQ, HD, T), F32)] + (comm.out_shape if hosted else []),
        in_specs=[krow, krow, ktr_h, qtr, qtr, qvec, qvec] + (comm.in_specs if hosted else []),
        out_specs=[ktr, ktr, whole] + (comm.out_specs if hosted else []),
        scratch_shapes=[pltpu.VMEM((HD, tk), F32), pltpu.VMEM((HD, tk), F32)] + (comm.scratch if hosted else []),
        compiler_params=_params(("arbitrary", "arbitrary", "arbitrary"), est),
    )(k, v, hT, hT, doT, lse_row, delta_row, *(comm.parts if hosted else []))


WB = WINDOW
WK = 3 * WINDOW


QB_COL0 = (ROPE_W + KW) // (GQ * HEAD_DIM)
KB_COL = (ROPE_W + KW + QW) // KW
GW = GQ * HEAD_DIM


WSTEP = 2


def _win_in_specs(T):
    nb = T // WB
    assert nb % WSTEP == 0
    ns = nb // WSTEP
    q = [pl.BlockSpec((WSTEP * WB, GW), functools.partial(lambda n, g: (n, QB_COL0 + g), g=g)) for g in range(KV)]
    kv = [pl.BlockSpec((WB, KW), functools.partial(lambda n, o, c: (jnp.clip(WSTEP * n + o, 0, nb - 1), c), o=o, c=c))
          for c in (KB_COL, KB_COL + 1) for o in range(-1, WSTEP + 1)]
    bias = [pl.BlockSpec((None, KV, WK, GQ * WB), lambda n: (jnp.where(n == 0, 0, 1), 0, 0, 0)),
            pl.BlockSpec((None, KV, WK, GQ * WB), lambda n: (jnp.where(n == ns - 1, 2, 1), 0, 0, 0))]
    return ns, q + kv, bias


def _end_tables(biasT):
    key = lax.broadcasted_iota(jnp.int32, biasT.shape, 1)
    return jnp.stack([jnp.where(key < WB, NEG, biasT), biasT, jnp.where(key >= 2 * WB, NEG, biasT)])


def _heads_to_lanes(t):
    return jnp.concatenate([t[i * HEAD_DIM:(i + 1) * HEAD_DIM] for i in range(GQ)], axis=1)


def _lanes_to_heads(t):
    return jnp.concatenate([t[:, i * WB:(i + 1) * WB] for i in range(GQ)], axis=0)


def _attn_b_fwd(h, biasT, sink_rows):
    T = h.shape[0]
    ns, in_specs, bias_specs = _win_in_specs(T)
    nkv = WSTEP + 2

    def body(*refs):
        q_refs, k_refs, v_refs = refs[:KV], refs[KV:KV + nkv], refs[KV + nkv:KV + 2 * nkv]
        b_refs, sk_ref, o_ref, lse_ref = refs[KV + 2 * nkv:KV + 2 * nkv + WSTEP], *refs[KV + 2 * nkv + WSTEP:]
        ks, vs = [r[...] for r in k_refs], [r[...] for r in v_refs]
        outs = []
        for b, b_ref in enumerate(b_refs):
            rows = slice(b * WB, (b + 1) * WB)
            kwin = jnp.concatenate(ks[b:b + 3], axis=0)
            vT = jnp.concatenate(vs[b:b + 3], axis=0).T
            qT = [_heads_to_lanes((q[rows, :] * SCALE).T).astype(BF16) for q in q_refs]
            sT = [jnp.dot(kwin[:, g * HEAD_DIM:(g + 1) * HEAD_DIM].astype(BF16), qT[g], preferred_element_type=F32)
                  for g in range(KV)]
            oT = []
            for g in range(KV):
                s = sT[g] + b_ref[g]
                sk = sk_ref[g]
                m = jnp.maximum(jnp.max(s, axis=0, keepdims=True), sk)
                p = jnp.exp(s - m)
                den = jnp.sum(p, axis=0, keepdims=True) + jnp.exp(sk - m)
                o = jnp.dot(vT[g * HEAD_DIM:(g + 1) * HEAD_DIM].astype(BF16), p.astype(BF16),
                            preferred_element_type=F32) / den
                lse_ref[b, g] = m + jnp.log(den)
                oT.append(_lanes_to_heads(o))
            outs.append(jnp.concatenate(oT, axis=0).T)
        o_ref[...] = jnp.concatenate(outs, axis=0)

    whole = lambda *shape: pl.BlockSpec(shape, lambda n: (0,) * len(shape))
    return pl.pallas_call(
        body, name="attn_b_fwd", grid=(ns,),
        out_shape=(jax.ShapeDtypeStruct((T, QW), F32), jax.ShapeDtypeStruct((ns * WSTEP, KV, 1, GQ * WB), F32)),
        in_specs=in_specs + bias_specs + [whole(KV, 1, GQ * WB)],
        out_specs=(pl.BlockSpec((WSTEP * WB, QW), lambda n: (n, 0)),
                   pl.BlockSpec((WSTEP, KV, 1, GQ * WB), lambda n: (n, 0, 0, 0))),
        compiler_params=_params(("parallel",), 32 << 20),
    )(*([h] * (KV + 2 * nkv)), *([biasT] * WSTEP), sink_rows)


def _attn_b_bwd(h, do, o, lse, biasT, sink_rows):
    T = h.shape[0]
    ns, in_specs, bias_specs = _win_in_specs(T)
    nkv = WSTEP + 2
    Tp = T + 2 * WB
    nt = (((1,), (1,)), ((), ()))

    def body(*refs):
        q_refs, k_refs, v_refs = refs[:KV], refs[KV:KV + nkv], refs[KV + nkv:KV + 2 * nkv]
        at = KV + 2 * nkv
        do_ref, o_ref, lse_ref = refs[at:at + 3]
        b_refs, sk_ref = refs[at + 3:at + 3 + WSTEP], refs[at + 3 + WSTEP]
        dq_ref, dk_ref, dv_ref, db_ref, dsk_ref = refs[at + 4 + WSTEP:]
        n = pl.program_id(0)

        @pl.when(n == 0)
        def _():
            dk_ref[...] = jnp.zeros_like(dk_ref)
            dv_ref[...] = jnp.zeros_like(dv_ref)
            db_ref[...] = jnp.zeros_like(db_ref)
            dsk_ref[...] = jnp.zeros_like(dsk_ref)

        ks, vs = [r[...] for r in k_refs], [r[...] for r in v_refs]
        for b, b_ref in enumerate(b_refs):
            rows = slice(b * WB, (b + 1) * WB)
            kwin = jnp.concatenate(ks[b:b + 3], axis=0)
            vwin = jnp.concatenate(vs[b:b + 3], axis=0)
            kT = kwin.T
            doT_all, oT_all = do_ref[rows, :].T, o_ref[rows, :].T
            qT, doT, delta, sT, dpT = [], [], [], [], []
            for g, q in enumerate(q_refs):
                hd = slice(g * HEAD_DIM, (g + 1) * HEAD_DIM)
                qT.append(_heads_to_lanes((q[rows, :] * SCALE).T).astype(BF16))
                d = _heads_to_lanes(doT_all[g * GW:(g + 1) * GW])
                delta.append(jnp.sum(d * _heads_to_lanes(oT_all[g * GW:(g + 1) * GW]), axis=0, keepdims=True))
                doT.append(d.astype(BF16))
                sT.append(jnp.dot(kwin[:, hd].astype(BF16), qT[g], preferred_element_type=F32))
                dpT.append(jnp.dot(vwin[:, hd].astype(BF16), doT[g], preferred_element_type=F32))
            dq, dk, dv = [], [], []
            for g in range(KV):
                lse_g = lse_ref[b, g]
                p = jnp.exp(sT[g] + b_ref[g] - lse_g)
                ds = p * (dpT[g] - delta[g])
                db_ref[g] += ds
                dsk_ref[g] -= jnp.exp(sk_ref[g] - lse_g) * delta[g]
                dsb = ds.astype(BF16)
                dqT = jnp.dot(kT[g * HEAD_DIM:(g + 1) * HEAD_DIM].astype(BF16), dsb, preferred_element_type=F32)
                dq.append(_lanes_to_heads(dqT))
                dk.append(lax.dot_general(dsb, qT[g], nt, preferred_element_type=F32))
                dv.append(lax.dot_general(p.astype(BF16), doT[g], nt, preferred_element_type=F32))
            dq_ref[rows, :] = (jnp.concatenate(dq, axis=0).T * SCALE).astype(BF16)
            win = pl.ds(pl.multiple_of((WSTEP * n + b) * WB, WB), WK)
            dk_ref[win, :] += jnp.concatenate(dk, axis=1)
            dv_ref[win, :] += jnp.concatenate(dv, axis=1)

    whole = lambda *shape: pl.BlockSpec(shape, lambda n: (0,) * len(shape))
    tok = pl.BlockSpec((WSTEP * WB, QW), lambda n: (n, 0))
    return pl.pallas_call(
        body, name="attn_b_bwd", grid=(ns,),
        out_shape=(jax.ShapeDtypeStruct((T, QW), BF16),
                   jax.ShapeDtypeStruct((Tp, KW), F32), jax.ShapeDtypeStruct((Tp, KW), F32),
                   jax.ShapeDtypeStruct((KV, WK, GQ * WB), F32), jax.ShapeDtypeStruct((KV, 1, GQ * WB), F32)),
        in_specs=in_specs + [tok, tok, pl.BlockSpec((WSTEP, KV, 1, GQ * WB), lambda n: (n, 0, 0, 0))]
        + bias_specs + [whole(KV, 1, GQ * WB)],
        out_specs=(tok, whole(Tp, KW), whole(Tp, KW), whole(KV, WK, GQ * WB), whole(KV, 1, GQ * WB)),
        compiler_params=_params(("arbitrary",), 48 << 20),
    )(*([h] * (KV + 2 * nkv)), do, o, lse, *([biasT] * WSTEP), sink_rows)


def _bias_table(rel_bias_t, bucket):
    nh, n = rel_bias_t.shape[0], bucket.shape[1]

    def body(rb_ref, bk_ref, o_ref):
        bk = bk_ref[...]
        out = jnp.full((nh, n), NEG, F32)
        for b in range(N_BUCKETS):
            out = jnp.where(bk == b, rb_ref[:, b:b + 1], out)
        o_ref[...] = out

    return pl.pallas_call(
        body, name="bias_table", out_shape=jax.ShapeDtypeStruct((nh, n), F32),
        compiler_params=pltpu.CompilerParams(vmem_limit_bytes=32 << 20),
    )(rel_bias_t, bucket)


def _bias_sink_grads(db_list, dsk_list, bucket):
    L = len(db_list)

    def body(*refs):
        db_refs, dsk_refs, bk_ref = refs[:L], refs[L:2 * L], refs[2 * L]
        drb_ref, dsink_ref = refs[2 * L + 1], refs[2 * L + 2]
        tot = db_refs[0][...]
        for r in db_refs[1:]:
            tot = tot + r[...]
        bk = bk_ref[...]
        lane = lax.broadcasted_iota(jnp.int32, (2 * GQ, N_BUCKETS), 1)
        out = jnp.zeros((2 * GQ, N_BUCKETS), F32)
        for b in range(N_BUCKETS):
            sb = jnp.sum(jnp.where(bk == b, tot, 0.0), axis=1, keepdims=True)
            out = jnp.where(lane == b, sb, out)
        drb_ref[...] = out
        for l in range(L):
            dsink_ref[l] = jnp.sum(dsk_refs[l][...], axis=1, keepdims=True)

    return pl.pallas_call(
        body, name="bias_sink_grads",
        out_shape=(jax.ShapeDtypeStruct((2 * GQ, N_BUCKETS), F32), jax.ShapeDtypeStruct((L, 2 * GQ, 1), F32)),
        compiler_params=pltpu.CompilerParams(vmem_limit_bytes=32 << 20),
    )(*db_list, *dsk_list, bucket)


def _outnorm_fwd(oaT, ob, ga, gb, *, tm=512):
    T = ob.shape[0]
    tm = min(tm, T)

    def body(oaT_ref, ob_ref, ga_ref, gb_ref, y_ref):
        for j, (o, g_ref) in enumerate(((oaT_ref[...].T, ga_ref), (ob_ref[...], gb_ref))):
            r = lax.rsqrt(jnp.mean(o * o, axis=1, keepdims=True) + RMS_EPS)
            y_ref[:, j * QW:(j + 1) * QW] = (o * r * g_ref[...]).astype(BF16)

    half = pl.BlockSpec((tm, QW), lambda i: (i, 0))
    halfT = pl.BlockSpec((QW, tm), lambda i: (0, i))
    vec = pl.BlockSpec((1, QW), lambda i: (0, 0))
    return pl.pallas_call(
        body, name="outnorm_fwd", grid=(T // tm,),
        out_shape=jax.ShapeDtypeStruct((T, 2 * QW), BF16),
        in_specs=[halfT, half, vec, vec], out_specs=pl.BlockSpec((tm, 2 * QW), lambda i: (i, 0)),
        compiler_params=_params(("parallel",), 16 << 20),
    )(oaT, ob, ga, gb)


def _outnorm_bwd(dy, oaT, ob, ga, gb, *, tm=512):
    T = ob.shape[0]
    tm = min(tm, T)
    nh = QW // HEAD_DIM

    def body(dy_ref, oaT_ref, ob_ref, ga_ref, gb_ref, doaT_ref, dl_ref, dob_ref, dg_ref):
        i = pl.program_id(0)

        @pl.when(i == 0)
        def _():
            dg_ref[...] = jnp.zeros_like(dg_ref)

        oaT = oaT_ref[...]
        for j, (o, g_ref) in enumerate(((oaT.T, ga_ref), (ob_ref[...], gb_ref))):
            d = dy_ref[:, j * QW:(j + 1) * QW]
            r = lax.rsqrt(jnp.mean(o * o, axis=1, keepdims=True) + RMS_EPS)
            orr = o * r
            dg_ref[j:j + 1, :] += jnp.sum(d * orr, axis=0, keepdims=True)
            dgv = d * g_ref[...]
            do = r * dgv - orr * (r * r) * jnp.mean(dgv * o, axis=1, keepdims=True)
            if j == 0:
                doT = do.T
                doaT_ref[...] = doT.astype(BF16)
                prod = doT * oaT
                dl_ref[...] = jnp.concatenate(
                    [jnp.sum(prod[a * HEAD_DIM:(a + 1) * HEAD_DIM], axis=0, keepdims=True) for a in range(nh)], axis=0)
            else:
                dob_ref[...] = do

    half = pl.BlockSpec((tm, QW), lambda i: (i, 0))
    halfT = pl.BlockSpec((QW, tm), lambda i: (0, i))
    vec = pl.BlockSpec((1, QW), lambda i: (0, 0))
    return pl.pallas_call(
        body, name="outnorm_bwd", grid=(T // tm,),
        out_shape=(jax.ShapeDtypeStruct((QW, T), BF16), jax.ShapeDtypeStruct((nh, T), F32),
                   jax.ShapeDtypeStruct((T, QW), F32), jax.ShapeDtypeStruct((8, QW), F32)),
        in_specs=[pl.BlockSpec((tm, 2 * QW), lambda i: (i, 0)), halfT, half, vec, vec],
        out_specs=(halfT, pl.BlockSpec((nh, tm), lambda i: (0, i)), half, pl.BlockSpec((8, QW), lambda i: (0, 0))),
        compiler_params=_params(("arbitrary",), 32 << 20),
    )(dy, oaT, ob, ga, gb)


GELU_C = math.sqrt(2.0 / math.pi)
GELU_A = 0.044715
HALO = 16
SUB = 8


def _gelu_parts(x):
    x2 = x * x
    t = jnp.tanh(x * (GELU_C + (GELU_C * GELU_A) * x2))
    return 0.5 * (1.0 + t), t, x2


def _halo_specs(tm, tn, T):
    nh = tm // HALO
    last = T // HALO - 1
    cur = pl.BlockSpec((tm, tn), lambda j, i: (i, j))
    prev = pl.BlockSpec((HALO, tn), lambda j, i: (jnp.maximum(i * nh - 1, 0), j))
    nxt = pl.BlockSpec((HALO, tn), lambda j, i: (jnp.minimum((i + 1) * nh, last), j))
    return cur, prev, nxt


def _conv_glu_fwd(g, u, conv_w, conv_b, *, tm=256, tn=1408):
    T, F = g.shape
    tm, tn = min(tm, T), min(tn, F)
    cur, prev, nxt = _halo_specs(tm, tn, T)

    def body(g_ref, gp_ref, gn_ref, u_ref, w_ref, b_ref, a_ref):
        i = pl.program_id(1)
        gv = g_ref[...]
        before = jnp.where(i * tm > 0, gp_ref[HALO - SUB:, :], 0.0)
        after = jnp.where((i + 1) * tm < T, gn_ref[0:SUB, :], 0.0)
        gm1 = pltpu.roll(jnp.concatenate([before, gv], axis=0), 1, 0)[SUB:]
        gp1 = pltpu.roll(jnp.concatenate([gv, after], axis=0), tm + SUB - 1, 0)[:tm]
        gc = ((b_ref[...] + gm1 * w_ref[0:1, :]) + gv * w_ref[1:2, :]) + gp1 * w_ref[2:3, :]
        cdf, _, _ = _gelu_parts(gc)
        a_ref[...] = (gc * cdf * u_ref[...].astype(F32)).astype(BF16)

    wspec = pl.BlockSpec((8, tn), lambda j, i: (0, j))
    est = 2 * (3 * _nbytes((tm, tn), F32)) + 8 * _nbytes((tm, tn), F32)
    return pl.pallas_call(
        body, name="conv_glu_fwd", grid=(F // tn, T // tm),
        out_shape=jax.ShapeDtypeStruct((T, F), BF16),
        in_specs=[cur, prev, nxt, cur, wspec, pl.BlockSpec((1, tn), lambda j, i: (0, j))],
        out_specs=cur,
        compiler_params=_params(("parallel", "parallel"), est),
    )(g, g, g, u, conv_w, conv_b)


def _conv_glu_bwd(dact, g, u, conv_w, conv_b, *, tm=256, tn=1408):
    T, F = g.shape
    tm, tn = min(tm, T), min(tn, F)
    cur, prev, nxt = _halo_specs(tm, tn, T)
    te = tm + 2 * HALO

    def body(d_ref, dp_ref, dn_ref, g_ref, gp_ref, gn_ref, u_ref, up_ref, un_ref, w_ref, b_ref,
             dg_ref, du_ref, dc_ref):
        i = pl.program_id(1)

        @pl.when(i == 0)
        def _():
            dc_ref[...] = jnp.zeros_like(dc_ref)

        has_prev, has_next = i * tm > 0, (i + 1) * tm < T
        ge = jnp.concatenate([jnp.where(has_prev, gp_ref[...], 0.0), g_ref[...],
                              jnp.where(has_next, gn_ref[...], 0.0)], axis=0)
        ue = jnp.concatenate([up_ref[...], u_ref[...], un_ref[...]], axis=0).astype(F32)
        de = jnp.concatenate([jnp.where(has_prev, dp_ref[...].astype(F32), 0.0), d_ref[...].astype(F32),
                              jnp.where(has_next, dn_ref[...].astype(F32), 0.0)], axis=0)
        w0, w1, w2 = w_ref[0:1, :], w_ref[1:2, :], w_ref[2:3, :]
        gm1 = pltpu.roll(ge, 1, 0)
        gp1 = pltpu.roll(ge, te - 1, 0)
        gc = ((b_ref[...] + gm1 * w0) + ge * w1) + gp1 * w2
        cdf, t, gc2 = _gelu_parts(gc)
        dgelu = cdf + (0.5 * gc) * (1.0 - t * t) * (GELU_C + (3.0 * GELU_C * GELU_A) * gc2)
        dgc = de * ue * dgelu
        dge = w0 * pltpu.roll(dgc, te - 1, 0) + w1 * dgc + w2 * pltpu.roll(dgc, 1, 0)
        mid = slice(HALO, HALO + tm)
        dg_ref[...] = dge[mid].astype(BF16)
        du_ref[...] = (de[mid] * (gc[mid] * cdf[mid])).astype(BF16)
        dgm = dgc[mid]
        dc_ref[0:1, :] += jnp.sum(dgm * gm1[mid], axis=0, keepdims=True)
        dc_ref[1:2, :] += jnp.sum(dgm * ge[mid], axis=0, keepdims=True)
        dc_ref[2:3, :] += jnp.sum(dgm * gp1[mid], axis=0, keepdims=True)
        dc_ref[3:4, :] += jnp.sum(dgm, axis=0, keepdims=True)

    wspec = pl.BlockSpec((8, tn), lambda j, i: (0, j))
    est = 2 * (3 * _nbytes((tm, tn), F32) + 2 * _nbytes((tm, tn), BF16)) + 16 * _nbytes((te, tn), F32)
    return pl.pallas_call(
        body, name="conv_glu_bwd", grid=(F // tn, T // tm),
        out_shape=(jax.ShapeDtypeStruct((T, F), BF16), jax.ShapeDtypeStruct((T, F), BF16),
                   jax.ShapeDtypeStruct((8, F), F32)),
        in_specs=[cur, prev, nxt, cur, prev, nxt, cur, prev, nxt, wspec, pl.BlockSpec((1, tn), lambda j, i: (0, j))],
        out_specs=(cur, cur, wspec),
        compiler_params=_params(("parallel", "arbitrary"), est),
    )(dact, dact, dact, g, g, g, u, u, u, conv_w, conv_b)


def _adamw_math(w, g, m, v):
    m = ADAM_B1 * m + (1.0 - ADAM_B1) * g
    v = ADAM_B2 * v + (1.0 - ADAM_B2) * (g * g)
    m_hat = m / (1.0 - ADAM_B1 ** ADAM_STEP)
    v_hat = v / (1.0 - ADAM_B2 ** ADAM_STEP)
    delta = -ADAM_LR * (m_hat / (jnp.sqrt(v_hat) + ADAM_EPS) + ADAM_WD * w)
    return delta, m, v


def _adamw(w, m, v, gparts, *, name):
    R, C = w.shape
    r = gparts[0].shape[1]
    assert len(gparts) * r == R
    tr = max(d for d in range(8, min(r, 256) + 1, 8) if r % d == 0) if r % 8 == 0 else r
    per = r // tr

    def body(w_ref, m_ref, v_ref, *rest):
        gp_refs, (g_ref, d_ref, nm_ref, nv_ref) = rest[:len(gparts)], rest[len(gparts):]
        i = pl.program_id(0)
        for l, gp_ref in enumerate(gp_refs):
            @pl.when(i // per == l)
            def _(gp_ref=gp_ref):
                g = gp_ref[0].astype(F32)
                for j in range(1, N_DEV):
                    g = g + gp_ref[j].astype(F32)
                delta, nm, nv = _adamw_math(w_ref[...], g, m_ref[...], v_ref[...])
                g_ref[...] = g
                d_ref[...] = delta
                nm_ref[...] = nm
                nv_ref[...] = nv

    blk = pl.BlockSpec((tr, C), lambda i: (i, 0))
    out = jax.ShapeDtypeStruct((R, C), F32)
    return pl.pallas_call(
        body, name=name, grid=(R // tr,), out_shape=(out, out, out, out),
        in_specs=[blk, blk, blk] + [pl.BlockSpec((N_DEV, tr, C), lambda i: (0, i % per, 0))] * len(gparts),
        out_specs=(blk, blk, blk, blk),
        compiler_params=_params(("parallel",), 24 << 20),
    )(w, m, v, *gparts)


def _rope_tables(T):
    rows_n = T // GRID_W
    row = jnp.repeat(jnp.arange(rows_n, dtype=F32), GRID_W)
    col = jnp.tile(jnp.arange(GRID_W, dtype=F32), rows_n)
    half = HEAD_DIM // 2
    inv_freq = ROPE_THETA ** (-jnp.arange(0, half, 2, dtype=F32) / half)
    ang = jnp.concatenate([row[:, None] * inv_freq, col[:, None] * inv_freq], axis=-1)
    cos, sin = jnp.cos(ang), jnp.sin(ang)
    cos64 = jnp.repeat(cos, 2, axis=-1)
    sin64 = jnp.stack([-sin, sin], axis=-1).reshape(T, HEAD_DIM)
    return jnp.tile(cos64, (1, 2)), jnp.tile(sin64, (1, 2))


def _t5_bucket(rel):
    half = N_BUCKETS // 2
    max_exact = half // 2
    bucket = jnp.where(rel > 0, half, 0)
    rp = jnp.abs(rel)
    rpf = jnp.maximum(rp, 1).astype(F32)
    large = max_exact + (jnp.log(rpf / max_exact) / math.log(MAX_DISTANCE / max_exact)
                         * (half - max_exact)).astype(jnp.int32)
    large = jnp.minimum(large, half - 1)
    return bucket + jnp.where(rp < max_exact, rp, large)


def _window_buckets():
    qpos = jnp.arange(WB, dtype=jnp.int32)
    kpos = jnp.arange(WK, dtype=jnp.int32) - WB
    rel = kpos[None, :] - qpos[:, None]
    return jnp.where(jnp.abs(rel) <= WINDOW, _t5_bucket(rel), -1)


def _heads_first(a, nh):
    T = a.shape[0]
    return a.reshape(T, nh, HEAD_DIM).transpose(1, 0, 2)


def _row(v):
    return v.reshape(1, -1)


def _rows8(rows, width):
    a = jnp.stack(list(rows), axis=0)
    return jnp.pad(a, ((0, 8 - a.shape[0]), (0, 0)))


def _layer_fwd(l, xin, W, tabs, comm=None, on_comm=None, target=None):
    xhat, xg, xb, x16 = xin
    T = xhat.shape[0]
    cos2, sin2, biasT = tabs
    h = _mm([x16], [W["w_in"][l]], name="mm_in", out_dtype=F32, tm=MM_ROWS, tn=IN_COLS, tk=D_MODEL)
    gains = _rows8([jnp.tile(W["q_norm"][l], 2), jnp.tile(W["k_norm"][l], 2)], LANES)
    hT, kv_nat = _qk_rope_fwd(h, gains, cos2, sin2, name="qk_rope_fwd")
    hT = hT.reshape(A_HEADS, HEAD_DIM, T)
    ka, va = _heads_first(kv_nat[:, :KW], KV), _heads_first(kv_nat[:, KW:], KV)
    res = _attn_a_fwd(ka, hT, comm=comm)
    oaT, lse_a = res[0].reshape(QW, T), res[1]
    if comm is not None:
        on_comm(res[2:])
    sink_rows = jnp.repeat(W["sink"][l], WB).reshape(KV, 1, GQ * WB)
    ob_t, lse_b = _attn_b_fwd(h, biasT, sink_rows)
    ga, gb = _row(W["out_norm_a"][l]), _row(W["out_norm_b"][l])
    ycat = _outnorm_fwd(oaT, ob_t, ga, gb)
    g1, b1 = _row(W["ln1_g"][l]), _row(W["ln1_b"][l])
    x1hat, rstd1, x1_16 = _mm_res_ln(ycat, W["w_out"][l], xhat, xg, xb, g1, b1, name="mm_out_ln", tm=512)
    gate = _mm([x1_16], [W["w_gate"][l]], name="mm_gate", out_dtype=F32, tm=MM_ROWS_WIDE, tn=D_FF, tk=D_MODEL)
    up = _mm([x1_16], [W["w_up"][l]], name="mm_up", out_dtype=BF16, tm=MM_ROWS_WIDE, tn=D_FF, tk=D_MODEL)
    cw = jnp.pad(W["conv_w"][l], ((0, 5), (0, 0)))
    cb = _row(W["conv_b"][l])
    act = _conv_glu_fwd(gate, up, cw, cb)
    g2, b2 = _row(W["ln2_g"][l]), _row(W["ln2_b"][l])
    saved = dict(x16=x16, h=h, gains=gains, hT=hT, ka=ka, va=va, oaT=oaT, lse_a=lse_a,
                 lse_b=lse_b, sink_rows=sink_rows, ob_t=ob_t,
                 ga=ga, gb=gb, ycat=ycat, x1hat=x1hat, rstd1=rstd1, x1_16=x1_16, g1=g1, b1=b1, gate=gate, up=up,
                 cw=cw, cb=cb, act=act, g2=g2, b2=b2)
    if target is not None:
        saved["head"] = _mm_res_ln(act, W["w_down"][l], x1hat, g1, b1, g2, b2, name="mm_down_ln_loss", tm=256,
                                   target=target)
        return None, saved
    x2hat, rstd2, x2_16 = _mm_res_ln(act, W["w_down"][l], x1hat, g1, b1, g2, b2, name="mm_down_ln", tm=256)
    saved.update(x2hat=x2hat, rstd2=rstd2)
    return (x2hat, g2, b2, x2_16), saved


def _layer_bwd(l, S, W, tabs, dz2, dz2_16, stats2, scatter=None):
    cos2, sin2, biasT = tabs
    T = dz2.shape[0]
    G = {}
    G["ln2_g"], G["ln2_b"] = stats2[0], stats2[1]
    G["w_down"] = _mm([S["act"]], [dz2_16], name="dw_down", out_dtype=BF16, trans_a=True, tm=D_FF // 2, tn=D_MODEL, tk=DW_TOKENS)
    dact = _mm([dz2_16], [W["w_down"][l]], name="mm_dact", out_dtype=BF16, trans_b=True, tm=MM_ROWS_WIDE, tn=D_FF,
               tk=D_MODEL)
    dg, du, dconv = _conv_glu_bwd(dact, S["gate"], S["up"], S["cw"], S["cb"])
    G["conv_w"], G["conv_b"] = dconv[0:3], dconv[3]
    G["w_gate"] = _mm([S["x1_16"]], [dg], name="dw_gate", out_dtype=BF16, trans_a=True, tm=D_MODEL, tn=D_FF // 2, tk=DW_TOKENS)
    G["w_up"] = _mm([S["x1_16"]], [du], name="dw_up", out_dtype=BF16, trans_a=True, tm=D_MODEL, tn=D_FF // 2, tk=DW_TOKENS)
    dx1 = _mm([dg, du], [W["w_gate"][l], W["w_up"][l]], name="mm_dx1", out_dtype=F32, trans_b=True, tm=MM_ROWS_WIDE,
              tn=D_MODEL, tk=D_FF, add=dz2, add_scale=ALPHA)
    dz1, dz1_16, stats1 = _ln_bwd(S["x1hat"], S["rstd1"], S["g1"], dx1, name="ln1_bwd")
    G["ln1_g"], G["ln1_b"] = stats1[0], stats1[1]
    G["w_out"] = _mm([S["ycat"]], [dz1_16], name="dw_out", out_dtype=BF16, trans_a=True, tm=D_MODEL, tn=D_MODEL, tk=DW_TOKENS)
    dycat = _mm([dz1_16], [W["w_out"][l]], name="mm_dycat", out_dtype=F32, trans_b=True, tm=MM_ROWS, tn=D_MODEL,
                tk=D_MODEL)
    doaT, delta, dob_t, dgn = _outnorm_bwd(dycat, S["oaT"], S["ob_t"], S["ga"], S["gb"])
    G["out_norm_a"], G["out_norm_b"] = dgn[0], dgn[1]
    res = _attn_a_bwd(S["ka"], S["va"], S["hT"], doaT.reshape(KV * GQ, HEAD_DIM, T), S["lse_a"],
                      delta.reshape(KV * GQ, 1, T), comm=scatter(G) if scatter is not None else None)
    dkaT, dvaT, dqaT = res[:3]
    dh_rope, dgain = _qk_rope_bwd(S["h"], dqaT.reshape(QW, T), dkaT.reshape(KW, T), S["gains"], cos2, sin2,
                                  name="qk_rope_bwd")
    G["q_norm"], G["k_norm"] = dgain[0, :HEAD_DIM], dgain[1, :HEAD_DIM]
    dqb_t, dkb, dvb, dbiasT, dsk = _attn_b_bwd(S["h"], dob_t, S["ob_t"], S["lse_b"], biasT, S["sink_rows"])
    dh = jnp.concatenate([
        dh_rope, dvaT.transpose(2, 0, 1).reshape(T, KW).astype(BF16), dqb_t,
        dkb[WB:WB + T].astype(BF16), dvb[WB:WB + T].astype(BF16)], axis=1)
    dbias = dbiasT.reshape(KV, WK, GQ, WB).transpose(0, 2, 1, 3)
    G["w_in"] = _mm([S["x16"]], [dh], name="dw_in", out_dtype=BF16, trans_a=True, tm=D_MODEL, tn=IN_COLS, tk=DW_TOKENS)
    dxin = _mm([dh], [W["w_in"][l]], name="mm_dxin", out_dtype=F32, trans_b=True, tm=MM_ROWS, tn=D_MODEL, tk=IN_COLS,
               add=dz1, add_scale=ALPHA)
    return dxin, G, dbias.reshape(KV * GQ, WK * WB), dsk.reshape(KV * GQ, WB), res[3:]


BIG = ("w_in", "w_out", "w_gate", "w_up", "w_down")
COL_SHARDED = ("w_in", "w_gate", "w_up")


def _unshard(name, blocks):
    _, r, c = blocks.shape
    if name in COL_SHARDED:
        return blocks.transpose(1, 0, 2).reshape(r, N_DEV * c)
    return blocks.reshape(N_DEV * r, c)


def _to_owner_blocks(name, full, shard_shape):
    _, r, c = shard_shape
    if name in COL_SHARDED:
        return full.reshape(r, N_DEV, c).transpose(1, 0, 2)
    return full.reshape(N_DEV, r, c)


def _pack_small(vals, tail):
    flat = jnp.concatenate([vals[n].reshape(-1).astype(F32) for n in SMALL_NAMES] + [tail])
    pad = (-flat.shape[0]) % (8 * LANES)
    return jnp.pad(flat, (0, pad)).reshape(-1, LANES)


def _unpack_small(packed, shapes):
    flat = packed.reshape(-1)
    out, off = {}, 0
    for n in SMALL_NAMES:
        size = math.prod(shapes[n])
        out[n] = flat[off:off + size].reshape(shapes[n])
        off += size
    return out, flat[off]


def kernel(x, rel_bias, w_in, q_norm, k_norm, sink, out_norm_a, out_norm_b, w_out, ln1_g, ln1_b, w_gate, w_up, conv_w, conv_b, w_down, ln2_g, ln2_b, loss_target, m_rel_bias, m_w_in, m_q_norm, m_k_norm, m_sink, m_out_norm_a, m_out_norm_b, m_w_out, m_ln1_g, m_ln1_b, m_w_gate, m_w_up, m_conv_w, m_conv_b, m_w_down, m_ln2_g, m_ln2_b, v_rel_bias, v_w_in, v_q_norm, v_k_norm, v_sink, v_out_norm_a, v_out_norm_b, v_w_out, v_ln1_g, v_ln1_b, v_w_gate, v_w_up, v_conv_w, v_conv_b, v_w_down, v_ln2_g, v_ln2_b):
    P = dict(rel_bias=rel_bias, w_in=w_in, q_norm=q_norm, k_norm=k_norm, sink=sink, out_norm_a=out_norm_a,
             out_norm_b=out_norm_b, w_out=w_out, ln1_g=ln1_g, ln1_b=ln1_b, w_gate=w_gate, w_up=w_up, conv_w=conv_w,
             conv_b=conv_b, w_down=w_down, ln2_g=ln2_g, ln2_b=ln2_b)
    M = dict(rel_bias=m_rel_bias, w_in=m_w_in, q_norm=m_q_norm, k_norm=m_k_norm, sink=m_sink, out_norm_a=m_out_norm_a,
             out_norm_b=m_out_norm_b, w_out=m_w_out, ln1_g=m_ln1_g, ln1_b=m_ln1_b, w_gate=m_w_gate, w_up=m_w_up,
             conv_w=m_conv_w, conv_b=m_conv_b, w_down=m_w_down, ln2_g=m_ln2_g, ln2_b=m_ln2_b)
    V = dict(rel_bias=v_rel_bias, w_in=v_w_in, q_norm=v_q_norm, k_norm=v_k_norm, sink=v_sink, out_norm_a=v_out_norm_a,
             out_norm_b=v_out_norm_b, w_out=v_w_out, ln1_g=v_ln1_g, ln1_b=v_ln1_b, w_gate=v_w_gate, w_up=v_w_up,
             conv_w=v_conv_w, conv_b=v_conv_b, w_down=v_w_down, ln2_g=v_ln2_g, ln2_b=v_ln2_b)
    names = list(P)
    T = x.shape[1]
    me = 4 * lax.axis_index("x") + 2 * lax.axis_index("y") + lax.axis_index("c")

    L, taps, fc = conv_w.shape
    W = {n: ([None] * DEPTH if n in BIG else P[n]) for n in names}

    def wire(n, l):
        return P[n][l].astype(BF16)

    def take(n, l, gathered):
        W[n][l] = _unshard(n, gathered)

    take("w_in", 0, _exchange([wire("w_in", 0)], [True], name="gather_w_in0")[0])
    later = [(n, l) for l in range(DEPTH) for n in BIG if (n, l) != ("w_in", 0)]
    cw_shard = conv_w.reshape(-1)
    cw_wire = jnp.pad(cw_shard, (0, (-cw_shard.shape[0]) % LANES)).reshape(-1, LANES)
    gather_rest = _Comm([wire(n, l) for n, l in later] + [cw_wire], [True] * (len(later) + 1))

    def on_gathered(outs):
        for (n, l), g in zip(later, outs):
            take(n, l, g)
        cw_all = outs[-1].reshape(N_DEV, -1)[:, :cw_shard.shape[0]].reshape(N_DEV, L, taps, fc)
        W["conv_w"] = cw_all.transpose(1, 2, 0, 3).reshape(L, taps, N_DEV * fc)

    cos2, sin2 = _rope_tables(T)
    bucket = _window_buckets()
    bias = _bias_table(rel_bias.T, bucket.reshape(1, WB * WK))
    biasT = bias.reshape(KV, GQ, WB, WK).transpose(0, 3, 1, 2).reshape(KV, WK, GQ * WB)
    biasT = _end_tables(biasT)
    tabs = (cos2, sin2, biasT)

    ones, zeros = jnp.ones((1, D_MODEL), F32), jnp.zeros((1, D_MODEL), F32)
    cur = (x[0], ones, zeros, x[0].astype(BF16))
    saved = []
    for l in range(DEPTH):
        cur, S = _layer_fwd(l, cur, W, tabs, comm=gather_rest if l == 0 else None, on_comm=on_gathered,
                            target=loss_target[0] if l == DEPTH - 1 else None)
        saved.append(S)

    def owner_blocks(n, l):
        return _to_owner_blocks(n, grads[l][n], P[n].shape)

    early = ([(n, l) for l in range(1, DEPTH) for n in BIG] + [(n, 0) for n in BIG if n != "w_in"])

    def scatter_early(g0):
        grads[0] = g0
        return _Comm([owner_blocks(n, l) for n, l in early], [False] * len(early))

    grads = [None] * DEPTH
    dbs, dsks = [None] * DEPTH, [None] * DEPTH
    dz, dz16, stats = saved[-1]["head"]
    loss_part = stats[2, 0:1]
    recv = {}
    for l in reversed(range(DEPTH)):
        S = saved[l]
        dxin, grads[l], dbs[l], dsks[l], got = _layer_bwd(l, S, W, tabs, dz, dz16, stats,
                                                         scatter=scatter_early if l == 0 else None)
        if l == 0:
            recv.update(zip(early, got))
        if l > 0:
            Sp = saved[l - 1]
            dz, dz16, stats = _ln_bwd(Sp["x2hat"], Sp["rstd2"], Sp["g2"], dxin, name="ln2_bwd")
    grad_x = dxin[None]

    drb, dsink = _bias_sink_grads(dbs, dsks, bucket.T.reshape(1, WK * WB))
    small_g = {n: jnp.stack([grads[l][n] for l in range(DEPTH)]) for n in SMALL_NAMES if n not in ("rel_bias", "sink")}
    small_g["rel_bias"] = drb.T
    small_g["sink"] = dsink.reshape(DEPTH, KV * GQ)
    recv[("w_in", 0)], small_recv = _exchange([owner_blocks("w_in", 0), _pack_small(small_g, loss_part)], [False, True],
                                              name="scatter_w_in0_gather_small")

    out_g, out_d, out_m, out_v = {}, {}, {}, {}
    for n in BIG:
        shp = P[n].shape
        rows, cols = shp[0] * shp[1], shp[2]
        res = _adamw(P[n].reshape(rows, cols), M[n].reshape(rows, cols), V[n].reshape(rows, cols),
                     [recv[(n, l)] for l in range(DEPTH)], name="adamw_" + n)
        out_g[n], out_d[n], out_m[n], out_v[n] = (r.reshape(shp) for r in res)
    full_shapes = {n: W[n].shape for n in SMALL_NAMES}

    def small_state(D):
        vals = {n: D[n] for n in SMALL_NAMES if n != "conv_w"}
        cw = jnp.zeros((L, taps, N_DEV, fc), F32)
        cw = lax.dynamic_update_slice(cw, D["conv_w"].reshape(L, taps, 1, fc), (0, 0, me, 0))
        vals["conv_w"] = cw.reshape(L, taps, N_DEV * fc)
        return _pack_small(vals, jnp.zeros((1,), F32))

    sw, sm, sv = small_state(P), small_state(M), small_state(V)
    res = _adamw(sw, sm, sv, [small_recv], name="adamw_small")
    loss = _unpack_small(res[0], full_shapes)[1]
    for dst, packed in zip((out_g, out_d, out_m, out_v), res):
        vals, _ = _unpack_small(packed, full_shapes)
        for n in SMALL_NAMES:
            if n == "conv_w":
                sl = lax.dynamic_slice(vals[n].reshape(L, taps, N_DEV, fc), (0, 0, me, 0), (L, taps, 1, fc))
                dst[n] = sl.reshape(L, taps, fc)
            else:
                dst[n] = vals[n]
    return (loss, grad_x, *[out_g[n] for n in names], *[out_d[n] for n in names],
            *[out_m[n] for n in names], *[out_v[n] for n in names])
```

```python
import functools
import math

import jax
import jax.numpy as jnp
from jax import lax
from jax.experimental import pallas as pl
from jax.experimental.pallas import tpu as pltpu

F32 = jnp.float32
BF16 = jnp.bfloat16
MESH = pl.DeviceIdType.MESH

N_DEV = 8
D_MODEL = 1024
DEPTH = 2
HEAD_DIM = 64
KV = 2
GQ = 4
QW = KV * GQ * HEAD_DIM
KW = KV * HEAD_DIM
ROPE_W = QW + KW
IN_COLS = 2 * (QW + 2 * KW)
D_FF = 2816
GRID_W = 64
ROPE_THETA = 10000.0
WINDOW = 128
N_BUCKETS = 32
MAX_DISTANCE = 128
ALPHA = (2.0 * DEPTH) ** 0.25
RMS_EPS = 1e-6
LN_EPS = 1e-5
SCALE = HEAD_DIM ** -0.5
LOG2E = math.log2(math.e)
LN2 = math.log(2.0)
NEG = -1e30
ONES_ROWS = 16

ADAM_LR = 0.001
ADAM_B1 = 0.9
ADAM_B2 = 0.999
ADAM_EPS = 1e-08
ADAM_WD = 0.01
ADAM_STEP = 10

LANES = 128
MM_ROWS = 1024
MM_ROWS_LN = 256
MM_ROWS_WIDE = 512
DW_TOKENS = 1024
VMEM_CAP = 60 * 1024 * 1024
SMALL_NAMES = ("rel_bias", "q_norm", "k_norm", "sink", "out_norm_a", "out_norm_b", "ln1_g", "ln1_b",
               "conv_b", "ln2_g", "ln2_b", "conv_w")


def _params(sem, est_bytes):
    limit = int(min(VMEM_CAP, est_bytes + (8 << 20)))
    return pltpu.CompilerParams(dimension_semantics=sem, vmem_limit_bytes=limit)


def _nbytes(shape, dtype):
    return math.prod(shape) * jnp.dtype(dtype).itemsize


class _Comm:
    def __init__(self, parts, gathers):
        self.parts, self.gathers, self.n = list(parts), list(gathers), len(parts)
        hbm = pl.BlockSpec(memory_space=pltpu.HBM)
        self.in_specs = [hbm] * self.n
        self.out_specs = [hbm] * self.n
        self.out_shape = [jax.ShapeDtypeStruct((N_DEV,) + tuple(p.shape if g else p.shape[1:]), p.dtype)
                          for p, g in zip(self.parts, self.gathers)]
        self.scratch = [pltpu.SemaphoreType.DMA((self.n * (N_DEV - 1),)), pltpu.SemaphoreType.DMA((self.n * (N_DEV - 1),)),
                        pltpu.SemaphoreType.DMA((self.n,))]

    def bind(self, ins, outs, sems):
        send_sems, recv_sems, local_sems = sems
        gathers, n = self.gathers, self.n
        me = 4 * lax.axis_index("x") + 2 * lax.axis_index("y") + lax.axis_index("c")

        def src(k, j):
            return ins[k] if gathers[k] else ins[k].at[j]

        def copy(k, d, peer, lands_in):
            return pltpu.make_async_remote_copy(
                src_ref=src(k, peer), dst_ref=outs[k].at[lands_in],
                send_sem=send_sems.at[k * (N_DEV - 1) + d - 1], recv_sem=recv_sems.at[k * (N_DEV - 1) + d - 1],
                device_id=(peer // 4, lax.rem(peer // 2, 2), lax.rem(peer, 2)), device_id_type=MESH)

        def send(k, d):
            return copy(k, d, lax.rem(me + d, N_DEV), me)

        def arrival(k, d):
            frm = lax.rem(me + N_DEV - d, N_DEV)
            return copy(k, d, frm, frm)

        def local(k):
            return pltpu.make_async_copy(src(k, me), outs[k].at[me], local_sems.at[k])

        def start():
            for k in range(n):
                local(k).start()
                for d in range(1, N_DEV):
                    send(k, d).start()

        def finish():
            for k in range(n):
                for d in range(1, N_DEV):
                    arrival(k, d).wait_recv()
            for k in range(n):
                for d in range(1, N_DEV):
                    send(k, d).wait_send()
                local(k).wait()

        return start, finish


def _host_comm(comm, refs, n_in, n_out, n_scratch, grid):
    n = comm.n if comm is not None else 0
    own_in, cin = refs[:n_in], refs[n_in:n_in + n]
    own_out, cout = refs[n_in + n:n_in + n + n_out], refs[n_in + n + n_out:n_in + 2 * n + n_out]
    base = n_in + 2 * n + n_out
    own_scratch, sems = refs[base:base + n_scratch], refs[base + n_scratch:]
    own = tuple(own_in) + tuple(own_out) + tuple(own_scratch)
    if comm is None:
        return own, lambda: None, lambda: None
    start, finish = comm.bind(cin, cout, sems)
    first = last = None
    for ax, size in enumerate(grid):
        pid = pl.program_id(ax)
        first = (pid == 0) if first is None else first & (pid == 0)
        last = (pid == size - 1) if last is None else last & (pid == size - 1)
    return own, lambda: pl.when(first)(start), lambda: pl.when(last)(finish)


def _exchange(parts, gathers, name):
    comm = _Comm(parts, gathers)
    n = comm.n

    def body(*refs):
        start, finish = comm.bind(refs[:n], refs[n:2 * n], refs[2 * n:])
        start()
        finish()

    return pl.pallas_call(body, name=name, out_shape=comm.out_shape, in_specs=comm.in_specs, out_specs=comm.out_specs,
                          scratch_shapes=comm.scratch)(*comm.parts)


def _mm(a_list, b_list, *, name, out_dtype, tm, tn, tk, trans_a=False, trans_b=False, add=None, add_scale=1.0,
        ln=None):
    assert not (trans_a and trans_b)
    na = len(a_list)
    if trans_a:
        K, M = a_list[0].shape
    else:
        M, K = a_list[0].shape
    N = b_list[0].shape[0 if trans_b else 1]
    tm, tn, tk = min(tm, M), min(tn, N), min(tk, K)
    assert M % tm == 0 and N % tn == 0 and K % tk == 0, (name, M, N, K, tm, tn, tk)
    nk = K // tk
    dims = (((0,), (0,)), ((), ())) if trans_a else (((1,), (1 if trans_b else 0,)), ((), ()))

    n_in = 2 * na + (add is not None) + (3 if ln is not None else 0)

    def body(*refs):
        a_refs, b_refs = refs[:na], refs[na:2 * na]
        add_ref = refs[2 * na] if add is not None else None
        o_ref = refs[n_in]
        k = pl.program_id(2)

        part = None
        for a_ref, b_ref in zip(a_refs, b_refs):
            prod = lax.dot_general(a_ref[...].astype(BF16), b_ref[...].astype(BF16), dims,
                                   preferred_element_type=F32)
            part = prod if part is None else part + prod

        def finish(res):
            if add_ref is not None:
                res = res + add_scale * add_ref[...]
            if ln is None:
                o_ref[...] = res.astype(o_ref.dtype)
                return
            xh_ref, rstd_ref, g_ref = refs[n_in - 3:n_in]
            dzb_ref, st_ref = refs[n_in + 1:n_in + 3]

            @pl.when(pl.program_id(0) == 0)
            def _():
                st_ref[...] = jnp.zeros_like(st_ref)

            xh = xh_ref[...]
            st_ref[0:1, :] += jnp.sum(res * xh, axis=0, keepdims=True)
            st_ref[1:2, :] += jnp.sum(res, axis=0, keepdims=True)
            dxh = res * g_ref[...]
            dz = rstd_ref[...] * (dxh - jnp.mean(dxh, axis=1, keepdims=True)
                                  - xh * jnp.mean(dxh * xh, axis=1, keepdims=True))
            o_ref[...] = dz
            dzb_ref[...] = dz.astype(BF16)

        if nk == 1:
            finish(part)
        else:
            acc_ref = refs[-1]

            @pl.when(k == 0)
            def _():
                acc_ref[...] = part

            @pl.when(k > 0)
            def _():
                acc_ref[...] += part

            @pl.when(k == nk - 1)
            def _():
                finish(acc_ref[...])

    if trans_a:
        a_spec = pl.BlockSpec((tk, tm), lambda i, j, k: (k, i))
    else:
        a_spec = pl.BlockSpec((tm, tk), lambda i, j, k: (i, k))
    if trans_b:
        b_spec = pl.BlockSpec((tn, tk), lambda i, j, k: (j, k))
    else:
        b_spec = pl.BlockSpec((tk, tn), lambda i, j, k: (k, j))
    o_spec = pl.BlockSpec((tm, tn), lambda i, j, k: (i, j))
    in_specs = [a_spec] * na + [b_spec] * na + ([o_spec] if add is not None else [])
    if ln is not None:
        assert tn == N and nk == 1
        in_specs += [o_spec, pl.BlockSpec((tm, 1), lambda i, j, k: (i, 0)), pl.BlockSpec((1, tn), lambda i, j, k: (0, 0))]
    est = (2 * na * (_nbytes((tm, tk), a_list[0].dtype) + _nbytes((tk, tn), b_list[0].dtype))
           + na * (_nbytes((tm, tk), BF16) + _nbytes((tk, tn), BF16))
           + 2 * _nbytes((tm, tn), out_dtype) + 3 * _nbytes((tm, tn), F32)
           + (2 * _nbytes((tm, tn), F32) if add is not None else 0))
    args = list(a_list) + list(b_list) + ([add] if add is not None else [])
    if ln is not None:
        return pl.pallas_call(
            body, name=name, grid=(M // tm, 1, 1),
            out_shape=(jax.ShapeDtypeStruct((M, N), F32), jax.ShapeDtypeStruct((M, N), BF16),
                       jax.ShapeDtypeStruct((8, N), F32)),
            in_specs=in_specs, out_specs=(o_spec, o_spec, pl.BlockSpec((8, tn), lambda i, j, k: (0, 0))),
            compiler_params=_params(("arbitrary",) * 3, est + 6 * _nbytes((tm, tn), F32)),
        )(*args, *ln)
    return pl.pallas_call(
        body, name=name, grid=(M // tm, N // tn, nk),
        out_shape=jax.ShapeDtypeStruct((M, N), out_dtype),
        in_specs=in_specs, out_specs=o_spec,
        scratch_shapes=[pltpu.VMEM((tm, tn), F32)] if nk > 1 else [],
        compiler_params=_params(("parallel", "parallel", "arbitrary"), est),
    )(*args)


def _mm_res_ln(a, w, res_hat, res_g, res_b, ln_g, ln_b, *, name, tm, target=None):
    T, K = a.shape
    D = w.shape[1]
    tm = min(tm, T)
    head = target is not None

    def body(a_ref, w_ref, rh_ref, rg_ref, rb_ref, g_ref, b_ref, *rest):
        branch = jnp.dot(a_ref[...].astype(BF16), w_ref[...], preferred_element_type=F32)
        z = ALPHA * (rh_ref[...] * rg_ref[...] + rb_ref[...]) + branch
        mu = jnp.mean(z, axis=1, keepdims=True)
        zc = z - mu
        var = jnp.mean(zc * zc, axis=1, keepdims=True)
        rstd = lax.rsqrt(var + LN_EPS)
        xhat = zc * rstd
        if not head:
            xhat_ref, rstd_ref, xb_ref = rest
            xhat_ref[...] = xhat
            rstd_ref[...] = rstd
            xb_ref[...] = (xhat * g_ref[...] + b_ref[...]).astype(BF16)
            return
        t_ref, dz_ref, dzb_ref, st_ref = rest

        @pl.when(pl.program_id(0) == 0)
        def _():
            st_ref[...] = jnp.zeros_like(st_ref)

        g = g_ref[...]
        err = (xhat * g + b_ref[...]) - t_ref[...]
        dx = err * (1.0 / D)
        st_ref[2:3, :] += 0.5 * jnp.sum(jnp.sum(err * err, axis=1, keepdims=True) * (1.0 / D), axis=0, keepdims=True)
        st_ref[0:1, :] += jnp.sum(dx * xhat, axis=0, keepdims=True)
        st_ref[1:2, :] += jnp.sum(dx, axis=0, keepdims=True)
        dxh = dx * g
        dz = rstd * (dxh - jnp.mean(dxh, axis=1, keepdims=True) - xhat * jnp.mean(dxh * xhat, axis=1, keepdims=True))
        dz_ref[...] = dz
        dzb_ref[...] = dz.astype(BF16)

    row = pl.BlockSpec((tm, D), lambda i: (i, 0))
    vec = pl.BlockSpec((1, D), lambda i: (0, 0))
    est = (2 * (_nbytes((tm, K), a.dtype) + _nbytes((K, D), BF16)) + 4 * _nbytes((tm, D), F32) * 2
           + 8 * _nbytes((tm, D), F32))
    in_specs = [pl.BlockSpec((tm, K), lambda i: (i, 0)), pl.BlockSpec((K, D), lambda i: (0, 0)), row, vec, vec, vec, vec]
    if head:
        return pl.pallas_call(
            body, name=name, grid=(T // tm,),
            out_shape=(jax.ShapeDtypeStruct((T, D), F32), jax.ShapeDtypeStruct((T, D), BF16),
                       jax.ShapeDtypeStruct((8, D), F32)),
            in_specs=in_specs + [row], out_specs=(row, row, pl.BlockSpec((8, D), lambda i: (0, 0))),
            compiler_params=_params(("arbitrary",), est),
        )(a, w, res_hat, res_g, res_b, ln_g, ln_b, target)
    return pl.pallas_call(
        body, name=name, grid=(T // tm,),
        out_shape=(jax.ShapeDtypeStruct((T, D), F32), jax.ShapeDtypeStruct((T, 1), F32),
                   jax.ShapeDtypeStruct((T, D), BF16)),
        in_specs=in_specs, out_specs=(row, pl.BlockSpec((tm, 1), lambda i: (i, 0)), row),
        compiler_params=_params(("parallel",), est),
    )(a, w, res_hat, res_g, res_b, ln_g, ln_b)


def _ln_bwd(xhat, rstd, ln_g, dx, *, name, tm=256):
    T, D = xhat.shape
    tm = min(tm, T)

    def body(xhat_ref, rstd_ref, g_ref, d_ref, dz_ref, dzb_ref, st_ref):
        i = pl.program_id(0)

        @pl.when(i == 0)
        def _():
            st_ref[...] = jnp.zeros_like(st_ref)

        xh = xhat_ref[...]
        dxv = d_ref[...]
        st_ref[0:1, :] += jnp.sum(dxv * xh, axis=0, keepdims=True)
        st_ref[1:2, :] += jnp.sum(dxv, axis=0, keepdims=True)
        dxh = dxv * g_ref[...]
        m1 = jnp.mean(dxh, axis=1, keepdims=True)
        m2 = jnp.mean(dxh * xh, axis=1, keepdims=True)
        dz = rstd_ref[...] * (dxh - m1 - xh * m2)
        dz_ref[...] = dz
        dzb_ref[...] = dz.astype(BF16)

    row = pl.BlockSpec((tm, D), lambda i: (i, 0))
    vec = pl.BlockSpec((1, D), lambda i: (0, 0))
    est = 2 * 4 * _nbytes((tm, D), F32) + 6 * _nbytes((tm, D), F32)
    return pl.pallas_call(
        body, name=name, grid=(T // tm,),
        out_shape=(jax.ShapeDtypeStruct((T, D), F32), jax.ShapeDtypeStruct((T, D), BF16),
                   jax.ShapeDtypeStruct((8, D), F32)),
        in_specs=[row, pl.BlockSpec((tm, 1), lambda i: (i, 0)), vec, row],
        out_specs=(row, row, pl.BlockSpec((8, D), lambda i: (0, 0))),
        compiler_params=_params(("arbitrary",), est),
    )(xhat, rstd, ln_g, dx)


def _pair_swap(v, even):
    return jnp.where(even, pltpu.roll(v, LANES - 1, 1), pltpu.roll(v, 1, 1))


def _half_sums(v, lo):
    s_lo = jnp.sum(jnp.where(lo, v, 0.0), axis=1, keepdims=True)
    s_hi = jnp.sum(jnp.where(lo, 0.0, v), axis=1, keepdims=True)
    return jnp.where(lo, s_lo, s_hi)


A_COLS = ROPE_W + KW
A_HEADS = A_COLS // HEAD_DIM
A_K0, A_V0 = KV * GQ, KV * GQ + KV


def _qk_rope_fwd(h, gains, cos2, sin2, *, name, tm=256):
    T = h.shape[0]
    tm = min(tm, T)
    nch = A_COLS // LANES

    def body(h_ref, g_ref, c_ref, s_ref, oT_ref, kv_ref):
        lane = lax.broadcasted_iota(jnp.int32, (tm, LANES), 1)
        lo, even = lane < HEAD_DIM, lane % 2 == 0
        c, s = c_ref[...], s_ref[...]
        for j in range(nch):
            x = h_ref[:, j * LANES:(j + 1) * LANES]
            isq, isv = j < QW // LANES, j == nch - 1
            if isv:
                out = x
            else:
                g = g_ref[0:1, :] if isq else g_ref[1:2, :]
                r = lax.rsqrt(_half_sums(x * x, lo) * (1.0 / HEAD_DIM) + RMS_EPS)
                nrm = x * r * g
                out = nrm * c + _pair_swap(nrm, even) * s
            if isq:
                out = out * (SCALE * LOG2E)
            else:
                kv_ref[:, (j - QW // LANES) * LANES:(j - QW // LANES + 1) * LANES] = out.astype(BF16)
            oT_ref[j * LANES:(j + 1) * LANES, :] = out.T.astype(BF16)

    est = 2 * (_nbytes((tm, A_COLS), F32) + 2 * _nbytes((tm, A_COLS), BF16) + 2 * _nbytes((tm, LANES), F32)) + (4 << 20)
    return pl.pallas_call(
        body, name=name, grid=(T // tm,),
        out_shape=(jax.ShapeDtypeStruct((A_COLS, T), BF16), jax.ShapeDtypeStruct((T, 2 * KW), BF16)),
        in_specs=[pl.BlockSpec((tm, A_COLS), lambda i: (i, 0)), pl.BlockSpec((8, LANES), lambda i: (0, 0)),
                  pl.BlockSpec((tm, LANES), lambda i: (i, 0)), pl.BlockSpec((tm, LANES), lambda i: (i, 0))],
        out_specs=(pl.BlockSpec((A_COLS, tm), lambda i: (0, i)), pl.BlockSpec((tm, 2 * KW), lambda i: (i, 0))),
        compiler_params=_params(("parallel",), est),
    )(h, gains, cos2, sin2)


def _qk_rope_bwd(h, dqT, dkT, gains, cos2, sin2, *, name, tm=256):
    T = h.shape[0]
    tm = min(tm, T)
    nch = ROPE_W // LANES

    def body(h_ref, dq_ref, dk_ref, g_ref, c_ref, s_ref, dh_ref, dg_ref):
        i = pl.program_id(0)

        @pl.when(i == 0)
        def _():
            dg_ref[...] = jnp.zeros_like(dg_ref)

        lane = lax.broadcasted_iota(jnp.int32, (tm, LANES), 1)
        lo, even = lane < HEAD_DIM, lane % 2 == 0
        c, s = c_ref[...], s_ref[...]
        acc = [None, None]
        for j in range(nch):
            x = h_ref[:, j * LANES:(j + 1) * LANES]
            isq = j < QW // LANES
            g = g_ref[0:1, :] if isq else g_ref[1:2, :]
            d = dq_ref[j * LANES:(j + 1) * LANES, :].T * SCALE if isq else dk_ref[...].T
            r = lax.rsqrt(_half_sums(x * x, lo) * (1.0 / HEAD_DIM) + RMS_EPS)
            dn = d * c + _pair_swap(d * s, even)
            xr = x * r
            part = jnp.sum(dn * xr, axis=0, keepdims=True)
            acc[0 if isq else 1] = part if acc[0 if isq else 1] is None else acc[0 if isq else 1] + part
            dng = dn * g
            dx = r * dng - xr * (r * r) * (_half_sums(dng * x, lo) * (1.0 / HEAD_DIM))
            dh_ref[:, j * LANES:(j + 1) * LANES] = dx.astype(BF16)
        for row in range(2):
            folded = acc[row] + pltpu.roll(acc[row], HEAD_DIM, 1)
            dg_ref[row:row + 1, :] += folded

    est = 2 * (2 * _nbytes((tm, ROPE_W), F32) + _nbytes((tm, ROPE_W), BF16) + 2 * _nbytes((tm, LANES), F32)) + (4 << 20)
    return pl.pallas_call(
        body, name=name, grid=(T // tm,),
        out_shape=(jax.ShapeDtypeStruct((T, ROPE_W), BF16), jax.ShapeDtypeStruct((8, LANES), F32)),
        in_specs=[pl.BlockSpec((tm, ROPE_W), lambda i: (i, 0)), pl.BlockSpec((QW, tm), lambda i: (0, i)),
                  pl.BlockSpec((KW, tm), lambda i: (0, i)), pl.BlockSpec((8, LANES), lambda i: (0, 0)),
                  pl.BlockSpec((tm, LANES), lambda i: (i, 0)), pl.BlockSpec((tm, LANES), lambda i: (i, 0))],
        out_specs=(pl.BlockSpec((tm, ROPE_W), lambda i: (i, 0)), pl.BlockSpec((8, LANES), lambda i: (0, 0))),
        compiler_params=_params(("arbitrary",), est),
    )(h, dqT, dkT, gains, cos2, sin2)


def _attn_a_fwd(k, hT, *, comm=None, tq=4096, tk=2048, cq=512):
    G, T, HD = k.shape
    HE = HD + ONES_ROWS
    tq, tk = min(tq, T), min(tk, T)
    cq = min(cq, tq)
    nk, nt = T // tk, T // tq
    grid = (G, GQ * nt, nk)

    def body(*refs):
        (k_ref, qT_ref, v_ref, oT_ref, lse_ref, m_sc, acc_sc), comm_start, comm_finish = _host_comm(
            comm, refs, 3, 2, 2, grid)
        kv = pl.program_id(2)
        comm_start()
        v1T = jnp.concatenate([v_ref[...], jnp.ones((ONES_ROWS, tk), BF16)], axis=0)

        @pl.when(kv == 0)
        def _():
            m_sc[...] = jnp.full_like(m_sc, NEG)
            acc_sc[...] = jnp.zeros_like(acc_sc)

        def scores(c):
            return jnp.dot(k_ref[...], qT_ref[:, c * cq:(c + 1) * cq], preferred_element_type=F32)

        nc = tq // cq
        ahead = scores(0)
        for c in range(nc):
            cols = slice(c * cq, (c + 1) * cq)
            sT = ahead
            if c + 1 < nc:
                ahead = scores(c + 1)
            m_prev = m_sc[:, cols]
            m_new = jnp.maximum(m_prev, jnp.max(sT, axis=0, keepdims=True))
            pT = jnp.exp2(sT - m_new).astype(BF16)
            acc_sc[:, cols] = (jnp.exp2(m_prev - m_new) * acc_sc[:, cols]
                               + jnp.dot(v1T, pT, preferred_element_type=F32))
            m_sc[:, cols] = m_new

        @pl.when(kv == nk - 1)
        def _():
            l = acc_sc[HD:HD + 1, :]
            oT_ref[...] = acc_sc[0:HD, :] / l
            lse_ref[...] = m_sc[...] + jnp.log2(l)

        comm_finish()

    qtr = pl.BlockSpec((None, HD, tq), lambda g, i, j: (g * GQ + i // nt, 0, i % nt))
    qvec = pl.BlockSpec((None, 1, tq), lambda g, i, j: (g * GQ + i // nt, 0, i % nt))
    est = 6 * _nbytes((cq, tk), F32) + (8 << 20)
    hosted = comm is not None
    return pl.pallas_call(
        body, name="attn_a_fwd_comm" if hosted else "attn_a_fwd", grid=grid,
        out_shape=[jax.ShapeDtypeStruct((G * GQ, HD, T), F32), jax.ShapeDtypeStruct((G * GQ, 1, T), F32)]
        + (comm.out_shape if hosted else []),
        in_specs=[pl.BlockSpec((None, tk, HD), lambda g, i, j: (g, j, 0)), qtr,
                  pl.BlockSpec((None, HD, tk), lambda g, i, j: (A_V0 + g, 0, j))] + (comm.in_specs if hosted else []),
        out_specs=[qtr, qvec] + (comm.out_specs if hosted else []),
        scratch_shapes=[pltpu.VMEM((1, tq), F32), pltpu.VMEM((HE, tq), F32)] + (comm.scratch if hosted else []),
        compiler_params=_params(("arbitrary",) * 3 if hosted else ("parallel", "parallel", "arbitrary"), est),
    )(k, hT, hT, *(comm.parts if hosted else []))


def _attn_a_bwd(k, v, hT, doT, lse_row, delta_row, *, comm=None, tq=4096, tk=1024, cq=256):
    G, T, HD = k.shape
    tq, tk = min(tq, T), min(tk, T)
    cq = min(cq, tq)
    nqt = T // tq
    nq, nc = GQ * nqt, tq // cq
    nt = (((1,), (1,)), ((), ()))

    grid = (G, T // tk, nq)

    def body(*refs):
        (k_ref, v_ref, kT_ref, qT_ref, doT_ref, lse_ref, dl_ref, dkT_ref, dvT_ref, dqT_ref, dk_sc, dv_sc), \
            comm_start, comm_finish = _host_comm(comm, refs, 7, 3, 2, grid)
        j, i = pl.program_id(1), pl.program_id(2)
        comm_start()

        @pl.when((j == 0) & (i == 0))
        def _():
            dqT_ref[...] = jnp.zeros_like(dqT_ref)

        @pl.when(i == 0)
        def _():
            dk_sc[...] = jnp.zeros_like(dk_sc)
            dv_sc[...] = jnp.zeros_like(dv_sc)

        def scores(c):
            cols = slice(c * cq, (c + 1) * cq)
            return (jnp.dot(k_ref[...], qT_ref[:, cols], preferred_element_type=F32),
                    jnp.dot(v_ref[...], doT_ref[:, cols], preferred_element_type=F32))

        ahead = scores(0)
        dk_part = dv_part = None
        for c in range(nc):
            cols = slice(c * cq, (c + 1) * cq)
            sT, dpT = ahead
            if c + 1 < nc:
                ahead = scores(c + 1)
            pT = jnp.exp2(sT - lse_ref[:, cols])
            dsT = (pT * (dpT - dl_ref[:, cols])).astype(BF16)
            dv_c = lax.dot_general(doT_ref[:, cols], pT.astype(BF16), nt, preferred_element_type=F32)
            dk_c = lax.dot_general(qT_ref[:, cols], dsT, nt, preferred_element_type=F32)
            dv_part = dv_c if dv_part is None else dv_part + dv_c
            dk_part = dk_c if dk_part is None else dk_part + dk_c
            out_cols = pl.ds(pl.multiple_of((i % nqt) * tq + c * cq, cq), cq)
            dqT_ref[i // nqt, :, out_cols] += jnp.dot(kT_ref[...], dsT, preferred_element_type=F32)
        dk_sc[...] += dk_part
        dv_sc[...] += dv_part

        @pl.when(i == nq - 1)
        def _():
            dkT_ref[...] = dk_sc[...] * LN2
            dvT_ref[...] = dv_sc[...]

        comm_finish()

    krow = pl.BlockSpec((None, tk, HD), lambda g, j, i: (g, j, 0))
    ktr = pl.BlockSpec((None, HD, tk), lambda g, j, i: (g, 0, j))
    ktr_h = pl.BlockSpec((None, HD, tk), lambda g, j, i: (A_K0 + g, 0, j))
    qtr = pl.BlockSpec((None, HD, tq), lambda g, j, i: (g * GQ + i // nqt, 0, i % nqt))
    qvec = pl.BlockSpec((None, 1, tq), lambda g, j, i: (g * GQ + i // nqt, 0, i % nqt))
    whole = pl.BlockSpec((GQ, HD, T), lambda g, j, i: (g, 0, 0))
    est = 8 * _nbytes((cq, tk), F32) + 2 * _nbytes((GQ, HD, T), F32) + (8 << 20)
    hosted = comm is not None
    return pl.pallas_call(
        body, name="attn_a_bwd_comm" if hosted else "attn_a_bwd", grid=grid,
        out_shape=[jax.ShapeDtypeStruct((G, HD, T), F32), jax.ShapeDtypeStruct((G, HD, T), F32),
                   jax.ShapeDtypeStruct((G * GQ, HD, T), F32)] + (comm.out_shape if hosted else []),
        in_specs=[krow, krow, ktr_h, qtr, qtr, qvec, qvec] + (comm.in_specs if hosted else []),
        out_specs=[ktr, ktr, whole] + (comm.out_specs if hosted else []),
        scratch_shapes=[pltpu.VMEM((HD, tk), F32), pltpu.VMEM((HD, tk), F32)] + (comm.scratch if hosted else []),
        compiler_params=_params(("arbitrary", "arbitrary", "arbitrary"), est),
    )(k, v, hT, hT, doT, lse_row, delta_row, *(comm.parts if hosted else []))


WB = WINDOW
WK = 3 * WINDOW


QB_COL0 = (ROPE_W + KW) // (GQ * HEAD_DIM)
KB_COL = (ROPE_W + KW + QW) // KW
GW = GQ * HEAD_DIM


WSTEP = 2


def _win_in_specs(T):
    nb = T // WB
    assert nb % WSTEP == 0
    ns = nb // WSTEP
    q = [pl.BlockSpec((WSTEP * WB, GW), functools.partial(lambda n, g: (n, QB_COL0 + g), g=g)) for g in range(KV)]
    kv = [pl.BlockSpec((WB, KW), functools.partial(lambda n, o, c: (jnp.clip(WSTEP * n + o, 0, nb - 1), c), o=o, c=c))
          for c in (KB_COL, KB_COL + 1) for o in range(-1, WSTEP + 1)]
    bias = [pl.BlockSpec((None, KV, WK, GQ * WB), lambda n: (jnp.where(n == 0, 0, 1), 0, 0, 0)),
            pl.BlockSpec((None, KV, WK, GQ * WB), lambda n: (jnp.where(n == ns - 1, 2, 1), 0, 0, 0))]
    return ns, q + kv, bias


def _end_tables(biasT):
    key = lax.broadcasted_iota(jnp.int32, biasT.shape, 1)
    return jnp.stack([jnp.where(key < WB, NEG, biasT), biasT, jnp.where(key >= 2 * WB, NEG, biasT)])


def _heads_to_lanes(t):
    return jnp.concatenate([t[i * HEAD_DIM:(i + 1) * HEAD_DIM] for i in range(GQ)], axis=1)


def _lanes_to_heads(t):
    return jnp.concatenate([t[:, i * WB:(i + 1) * WB] for i in range(GQ)], axis=0)


def _attn_b_fwd(h, biasT, sink_rows):
    T = h.shape[0]
    ns, in_specs, bias_specs = _win_in_specs(T)
    nkv = WSTEP + 2

    def body(*refs):
        q_refs, k_refs, v_refs = refs[:KV], refs[KV:KV + nkv], refs[KV + nkv:KV + 2 * nkv]
        b_refs, sk_ref, o_ref, lse_ref = refs[KV + 2 * nkv:KV + 2 * nkv + WSTEP], *refs[KV + 2 * nkv + WSTEP:]
        ks, vs = [r[...] for r in k_refs], [r[...] for r in v_refs]
        outs = []
        for b, b_ref in enumerate(b_refs):
            rows = slice(b * WB, (b + 1) * WB)
            kwin = jnp.concatenate(ks[b:b + 3], axis=0)
            vT = jnp.concatenate(vs[b:b + 3], axis=0).T
            qT = [_heads_to_lanes((q[rows, :] * SCALE).T).astype(BF16) for q in q_refs]
            sT = [jnp.dot(kwin[:, g * HEAD_DIM:(g + 1) * HEAD_DIM].astype(BF16), qT[g], preferred_element_type=F32)
                  for g in range(KV)]
            oT = []
            for g in range(KV):
                s = sT[g] + b_ref[g]
                sk = sk_ref[g]
                m = jnp.maximum(jnp.max(s, axis=0, keepdims=True), sk)
                p = jnp.exp(s - m)
                den = jnp.sum(p, axis=0, keepdims=True) + jnp.exp(sk - m)
                o = jnp.dot(vT[g * HEAD_DIM:(g + 1) * HEAD_DIM].astype(BF16), p.astype(BF16),
                            preferred_element_type=F32) / den
                lse_ref[b, g] = m + jnp.log(den)
                oT.append(_lanes_to_heads(o))
            outs.append(jnp.concatenate(oT, axis=0).T)
        o_ref[...] = jnp.concatenate(outs, axis=0)

    whole = lambda *shape: pl.BlockSpec(shape, lambda n: (0,) * len(shape))
    return pl.pallas_call(
        body, name="attn_b_fwd", grid=(ns,),
        out_shape=(jax.ShapeDtypeStruct((T, QW), F32), jax.ShapeDtypeStruct((ns * WSTEP, KV, 1, GQ * WB), F32)),
        in_specs=in_specs + bias_specs + [whole(KV, 1, GQ * WB)],
        out_specs=(pl.BlockSpec((WSTEP * WB, QW), lambda n: (n, 0)),
                   pl.BlockSpec((WSTEP, KV, 1, GQ * WB), lambda n: (n, 0, 0, 0))),
        compiler_params=_params(("parallel",), 32 << 20),
    )(*([h] * (KV + 2 * nkv)), *([biasT] * WSTEP), sink_rows)


def _attn_b_bwd(h, do, o, lse, biasT, sink_rows):
    T = h.shape[0]
    ns, in_specs, bias_specs = _win_in_specs(T)
    nkv = WSTEP + 2
    Tp = T + 2 * WB
    nt = (((1,), (1,)), ((), ()))

    def body(*refs):
        q_refs, k_refs, v_refs = refs[:KV], refs[KV:KV + nkv], refs[KV + nkv:KV + 2 * nkv]
        at = KV + 2 * nkv
        do_ref, o_ref, lse_ref = refs[at:at + 3]
        b_refs, sk_ref = refs[at + 3:at + 3 + WSTEP], refs[at + 3 + WSTEP]
        dq_ref, dk_ref, dv_ref, db_ref, dsk_ref = refs[at + 4 + WSTEP:]
        n = pl.program_id(0)

        @pl.when(n == 0)
        def _():
            dk_ref[...] = jnp.zeros_like(dk_ref)
            dv_ref[...] = jnp.zeros_like(dv_ref)
            db_ref[...] = jnp.zeros_like(db_ref)
            dsk_ref[...] = jnp.zeros_like(dsk_ref)

        ks, vs = [r[...] for r in k_refs], [r[...] for r in v_refs]
        for b, b_ref in enumerate(b_refs):
            rows = slice(b * WB, (b + 1) * WB)
            kwin = jnp.concatenate(ks[b:b + 3], axis=0)
            vwin = jnp.concatenate(vs[b:b + 3], axis=0)
            kT = kwin.T
            doT_all, oT_all = do_ref[rows, :].T, o_ref[rows, :].T
            qT, doT, delta, sT, dpT = [], [], [], [], []
            for g, q in enumerate(q_refs):
                hd = slice(g * HEAD_DIM, (g + 1) * HEAD_DIM)
                qT.append(_heads_to_lanes((q[rows, :] * SCALE).T).astype(BF16))
                d = _heads_to_lanes(doT_all[g * GW:(g + 1) * GW])
                delta.append(jnp.sum(d * _heads_to_lanes(oT_all[g * GW:(g + 1) * GW]), axis=0, keepdims=True))
                doT.append(d.astype(BF16))
                sT.append(jnp.dot(kwin[:, hd].astype(BF16), qT[g], preferred_element_type=F32))
                dpT.append(jnp.dot(vwin[:, hd].astype(BF16), doT[g], preferred_element_type=F32))
            dq, dk, dv = [], [], []
            for g in range(KV):
                lse_g = lse_ref[b, g]
                p = jnp.exp(sT[g] + b_ref[g] - lse_g)
                ds = p * (dpT[g] - delta[g])
                db_ref[g] += ds
                dsk_ref[g] -= jnp.exp(sk_ref[g] - lse_g) * delta[g]
                dsb = ds.astype(BF16)
                dqT = jnp.dot(kT[g * HEAD_DIM:(g + 1) * HEAD_DIM].astype(BF16), dsb, preferred_element_type=F32)
                dq.append(_lanes_to_heads(dqT))
                dk.append(lax.dot_general(dsb, qT[g], nt, preferred_element_type=F32))
                dv.append(lax.dot_general(p.astype(BF16), doT[g], nt, preferred_element_type=F32))
            dq_ref[rows, :] = (jnp.concatenate(dq, axis=0).T * SCALE).astype(BF16)
            win = pl.ds(pl.multiple_of((WSTEP * n + b) * WB, WB), WK)
            dk_ref[win, :] += jnp.concatenate(dk, axis=1)
            dv_ref[win, :] += jnp.concatenate(dv, axis=1)

    whole = lambda *shape: pl.BlockSpec(shape, lambda n: (0,) * len(shape))
    tok = pl.BlockSpec((WSTEP * WB, QW), lambda n: (n, 0))
    return pl.pallas_call(
        body, name="attn_b_bwd", grid=(ns,),
        out_shape=(jax.ShapeDtypeStruct((T, QW), BF16),
                   jax.ShapeDtypeStruct((Tp, KW), F32), jax.ShapeDtypeStruct((Tp, KW), F32),
                   jax.ShapeDtypeStruct((KV, WK, GQ * WB), F32), jax.ShapeDtypeStruct((KV, 1, GQ * WB), F32)),
        in_specs=in_specs + [tok, tok, pl.BlockSpec((WSTEP, KV, 1, GQ * WB), lambda n: (n, 0, 0, 0))]
        + bias_specs + [whole(KV, 1, GQ * WB)],
        out_specs=(tok, whole(Tp, KW), whole(Tp, KW), whole(KV, WK, GQ * WB), whole(KV, 1, GQ * WB)),
        compiler_params=_params(("arbitrary",), 48 << 20),
    )(*([h] * (KV + 2 * nkv)), do, o, lse, *([biasT] * WSTEP), sink_rows)


def _bias_table(rel_bias_t, bucket):
    nh, n = rel_bias_t.shape[0], bucket.shape[1]

    def body(rb_ref, bk_ref, o_ref):
        bk = bk_ref[...]
        out = jnp.full((nh, n), NEG, F32)
        for b in range(N_BUCKETS):
            out = jnp.where(bk == b, rb_ref[:, b:b + 1], out)
        o_ref[...] = out

    return pl.pallas_call(
        body, name="bias_table", out_shape=jax.ShapeDtypeStruct((nh, n), F32),
        compiler_params=pltpu.CompilerParams(vmem_limit_bytes=32 << 20),
    )(rel_bias_t, bucket)


def _bias_sink_grads(db_list, dsk_list, bucket):
    L = len(db_list)

    def body(*refs):
        db_refs, dsk_refs, bk_ref = refs[:L], refs[L:2 * L], refs[2 * L]
        drb_ref, dsink_ref = refs[2 * L + 1], refs[2 * L + 2]
        tot = db_refs[0][...]
        for r in db_refs[1:]:
            tot = tot + r[...]
        bk = bk_ref[...]
        lane = lax.broadcasted_iota(jnp.int32, (2 * GQ, N_BUCKETS), 1)
        out = jnp.zeros((2 * GQ, N_BUCKETS), F32)
        for b in range(N_BUCKETS):
            sb = jnp.sum(jnp.where(bk == b, tot, 0.0), axis=1, keepdims=True)
            out = jnp.where(lane == b, sb, out)
        drb_ref[...] = out
        for l in range(L):
            dsink_ref[l] = jnp.sum(dsk_refs[l][...], axis=1, keepdims=True)

    return pl.pallas_call(
        body, name="bias_sink_grads",
        out_shape=(jax.ShapeDtypeStruct((2 * GQ, N_BUCKETS), F32), jax.ShapeDtypeStruct((L, 2 * GQ, 1), F32)),
        compiler_params=pltpu.CompilerParams(vmem_limit_bytes=32 << 20),
    )(*db_list, *dsk_list, bucket)


def _outnorm_fwd(oaT, ob, ga, gb, *, tm=512):
    T = ob.shape[0]
    tm = min(tm, T)

    def body(oaT_ref, ob_ref, ga_ref, gb_ref, y_ref):
        for j, (o, g_ref) in enumerate(((oaT_ref[...].T, ga_ref), (ob_ref[...], gb_ref))):
            r = lax.rsqrt(jnp.mean(o * o, axis=1, keepdims=True) + RMS_EPS)
            y_ref[:, j * QW:(j + 1) * QW] = (o * r * g_ref[...]).astype(BF16)

    half = pl.BlockSpec((tm, QW), lambda i: (i, 0))
    halfT = pl.BlockSpec((QW, tm), lambda i: (0, i))
    vec = pl.BlockSpec((1, QW), lambda i: (0, 0))
    return pl.pallas_call(
        body, name="outnorm_fwd", grid=(T // tm,),
        out_shape=jax.ShapeDtypeStruct((T, 2 * QW), BF16),
        in_specs=[halfT, half, vec, vec], out_specs=pl.BlockSpec((tm, 2 * QW), lambda i: (i, 0)),
        compiler_params=_params(("parallel",), 16 << 20),
    )(oaT, ob, ga, gb)


def _outnorm_bwd(dy, oaT, ob, ga, gb, *, tm=512):
    T = ob.shape[0]
    tm = min(tm, T)
    nh = QW // HEAD_DIM

    def body(dy_ref, oaT_ref, ob_ref, ga_ref, gb_ref, doaT_ref, dl_ref, dob_ref, dg_ref):
        i = pl.program_id(0)

        @pl.when(i == 0)
        def _():
            dg_ref[...] = jnp.zeros_like(dg_ref)

        oaT = oaT_ref[...]
        for j, (o, g_ref) in enumerate(((oaT.T, ga_ref), (ob_ref[...], gb_ref))):
            d = dy_ref[:, j * QW:(j + 1) * QW]
            r = lax.rsqrt(jnp.mean(o * o, axis=1, keepdims=True) + RMS_EPS)
            orr = o * r
            dg_ref[j:j + 1, :] += jnp.sum(d * orr, axis=0, keepdims=True)
            dgv = d * g_ref[...]
            do = r * dgv - orr * (r * r) * jnp.mean(dgv * o, axis=1, keepdims=True)
            if j == 0:
                doT = do.T
                doaT_ref[...] = doT.astype(BF16)
                prod = doT * oaT
                dl_ref[...] = jnp.concatenate(
                    [jnp.sum(prod[a * HEAD_DIM:(a + 1) * HEAD_DIM], axis=0, keepdims=True) for a in range(nh)], axis=0)
            else:
                dob_ref[...] = do

    half = pl.BlockSpec((tm, QW), lambda i: (i, 0))
    halfT = pl.BlockSpec((QW, tm), lambda i: (0, i))
    vec = pl.BlockSpec((1, QW), lambda i: (0, 0))
    return pl.pallas_call(
        body, name="outnorm_bwd", grid=(T // tm,),
        out_shape=(jax.ShapeDtypeStruct((QW, T), BF16), jax.ShapeDtypeStruct((nh, T), F32),
                   jax.ShapeDtypeStruct((T, QW), F32), jax.ShapeDtypeStruct((8, QW), F32)),
        in_specs=[pl.BlockSpec((tm, 2 * QW), lambda i: (i, 0)), halfT, half, vec, vec],
        out_specs=(halfT, pl.BlockSpec((nh, tm), lambda i: (0, i)), half, pl.BlockSpec((8, QW), lambda i: (0, 0))),
        compiler_params=_params(("arbitrary",), 32 << 20),
    )(dy, oaT, ob, ga, gb)


GELU_C = math.sqrt(2.0 / math.pi)
GELU_A = 0.044715
HALO = 16
SUB = 8


def _gelu_parts(x):
    x2 = x * x
    t = jnp.tanh(x * (GELU_C + (GELU_C * GELU_A) * x2))
    return 0.5 * (1.0 + t), t, x2


def _halo_specs(tm, tn, T):
    nh = tm // HALO
    last = T // HALO - 1
    cur = pl.BlockSpec((tm, tn), lambda j, i: (i, j))
    prev = pl.BlockSpec((HALO, tn), lambda j, i: (jnp.maximum(i * nh - 1, 0), j))
    nxt = pl.BlockSpec((HALO, tn), lambda j, i: (jnp.minimum((i + 1) * nh, last), j))
    return cur, prev, nxt


def _conv_glu_fwd(g, u, conv_w, conv_b, *, tm=256, tn=1408):
    T, F = g.shape
    tm, tn = min(tm, T), min(tn, F)
    cur, prev, nxt = _halo_specs(tm, tn, T)

    def body(g_ref, gp_ref, gn_ref, u_ref, w_ref, b_ref, a_ref):
        i = pl.program_id(1)
        gv = g_ref[...]
        before = jnp.where(i * tm > 0, gp_ref[HALO - SUB:, :], 0.0)
        after = jnp.where((i + 1) * tm < T, gn_ref[0:SUB, :], 0.0)
        gm1 = pltpu.roll(jnp.concatenate([before, gv], axis=0), 1, 0)[SUB:]
        gp1 = pltpu.roll(jnp.concatenate([gv, after], axis=0), tm + SUB - 1, 0)[:tm]
        gc = ((b_ref[...] + gm1 * w_ref[0:1, :]) + gv * w_ref[1:2, :]) + gp1 * w_ref[2:3, :]
        cdf, _, _ = _gelu_parts(gc)
        a_ref[...] = (gc * cdf * u_ref[...].astype(F32)).astype(BF16)

    wspec = pl.BlockSpec((8, tn), lambda j, i: (0, j))
    est = 2 * (3 * _nbytes((tm, tn), F32)) + 8 * _nbytes((tm, tn), F32)
    return pl.pallas_call(
        body, name="conv_glu_fwd", grid=(F // tn, T // tm),
        out_shape=jax.ShapeDtypeStruct((T, F), BF16),
        in_specs=[cur, prev, nxt, cur, wspec, pl.BlockSpec((1, tn), lambda j, i: (0, j))],
        out_specs=cur,
        compiler_params=_params(("parallel", "parallel"), est),
    )(g, g, g, u, conv_w, conv_b)


def _conv_glu_bwd(dact, g, u, conv_w, conv_b, *, tm=256, tn=1408):
    T, F = g.shape
    tm, tn = min(tm, T), min(tn, F)
    cur, prev, nxt = _halo_specs(tm, tn, T)
    te = tm + 2 * HALO

    def body(d_ref, dp_ref, dn_ref, g_ref, gp_ref, gn_ref, u_ref, up_ref, un_ref, w_ref, b_ref,
             dg_ref, du_ref, dc_ref):
        i = pl.program_id(1)

        @pl.when(i == 0)
        def _():
            dc_ref[...] = jnp.zeros_like(dc_ref)

        has_prev, has_next = i * tm > 0, (i + 1) * tm < T
        ge = jnp.concatenate([jnp.where(has_prev, gp_ref[...], 0.0), g_ref[...],
                              jnp.where(has_next, gn_ref[...], 0.0)], axis=0)
        ue = jnp.concatenate([up_ref[...], u_ref[...], un_ref[...]], axis=0).astype(F32)
        de = jnp.concatenate([jnp.where(has_prev, dp_ref[...].astype(F32), 0.0), d_ref[...].astype(F32),
                              jnp.where(has_next, dn_ref[...].astype(F32), 0.0)], axis=0)
        w0, w1, w2 = w_ref[0:1, :], w_ref[1:2, :], w_ref[2:3, :]
        gm1 = pltpu.roll(ge, 1, 0)
        gp1 = pltpu.roll(ge, te - 1, 0)
        gc = ((b_ref[...] + gm1 * w0) + ge * w1) + gp1 * w2
        cdf, t, gc2 = _gelu_parts(gc)
        dgelu = cdf + (0.5 * gc) * (1.0 - t * t) * (GELU_C + (3.0 * GELU_C * GELU_A) * gc2)
        dgc = de * ue * dgelu
        dge = w0 * pltpu.roll(dgc, te - 1, 0) + w1 * dgc + w2 * pltpu.roll(dgc, 1, 0)
        mid = slice(HALO, HALO + tm)
        dg_ref[...] = dge[mid].astype(BF16)
        du_ref[...] = (de[mid] * (gc[mid] * cdf[mid])).astype(BF16)
        dgm = dgc[mid]
        dc_ref[0:1, :] += jnp.sum(dgm * gm1[mid], axis=0, keepdims=True)
        dc_ref[1:2, :] += jnp.sum(dgm * ge[mid], axis=0, keepdims=True)
        dc_ref[2:3, :] += jnp.sum(dgm * gp1[mid], axis=0, keepdims=True)
        dc_ref[3:4, :] += jnp.sum(dgm, axis=0, keepdims=True)

    wspec = pl.BlockSpec((8, tn), lambda j, i: (0, j))
    est = 2 * (3 * _nbytes((tm, tn), F32) + 2 * _nbytes((tm, tn), BF16)) + 16 * _nbytes((te, tn), F32)
    return pl.pallas_call(
        body, name="conv_glu_bwd", grid=(F // tn, T // tm),
        out_shape=(jax.ShapeDtypeStruct((T, F), BF16), jax.ShapeDtypeStruct((T, F), BF16),
                   jax.ShapeDtypeStruct((8, F), F32)),
        in_specs=[cur, prev, nxt, cur, prev, nxt, cur, prev, nxt, wspec, pl.BlockSpec((1, tn), lambda j, i: (0, j))],
        out_specs=(cur, cur, wspec),
        compiler_params=_params(("parallel", "arbitrary"), est),
    )(dact, dact, dact, g, g, g, u, u, u, conv_w, conv_b)


def _adamw_math(w, g, m, v):
    m = ADAM_B1 * m + (1.0 - ADAM_B1) * g
    v = ADAM_B2 * v + (1.0 - ADAM_B2) * (g * g)
    m_hat = m / (1.0 - ADAM_B1 ** ADAM_STEP)
    v_hat = v / (1.0 - ADAM_B2 ** ADAM_STEP)
    delta = -ADAM_LR * (m_hat / (jnp.sqrt(v_hat) + ADAM_EPS) + ADAM_WD * w)
    return delta, m, v


def _adamw(w, m, v, gparts, *, name):
    R, C = w.shape
    r = gparts[0].shape[1]
    assert len(gparts) * r == R
    tr = max(d for d in range(8, min(r, 256) + 1, 8) if r % d == 0) if r % 8 == 0 else r
    per = r // tr

    def body(w_ref, m_ref, v_ref, *rest):
        gp_refs, (g_ref, d_ref, nm_ref, nv_ref) = rest[:len(gparts)], rest[len(gparts):]
        i = pl.program_id(0)
        for l, gp_ref in enumerate(gp_refs):
            @pl.when(i // per == l)
            def _(gp_ref=gp_ref):
                g = gp_ref[0].astype(F32)
                for j in range(1, N_DEV):
                    g = g + gp_ref[j].astype(F32)
                delta, nm, nv = _adamw_math(w_ref[...], g, m_ref[...], v_ref[...])
                g_ref[...] = g
                d_ref[...] = delta
                nm_ref[...] = nm
                nv_ref[...] = nv

    blk = pl.BlockSpec((tr, C), lambda i: (i, 0))
    out = jax.ShapeDtypeStruct((R, C), F32)
    return pl.pallas_call(
        body, name=name, grid=(R // tr,), out_shape=(out, out, out, out),
        in_specs=[blk, blk, blk] + [pl.BlockSpec((N_DEV, tr, C), lambda i: (0, i % per, 0))] * len(gparts),
        out_specs=(blk, blk, blk, blk),
        compiler_params=_params(("parallel",), 24 << 20),
    )(w, m, v, *gparts)


def _rope_tables(T):
    rows_n = T // GRID_W
    row = jnp.repeat(jnp.arange(rows_n, dtype=F32), GRID_W)
    col = jnp.tile(jnp.arange(GRID_W, dtype=F32), rows_n)
    half = HEAD_DIM // 2
    inv_freq = ROPE_THETA ** (-jnp.arange(0, half, 2, dtype=F32) / half)
    ang = jnp.concatenate([row[:, None] * inv_freq, col[:, None] * inv_freq], axis=-1)
    cos, sin = jnp.cos(ang), jnp.sin(ang)
    cos64 = jnp.repeat(cos, 2, axis=-1)
    sin64 = jnp.stack([-sin, sin], axis=-1).reshape(T, HEAD_DIM)
    return jnp.tile(cos64, (1, 2)), jnp.tile(sin64, (1, 2))


def _t5_bucket(rel):
    half = N_BUCKETS // 2
    max_exact = half // 2
    bucket = jnp.where(rel > 0, half, 0)
    rp = jnp.abs(rel)
    rpf = jnp.maximum(rp, 1).astype(F32)
    large = max_exact + (jnp.log(rpf / max_exact) / math.log(MAX_DISTANCE / max_exact)
                         * (half - max_exact)).astype(jnp.int32)
    large = jnp.minimum(large, half - 1)
    return bucket + jnp.where(rp < max_exact, rp, large)


def _window_buckets():
    qpos = jnp.arange(WB, dtype=jnp.int32)
    kpos = jnp.arange(WK, dtype=jnp.int32) - WB
    rel = kpos[None, :] - qpos[:, None]
    return jnp.where(jnp.abs(rel) <= WINDOW, _t5_bucket(rel), -1)


def _heads_first(a, nh):
    T = a.shape[0]
    return a.reshape(T, nh, HEAD_DIM).transpose(1, 0, 2)


def _row(v):
    return v.reshape(1, -1)


def _rows8(rows, width):
    a = jnp.stack(list(rows), axis=0)
    return jnp.pad(a, ((0, 8 - a.shape[0]), (0, 0)))


def _layer_fwd(l, xin, W, tabs, comm=None, on_comm=None, target=None):
    xhat, xg, xb, x16 = xin
    T = xhat.shape[0]
    cos2, sin2, biasT = tabs
    h = _mm([x16], [W["w_in"][l]], name="mm_in", out_dtype=F32, tm=MM_ROWS, tn=IN_COLS, tk=D_MODEL)
    gains = _rows8([jnp.tile(W["q_norm"][l], 2), jnp.tile(W["k_norm"][l], 2)], LANES)
    hT, kv_nat = _qk_rope_fwd(h, gains, cos2, sin2, name="qk_rope_fwd")
    hT = hT.reshape(A_HEADS, HEAD_DIM, T)
    ka, va = _heads_first(kv_nat[:, :KW], KV), _heads_first(kv_nat[:, KW:], KV)
    res = _attn_a_fwd(ka, hT, comm=comm)
    oaT, lse_a = res[0].reshape(QW, T), res[1]
    if comm is not None:
        on_comm(res[2:])
    sink_rows = jnp.repeat(W["sink"][l], WB).reshape(KV, 1, GQ * WB)
    ob_t, lse_b = _attn_b_fwd(h, biasT, sink_rows)
    ga, gb = _row(W["out_norm_a"][l]), _row(W["out_norm_b"][l])
    ycat = _outnorm_fwd(oaT, ob_t, ga, gb)
    g1, b1 = _row(W["ln1_g"][l]), _row(W["ln1_b"][l])
    x1hat, rstd1, x1_16 = _mm_res_ln(ycat, W["w_out"][l], xhat, xg, xb, g1, b1, name="mm_out_ln", tm=512)
    gate = _mm([x1_16], [W["w_gate"][l]], name="mm_gate", out_dtype=F32, tm=MM_ROWS_WIDE, tn=D_FF, tk=D_MODEL)
    up = _mm([x1_16], [W["w_up"][l]], name="mm_up", out_dtype=BF16, tm=MM_ROWS_WIDE, tn=D_FF, tk=D_MODEL)
    cw = jnp.pad(W["conv_w"][l], ((0, 5), (0, 0)))
    cb = _row(W["conv_b"][l])
    act = _conv_glu_fwd(gate, up, cw, cb)
    g2, b2 = _row(W["ln2_g"][l]), _row(W["ln2_b"][l])
    saved = dict(x16=x16, h=h, gains=gains, hT=hT, ka=ka, va=va, oaT=oaT, lse_a=lse_a,
                 lse_b=lse_b, sink_rows=sink_rows, ob_t=ob_t,
                 ga=ga, gb=gb, ycat=ycat, x1hat=x1hat, rstd1=rstd1, x1_16=x1_16, g1=g1, b1=b1, gate=gate, up=up,
                 cw=cw, cb=cb, act=act, g2=g2, b2=b2)
    if target is not None:
        saved["head"] = _mm_res_ln(act, W["w_down"][l], x1hat, g1, b1, g2, b2, name="mm_down_ln_loss", tm=256,
                                   target=target)
        return None, saved
    x2hat, rstd2, x2_16 = _mm_res_ln(act, W["w_down"][l], x1hat, g1, b1, g2, b2, name="mm_down_ln", tm=256)
    saved.update(x2hat=x2hat, rstd2=rstd2)
    return (x2hat, g2, b2, x2_16), saved


def _layer_bwd(l, S, W, tabs, dz2, dz2_16, stats2, scatter=None):
    cos2, sin2, biasT = tabs
    T = dz2.shape[0]
    G = {}
    G["ln2_g"], G["ln2_b"] = stats2[0], stats2[1]
    G["w_down"] = _mm([S["act"]], [dz2_16], name="dw_down", out_dtype=BF16, trans_a=True, tm=D_FF // 2, tn=D_MODEL, tk=DW_TOKENS)
    dact = _mm([dz2_16], [W["w_down"][l]], name="mm_dact", out_dtype=BF16, trans_b=True, tm=MM_ROWS_WIDE, tn=D_FF,
               tk=D_MODEL)
    dg, du, dconv = _conv_glu_bwd(dact, S["gate"], S["up"], S["cw"], S["cb"])
    G["conv_w"], G["conv_b"] = dconv[0:3], dconv[3]
    G["w_gate"] = _mm([S["x1_16"]], [dg], name="dw_gate", out_dtype=BF16, trans_a=True, tm=D_MODEL, tn=D_FF // 2, tk=DW_TOKENS)
    G["w_up"] = _mm([S["x1_16"]], [du], name="dw_up", out_dtype=BF16, trans_a=True, tm=D_MODEL, tn=D_FF // 2, tk=DW_TOKENS)
    dz1, dz1_16, stats1 = _mm([dg, du], [W["w_gate"][l], W["w_up"][l]], name="mm_dx1_ln", out_dtype=F32, trans_b=True,
                              tm=MM_ROWS_LN, tn=D_MODEL, tk=D_FF, add=dz2, add_scale=ALPHA,
                              ln=(S["x1hat"], S["rstd1"], S["g1"]))
    G["ln1_g"], G["ln1_b"] = stats1[0], stats1[1]
    G["w_out"] = _mm([S["ycat"]], [dz1_16], name="dw_out", out_dtype=BF16, trans_a=True, tm=D_MODEL, tn=D_MODEL, tk=DW_TOKENS)
    dycat = _mm([dz1_16], [W["w_out"][l]], name="mm_dycat", out_dtype=F32, trans_b=True, tm=MM_ROWS, tn=D_MODEL,
                tk=D_MODEL)
    doaT, delta, dob_t, dgn = _outnorm_bwd(dycat, S["oaT"], S["ob_t"], S["ga"], S["gb"])
    G["out_norm_a"], G["out_norm_b"] = dgn[0], dgn[1]
    res = _attn_a_bwd(S["ka"], S["va"], S["hT"], doaT.reshape(KV * GQ, HEAD_DIM, T), S["lse_a"],
                      delta.reshape(KV * GQ, 1, T), comm=scatter(G) if scatter is not None else None)
    dkaT, dvaT, dqaT = res[:3]
    dh_rope, dgain = _qk_rope_bwd(S["h"], dqaT.reshape(QW, T), dkaT.reshape(KW, T), S["gains"], cos2, sin2,
                                  name="qk_rope_bwd")
    G["q_norm"], G["k_norm"] = dgain[0, :HEAD_DIM], dgain[1, :HEAD_DIM]
    dqb_t, dkb, dvb, dbiasT, dsk = _attn_b_bwd(S["h"], dob_t, S["ob_t"], S["lse_b"], biasT, S["sink_rows"])
    dh = jnp.concatenate([
        dh_rope, dvaT.transpose(2, 0, 1).reshape(T, KW).astype(BF16), dqb_t,
        dkb[WB:WB + T].astype(BF16), dvb[WB:WB + T].astype(BF16)], axis=1)
    dbias = dbiasT.reshape(KV, WK, GQ, WB).transpose(0, 2, 1, 3)
    G["w_in"] = _mm([S["x16"]], [dh], name="dw_in", out_dtype=BF16, trans_a=True, tm=D_MODEL, tn=IN_COLS, tk=DW_TOKENS)
    dxin = _mm([dh], [W["w_in"][l]], name="mm_dxin", out_dtype=F32, trans_b=True, tm=MM_ROWS, tn=D_MODEL, tk=IN_COLS,
               add=dz1, add_scale=ALPHA)
    return dxin, G, dbias.reshape(KV * GQ, WK * WB), dsk.reshape(KV * GQ, WB), res[3:]


BIG = ("w_in", "w_out", "w_gate", "w_up", "w_down")
COL_SHARDED = ("w_in", "w_gate", "w_up")


def _unshard(name, blocks):
    _, r, c = blocks.shape
    if name in COL_SHARDED:
        return blocks.transpose(1, 0, 2).reshape(r, N_DEV * c)
    return blocks.reshape(N_DEV * r, c)


def _to_owner_blocks(name, full, shard_shape):
    _, r, c = shard_shape
    if name in COL_SHARDED:
        return full.reshape(r, N_DEV, c).transpose(1, 0, 2)
    return full.reshape(N_DEV, r, c)


def _pack_small(vals, tail):
    flat = jnp.concatenate([vals[n].reshape(-1).astype(F32) for n in SMALL_NAMES] + [tail])
    pad = (-flat.shape[0]) % (8 * LANES)
    return jnp.pad(flat, (0, pad)).reshape(-1, LANES)


def _unpack_small(packed, shapes):
    flat = packed.reshape(-1)
    out, off = {}, 0
    for n in SMALL_NAMES:
        size = math.prod(shapes[n])
        out[n] = flat[off:off + size].reshape(shapes[n])
        off += size
    return out, flat[off]


def kernel(x, rel_bias, w_in, q_norm, k_norm, sink, out_norm_a, out_norm_b, w_out, ln1_g, ln1_b, w_gate, w_up, conv_w, conv_b, w_down, ln2_g, ln2_b, loss_target, m_rel_bias, m_w_in, m_q_norm, m_k_norm, m_sink, m_out_norm_a, m_out_norm_b, m_w_out, m_ln1_g, m_ln1_b, m_w_gate, m_w_up, m_conv_w, m_conv_b, m_w_down, m_ln2_g, m_ln2_b, v_rel_bias, v_w_in, v_q_norm, v_k_norm, v_sink, v_out_norm_a, v_out_norm_b, v_w_out, v_ln1_g, v_ln1_b, v_w_gate, v_w_up, v_conv_w, v_conv_b, v_w_down, v_ln2_g, v_ln2_b):
    P = dict(rel_bias=rel_bias, w_in=w_in, q_norm=q_norm, k_norm=k_norm, sink=sink, out_norm_a=out_norm_a,
             out_norm_b=out_norm_b, w_out=w_out, ln1_g=ln1_g, ln1_b=ln1_b, w_gate=w_gate, w_up=w_up, conv_w=conv_w,
             conv_b=conv_b, w_down=w_down, ln2_g=ln2_g, ln2_b=ln2_b)
    M = dict(rel_bias=m_rel_bias, w_in=m_w_in, q_norm=m_q_norm, k_norm=m_k_norm, sink=m_sink, out_norm_a=m_out_norm_a,
             out_norm_b=m_out_norm_b, w_out=m_w_out, ln1_g=m_ln1_g, ln1_b=m_ln1_b, w_gate=m_w_gate, w_up=m_w_up,
             conv_w=m_conv_w, conv_b=m_conv_b, w_down=m_w_down, ln2_g=m_ln2_g, ln2_b=m_ln2_b)
    V = dict(rel_bias=v_rel_bias, w_in=v_w_in, q_norm=v_q_norm, k_norm=v_k_norm, sink=v_sink, out_norm_a=v_out_norm_a,
             out_norm_b=v_out_norm_b, w_out=v_w_out, ln1_g=v_ln1_g, ln1_b=v_ln1_b, w_gate=v_w_gate, w_up=v_w_up,
             conv_w=v_conv_w, conv_b=v_conv_b, w_down=v_w_down, ln2_g=v_ln2_g, ln2_b=v_ln2_b)
    names = list(P)
    T = x.shape[1]
    me = 4 * lax.axis_index("x") + 2 * lax.axis_index("y") + lax.axis_index("c")

    L, taps, fc = conv_w.shape
    W = {n: ([None] * DEPTH if n in BIG else P[n]) for n in names}

    def wire(n, l):
        return P[n][l].astype(BF16)

    def take(n, l, gathered):
        W[n][l] = _unshard(n, gathered)

    take("w_in", 0, _exchange([wire("w_in", 0)], [True], name="gather_w_in0")[0])
    later = [(n, l) for l in range(DEPTH) for n in BIG if (n, l) != ("w_in", 0)]
    cw_shard = conv_w.reshape(-1)
    cw_wire = jnp.pad(cw_shard, (0, (-cw_shard.shape[0]) % LANES)).reshape(-1, LANES)
    gather_rest = _Comm([wire(n, l) for n, l in later] + [cw_wire], [True] * (len(later) + 1))

    def on_gathered(outs):
        for (n, l), g in zip(later, outs):
            take(n, l, g)
        cw_all = outs[-1].reshape(N_DEV, -1)[:, :cw_shard.shape[0]].reshape(N_DEV, L, taps, fc)
        W["conv_w"] = cw_all.transpose(1, 2, 0, 3).reshape(L, taps, N_DEV * fc)

    cos2, sin2 = _rope_tables(T)
    bucket = _window_buckets()
    bias = _bias_table(rel_bias.T, bucket.reshape(1, WB * WK))
    biasT = bias.reshape(KV, GQ, WB, WK).transpose(0, 3, 1, 2).reshape(KV, WK, GQ * WB)
    biasT = _end_tables(biasT)
    tabs = (cos2, sin2, biasT)

    ones, zeros = jnp.ones((1, D_MODEL), F32), jnp.zeros((1, D_MODEL), F32)
    cur = (x[0], ones, zeros, x[0].astype(BF16))
    saved = []
    for l in range(DEPTH):
        cur, S = _layer_fwd(l, cur, W, tabs, comm=gather_rest if l == 0 else None, on_comm=on_gathered,
                            target=loss_target[0] if l == DEPTH - 1 else None)
        saved.append(S)

    def owner_blocks(n, l):
        return _to_owner_blocks(n, grads[l][n], P[n].shape)

    early = ([(n, l) for l in range(1, DEPTH) for n in BIG] + [(n, 0) for n in BIG if n != "w_in"])

    def scatter_early(g0):
        grads[0] = g0
        return _Comm([owner_blocks(n, l) for n, l in early], [False] * len(early))

    grads = [None] * DEPTH
    dbs, dsks = [None] * DEPTH, [None] * DEPTH
    dz, dz16, stats = saved[-1]["head"]
    loss_part = stats[2, 0:1]
    recv = {}
    for l in reversed(range(DEPTH)):
        S = saved[l]
        dxin, grads[l], dbs[l], dsks[l], got = _layer_bwd(l, S, W, tabs, dz, dz16, stats,
                                                         scatter=scatter_early if l == 0 else None)
        if l == 0:
            recv.update(zip(early, got))
        if l > 0:
            Sp = saved[l - 1]
            dz, dz16, stats = _ln_bwd(Sp["x2hat"], Sp["rstd2"], Sp["g2"], dxin, name="ln2_bwd")
    grad_x = dxin[None]

    drb, dsink = _bias_sink_grads(dbs, dsks, bucket.T.reshape(1, WK * WB))
    small_g = {n: jnp.stack([grads[l][n] for l in range(DEPTH)]) for n in SMALL_NAMES if n not in ("rel_bias", "sink")}
    small_g["rel_bias"] = drb.T
    small_g["sink"] = dsink.reshape(DEPTH, KV * GQ)
    recv[("w_in", 0)], small_recv = _exchange([owner_blocks("w_in", 0), _pack_small(small_g, loss_part)], [False, True],
                                              name="scatter_w_in0_gather_small")

    out_g, out_d, out_m, out_v = {}, {}, {}, {}
    for n in BIG:
        shp = P[n].shape
        rows, cols = shp[0] * shp[1], shp[2]
        res = _adamw(P[n].reshape(rows, cols), M[n].reshape(rows, cols), V[n].reshape(rows, cols),
                     [recv[(n, l)] for l in range(DEPTH)], name="adamw_" + n)
        out_g[n], out_d[n], out_m[n], out_v[n] = (r.reshape(shp) for r in res)
    full_shapes = {n: W[n].shape for n in SMALL_NAMES}

    def small_state(D):
        vals = {n: D[n] for n in SMALL_NAMES if n != "conv_w"}
        cw = jnp.zeros((L, taps, N_DEV, fc), F32)
        cw = lax.dynamic_update_slice(cw, D["conv_w"].reshape(L, taps, 1, fc), (0, 0, me, 0))
        vals["conv_w"] = cw.reshape(L, taps, N_DEV * fc)
        return _pack_small(vals, jnp.zeros((1,), F32))

    sw, sm, sv = small_state(P), small_state(M), small_state(V)
    res = _adamw(sw, sm, sv, [small_recv], name="adamw_small")
    loss = _unpack_small(res[0], full_shapes)[1]
    for dst, packed in zip((out_g, out_d, out_m, out_v), res):
        vals, _ = _unpack_small(packed, full_shapes)
        for n in SMALL_NAMES:
            if n == "conv_w":
                sl = lax.dynamic_slice(vals[n].reshape(L, taps, N_DEV, fc), (0, 0, me, 0), (L, taps, 1, fc))
                dst[n] = sl.reshape(L, taps, fc)
            else:
                dst[n] = vals[n]
    return (loss, grad_x, *[out_g[n] for n in names], *[out_d[n] for n in names],
            *[out_m[n] for n in names], *[out_v[n] for n in names])
```

```python
import functools
import math

import jax
import jax.numpy as jnp
from jax import lax
from jax.experimental import pallas as pl
from jax.experimental.pallas import tpu as pltpu

F32 = jnp.float32
BF16 = jnp.bfloat16
MESH = pl.DeviceIdType.MESH

N_DEV = 8
D_MODEL = 1024
DEPTH = 2
HEAD_DIM = 64
KV = 2
GQ = 4
QW = KV * GQ * HEAD_DIM
KW = KV * HEAD_DIM
ROPE_W = QW + KW
IN_COLS = 2 * (QW + 2 * KW)
D_FF = 2816
GRID_W = 64
ROPE_THETA = 10000.0
WINDOW = 128
N_BUCKETS = 32
MAX_DISTANCE = 128
ALPHA = (2.0 * DEPTH) ** 0.25
RMS_EPS = 1e-6
LN_EPS = 1e-5
SCALE = HEAD_DIM ** -0.5
LOG2E = math.log2(math.e)
LN2 = math.log(2.0)
NEG = -1e30
ONES_ROWS = 16

ADAM_LR = 0.001
ADAM_B1 = 0.9
ADAM_B2 = 0.999
ADAM_EPS = 1e-08
ADAM_WD = 0.01
ADAM_STEP = 10

LANES = 128
MM_ROWS = 1024
MM_ROWS_LN = 256
MM_ROWS_WIDE = 512
DW_TOKENS = 1024
VMEM_CAP = 60 * 1024 * 1024
SMALL_NAMES = ("rel_bias", "q_norm", "k_norm", "sink", "out_norm_a", "out_norm_b", "ln1_g", "ln1_b",
               "conv_b", "ln2_g", "ln2_b", "conv_w")


def _params(sem, est_bytes):
    limit = int(min(VMEM_CAP, est_bytes + (8 << 20)))
    return pltpu.CompilerParams(dimension_semantics=sem, vmem_limit_bytes=limit)


def _nbytes(shape, dtype):
    return math.prod(shape) * jnp.dtype(dtype).itemsize


class _Comm:
    def __init__(self, parts, gathers):
        self.parts, self.gathers, self.n = list(parts), list(gathers), len(parts)
        hbm = pl.BlockSpec(memory_space=pltpu.HBM)
        self.in_specs = [hbm] * self.n
        self.out_specs = [hbm] * self.n
        self.out_shape = [jax.ShapeDtypeStruct((N_DEV,) + tuple(p.shape if g else p.shape[1:]), p.dtype)
                          for p, g in zip(self.parts, self.gathers)]
        self.scratch = [pltpu.SemaphoreType.DMA((self.n * (N_DEV - 1),)), pltpu.SemaphoreType.DMA((self.n * (N_DEV - 1),)),
                        pltpu.SemaphoreType.DMA((self.n,))]

    def bind(self, ins, outs, sems):
        send_sems, recv_sems, local_sems = sems
        gathers, n = self.gathers, self.n
        me = 4 * lax.axis_index("x") + 2 * lax.axis_index("y") + lax.axis_index("c")

        def src(k, j):
            return ins[k] if gathers[k] else ins[k].at[j]

        def copy(k, d, peer, lands_in):
            return pltpu.make_async_remote_copy(
                src_ref=src(k, peer), dst_ref=outs[k].at[lands_in],
                send_sem=send_sems.at[k * (N_DEV - 1) + d - 1], recv_sem=recv_sems.at[k * (N_DEV - 1) + d - 1],
                device_id=(peer // 4, lax.rem(peer // 2, 2), lax.rem(peer, 2)), device_id_type=MESH)

        def send(k, d):
            return copy(k, d, lax.rem(me + d, N_DEV), me)

        def arrival(k, d):
            frm = lax.rem(me + N_DEV - d, N_DEV)
            return copy(k, d, frm, frm)

        def local(k):
            return pltpu.make_async_copy(src(k, me), outs[k].at[me], local_sems.at[k])

        def start():
            for k in range(n):
                local(k).start()
                for d in range(1, N_DEV):
                    send(k, d).start()

        def finish():
            for k in range(n):
                for d in range(1, N_DEV):
                    arrival(k, d).wait_recv()
            for k in range(n):
                for d in range(1, N_DEV):
                    send(k, d).wait_send()
                local(k).wait()

        return start, finish


def _host_comm(comm, refs, n_in, n_out, n_scratch, grid):
    n = comm.n if comm is not None else 0
    own_in, cin = refs[:n_in], refs[n_in:n_in + n]
    own_out, cout = refs[n_in + n:n_in + n + n_out], refs[n_in + n + n_out:n_in + 2 * n + n_out]
    base = n_in + 2 * n + n_out
    own_scratch, sems = refs[base:base + n_scratch], refs[base + n_scratch:]
    own = tuple(own_in) + tuple(own_out) + tuple(own_scratch)
    if comm is None:
        return own, lambda: None, lambda: None
    start, finish = comm.bind(cin, cout, sems)
    first = last = None
    for ax, size in enumerate(grid):
        pid = pl.program_id(ax)
        first = (pid == 0) if first is None else first & (pid == 0)
        last = (pid == size - 1) if last is None else last & (pid == size - 1)
    return own, lambda: pl.when(first)(start), lambda: pl.when(last)(finish)


def _exchange(parts, gathers, name):
    comm = _Comm(parts, gathers)
    n = comm.n

    def body(*refs):
        start, finish = comm.bind(refs[:n], refs[n:2 * n], refs[2 * n:])
        start()
        finish()

    return pl.pallas_call(body, name=name, out_shape=comm.out_shape, in_specs=comm.in_specs, out_specs=comm.out_specs,
                          scratch_shapes=comm.scratch)(*comm.parts)


def _mm(a_list, b_list, *, name, out_dtype, tm, tn, tk, trans_a=False, trans_b=False, add=None, add_scale=1.0,
        ln=None):
    assert not (trans_a and trans_b)
    na = len(a_list)
    if trans_a:
        K, M = a_list[0].shape
    else:
        M, K = a_list[0].shape
    N = b_list[0].shape[0 if trans_b else 1]
    tm, tn, tk = min(tm, M), min(tn, N), min(tk, K)
    assert M % tm == 0 and N % tn == 0 and K % tk == 0, (name, M, N, K, tm, tn, tk)
    nk = K // tk
    dims = (((0,), (0,)), ((), ())) if trans_a else (((1,), (1 if trans_b else 0,)), ((), ()))

    n_in = 2 * na + (add is not None) + (3 if ln is not None else 0)

    def body(*refs):
        a_refs, b_refs = refs[:na], refs[na:2 * na]
        add_ref = refs[2 * na] if add is not None else None
        o_ref = refs[n_in]
        k = pl.program_id(2)

        part = None
        for a_ref, b_ref in zip(a_refs, b_refs):
            prod = lax.dot_general(a_ref[...].astype(BF16), b_ref[...].astype(BF16), dims,
                                   preferred_element_type=F32)
            part = prod if part is None else part + prod

        def finish(res):
            if add_ref is not None:
                res = res + add_scale * add_ref[...]
            if ln is None:
                o_ref[...] = res.astype(o_ref.dtype)
                return
            xh_ref, rstd_ref, g_ref = refs[n_in - 3:n_in]
            dzb_ref, st_ref = refs[n_in + 1:n_in + 3]

            @pl.when(pl.program_id(0) == 0)
            def _():
                st_ref[...] = jnp.zeros_like(st_ref)

            xh = xh_ref[...]
            st_ref[0:1, :] += jnp.sum(res * xh, axis=0, keepdims=True)
            st_ref[1:2, :] += jnp.sum(res, axis=0, keepdims=True)
            dxh = res * g_ref[...]
            dz = rstd_ref[...] * (dxh - jnp.mean(dxh, axis=1, keepdims=True)
                                  - xh * jnp.mean(dxh * xh, axis=1, keepdims=True))
            o_ref[...] = dz
            dzb_ref[...] = dz.astype(BF16)

        if nk == 1:
            finish(part)
        else:
            acc_ref = refs[-1]

            @pl.when(k == 0)
            def _():
                acc_ref[...] = part

            @pl.when(k > 0)
            def _():
                acc_ref[...] += part

            @pl.when(k == nk - 1)
            def _():
                finish(acc_ref[...])

    if trans_a:
        a_spec = pl.BlockSpec((tk, tm), lambda i, j, k: (k, i))
    else:
        a_spec = pl.BlockSpec((tm, tk), lambda i, j, k: (i, k))
    if trans_b:
        b_spec = pl.BlockSpec((tn, tk), lambda i, j, k: (j, k))
    else:
        b_spec = pl.BlockSpec((tk, tn), lambda i, j, k: (k, j))
    o_spec = pl.BlockSpec((tm, tn), lambda i, j, k: (i, j))
    in_specs = [a_spec] * na + [b_spec] * na + ([o_spec] if add is not None else [])
    if ln is not None:
        assert tn == N and nk == 1
        in_specs += [o_spec, pl.BlockSpec((tm, 1), lambda i, j, k: (i, 0)), pl.BlockSpec((1, tn), lambda i, j, k: (0, 0))]
    est = (2 * na * (_nbytes((tm, tk), a_list[0].dtype) + _nbytes((tk, tn), b_list[0].dtype))
           + na * (_nbytes((tm, tk), BF16) + _nbytes((tk, tn), BF16))
           + 2 * _nbytes((tm, tn), out_dtype) + 3 * _nbytes((tm, tn), F32)
           + (2 * _nbytes((tm, tn), F32) if add is not None else 0))
    args = list(a_list) + list(b_list) + ([add] if add is not None else [])
    if ln is not None:
        return pl.pallas_call(
            body, name=name, grid=(M // tm, 1, 1),
            out_shape=(jax.ShapeDtypeStruct((M, N), F32), jax.ShapeDtypeStruct((M, N), BF16),
                       jax.ShapeDtypeStruct((8, N), F32)),
            in_specs=in_specs, out_specs=(o_spec, o_spec, pl.BlockSpec((8, tn), lambda i, j, k: (0, 0))),
            compiler_params=_params(("arbitrary",) * 3, est + 6 * _nbytes((tm, tn), F32)),
        )(*args, *ln)
    return pl.pallas_call(
        body, name=name, grid=(M // tm, N // tn, nk),
        out_shape=jax.ShapeDtypeStruct((M, N), out_dtype),
        in_specs=in_specs, out_specs=o_spec,
        scratch_shapes=[pltpu.VMEM((tm, tn), F32)] if nk > 1 else [],
        compiler_params=_params(("parallel", "parallel", "arbitrary"), est),
    )(*args)


def _mm_res_ln(a, w, res_hat, res_g, res_b, ln_g, ln_b, *, name, tm, target=None):
    T, K = a.shape
    D = w.shape[1]
    tm = min(tm, T)
    head = target is not None

    def body(a_ref, w_ref, rh_ref, rg_ref, rb_ref, g_ref, b_ref, *rest):
        branch = jnp.dot(a_ref[...].astype(BF16), w_ref[...], preferred_element_type=F32)
        z = ALPHA * (rh_ref[...] * rg_ref[...] + rb_ref[...]) + branch
        mu = jnp.mean(z, axis=1, keepdims=True)
        zc = z - mu
        var = jnp.mean(zc * zc, axis=1, keepdims=True)
        rstd = lax.rsqrt(var + LN_EPS)
        xhat = zc * rstd
        if not head:
            xhat_ref, rstd_ref, xb_ref = rest
            xhat_ref[...] = xhat
            rstd_ref[...] = rstd
            xb_ref[...] = (xhat * g_ref[...] + b_ref[...]).astype(BF16)
            return
        t_ref, dz_ref, dzb_ref, st_ref = rest

        @pl.when(pl.program_id(0) == 0)
        def _():
            st_ref[...] = jnp.zeros_like(st_ref)

        g = g_ref[...]
        err = (xhat * g + b_ref[...]) - t_ref[...]
        dx = err * (1.0 / D)
        st_ref[2:3, :] += 0.5 * jnp.sum(jnp.sum(err * err, axis=1, keepdims=True) * (1.0 / D), axis=0, keepdims=True)
        st_ref[0:1, :] += jnp.sum(dx * xhat, axis=0, keepdims=True)
        st_ref[1:2, :] += jnp.sum(dx, axis=0, keepdims=True)
        dxh = dx * g
        dz = rstd * (dxh - jnp.mean(dxh, axis=1, keepdims=True) - xhat * jnp.mean(dxh * xhat, axis=1, keepdims=True))
        dz_ref[...] = dz
        dzb_ref[...] = dz.astype(BF16)

    row = pl.BlockSpec((tm, D), lambda i: (i, 0))
    vec = pl.BlockSpec((1, D), lambda i: (0, 0))
    est = (2 * (_nbytes((tm, K), a.dtype) + _nbytes((K, D), BF16)) + 4 * _nbytes((tm, D), F32) * 2
           + 8 * _nbytes((tm, D), F32))
    in_specs = [pl.BlockSpec((tm, K), lambda i: (i, 0)), pl.BlockSpec((K, D), lambda i: (0, 0)), row, vec, vec, vec, vec]
    if head:
        return pl.pallas_call(
            body, name=name, grid=(T // tm,),
            out_shape=(jax.ShapeDtypeStruct((T, D), F32), jax.ShapeDtypeStruct((T, D), BF16),
                       jax.ShapeDtypeStruct((8, D), F32)),
            in_specs=in_specs + [row], out_specs=(row, row, pl.BlockSpec((8, D), lambda i: (0, 0))),
            compiler_params=_params(("arbitrary",), est),
        )(a, w, res_hat, res_g, res_b, ln_g, ln_b, target)
    return pl.pallas_call(
        body, name=name, grid=(T // tm,),
        out_shape=(jax.ShapeDtypeStruct((T, D), F32), jax.ShapeDtypeStruct((T, 1), F32),
                   jax.ShapeDtypeStruct((T, D), BF16)),
        in_specs=in_specs, out_specs=(row, pl.BlockSpec((tm, 1), lambda i: (i, 0)), row),
        compiler_params=_params(("parallel",), est),
    )(a, w, res_hat, res_g, res_b, ln_g, ln_b)


def _pair_swap(v, even):
    return jnp.where(even, pltpu.roll(v, LANES - 1, 1), pltpu.roll(v, 1, 1))


def _half_sums(v, lo):
    s_lo = jnp.sum(jnp.where(lo, v, 0.0), axis=1, keepdims=True)
    s_hi = jnp.sum(jnp.where(lo, 0.0, v), axis=1, keepdims=True)
    return jnp.where(lo, s_lo, s_hi)


A_COLS = ROPE_W + KW
A_HEADS = A_COLS // HEAD_DIM
A_K0, A_V0 = KV * GQ, KV * GQ + KV


def _qk_rope_fwd(h, gains, cos2, sin2, *, name, tm=256):
    T = h.shape[0]
    tm = min(tm, T)
    nch = A_COLS // LANES

    def body(h_ref, g_ref, c_ref, s_ref, oT_ref, kv_ref):
        lane = lax.broadcasted_iota(jnp.int32, (tm, LANES), 1)
        lo, even = lane < HEAD_DIM, lane % 2 == 0
        c, s = c_ref[...], s_ref[...]
        for j in range(nch):
            x = h_ref[:, j * LANES:(j + 1) * LANES]
            isq, isv = j < QW // LANES, j == nch - 1
            if isv:
                out = x
            else:
                g = g_ref[0:1, :] if isq else g_ref[1:2, :]
                r = lax.rsqrt(_half_sums(x * x, lo) * (1.0 / HEAD_DIM) + RMS_EPS)
                nrm = x * r * g
                out = nrm * c + _pair_swap(nrm, even) * s
            if isq:
                out = out * (SCALE * LOG2E)
            else:
                kv_ref[:, (j - QW // LANES) * LANES:(j - QW // LANES + 1) * LANES] = out.astype(BF16)
            oT_ref[j * LANES:(j + 1) * LANES, :] = out.T.astype(BF16)

    est = 2 * (_nbytes((tm, A_COLS), F32) + 2 * _nbytes((tm, A_COLS), BF16) + 2 * _nbytes((tm, LANES), F32)) + (4 << 20)
    return pl.pallas_call(
        body, name=name, grid=(T // tm,),
        out_shape=(jax.ShapeDtypeStruct((A_COLS, T), BF16), jax.ShapeDtypeStruct((T, 2 * KW), BF16)),
        in_specs=[pl.BlockSpec((tm, A_COLS), lambda i: (i, 0)), pl.BlockSpec((8, LANES), lambda i: (0, 0)),
                  pl.BlockSpec((tm, LANES), lambda i: (i, 0)), pl.BlockSpec((tm, LANES), lambda i: (i, 0))],
        out_specs=(pl.BlockSpec((A_COLS, tm), lambda i: (0, i)), pl.BlockSpec((tm, 2 * KW), lambda i: (i, 0))),
        compiler_params=_params(("parallel",), est),
    )(h, gains, cos2, sin2)


def _qk_rope_bwd(h, dqT, dkT, gains, cos2, sin2, *, name, tm=256):
    T = h.shape[0]
    tm = min(tm, T)
    nch = ROPE_W // LANES

    def body(h_ref, dq_ref, dk_ref, g_ref, c_ref, s_ref, dh_ref, dg_ref):
        i = pl.program_id(0)

        @pl.when(i == 0)
        def _():
            dg_ref[...] = jnp.zeros_like(dg_ref)

        lane = lax.broadcasted_iota(jnp.int32, (tm, LANES), 1)
        lo, even = lane < HEAD_DIM, lane % 2 == 0
        c, s = c_ref[...], s_ref[...]
        acc = [None, None]
        for j in range(nch):
            x = h_ref[:, j * LANES:(j + 1) * LANES]
            isq = j < QW // LANES
            g = g_ref[0:1, :] if isq else g_ref[1:2, :]
            d = dq_ref[j * LANES:(j + 1) * LANES, :].T * SCALE if isq else dk_ref[...].T
            r = lax.rsqrt(_half_sums(x * x, lo) * (1.0 / HEAD_DIM) + RMS_EPS)
            dn = d * c + _pair_swap(d * s, even)
            xr = x * r
            part = jnp.sum(dn * xr, axis=0, keepdims=True)
            acc[0 if isq else 1] = part if acc[0 if isq else 1] is None else acc[0 if isq else 1] + part
            dng = dn * g
            dx = r * dng - xr * (r * r) * (_half_sums(dng * x, lo) * (1.0 / HEAD_DIM))
            dh_ref[:, j * LANES:(j + 1) * LANES] = dx.astype(BF16)
        for row in range(2):
            folded = acc[row] + pltpu.roll(acc[row], HEAD_DIM, 1)
            dg_ref[row:row + 1, :] += folded

    est = 2 * (2 * _nbytes((tm, ROPE_W), F32) + _nbytes((tm, ROPE_W), BF16) + 2 * _nbytes((tm, LANES), F32)) + (4 << 20)
    return pl.pallas_call(
        body, name=name, grid=(T // tm,),
        out_shape=(jax.ShapeDtypeStruct((T, ROPE_W), BF16), jax.ShapeDtypeStruct((8, LANES), F32)),
        in_specs=[pl.BlockSpec((tm, ROPE_W), lambda i: (i, 0)), pl.BlockSpec((QW, tm), lambda i: (0, i)),
                  pl.BlockSpec((KW, tm), lambda i: (0, i)), pl.BlockSpec((8, LANES), lambda i: (0, 0)),
                  pl.BlockSpec((tm, LANES), lambda i: (i, 0)), pl.BlockSpec((tm, LANES), lambda i: (i, 0))],
        out_specs=(pl.BlockSpec((tm, ROPE_W), lambda i: (i, 0)), pl.BlockSpec((8, LANES), lambda i: (0, 0))),
        compiler_params=_params(("arbitrary",), est),
    )(h, dqT, dkT, gains, cos2, sin2)


def _attn_a_fwd(k, hT, *, comm=None, tq=4096, tk=2048, cq=512):
    G, T, HD = k.shape
    HE = HD + ONES_ROWS
    tq, tk = min(tq, T), min(tk, T)
    cq = min(cq, tq)
    nk, nt = T // tk, T // tq
    grid = (G, GQ * nt, nk)

    def body(*refs):
        (k_ref, qT_ref, v_ref, oT_ref, lse_ref, m_sc, acc_sc), comm_start, comm_finish = _host_comm(
            comm, refs, 3, 2, 2, grid)
        kv = pl.program_id(2)
        comm_start()
        v1T = jnp.concatenate([v_ref[...], jnp.ones((ONES_ROWS, tk), BF16)], axis=0)

        @pl.when(kv == 0)
        def _():
            m_sc[...] = jnp.full_like(m_sc, NEG)
            acc_sc[...] = jnp.zeros_like(acc_sc)

        def scores(c):
            return jnp.dot(k_ref[...], qT_ref[:, c * cq:(c + 1) * cq], preferred_element_type=F32)

        nc = tq // cq
        ahead = scores(0)
        for c in range(nc):
            cols = slice(c * cq, (c + 1) * cq)
            sT = ahead
            if c + 1 < nc:
                ahead = scores(c + 1)
            m_prev = m_sc[:, cols]
            m_new = jnp.maximum(m_prev, jnp.max(sT, axis=0, keepdims=True))
            pT = jnp.exp2(sT - m_new).astype(BF16)
            acc_sc[:, cols] = (jnp.exp2(m_prev - m_new) * acc_sc[:, cols]
                               + jnp.dot(v1T, pT, preferred_element_type=F32))
            m_sc[:, cols] = m_new

        @pl.when(kv == nk - 1)
        def _():
            l = acc_sc[HD:HD + 1, :]
            oT_ref[...] = acc_sc[0:HD, :] / l
            lse_ref[...] = m_sc[...] + jnp.log2(l)

        comm_finish()

    qtr = pl.BlockSpec((None, HD, tq), lambda g, i, j: (g * GQ + i // nt, 0, i % nt))
    qvec = pl.BlockSpec((None, 1, tq), lambda g, i, j: (g * GQ + i // nt, 0, i % nt))
    est = 6 * _nbytes((cq, tk), F32) + (8 << 20)
    hosted = comm is not None
    return pl.pallas_call(
        body, name="attn_a_fwd_comm" if hosted else "attn_a_fwd", grid=grid,
        out_shape=[jax.ShapeDtypeStruct((G * GQ, HD, T), F32), jax.ShapeDtypeStruct((G * GQ, 1, T), F32)]
        + (comm.out_shape if hosted else []),
        in_specs=[pl.BlockSpec((None, tk, HD), lambda g, i, j: (g, j, 0)), qtr,
                  pl.BlockSpec((None, HD, tk), lambda g, i, j: (A_V0 + g, 0, j))] + (comm.in_specs if hosted else []),
        out_specs=[qtr, qvec] + (comm.out_specs if hosted else []),
        scratch_shapes=[pltpu.VMEM((1, tq), F32), pltpu.VMEM((HE, tq), F32)] + (comm.scratch if hosted else []),
        compiler_params=_params(("arbitrary",) * 3 if hosted else ("parallel", "parallel", "arbitrary"), est),
    )(k, hT, hT, *(comm.parts if hosted else []))


def _attn_a_bwd(k, v, hT, doT, lse_row, delta_row, *, comm=None, tq=4096, tk=1024, cq=256):
    G, T, HD = k.shape
    tq, tk = min(tq, T), min(tk, T)
    cq = min(cq, tq)
    nqt = T // tq
    nq, nc = GQ * nqt, tq // cq
    nt = (((1,), (1,)), ((), ()))

    grid = (G, T // tk, nq)

    def body(*refs):
        (k_ref, v_ref, kT_ref, qT_ref, doT_ref, lse_ref, dl_ref, dkT_ref, dvT_ref, dqT_ref, dk_sc, dv_sc), \
            comm_start, comm_finish = _host_comm(comm, refs, 7, 3, 2, grid)
        j, i = pl.program_id(1), pl.program_id(2)
        comm_start()

        @pl.when((j == 0) & (i == 0))
        def _():
            dqT_ref[...] = jnp.zeros_like(dqT_ref)

        @pl.when(i == 0)
        def _():
            dk_sc[...] = jnp.zeros_like(dk_sc)
            dv_sc[...] = jnp.zeros_like(dv_sc)

        def scores(c):
            cols = slice(c * cq, (c + 1) * cq)
            return (jnp.dot(k_ref[...], qT_ref[:, cols], preferred_element_type=F32),
                    jnp.dot(v_ref[...], doT_ref[:, cols], preferred_element_type=F32))

        ahead = scores(0)
        dk_part = dv_part = None
        for c in range(nc):
            cols = slice(c * cq, (c + 1) * cq)
            sT, dpT = ahead
            if c + 1 < nc:
                ahead = scores(c + 1)
            pT = jnp.exp2(sT - lse_ref[:, cols])
            dsT = (pT * (dpT - dl_ref[:, cols])).astype(BF16)
            dv_c = lax.dot_general(doT_ref[:, cols], pT.astype(BF16), nt, preferred_element_type=F32)
            dk_c = lax.dot_general(qT_ref[:, cols], dsT, nt, preferred_element_type=F32)
            dv_part = dv_c if dv_part is None else dv_part + dv_c
            dk_part = dk_c if dk_part is None else dk_part + dk_c
            out_cols = pl.ds(pl.multiple_of((i % nqt) * tq + c * cq, cq), cq)
            dqT_ref[i // nqt, :, out_cols] += jnp.dot(kT_ref[...], dsT, preferred_element_type=F32)
        dk_sc[...] += dk_part
        dv_sc[...] += dv_part

        @pl.when(i == nq - 1)
        def _():
            dkT_ref[...] = dk_sc[...] * LN2
            dvT_ref[...] = dv_sc[...]

        comm_finish()

    krow = pl.BlockSpec((None, tk, HD), lambda g, j, i: (g, j, 0))
    ktr = pl.BlockSpec((None, HD, tk), lambda g, j, i: (g, 0, j))
    ktr_h = pl.BlockSpec((None, HD, tk), lambda g, j, i: (A_K0 + g, 0, j))
    qtr = pl.BlockSpec((None, HD, tq), lambda g, j, i: (g * GQ + i // nqt, 0, i % nqt))
    qvec = pl.BlockSpec((None, 1, tq), lambda g, j, i: (g * GQ + i // nqt, 0, i % nqt))
    whole = pl.BlockSpec((GQ, HD, T), lambda g, j, i: (g, 0, 0))
    est = 8 * _nbytes((cq, tk), F32) + 2 * _nbytes((GQ, HD, T), F32) + (8 << 20)
    hosted = comm is not None
    return pl.pallas_call(
        body, name="attn_a_bwd_comm" if hosted else "attn_a_bwd", grid=grid,
        out_shape=[jax.ShapeDtypeStruct((G, HD, T), F32), jax.ShapeDtypeStruct((G, HD, T), F32),
                   jax.ShapeDtypeStruct((G * GQ, HD, T), F32)] + (comm.out_shape if hosted else []),
        in_specs=[krow, krow, ktr_h, qtr, qtr, qvec, qvec] + (comm.in_specs if hosted else []),
        out_specs=[ktr, ktr, whole] + (comm.out_specs if hosted else []),
        scratch_shapes=[pltpu.VMEM((HD, tk), F32), pltpu.VMEM((HD, tk), F32)] + (comm.scratch if hosted else []),
        compiler_params=_params(("arbitrary", "arbitrary", "arbitrary"), est),
    )(k, v, hT, hT, doT, lse_row, delta_row, *(comm.parts if hosted else []))


WB = WINDOW
WK = 3 * WINDOW


QB_COL0 = (ROPE_W + KW) // (GQ * HEAD_DIM)
KB_COL = (ROPE_W + KW + QW) // KW
GW = GQ * HEAD_DIM


WSTEP = 2


def _win_in_specs(T):
    nb = T // WB
    assert nb % WSTEP == 0
    ns = nb // WSTEP
    q = [pl.BlockSpec((WSTEP * WB, GW), functools.partial(lambda n, g: (n, QB_COL0 + g), g=g)) for g in range(KV)]
    kv = [pl.BlockSpec((WB, KW), functools.partial(lambda n, o, c: (jnp.clip(WSTEP * n + o, 0, nb - 1), c), o=o, c=c))
          for c in (KB_COL, KB_COL + 1) for o in range(-1, WSTEP + 1)]
    bias = [pl.BlockSpec((None, KV, WK, GQ * WB), lambda n: (jnp.where(n == 0, 0, 1), 0, 0, 0)),
            pl.BlockSpec((None, KV, WK, GQ * WB), lambda n: (jnp.where(n == ns - 1, 2, 1), 0, 0, 0))]
    return ns, q + kv, bias


def _end_tables(biasT):
    key = lax.broadcasted_iota(jnp.int32, biasT.shape, 1)
    return jnp.stack([jnp.where(key < WB, NEG, biasT), biasT, jnp.where(key >= 2 * WB, NEG, biasT)])


def _heads_to_lanes(t):
    return jnp.concatenate([t[i * HEAD_DIM:(i + 1) * HEAD_DIM] for i in range(GQ)], axis=1)


def _lanes_to_heads(t):
    return jnp.concatenate([t[:, i * WB:(i + 1) * WB] for i in range(GQ)], axis=0)


def _attn_b_fwd(h, biasT, sink_rows):
    T = h.shape[0]
    ns, in_specs, bias_specs = _win_in_specs(T)
    nkv = WSTEP + 2

    def body(*refs):
        q_refs, k_refs, v_refs = refs[:KV], refs[KV:KV + nkv], refs[KV + nkv:KV + 2 * nkv]
        b_refs, sk_ref, o_ref, lse_ref = refs[KV + 2 * nkv:KV + 2 * nkv + WSTEP], *refs[KV + 2 * nkv + WSTEP:]
        ks, vs = [r[...] for r in k_refs], [r[...] for r in v_refs]
        outs = []
        for b, b_ref in enumerate(b_refs):
            rows = slice(b * WB, (b + 1) * WB)
            kwin = jnp.concatenate(ks[b:b + 3], axis=0)
            vT = jnp.concatenate(vs[b:b + 3], axis=0).T
            qT = [_heads_to_lanes((q[rows, :] * SCALE).T).astype(BF16) for q in q_refs]
            sT = [jnp.dot(kwin[:, g * HEAD_DIM:(g + 1) * HEAD_DIM].astype(BF16), qT[g], preferred_element_type=F32)
                  for g in range(KV)]
            oT = []
            for g in range(KV):
                s = sT[g] + b_ref[g]
                sk = sk_ref[g]
                m = jnp.maximum(jnp.max(s, axis=0, keepdims=True), sk)
                p = jnp.exp(s - m)
                den = jnp.sum(p, axis=0, keepdims=True) + jnp.exp(sk - m)
                o = jnp.dot(vT[g * HEAD_DIM:(g + 1) * HEAD_DIM].astype(BF16), p.astype(BF16),
                            preferred_element_type=F32) / den
                lse_ref[b, g] = m + jnp.log(den)
                oT.append(_lanes_to_heads(o))
            outs.append(jnp.concatenate(oT, axis=0).T)
        o_ref[...] = jnp.concatenate(outs, axis=0)

    whole = lambda *shape: pl.BlockSpec(shape, lambda n: (0,) * len(shape))
    return pl.pallas_call(
        body, name="attn_b_fwd", grid=(ns,),
        out_shape=(jax.ShapeDtypeStruct((T, QW), F32), jax.ShapeDtypeStruct((ns * WSTEP, KV, 1, GQ * WB), F32)),
        in_specs=in_specs + bias_specs + [whole(KV, 1, GQ * WB)],
        out_specs=(pl.BlockSpec((WSTEP * WB, QW), lambda n: (n, 0)),
                   pl.BlockSpec((WSTEP, KV, 1, GQ * WB), lambda n: (n, 0, 0, 0))),
        compiler_params=_params(("parallel",), 32 << 20),
    )(*([h] * (KV + 2 * nkv)), *([biasT] * WSTEP), sink_rows)


def _attn_b_bwd(h, do, o, lse, biasT, sink_rows):
    T = h.shape[0]
    ns, in_specs, bias_specs = _win_in_specs(T)
    nkv = WSTEP + 2
    Tp = T + 2 * WB
    nt = (((1,), (1,)), ((), ()))

    def body(*refs):
        q_refs, k_refs, v_refs = refs[:KV], refs[KV:KV + nkv], refs[KV + nkv:KV + 2 * nkv]
        at = KV + 2 * nkv
        do_ref, o_ref, lse_ref = refs[at:at + 3]
        b_refs, sk_ref = refs[at + 3:at + 3 + WSTEP], refs[at + 3 + WSTEP]
        dq_ref, dk_ref, dv_ref, db_ref, dsk_ref = refs[at + 4 + WSTEP:]
        n = pl.program_id(0)

        @pl.when(n == 0)
        def _():
            dk_ref[...] = jnp.zeros_like(dk_ref)
            dv_ref[...] = jnp.zeros_like(dv_ref)
            db_ref[...] = jnp.zeros_like(db_ref)
            dsk_ref[...] = jnp.zeros_like(dsk_ref)

        ks, vs = [r[...] for r in k_refs], [r[...] for r in v_refs]
        for b, b_ref in enumerate(b_refs):
            rows = slice(b * WB, (b + 1) * WB)
            kwin = jnp.concatenate(ks[b:b + 3], axis=0)
            vwin = jnp.concatenate(vs[b:b + 3], axis=0)
            kT = kwin.T
            doT_all, oT_all = do_ref[rows, :].T, o_ref[rows, :].T
            qT, doT, delta, sT, dpT = [], [], [], [], []
            for g, q in enumerate(q_refs):
                hd = slice(g * HEAD_DIM, (g + 1) * HEAD_DIM)
                qT.append(_heads_to_lanes((q[rows, :] * SCALE).T).astype(BF16))
                d = _heads_to_lanes(doT_all[g * GW:(g + 1) * GW])
                delta.append(jnp.sum(d * _heads_to_lanes(oT_all[g * GW:(g + 1) * GW]), axis=0, keepdims=True))
                doT.append(d.astype(BF16))
                sT.append(jnp.dot(kwin[:, hd].astype(BF16), qT[g], preferred_element_type=F32))
                dpT.append(jnp.dot(vwin[:, hd].astype(BF16), doT[g], preferred_element_type=F32))
            dq, dk, dv = [], [], []
            for g in range(KV):
                lse_g = lse_ref[b, g]
                p = jnp.exp(sT[g] + b_ref[g] - lse_g)
                ds = p * (dpT[g] - delta[g])
                db_ref[g] += ds
                dsk_ref[g] -= jnp.exp(sk_ref[g] - lse_g) * delta[g]
                dsb = ds.astype(BF16)
                dqT = jnp.dot(kT[g * HEAD_DIM:(g + 1) * HEAD_DIM].astype(BF16), dsb, preferred_element_type=F32)
                dq.append(_lanes_to_heads(dqT))
                dk.append(lax.dot_general(dsb, qT[g], nt, preferred_element_type=F32))
                dv.append(lax.dot_general(p.astype(BF16), doT[g], nt, preferred_element_type=F32))
            dq_ref[rows, :] = (jnp.concatenate(dq, axis=0).T * SCALE).astype(BF16)
            win = pl.ds(pl.multiple_of((WSTEP * n + b) * WB, WB), WK)
            dk_ref[win, :] += jnp.concatenate(dk, axis=1)
            dv_ref[win, :] += jnp.concatenate(dv, axis=1)

    whole = lambda *shape: pl.BlockSpec(shape, lambda n: (0,) * len(shape))
    tok = pl.BlockSpec((WSTEP * WB, QW), lambda n: (n, 0))
    return pl.pallas_call(
        body, name="attn_b_bwd", grid=(ns,),
        out_shape=(jax.ShapeDtypeStruct((T, QW), BF16),
                   jax.ShapeDtypeStruct((Tp, KW), F32), jax.ShapeDtypeStruct((Tp, KW), F32),
                   jax.ShapeDtypeStruct((KV, WK, GQ * WB), F32), jax.ShapeDtypeStruct((KV, 1, GQ * WB), F32)),
        in_specs=in_specs + [tok, tok, pl.BlockSpec((WSTEP, KV, 1, GQ * WB), lambda n: (n, 0, 0, 0))]
        + bias_specs + [whole(KV, 1, GQ * WB)],
        out_specs=(tok, whole(Tp, KW), whole(Tp, KW), whole(KV, WK, GQ * WB), whole(KV, 1, GQ * WB)),
        compiler_params=_params(("arbitrary",), 48 << 20),
    )(*([h] * (KV + 2 * nkv)), do, o, lse, *([biasT] * WSTEP), sink_rows)


def _bias_table(rel_bias_t, bucket):
    nh, n = rel_bias_t.shape[0], bucket.shape[1]

    def body(rb_ref, bk_ref, o_ref):
        bk = bk_ref[...]
        out = jnp.full((nh, n), NEG, F32)
        for b in range(N_BUCKETS):
            out = jnp.where(bk == b, rb_ref[:, b:b + 1], out)
        o_ref[...] = out

    return pl.pallas_call(
        body, name="bias_table", out_shape=jax.ShapeDtypeStruct((nh, n), F32),
        compiler_params=pltpu.CompilerParams(vmem_limit_bytes=32 << 20),
    )(rel_bias_t, bucket)


def _bias_sink_grads(db_list, dsk_list, bucket):
    L = len(db_list)

    def body(*refs):
        db_refs, dsk_refs, bk_ref = refs[:L], refs[L:2 * L], refs[2 * L]
        drb_ref, dsink_ref = refs[2 * L + 1], refs[2 * L + 2]
        tot = db_refs[0][...]
        for r in db_refs[1:]:
            tot = tot + r[...]
        bk = bk_ref[...]
        lane = lax.broadcasted_iota(jnp.int32, (2 * GQ, N_BUCKETS), 1)
        out = jnp.zeros((2 * GQ, N_BUCKETS), F32)
        for b in range(N_BUCKETS):
            sb = jnp.sum(jnp.where(bk == b, tot, 0.0), axis=1, keepdims=True)
            out = jnp.where(lane == b, sb, out)
        drb_ref[...] = out
        for l in range(L):
            dsink_ref[l] = jnp.sum(dsk_refs[l][...], axis=1, keepdims=True)

    return pl.pallas_call(
        body, name="bias_sink_grads",
        out_shape=(jax.ShapeDtypeStruct((2 * GQ, N_BUCKETS), F32), jax.ShapeDtypeStruct((L, 2 * GQ, 1), F32)),
        compiler_params=pltpu.CompilerParams(vmem_limit_bytes=32 << 20),
    )(*db_list, *dsk_list, bucket)


def _outnorm_fwd(oaT, ob, ga, gb, *, tm=512):
    T = ob.shape[0]
    tm = min(tm, T)

    def body(oaT_ref, ob_ref, ga_ref, gb_ref, y_ref):
        for j, (o, g_ref) in enumerate(((oaT_ref[...].T, ga_ref), (ob_ref[...], gb_ref))):
            r = lax.rsqrt(jnp.mean(o * o, axis=1, keepdims=True) + RMS_EPS)
            y_ref[:, j * QW:(j + 1) * QW] = (o * r * g_ref[...]).astype(BF16)

    half = pl.BlockSpec((tm, QW), lambda i: (i, 0))
    halfT = pl.BlockSpec((QW, tm), lambda i: (0, i))
    vec = pl.BlockSpec((1, QW), lambda i: (0, 0))
    return pl.pallas_call(
        body, name="outnorm_fwd", grid=(T // tm,),
        out_shape=jax.ShapeDtypeStruct((T, 2 * QW), BF16),
        in_specs=[halfT, half, vec, vec], out_specs=pl.BlockSpec((tm, 2 * QW), lambda i: (i, 0)),
        compiler_params=_params(("parallel",), 16 << 20),
    )(oaT, ob, ga, gb)


def _outnorm_bwd(dy, oaT, ob, ga, gb, *, tm=512):
    T = ob.shape[0]
    tm = min(tm, T)
    nh = QW // HEAD_DIM

    def body(dy_ref, oaT_ref, ob_ref, ga_ref, gb_ref, doaT_ref, dl_ref, dob_ref, dg_ref):
        i = pl.program_id(0)

        @pl.when(i == 0)
        def _():
            dg_ref[...] = jnp.zeros_like(dg_ref)

        oaT = oaT_ref[...]
        for j, (o, g_ref) in enumerate(((oaT.T, ga_ref), (ob_ref[...], gb_ref))):
            d = dy_ref[:, j * QW:(j + 1) * QW]
            r = lax.rsqrt(jnp.mean(o * o, axis=1, keepdims=True) + RMS_EPS)
            orr = o * r
            dg_ref[j:j + 1, :] += jnp.sum(d * orr, axis=0, keepdims=True)
            dgv = d * g_ref[...]
            do = r * dgv - orr * (r * r) * jnp.mean(dgv * o, axis=1, keepdims=True)
            if j == 0:
                doT = do.T
                doaT_ref[...] = doT.astype(BF16)
                prod = doT * oaT
                dl_ref[...] = jnp.concatenate(
                    [jnp.sum(prod[a * HEAD_DIM:(a + 1) * HEAD_DIM], axis=0, keepdims=True) for a in range(nh)], axis=0)
            else:
                dob_ref[...] = do

    half = pl.BlockSpec((tm, QW), lambda i: (i, 0))
    halfT = pl.BlockSpec((QW, tm), lambda i: (0, i))
    vec = pl.BlockSpec((1, QW), lambda i: (0, 0))
    return pl.pallas_call(
        body, name="outnorm_bwd", grid=(T // tm,),
        out_shape=(jax.ShapeDtypeStruct((QW, T), BF16), jax.ShapeDtypeStruct((nh, T), F32),
                   jax.ShapeDtypeStruct((T, QW), F32), jax.ShapeDtypeStruct((8, QW), F32)),
        in_specs=[pl.BlockSpec((tm, 2 * QW), lambda i: (i, 0)), halfT, half, vec, vec],
        out_specs=(halfT, pl.BlockSpec((nh, tm), lambda i: (0, i)), half, pl.BlockSpec((8, QW), lambda i: (0, 0))),
        compiler_params=_params(("arbitrary",), 32 << 20),
    )(dy, oaT, ob, ga, gb)


GELU_C = math.sqrt(2.0 / math.pi)
GELU_A = 0.044715
HALO = 16
SUB = 8


def _gelu_parts(x):
    x2 = x * x
    t = jnp.tanh(x * (GELU_C + (GELU_C * GELU_A) * x2))
    return 0.5 * (1.0 + t), t, x2


def _halo_specs(tm, tn, T):
    nh = tm // HALO
    last = T // HALO - 1
    cur = pl.BlockSpec((tm, tn), lambda j, i: (i, j))
    prev = pl.BlockSpec((HALO, tn), lambda j, i: (jnp.maximum(i * nh - 1, 0), j))
    nxt = pl.BlockSpec((HALO, tn), lambda j, i: (jnp.minimum((i + 1) * nh, last), j))
    return cur, prev, nxt


def _conv_glu_fwd(g, u, conv_w, conv_b, *, tm=256, tn=1408):
    T, F = g.shape
    tm, tn = min(tm, T), min(tn, F)
    cur, prev, nxt = _halo_specs(tm, tn, T)

    def body(g_ref, gp_ref, gn_ref, u_ref, w_ref, b_ref, a_ref):
        i = pl.program_id(1)
        gv = g_ref[...]
        before = jnp.where(i * tm > 0, gp_ref[HALO - SUB:, :], 0.0)
        after = jnp.where((i + 1) * tm < T, gn_ref[0:SUB, :], 0.0)
        gm1 = pltpu.roll(jnp.concatenate([before, gv], axis=0), 1, 0)[SUB:]
        gp1 = pltpu.roll(jnp.concatenate([gv, after], axis=0), tm + SUB - 1, 0)[:tm]
        gc = ((b_ref[...] + gm1 * w_ref[0:1, :]) + gv * w_ref[1:2, :]) + gp1 * w_ref[2:3, :]
        cdf, _, _ = _gelu_parts(gc)
        a_ref[...] = (gc * cdf * u_ref[...].astype(F32)).astype(BF16)

    wspec = pl.BlockSpec((8, tn), lambda j, i: (0, j))
    est = 2 * (3 * _nbytes((tm, tn), F32)) + 8 * _nbytes((tm, tn), F32)
    return pl.pallas_call(
        body, name="conv_glu_fwd", grid=(F // tn, T // tm),
        out_shape=jax.ShapeDtypeStruct((T, F), BF16),
        in_specs=[cur, prev, nxt, cur, wspec, pl.BlockSpec((1, tn), lambda j, i: (0, j))],
        out_specs=cur,
        compiler_params=_params(("parallel", "parallel"), est),
    )(g, g, g, u, conv_w, conv_b)


def _conv_glu_bwd(dact, g, u, conv_w, conv_b, *, tm=256, tn=1408):
    T, F = g.shape
    tm, tn = min(tm, T), min(tn, F)
    cur, prev, nxt = _halo_specs(tm, tn, T)
    te = tm + 2 * HALO

    def body(d_ref, dp_ref, dn_ref, g_ref, gp_ref, gn_ref, u_ref, up_ref, un_ref, w_ref, b_ref,
             dg_ref, du_ref, dc_ref):
        i = pl.program_id(1)

        @pl.when(i == 0)
        def _():
            dc_ref[...] = jnp.zeros_like(dc_ref)

        has_prev, has_next = i * tm > 0, (i + 1) * tm < T
        ge = jnp.concatenate([jnp.where(has_prev, gp_ref[...], 0.0), g_ref[...],
                              jnp.where(has_next, gn_ref[...], 0.0)], axis=0)
        ue = jnp.concatenate([up_ref[...], u_ref[...], un_ref[...]], axis=0).astype(F32)
        de = jnp.concatenate([jnp.where(has_prev, dp_ref[...].astype(F32), 0.0), d_ref[...].astype(F32),
                              jnp.where(has_next, dn_ref[...].astype(F32), 0.0)], axis=0)
        w0, w1, w2 = w_ref[0:1, :], w_ref[1:2, :], w_ref[2:3, :]
        gm1 = pltpu.roll(ge, 1, 0)
        gp1 = pltpu.roll(ge, te - 1, 0)
        gc = ((b_ref[...] + gm1 * w0) + ge * w1) + gp1 * w2
        cdf, t, gc2 = _gelu_parts(gc)
        dgelu = cdf + (0.5 * gc) * (1.0 - t * t) * (GELU_C + (3.0 * GELU_C * GELU_A) * gc2)
        dgc = de * ue * dgelu
        dge = w0 * pltpu.roll(dgc, te - 1, 0) + w1 * dgc + w2 * pltpu.roll(dgc, 1, 0)
        mid = slice(HALO, HALO + tm)
        dg_ref[...] = dge[mid].astype(BF16)
        du_ref[...] = (de[mid] * (gc[mid] * cdf[mid])).astype(BF16)
        dgm = dgc[mid]
        dc_ref[0:1, :] += jnp.sum(dgm * gm1[mid], axis=0, keepdims=True)
        dc_ref[1:2, :] += jnp.sum(dgm * ge[mid], axis=0, keepdims=True)
        dc_ref[2:3, :] += jnp.sum(dgm * gp1[mid], axis=0, keepdims=True)
        dc_ref[3:4, :] += jnp.sum(dgm, axis=0, keepdims=True)

    wspec = pl.BlockSpec((8, tn), lambda j, i: (0, j))
    est = 2 * (3 * _nbytes((tm, tn), F32) + 2 * _nbytes((tm, tn), BF16)) + 16 * _nbytes((te, tn), F32)
    return pl.pallas_call(
        body, name="conv_glu_bwd", grid=(F // tn, T // tm),
        out_shape=(jax.ShapeDtypeStruct((T, F), BF16), jax.ShapeDtypeStruct((T, F), BF16),
                   jax.ShapeDtypeStruct((8, F), F32)),
        in_specs=[cur, prev, nxt, cur, prev, nxt, cur, prev, nxt, wspec, pl.BlockSpec((1, tn), lambda j, i: (0, j))],
        out_specs=(cur, cur, wspec),
        compiler_params=_params(("parallel", "arbitrary"), est),
    )(dact, dact, dact, g, g, g, u, u, u, conv_w, conv_b)


def _adamw_math(w, g, m, v):
    m = ADAM_B1 * m + (1.0 - ADAM_B1) * g
    v = ADAM_B2 * v + (1.0 - ADAM_B2) * (g * g)
    m_hat = m / (1.0 - ADAM_B1 ** ADAM_STEP)
    v_hat = v / (1.0 - ADAM_B2 ** ADAM_STEP)
    delta = -ADAM_LR * (m_hat / (jnp.sqrt(v_hat) + ADAM_EPS) + ADAM_WD * w)
    return delta, m, v


def _adamw(w, m, v, gparts, *, name):
    R, C = w.shape
    r = gparts[0].shape[1]
    assert len(gparts) * r == R
    tr = max(d for d in range(8, min(r, 256) + 1, 8) if r % d == 0) if r % 8 == 0 else r
    per = r // tr

    def body(w_ref, m_ref, v_ref, *rest):
        gp_refs, (g_ref, d_ref, nm_ref, nv_ref) = rest[:len(gparts)], rest[len(gparts):]
        i = pl.program_id(0)
        for l, gp_ref in enumerate(gp_refs):
            @pl.when(i // per == l)
            def _(gp_ref=gp_ref):
                g = gp_ref[0].astype(F32)
                for j in range(1, N_DEV):
                    g = g + gp_ref[j].astype(F32)
                delta, nm, nv = _adamw_math(w_ref[...], g, m_ref[...], v_ref[...])
                g_ref[...] = g
                d_ref[...] = delta
                nm_ref[...] = nm
                nv_ref[...] = nv

    blk = pl.BlockSpec((tr, C), lambda i: (i, 0))
    out = jax.ShapeDtypeStruct((R, C), F32)
    return pl.pallas_call(
        body, name=name, grid=(R // tr,), out_shape=(out, out, out, out),
        in_specs=[blk, blk, blk] + [pl.BlockSpec((N_DEV, tr, C), lambda i: (0, i % per, 0))] * len(gparts),
        out_specs=(blk, blk, blk, blk),
        compiler_params=_params(("parallel",), 24 << 20),
    )(w, m, v, *gparts)


def _rope_tables(T):
    rows_n = T // GRID_W
    row = jnp.repeat(jnp.arange(rows_n, dtype=F32), GRID_W)
    col = jnp.tile(jnp.arange(GRID_W, dtype=F32), rows_n)
    half = HEAD_DIM // 2
    inv_freq = ROPE_THETA ** (-jnp.arange(0, half, 2, dtype=F32) / half)
    ang = jnp.concatenate([row[:, None] * inv_freq, col[:, None] * inv_freq], axis=-1)
    cos, sin = jnp.cos(ang), jnp.sin(ang)
    cos64 = jnp.repeat(cos, 2, axis=-1)
    sin64 = jnp.stack([-sin, sin], axis=-1).reshape(T, HEAD_DIM)
    return jnp.tile(cos64, (1, 2)), jnp.tile(sin64, (1, 2))


def _t5_bucket(rel):
    half = N_BUCKETS // 2
    max_exact = half // 2
    bucket = jnp.where(rel > 0, half, 0)
    rp = jnp.abs(rel)
    rpf = jnp.maximum(rp, 1).astype(F32)
    large = max_exact + (jnp.log(rpf / max_exact) / math.log(MAX_DISTANCE / max_exact)
                         * (half - max_exact)).astype(jnp.int32)
    large = jnp.minimum(large, half - 1)
    return bucket + jnp.where(rp < max_exact, rp, large)


def _window_buckets():
    qpos = jnp.arange(WB, dtype=jnp.int32)
    kpos = jnp.arange(WK, dtype=jnp.int32) - WB
    rel = kpos[None, :] - qpos[:, None]
    return jnp.where(jnp.abs(rel) <= WINDOW, _t5_bucket(rel), -1)


def _heads_first(a, nh):
    T = a.shape[0]
    return a.reshape(T, nh, HEAD_DIM).transpose(1, 0, 2)


def _row(v):
    return v.reshape(1, -1)


def _rows8(rows, width):
    a = jnp.stack(list(rows), axis=0)
    return jnp.pad(a, ((0, 8 - a.shape[0]), (0, 0)))


def _layer_fwd(l, xin, W, tabs, comm=None, on_comm=None, target=None):
    xhat, xg, xb, x16 = xin
    T = xhat.shape[0]
    cos2, sin2, biasT = tabs
    h = _mm([x16], [W["w_in"][l]], name="mm_in", out_dtype=F32, tm=MM_ROWS, tn=IN_COLS, tk=D_MODEL)
    gains = _rows8([jnp.tile(W["q_norm"][l], 2), jnp.tile(W["k_norm"][l], 2)], LANES)
    hT, kv_nat = _qk_rope_fwd(h, gains, cos2, sin2, name="qk_rope_fwd")
    hT = hT.reshape(A_HEADS, HEAD_DIM, T)
    ka, va = _heads_first(kv_nat[:, :KW], KV), _heads_first(kv_nat[:, KW:], KV)
    res = _attn_a_fwd(ka, hT, comm=comm)
    oaT, lse_a = res[0].reshape(QW, T), res[1]
    if comm is not None:
        on_comm(res[2:])
    sink_rows = jnp.repeat(W["sink"][l], WB).reshape(KV, 1, GQ * WB)
    ob_t, lse_b = _attn_b_fwd(h, biasT, sink_rows)
    ga, gb = _row(W["out_norm_a"][l]), _row(W["out_norm_b"][l])
    ycat = _outnorm_fwd(oaT, ob_t, ga, gb)
    g1, b1 = _row(W["ln1_g"][l]), _row(W["ln1_b"][l])
    x1hat, rstd1, x1_16 = _mm_res_ln(ycat, W["w_out"][l], xhat, xg, xb, g1, b1, name="mm_out_ln", tm=512)
    gate = _mm([x1_16], [W["w_gate"][l]], name="mm_gate", out_dtype=F32, tm=MM_ROWS_WIDE, tn=D_FF, tk=D_MODEL)
    up = _mm([x1_16], [W["w_up"][l]], name="mm_up", out_dtype=BF16, tm=MM_ROWS_WIDE, tn=D_FF, tk=D_MODEL)
    cw = jnp.pad(W["conv_w"][l], ((0, 5), (0, 0)))
    cb = _row(W["conv_b"][l])
    act = _conv_glu_fwd(gate, up, cw, cb)
    g2, b2 = _row(W["ln2_g"][l]), _row(W["ln2_b"][l])
    saved = dict(x16=x16, h=h, gains=gains, hT=hT, ka=ka, va=va, oaT=oaT, lse_a=lse_a,
                 lse_b=lse_b, sink_rows=sink_rows, ob_t=ob_t,
                 ga=ga, gb=gb, ycat=ycat, x1hat=x1hat, rstd1=rstd1, x1_16=x1_16, g1=g1, b1=b1, gate=gate, up=up,
                 cw=cw, cb=cb, act=act, g2=g2, b2=b2)
    if target is not None:
        saved["head"] = _mm_res_ln(act, W["w_down"][l], x1hat, g1, b1, g2, b2, name="mm_down_ln_loss", tm=256,
                                   target=target)
        return None, saved
    x2hat, rstd2, x2_16 = _mm_res_ln(act, W["w_down"][l], x1hat, g1, b1, g2, b2, name="mm_down_ln", tm=256)
    saved.update(x2hat=x2hat, rstd2=rstd2)
    return (x2hat, g2, b2, x2_16), saved


def _layer_bwd(l, S, W, tabs, dz2, dz2_16, stats2, scatter=None, below=None):
    cos2, sin2, biasT = tabs
    T = dz2.shape[0]
    G = {}
    G["ln2_g"], G["ln2_b"] = stats2[0], stats2[1]
    G["w_down"] = _mm([S["act"]], [dz2_16], name="dw_down", out_dtype=BF16, trans_a=True, tm=D_FF // 2, tn=D_MODEL, tk=DW_TOKENS)
    dact = _mm([dz2_16], [W["w_down"][l]], name="mm_dact", out_dtype=BF16, trans_b=True, tm=MM_ROWS_WIDE, tn=D_FF,
               tk=D_MODEL)
    dg, du, dconv = _conv_glu_bwd(dact, S["gate"], S["up"], S["cw"], S["cb"])
    G["conv_w"], G["conv_b"] = dconv[0:3], dconv[3]
    G["w_gate"] = _mm([S["x1_16"]], [dg], name="dw_gate", out_dtype=BF16, trans_a=True, tm=D_MODEL, tn=D_FF // 2, tk=DW_TOKENS)
    G["w_up"] = _mm([S["x1_16"]], [du], name="dw_up", out_dtype=BF16, trans_a=True, tm=D_MODEL, tn=D_FF // 2, tk=DW_TOKENS)
    dz1, dz1_16, stats1 = _mm([dg, du], [W["w_gate"][l], W["w_up"][l]], name="mm_dx1_ln", out_dtype=F32, trans_b=True,
                              tm=MM_ROWS_LN, tn=D_MODEL, tk=D_FF, add=dz2, add_scale=ALPHA,
                              ln=(S["x1hat"], S["rstd1"], S["g1"]))
    G["ln1_g"], G["ln1_b"] = stats1[0], stats1[1]
    G["w_out"] = _mm([S["ycat"]], [dz1_16], name="dw_out", out_dtype=BF16, trans_a=True, tm=D_MODEL, tn=D_MODEL, tk=DW_TOKENS)
    dycat = _mm([dz1_16], [W["w_out"][l]], name="mm_dycat", out_dtype=F32, trans_b=True, tm=MM_ROWS, tn=D_MODEL,
                tk=D_MODEL)
    doaT, delta, dob_t, dgn = _outnorm_bwd(dycat, S["oaT"], S["ob_t"], S["ga"], S["gb"])
    G["out_norm_a"], G["out_norm_b"] = dgn[0], dgn[1]
    res = _attn_a_bwd(S["ka"], S["va"], S["hT"], doaT.reshape(KV * GQ, HEAD_DIM, T), S["lse_a"],
                      delta.reshape(KV * GQ, 1, T), comm=scatter(G) if scatter is not None else None)
    dkaT, dvaT, dqaT = res[:3]
    dh_rope, dgain = _qk_rope_bwd(S["h"], dqaT.reshape(QW, T), dkaT.reshape(KW, T), S["gains"], cos2, sin2,
                                  name="qk_rope_bwd")
    G["q_norm"], G["k_norm"] = dgain[0, :HEAD_DIM], dgain[1, :HEAD_DIM]
    dqb_t, dkb, dvb, dbiasT, dsk = _attn_b_bwd(S["h"], dob_t, S["ob_t"], S["lse_b"], biasT, S["sink_rows"])
    dh = jnp.concatenate([
        dh_rope, dvaT.transpose(2, 0, 1).reshape(T, KW).astype(BF16), dqb_t,
        dkb[WB:WB + T].astype(BF16), dvb[WB:WB + T].astype(BF16)], axis=1)
    dbias = dbiasT.reshape(KV, WK, GQ, WB).transpose(0, 2, 1, 3)
    G["w_in"] = _mm([S["x16"]], [dh], name="dw_in", out_dtype=BF16, trans_a=True, tm=D_MODEL, tn=IN_COLS, tk=DW_TOKENS)
    dxin = _mm([dh], [W["w_in"][l]], name="mm_dxin" if below is None else "mm_dxin_ln", out_dtype=F32, trans_b=True,
               tm=MM_ROWS if below is None else MM_ROWS_WIDE, tn=D_MODEL, tk=IN_COLS, add=dz1, add_scale=ALPHA, ln=below)
    return dxin, G, dbias.reshape(KV * GQ, WK * WB), dsk.reshape(KV * GQ, WB), res[3:]


BIG = ("w_in", "w_out", "w_gate", "w_up", "w_down")
COL_SHARDED = ("w_in", "w_gate", "w_up")


def _unshard(name, blocks):
    _, r, c = blocks.shape
    if name in COL_SHARDED:
        return blocks.transpose(1, 0, 2).reshape(r, N_DEV * c)
    return blocks.reshape(N_DEV * r, c)


def _to_owner_blocks(name, full, shard_shape):
    _, r, c = shard_shape
    if name in COL_SHARDED:
        return full.reshape(r, N_DEV, c).transpose(1, 0, 2)
    return full.reshape(N_DEV, r, c)


def _pack_small(vals, tail):
    flat = jnp.concatenate([vals[n].reshape(-1).astype(F32) for n in SMALL_NAMES] + [tail])
    pad = (-flat.shape[0]) % (8 * LANES)
    return jnp.pad(flat, (0, pad)).reshape(-1, LANES)


def _unpack_small(packed, shapes):
    flat = packed.reshape(-1)
    out, off = {}, 0
    for n in SMALL_NAMES:
        size = math.prod(shapes[n])
        out[n] = flat[off:off + size].reshape(shapes[n])
        off += size
    return out, flat[off]


def kernel(x, rel_bias, w_in, q_norm, k_norm, sink, out_norm_a, out_norm_b, w_out, ln1_g, ln1_b, w_gate, w_up, conv_w, conv_b, w_down, ln2_g, ln2_b, loss_target, m_rel_bias, m_w_in, m_q_norm, m_k_norm, m_sink, m_out_norm_a, m_out_norm_b, m_w_out, m_ln1_g, m_ln1_b, m_w_gate, m_w_up, m_conv_w, m_conv_b, m_w_down, m_ln2_g, m_ln2_b, v_rel_bias, v_w_in, v_q_norm, v_k_norm, v_sink, v_out_norm_a, v_out_norm_b, v_w_out, v_ln1_g, v_ln1_b, v_w_gate, v_w_up, v_conv_w, v_conv_b, v_w_down, v_ln2_g, v_ln2_b):
    P = dict(rel_bias=rel_bias, w_in=w_in, q_norm=q_norm, k_norm=k_norm, sink=sink, out_norm_a=out_norm_a,
             out_norm_b=out_norm_b, w_out=w_out, ln1_g=ln1_g, ln1_b=ln1_b, w_gate=w_gate, w_up=w_up, conv_w=conv_w,
             conv_b=conv_b, w_down=w_down, ln2_g=ln2_g, ln2_b=ln2_b)
    M = dict(rel_bias=m_rel_bias, w_in=m_w_in, q_norm=m_q_norm, k_norm=m_k_norm, sink=m_sink, out_norm_a=m_out_norm_a,
             out_norm_b=m_out_norm_b, w_out=m_w_out, ln1_g=m_ln1_g, ln1_b=m_ln1_b, w_gate=m_w_gate, w_up=m_w_up,
             conv_w=m_conv_w, conv_b=m_conv_b, w_down=m_w_down, ln2_g=m_ln2_g, ln2_b=m_ln2_b)
    V = dict(rel_bias=v_rel_bias, w_in=v_w_in, q_norm=v_q_norm, k_norm=v_k_norm, sink=v_sink, out_norm_a=v_out_norm_a,
             out_norm_b=v_out_norm_b, w_out=v_w_out, ln1_g=v_ln1_g, ln1_b=v_ln1_b, w_gate=v_w_gate, w_up=v_w_up,
             conv_w=v_conv_w, conv_b=v_conv_b, w_down=v_w_down, ln2_g=v_ln2_g, ln2_b=v_ln2_b)
    names = list(P)
    T = x.shape[1]
    me = 4 * lax.axis_index("x") + 2 * lax.axis_index("y") + lax.axis_index("c")

    L, taps, fc = conv_w.shape
    W = {n: ([None] * DEPTH if n in BIG else P[n]) for n in names}

    def wire(n, l):
        return P[n][l].astype(BF16)

    def take(n, l, gathered):
        W[n][l] = _unshard(n, gathered)

    take("w_in", 0, _exchange([wire("w_in", 0)], [True], name="gather_w_in0")[0])
    later = [(n, l) for l in range(DEPTH) for n in BIG if (n, l) != ("w_in", 0)]
    cw_shard = conv_w.reshape(-1)
    cw_wire = jnp.pad(cw_shard, (0, (-cw_shard.shape[0]) % LANES)).reshape(-1, LANES)
    gather_rest = _Comm([wire(n, l) for n, l in later] + [cw_wire], [True] * (len(later) + 1))

    def on_gathered(outs):
        for (n, l), g in zip(later, outs):
            take(n, l, g)
        cw_all = outs[-1].reshape(N_DEV, -1)[:, :cw_shard.shape[0]].reshape(N_DEV, L, taps, fc)
        W["conv_w"] = cw_all.transpose(1, 2, 0, 3).reshape(L, taps, N_DEV * fc)

    cos2, sin2 = _rope_tables(T)
    bucket = _window_buckets()
    bias = _bias_table(rel_bias.T, bucket.reshape(1, WB * WK))
    biasT = bias.reshape(KV, GQ, WB, WK).transpose(0, 3, 1, 2).reshape(KV, WK, GQ * WB)
    biasT = _end_tables(biasT)
    tabs = (cos2, sin2, biasT)

    ones, zeros = jnp.ones((1, D_MODEL), F32), jnp.zeros((1, D_MODEL), F32)
    cur = (x[0], ones, zeros, x[0].astype(BF16))
    saved = []
    for l in range(DEPTH):
        cur, S = _layer_fwd(l, cur, W, tabs, comm=gather_rest if l == 0 else None, on_comm=on_gathered,
                            target=loss_target[0] if l == DEPTH - 1 else None)
        saved.append(S)

    def owner_blocks(n, l):
        return _to_owner_blocks(n, grads[l][n], P[n].shape)

    early = ([(n, l) for l in range(1, DEPTH) for n in BIG] + [(n, 0) for n in BIG if n != "w_in"])

    def scatter_early(g0):
        grads[0] = g0
        return _Comm([owner_blocks(n, l) for n, l in early], [False] * len(early))

    grads = [None] * DEPTH
    dbs, dsks = [None] * DEPTH, [None] * DEPTH
    dz, dz16, stats = saved[-1]["head"]
    loss_part = stats[2, 0:1]
    recv = {}
    for l in reversed(range(DEPTH)):
        S = saved[l]
        Sp = saved[l - 1] if l > 0 else None
        dxin, grads[l], dbs[l], dsks[l], got = _layer_bwd(
            l, S, W, tabs, dz, dz16, stats, scatter=scatter_early if l == 0 else None,
            below=(Sp["x2hat"], Sp["rstd2"], Sp["g2"]) if l > 0 else None)
        if l == 0:
            recv.update(zip(early, got))
        else:
            dz, dz16, stats = dxin
    grad_x = dxin[None]

    drb, dsink = _bias_sink_grads(dbs, dsks, bucket.T.reshape(1, WK * WB))
    small_g = {n: jnp.stack([grads[l][n] for l in range(DEPTH)]) for n in SMALL_NAMES if n not in ("rel_bias", "sink")}
    small_g["rel_bias"] = drb.T
    small_g["sink"] = dsink.reshape(DEPTH, KV * GQ)
    recv[("w_in", 0)], small_recv = _exchange([owner_blocks("w_in", 0), _pack_small(small_g, loss_part)], [False, True],
                                              name="scatter_w_in0_gather_small")

    out_g, out_d, out_m, out_v = {}, {}, {}, {}
    for n in BIG:
        shp = P[n].shape
        rows, cols = shp[0] * shp[1], shp[2]
        res = _adamw(P[n].reshape(rows, cols), M[n].reshape(rows, cols), V[n].reshape(rows, cols),
                     [recv[(n, l)] for l in range(DEPTH)], name="adamw_" + n)
        out_g[n], out_d[n], out_m[n], out_v[n] = (r.reshape(shp) for r in res)
    full_shapes = {n: W[n].shape for n in SMALL_NAMES}

    def small_state(D):
        vals = {n: D[n] for n in SMALL_NAMES if n != "conv_w"}
        cw = jnp.zeros((L, taps, N_DEV, fc), F32)
        cw = lax.dynamic_update_slice(cw, D["conv_w"].reshape(L, taps, 1, fc), (0, 0, me, 0))
        vals["conv_w"] = cw.reshape(L, taps, N_DEV * fc)
        return _pack_small(vals, jnp.zeros((1,), F32))

    sw, sm, sv = small_state(P), small_state(M), small_state(V)
    res = _adamw(sw, sm, sv, [small_recv], name="adamw_small")
    loss = _unpack_small(res[0], full_shapes)[1]
    for dst, packed in zip((out_g, out_d, out_m, out_v), res):
        vals, _ = _unpack_small(packed, full_shapes)
        for n in SMALL_NAMES:
            if n == "conv_w":
                sl = lax.dynamic_slice(vals[n].reshape(L, taps, N_DEV, fc), (0, 0, me, 0), (L, taps, 1, fc))
                dst[n] = sl.reshape(L, taps, fc)
            else:
                dst[n] = vals[n]
    return (loss, grad_x, *[out_g[n] for n in names], *[out_d[n] for n in names],
            *[out_m[n] for n in names], *[out_v[n] for n in names])
```

```python
import functools
import math

import jax
import jax.numpy as jnp
from jax import lax
from jax.experimental import pallas as pl
from jax.experimental.pallas import tpu as pltpu

F32 = jnp.float32
BF16 = jnp.bfloat16
MESH = pl.DeviceIdType.MESH

N_DEV = 8
D_MODEL = 1024
DEPTH = 2
HEAD_DIM = 64
KV = 2
GQ = 4
QW = KV * GQ * HEAD_DIM
KW = KV * HEAD_DIM
ROPE_W = QW + KW
IN_COLS = 2 * (QW + 2 * KW)
D_FF = 2816
GRID_W = 64
ROPE_THETA = 10000.0
WINDOW = 128
N_BUCKETS = 32
MAX_DISTANCE = 128
ALPHA = (2.0 * DEPTH) ** 0.25
RMS_EPS = 1e-6
LN_EPS = 1e-5
SCALE = HEAD_DIM ** -0.5
LOG2E = math.log2(math.e)
LN2 = math.log(2.0)
NEG = -1e30
ONES_ROWS = 16

ADAM_LR = 0.001
ADAM_B1 = 0.9
ADAM_B2 = 0.999
ADAM_EPS = 1e-08
ADAM_WD = 0.01
ADAM_STEP = 10

LANES = 128
MM_ROWS = 1024
MM_ROWS_LN = 256
MM_ROWS_WIDE = 512
DW_TOKENS = 1024
VMEM_CAP = 60 * 1024 * 1024
SMALL_NAMES = ("rel_bias", "q_norm", "k_norm", "sink", "out_norm_a", "out_norm_b", "ln1_g", "ln1_b",
               "conv_b", "ln2_g", "ln2_b", "conv_w")


def _params(sem, est_bytes):
    limit = int(min(VMEM_CAP, est_bytes + (8 << 20)))
    return pltpu.CompilerParams(dimension_semantics=sem, vmem_limit_bytes=limit)


def _nbytes(shape, dtype):
    return math.prod(shape) * jnp.dtype(dtype).itemsize


class _Comm:
    def __init__(self, parts, gathers):
        self.parts, self.gathers, self.n = list(parts), list(gathers), len(parts)
        hbm = pl.BlockSpec(memory_space=pltpu.HBM)
        self.in_specs = [hbm] * self.n
        self.out_specs = [hbm] * self.n
        self.out_shape = [jax.ShapeDtypeStruct((N_DEV,) + tuple(p.shape if g else p.shape[1:]), p.dtype)
                          for p, g in zip(self.parts, self.gathers)]
        self.scratch = [pltpu.SemaphoreType.DMA((self.n * (N_DEV - 1),)), pltpu.SemaphoreType.DMA((self.n * (N_DEV - 1),)),
                        pltpu.SemaphoreType.DMA((self.n,))]

    def bind(self, ins, outs, sems):
        send_sems, recv_sems, local_sems = sems
        gathers, n = self.gathers, self.n
        me = 4 * lax.axis_index("x") + 2 * lax.axis_index("y") + lax.axis_index("c")

        def src(k, j):
            return ins[k] if gathers[k] else ins[k].at[j]

        def copy(k, d, peer, lands_in):
            return pltpu.make_async_remote_copy(
                src_ref=src(k, peer), dst_ref=outs[k].at[lands_in],
                send_sem=send_sems.at[k * (N_DEV - 1) + d - 1], recv_sem=recv_sems.at[k * (N_DEV - 1) + d - 1],
                device_id=(peer // 4, lax.rem(peer // 2, 2), lax.rem(peer, 2)), device_id_type=MESH)

        def send(k, d):
            return copy(k, d, lax.rem(me + d, N_DEV), me)

        def arrival(k, d):
            frm = lax.rem(me + N_DEV - d, N_DEV)
            return copy(k, d, frm, frm)

        def local(k):
            return pltpu.make_async_copy(src(k, me), outs[k].at[me], local_sems.at[k])

        def start():
            for k in range(n):
                local(k).start()
                for d in range(1, N_DEV):
                    send(k, d).start()

        def finish():
            for k in range(n):
                for d in range(1, N_DEV):
                    arrival(k, d).wait_recv()
            for k in range(n):
                for d in range(1, N_DEV):
                    send(k, d).wait_send()
                local(k).wait()

        return start, finish


def _host_comm(comm, refs, n_in, n_out, n_scratch, grid):
    n = comm.n if comm is not None else 0
    own_in, cin = refs[:n_in], refs[n_in:n_in + n]
    own_out, cout = refs[n_in + n:n_in + n + n_out], refs[n_in + n + n_out:n_in + 2 * n + n_out]
    base = n_in + 2 * n + n_out
    own_scratch, sems = refs[base:base + n_scratch], refs[base + n_scratch:]
    own = tuple(own_in) + tuple(own_out) + tuple(own_scratch)
    if comm is None:
        return own, lambda: None, lambda: None
    start, finish = comm.bind(cin, cout, sems)
    first = last = None
    for ax, size in enumerate(grid):
        pid = pl.program_id(ax)
        first = (pid == 0) if first is None else first & (pid == 0)
        last = (pid == size - 1) if last is None else last & (pid == size - 1)
    return own, lambda: pl.when(first)(start), lambda: pl.when(last)(finish)


def _exchange(parts, gathers, name):
    comm = _Comm(parts, gathers)
    n = comm.n

    def body(*refs):
        start, finish = comm.bind(refs[:n], refs[n:2 * n], refs[2 * n:])
        start()
        finish()

    return pl.pallas_call(body, name=name, out_shape=comm.out_shape, in_specs=comm.in_specs, out_specs=comm.out_specs,
                          scratch_shapes=comm.scratch)(*comm.parts)


def _mm(a_list, b_list, *, name, out_dtype, tm, tn, tk, trans_a=False, trans_b=False, add=None, add_scale=1.0,
        ln=None):
    assert not (trans_a and trans_b)
    na = len(a_list)
    if trans_a:
        K, M = a_list[0].shape
    else:
        M, K = a_list[0].shape
    N = b_list[0].shape[0 if trans_b else 1]
    tm, tn, tk = min(tm, M), min(tn, N), min(tk, K)
    assert M % tm == 0 and N % tn == 0 and K % tk == 0, (name, M, N, K, tm, tn, tk)
    nk = K // tk
    dims = (((0,), (0,)), ((), ())) if trans_a else (((1,), (1 if trans_b else 0,)), ((), ()))

    n_in = 2 * na + (add is not None) + (3 if ln is not None else 0)

    def body(*refs):
        a_refs, b_refs = refs[:na], refs[na:2 * na]
        add_ref = refs[2 * na] if add is not None else None
        o_ref = refs[n_in]
        k = pl.program_id(2)

        part = None
        for a_ref, b_ref in zip(a_refs, b_refs):
            prod = lax.dot_general(a_ref[...].astype(BF16), b_ref[...].astype(BF16), dims,
                                   preferred_element_type=F32)
            part = prod if part is None else part + prod

        def finish(res):
            if add_ref is not None:
                res = res + add_scale * add_ref[...]
            if ln is None:
                o_ref[...] = res.astype(o_ref.dtype)
                return
            xh_ref, rstd_ref, g_ref = refs[n_in - 3:n_in]
            dzb_ref, st_ref = refs[n_in + 1:n_in + 3]

            @pl.when(pl.program_id(0) == 0)
            def _():
                st_ref[...] = jnp.zeros_like(st_ref)

            xh = xh_ref[...]
            st_ref[0:1, :] += jnp.sum(res * xh, axis=0, keepdims=True)
            st_ref[1:2, :] += jnp.sum(res, axis=0, keepdims=True)
            dxh = res * g_ref[...]
            dz = rstd_ref[...] * (dxh - jnp.mean(dxh, axis=1, keepdims=True)
                                  - xh * jnp.mean(dxh * xh, axis=1, keepdims=True))
            o_ref[...] = dz
            dzb_ref[...] = dz.astype(BF16)

        if nk == 1:
            finish(part)
        else:
            acc_ref = refs[-1]

            @pl.when(k == 0)
            def _():
                acc_ref[...] = part

            @pl.when(k > 0)
            def _():
                acc_ref[...] += part

            @pl.when(k == nk - 1)
            def _():
                finish(acc_ref[...])

    if trans_a:
        a_spec = pl.BlockSpec((tk, tm), lambda i, j, k: (k, i))
    else:
        a_spec = pl.BlockSpec((tm, tk), lambda i, j, k: (i, k))
    if trans_b:
        b_spec = pl.BlockSpec((tn, tk), lambda i, j, k: (j, k))
    else:
        b_spec = pl.BlockSpec((tk, tn), lambda i, j, k: (k, j))
    o_spec = pl.BlockSpec((tm, tn), lambda i, j, k: (i, j))
    in_specs = [a_spec] * na + [b_spec] * na + ([o_spec] if add is not None else [])
    if ln is not None:
        assert tn == N and nk == 1
        in_specs += [o_spec, pl.BlockSpec((tm, 1), lambda i, j, k: (i, 0)), pl.BlockSpec((1, tn), lambda i, j, k: (0, 0))]
    est = (2 * na * (_nbytes((tm, tk), a_list[0].dtype) + _nbytes((tk, tn), b_list[0].dtype))
           + na * (_nbytes((tm, tk), BF16) + _nbytes((tk, tn), BF16))
           + 2 * _nbytes((tm, tn), out_dtype) + 3 * _nbytes((tm, tn), F32)
           + (2 * _nbytes((tm, tn), F32) if add is not None else 0))
    args = list(a_list) + list(b_list) + ([add] if add is not None else [])
    if ln is not None:
        return pl.pallas_call(
            body, name=name, grid=(M // tm, 1, 1),
            out_shape=(jax.ShapeDtypeStruct((M, N), F32), jax.ShapeDtypeStruct((M, N), BF16),
                       jax.ShapeDtypeStruct((8, N), F32)),
            in_specs=in_specs, out_specs=(o_spec, o_spec, pl.BlockSpec((8, tn), lambda i, j, k: (0, 0))),
            compiler_params=_params(("arbitrary",) * 3, est + 6 * _nbytes((tm, tn), F32)),
        )(*args, *ln)
    return pl.pallas_call(
        body, name=name, grid=(M // tm, N // tn, nk),
        out_shape=jax.ShapeDtypeStruct((M, N), out_dtype),
        in_specs=in_specs, out_specs=o_spec,
        scratch_shapes=[pltpu.VMEM((tm, tn), F32)] if nk > 1 else [],
        compiler_params=_params(("parallel", "parallel", "arbitrary"), est),
    )(*args)


def _mm_res_ln(a, w, res_hat, res_g, res_b, ln_g, ln_b, *, name, tm, target=None):
    T, K = a.shape
    D = w.shape[1]
    tm = min(tm, T)
    head = target is not None

    def body(a_ref, w_ref, rh_ref, rg_ref, rb_ref, g_ref, b_ref, *rest):
        branch = jnp.dot(a_ref[...].astype(BF16), w_ref[...], preferred_element_type=F32)
        z = ALPHA * (rh_ref[...] * rg_ref[...] + rb_ref[...]) + branch
        mu = jnp.mean(z, axis=1, keepdims=True)
        zc = z - mu
        var = jnp.mean(zc * zc, axis=1, keepdims=True)
        rstd = lax.rsqrt(var + LN_EPS)
        xhat = zc * rstd
        if not head:
            xhat_ref, rstd_ref, xb_ref = rest
            xhat_ref[...] = xhat
            rstd_ref[...] = rstd
            xb_ref[...] = (xhat * g_ref[...] + b_ref[...]).astype(BF16)
            return
        t_ref, dz_ref, dzb_ref, st_ref = rest

        @pl.when(pl.program_id(0) == 0)
        def _():
            st_ref[...] = jnp.zeros_like(st_ref)

        g = g_ref[...]
        err = (xhat * g + b_ref[...]) - t_ref[...]
        dx = err * (1.0 / D)
        st_ref[2:3, :] += 0.5 * jnp.sum(jnp.sum(err * err, axis=1, keepdims=True) * (1.0 / D), axis=0, keepdims=True)
        st_ref[0:1, :] += jnp.sum(dx * xhat, axis=0, keepdims=True)
        st_ref[1:2, :] += jnp.sum(dx, axis=0, keepdims=True)
        dxh = dx * g
        dz = rstd * (dxh - jnp.mean(dxh, axis=1, keepdims=True) - xhat * jnp.mean(dxh * xhat, axis=1, keepdims=True))
        dz_ref[...] = dz
        dzb_ref[...] = dz.astype(BF16)

    row = pl.BlockSpec((tm, D), lambda i: (i, 0))
    vec = pl.BlockSpec((1, D), lambda i: (0, 0))
    est = (2 * (_nbytes((tm, K), a.dtype) + _nbytes((K, D), BF16)) + 4 * _nbytes((tm, D), F32) * 2
           + 8 * _nbytes((tm, D), F32))
    in_specs = [pl.BlockSpec((tm, K), lambda i: (i, 0)), pl.BlockSpec((K, D), lambda i: (0, 0)), row, vec, vec, vec, vec]
    if head:
        return pl.pallas_call(
            body, name=name, grid=(T // tm,),
            out_shape=(jax.ShapeDtypeStruct((T, D), F32), jax.ShapeDtypeStruct((T, D), BF16),
                       jax.ShapeDtypeStruct((8, D), F32)),
            in_specs=in_specs + [row], out_specs=(row, row, pl.BlockSpec((8, D), lambda i: (0, 0))),
            compiler_params=_params(("arbitrary",), est),
        )(a, w, res_hat, res_g, res_b, ln_g, ln_b, target)
    return pl.pallas_call(
        body, name=name, grid=(T // tm,),
        out_shape=(jax.ShapeDtypeStruct((T, D), F32), jax.ShapeDtypeStruct((T, 1), F32),
                   jax.ShapeDtypeStruct((T, D), BF16)),
        in_specs=in_specs, out_specs=(row, pl.BlockSpec((tm, 1), lambda i: (i, 0)), row),
        compiler_params=_params(("parallel",), est),
    )(a, w, res_hat, res_g, res_b, ln_g, ln_b)


def _ln_bwd(xhat, rstd, ln_g, dx, *, name, tm=256):
    T, D = xhat.shape
    tm = min(tm, T)

    def body(xhat_ref, rstd_ref, g_ref, d_ref, dz_ref, dzb_ref, st_ref):
        i = pl.program_id(0)

        @pl.when(i == 0)
        def _():
            st_ref[...] = jnp.zeros_like(st_ref)

        xh = xhat_ref[...]
        dxv = d_ref[...]
        st_ref[0:1, :] += jnp.sum(dxv * xh, axis=0, keepdims=True)
        st_ref[1:2, :] += jnp.sum(dxv, axis=0, keepdims=True)
        dxh = dxv * g_ref[...]
        m1 = jnp.mean(dxh, axis=1, keepdims=True)
        m2 = jnp.mean(dxh * xh, axis=1, keepdims=True)
        dz = rstd_ref[...] * (dxh - m1 - xh * m2)
        dz_ref[...] = dz
        dzb_ref[...] = dz.astype(BF16)

    row = pl.BlockSpec((tm, D), lambda i: (i, 0))
    vec = pl.BlockSpec((1, D), lambda i: (0, 0))
    est = 2 * 4 * _nbytes((tm, D), F32) + 6 * _nbytes((tm, D), F32)
    return pl.pallas_call(
        body, name=name, grid=(T // tm,),
        out_shape=(jax.ShapeDtypeStruct((T, D), F32), jax.ShapeDtypeStruct((T, D), BF16),
                   jax.ShapeDtypeStruct((8, D), F32)),
        in_specs=[row, pl.BlockSpec((tm, 1), lambda i: (i, 0)), vec, row],
        out_specs=(row, row, pl.BlockSpec((8, D), lambda i: (0, 0))),
        compiler_params=_params(("arbitrary",), est),
    )(xhat, rstd, ln_g, dx)


def _pair_swap(v, even):
    return jnp.where(even, pltpu.roll(v, LANES - 1, 1), pltpu.roll(v, 1, 1))


def _half_sums(v, lo):
    s_lo = jnp.sum(jnp.where(lo, v, 0.0), axis=1, keepdims=True)
    s_hi = jnp.sum(jnp.where(lo, 0.0, v), axis=1, keepdims=True)
    return jnp.where(lo, s_lo, s_hi)


A_COLS = ROPE_W + KW
A_HEADS = A_COLS // HEAD_DIM
A_K0, A_V0 = KV * GQ, KV * GQ + KV


def _qk_rope_fwd(h, gains, cos2, sin2, *, name, tm=256):
    T = h.shape[0]
    tm = min(tm, T)
    nch = A_COLS // LANES

    def body(h_ref, g_ref, c_ref, s_ref, oT_ref, kv_ref):
        lane = lax.broadcasted_iota(jnp.int32, (tm, LANES), 1)
        lo, even = lane < HEAD_DIM, lane % 2 == 0
        c, s = c_ref[...], s_ref[...]
        for j in range(nch):
            x = h_ref[:, j * LANES:(j + 1) * LANES]
            isq, isv = j < QW // LANES, j == nch - 1
            if isv:
                out = x
            else:
                g = g_ref[0:1, :] if isq else g_ref[1:2, :]
                r = lax.rsqrt(_half_sums(x * x, lo) * (1.0 / HEAD_DIM) + RMS_EPS)
                nrm = x * r * g
                out = nrm * c + _pair_swap(nrm, even) * s
            if isq:
                out = out * (SCALE * LOG2E)
            else:
                kv_ref[:, (j - QW // LANES) * LANES:(j - QW // LANES + 1) * LANES] = out.astype(BF16)
            oT_ref[j * LANES:(j + 1) * LANES, :] = out.T.astype(BF16)

    est = 2 * (_nbytes((tm, A_COLS), F32) + 2 * _nbytes((tm, A_COLS), BF16) + 2 * _nbytes((tm, LANES), F32)) + (4 << 20)
    return pl.pallas_call(
        body, name=name, grid=(T // tm,),
        out_shape=(jax.ShapeDtypeStruct((A_COLS, T), BF16), jax.ShapeDtypeStruct((T, 2 * KW), BF16)),
        in_specs=[pl.BlockSpec((tm, A_COLS), lambda i: (i, 0)), pl.BlockSpec((8, LANES), lambda i: (0, 0)),
                  pl.BlockSpec((tm, LANES), lambda i: (i, 0)), pl.BlockSpec((tm, LANES), lambda i: (i, 0))],
        out_specs=(pl.BlockSpec((A_COLS, tm), lambda i: (0, i)), pl.BlockSpec((tm, 2 * KW), lambda i: (i, 0))),
        compiler_params=_params(("parallel",), est),
    )(h, gains, cos2, sin2)


def _qk_rope_bwd(h, dqT, dkT, gains, cos2, sin2, *, name, tm=256):
    T = h.shape[0]
    tm = min(tm, T)
    nch = ROPE_W // LANES

    def body(h_ref, dq_ref, dk_ref, g_ref, c_ref, s_ref, dh_ref, dg_ref):
        i = pl.program_id(0)

        @pl.when(i == 0)
        def _():
            dg_ref[...] = jnp.zeros_like(dg_ref)

        lane = lax.broadcasted_iota(jnp.int32, (tm, LANES), 1)
        lo, even = lane < HEAD_DIM, lane % 2 == 0
        c, s = c_ref[...], s_ref[...]
        acc = [None, None]
        for j in range(nch):
            x = h_ref[:, j * LANES:(j + 1) * LANES]
            isq = j < QW // LANES
            g = g_ref[0:1, :] if isq else g_ref[1:2, :]
            d = dq_ref[j * LANES:(j + 1) * LANES, :].T * SCALE if isq else dk_ref[...].T
            r = lax.rsqrt(_half_sums(x * x, lo) * (1.0 / HEAD_DIM) + RMS_EPS)
            dn = d * c + _pair_swap(d * s, even)
            xr = x * r
            part = jnp.sum(dn * xr, axis=0, keepdims=True)
            acc[0 if isq else 1] = part if acc[0 if isq else 1] is None else acc[0 if isq else 1] + part
            dng = dn * g
            dx = r * dng - xr * (r * r) * (_half_sums(dng * x, lo) * (1.0 / HEAD_DIM))
            dh_ref[:, j * LANES:(j + 1) * LANES] = dx.astype(BF16)
        for row in range(2):
            folded = acc[row] + pltpu.roll(acc[row], HEAD_DIM, 1)
            dg_ref[row:row + 1, :] += folded

    est = 2 * (2 * _nbytes((tm, ROPE_W), F32) + _nbytes((tm, ROPE_W), BF16) + 2 * _nbytes((tm, LANES), F32)) + (4 << 20)
    return pl.pallas_call(
        body, name=name, grid=(T // tm,),
        out_shape=(jax.ShapeDtypeStruct((T, ROPE_W), BF16), jax.ShapeDtypeStruct((8, LANES), F32)),
        in_specs=[pl.BlockSpec((tm, ROPE_W), lambda i: (i, 0)), pl.BlockSpec((QW, tm), lambda i: (0, i)),
                  pl.BlockSpec((KW, tm), lambda i: (0, i)), pl.BlockSpec((8, LANES), lambda i: (0, 0)),
                  pl.BlockSpec((tm, LANES), lambda i: (i, 0)), pl.BlockSpec((tm, LANES), lambda i: (i, 0))],
        out_specs=(pl.BlockSpec((tm, ROPE_W), lambda i: (i, 0)), pl.BlockSpec((8, LANES), lambda i: (0, 0))),
        compiler_params=_params(("arbitrary",), est),
    )(h, dqT, dkT, gains, cos2, sin2)


def _attn_a_fwd(k, hT, *, comm=None, tq=8192, tk=2048, cq=512):
    G, T, HD = k.shape
    HE = HD + ONES_ROWS
    tq, tk = min(tq, T), min(tk, T)
    cq = min(cq, tq)
    nk, nt = T // tk, T // tq
    grid = (G, GQ * nt, nk)

    def body(*refs):
        (k_ref, qT_ref, v_ref, oT_ref, lse_ref, m_sc, acc_sc), comm_start, comm_finish = _host_comm(
            comm, refs, 3, 2, 2, grid)
        kv = pl.program_id(2)
        comm_start()
        v1T = jnp.concatenate([v_ref[...], jnp.ones((ONES_ROWS, tk), BF16)], axis=0)

        @pl.when(kv == 0)
        def _():
            m_sc[...] = jnp.full_like(m_sc, NEG)
            acc_sc[...] = jnp.zeros_like(acc_sc)

        def scores(c):
            return jnp.dot(k_ref[...], qT_ref[:, c * cq:(c + 1) * cq], preferred_element_type=F32)

        nc = tq // cq
        ahead = scores(0)
        for c in range(nc):
            cols = slice(c * cq, (c + 1) * cq)
            sT = ahead
            if c + 1 < nc:
                ahead = scores(c + 1)
            m_prev = m_sc[:, cols]
            m_new = jnp.maximum(m_prev, jnp.max(sT, axis=0, keepdims=True))
            pT = jnp.exp2(sT - m_new).astype(BF16)
            acc_sc[:, cols] = (jnp.exp2(m_prev - m_new) * acc_sc[:, cols]
                               + jnp.dot(v1T, pT, preferred_element_type=F32))
            m_sc[:, cols] = m_new

        @pl.when(kv == nk - 1)
        def _():
            l = acc_sc[HD:HD + 1, :]
            oT_ref[...] = acc_sc[0:HD, :] / l
            lse_ref[...] = m_sc[...] + jnp.log2(l)

        comm_finish()

    qtr = pl.BlockSpec((None, HD, tq), lambda g, i, j: (g * GQ + i // nt, 0, i % nt))
    qvec = pl.BlockSpec((None, 1, tq), lambda g, i, j: (g * GQ + i // nt, 0, i % nt))
    est = 6 * _nbytes((cq, tk), F32) + (8 << 20)
    hosted = comm is not None
    return pl.pallas_call(
        body, name="attn_a_fwd_comm" if hosted else "attn_a_fwd", grid=grid,
        out_shape=[jax.ShapeDtypeStruct((G * GQ, HD, T), F32), jax.ShapeDtypeStruct((G * GQ, 1, T), F32)]
        + (comm.out_shape if hosted else []),
        in_specs=[pl.BlockSpec((None, tk, HD), lambda g, i, j: (g, j, 0)), qtr,
                  pl.BlockSpec((None, HD, tk), lambda g, i, j: (A_V0 + g, 0, j))] + (comm.in_specs if hosted else []),
        out_specs=[qtr, qvec] + (comm.out_specs if hosted else []),
        scratch_shapes=[pltpu.VMEM((1, tq), F32), pltpu.VMEM((HE, tq), F32)] + (comm.scratch if hosted else []),
        compiler_params=_params(("arbitrary",) * 3 if hosted else ("parallel", "parallel", "arbitrary"), est),
    )(k, hT, hT, *(comm.parts if hosted else []))


def _attn_a_bwd(k, v, hT, doT, lse_row, delta_row, *, comm=None, tq=8192, tk=1024, cq=256):
    G, T, HD = k.shape
    tq, tk = min(tq, T), min(tk, T)
    cq = min(cq, tq)
    nqt = T // tq
    nq, nc = GQ * nqt, tq // cq
    nt = (((1,), (1,)), ((), ()))

    grid = (G, T // tk, nq)

    def body(*refs):
        (k_ref, v_ref, kT_ref, qT_ref, doT_ref, lse_ref, dl_ref, dkT_ref, dvT_ref, dqT_ref, dk_sc, dv_sc), \
            comm_start, comm_finish = _host_comm(comm, refs, 7, 3, 2, grid)
        j, i = pl.program_id(1), pl.program_id(2)
        comm_start()

        @pl.when((j == 0) & (i == 0))
        def _():
            dqT_ref[...] = jnp.zeros_like(dqT_ref)

        @pl.when(i == 0)
        def _():
            dk_sc[...] = jnp.zeros_like(dk_sc)
            dv_sc[...] = jnp.zeros_like(dv_sc)

        def scores(c):
            cols = slice(c * cq, (c + 1) * cq)
            return (jnp.dot(k_ref[...], qT_ref[:, cols], preferred_element_type=F32),
                    jnp.dot(v_ref[...], doT_ref[:, cols], preferred_element_type=F32))

        ahead = scores(0)
        dk_part = dv_part = None
        for c in range(nc):
            cols = slice(c * cq, (c + 1) * cq)
            sT, dpT = ahead
            if c + 1 < nc:
                ahead = scores(c + 1)
            pT = jnp.exp2(sT - lse_ref[:, cols])
            dsT = (pT * (dpT - dl_ref[:, cols])).astype(BF16)
            dv_c = lax.dot_general(doT_ref[:, cols], pT.astype(BF16), nt, preferred_element_type=F32)
            dk_c = lax.dot_general(qT_ref[:, cols], dsT, nt, preferred_element_type=F32)
            dv_part = dv_c if dv_part is None else dv_part + dv_c
            dk_part = dk_c if dk_part is None else dk_part + dk_c
            out_cols = pl.ds(pl.multiple_of((i % nqt) * tq + c * cq, cq), cq)
            dqT_ref[i // nqt, :, out_cols] += jnp.dot(kT_ref[...], dsT, preferred_element_type=F32)
        dk_sc[...] += dk_part
        dv_sc[...] += dv_part

        @pl.when(i == nq - 1)
        def _():
            dkT_ref[...] = dk_sc[...] * LN2
            dvT_ref[...] = dv_sc[...]

        comm_finish()

    krow = pl.BlockSpec((None, tk, HD), lambda g, j, i: (g, j, 0))
    ktr = pl.BlockSpec((None, HD, tk), lambda g, j, i: (g, 0, j))
    ktr_h = pl.BlockSpec((None, HD, tk), lambda g, j, i: (A_K0 + g, 0, j))
    qtr = pl.BlockSpec((None, HD, tq), lambda g, j, i: (g * GQ + i // nqt, 0, i % nqt))
    qvec = pl.BlockSpec((None, 1, tq), lambda g, j, i: (g * GQ + i // nqt, 0, i % nqt))
    whole = pl.BlockSpec((GQ, HD, T), lambda g, j, i: (g, 0, 0))
    est = 8 * _nbytes((cq, tk), F32) + 2 * _nbytes((GQ, HD, T), F32) + (8 << 20)
    hosted = comm is not None
    return pl.pallas_call(
        body, name="attn_a_bwd_comm" if hosted else "attn_a_bwd", grid=grid,
        out_shape=[jax.ShapeDtypeStruct((G, HD, T), F32), jax.ShapeDtypeStruct((G, HD, T), F32),
                   jax.ShapeDtypeStruct((G * GQ, HD, T), F32)] + (comm.out_shape if hosted else []),
        in_specs=[krow, krow, ktr_h, qtr, qtr, qvec, qvec] + (comm.in_specs if hosted else []),
        out_specs=[ktr, ktr, whole] + (comm.out_specs if hosted else []),
        scratch_shapes=[pltpu.VMEM((HD, tk), F32), pltpu.VMEM((HD, tk), F32)] + (comm.scratch if hosted else []),
        compiler_params=_params(("arbitrary", "arbitrary", "arbitrary"), est),
    )(k, v, hT, hT, doT, lse_row, delta_row, *(comm.parts if hosted else []))


WB = WINDOW
WK = 3 * WINDOW


QB_COL0 = (ROPE_W + KW) // (GQ * HEAD_DIM)
KB_COL = (ROPE_W + KW + QW) // KW
GW = GQ * HEAD_DIM


WSTEP = 2


def _win_in_specs(T):
    nb = T // WB
    assert nb % WSTEP == 0
    ns = nb // WSTEP
    q = [pl.BlockSpec((WSTEP * WB, GW), functools.partial(lambda n, g: (n, QB_COL0 + g), g=g)) for g in range(KV)]
    kv = [pl.BlockSpec((WB, KW), functools.partial(lambda n, o, c: (jnp.clip(WSTEP * n + o, 0, nb - 1), c), o=o, c=c))
          for c in (KB_COL, KB_COL + 1) for o in range(-1, WSTEP + 1)]
    bias = [pl.BlockSpec((None, KV, WK, GQ * WB), lambda n: (jnp.where(n == 0, 0, 1), 0, 0, 0)),
            pl.BlockSpec((None, KV, WK, GQ * WB), lambda n: (jnp.where(n == ns - 1, 2, 1), 0, 0, 0))]
    return ns, q + kv, bias


def _end_tables(biasT):
    key = lax.broadcasted_iota(jnp.int32, biasT.shape, 1)
    return jnp.stack([jnp.where(key < WB, NEG, biasT), biasT, jnp.where(key >= 2 * WB, NEG, biasT)])


def _heads_to_lanes(t):
    return jnp.concatenate([t[i * HEAD_DIM:(i + 1) * HEAD_DIM] for i in range(GQ)], axis=1)


def _lanes_to_heads(t):
    return jnp.concatenate([t[:, i * WB:(i + 1) * WB] for i in range(GQ)], axis=0)


def _attn_b_fwd(h, biasT, sink_rows):
    T = h.shape[0]
    ns, in_specs, bias_specs = _win_in_specs(T)
    nkv = WSTEP + 2

    def body(*refs):
        q_refs, k_refs, v_refs = refs[:KV], refs[KV:KV + nkv], refs[KV + nkv:KV + 2 * nkv]
        b_refs, sk_ref, o_ref, lse_ref = refs[KV + 2 * nkv:KV + 2 * nkv + WSTEP], *refs[KV + 2 * nkv + WSTEP:]
        ks, vs = [r[...] for r in k_refs], [r[...] for r in v_refs]
        outs = []
        for b, b_ref in enumerate(b_refs):
            rows = slice(b * WB, (b + 1) * WB)
            kwin = jnp.concatenate(ks[b:b + 3], axis=0)
            vT = jnp.concatenate(vs[b:b + 3], axis=0).T
            qT = [_heads_to_lanes((q[rows, :] * SCALE).T).astype(BF16) for q in q_refs]
            sT = [jnp.dot(kwin[:, g * HEAD_DIM:(g + 1) * HEAD_DIM].astype(BF16), qT[g], preferred_element_type=F32)
                  for g in range(KV)]
            oT = []
            for g in range(KV):
                s = sT[g] + b_ref[g]
                sk = sk_ref[g]
                m = jnp.maximum(jnp.max(s, axis=0, keepdims=True), sk)
                p = jnp.exp(s - m)
                den = jnp.sum(p, axis=0, keepdims=True) + jnp.exp(sk - m)
                o = jnp.dot(vT[g * HEAD_DIM:(g + 1) * HEAD_DIM].astype(BF16), p.astype(BF16),
                            preferred_element_type=F32) / den
                lse_ref[b, g] = m + jnp.log(den)
                oT.append(_lanes_to_heads(o))
            outs.append(jnp.concatenate(oT, axis=0).T)
        o_ref[...] = jnp.concatenate(outs, axis=0)

    whole = lambda *shape: pl.BlockSpec(shape, lambda n: (0,) * len(shape))
    return pl.pallas_call(
        body, name="attn_b_fwd", grid=(ns,),
        out_shape=(jax.ShapeDtypeStruct((T, QW), F32), jax.ShapeDtypeStruct((ns * WSTEP, KV, 1, GQ * WB), F32)),
        in_specs=in_specs + bias_specs + [whole(KV, 1, GQ * WB)],
        out_specs=(pl.BlockSpec((WSTEP * WB, QW), lambda n: (n, 0)),
                   pl.BlockSpec((WSTEP, KV, 1, GQ * WB), lambda n: (n, 0, 0, 0))),
        compiler_params=_params(("parallel",), 32 << 20),
    )(*([h] * (KV + 2 * nkv)), *([biasT] * WSTEP), sink_rows)


def _attn_b_bwd(h, do, o, lse, biasT, sink_rows):
    T = h.shape[0]
    ns, in_specs, bias_specs = _win_in_specs(T)
    nkv = WSTEP + 2
    Tp = T + 2 * WB
    nt = (((1,), (1,)), ((), ()))

    def body(*refs):
        q_refs, k_refs, v_refs = refs[:KV], refs[KV:KV + nkv], refs[KV + nkv:KV + 2 * nkv]
        at = KV + 2 * nkv
        do_ref, o_ref, lse_ref = refs[at:at + 3]
        b_refs, sk_ref = refs[at + 3:at + 3 + WSTEP], refs[at + 3 + WSTEP]
        dq_ref, dk_ref, dv_ref, db_ref, dsk_ref = refs[at + 4 + WSTEP:]
        n = pl.program_id(0)

        @pl.when(n == 0)
        def _():
            dk_ref[...] = jnp.zeros_like(dk_ref)
            dv_ref[...] = jnp.zeros_like(dv_ref)
            db_ref[...] = jnp.zeros_like(db_ref)
            dsk_ref[...] = jnp.zeros_like(dsk_ref)

        ks, vs = [r[...] for r in k_refs], [r[...] for r in v_refs]
        for b, b_ref in enumerate(b_refs):
            rows = slice(b * WB, (b + 1) * WB)
            kwin = jnp.concatenate(ks[b:b + 3], axis=0)
            vwin = jnp.concatenate(vs[b:b + 3], axis=0)
            kT = kwin.T
            doT_all, oT_all = do_ref[rows, :].T, o_ref[rows, :].T
            qT, doT, delta, sT, dpT = [], [], [], [], []
            for g, q in enumerate(q_refs):
                hd = slice(g * HEAD_DIM, (g + 1) * HEAD_DIM)
                qT.append(_heads_to_lanes((q[rows, :] * SCALE).T).astype(BF16))
                d = _heads_to_lanes(doT_all[g * GW:(g + 1) * GW])
                delta.append(jnp.sum(d * _heads_to_lanes(oT_all[g * GW:(g + 1) * GW]), axis=0, keepdims=True))
                doT.append(d.astype(BF16))
                sT.append(jnp.dot(kwin[:, hd].astype(BF16), qT[g], preferred_element_type=F32))
                dpT.append(jnp.dot(vwin[:, hd].astype(BF16), doT[g], preferred_element_type=F32))
            dq, dk, dv = [], [], []
            for g in range(KV):
                lse_g = lse_ref[b, g]
                p = jnp.exp(sT[g] + b_ref[g] - lse_g)
                ds = p * (dpT[g] - delta[g])
                db_ref[g] += ds
                dsk_ref[g] -= jnp.exp(sk_ref[g] - lse_g) * delta[g]
                dsb = ds.astype(BF16)
                dqT = jnp.dot(kT[g * HEAD_DIM:(g + 1) * HEAD_DIM].astype(BF16), dsb, preferred_element_type=F32)
                dq.append(_lanes_to_heads(dqT))
                dk.append(lax.dot_general(dsb, qT[g], nt, preferred_element_type=F32))
                dv.append(lax.dot_general(p.astype(BF16), doT[g], nt, preferred_element_type=F32))
            dq_ref[rows, :] = (jnp.concatenate(dq, axis=0).T * SCALE).astype(BF16)
            win = pl.ds(pl.multiple_of((WSTEP * n + b) * WB, WB), WK)
            dk_ref[win, :] += jnp.concatenate(dk, axis=1)
            dv_ref[win, :] += jnp.concatenate(dv, axis=1)

    whole = lambda *shape: pl.BlockSpec(shape, lambda n: (0,) * len(shape))
    tok = pl.BlockSpec((WSTEP * WB, QW), lambda n: (n, 0))
    return pl.pallas_call(
        body, name="attn_b_bwd", grid=(ns,),
        out_shape=(jax.ShapeDtypeStruct((T, QW), BF16),
                   jax.ShapeDtypeStruct((Tp, KW), F32), jax.ShapeDtypeStruct((Tp, KW), F32),
                   jax.ShapeDtypeStruct((KV, WK, GQ * WB), F32), jax.ShapeDtypeStruct((KV, 1, GQ * WB), F32)),
        in_specs=in_specs + [tok, tok, pl.BlockSpec((WSTEP, KV, 1, GQ * WB), lambda n: (n, 0, 0, 0))]
        + bias_specs + [whole(KV, 1, GQ * WB)],
        out_specs=(tok, whole(Tp, KW), whole(Tp, KW), whole(KV, WK, GQ * WB), whole(KV, 1, GQ * WB)),
        compiler_params=_params(("arbitrary",), 48 << 20),
    )(*([h] * (KV + 2 * nkv)), do, o, lse, *([biasT] * WSTEP), sink_rows)


def _bias_table(rel_bias_t, bucket):
    nh, n = rel_bias_t.shape[0], bucket.shape[1]

    def body(rb_ref, bk_ref, o_ref):
        bk = bk_ref[...]
        out = jnp.full((nh, n), NEG, F32)
        for b in range(N_BUCKETS):
            out = jnp.where(bk == b, rb_ref[:, b:b + 1], out)
        o_ref[...] = out

    return pl.pallas_call(
        body, name="bias_table", out_shape=jax.ShapeDtypeStruct((nh, n), F32),
        compiler_params=pltpu.CompilerParams(vmem_limit_bytes=32 << 20),
    )(rel_bias_t, bucket)


def _bias_sink_grads(db_list, dsk_list, bucket):
    L = len(db_list)

    def body(*refs):
        db_refs, dsk_refs, bk_ref = refs[:L], refs[L:2 * L], refs[2 * L]
        drb_ref, dsink_ref = refs[2 * L + 1], refs[2 * L + 2]
        tot = db_refs[0][...]
        for r in db_refs[1:]:
            tot = tot + r[...]
        bk = bk_ref[...]
        lane = lax.broadcasted_iota(jnp.int32, (2 * GQ, N_BUCKETS), 1)
        out = jnp.zeros((2 * GQ, N_BUCKETS), F32)
        for b in range(N_BUCKETS):
            sb = jnp.sum(jnp.where(bk == b, tot, 0.0), axis=1, keepdims=True)
            out = jnp.where(lane == b, sb, out)
        drb_ref[...] = out
        for l in range(L):
            dsink_ref[l] = jnp.sum(dsk_refs[l][...], axis=1, keepdims=True)

    return pl.pallas_call(
        body, name="bias_sink_grads",
        out_shape=(jax.ShapeDtypeStruct((2 * GQ, N_BUCKETS), F32), jax.ShapeDtypeStruct((L, 2 * GQ, 1), F32)),
        compiler_params=pltpu.CompilerParams(vmem_limit_bytes=32 << 20),
    )(*db_list, *dsk_list, bucket)


def _outnorm_fwd(oaT, ob, ga, gb, *, tm=512):
    T = ob.shape[0]
    tm = min(tm, T)

    def body(oaT_ref, ob_ref, ga_ref, gb_ref, y_ref):
        for j, (o, g_ref) in enumerate(((oaT_ref[...].T, ga_ref), (ob_ref[...], gb_ref))):
            r = lax.rsqrt(jnp.mean(o * o, axis=1, keepdims=True) + RMS_EPS)
            y_ref[:, j * QW:(j + 1) * QW] = (o * r * g_ref[...]).astype(BF16)

    half = pl.BlockSpec((tm, QW), lambda i: (i, 0))
    halfT = pl.BlockSpec((QW, tm), lambda i: (0, i))
    vec = pl.BlockSpec((1, QW), lambda i: (0, 0))
    return pl.pallas_call(
        body, name="outnorm_fwd", grid=(T // tm,),
        out_shape=jax.ShapeDtypeStruct((T, 2 * QW), BF16),
        in_specs=[halfT, half, vec, vec], out_specs=pl.BlockSpec((tm, 2 * QW), lambda i: (i, 0)),
        compiler_params=_params(("parallel",), 16 << 20),
    )(oaT, ob, ga, gb)


def _outnorm_bwd(dy, oaT, ob, ga, gb, *, tm=512):
    T = ob.shape[0]
    tm = min(tm, T)
    nh = QW // HEAD_DIM

    def body(dy_ref, oaT_ref, ob_ref, ga_ref, gb_ref, doaT_ref, dl_ref, dob_ref, dg_ref):
        i = pl.program_id(0)

        @pl.when(i == 0)
        def _():
            dg_ref[...] = jnp.zeros_like(dg_ref)

        oaT = oaT_ref[...]
        for j, (o, g_ref) in enumerate(((oaT.T, ga_ref), (ob_ref[...], gb_ref))):
            d = dy_ref[:, j * QW:(j + 1) * QW]
            r = lax.rsqrt(jnp.mean(o * o, axis=1, keepdims=True) + RMS_EPS)
            orr = o * r
            dg_ref[j:j + 1, :] += jnp.sum(d * orr, axis=0, keepdims=True)
            dgv = d * g_ref[...]
            do = r * dgv - orr * (r * r) * jnp.mean(dgv * o, axis=1, keepdims=True)
            if j == 0:
                doT = do.T
                doaT_ref[...] = doT.astype(BF16)
                prod = doT * oaT
                dl_ref[...] = jnp.concatenate(
                    [jnp.sum(prod[a * HEAD_DIM:(a + 1) * HEAD_DIM], axis=0, keepdims=True) for a in range(nh)], axis=0)
            else:
                dob_ref[...] = do

    half = pl.BlockSpec((tm, QW), lambda i: (i, 0))
    halfT = pl.BlockSpec((QW, tm), lambda i: (0, i))
    vec = pl.BlockSpec((1, QW), lambda i: (0, 0))
    return pl.pallas_call(
        body, name="outnorm_bwd", grid=(T // tm,),
        out_shape=(jax.ShapeDtypeStruct((QW, T), BF16), jax.ShapeDtypeStruct((nh, T), F32),
                   jax.ShapeDtypeStruct((T, QW), F32), jax.ShapeDtypeStruct((8, QW), F32)),
        in_specs=[pl.BlockSpec((tm, 2 * QW), lambda i: (i, 0)), halfT, half, vec, vec],
        out_specs=(halfT, pl.BlockSpec((nh, tm), lambda i: (0, i)), half, pl.BlockSpec((8, QW), lambda i: (0, 0))),
        compiler_params=_params(("arbitrary",), 32 << 20),
    )(dy, oaT, ob, ga, gb)


GELU_C = math.sqrt(2.0 / math.pi)
GELU_A = 0.044715
HALO = 16
SUB = 8


def _gelu_parts(x):
    x2 = x * x
    t = jnp.tanh(x * (GELU_C + (GELU_C * GELU_A) * x2))
    return 0.5 * (1.0 + t), t, x2


def _halo_specs(tm, tn, T):
    nh = tm // HALO
    last = T // HALO - 1
    cur = pl.BlockSpec((tm, tn), lambda j, i: (i, j))
    prev = pl.BlockSpec((HALO, tn), lambda j, i: (jnp.maximum(i * nh - 1, 0), j))
    nxt = pl.BlockSpec((HALO, tn), lambda j, i: (jnp.minimum((i + 1) * nh, last), j))
    return cur, prev, nxt


def _conv_glu_fwd(g, u, conv_w, conv_b, *, tm=256, tn=1408):
    T, F = g.shape
    tm, tn = min(tm, T), min(tn, F)
    cur, prev, nxt = _halo_specs(tm, tn, T)

    def body(g_ref, gp_ref, gn_ref, u_ref, w_ref, b_ref, a_ref):
        i = pl.program_id(1)
        gv = g_ref[...]
        before = jnp.where(i * tm > 0, gp_ref[HALO - SUB:, :], 0.0)
        after = jnp.where((i + 1) * tm < T, gn_ref[0:SUB, :], 0.0)
        gm1 = pltpu.roll(jnp.concatenate([before, gv], axis=0), 1, 0)[SUB:]
        gp1 = pltpu.roll(jnp.concatenate([gv, after], axis=0), tm + SUB - 1, 0)[:tm]
        gc = ((b_ref[...] + gm1 * w_ref[0:1, :]) + gv * w_ref[1:2, :]) + gp1 * w_ref[2:3, :]
        cdf, _, _ = _gelu_parts(gc)
        a_ref[...] = (gc * cdf * u_ref[...].astype(F32)).astype(BF16)

    wspec = pl.BlockSpec((8, tn), lambda j, i: (0, j))
    est = 2 * (3 * _nbytes((tm, tn), F32)) + 8 * _nbytes((tm, tn), F32)
    return pl.pallas_call(
        body, name="conv_glu_fwd", grid=(F // tn, T // tm),
        out_shape=jax.ShapeDtypeStruct((T, F), BF16),
        in_specs=[cur, prev, nxt, cur, wspec, pl.BlockSpec((1, tn), lambda j, i: (0, j))],
        out_specs=cur,
        compiler_params=_params(("parallel", "parallel"), est),
    )(g, g, g, u, conv_w, conv_b)


def _conv_glu_bwd(dact, g, u, conv_w, conv_b, *, tm=256, tn=1408):
    T, F = g.shape
    tm, tn = min(tm, T), min(tn, F)
    cur, prev, nxt = _halo_specs(tm, tn, T)
    te = tm + 2 * HALO

    def body(d_ref, dp_ref, dn_ref, g_ref, gp_ref, gn_ref, u_ref, up_ref, un_ref, w_ref, b_ref,
             dg_ref, du_ref, dc_ref):
        i = pl.program_id(1)

        @pl.when(i == 0)
        def _():
            dc_ref[...] = jnp.zeros_like(dc_ref)

        has_prev, has_next = i * tm > 0, (i + 1) * tm < T
        ge = jnp.concatenate([jnp.where(has_prev, gp_ref[...], 0.0), g_ref[...],
                              jnp.where(has_next, gn_ref[...], 0.0)], axis=0)
        ue = jnp.concatenate([up_ref[...], u_ref[...], un_ref[...]], axis=0).astype(F32)
        de = jnp.concatenate([jnp.where(has_prev, dp_ref[...].astype(F32), 0.0), d_ref[...].astype(F32),
                              jnp.where(has_next, dn_ref[...].astype(F32), 0.0)], axis=0)
        w0, w1, w2 = w_ref[0:1, :], w_ref[1:2, :], w_ref[2:3, :]
        gm1 = pltpu.roll(ge, 1, 0)
        gp1 = pltpu.roll(ge, te - 1, 0)
        gc = ((b_ref[...] + gm1 * w0) + ge * w1) + gp1 * w2
        cdf, t, gc2 = _gelu_parts(gc)
        dgelu = cdf + (0.5 * gc) * (1.0 - t * t) * (GELU_C + (3.0 * GELU_C * GELU_A) * gc2)
        dgc = de * ue * dgelu
        dge = w0 * pltpu.roll(dgc, te - 1, 0) + w1 * dgc + w2 * pltpu.roll(dgc, 1, 0)
        mid = slice(HALO, HALO + tm)
        dg_ref[...] = dge[mid].astype(BF16)
        du_ref[...] = (de[mid] * (gc[mid] * cdf[mid])).astype(BF16)
        dgm = dgc[mid]
        dc_ref[0:1, :] += jnp.sum(dgm * gm1[mid], axis=0, keepdims=True)
        dc_ref[1:2, :] += jnp.sum(dgm * ge[mid], axis=0, keepdims=True)
        dc_ref[2:3, :] += jnp.sum(dgm * gp1[mid], axis=0, keepdims=True)
        dc_ref[3:4, :] += jnp.sum(dgm, axis=0, keepdims=True)

    wspec = pl.BlockSpec((8, tn), lambda j, i: (0, j))
    est = 2 * (3 * _nbytes((tm, tn), F32) + 2 * _nbytes((tm, tn), BF16)) + 16 * _nbytes((te, tn), F32)
    return pl.pallas_call(
        body, name="conv_glu_bwd", grid=(F // tn, T // tm),
        out_shape=(jax.ShapeDtypeStruct((T, F), BF16), jax.ShapeDtypeStruct((T, F), BF16),
                   jax.ShapeDtypeStruct((8, F), F32)),
        in_specs=[cur, prev, nxt, cur, prev, nxt, cur, prev, nxt, wspec, pl.BlockSpec((1, tn), lambda j, i: (0, j))],
        out_specs=(cur, cur, wspec),
        compiler_params=_params(("parallel", "arbitrary"), est),
    )(dact, dact, dact, g, g, g, u, u, u, conv_w, conv_b)


def _adamw_math(w, g, m, v):
    m = ADAM_B1 * m + (1.0 - ADAM_B1) * g
    v = ADAM_B2 * v + (1.0 - ADAM_B2) * (g * g)
    m_hat = m / (1.0 - ADAM_B1 ** ADAM_STEP)
    v_hat = v / (1.0 - ADAM_B2 ** ADAM_STEP)
    delta = -ADAM_LR * (m_hat / (jnp.sqrt(v_hat) + ADAM_EPS) + ADAM_WD * w)
    return delta, m, v


def _adamw(w, m, v, gparts, *, name):
    R, C = w.shape
    r = gparts[0].shape[1]
    assert len(gparts) * r == R
    tr = max(d for d in range(8, min(r, 256) + 1, 8) if r % d == 0) if r % 8 == 0 else r
    per = r // tr

    def body(w_ref, m_ref, v_ref, *rest):
        gp_refs, (g_ref, d_ref, nm_ref, nv_ref) = rest[:len(gparts)], rest[len(gparts):]
        i = pl.program_id(0)
        for l, gp_ref in enumerate(gp_refs):
            @pl.when(i // per == l)
            def _(gp_ref=gp_ref):
                g = gp_ref[0].astype(F32)
                for j in range(1, N_DEV):
                    g = g + gp_ref[j].astype(F32)
                delta, nm, nv = _adamw_math(w_ref[...], g, m_ref[...], v_ref[...])
                g_ref[...] = g
                d_ref[...] = delta
                nm_ref[...] = nm
                nv_ref[...] = nv

    blk = pl.BlockSpec((tr, C), lambda i: (i, 0))
    out = jax.ShapeDtypeStruct((R, C), F32)
    return pl.pallas_call(
        body, name=name, grid=(R // tr,), out_shape=(out, out, out, out),
        in_specs=[blk, blk, blk] + [pl.BlockSpec((N_DEV, tr, C), lambda i: (0, i % per, 0))] * len(gparts),
        out_specs=(blk, blk, blk, blk),
        compiler_params=_params(("parallel",), 24 << 20),
    )(w, m, v, *gparts)


def _rope_tables(T):
    rows_n = T // GRID_W
    row = jnp.repeat(jnp.arange(rows_n, dtype=F32), GRID_W)
    col = jnp.tile(jnp.arange(GRID_W, dtype=F32), rows_n)
    half = HEAD_DIM // 2
    inv_freq = ROPE_THETA ** (-jnp.arange(0, half, 2, dtype=F32) / half)
    ang = jnp.concatenate([row[:, None] * inv_freq, col[:, None] * inv_freq], axis=-1)
    cos, sin = jnp.cos(ang), jnp.sin(ang)
    cos64 = jnp.repeat(cos, 2, axis=-1)
    sin64 = jnp.stack([-sin, sin], axis=-1).reshape(T, HEAD_DIM)
    return jnp.tile(cos64, (1, 2)), jnp.tile(sin64, (1, 2))


def _t5_bucket(rel):
    half = N_BUCKETS // 2
    max_exact = half // 2
    bucket = jnp.where(rel > 0, half, 0)
    rp = jnp.abs(rel)
    rpf = jnp.maximum(rp, 1).astype(F32)
    large = max_exact + (jnp.log(rpf / max_exact) / math.log(MAX_DISTANCE / max_exact)
                         * (half - max_exact)).astype(jnp.int32)
    large = jnp.minimum(large, half - 1)
    return bucket + jnp.where(rp < max_exact, rp, large)


def _window_buckets():
    qpos = jnp.arange(WB, dtype=jnp.int32)
    kpos = jnp.arange(WK, dtype=jnp.int32) - WB
    rel = kpos[None, :] - qpos[:, None]
    return jnp.where(jnp.abs(rel) <= WINDOW, _t5_bucket(rel), -1)


def _heads_first(a, nh):
    T = a.shape[0]
    return a.reshape(T, nh, HEAD_DIM).transpose(1, 0, 2)


def _row(v):
    return v.reshape(1, -1)


def _rows8(rows, width):
    a = jnp.stack(list(rows), axis=0)
    return jnp.pad(a, ((0, 8 - a.shape[0]), (0, 0)))


def _layer_fwd(l, xin, W, tabs, comm=None, on_comm=None, target=None):
    xhat, xg, xb, x16 = xin
    T = xhat.shape[0]
    cos2, sin2, biasT = tabs
    h = _mm([x16], [W["w_in"][l]], name="mm_in", out_dtype=F32, tm=MM_ROWS, tn=IN_COLS, tk=D_MODEL)
    gains = _rows8([jnp.tile(W["q_norm"][l], 2), jnp.tile(W["k_norm"][l], 2)], LANES)
    hT, kv_nat = _qk_rope_fwd(h, gains, cos2, sin2, name="qk_rope_fwd")
    hT = hT.reshape(A_HEADS, HEAD_DIM, T)
    ka, va = _heads_first(kv_nat[:, :KW], KV), _heads_first(kv_nat[:, KW:], KV)
    res = _attn_a_fwd(ka, hT, comm=comm)
    oaT, lse_a = res[0].reshape(QW, T), res[1]
    if comm is not None:
        on_comm(res[2:])
    sink_rows = jnp.repeat(W["sink"][l], WB).reshape(KV, 1, GQ * WB)
    ob_t, lse_b = _attn_b_fwd(h, biasT, sink_rows)
    ga, gb = _row(W["out_norm_a"][l]), _row(W["out_norm_b"][l])
    ycat = _outnorm_fwd(oaT, ob_t, ga, gb)
    g1, b1 = _row(W["ln1_g"][l]), _row(W["ln1_b"][l])
    x1hat, rstd1, x1_16 = _mm_res_ln(ycat, W["w_out"][l], xhat, xg, xb, g1, b1, name="mm_out_ln", tm=512)
    gate = _mm([x1_16], [W["w_gate"][l]], name="mm_gate", out_dtype=F32, tm=MM_ROWS_WIDE, tn=D_FF, tk=D_MODEL)
    up = _mm([x1_16], [W["w_up"][l]], name="mm_up", out_dtype=BF16, tm=MM_ROWS_WIDE, tn=D_FF, tk=D_MODEL)
    cw = jnp.pad(W["conv_w"][l], ((0, 5), (0, 0)))
    cb = _row(W["conv_b"][l])
    act = _conv_glu_fwd(gate, up, cw, cb)
    g2, b2 = _row(W["ln2_g"][l]), _row(W["ln2_b"][l])
    saved = dict(x16=x16, h=h, gains=gains, hT=hT, ka=ka, va=va, oaT=oaT, lse_a=lse_a,
                 lse_b=lse_b, sink_rows=sink_rows, ob_t=ob_t,
                 ga=ga, gb=gb, ycat=ycat, x1hat=x1hat, rstd1=rstd1, x1_16=x1_16, g1=g1, b1=b1, gate=gate, up=up,
                 cw=cw, cb=cb, act=act, g2=g2, b2=b2)
    if target is not None:
        saved["head"] = _mm_res_ln(act, W["w_down"][l], x1hat, g1, b1, g2, b2, name="mm_down_ln_loss", tm=256,
                                   target=target)
        return None, saved
    x2hat, rstd2, x2_16 = _mm_res_ln(act, W["w_down"][l], x1hat, g1, b1, g2, b2, name="mm_down_ln", tm=256)
    saved.update(x2hat=x2hat, rstd2=rstd2)
    return (x2hat, g2, b2, x2_16), saved


def _layer_bwd(l, S, W, tabs, dz2, dz2_16, stats2, scatter=None):
    cos2, sin2, biasT = tabs
    T = dz2.shape[0]
    G = {}
    G["ln2_g"], G["ln2_b"] = stats2[0], stats2[1]
    G["w_down"] = _mm([S["act"]], [dz2_16], name="dw_down", out_dtype=BF16, trans_a=True, tm=D_FF // 2, tn=D_MODEL, tk=DW_TOKENS)
    dact = _mm([dz2_16], [W["w_down"][l]], name="mm_dact", out_dtype=BF16, trans_b=True, tm=MM_ROWS_WIDE, tn=D_FF,
               tk=D_MODEL)
    dg, du, dconv = _conv_glu_bwd(dact, S["gate"], S["up"], S["cw"], S["cb"])
    G["conv_w"], G["conv_b"] = dconv[0:3], dconv[3]
    G["w_gate"] = _mm([S["x1_16"]], [dg], name="dw_gate", out_dtype=BF16, trans_a=True, tm=D_MODEL, tn=D_FF // 2, tk=DW_TOKENS)
    G["w_up"] = _mm([S["x1_16"]], [du], name="dw_up", out_dtype=BF16, trans_a=True, tm=D_MODEL, tn=D_FF // 2, tk=DW_TOKENS)
    dz1, dz1_16, stats1 = _mm([dg, du], [W["w_gate"][l], W["w_up"][l]], name="mm_dx1_ln", out_dtype=F32, trans_b=True,
                              tm=MM_ROWS_LN, tn=D_MODEL, tk=D_FF, add=dz2, add_scale=ALPHA,
                              ln=(S["x1hat"], S["rstd1"], S["g1"]))
    G["ln1_g"], G["ln1_b"] = stats1[0], stats1[1]
    G["w_out"] = _mm([S["ycat"]], [dz1_16], name="dw_out", out_dtype=BF16, trans_a=True, tm=D_MODEL, tn=D_MODEL, tk=DW_TOKENS)
    dycat = _mm([dz1_16], [W["w_out"][l]], name="mm_dycat", out_dtype=F32, trans_b=True, tm=MM_ROWS, tn=D_MODEL,
                tk=D_MODEL)
    doaT, delta, dob_t, dgn = _outnorm_bwd(dycat, S["oaT"], S["ob_t"], S["ga"], S["gb"])
    G["out_norm_a"], G["out_norm_b"] = dgn[0], dgn[1]
    res = _attn_a_bwd(S["ka"], S["va"], S["hT"], doaT.reshape(KV * GQ, HEAD_DIM, T), S["lse_a"],
                      delta.reshape(KV * GQ, 1, T), comm=scatter(G) if scatter is not None else None)
    dkaT, dvaT, dqaT = res[:3]
    dh_rope, dgain = _qk_rope_bwd(S["h"], dqaT.reshape(QW, T), dkaT.reshape(KW, T), S["gains"], cos2, sin2,
                                  name="qk_rope_bwd")
    G["q_norm"], G["k_norm"] = dgain[0, :HEAD_DIM], dgain[1, :HEAD_DIM]
    dqb_t, dkb, dvb, dbiasT, dsk = _attn_b_bwd(S["h"], dob_t, S["ob_t"], S["lse_b"], biasT, S["sink_rows"])
    dh = jnp.concatenate([
        dh_rope, dvaT.transpose(2, 0, 1).reshape(T, KW).astype(BF16), dqb_t,
        dkb[WB:WB + T].astype(BF16), dvb[WB:WB + T].astype(BF16)], axis=1)
    dbias = dbiasT.reshape(KV, WK, GQ, WB).transpose(0, 2, 1, 3)
    G["w_in"] = _mm([S["x16"]], [dh], name="dw_in", out_dtype=BF16, trans_a=True, tm=D_MODEL, tn=IN_COLS, tk=DW_TOKENS)
    dxin = _mm([dh], [W["w_in"][l]], name="mm_dxin", out_dtype=F32, trans_b=True, tm=MM_ROWS, tn=D_MODEL, tk=IN_COLS,
               add=dz1, add_scale=ALPHA)
    return dxin, G, dbias.reshape(KV * GQ, WK * WB), dsk.reshape(KV * GQ, WB), res[3:]


BIG = ("w_in", "w_out", "w_gate", "w_up", "w_down")
COL_SHARDED = ("w_in", "w_gate", "w_up")


def _unshard(name, blocks):
    _, r, c = blocks.shape
    if name in COL_SHARDED:
        return blocks.transpose(1, 0, 2).reshape(r, N_DEV * c)
    return blocks.reshape(N_DEV * r, c)


def _to_owner_blocks(name, full, shard_shape):
    _, r, c = shard_shape
    if name in COL_SHARDED:
        return full.reshape(r, N_DEV, c).transpose(1, 0, 2)
    return full.reshape(N_DEV, r, c)


def _pack_small(vals, tail):
    flat = jnp.concatenate([vals[n].reshape(-1).astype(F32) for n in SMALL_NAMES] + [tail])
    pad = (-flat.shape[0]) % (8 * LANES)
    return jnp.pad(flat, (0, pad)).reshape(-1, LANES)


def _unpack_small(packed, shapes):
    flat = packed.reshape(-1)
    out, off = {}, 0
    for n in SMALL_NAMES:
        size = math.prod(shapes[n])
        out[n] = flat[off:off + size].reshape(shapes[n])
        off += size
    return out, flat[off]


def kernel(x, rel_bias, w_in, q_norm, k_norm, sink, out_norm_a, out_norm_b, w_out, ln1_g, ln1_b, w_gate, w_up, conv_w, conv_b, w_down, ln2_g, ln2_b, loss_target, m_rel_bias, m_w_in, m_q_norm, m_k_norm, m_sink, m_out_norm_a, m_out_norm_b, m_w_out, m_ln1_g, m_ln1_b, m_w_gate, m_w_up, m_conv_w, m_conv_b, m_w_down, m_ln2_g, m_ln2_b, v_rel_bias, v_w_in, v_q_norm, v_k_norm, v_sink, v_out_norm_a, v_out_norm_b, v_w_out, v_ln1_g, v_ln1_b, v_w_gate, v_w_up, v_conv_w, v_conv_b, v_w_down, v_ln2_g, v_ln2_b):
    P = dict(rel_bias=rel_bias, w_in=w_in, q_norm=q_norm, k_norm=k_norm, sink=sink, out_norm_a=out_norm_a,
             out_norm_b=out_norm_b, w_out=w_out, ln1_g=ln1_g, ln1_b=ln1_b, w_gate=w_gate, w_up=w_up, conv_w=conv_w,
             conv_b=conv_b, w_down=w_down, ln2_g=ln2_g, ln2_b=ln2_b)
    M = dict(rel_bias=m_rel_bias, w_in=m_w_in, q_norm=m_q_norm, k_norm=m_k_norm, sink=m_sink, out_norm_a=m_out_norm_a,
             out_norm_b=m_out_norm_b, w_out=m_w_out, ln1_g=m_ln1_g, ln1_b=m_ln1_b, w_gate=m_w_gate, w_up=m_w_up,
             conv_w=m_conv_w, conv_b=m_conv_b, w_down=m_w_down, ln2_g=m_ln2_g, ln2_b=m_ln2_b)
    V = dict(rel_bias=v_rel_bias, w_in=v_w_in, q_norm=v_q_norm, k_norm=v_k_norm, sink=v_sink, out_norm_a=v_out_norm_a,
             out_norm_b=v_out_norm_b, w_out=v_w_out, ln1_g=v_ln1_g, ln1_b=v_ln1_b, w_gate=v_w_gate, w_up=v_w_up,
             conv_w=v_conv_w, conv_b=v_conv_b, w_down=v_w_down, ln2_g=v_ln2_g, ln2_b=v_ln2_b)
    names = list(P)
    T = x.shape[1]
    me = 4 * lax.axis_index("x") + 2 * lax.axis_index("y") + lax.axis_index("c")

    L, taps, fc = conv_w.shape
    W = {n: ([None] * DEPTH if n in BIG else P[n]) for n in names}

    def wire(n, l):
        return P[n][l].astype(BF16)

    def take(n, l, gathered):
        W[n][l] = _unshard(n, gathered)

    take("w_in", 0, _exchange([wire("w_in", 0)], [True], name="gather_w_in0")[0])
    later = [(n, l) for l in range(DEPTH) for n in BIG if (n, l) != ("w_in", 0)]
    cw_shard = conv_w.reshape(-1)
    cw_wire = jnp.pad(cw_shard, (0, (-cw_shard.shape[0]) % LANES)).reshape(-1, LANES)
    gather_rest = _Comm([wire(n, l) for n, l in later] + [cw_wire], [True] * (len(later) + 1))

    def on_gathered(outs):
        for (n, l), g in zip(later, outs):
            take(n, l, g)
        cw_all = outs[-1].reshape(N_DEV, -1)[:, :cw_shard.shape[0]].reshape(N_DEV, L, taps, fc)
        W["conv_w"] = cw_all.transpose(1, 2, 0, 3).reshape(L, taps, N_DEV * fc)

    cos2, sin2 = _rope_tables(T)
    bucket = _window_buckets()
    bias = _bias_table(rel_bias.T, bucket.reshape(1, WB * WK))
    biasT = bias.reshape(KV, GQ, WB, WK).transpose(0, 3, 1, 2).reshape(KV, WK, GQ * WB)
    biasT = _end_tables(biasT)
    tabs = (cos2, sin2, biasT)

    ones, zeros = jnp.ones((1, D_MODEL), F32), jnp.zeros((1, D_MODEL), F32)
    cur = (x[0], ones, zeros, x[0].astype(BF16))
    saved = []
    for l in range(DEPTH):
        cur, S = _layer_fwd(l, cur, W, tabs, comm=gather_rest if l == 0 else None, on_comm=on_gathered,
                            target=loss_target[0] if l == DEPTH - 1 else None)
        saved.append(S)

    def owner_blocks(n, l):
        return _to_owner_blocks(n, grads[l][n], P[n].shape)

    early = ([(n, l) for l in range(1, DEPTH) for n in BIG] + [(n, 0) for n in BIG if n != "w_in"])

    def scatter_early(g0):
        grads[0] = g0
        return _Comm([owner_blocks(n, l) for n, l in early], [False] * len(early))

    grads = [None] * DEPTH
    dbs, dsks = [None] * DEPTH, [None] * DEPTH
    dz, dz16, stats = saved[-1]["head"]
    loss_part = stats[2, 0:1]
    recv = {}
    for l in reversed(range(DEPTH)):
        S = saved[l]
        dxin, grads[l], dbs[l], dsks[l], got = _layer_bwd(l, S, W, tabs, dz, dz16, stats,
                                                         scatter=scatter_early if l == 0 else None)
        if l == 0:
            recv.update(zip(early, got))
        if l > 0:
            Sp = saved[l - 1]
            dz, dz16, stats = _ln_bwd(Sp["x2hat"], Sp["rstd2"], Sp["g2"], dxin, name="ln2_bwd")
    grad_x = dxin[None]

    drb, dsink = _bias_sink_grads(dbs, dsks, bucket.T.reshape(1, WK * WB))
    small_g = {n: jnp.stack([grads[l][n] for l in range(DEPTH)]) for n in SMALL_NAMES if n not in ("rel_bias", "sink")}
    small_g["rel_bias"] = drb.T
    small_g["sink"] = dsink.reshape(DEPTH, KV * GQ)
    recv[("w_in", 0)], small_recv = _exchange([owner_blocks("w_in", 0), _pack_small(small_g, loss_part)], [False, True],
                                              name="scatter_w_in0_gather_small")

    out_g, out_d, out_m, out_v = {}, {}, {}, {}
    for n in BIG:
        shp = P[n].shape
        rows, cols = shp[0] * shp[1], shp[2]
        res = _adamw(P[n].reshape(rows, cols), M[n].reshape(rows, cols), V[n].reshape(rows, cols),
                     [recv[(n, l)] for l in range(DEPTH)], name="adamw_" + n)
        out_g[n], out_d[n], out_m[n], out_v[n] = (r.reshape(shp) for r in res)
    full_shapes = {n: W[n].shape for n in SMALL_NAMES}

    def small_state(D):
        vals = {n: D[n] for n in SMALL_NAMES if n != "conv_w"}
        cw = jnp.zeros((L, taps, N_DEV, fc), F32)
        cw = lax.dynamic_update_slice(cw, D["conv_w"].reshape(L, taps, 1, fc), (0, 0, me, 0))
        vals["conv_w"] = cw.reshape(L, taps, N_DEV * fc)
        return _pack_small(vals, jnp.zeros((1,), F32))

    sw, sm, sv = small_state(P), small_state(M), small_state(V)
    res = _adamw(sw, sm, sv, [small_recv], name="adamw_small")
    loss = _unpack_small(res[0], full_shapes)[1]
    for dst, packed in zip((out_g, out_d, out_m, out_v), res):
        vals, _ = _unpack_small(packed, full_shapes)
        for n in SMALL_NAMES:
            if n == "conv_w":
                sl = lax.dynamic_slice(vals[n].reshape(L, taps, N_DEV, fc), (0, 0, me, 0), (L, taps, 1, fc))
                dst[n] = sl.reshape(L, taps, fc)
            else:
                dst[n] = vals[n]
    return (loss, grad_x, *[out_g[n] for n in names], *[out_d[n] for n in names],
            *[out_m[n] for n in names], *[out_v[n] for n in names])
```
